```python
import jax, jax.numpy as jnp
from jax import lax
import numpy as np

D_MODEL = 1024
BATCH = 8
SEQ = 8192
DEPTH = 1

D_RNN = 5 * D_MODEL // 4
RG_BLOCKS = 16
RG_BLOCK_W = D_RNN // RG_BLOCKS
RG_CONV = 4
RG_C = 8.0
DN_QK_HEADS = D_MODEL // 128
DN_V_HEADS = 2 * DN_QK_HEADS
DN_DK = 128
DN_DV = 128
DN_QK = DN_QK_HEADS * DN_DK
DN_V = DN_V_HEADS * DN_DV
DN_CONV = 4
DN_CHUNK = 64
D_FF = 11 * D_MODEL // 4
FFN_CONV = 3
LN_EPS = 1e-5
RMS_EPS = 1e-6
L2_EPS = 1e-6
DEEPNORM_ALPHA = (2 * DEPTH) ** 0.25
DEEPNORM_BETA = (8 * DEPTH) ** -0.25
IN_SPLITS = (D_RNN, D_RNN, DN_QK, DN_QK, DN_V, DN_V, DN_V_HEADS, DN_V_HEADS, D_MODEL, D_MODEL)
D_IN = sum(IN_SPLITS)

kernel_name = "hybrid_rglru_gdn_convffn_deepnorm_adaln"

F32 = jnp.float32


def split_cols(t, sizes):
    idx = np.cumsum(sizes)[:-1].tolist()
    return jnp.split(t, idx, axis=-1)


def causal_dwconv(x, w):
    k, ch = w.shape
    return lax.conv_general_dilated(
        x, w[:, None, :].astype(x.dtype), window_strides=(1,), padding=[(k - 1, 0)],
        dimension_numbers=('NWC', 'WIO', 'NWC'), feature_group_count=ch)


def layer_norm(x, g, b):
    xf = x.astype(F32)
    mu = jnp.mean(xf, axis=-1, keepdims=True)
    xc = xf - mu
    var = jnp.mean(xc * xc, axis=-1, keepdims=True)
    return (xc * lax.rsqrt(var + LN_EPS) * g.astype(F32) + b.astype(F32)).astype(x.dtype)


def l2_normalize(t):
    return t * lax.rsqrt(jnp.sum(t * t, axis=-1, keepdims=True) + L2_EPS)


def rg_lru(xr, w_a, b_a, w_x, b_x, lam):
    bsz, s, _ = xr.shape
    xf = xr.astype(F32)
    xb = xf.reshape(bsz, s, RG_BLOCKS, RG_BLOCK_W)
    gate_r = jax.nn.sigmoid(jnp.einsum('bsni,nij->bsnj', xb, w_a.astype(F32)).reshape(bsz, s, D_RNN) + b_a.astype(F32))
    gate_i = jax.nn.sigmoid(jnp.einsum('bsni,nij->bsnj', xb, w_x.astype(F32)).reshape(bsz, s, D_RNN) + b_x.astype(F32))
    log_a = -RG_C * gate_r * jax.nn.softplus(-lam.astype(F32))
    a = jnp.exp(log_a)
    mult = jnp.sqrt(-jnp.expm1(2.0 * log_a))
    u = mult * gate_i * xf

    def combine(left, right):
        a_l, h_l = left
        a_r, h_r = right
        return a_l * a_r, a_r * h_l + h_r

    _, h = lax.associative_scan(combine, (a, u), axis=1)
    return h


def chunk_gated_delta_rule(q, k, v, g, beta):
    bsz, nh, s, dk = q.shape
    dv = v.shape[-1]
    n = s // DN_CHUNK
    c = DN_CHUNK
    rs = lambda t: t.reshape((bsz, nh, n, c) + t.shape[3:])
    q, k, v, g, beta = rs(q), rs(k), rs(v), rs(g), rs(beta)
    G = jnp.cumsum(g, axis=-1)
    causal = jnp.tril(jnp.ones((c, c), dtype=bool))
    strict = jnp.tril(jnp.ones((c, c), dtype=bool), k=-1)
    diff = G[..., :, None] - G[..., None, :]
    decay = jnp.exp(jnp.where(causal, diff, -jnp.inf))
    kb = k * beta[..., None]
    A = jnp.where(strict, jnp.einsum('bhnid,bhnjd->bhnij', kb, k) * decay, 0.0)
    T = A + jnp.eye(c, dtype=F32)
    u = lax.linalg.triangular_solve(T, v * beta[..., None], left_side=True, lower=True)
    w = lax.linalg.triangular_solve(T, kb * jnp.exp(G)[..., None], left_side=True, lower=True)
    qk = jnp.einsum('bhnid,bhnjd->bhnij', q, k) * decay
    q_dec = q * jnp.exp(G)[..., None]
    k_dec = k * jnp.exp(G[..., -1:] - G)[..., None]
    g_last = jnp.exp(G[..., -1])
    xs = tuple(jnp.moveaxis(t, 2, 0) for t in (qk, q_dec, k_dec, u, w, g_last))

    def step(state, inp):
        qk_n, qd_n, kd_n, u_n, w_n, gl_n = inp
        v_new = u_n - jnp.einsum('bhcd,bhde->bhce', w_n, state)
        o = jnp.einsum('bhcd,bhde->bhce', qd_n, state) + jnp.einsum('bhij,bhje->bhie', qk_n, v_new)
        state = gl_n[..., None, None] * state + jnp.einsum('bhcd,bhce->bhde', kd_n, v_new)
        return state, o

    _, o = lax.scan(step, jnp.zeros((bsz, nh, dk, dv), F32), xs)
    return jnp.moveaxis(o, 0, 2).reshape(bsz, nh, s, dv)


def gated_deltanet(q, k, v, z, a_in, b_in, conv_w, a_log, dt_bias, norm_w):
    bsz, s, _ = q.shape
    qkv = jax.nn.silu(causal_dwconv(jnp.concatenate([q, k, v], axis=-1).astype(F32), conv_w.astype(F32)))
    q, k, v = split_cols(qkv, (DN_QK, DN_QK, DN_V))
    rep = DN_V_HEADS // DN_QK_HEADS
    q = jnp.repeat(l2_normalize(q.reshape(bsz, s, DN_QK_HEADS, DN_DK)), rep, axis=2) * (DN_DK ** -0.5)
    k = jnp.repeat(l2_normalize(k.reshape(bsz, s, DN_QK_HEADS, DN_DK)), rep, axis=2)
    v = v.reshape(bsz, s, DN_V_HEADS, DN_DV)
    beta = jax.nn.sigmoid(b_in.astype(F32))
    g = -jnp.exp(a_log.astype(F32)) * jax.nn.softplus(a_in.astype(F32) + dt_bias.astype(F32))
    o = chunk_gated_delta_rule(jnp.swapaxes(q, 1, 2), jnp.swapaxes(k, 1, 2), jnp.swapaxes(v, 1, 2),
                               jnp.swapaxes(g, 1, 2), jnp.swapaxes(beta, 1, 2))
    o = jnp.swapaxes(o, 1, 2)
    o = o * lax.rsqrt(jnp.mean(o * o, axis=-1, keepdims=True) + RMS_EPS) * norm_w.astype(F32)
    o = o * jax.nn.silu(z.astype(F32).reshape(bsz, s, DN_V_HEADS, DN_DV))
    return o.reshape(bsz, s, DN_V)


def token_mixer(h, w_in, rg_conv_w, rg_conv_b, rg_w_a, rg_b_a, rg_w_x, rg_b_x, rg_lambda,
                dn_conv_w, dn_a_log, dn_dt_bias, dn_norm_w, w_proj_a, w_proj_b, w_out):
    proj = h @ w_in
    xr, gr, q, k, v, z, a_in, b_in, g_a, g_b = split_cols(proj, IN_SPLITS)
    xr = causal_dwconv(xr, rg_conv_w) + rg_conv_b
    rec = rg_lru(xr, rg_w_a, rg_b_a, rg_w_x, rg_b_x, rg_lambda) * jax.nn.gelu(gr.astype(F32))
    y_a = rec.astype(h.dtype) @ w_proj_a
    dn = gated_deltanet(q, k, v, z, a_in, b_in, dn_conv_w, dn_a_log, dn_dt_bias, dn_norm_w)
    y_b = dn.astype(h.dtype) @ w_proj_b
    merged = jax.nn.sigmoid(g_a) * y_a + jax.nn.sigmoid(g_b) * y_b
    return merged @ w_out


def conv_ffn(h, w_gate, w_up, conv_w, conv_b, w_down):
    gate = causal_dwconv(h @ w_gate, conv_w) + conv_b
    return (jax.nn.gelu(gate) * (h @ w_up)) @ w_down


def _fwd_setup_inputs(seed: int = 0) -> dict:
    key = jax.random.key(seed)
    ks = jax.random.split(key, 32)
    L, D = DEPTH, D_MODEL
    nrm = lambda kk, shape, scale: jax.random.normal(kk, shape, F32) * scale
    u_a = jax.random.uniform(ks[9], (L, D_RNN), F32, 0.9, 0.999)
    s_a = u_a ** (1.0 / RG_C)
    rg_lambda = jnp.log(s_a) - jnp.log1p(-s_a)
    dt = jnp.exp(jax.random.uniform(ks[12], (L, DN_V_HEADS), F32, np.log(1e-3), np.log(1e-1)))
    dt = jnp.maximum(dt, 1e-4)
    return {
        "x": nrm(ks[0], (BATCH, SEQ, D), 1.0),
        "c": nrm(ks[1], (BATCH, D), 1.0),
        "w_ada": nrm(ks[2], (L, D, 6 * D), 0.1 * D ** -0.5),
        "b_ada": nrm(ks[3], (L, 6 * D), 0.01),
        "w_in": nrm(ks[4], (L, D, D_IN), D ** -0.5),
        "rg_conv_w": nrm(ks[5], (L, RG_CONV, D_RNN), RG_CONV ** -0.5),
        "rg_conv_b": nrm(ks[6], (L, D_RNN), 0.01),
        "rg_w_a": nrm(ks[7], (L, RG_BLOCKS, RG_BLOCK_W, RG_BLOCK_W), RG_BLOCK_W ** -0.5),
        "rg_b_a": nrm(ks[8], (L, D_RNN), 0.01),
        "rg_w_x": nrm(ks[10], (L, RG_BLOCKS, RG_BLOCK_W, RG_BLOCK_W), RG_BLOCK_W ** -0.5),
        "rg_b_x": nrm(ks[11], (L, D_RNN), 0.01),
        "rg_lambda": rg_lambda,
        "dn_conv_w": nrm(ks[13], (L, DN_CONV, 2 * DN_QK + DN_V), DN_CONV ** -0.5),
        "dn_a_log": jnp.log(jax.random.uniform(ks[14], (L, DN_V_HEADS), F32, 1.0, 16.0)),
        "dn_dt_bias": dt + jnp.log(-jnp.expm1(-dt)),
        "dn_norm_w": 1.0 + nrm(ks[15], (L, DN_DV), 0.02),
        "w_proj_a": nrm(ks[16], (L, D_RNN, D), D_RNN ** -0.5),
        "w_proj_b": nrm(ks[17], (L, DN_V, D), DN_V ** -0.5),
        "w_out": nrm(ks[18], (L, D, D), DEEPNORM_BETA * D ** -0.5),
        "ln1_g": 1.0 + nrm(ks[19], (L, D), 0.02),
        "ln1_b": nrm(ks[20], (L, D), 0.01),
        "ffn_w_gate": nrm(ks[21], (L, D, D_FF), D ** -0.5),
        "ffn_w_up": nrm(ks[22], (L, D, D_FF), D ** -0.5),
        "ffn_conv_w": nrm(ks[23], (L, FFN_CONV, D_FF), FFN_CONV ** -0.5),
        "ffn_conv_b": nrm(ks[24], (L, D_FF), 0.01),
        "ffn_w_down": nrm(ks[25], (L, D_FF, D), DEEPNORM_BETA * D_FF ** -0.5),
        "ln2_g": 1.0 + nrm(ks[26], (L, D), 0.02),
        "ln2_b": nrm(ks[27], (L, D), 0.01),
    }


def _fwd_reference(x, c, w_ada, b_ada, w_in, rg_conv_w, rg_conv_b, rg_w_a, rg_b_a, rg_w_x, rg_b_x,
              rg_lambda, dn_conv_w, dn_a_log, dn_dt_bias, dn_norm_w, w_proj_a, w_proj_b, w_out,
              ln1_g, ln1_b, ffn_w_gate, ffn_w_up, ffn_conv_w, ffn_conv_b, ffn_w_down, ln2_g, ln2_b):
    for l in range(DEPTH):
        ada = jax.nn.silu(c) @ w_ada[l] + b_ada[l]
        sh1, sc1, gt1, sh2, sc2, gt2 = [t[:, None, :] for t in jnp.split(ada, 6, axis=-1)]
        h = x * (1.0 + sc1) + sh1
        mix = token_mixer(h, w_in[l], rg_conv_w[l], rg_conv_b[l], rg_w_a[l], rg_b_a[l], rg_w_x[l],
                          rg_b_x[l], rg_lambda[l], dn_conv_w[l], dn_a_log[l], dn_dt_bias[l],
                          dn_norm_w[l], w_proj_a[l], w_proj_b[l], w_out[l])
        x = layer_norm(DEEPNORM_ALPHA * x + (1.0 + gt1) * mix, ln1_g[l], ln1_b[l])
        h = x * (1.0 + sc2) + sh2
        ff = conv_ffn(h, ffn_w_gate[l], ffn_w_up[l], ffn_conv_w[l], ffn_conv_b[l], ffn_w_down[l])
        x = layer_norm(DEEPNORM_ALPHA * x + (1.0 + gt2) * ff, ln2_g[l], ln2_b[l])
    return x


import jax as _jax
import jax.numpy as _jnp

TWIN_FORMAT = 'train_step'
FWD_PARAMS = ['x', 'c', 'w_ada', 'b_ada', 'w_in', 'rg_conv_w', 'rg_conv_b', 'rg_w_a', 'rg_b_a', 'rg_w_x', 'rg_b_x', 'rg_lambda', 'dn_conv_w', 'dn_a_log', 'dn_dt_bias', 'dn_norm_w', 'w_proj_a', 'w_proj_b', 'w_out', 'ln1_g', 'ln1_b', 'ffn_w_gate', 'ffn_w_up', 'ffn_conv_w', 'ffn_conv_b', 'ffn_w_down', 'ln2_g', 'ln2_b']
TWIN_WEIGHTS = ['w_ada', 'b_ada', 'w_in', 'rg_conv_w', 'rg_conv_b', 'rg_w_a', 'rg_b_a', 'rg_w_x', 'rg_b_x', 'rg_lambda', 'dn_conv_w', 'dn_a_log', 'dn_dt_bias', 'dn_norm_w', 'w_proj_a', 'w_proj_b', 'w_out', 'ln1_g', 'ln1_b', 'ffn_w_gate', 'ffn_w_up', 'ffn_conv_w', 'ffn_conv_b', 'ffn_w_down', 'ln2_g', 'ln2_b']
TWIN_DIFF_INPUT = 'x'
TWIN_INPUTS = ['x', 'c', 'w_ada', 'b_ada', 'w_in', 'rg_conv_w', 'rg_conv_b', 'rg_w_a', 'rg_b_a', 'rg_w_x', 'rg_b_x', 'rg_lambda', 'dn_conv_w', 'dn_a_log', 'dn_dt_bias', 'dn_norm_w', 'w_proj_a', 'w_proj_b', 'w_out', 'ln1_g', 'ln1_b', 'ffn_w_gate', 'ffn_w_up', 'ffn_conv_w', 'ffn_conv_b', 'ffn_w_down', 'ln2_g', 'ln2_b', 'loss_target', 'm_w_ada', 'm_b_ada', 'm_w_in', 'm_rg_conv_w', 'm_rg_conv_b', 'm_rg_w_a', 'm_rg_b_a', 'm_rg_w_x', 'm_rg_b_x', 'm_rg_lambda', 'm_dn_conv_w', 'm_dn_a_log', 'm_dn_dt_bias', 'm_dn_norm_w', 'm_w_proj_a', 'm_w_proj_b', 'm_w_out', 'm_ln1_g', 'm_ln1_b', 'm_ffn_w_gate', 'm_ffn_w_up', 'm_ffn_conv_w', 'm_ffn_conv_b', 'm_ffn_w_down', 'm_ln2_g', 'm_ln2_b', 'v_w_ada', 'v_b_ada', 'v_w_in', 'v_rg_conv_w', 'v_rg_conv_b', 'v_rg_w_a', 'v_rg_b_a', 'v_rg_w_x', 'v_rg_b_x', 'v_rg_lambda', 'v_dn_conv_w', 'v_dn_a_log', 'v_dn_dt_bias', 'v_dn_norm_w', 'v_w_proj_a', 'v_w_proj_b', 'v_w_out', 'v_ln1_g', 'v_ln1_b', 'v_ffn_w_gate', 'v_ffn_w_up', 'v_ffn_conv_w', 'v_ffn_conv_b', 'v_ffn_w_down', 'v_ln2_g', 'v_ln2_b']
TWIN_OUTPUTS = ['loss', 'grad_x', 'grad_w_ada', 'grad_b_ada', 'grad_w_in', 'grad_rg_conv_w', 'grad_rg_conv_b', 'grad_rg_w_a', 'grad_rg_b_a', 'grad_rg_w_x', 'grad_rg_b_x', 'grad_rg_lambda', 'grad_dn_conv_w', 'grad_dn_a_log', 'grad_dn_dt_bias', 'grad_dn_norm_w', 'grad_w_proj_a', 'grad_w_proj_b', 'grad_w_out', 'grad_ln1_g', 'grad_ln1_b', 'grad_ffn_w_gate', 'grad_ffn_w_up', 'grad_ffn_conv_w', 'grad_ffn_conv_b', 'grad_ffn_w_down', 'grad_ln2_g', 'grad_ln2_b', 'delta_w_ada', 'delta_b_ada', 'delta_w_in', 'delta_rg_conv_w', 'delta_rg_conv_b', 'delta_rg_w_a', 'delta_rg_b_a', 'delta_rg_w_x', 'delta_rg_b_x', 'delta_rg_lambda', 'delta_dn_conv_w', 'delta_dn_a_log', 'delta_dn_dt_bias', 'delta_dn_norm_w', 'delta_w_proj_a', 'delta_w_proj_b', 'delta_w_out', 'delta_ln1_g', 'delta_ln1_b', 'delta_ffn_w_gate', 'delta_ffn_w_up', 'delta_ffn_conv_w', 'delta_ffn_conv_b', 'delta_ffn_w_down', 'delta_ln2_g', 'delta_ln2_b', 'new_m_w_ada', 'new_m_b_ada', 'new_m_w_in', 'new_m_rg_conv_w', 'new_m_rg_conv_b', 'new_m_rg_w_a', 'new_m_rg_b_a', 'new_m_rg_w_x', 'new_m_rg_b_x', 'new_m_rg_lambda', 'new_m_dn_conv_w', 'new_m_dn_a_log', 'new_m_dn_dt_bias', 'new_m_dn_norm_w', 'new_m_w_proj_a', 'new_m_w_proj_b', 'new_m_w_out', 'new_m_ln1_g', 'new_m_ln1_b', 'new_m_ffn_w_gate', 'new_m_ffn_w_up', 'new_m_ffn_conv_w', 'new_m_ffn_conv_b', 'new_m_ffn_w_down', 'new_m_ln2_g', 'new_m_ln2_b', 'new_v_w_ada', 'new_v_b_ada', 'new_v_w_in', 'new_v_rg_conv_w', 'new_v_rg_conv_b', 'new_v_rg_w_a', 'new_v_rg_b_a', 'new_v_rg_w_x', 'new_v_rg_b_x', 'new_v_rg_lambda', 'new_v_dn_conv_w', 'new_v_dn_a_log', 'new_v_dn_dt_bias', 'new_v_dn_norm_w', 'new_v_w_proj_a', 'new_v_w_proj_b', 'new_v_w_out', 'new_v_ln1_g', 'new_v_ln1_b', 'new_v_ffn_w_gate', 'new_v_ffn_w_up', 'new_v_ffn_conv_w', 'new_v_ffn_conv_b', 'new_v_ffn_w_down', 'new_v_ln2_g', 'new_v_ln2_b']
TWIN_LEAF_KINDS = {'loss': 'loss', 'grad_x': 'grad_x', 'grad_w_ada': 'grad_w', 'grad_b_ada': 'grad_w', 'grad_w_in': 'grad_w', 'grad_rg_conv_w': 'grad_w', 'grad_rg_conv_b': 'grad_w', 'grad_rg_w_a': 'grad_w', 'grad_rg_b_a': 'grad_w', 'grad_rg_w_x': 'grad_w', 'grad_rg_b_x': 'grad_w', 'grad_rg_lambda': 'grad_w', 'grad_dn_conv_w': 'grad_w', 'grad_dn_a_log': 'grad_w', 'grad_dn_dt_bias': 'grad_w', 'grad_dn_norm_w': 'grad_w', 'grad_w_proj_a': 'grad_w', 'grad_w_proj_b': 'grad_w', 'grad_w_out': 'grad_w', 'grad_ln1_g': 'grad_w', 'grad_ln1_b': 'grad_w', 'grad_ffn_w_gate': 'grad_w', 'grad_ffn_w_up': 'grad_w', 'grad_ffn_conv_w': 'grad_w', 'grad_ffn_conv_b': 'grad_w', 'grad_ffn_w_down': 'grad_w', 'grad_ln2_g': 'grad_w', 'grad_ln2_b': 'grad_w', 'delta_w_ada': 'delta_w', 'delta_b_ada': 'delta_w', 'delta_w_in': 'delta_w', 'delta_rg_conv_w': 'delta_w', 'delta_rg_conv_b': 'delta_w', 'delta_rg_w_a': 'delta_w', 'delta_rg_b_a': 'delta_w', 'delta_rg_w_x': 'delta_w', 'delta_rg_b_x': 'delta_w', 'delta_rg_lambda': 'delta_w', 'delta_dn_conv_w': 'delta_w', 'delta_dn_a_log': 'delta_w', 'delta_dn_dt_bias': 'delta_w', 'delta_dn_norm_w': 'delta_w', 'delta_w_proj_a': 'delta_w', 'delta_w_proj_b': 'delta_w', 'delta_w_out': 'delta_w', 'delta_ln1_g': 'delta_w', 'delta_ln1_b': 'delta_w', 'delta_ffn_w_gate': 'delta_w', 'delta_ffn_w_up': 'delta_w', 'delta_ffn_conv_w': 'delta_w', 'delta_ffn_conv_b': 'delta_w', 'delta_ffn_w_down': 'delta_w', 'delta_ln2_g': 'delta_w', 'delta_ln2_b': 'delta_w', 'new_m_w_ada': 'new_m', 'new_m_b_ada': 'new_m', 'new_m_w_in': 'new_m', 'new_m_rg_conv_w': 'new_m', 'new_m_rg_conv_b': 'new_m', 'new_m_rg_w_a': 'new_m', 'new_m_rg_b_a': 'new_m', 'new_m_rg_w_x': 'new_m', 'new_m_rg_b_x': 'new_m', 'new_m_rg_lambda': 'new_m', 'new_m_dn_conv_w': 'new_m', 'new_m_dn_a_log': 'new_m', 'new_m_dn_dt_bias': 'new_m', 'new_m_dn_norm_w': 'new_m', 'new_m_w_proj_a': 'new_m', 'new_m_w_proj_b': 'new_m', 'new_m_w_out': 'new_m', 'new_m_ln1_g': 'new_m', 'new_m_ln1_b': 'new_m', 'new_m_ffn_w_gate': 'new_m', 'new_m_ffn_w_up': 'new_m', 'new_m_ffn_conv_w': 'new_m', 'new_m_ffn_conv_b': 'new_m', 'new_m_ffn_w_down': 'new_m', 'new_m_ln2_g': 'new_m', 'new_m_ln2_b': 'new_m', 'new_v_w_ada': 'new_v', 'new_v_b_ada': 'new_v', 'new_v_w_in': 'new_v', 'new_v_rg_conv_w': 'new_v', 'new_v_rg_conv_b': 'new_v', 'new_v_rg_w_a': 'new_v', 'new_v_rg_b_a': 'new_v', 'new_v_rg_w_x': 'new_v', 'new_v_rg_b_x': 'new_v', 'new_v_rg_lambda': 'new_v', 'new_v_dn_conv_w': 'new_v', 'new_v_dn_a_log': 'new_v', 'new_v_dn_dt_bias': 'new_v', 'new_v_dn_norm_w': 'new_v', 'new_v_w_proj_a': 'new_v', 'new_v_w_proj_b': 'new_v', 'new_v_w_out': 'new_v', 'new_v_ln1_g': 'new_v', 'new_v_ln1_b': 'new_v', 'new_v_ffn_w_gate': 'new_v', 'new_v_ffn_w_up': 'new_v', 'new_v_ffn_conv_w': 'new_v', 'new_v_ffn_conv_b': 'new_v', 'new_v_ffn_w_down': 'new_v', 'new_v_ln2_g': 'new_v', 'new_v_ln2_b': 'new_v'}


def _forward(args):
    return _fwd_reference(*[args[k] for k in FWD_PARAMS])


def _output_shape():
    def fwd():
        inp = _fwd_setup_inputs(0)
        return _fwd_reference(*[inp[k] for k in FWD_PARAMS])
    out = _jax.eval_shape(fwd)
    return out.shape, out.dtype

N_MICROBATCH = 1
ADAM_LR = 0.001
ADAM_B1 = 0.9
ADAM_B2 = 0.999
ADAM_EPS = 1e-08
ADAM_WD = 0.01
ADAM_STEP = 10
PER_EXAMPLE_BATCH_AXIS = {'x': 0, 'c': 0, 'loss_target': 0}
SHARED_INPUTS = []
_WEIGHT_DTYPES = {'w_ada': _jnp.float32, 'b_ada': _jnp.float32, 'w_in': _jnp.float32, 'rg_conv_w': _jnp.float32, 'rg_conv_b': _jnp.float32, 'rg_w_a': _jnp.float32, 'rg_b_a': _jnp.float32, 'rg_w_x': _jnp.float32, 'rg_b_x': _jnp.float32, 'rg_lambda': _jnp.float32, 'dn_conv_w': _jnp.float32, 'dn_a_log': _jnp.float32, 'dn_dt_bias': _jnp.float32, 'dn_norm_w': _jnp.float32, 'w_proj_a': _jnp.float32, 'w_proj_b': _jnp.float32, 'w_out': _jnp.float32, 'ln1_g': _jnp.float32, 'ln1_b': _jnp.float32, 'ffn_w_gate': _jnp.float32, 'ffn_w_up': _jnp.float32, 'ffn_conv_w': _jnp.float32, 'ffn_conv_b': _jnp.float32, 'ffn_w_down': _jnp.float32, 'ln2_g': _jnp.float32, 'ln2_b': _jnp.float32}
MOMENT_SCALE = {'w_ada': 9.670685e-02, 'b_ada': 1.856971e-01, 'w_in': 2.738259e-02, 'rg_conv_w': 3.747288e-02, 'rg_conv_b': 3.602720e-01, 'rg_w_a': 1.082804e-02, 'rg_b_a': 8.310881e-03, 'rg_w_x': 1.973741e-02, 'rg_b_x': 1.460830e-02, 'rg_lambda': 1.734750e-02, 'dn_conv_w': 2.819768e-02, 'dn_a_log': 9.885778e-02, 'dn_dt_bias': 9.397737e-02, 'dn_norm_w': 1.251683e-01, 'w_proj_a': 3.923184e-02, 'w_proj_b': 4.261718e-02, 'w_out': 9.587748e-02, 'ln1_g': 2.067008e+00, 'ln1_b': 4.698860e-01, 'ffn_w_gate': 4.970077e-02, 'ffn_w_up': 4.823315e-02, 'ffn_conv_w': 5.127184e-02, 'ffn_conv_b': 4.764278e-02, 'ffn_w_down': 1.351975e-01, 'ln2_g': 6.401607e+01, 'ln2_b': 2.408627e+00}


def _to_microbatches(a, axis):
    t = _jnp.moveaxis(a, axis, 0)
    t = t.reshape((N_MICROBATCH, t.shape[0] // N_MICROBATCH) + t.shape[1:])
    return _jnp.moveaxis(t, 1, axis + 1)


def setup_inputs(seed: int = 0) -> dict:
    inp = _fwd_setup_inputs(seed)
    key = _jax.random.fold_in(_jax.random.key(seed), 7919)
    shape, _ = _output_shape()
    out = dict(inp)
    out["loss_target"] = _jax.random.normal(_jax.random.fold_in(key, 0), shape, _jnp.float32)
    for i, name in enumerate(TWIN_WEIGHTS):
        w = inp[name].astype(_jnp.float32)
        if MOMENT_SCALE is None:
            s = _jnp.sqrt(_jnp.mean(_jnp.square(w)) + 1e-30)
        else:
            s = MOMENT_SCALE[name]
        km, kv = _jax.random.split(_jax.random.fold_in(key, i + 1))
        out[name] = w
        out["m_" + name] = s * _jax.random.normal(km, w.shape, _jnp.float32)
        out["v_" + name] = (s * s) * _jax.random.uniform(kv, w.shape, _jnp.float32, 0.5, 1.5)
    if N_MICROBATCH > 1:
        for name, axis in PER_EXAMPLE_BATCH_AXIS.items():
            out[name] = _to_microbatches(out[name], axis)
    return {'x': out['x'], 'c': out['c'], 'w_ada': out['w_ada'], 'b_ada': out['b_ada'], 'w_in': out['w_in'], 'rg_conv_w': out['rg_conv_w'], 'rg_conv_b': out['rg_conv_b'], 'rg_w_a': out['rg_w_a'], 'rg_b_a': out['rg_b_a'], 'rg_w_x': out['rg_w_x'], 'rg_b_x': out['rg_b_x'], 'rg_lambda': out['rg_lambda'], 'dn_conv_w': out['dn_conv_w'], 'dn_a_log': out['dn_a_log'], 'dn_dt_bias': out['dn_dt_bias'], 'dn_norm_w': out['dn_norm_w'], 'w_proj_a': out['w_proj_a'], 'w_proj_b': out['w_proj_b'], 'w_out': out['w_out'], 'ln1_g': out['ln1_g'], 'ln1_b': out['ln1_b'], 'ffn_w_gate': out['ffn_w_gate'], 'ffn_w_up': out['ffn_w_up'], 'ffn_conv_w': out['ffn_conv_w'], 'ffn_conv_b': out['ffn_conv_b'], 'ffn_w_down': out['ffn_w_down'], 'ln2_g': out['ln2_g'], 'ln2_b': out['ln2_b'], 'loss_target': out['loss_target'], 'm_w_ada': out['m_w_ada'], 'm_b_ada': out['m_b_ada'], 'm_w_in': out['m_w_in'], 'm_rg_conv_w': out['m_rg_conv_w'], 'm_rg_conv_b': out['m_rg_conv_b'], 'm_rg_w_a': out['m_rg_w_a'], 'm_rg_b_a': out['m_rg_b_a'], 'm_rg_w_x': out['m_rg_w_x'], 'm_rg_b_x': out['m_rg_b_x'], 'm_rg_lambda': out['m_rg_lambda'], 'm_dn_conv_w': out['m_dn_conv_w'], 'm_dn_a_log': out['m_dn_a_log'], 'm_dn_dt_bias': out['m_dn_dt_bias'], 'm_dn_norm_w': out['m_dn_norm_w'], 'm_w_proj_a': out['m_w_proj_a'], 'm_w_proj_b': out['m_w_proj_b'], 'm_w_out': out['m_w_out'], 'm_ln1_g': out['m_ln1_g'], 'm_ln1_b': out['m_ln1_b'], 'm_ffn_w_gate': out['m_ffn_w_gate'], 'm_ffn_w_up': out['m_ffn_w_up'], 'm_ffn_conv_w': out['m_ffn_conv_w'], 'm_ffn_conv_b': out['m_ffn_conv_b'], 'm_ffn_w_down': out['m_ffn_w_down'], 'm_ln2_g': out['m_ln2_g'], 'm_ln2_b': out['m_ln2_b'], 'v_w_ada': out['v_w_ada'], 'v_b_ada': out['v_b_ada'], 'v_w_in': out['v_w_in'], 'v_rg_conv_w': out['v_rg_conv_w'], 'v_rg_conv_b': out['v_rg_conv_b'], 'v_rg_w_a': out['v_rg_w_a'], 'v_rg_b_a': out['v_rg_b_a'], 'v_rg_w_x': out['v_rg_w_x'], 'v_rg_b_x': out['v_rg_b_x'], 'v_rg_lambda': out['v_rg_lambda'], 'v_dn_conv_w': out['v_dn_conv_w'], 'v_dn_a_log': out['v_dn_a_log'], 'v_dn_dt_bias': out['v_dn_dt_bias'], 'v_dn_norm_w': out['v_dn_norm_w'], 'v_w_proj_a': out['v_w_proj_a'], 'v_w_proj_b': out['v_w_proj_b'], 'v_w_out': out['v_w_out'], 'v_ln1_g': out['v_ln1_g'], 'v_ln1_b': out['v_ln1_b'], 'v_ffn_w_gate': out['v_ffn_w_gate'], 'v_ffn_w_up': out['v_ffn_w_up'], 'v_ffn_conv_w': out['v_ffn_conv_w'], 'v_ffn_conv_b': out['v_ffn_conv_b'], 'v_ffn_w_down': out['v_ffn_w_down'], 'v_ln2_g': out['v_ln2_g'], 'v_ln2_b': out['v_ln2_b']}


def _loss(weights, diff, rest, loss_target):
    with _jax.named_scope("forward"):
        args = {**rest, TWIN_DIFF_INPUT: diff, **{k: w.astype(_WEIGHT_DTYPES[k]) for k, w in weights.items()}}
        y = _forward(args)
    with _jax.named_scope("loss_head"):
        err = _jnp.square(y.astype(_jnp.float32) - loss_target)
        return 0.5 * _jnp.sum(_jnp.mean(err, axis=-1)) if err.ndim else 0.5 * err


def _adamw(w, g, m, v):
    m = ADAM_B1 * m + (1.0 - ADAM_B1) * g
    v = ADAM_B2 * v + (1.0 - ADAM_B2) * _jnp.square(g)
    m_hat = m / (1.0 - ADAM_B1 ** ADAM_STEP)
    v_hat = v / (1.0 - ADAM_B2 ** ADAM_STEP)
    delta = -ADAM_LR * (m_hat / (_jnp.sqrt(v_hat) + ADAM_EPS) + ADAM_WD * w)
    return delta, m, v


def reference(x, c, w_ada, b_ada, w_in, rg_conv_w, rg_conv_b, rg_w_a, rg_b_a, rg_w_x, rg_b_x, rg_lambda, dn_conv_w, dn_a_log, dn_dt_bias, dn_norm_w, w_proj_a, w_proj_b, w_out, ln1_g, ln1_b, ffn_w_gate, ffn_w_up, ffn_conv_w, ffn_conv_b, ffn_w_down, ln2_g, ln2_b, loss_target, m_w_ada, m_b_ada, m_w_in, m_rg_conv_w, m_rg_conv_b, m_rg_w_a, m_rg_b_a, m_rg_w_x, m_rg_b_x, m_rg_lambda, m_dn_conv_w, m_dn_a_log, m_dn_dt_bias, m_dn_norm_w, m_w_proj_a, m_w_proj_b, m_w_out, m_ln1_g, m_ln1_b, m_ffn_w_gate, m_ffn_w_up, m_ffn_conv_w, m_ffn_conv_b, m_ffn_w_down, m_ln2_g, m_ln2_b, v_w_ada, v_b_ada, v_w_in, v_rg_conv_w, v_rg_conv_b, v_rg_w_a, v_rg_b_a, v_rg_w_x, v_rg_b_x, v_rg_lambda, v_dn_conv_w, v_dn_a_log, v_dn_dt_bias, v_dn_norm_w, v_w_proj_a, v_w_proj_b, v_w_out, v_ln1_g, v_ln1_b, v_ffn_w_gate, v_ffn_w_up, v_ffn_conv_w, v_ffn_conv_b, v_ffn_w_down, v_ln2_g, v_ln2_b):
    given = dict(x=x, c=c, w_ada=w_ada, b_ada=b_ada, w_in=w_in, rg_conv_w=rg_conv_w, rg_conv_b=rg_conv_b, rg_w_a=rg_w_a, rg_b_a=rg_b_a, rg_w_x=rg_w_x, rg_b_x=rg_b_x, rg_lambda=rg_lambda, dn_conv_w=dn_conv_w, dn_a_log=dn_a_log, dn_dt_bias=dn_dt_bias, dn_norm_w=dn_norm_w, w_proj_a=w_proj_a, w_proj_b=w_proj_b, w_out=w_out, ln1_g=ln1_g, ln1_b=ln1_b, ffn_w_gate=ffn_w_gate, ffn_w_up=ffn_w_up, ffn_conv_w=ffn_conv_w, ffn_conv_b=ffn_conv_b, ffn_w_down=ffn_w_down, ln2_g=ln2_g, ln2_b=ln2_b, loss_target=loss_target, m_w_ada=m_w_ada, m_b_ada=m_b_ada, m_w_in=m_w_in, m_rg_conv_w=m_rg_conv_w, m_rg_conv_b=m_rg_conv_b, m_rg_w_a=m_rg_w_a, m_rg_b_a=m_rg_b_a, m_rg_w_x=m_rg_w_x, m_rg_b_x=m_rg_b_x, m_rg_lambda=m_rg_lambda, m_dn_conv_w=m_dn_conv_w, m_dn_a_log=m_dn_a_log, m_dn_dt_bias=m_dn_dt_bias, m_dn_norm_w=m_dn_norm_w, m_w_proj_a=m_w_proj_a, m_w_proj_b=m_w_proj_b, m_w_out=m_w_out, m_ln1_g=m_ln1_g, m_ln1_b=m_ln1_b, m_ffn_w_gate=m_ffn_w_gate, m_ffn_w_up=m_ffn_w_up, m_ffn_conv_w=m_ffn_conv_w, m_ffn_conv_b=m_ffn_conv_b, m_ffn_w_down=m_ffn_w_down, m_ln2_g=m_ln2_g, m_ln2_b=m_ln2_b, v_w_ada=v_w_ada, v_b_ada=v_b_ada, v_w_in=v_w_in, v_rg_conv_w=v_rg_conv_w, v_rg_conv_b=v_rg_conv_b, v_rg_w_a=v_rg_w_a, v_rg_b_a=v_rg_b_a, v_rg_w_x=v_rg_w_x, v_rg_b_x=v_rg_b_x, v_rg_lambda=v_rg_lambda, v_dn_conv_w=v_dn_conv_w, v_dn_a_log=v_dn_a_log, v_dn_dt_bias=v_dn_dt_bias, v_dn_norm_w=v_dn_norm_w, v_w_proj_a=v_w_proj_a, v_w_proj_b=v_w_proj_b, v_w_out=v_w_out, v_ln1_g=v_ln1_g, v_ln1_b=v_ln1_b, v_ffn_w_gate=v_ffn_w_gate, v_ffn_w_up=v_ffn_w_up, v_ffn_conv_w=v_ffn_conv_w, v_ffn_conv_b=v_ffn_conv_b, v_ffn_w_down=v_ffn_w_down, v_ln2_g=v_ln2_g, v_ln2_b=v_ln2_b)
    weights = {n: given[n] for n in TWIN_WEIGHTS}
    shared = {n: given[n] for n in SHARED_INPUTS}
    per_example = {n: given[n] for n in ['x', 'c']}
    grad_fn = _jax.value_and_grad(_loss, argnums=(0, 1))

    def one_microbatch(ex, loss_target):
        ex = dict(ex)
        diff = ex.pop(TWIN_DIFF_INPUT)
        return grad_fn(weights, diff, {**shared, **ex}, loss_target)

    if N_MICROBATCH == 1:
        loss, (grad_w, grad_x) = one_microbatch(per_example, given["loss_target"])
    else:
        def body(carry, xs):
            loss_sum, grad_sum = carry
            l_k, (gw_k, gx_k) = one_microbatch(xs[0], xs[1])
            with _jax.named_scope("update"):
                return (loss_sum + l_k, _jax.tree.map(_jnp.add, grad_sum, gw_k)), gx_k

        init = (_jnp.zeros((), _jnp.float32), _jax.tree.map(_jnp.zeros_like, weights))
        (loss, grad_w), grad_x = _jax.lax.scan(body, init, (per_example, given["loss_target"]))
    with _jax.named_scope("update"):
        delta_w, new_m, new_v = {}, {}, {}
        for n in TWIN_WEIGHTS:
            delta_w[n], new_m[n], new_v[n] = _adamw(weights[n], grad_w[n], given["m_" + n], given["v_" + n])
    return (loss, grad_x, *[grad_w[n] for n in TWIN_WEIGHTS], *[delta_w[n] for n in TWIN_WEIGHTS],
            *[new_m[n] for n in TWIN_WEIGHTS], *[new_v[n] for n in TWIN_WEIGHTS])
```

```python
import functools

import jax
import jax.numpy as jnp
from jax import lax
from jax.experimental import pallas as pl
from jax.experimental.pallas import tpu as pltpu

F32 = jnp.float32
BF16 = jnp.bfloat16
MXU_DT = BF16
HI = lax.Precision.HIGHEST

D = 1024
D_RNN = 1280
RG_BLOCKS = 16
RG_BW = 80
RG_C = 8.0
NQ = 8
NV = 16
HD = 128
CH = 64
D_FF = 2816
LN_EPS = 1e-5
RMS_EPS = 1e-6
L2_EPS = 1e-6
ALPHA = 2.0 ** 0.25
Q_SCALE = HD ** -0.5
N_CAT = 10880
VMEM_LIMIT = 56 * 1024 * 1024
MESH = pl.DeviceIdType.MESH

ADAM_LR, ADAM_B1, ADAM_B2, ADAM_EPS, ADAM_WD, ADAM_STEP = 1e-3, 0.9, 0.999, 1e-8, 0.01, 10


def _sigmoid(x):
    return 1.0 / (1.0 + jnp.exp(-x))


def _softplus(x):
    return jnp.maximum(x, 0.0) + jnp.log1p(jnp.exp(-jnp.abs(x)))


_GC = 0.7978845608028654


def _gelu(x):
    return 0.5 * x * (1.0 + jnp.tanh(_GC * (x + 0.044715 * x * x * x)))


def _gelu_and_grad(x):
    t = jnp.tanh(_GC * (x + 0.044715 * x * x * x))
    g = 0.5 * x * (1.0 + t)
    dg = 0.5 * (1.0 + t) + 0.5 * x * (1.0 - t * t) * _GC * (1.0 + 3 * 0.044715 * x * x)
    return g, dg


def _neg_expm1(y):
    series = -y * (1.0 + 0.5 * y * (1.0 + y * (1.0 / 3.0)))
    return jnp.where(y > -0.01, series, 1.0 - jnp.exp(y))


def _dot(a, b):
    return jnp.dot(a.astype(MXU_DT), b.astype(MXU_DT), preferred_element_type=F32)


def _dot_nt(a, b):
    return lax.dot_general(a.astype(MXU_DT), b.astype(MXU_DT), (((1,), (1,)), ((), ())),
                           preferred_element_type=F32)


def _dot_tn(a, b):
    return lax.dot_general(a.astype(MXU_DT), b.astype(MXU_DT), (((0,), (0,)), ((), ())),
                           preferred_element_type=F32)


def _dot_hi(a, b):
    return jnp.dot(a, b, precision=HI, preferred_element_type=F32)


def _dot_tn_hi(a, b):
    return lax.dot_general(a, b, (((0,), (0,)), ((), ())), precision=HI, preferred_element_type=F32)


def _iota(shape, dim):
    return lax.broadcasted_iota(jnp.int32, shape, dim)


def _shift_down(x, before, j):
    if j == 0:
        return x
    xr = pltpu.roll(x, j, 0)
    br = pltpu.roll(before, j, 0)
    top = jnp.where(_iota(br.shape, 0) < j, br, xr[:8])
    return jnp.concatenate([top, xr[8:]], axis=0)


def _shift_up(x, after, j):
    if j == 0:
        return x
    t = x.shape[0]
    xr = pltpu.roll(x, t - j, 0)
    ar = pltpu.roll(after, 8 - j, 0)
    bot = jnp.where(_iota(ar.shape, 0) >= 8 - j, ar, xr[t - 8:])
    return jnp.concatenate([xr[:t - 8], bot], axis=0)


def _conv_causal(x, before, w_ref, k):
    y = w_ref[k - 1:k, :] * x
    for i in range(k - 1):
        y = y + w_ref[i:i + 1, :] * _shift_down(x, before, k - 1 - i)
    return y


def _conv_causal_bwd(dy, after, w_ref, k):
    dx = w_ref[k - 1:k, :] * dy
    for i in range(k - 1):
        dx = dx + w_ref[i:i + 1, :] * _shift_up(dy, after, k - 1 - i)
    return dx


def _scan_fwd(a, u):
    t = a.shape[0]
    rows = _iota(a.shape, 0)
    d = 1
    while d < t:
        m = rows >= d
        u = u + jnp.where(m, a * pltpu.roll(u, d, 0), 0.0)
        a = jnp.where(m, a * pltpu.roll(a, d, 0), a)
        d *= 2
    return a, u


def _scan_rev(a, u):
    t = a.shape[0]
    rows = _iota(a.shape, 0)
    d = 1
    while d < t:
        m = rows < t - d
        u = u + jnp.where(m, a * pltpu.roll(u, t - d, 0), 0.0)
        a = jnp.where(m, a * pltpu.roll(a, t - d, 0), a)
        d *= 2
    return a, u


def _chunk_cumsum(g, axis, rev=False):
    n = g.shape[axis]
    pos = _iota(g.shape, axis) & (CH - 1)
    d = 1
    while d < CH:
        if rev:
            g = g + jnp.where(pos < CH - d, pltpu.roll(g, n - d, axis), 0.0)
        else:
            g = g + jnp.where(pos >= d, pltpu.roll(g, d, axis), 0.0)
        d *= 2
    return g


def _ln_stats(r):
    mu = jnp.mean(r, axis=-1, keepdims=True)
    xc = r - mu
    var = jnp.mean(xc * xc, axis=-1, keepdims=True)
    rstd = lax.rsqrt(var + LN_EPS)
    return xc * rstd, rstd


def _ln_bwd(dy, xhat, rstd, g):
    dxh = dy * g
    return rstd * (dxh - jnp.mean(dxh, axis=-1, keepdims=True)
                   - xhat * jnp.mean(dxh * xhat, axis=-1, keepdims=True))


def _rsum(x):
    return jnp.sum(x, axis=0, keepdims=True)


def _params(sem):
    return pltpu.CompilerParams(dimension_semantics=sem, vmem_limit_bytes=VMEM_LIMIT)


def _pick(n, cands):
    for c in cands:
        if n % c == 0:
            return c
    return n


def _rows(tm, w, col=0, nt=None):
    if nt is None:
        return pl.BlockSpec((tm, w), lambda i: (i, col))
    return pl.BlockSpec((tm, w), lambda i: (nt - 1 - i, col))


def _before(tm, w, col=0, nt=None):
    r = tm // 8
    if nt is None:
        return pl.BlockSpec((8, w), lambda i: (jnp.maximum(i * r - 1, 0), col))
    return pl.BlockSpec((8, w), lambda i: (jnp.maximum((nt - 1 - i) * r - 1, 0), col))


def _whole(shape):
    return pl.BlockSpec(shape, lambda *_: (0,) * len(shape))


def _mm(a, b, *, name, trans_a=False, out_dtype=F32):
    if trans_a:
        s, m = a.shape
        _, n = b.shape
        tm = _pick(m, (512, 256, 128))
        tn = _pick(n, (512, 640, 256, 128))
        ts = _pick(s, (1024, 512, 256, 128))
        ns = s // ts

        def body(a_ref, b_ref, o_ref, acc):
            k = pl.program_id(2)

            @pl.when(k == 0)
            def _():
                acc[...] = jnp.zeros_like(acc)
            acc[...] += _dot_tn(a_ref[...], b_ref[...])

            @pl.when(k == ns - 1)
            def _():
                o_ref[...] = acc[...].astype(out_dtype)

        return pl.pallas_call(
            body, name=name, grid=(m // tm, n // tn, ns),
            in_specs=[pl.BlockSpec((ts, tm), lambda i, j, k: (k, i)),
                      pl.BlockSpec((ts, tn), lambda i, j, k: (k, j))],
            out_specs=pl.BlockSpec((tm, tn), lambda i, j, k: (i, j)),
            out_shape=jax.ShapeDtypeStruct((m, n), out_dtype),
            scratch_shapes=[pltpu.VMEM((tm, tn), F32)],
            compiler_params=_params(("parallel", "parallel", "arbitrary")),
        )(a, b)

    m, kk = a.shape
    _, n = b.shape
    tm = _pick(m, (512, 256, 128))
    tn = _pick(n, (512, 640, 256, 128))
    tk = kk if kk <= 2048 else _pick(kk, (1408, 2176, 1024, 512))
    nk = kk // tk

    def body(a_ref, b_ref, o_ref, acc):
        k = pl.program_id(2)

        @pl.when(k == 0)
        def _():
            acc[...] = jnp.zeros_like(acc)
        acc[...] += _dot(a_ref[...], b_ref[...])

        @pl.when(k == nk - 1)
        def _():
            o_ref[...] = acc[...].astype(out_dtype)

    return pl.pallas_call(
        body, name=name, grid=(m // tm, n // tn, nk),
        in_specs=[pl.BlockSpec((tm, tk), lambda i, j, k: (i, k)),
                  pl.BlockSpec((tk, tn), lambda i, j, k: (k, j))],
        out_specs=pl.BlockSpec((tm, tn), lambda i, j, k: (i, j)),
        out_shape=jax.ShapeDtypeStruct((m, n), out_dtype),
        scratch_shapes=[pltpu.VMEM((tm, tn), F32)],
        compiler_params=_params(("parallel", "parallel", "arbitrary")),
    )(a, b)


def _modulate(x, adam):
    s = x.shape[0]
    tm = _pick(s, (512, 256, 128))

    def body(x_ref, ada_ref, o_ref):
        o_ref[...] = (x_ref[...] * (1.0 + ada_ref[1:2, :]) + ada_ref[0:1, :]).astype(MXU_DT)

    return pl.pallas_call(
        body, name="modulate1", grid=(s // tm,),
        in_specs=[_rows(tm, D), _whole((8, D))], out_specs=_rows(tm, D),
        out_shape=jax.ShapeDtypeStruct((s, D), MXU_DT), compiler_params=_params(("parallel",)),
    )(x, adam)


def _rg_gates(xc, wa_ref, wx_ref, vec_ref):
    xb = xc.astype(MXU_DT)
    r = _sigmoid(jnp.dot(xb, wa_ref[...], preferred_element_type=F32) + vec_ref[1:2, :])
    ig = _sigmoid(jnp.dot(xb, wx_ref[...], preferred_element_type=F32) + vec_ref[2:3, :])
    sp = _softplus(-vec_ref[3:4, :])
    la = -RG_C * r * sp
    a = jnp.exp(la)
    mult = jnp.sqrt(_neg_expm1(2.0 * la))
    return r, ig, a, mult, sp


def _rg_fwd(px, cw, vec, wa, wx):
    s = px.shape[0]
    tm = _pick(s, (256, 128))
    w = D_RNN

    def body(xr_ref, gr_ref, cw_ref, vec_ref, wa_ref, wx_ref, h_ref, rec_ref, prev_x, prev_h):
        @pl.when(pl.program_id(0) == 0)
        def _():
            prev_x[...] = jnp.zeros_like(prev_x)
            prev_h[...] = jnp.zeros_like(prev_h)
        x = xr_ref[...]
        xc = _conv_causal(x, prev_x[...], cw_ref, 4) + vec_ref[0:1, :]
        prev_x[...] = x[tm - 8:, :]
        _, ig, a, mult, _ = _rg_gates(xc, wa_ref, wx_ref, vec_ref)
        acum, h = _scan_fwd(a, mult * ig * xc)
        h = h + acum * prev_h[7:8, :]
        prev_h[...] = h[tm - 8:, :]
        h_ref[...] = h
        rec_ref[...] = (h * _gelu(gr_ref[...])).astype(MXU_DT)

    return pl.pallas_call(
        body, name="rg_fwd", grid=(s // tm,),
        in_specs=[_rows(tm, w, 0), _rows(tm, w, 1), _whole((8, w)), _whole((8, w)),
                  _whole((w, w)), _whole((w, w))],
        out_specs=[_rows(tm, w), _rows(tm, w)],
        out_shape=[jax.ShapeDtypeStruct((s, w), F32), jax.ShapeDtypeStruct((s, w), MXU_DT)],
        scratch_shapes=[pltpu.VMEM((8, w), F32), pltpu.VMEM((8, w), F32)],
        compiler_params=_params(("arbitrary",)),
    )(px, px, cw, vec, wa, wx)


def _dn_scalars(ab, arow, drow):
    lane = _iota(ab.shape, 1)
    g = jnp.where(lane < NV, -jnp.exp(arow) * _softplus(ab + drow), 0.0)
    beta = _sigmoid(ab)
    return lane, g, beta


def _l2n_heads(c, out_ref, off, scale):
    for hh in range(NQ):
        x = c[:, off + hh * HD: off + (hh + 1) * HD]
        r = lax.rsqrt(jnp.sum(x * x, axis=-1, keepdims=True) + L2_EPS)
        out_ref[:, hh * HD:(hh + 1) * HD] = x * (r * scale)


def _dn_prep(pqkv, pab, cw, arow, drow):
    s = pqkv.shape[0]
    tm = 128
    wq = NQ * HD

    def body(x_ref, ab_ref, cw_ref, a_ref, d_ref, q_ref, k_ref, v_ref, rt_ref, prev_x):
        @pl.when(pl.program_id(0) == 0)
        def _():
            prev_x[...] = jnp.zeros_like(prev_x)
        x = x_ref[...]
        cp = _conv_causal(x, prev_x[...], cw_ref, 4)
        prev_x[...] = x[tm - 8:, :]
        c = cp * _sigmoid(cp)
        _l2n_heads(c, q_ref, 0, Q_SCALE)
        _l2n_heads(c, k_ref, wq, 1.0)
        v_ref[...] = c[:, 2 * wq:]
        lane, g, beta = _dn_scalars(ab_ref[...], a_ref[...], d_ref[...])
        gc = _chunk_cumsum(g, 0)
        gl = gc + _chunk_cumsum(g, 0, rev=True) - g
        pack = jnp.where(lane < NV, gc, jnp.where(lane < 2 * NV, beta,
                         jnp.where(lane < 3 * NV, pltpu.roll(gl, 2 * NV, 1), 0.0)))
        rt_ref[...] = pack.T[0:3 * NV, :]

    return pl.pallas_call(
        body, name="dn_prep", grid=(s // tm,),
        in_specs=[_rows(tm, 4 * wq), _rows(tm, 128), _whole((8, 4 * wq)), _whole((1, 128)), _whole((1, 128))],
        out_specs=[_rows(tm, wq), _rows(tm, wq), _rows(tm, 2 * wq),
                   pl.BlockSpec((3 * NV, tm), lambda i: (0, i))],
        out_shape=[jax.ShapeDtypeStruct((s, wq), F32), jax.ShapeDtypeStruct((s, wq), F32),
                   jax.ShapeDtypeStruct((s, 2 * wq), F32), jax.ShapeDtypeStruct((3 * NV, s), F32)],
        scratch_shapes=[pltpu.VMEM((8, 4 * wq), F32)],
        compiler_params=_params(("arbitrary",)),
    )(pqkv, pab, cw, arow, drow)


def _pair_masks():
    i = _iota((2 * CH, 2 * CH), 0)
    j = _iota((2 * CH, 2 * CH), 1)
    same = (i >> 6) == (j >> 6)
    return same & (i >= j), same & (i > j)


def _head_cols(rt_ref, h):
    shp = (2 * CH, 2 * CH)
    g_r = jnp.broadcast_to(rt_ref[pl.ds(h, 1), :], shp)
    b_r = jnp.broadcast_to(rt_ref[pl.ds(NV + h, 1), :], shp)
    l_r = jnp.broadcast_to(rt_ref[pl.ds(2 * NV + h, 1), :], shp)
    return g_r, g_r.T, b_r.T, l_r, l_r.T


def _inv_unit_lower(a):
    n = a.shape[0]
    eye = (_iota((n, n), 0) == _iota((n, n), 1)).astype(F32)
    b = -a
    x = eye + b
    bp = b
    for _ in range(5):
        bp = _dot_hi(bp, bp)
        x = x + _dot_hi(x, bp)
    return x


def _gam_rows(l_r):
    lrow = l_r[0:1, :]
    lane = _iota(lrow.shape, 1)
    other = pltpu.roll(lrow, CH, 1)
    return jnp.exp(jnp.where(lane < CH, lrow, other)), jnp.exp(jnp.where(lane >= CH, lrow, other))


def _dn_intra(q, k, v, rt):
    s = q.shape[0]
    nb = s // (2 * CH)
    blk = pl.BlockSpec((2 * CH, HD), lambda i, h: (i, h))
    blk2 = pl.BlockSpec((2 * CH, 2 * HD), lambda i, h: (i, h))

    def body(q_ref, k_ref, v_ref, rt_ref, u_ref, w_ref, qd_ref, kd_ref, p_ref, ti_ref, gam_ref):
        hq = pl.program_id(1)
        qq = q_ref[...]
        kk_ = k_ref[...]
        kk = _dot_nt(kk_, kk_)
        qk = _dot_nt(qq, kk_)
        mc, ms = _pair_masks()
        gam_ref[...] = jnp.zeros_like(gam_ref)
        for j in range(2):
            cs = slice(j * HD, (j + 1) * HD)
            g_r, g_c, b_c, l_r, l_c = _head_cols(rt_ref, 2 * hq + j)
            dec = jnp.where(mc, jnp.exp(jnp.where(mc, g_c - g_r, 0.0)), 0.0)
            a = jnp.where(ms, b_c * kk * dec, 0.0)
            tinv = _inv_unit_lower(a)
            eg = jnp.exp(g_c)
            u_ref[:, cs] = _dot_hi(tinv, b_c * v_ref[:, cs])
            w_ref[:, cs] = _dot_hi(tinv, b_c * eg * kk_)
            p_ref[:, cs] = jnp.where(mc, qk * dec, 0.0)
            ti_ref[:, cs] = tinv
            qd_ref[:, cs] = eg * qq
            kd_ref[:, cs] = jnp.exp(l_c - g_c) * kk_
            ga, gb = _gam_rows(l_r)
            gam_ref[0, 0, 2 * j:2 * j + 1, :] = ga
            gam_ref[0, 0, 2 * j + 1:2 * j + 2, :] = gb

    big = jax.ShapeDtypeStruct((s, NV * HD), F32)
    return pl.pallas_call(
        body, name="dn_intra", grid=(nb, NQ),
        in_specs=[blk, blk, blk2, pl.BlockSpec((3 * NV, 2 * CH), lambda i, h: (0, i))],
        out_specs=[blk2] * 6 + [pl.BlockSpec((1, 1, 8, 128), lambda i, h: (i, h, 0, 0))],
        out_shape=[big] * 6 + [jax.ShapeDtypeStruct((nb, NQ, 8, 128), F32)],
        compiler_params=_params(("parallel", "parallel")),
    )(q, k, v, rt)


def _dn_seq(u, w, qd, kd, p, gam):
    s = u.shape[0]
    nb = s // (2 * CH)
    wide = pl.BlockSpec((2 * CH, NV * HD), lambda i: (i, 0))

    def body(u_ref, w_ref, qd_ref, kd_ref, p_ref, gam_ref, o_ref, vn_ref, ss_ref, st):
        @pl.when(pl.program_id(0) == 0)
        def _():
            st[...] = jnp.zeros_like(st)
        for h in range(NV):
            cs = slice(h * HD, (h + 1) * HD)
            ra, rb = slice(0, CH), slice(CH, 2 * CH)
            s0 = st[h]
            ga = gam_ref[0, h // 2, 2 * (h % 2):2 * (h % 2) + 1, :]
            gb = gam_ref[0, h // 2, 2 * (h % 2) + 1:2 * (h % 2) + 2, :]
            vna = u_ref[ra, cs] - _dot(w_ref[ra, cs], s0)
            oa = _dot(qd_ref[ra, cs], s0)
            s1 = ga * s0 + _dot_tn(kd_ref[ra, cs], vna)
            vnb = u_ref[rb, cs] - _dot(w_ref[rb, cs], s1)
            ob = _dot(qd_ref[rb, cs], s1)
            s2 = gb * s1 + _dot_tn(kd_ref[rb, cs], vnb)
            vn = jnp.concatenate([vna, vnb], axis=0)
            o_ref[:, cs] = jnp.concatenate([oa, ob], axis=0) + _dot(p_ref[:, cs], vn)
            vn_ref[:, cs] = vn
            ss_ref[h, 0:HD, :] = s0
            ss_ref[h, HD:2 * HD, :] = s1
            st[h] = s2

    big = jax.ShapeDtypeStruct((s, NV * HD), F32)
    return pl.pallas_call(
        body, name="dn_seq", grid=(nb,),
        in_specs=[wide] * 5 + [pl.BlockSpec((1, NQ, 8, 128), lambda i: (i, 0, 0, 0))],
        out_specs=[wide, wide, pl.BlockSpec((NV, 2 * HD, HD), lambda i: (0, i, 0))],
        out_shape=[big, big, jax.ShapeDtypeStruct((NV, 2 * s, HD), F32)],
        scratch_shapes=[pltpu.VMEM((NV, HD, HD), F32)],
        compiler_params=_params(("arbitrary",)),
    )(u, w, qd, kd, p, gam)


def _dn_post(o, pz, nw):
    s = o.shape[0]
    tm = _pick(s, (256, 128))

    def body(o_ref, z_ref, nw_ref, y_ref):
        for h in range(NV):
            cs = slice(h * HD, (h + 1) * HD)
            oh = o_ref[:, cs]
            z = z_ref[:, cs]
            rs = lax.rsqrt(jnp.mean(oh * oh, axis=-1, keepdims=True) + RMS_EPS)
            y_ref[:, cs] = (oh * rs * nw_ref[...] * (z * _sigmoid(z))).astype(MXU_DT)

    return pl.pallas_call(
        body, name="dn_post", grid=(s // tm,),
        in_specs=[_rows(tm, NV * HD), _rows(tm, NV * HD), _whole((1, HD))],
        out_specs=_rows(tm, NV * HD),
        out_shape=jax.ShapeDtypeStruct((s, NV * HD), MXU_DT), compiler_params=_params(("parallel",)),
    )(o, pz, nw)


def _merge(pg, ya, yb):
    s = ya.shape[0]
    tm = _pick(s, (512, 256, 128))

    def body(ga_ref, gb_ref, ya_ref, yb_ref, o_ref):
        o_ref[...] = (_sigmoid(ga_ref[...]) * ya_ref[...] + _sigmoid(gb_ref[...]) * yb_ref[...]).astype(MXU_DT)

    return pl.pallas_call(
        body, name="merge", grid=(s // tm,),
        in_specs=[_rows(tm, D, 0), _rows(tm, D, 1), _rows(tm, D), _rows(tm, D)],
        out_specs=_rows(tm, D),
        out_shape=jax.ShapeDtypeStruct((s, D), MXU_DT), compiler_params=_params(("parallel",)),
    )(pg, pg, ya, yb)


def _ln1(x, mix, adam, lng, lnb):
    s = x.shape[0]
    tm = _pick(s, (512, 256, 128))

    def body(x_ref, m_ref, ada_ref, g_ref, b_ref, r_ref, x2_ref, h2_ref):
        r = ALPHA * x_ref[...] + (1.0 + ada_ref[2:3, :]) * m_ref[...]
        xhat, _ = _ln_stats(r)
        x2 = xhat * g_ref[...] + b_ref[...]
        r_ref[...] = r
        x2_ref[...] = x2
        h2_ref[...] = (x2 * (1.0 + ada_ref[4:5, :]) + ada_ref[3:4, :]).astype(MXU_DT)

    return pl.pallas_call(
        body, name="ln1", grid=(s // tm,),
        in_specs=[_rows(tm, D), _rows(tm, D), _whole((8, D)), _whole((1, D)), _whole((1, D))],
        out_specs=[_rows(tm, D)] * 3,
        out_shape=[jax.ShapeDtypeStruct((s, D), F32), jax.ShapeDtypeStruct((s, D), F32),
                   jax.ShapeDtypeStruct((s, D), MXU_DT)],
        compiler_params=_params(("parallel",)),
    )(x, mix, adam, lng, lnb)


def _ffn_act(gu, cw, cb):
    s = gu.shape[0]
    tm = _pick(s, (256, 128))
    w = D_FF

    def body(g_ref, u_ref, cw_ref, cb_ref, o_ref, prev):
        @pl.when(pl.program_id(0) == 0)
        def _():
            prev[...] = jnp.zeros_like(prev)
        g = g_ref[...]
        gc = _conv_causal(g, prev[...], cw_ref, 3) + cb_ref[...]
        prev[...] = g[tm - 8:, :]
        o_ref[...] = (_gelu(gc) * u_ref[...]).astype(MXU_DT)

    return pl.pallas_call(
        body, name="ffn_act", grid=(s // tm,),
        in_specs=[_rows(tm, w, 0), _rows(tm, w, 1), _whole((8, w)), _whole((1, w))],
        out_specs=_rows(tm, w),
        out_shape=jax.ShapeDtypeStruct((s, w), MXU_DT),
        scratch_shapes=[pltpu.VMEM((8, w), F32)],
        compiler_params=_params(("arbitrary",)),
    )(gu, gu, cw, cb)


def _ln2_loss(x2, ff, tgt, adam, lng, lnb):
    s = x2.shape[0]
    tm = _pick(s, (512, 256, 128))

    def body(x_ref, f_ref, t_ref, ada_ref, g_ref, b_ref, dff_ref, dx_ref, red_ref):
        @pl.when(pl.program_id(0) == 0)
        def _():
            red_ref[...] = jnp.zeros_like(red_ref)
        ff_ = f_ref[...]
        r = ALPHA * x_ref[...] + (1.0 + ada_ref[5:6, :]) * ff_
        xhat, rstd = _ln_stats(r)
        err = xhat * g_ref[...] + b_ref[...] - t_ref[...]
        dy = err * (1.0 / D)
        dr = _ln_bwd(dy, xhat, rstd, g_ref[...])
        dff_ref[...] = ((1.0 + ada_ref[5:6, :]) * dr).astype(MXU_DT)
        dx_ref[...] = ALPHA * dr
        red_ref[0:1, :] += _rsum(dy * xhat)
        red_ref[1:2, :] += _rsum(dy)
        red_ref[2:3, :] += _rsum(dr * ff_)
        red_ref[3:4, :] += jnp.sum(_rsum(err * err), axis=1, keepdims=True) * (0.5 / D)

    return pl.pallas_call(
        body, name="ln2_loss", grid=(s // tm,),
        in_specs=[_rows(tm, D)] * 3 + [_whole((8, D)), _whole((1, D)), _whole((1, D))],
        out_specs=[_rows(tm, D), _rows(tm, D), _whole((8, D))],
        out_shape=[jax.ShapeDtypeStruct((s, D), MXU_DT), jax.ShapeDtypeStruct((s, D), F32),
                   jax.ShapeDtypeStruct((8, D), F32)],
        compiler_params=_params(("arbitrary",)),
    )(x2, ff, tgt, adam, lng, lnb)


def _ffn_bwd(dact, gu, cw, cb):
    s = dact.shape[0]
    tm = _pick(s, (256, 128))
    nt = s // tm
    w = D_FF

    def body(da_ref, g_ref, gb_ref, u_ref, cw_ref, cb_ref, o_ref, gcw_ref, gcb_ref, nxt):
        i = pl.program_id(0)

        @pl.when(i == 0)
        def _():
            nxt[...] = jnp.zeros_like(nxt)
            gcw_ref[...] = jnp.zeros_like(gcw_ref)
            gcb_ref[...] = jnp.zeros_like(gcb_ref)
        g = g_ref[...]
        before = jnp.where(i < nt - 1, gb_ref[...], 0.0)
        gc = _conv_causal(g, before, cw_ref, 3) + cb_ref[...]
        gel, dgel = _gelu_and_grad(gc)
        da = da_ref[...]
        dgc = da * u_ref[...] * dgel
        o_ref[:, w:] = (da * gel).astype(MXU_DT)
        o_ref[:, :w] = _conv_causal_bwd(dgc, nxt[...], cw_ref, 3).astype(MXU_DT)
        nxt[...] = dgc[:8, :]
        for k in range(3):
            gcw_ref[k:k + 1, :] += _rsum(dgc * _shift_down(g, before, 2 - k))
        gcb_ref[...] += _rsum(dgc)

    return pl.pallas_call(
        body, name="ffn_bwd", grid=(nt,),
        in_specs=[_rows(tm, w, 0, nt), _rows(tm, w, 0, nt), _before(tm, w, 0, nt), _rows(tm, w, 1, nt),
                  _whole((8, w)), _whole((1, w))],
        out_specs=[_rows(tm, 2 * w, 0, nt), _whole((8, w)), _whole((1, w))],
        out_shape=[jax.ShapeDtypeStruct((s, 2 * w), MXU_DT), jax.ShapeDtypeStruct((8, w), F32),
                   jax.ShapeDtypeStruct((1, w), F32)],
        scratch_shapes=[pltpu.VMEM((8, w), F32)],
        compiler_params=_params(("arbitrary",)),
    )(dact, gu, gu, gu, cw, cb)


def _ln1_bwd(dh2, dx2a, x2, r1, mix, adam, lng):
    s = dh2.shape[0]
    tm = _pick(s, (512, 256, 128))

    def body(dh_ref, dxa_ref, x2_ref, r_ref, m_ref, ada_ref, g_ref, dm_ref, dx_ref, red_ref):
        @pl.when(pl.program_id(0) == 0)
        def _():
            red_ref[...] = jnp.zeros_like(red_ref)
        dh = dh_ref[...]
        dx2 = dxa_ref[...] + dh * (1.0 + ada_ref[4:5, :])
        xhat, rstd = _ln_stats(r_ref[...])
        dr = _ln_bwd(dx2, xhat, rstd, g_ref[...])
        dm_ref[...] = ((1.0 + ada_ref[2:3, :]) * dr).astype(MXU_DT)
        dx_ref[...] = ALPHA * dr
        red_ref[0:1, :] += _rsum(dh * x2_ref[...])
        red_ref[1:2, :] += _rsum(dh)
        red_ref[2:3, :] += _rsum(dx2 * xhat)
        red_ref[3:4, :] += _rsum(dx2)
        red_ref[4:5, :] += _rsum(dr * m_ref[...])

    return pl.pallas_call(
        body, name="ln1_bwd", grid=(s // tm,),
        in_specs=[_rows(tm, D)] * 5 + [_whole((8, D)), _whole((1, D))],
        out_specs=[_rows(tm, D), _rows(tm, D), _whole((8, D))],
        out_shape=[jax.ShapeDtypeStruct((s, D), MXU_DT), jax.ShapeDtypeStruct((s, D), F32),
                   jax.ShapeDtypeStruct((8, D), F32)],
        compiler_params=_params(("arbitrary",)),
    )(dh2, dx2a, x2, r1, mix, adam, lng)


def _merge_bwd(dmg, pg, ya, yb):
    s = dmg.shape[0]
    tm = _pick(s, (512, 256, 128))

    def body(d_ref, ga_ref, gb_ref, ya_ref, yb_ref, dya_ref, dyb_ref, dpg_ref):
        d = d_ref[...]
        sa = _sigmoid(ga_ref[...])
        sb = _sigmoid(gb_ref[...])
        dya_ref[...] = (d * sa).astype(MXU_DT)
        dyb_ref[...] = (d * sb).astype(MXU_DT)
        dpg_ref[:, :D] = (d * ya_ref[...] * sa * (1.0 - sa)).astype(MXU_DT)
        dpg_ref[:, D:] = (d * yb_ref[...] * sb * (1.0 - sb)).astype(MXU_DT)

    return pl.pallas_call(
        body, name="merge_bwd", grid=(s // tm,),
        in_specs=[_rows(tm, D), _rows(tm, D, 0), _rows(tm, D, 1), _rows(tm, D), _rows(tm, D)],
        out_specs=[_rows(tm, D), _rows(tm, D), _rows(tm, 2 * D)],
        out_shape=[jax.ShapeDtypeStruct((s, D), MXU_DT), jax.ShapeDtypeStruct((s, D), MXU_DT),
                   jax.ShapeDtypeStruct((s, 2 * D), MXU_DT)],
        compiler_params=_params(("parallel",)),
    )(dmg, pg, pg, ya, yb)


def _dn_post_bwd(ddn, o, pz, nw):
    s = o.shape[0]
    tm = _pick(s, (256, 128))

    def body(d_ref, o_ref, z_ref, nw_ref, do_ref, dz_ref, gnw_ref):
        @pl.when(pl.program_id(0) == 0)
        def _():
            gnw_ref[...] = jnp.zeros_like(gnw_ref)
        acc = jnp.zeros((1, HD), F32)
        for h in range(NV):
            cs = slice(h * HD, (h + 1) * HD)
            oh = o_ref[:, cs]
            z = z_ref[:, cs]
            d = d_ref[:, cs]
            sg = _sigmoid(z)
            rs = lax.rsqrt(jnp.mean(oh * oh, axis=-1, keepdims=True) + RMS_EPS)
            n = oh * rs
            dz_ref[:, cs] = (d * n * nw_ref[...] * sg * (1.0 + z * (1.0 - sg))).astype(MXU_DT)
            dn_ = d * (z * sg)
            acc = acc + _rsum(dn_ * n)
            dnn = dn_ * nw_ref[...]
            do_ref[:, cs] = rs * (dnn - n * jnp.mean(dnn * n, axis=-1, keepdims=True))
        gnw_ref[...] += acc

    return pl.pallas_call(
        body, name="dn_post_bwd", grid=(s // tm,),
        in_specs=[_rows(tm, NV * HD)] * 3 + [_whole((1, HD))],
        out_specs=[_rows(tm, NV * HD), _rows(tm, NV * HD), _whole((1, HD))],
        out_shape=[jax.ShapeDtypeStruct((s, NV * HD), F32), jax.ShapeDtypeStruct((s, NV * HD), MXU_DT),
                   jax.ShapeDtypeStruct((1, HD), F32)],
        compiler_params=_params(("arbitrary",)),
    )(ddn, o, pz, nw)


def _dn_seq_bwd(do, qd, kd, p, w, vn, ssave, gam):
    s = do.shape[0]
    nb = s // (2 * CH)
    wide = pl.BlockSpec((2 * CH, NV * HD), lambda i: (nb - 1 - i, 0))
    gspec = pl.BlockSpec((1, NQ, 8, 128), lambda i: (nb - 1 - i, 0, 0, 0))

    def body(do_ref, qd_ref, kd_ref, p_ref, w_ref, vn_ref, ss_ref, gam_ref, dvn_ref, dkd_ref, dgam_ref, dst):
        @pl.when(pl.program_id(0) == 0)
        def _():
            dst[...] = jnp.zeros_like(dst)
        dgam_ref[...] = jnp.zeros_like(dgam_ref)
        ra, rb = slice(0, CH), slice(CH, 2 * CH)
        for h in range(NV):
            cs = slice(h * HD, (h + 1) * HD)
            row = 2 * (h % 2)
            ds2 = dst[h]
            s0 = ss_ref[h, 0:HD, :]
            s1 = ss_ref[h, HD:2 * HD, :]
            ga = gam_ref[0, h // 2, row:row + 1, :]
            gb = gam_ref[0, h // 2, row + 1:row + 2, :]
            dob = do_ref[:, cs]
            pdo = _dot_tn(p_ref[:, cs], dob)
            dvb = pdo[rb] + _dot(kd_ref[rb, cs], ds2)
            dkd_ref[rb, cs] = _dot_nt(vn_ref[rb, cs], ds2)
            dgb = jnp.sum(jnp.sum(ds2 * s1, axis=1, keepdims=True), axis=0, keepdims=True)
            ds1 = gb * ds2 + _dot_tn(qd_ref[rb, cs], dob[rb]) - _dot_tn(w_ref[rb, cs], dvb)
            dva = pdo[ra] + _dot(kd_ref[ra, cs], ds1)
            dkd_ref[ra, cs] = _dot_nt(vn_ref[ra, cs], ds1)
            dga = jnp.sum(jnp.sum(ds1 * s0, axis=1, keepdims=True), axis=0, keepdims=True)
            ds0 = ga * ds1 + _dot_tn(qd_ref[ra, cs], dob[ra]) - _dot_tn(w_ref[ra, cs], dva)
            dvn_ref[ra, cs] = dva
            dvn_ref[rb, cs] = dvb
            dgam_ref[0, h // 2, row:row + 1, :] = jnp.broadcast_to(dga, (1, 128))
            dgam_ref[0, h // 2, row + 1:row + 2, :] = jnp.broadcast_to(dgb, (1, 128))
            dst[h] = ds0

    big = jax.ShapeDtypeStruct((s, NV * HD), F32)
    return pl.pallas_call(
        body, name="dn_seq_bwd", grid=(nb,),
        in_specs=[wide] * 6 + [pl.BlockSpec((NV, 2 * HD, HD), lambda i: (0, nb - 1 - i, 0)), gspec],
        out_specs=[wide, wide, gspec],
        out_shape=[big, big, jax.ShapeDtypeStruct((nb, NQ, 8, 128), F32)],
        scratch_shapes=[pltpu.VMEM((NV, HD, HD), F32)],
        compiler_params=_params(("arbitrary",)),
    )(do, qd, kd, p, w, vn, ssave, gam)


def _dn_intra_bwd(q, k, v, rt, do, dvn, dkd, gam, dgam, ssave, tinv, u, w, vn):
    s = q.shape[0]
    nb = s // (2 * CH)
    blk = pl.BlockSpec((2 * CH, HD), lambda i, h: (i, h))
    blk2 = pl.BlockSpec((2 * CH, 2 * HD), lambda i, h: (i, h))
    gspec = pl.BlockSpec((1, 1, 8, 128), lambda i, h: (i, h, 0, 0))
    rspec = pl.BlockSpec((3 * NV, 2 * CH), lambda i, h: (0, i))

    def body(q_ref, k_ref, v_ref, rt_ref, do_ref, dvn_ref, dkd_ref, gam_ref, dgam_ref, ss_ref,
             ti_ref, u_ref, w_ref, vn_ref, dq_ref, dk_ref, dv_ref, drt_ref, acc):
        hq = pl.program_id(1)

        @pl.when(hq == 0)
        def _():
            acc[...] = jnp.zeros_like(acc)
        qq = q_ref[...]
        kk_ = k_ref[...]
        kk = _dot_nt(kk_, kk_)
        qk = _dot_nt(qq, kk_)
        mc, ms = _pair_masks()
        ra, rb = slice(0, CH), slice(CH, 2 * CH)
        lane = _iota((1, 2 * CH), 1)
        dq = jnp.zeros_like(qq)
        dk = jnp.zeros_like(kk_)
        for j in range(2):
            cs = slice(j * HD, (j + 1) * HD)
            g_r, g_c, b_c, l_r, l_c = _head_cols(rt_ref, 2 * hq + j)
            dec = jnp.where(mc, jnp.exp(jnp.where(mc, g_c - g_r, 0.0)), 0.0)
            a = jnp.where(ms, b_c * kk * dec, 0.0)
            pm = jnp.where(mc, qk * dec, 0.0)
            eg = jnp.exp(g_c)
            egl = jnp.exp(l_c - g_c)
            qd = eg * qq
            kd = egl * kk_
            s0 = ss_ref[j, 0:HD, :]
            s1 = ss_ref[j, HD:2 * HD, :]
            tinv = ti_ref[:, cs]
            uu = u_ref[:, cs]
            ww = w_ref[:, cs]
            vv = v_ref[:, cs]
            dob = do_ref[:, cs]
            dvb = dvn_ref[:, cs]
            dkdb = dkd_ref[:, cs]
            dqd = jnp.concatenate([_dot_nt(dob[ra], s0), _dot_nt(dob[rb], s1)], axis=0)
            dw = -jnp.concatenate([_dot_nt(dvb[ra], s0), _dot_nt(dvb[rb], s1)], axis=0)
            dbu = _dot_tn_hi(tinv, dvb)
            dbw = _dot_tn_hi(tinv, dw)
            da = jnp.where(ms, -(_dot_nt(dbu, uu) + _dot_nt(dbw, ww)), 0.0)
            dp = jnp.where(mc, _dot_nt(dob, vn_ref[:, cs]), 0.0)
            dm = da * dec
            dn_ = dp * dec
            dbk = _dot(dm, kk_)
            dq = dq + _dot(dn_, kk_) + eg * dqd
            dk = dk + _dot_tn(dm, b_c * kk_) + _dot_tn(dn_, qq) + egl * dkdb + b_c * (eg * dbw + dbk)
            dv_ref[:, cs] = b_c * dbu
            e = da * a + dp * pm
            x = dkdb * kd
            z = e + dqd * qd - x + dbw * (b_c * eg * kk_)
            zb = dbw * (eg * kk_) + dbu * vv + dbk * kk_
            drow = _rsum(z.T - e)
            brow = _rsum(zb.T)
            sa = jnp.sum(jnp.sum(x[ra], axis=1, keepdims=True), axis=0, keepdims=True)
            sb = jnp.sum(jnp.sum(x[rb], axis=1, keepdims=True), axis=0, keepdims=True)
            la = sa + dgam_ref[0, 0, 2 * j:2 * j + 1, :] * gam_ref[0, 0, 2 * j:2 * j + 1, :]
            lb = sb + dgam_ref[0, 0, 2 * j + 1:2 * j + 2, :] * gam_ref[0, 0, 2 * j + 1:2 * j + 2, :]
            lrow = jnp.where(lane < CH, la, lb)
            h = 2 * hq + j
            acc[pl.ds(h, 1), :] = drow
            acc[pl.ds(NV + h, 1), :] = brow
            acc[pl.ds(2 * NV + h, 1), :] = lrow
        dq_ref[...] = dq
        dk_ref[...] = dk

        @pl.when(hq == NQ - 1)
        def _():
            drt_ref[...] = acc[...]

    return pl.pallas_call(
        body, name="dn_intra_bwd", grid=(nb, NQ),
        in_specs=[blk, blk, blk2, rspec, blk2, blk2, blk2, gspec, gspec,
                  pl.BlockSpec((2, 2 * HD, HD), lambda i, h: (h, i, 0)), blk2, blk2, blk2, blk2],
        out_specs=[blk, blk, blk2, rspec],
        out_shape=[jax.ShapeDtypeStruct((s, NQ * HD), F32), jax.ShapeDtypeStruct((s, NQ * HD), F32),
                   jax.ShapeDtypeStruct((s, NV * HD), F32), jax.ShapeDtypeStruct((3 * NV, s), F32)],
        scratch_shapes=[pltpu.VMEM((3 * NV, 2 * CH), F32)],
        compiler_params=_params(("parallel", "arbitrary")),
    )(q, k, v, rt, do, dvn, dkd, gam, dgam, ssave, tinv, u, w, vn)


def _l2n_heads_bwd(c, d_ref, dc_ref, off, scale):
    for hh in range(NQ):
        cs = slice(off + hh * HD, off + (hh + 1) * HD)
        x = c[:, cs]
        dy = d_ref[:, hh * HD:(hh + 1) * HD]
        r = lax.rsqrt(jnp.sum(x * x, axis=-1, keepdims=True) + L2_EPS)
        dc_ref[:, cs] = (scale * r) * (dy - x * (r * r) * jnp.sum(dy * x, axis=-1, keepdims=True))


def _dn_prep_bwd(dq, dk, dv, drt, pqkv, pab, cw, arow, drow, acol, dcol):
    s = pqkv.shape[0]
    tm = 128
    nt = s // tm
    wq = NQ * HD

    def body(dq_ref, dk_ref, dv_ref, drt_ref, x_ref, xb_ref, ab_ref, cw_ref, ar_ref, dr_ref, ac_ref,
             dc_ref, dx_ref, dab_ref, gcw_ref, gsc_ref, dcs, nxt):
        i = pl.program_id(0)

        @pl.when(i == 0)
        def _():
            nxt[...] = jnp.zeros_like(nxt)
            gcw_ref[...] = jnp.zeros_like(gcw_ref)
            gsc_ref[...] = jnp.zeros_like(gsc_ref)
        x = x_ref[...]
        before = jnp.where(i < nt - 1, xb_ref[...], 0.0)
        cp = _conv_causal(x, before, cw_ref, 4)
        sg = _sigmoid(cp)
        c = cp * sg
        _l2n_heads_bwd(c, dq_ref, dcs, 0, Q_SCALE)
        _l2n_heads_bwd(c, dk_ref, dcs, wq, 1.0)
        dcs[:, 2 * wq:] = dv_ref[...]
        dcp = dcs[...] * (sg * (1.0 + cp * (1.0 - sg)))
        dx_ref[...] = _conv_causal_bwd(dcp, nxt[...], cw_ref, 4).astype(MXU_DT)
        nxt[...] = dcp[:8, :]
        for kq in range(4):
            gcw_ref[kq:kq + 1, :] += _rsum(dcp * _shift_down(x, before, 3 - kq))
        lane = _iota((NV, tm), 1)
        dgt = drt_ref[0:NV, :] + jnp.where((lane & (CH - 1)) == CH - 1, drt_ref[2 * NV:3 * NV, :], 0.0)
        dg = _chunk_cumsum(dgt, 1, rev=True)
        abt = ab_ref[...].T
        zt = abt[0:NV, :] + dc_ref[...]
        gt = -jnp.exp(ac_ref[...]) * _softplus(zt)
        dat = dg * (-jnp.exp(ac_ref[...])) * _sigmoid(zt)
        bt = _sigmoid(abt[NV:2 * NV, :])
        dbt = drt_ref[NV:2 * NV, :] * bt * (1.0 - bt)
        full = jnp.concatenate([dat, dbt, jnp.zeros((128 - 2 * NV, tm), F32)], axis=0)
        dab_ref[...] = full.T.astype(MXU_DT)
        l2 = _iota((NV, 128), 1)
        gsc_ref[...] += jnp.where(l2 == 0, jnp.sum(dg * gt, axis=1, keepdims=True),
                                  jnp.where(l2 == 1, jnp.sum(dat, axis=1, keepdims=True), 0.0))

    return pl.pallas_call(
        body, name="dn_prep_bwd", grid=(nt,),
        in_specs=[_rows(tm, wq, 0, nt), _rows(tm, wq, 0, nt), _rows(tm, 2 * wq, 0, nt),
                  pl.BlockSpec((3 * NV, tm), lambda i: (0, nt - 1 - i)),
                  _rows(tm, 4 * wq, 0, nt), _before(tm, 4 * wq, 0, nt), _rows(tm, 128, 0, nt),
                  _whole((8, 4 * wq)), _whole((1, 128)), _whole((1, 128)), _whole((NV, 1)), _whole((NV, 1))],
        out_specs=[_rows(tm, 4 * wq, 0, nt), _rows(tm, 128, 0, nt), _whole((8, 4 * wq)), _whole((NV, 128))],
        out_shape=[jax.ShapeDtypeStruct((s, 4 * wq), MXU_DT), jax.ShapeDtypeStruct((s, 128), MXU_DT),
                   jax.ShapeDtypeStruct((8, 4 * wq), F32), jax.ShapeDtypeStruct((NV, 128), F32)],
        scratch_shapes=[pltpu.VMEM((tm, 4 * wq), F32), pltpu.VMEM((8, 4 * wq), F32)],
        compiler_params=_params(("arbitrary",)),
    )(dq, dk, dv, drt, pqkv, pqkv, pab, cw, arow, drow, acol, dcol)


def _rg_bwd(drec, px, h, cw, vec, wa, wx, wat, wxt):
    s = px.shape[0]
    tm = _pick(s, (256, 128))
    nt = s // tm
    w = D_RNN

    def body(dr_ref, xr_ref, xb_ref, gr_ref, h_ref, hb_ref, cw_ref, vec_ref, wa_ref, wx_ref, wat_ref, wxt_ref,
             o_ref, gwa_ref, gwx_ref, gcw_ref, gvec_ref, nxt_a, nxt_l, nxt_d):
        i = pl.program_id(0)

        @pl.when(i == 0)
        def _():
            nxt_a[...] = jnp.zeros_like(nxt_a)
            nxt_l[...] = jnp.zeros_like(nxt_l)
            nxt_d[...] = jnp.zeros_like(nxt_d)
            gwa_ref[...] = jnp.zeros_like(gwa_ref)
            gwx_ref[...] = jnp.zeros_like(gwx_ref)
            gcw_ref[...] = jnp.zeros_like(gcw_ref)
            gvec_ref[...] = jnp.zeros_like(gvec_ref)
        first = i == nt - 1
        x = xr_ref[...]
        xbefore = jnp.where(first, 0.0, xb_ref[...])
        hbefore = jnp.where(first, 0.0, hb_ref[...])
        xc = _conv_causal(x, xbefore, cw_ref, 4) + vec_ref[0:1, :]
        r, ig, a, mult, sp = _rg_gates(xc, wa_ref, wx_ref, vec_ref)
        hh = h_ref[...]
        gel, dgel = _gelu_and_grad(gr_ref[...])
        drec_ = dr_ref[...]
        o_ref[:, w:] = (drec_ * hh * dgel).astype(MXU_DT)
        acum, lam = _scan_rev(_shift_up(a, nxt_a[...], 1), drec_ * gel)
        lam = lam + acum * nxt_l[0:1, :]
        nxt_a[...] = a[:8, :]
        nxt_l[...] = lam[:8, :]
        da = lam * _shift_down(hh, hbefore, 1)
        dxc = lam * mult * ig
        dla = da * a - (lam * ig * xc) * (a * a) / mult
        dpr = dla * (-RG_C * sp) * r * (1.0 - r)
        dpi = (lam * mult * xc) * ig * (1.0 - ig)
        dprb = dpr.astype(MXU_DT)
        dpib = dpi.astype(MXU_DT)
        dxc = dxc + jnp.dot(dprb, wat_ref[...], preferred_element_type=F32) \
                  + jnp.dot(dpib, wxt_ref[...], preferred_element_type=F32)
        xcb = xc.astype(MXU_DT)
        gwa_ref[...] += _dot_tn(xcb, dprb)
        gwx_ref[...] += _dot_tn(xcb, dpib)
        o_ref[:, :w] = _conv_causal_bwd(dxc, nxt_d[...], cw_ref, 4).astype(MXU_DT)
        nxt_d[...] = dxc[:8, :]
        for kq in range(4):
            gcw_ref[kq:kq + 1, :] += _rsum(dxc * _shift_down(x, xbefore, 3 - kq))
        gvec_ref[0:1, :] += _rsum(dxc)
        gvec_ref[1:2, :] += _rsum(dpr)
        gvec_ref[2:3, :] += _rsum(dpi)
        gvec_ref[3:4, :] += _rsum(dla * (-RG_C * r)) * (-_sigmoid(-vec_ref[3:4, :]))

    return pl.pallas_call(
        body, name="rg_bwd", grid=(nt,),
        in_specs=[_rows(tm, w, 0, nt), _rows(tm, w, 0, nt), _before(tm, w, 0, nt), _rows(tm, w, 1, nt),
                  _rows(tm, w, 0, nt), _before(tm, w, 0, nt), _whole((8, w)), _whole((8, w)),
                  _whole((w, w)), _whole((w, w)), _whole((w, w)), _whole((w, w))],
        out_specs=[_rows(tm, 2 * w, 0, nt), _whole((w, w)), _whole((w, w)), _whole((8, w)), _whole((8, w))],
        out_shape=[jax.ShapeDtypeStruct((s, 2 * w), MXU_DT), jax.ShapeDtypeStruct((w, w), F32),
                   jax.ShapeDtypeStruct((w, w), F32), jax.ShapeDtypeStruct((8, w), F32),
                   jax.ShapeDtypeStruct((8, w), F32)],
        scratch_shapes=[pltpu.VMEM((8, w), F32)] * 3,
        compiler_params=_params(("arbitrary",)),
    )(drec, px, px, px, h, h, cw, vec, wa, wx, wat, wxt)


def _modulate_bwd(dh1, dxa, x, adam):
    s = x.shape[0]
    tm = _pick(s, (512, 256, 128))

    def body(dh_ref, dxa_ref, x_ref, ada_ref, gx_ref, red_ref):
        @pl.when(pl.program_id(0) == 0)
        def _():
            red_ref[...] = jnp.zeros_like(red_ref)
        dh = dh_ref[...]
        gx_ref[...] = dxa_ref[...] + dh * (1.0 + ada_ref[1:2, :])
        red_ref[0:1, :] += _rsum(dh * x_ref[...])
        red_ref[1:2, :] += _rsum(dh)

    return pl.pallas_call(
        body, name="modulate1_bwd", grid=(s // tm,),
        in_specs=[_rows(tm, D)] * 3 + [_whole((8, D))],
        out_specs=[_rows(tm, D), _whole((8, D))],
        out_shape=[jax.ShapeDtypeStruct((s, D), F32), jax.ShapeDtypeStruct((8, D), F32)],
        compiler_params=_params(("arbitrary",)),
    )(dh1, dxa, x, adam)


def _adamw(parts, w, m, v, name):
    r, c = w.shape
    tm = _pick(r, (256, 128, 64, 32, 16, 8))
    n_parts = len(parts)
    c1 = 1.0 - ADAM_B1 ** ADAM_STEP
    c2 = 1.0 - ADAM_B2 ** ADAM_STEP

    def body(*refs):
        g = refs[0][...]
        for p_ref in refs[1:n_parts]:
            g = g + p_ref[...]
        w_ref, m_ref, v_ref, g_out, d_out, m_out, v_out = refs[n_parts:]
        mn = ADAM_B1 * m_ref[...] + (1.0 - ADAM_B1) * g
        vn = ADAM_B2 * v_ref[...] + (1.0 - ADAM_B2) * (g * g)
        g_out[...] = g
        m_out[...] = mn
        v_out[...] = vn
        d_out[...] = -ADAM_LR * ((mn / c1) / (jnp.sqrt(vn / c2) + ADAM_EPS) + ADAM_WD * w_ref[...])

    spec = pl.BlockSpec((tm, c), lambda i: (i, 0))
    return pl.pallas_call(
        body, name=name, grid=(r // tm,),
        in_specs=[spec] * (n_parts + 3), out_specs=[spec] * 4,
        out_shape=[jax.ShapeDtypeStruct((r, c), F32)] * 4, compiler_params=_params(("parallel",)),
    )(*parts, w, m, v)


def _sum_partials(own, recv):
    r, c = own.shape
    tm = _pick(r, (256, 128, 64, 32, 16))

    def body(o_ref, r_ref, out_ref):
        out_ref[...] = ((o_ref[...] + r_ref[0].astype(F32)) + r_ref[1].astype(F32)) + r_ref[2].astype(F32)

    return pl.pallas_call(
        body, name="sum_partials", grid=(r // tm,),
        in_specs=[pl.BlockSpec((tm, c), lambda i: (i, 0)), pl.BlockSpec((3, tm, c), lambda i: (0, i, 0))],
        out_specs=pl.BlockSpec((tm, c), lambda i: (i, 0)),
        out_shape=jax.ShapeDtypeStruct((r, c), F32), compiler_params=_params(("parallel",)),
    )(own, recv)


def _sum8(g):
    _, r, c = g.shape
    tm = _pick(r, (256, 128, 64, 32, 16, 8))

    def body(g_ref, o_ref):
        acc = g_ref[0]
        for k in range(1, 8):
            acc = acc + g_ref[k]
        o_ref[...] = acc

    return pl.pallas_call(
        body, name="sum8", grid=(r // tm,),
        in_specs=[pl.BlockSpec((8, tm, c), lambda i: (0, i, 0))],
        out_specs=pl.BlockSpec((tm, c), lambda i: (i, 0)),
        out_shape=jax.ShapeDtypeStruct((r, c), F32), compiler_params=_params(("parallel",)),
    )(g)


def _silu_rows(x):
    def body(x_ref, o_ref):
        xx = x_ref[...]
        o_ref[...] = xx * _sigmoid(xx)

    return pl.pallas_call(body, name="silu_rows", out_shape=jax.ShapeDtypeStruct(x.shape, F32))(x)


def _coords():
    return lax.axis_index("x"), lax.axis_index("y"), lax.axis_index("c")


def _allgather8(v, name):
    r, n = v.shape

    def body(v_ref, out_ref, send_sems, recv_sems):
        x, y, c = _coords()
        me = 4 * x + 2 * y + c
        out_ref[me] = v_ref[...]
        peers = []
        for k in range(1, 8):
            px = 1 - x if k & 4 else x
            py = 1 - y if k & 2 else y
            pc = 1 - c if k & 1 else c
            peers.append((px, py, pc))

        def copy(k, slot, to):
            return pltpu.make_async_remote_copy(
                src_ref=v_ref, dst_ref=out_ref.at[slot], send_sem=send_sems.at[k], recv_sem=recv_sems.at[k],
                device_id=to, device_id_type=MESH)

        sends = [copy(k, me, p) for k, p in enumerate(peers)]
        for cp in sends:
            cp.start()
        for k, (px, py, pc) in enumerate(peers):
            copy(k, 4 * px + 2 * py + pc, (px, py, pc)).wait_recv()
        for cp in sends:
            cp.wait_send()

    return pl.pallas_call(
        body, name=name, out_shape=jax.ShapeDtypeStruct((8, r, n), v.dtype),
        in_specs=[pl.BlockSpec(memory_space=pltpu.VMEM)], out_specs=pl.BlockSpec(memory_space=pltpu.VMEM),
        scratch_shapes=[pltpu.SemaphoreType.DMA((7,)), pltpu.SemaphoreType.DMA((7,))],
        compiler_params=pltpu.CompilerParams(vmem_limit_bytes=VMEM_LIMIT),
    )(v)


def _other_chips(x, y):
    return [(1 - x, y), (x, 1 - y), (1 - x, 1 - y)]


def _gather_shards(v):
    r, ncol = v.shape
    rh = r // 2

    def body(v_ref, out_ref, send_sems, recv_sems, local_sem):
        x, y, c = _coords()
        s_me = 2 * x + y
        sibling = (x, y, 1 - c)
        chips = _other_chips(x, y)
        mine = pltpu.make_async_copy(v_ref, out_ref.at[s_me], local_sem)
        mine.start()

        def half(slot, hc):
            return out_ref.at[slot, pl.ds(pl.multiple_of(hc * rh, 16), rh), :]

        def copy(k, src, dst, to):
            return pltpu.make_async_remote_copy(src_ref=src, dst_ref=dst, send_sem=send_sems.at[k],
                                                recv_sem=recv_sems.at[k], device_id=to, device_id_type=MESH)

        my_half = v_ref.at[pl.ds(pl.multiple_of(c * rh, 16), rh), :]
        first = [copy(j, my_half, half(s_me, c), (px, py, c)) for j, (px, py) in enumerate(chips)]
        for cp in first:
            cp.start()
        passed = []
        for j, (px, py) in enumerate(chips):
            land = half(2 * px + py, c)
            copy(j, land, land, (px, py, c)).wait_recv()
            fw = copy(3 + j, land, land, sibling)
            fw.start()
            passed.append(fw)
        for j, (px, py) in enumerate(chips):
            land = half(2 * px + py, 1 - c)
            copy(3 + j, land, land, sibling).wait_recv()
        for cp in first + passed:
            cp.wait_send()
        mine.wait()

    return pl.pallas_call(
        body, name="gather_shards", out_shape=jax.ShapeDtypeStruct((4, r, ncol), v.dtype),
        in_specs=[pl.BlockSpec(memory_space=pl.ANY)], out_specs=pl.BlockSpec(memory_space=pl.ANY),
        scratch_shapes=[pltpu.SemaphoreType.DMA((6,)), pltpu.SemaphoreType.DMA((6,)), pltpu.SemaphoreType.DMA],
    )(v)


def _scatter_partials(g):
    _, r, ncol = g.shape

    def body(g_ref, out_ref, send_sems, recv_sems):
        x, y, c = _coords()
        chips = _other_chips(x, y)

        def copy(j, px, py):
            return pltpu.make_async_remote_copy(
                src_ref=g_ref.at[2 * px + py], dst_ref=out_ref.at[j], send_sem=send_sems.at[j],
                recv_sem=recv_sems.at[j], device_id=(px, py, c), device_id_type=MESH)

        cps = [copy(j, px, py) for j, (px, py) in enumerate(chips)]
        for cp in cps:
            cp.start()
        for cp in cps:
            cp.wait_recv()
        for cp in cps:
            cp.wait_send()

    return pl.pallas_call(
        body, name="scatter_partials", out_shape=jax.ShapeDtypeStruct((3, r, ncol), g.dtype),
        in_specs=[pl.BlockSpec(memory_space=pl.ANY)], out_specs=pl.BlockSpec(memory_space=pl.ANY),
        scratch_shapes=[pltpu.SemaphoreType.DMA((3,)), pltpu.SemaphoreType.DMA((3,))],
    )(g)


def _swap_sibling(v):
    def body(v_ref, out_ref, send_sem, recv_sem):
        x, y, c = _coords()
        cp = pltpu.make_async_remote_copy(src_ref=v_ref, dst_ref=out_ref, send_sem=send_sem, recv_sem=recv_sem,
                                          device_id=(x, y, 1 - c), device_id_type=MESH)
        cp.start()
        cp.wait()

    return pl.pallas_call(
        body, name="swap_sibling", out_shape=jax.ShapeDtypeStruct(v.shape, v.dtype),
        in_specs=[pl.BlockSpec(memory_space=pl.ANY)], out_specs=pl.BlockSpec(memory_space=pl.ANY),
        scratch_shapes=[pltpu.SemaphoreType.DMA, pltpu.SemaphoreType.DMA],
    )(v)


def _pad_rows(a, rows):
    return jnp.pad(a, ((0, rows - a.shape[0]), (0, 0)))


def _block_diag(w):
    eye = jnp.eye(RG_BLOCKS, dtype=w.dtype)
    return (eye[:, None, :, None] * w[:, :, None, :]).reshape(D_RNN, D_RNN)


def _diag_blocks(g):
    g4 = g.reshape(RG_BLOCKS, RG_BW, RG_BLOCKS, RG_BW)
    idx = jnp.arange(RG_BLOCKS)
    return g4[idx, :, idx, :]


def _prepare_weights(p):
    w = {}
    wi = p["w_in"].astype(MXU_DT)
    cat = jnp.concatenate([wi[:, 2560:6656], wi[:, 6656:8704], wi[:, 8736:10784], wi[:, 0:2560],
                           wi[:, 8704:8736], jnp.zeros((D, 96), MXU_DT)], axis=1)
    w["in_cat"], w["in_cat_t"] = cat, cat.T
    for k_, n_ in (("pa", "w_proj_a"), ("pb", "w_proj_b"), ("out", "w_out"), ("down", "ffn_w_down")):
        w[k_] = p[n_].astype(MXU_DT)
        w[k_ + "_t"] = w[k_].T
    w["gu"] = jnp.concatenate([p["ffn_w_gate"], p["ffn_w_up"]], axis=1).astype(MXU_DT)
    w["gu_t"] = w["gu"].T
    w["rg_cw"] = _pad_rows(p["rg_conv_w"], 8)
    w["dn_cw"] = _pad_rows(p["dn_conv_w"], 8)
    w["ffn_cw"] = _pad_rows(p["ffn_conv_w"], 8)
    w["rg_vec"] = _pad_rows(jnp.stack([p["rg_conv_b"], p["rg_b_a"], p["rg_b_x"], p["rg_lambda"]]), 8)
    w["wa"] = _block_diag(p["rg_w_a"]).astype(MXU_DT)
    w["wx"] = _block_diag(p["rg_w_x"]).astype(MXU_DT)
    w["wa_t"], w["wx_t"] = w["wa"].T, w["wx"].T
    w["arow"] = jnp.pad(p["dn_a_log"], (0, 128 - NV))[None, :]
    w["drow"] = jnp.pad(p["dn_dt_bias"], (0, 128 - NV))[None, :]
    w["acol"] = p["dn_a_log"][:, None]
    w["dcol"] = p["dn_dt_bias"][:, None]
    w["nw"] = p["dn_norm_w"][None, :]
    w["ffn_cb"] = p["ffn_conv_b"][None, :]
    for n_ in ("ln1_g", "ln1_b", "ln2_g", "ln2_b"):
        w[n_] = p[n_][None, :]
    return w


def _local_step(x, tgt, adam, w):
    h1 = _modulate(x, adam)
    cat = w["in_cat"]
    pqkv = _mm(h1, cat[:, 0:4096], name="proj_qkv")
    pz = _mm(h1, cat[:, 4096:6144], name="proj_z")
    pg = _mm(h1, cat[:, 6144:8192], name="proj_g")
    px = _mm(h1, cat[:, 8192:10752], name="proj_x")
    pab = _mm(h1, cat[:, 10752:10880], name="proj_ab")
    hrec, rec = _rg_fwd(px, w["rg_cw"], w["rg_vec"], w["wa"], w["wx"])
    q, k, v, rt = _dn_prep(pqkv, pab, w["dn_cw"], w["arow"], w["drow"])
    u, ww, qd, kd, pm, tinv, gam = _dn_intra(q, k, v, rt)
    o, vn, ssave = _dn_seq(u, ww, qd, kd, pm, gam)
    dn = _dn_post(o, pz, w["nw"])
    ya = _mm(rec, w["pa"], name="proj_a")
    yb = _mm(dn, w["pb"], name="proj_b")
    merged = _merge(pg, ya, yb)
    mix = _mm(merged, w["out"], name="proj_out")
    r1, x2, h2 = _ln1(x, mix, adam, w["ln1_g"], w["ln1_b"])
    gu = _mm(h2, w["gu"], name="ffn_gu")
    act = _ffn_act(gu, w["ffn_cw"], w["ffn_cb"])
    ff = _mm(act, w["down"], name="ffn_down")
    dff, dx2a, red2 = _ln2_loss(x2, ff, tgt, adam, w["ln2_g"], w["ln2_b"])
    g = {}
    dact = _mm(dff, w["down_t"], name="d_act")
    g["ffn_w_down"] = _mm(act, dff, name="g_down", trans_a=True)
    dgu, gcw_f, gcb_f = _ffn_bwd(dact, gu, w["ffn_cw"], w["ffn_cb"])
    dh2 = _mm(dgu, w["gu_t"], name="d_h2")
    ggu = _mm(h2, dgu, name="g_gu", trans_a=True)
    g["ffn_w_gate"], g["ffn_w_up"] = ggu[:, :D_FF], ggu[:, D_FF:]
    g["ffn_conv_w"], g["ffn_conv_b"] = gcw_f[0:3], gcb_f[0]
    dmix, dxa, red1 = _ln1_bwd(dh2, dx2a, x2, r1, mix, adam, w["ln1_g"])
    dmg = _mm(dmix, w["out_t"], name="d_merged")
    g["w_out"] = _mm(merged, dmix, name="g_out", trans_a=True)
    dya, dyb, dpg = _merge_bwd(dmg, pg, ya, yb)
    drec = _mm(dya, w["pa_t"], name="d_rec")
    g["w_proj_a"] = _mm(rec, dya, name="g_pa", trans_a=True)
    ddn = _mm(dyb, w["pb_t"], name="d_dn")
    g["w_proj_b"] = _mm(dn, dyb, name="g_pb", trans_a=True)
    do, dpz, gnw = _dn_post_bwd(ddn, o, pz, w["nw"])
    dvn, dkd, dgam = _dn_seq_bwd(do, qd, kd, pm, ww, vn, ssave, gam)
    dq, dk, dv, drt = _dn_intra_bwd(q, k, v, rt, do, dvn, dkd, gam, dgam, ssave, tinv, u, ww, vn)
    dpqkv, dpab, gcw_d, gsc = _dn_prep_bwd(dq, dk, dv, drt, pqkv, pab, w["dn_cw"], w["arow"], w["drow"],
                                           w["acol"], w["dcol"])
    dpx, gwa, gwx, gcw_r, gvec = _rg_bwd(drec, px, hrec, w["rg_cw"], w["rg_vec"], w["wa"], w["wx"],
                                         w["wa_t"], w["wx_t"])
    dproj = jnp.concatenate([dpqkv, dpz, dpg, dpx, dpab], axis=1)
    dh1 = _mm(dproj, w["in_cat_t"], name="d_h1")
    gc = _mm(h1, dproj, name="g_in", trans_a=True)
    g["w_in"] = jnp.concatenate([gc[:, 8192:10752], gc[:, 0:4096], gc[:, 4096:6144], gc[:, 10752:10784],
                                 gc[:, 6144:8192]], axis=1)
    gx, red0 = _modulate_bwd(dh1, dxa, x, adam)
    g["rg_conv_w"], g["rg_conv_b"] = gcw_r[0:4], gvec[0]
    g["rg_w_a"], g["rg_w_x"] = _diag_blocks(gwa), _diag_blocks(gwx)
    g["rg_b_a"], g["rg_b_x"], g["rg_lambda"] = gvec[1], gvec[2], gvec[3]
    g["dn_conv_w"] = gcw_d[0:4]
    g["dn_a_log"], g["dn_dt_bias"], g["dn_norm_w"] = gsc[:, 0], gsc[:, 1], gnw[0]
    g["ln1_g"], g["ln1_b"] = red1[2], red1[3]
    g["ln2_g"], g["ln2_b"] = red2[0], red2[1]
    d_ada = jnp.concatenate([red0[1], red0[0], red1[4], red1[1], red1[0], red2[2]])
    return red2[3, 0], gx, g, d_ada


_BIG = ("w_in", "w_proj_a", "w_proj_b", "w_out", "ffn_w_gate", "ffn_w_up", "ffn_w_down")
_COL_SHARDED = ("w_in", "ffn_w_gate", "ffn_w_up")
_CONV = ("rg_conv_w", "dn_conv_w", "ffn_conv_w")
_REPL = ("b_ada", "rg_conv_b", "rg_w_a", "rg_b_a", "rg_w_x", "rg_b_x", "rg_lambda", "dn_a_log",
         "dn_dt_bias", "dn_norm_w", "ln1_g", "ln1_b", "ffn_conv_b", "ln2_g", "ln2_b")
_NAMES = ("w_ada", "b_ada", "w_in", "rg_conv_w", "rg_conv_b", "rg_w_a", "rg_b_a", "rg_w_x", "rg_b_x",
          "rg_lambda", "dn_conv_w", "dn_a_log", "dn_dt_bias", "dn_norm_w", "w_proj_a", "w_proj_b", "w_out",
          "ln1_g", "ln1_b", "ffn_w_gate", "ffn_w_up", "ffn_conv_w", "ffn_conv_b", "ffn_w_down", "ln2_g", "ln2_b")
_FLAT_W = 1024


def _pack(arrs, width, row_mult):
    pieces = []
    for a in arrs:
        f = a.reshape(-1)
        pieces.append(jnp.pad(f, (0, (-f.shape[0]) % width)))
    flat = jnp.concatenate(pieces)
    rows = flat.shape[0] // width
    return jnp.pad(flat, (0, ((-rows) % row_mult) * width)).reshape(-1, width)


def _unpack(flat, shapes, width):
    out, row = [], 0
    for shp in shapes:
        n = 1
        for d_ in shp:
            n *= d_
        rows = -(-n // width)
        out.append(flat[row:row + rows].reshape(-1)[:n].reshape(shp))
        row += rows
    return out


def _stack_shards(name, full):
    if name in _COL_SHARDED or name in _CONV:
        r, ncol = full.shape
        return full.reshape(r, 4, ncol // 4).transpose(1, 0, 2)
    return full.reshape((4, full.shape[0] // 4) + full.shape[1:])


def _unstack_shards(name, st):
    if name in _COL_SHARDED or name in _CONV:
        return st.transpose(1, 0, 2).reshape(st.shape[1], 4 * st.shape[2])
    return st.reshape((4 * st.shape[1],) + st.shape[2:])


def kernel(x, c, w_ada, b_ada, w_in, rg_conv_w, rg_conv_b, rg_w_a, rg_b_a, rg_w_x, rg_b_x, rg_lambda, dn_conv_w, dn_a_log, dn_dt_bias, dn_norm_w, w_proj_a, w_proj_b, w_out, ln1_g, ln1_b, ffn_w_gate, ffn_w_up, ffn_conv_w, ffn_conv_b, ffn_w_down, ln2_g, ln2_b, loss_target, m_w_ada, m_b_ada, m_w_in, m_rg_conv_w, m_rg_conv_b, m_rg_w_a, m_rg_b_a, m_rg_w_x, m_rg_b_x, m_rg_lambda, m_dn_conv_w, m_dn_a_log, m_dn_dt_bias, m_dn_norm_w, m_w_proj_a, m_w_proj_b, m_w_out, m_ln1_g, m_ln1_b, m_ffn_w_gate, m_ffn_w_up, m_ffn_conv_w, m_ffn_conv_b, m_ffn_w_down, m_ln2_g, m_ln2_b, v_w_ada, v_b_ada, v_w_in, v_rg_conv_w, v_rg_conv_b, v_rg_w_a, v_rg_b_a, v_rg_w_x, v_rg_b_x, v_rg_lambda, v_dn_conv_w, v_dn_a_log, v_dn_dt_bias, v_dn_norm_w, v_w_proj_a, v_w_proj_b, v_w_out, v_ln1_g, v_ln1_b, v_ffn_w_gate, v_ffn_w_up, v_ffn_conv_w, v_ffn_conv_b, v_ffn_w_down, v_ln2_g, v_ln2_b):
    args = locals()
    wts = {n: args[n][0] for n in _NAMES}
    mom = {n: args["m_" + n][0] for n in _NAMES}
    var = {n: args["v_" + n][0] for n in _NAMES}
    xs, tgt = x[0], loss_target[0]
    ix, iy, ic = _coords()
    shard = 2 * ix + iy
    batch = 4 * ix + 2 * iy + ic

    c_all = _allgather8(_pad_rows(c, 8), "gather_c")[:, 0, :]
    sc16 = _pad_rows(_silu_rows(c_all), 16)
    ada_cols = _mm(sc16, w_ada[0], name="ada")[:8]
    ada_g = _allgather8(ada_cols, "gather_ada")
    ada_all = jnp.concatenate([ada_g[0], ada_g[2], ada_g[4], ada_g[6]], axis=1) + b_ada
    adam = _pad_rows(lax.dynamic_index_in_dim(ada_all, batch, 0, keepdims=False).reshape(6, D), 8)

    big_shard = _pack([wts[n].astype(MXU_DT) for n in _BIG], _FLAT_W, 32)
    big_all = _gather_shards(big_shard)
    shapes_big = [wts[n].shape for n in _BIG]
    per_shard = [_unpack(big_all[s], shapes_big, _FLAT_W) for s in range(4)]
    full = {n: _unstack_shards(n, jnp.stack([per_shard[s][i] for s in range(4)]))
            for i, n in enumerate(_BIG)}
    conv_shard = _pack([wts[n] for n in _CONV], 128, 8)
    conv_all = _allgather8(conv_shard, "gather_conv")
    shapes_conv = [wts[n].shape for n in _CONV]
    per_shard = [_unpack(conv_all[2 * s], shapes_conv, 128) for s in range(4)]
    for i, n in enumerate(_CONV):
        full[n] = _unstack_shards(n, jnp.stack([per_shard[s][i] for s in range(4)]))
    for n in _REPL:
        full[n] = wts[n]

    loss_b, gx, g, d_ada = _local_step(xs, tgt, adam, _prepare_weights(full))

    g_big = jnp.stack([_pack([_stack_shards(n, g[n])[s] for n in _BIG], _FLAT_W, 32) for s in range(4)])
    recv = _scatter_partials(g_big.astype(MXU_DT))
    own = lax.dynamic_index_in_dim(g_big, shard, 0, keepdims=False)
    part = _sum_partials(own, recv)
    part_sib = _swap_sibling(part)
    pk = lambda d_: _pack([d_[n] for n in _BIG], _FLAT_W, 32)
    res_big = _adamw([part, part_sib], pk(wts), pk(mom), pk(var), "adamw_big")
    out = {n: [] for n in _NAMES}
    for r_ in res_big:
        for n, a in zip(_BIG, _unpack(r_, shapes_big, _FLAT_W)):
            out[n].append(a)

    small_names = _CONV + _REPL[1:]
    small = _pack([d_ada, jnp.full((1,), loss_b, F32)] + [g[n] for n in small_names], 128, 8)
    small_all = _allgather8(small, "gather_small")
    total = _sum8(small_all)
    shapes_small = [(6 * D,), (1,)] + [full[n].shape for n in small_names]
    tot = _unpack(total, shapes_small, 128)
    gsum = dict(zip(small_names, tot[2:]))
    gsum["b_ada"] = tot[0]
    loss = tot[1][0]
    for n in _CONV:
        gsum[n] = lax.dynamic_index_in_dim(_stack_shards(n, gsum[n]), shard, 0, keepdims=False)
    d_ada_all = small_all[:, :6 * D // 128, :].reshape(8, 6 * D)
    cols = lax.dynamic_slice_in_dim(d_ada_all, shard * (6 * D // 4), 6 * D // 4, axis=1)
    g_wada = _mm(sc16, _pad_rows(cols, 16), name="g_ada", trans_a=True)
    res = _adamw([g_wada], wts["w_ada"], mom["w_ada"], var["w_ada"], "adamw_ada")
    out["w_ada"] = list(res)
    names_s = _CONV + _REPL
    shapes_s = [wts[n].shape for n in names_s]
    pk = lambda d_: _pack([d_[n] for n in names_s], 128, 8)
    res_s = _adamw([pk(gsum)], pk(wts), pk(mom), pk(var), "adamw_small")
    for r_ in res_s:
        for n, a in zip(names_s, _unpack(r_, shapes_s, 128)):
            out[n].append(a)

    outs = [loss, gx[None]]
    for i in range(4):
        outs += [out[n][i][None] for n in _NAMES]
    return tuple(outs)
```

```python
import functools

import jax
import jax.numpy as jnp
from jax import lax
from jax.experimental import pallas as pl
from jax.experimental.pallas import tpu as pltpu

F32 = jnp.float32
BF16 = jnp.bfloat16
MXU_DT = BF16

D = 1024
D_RNN = 1280
RG_BLOCKS = 16
RG_BW = 80
RG_C = 8.0
NQ = 8
NV = 16
HD = 128
CH = 64
D_FF = 2816
LN_EPS = 1e-5
RMS_EPS = 1e-6
L2_EPS = 1e-6
ALPHA = 2.0 ** 0.25
Q_SCALE = HD ** -0.5
N_CAT = 10880
VMEM_LIMIT = 56 * 1024 * 1024
MM_VMEM_BUDGET = 36 * 1024 * 1024
SEQ_GROUP = 8
MESH = pl.DeviceIdType.MESH

ADAM_LR, ADAM_B1, ADAM_B2, ADAM_EPS, ADAM_WD, ADAM_STEP = 1e-3, 0.9, 0.999, 1e-8, 0.01, 10


def _sigmoid(x):
    return 1.0 / (1.0 + jnp.exp(-x))


def _softplus(x):
    return jnp.maximum(x, 0.0) + jnp.log1p(jnp.exp(-jnp.abs(x)))


_GC = 0.7978845608028654


def _gelu(x):
    return 0.5 * x * (1.0 + jnp.tanh(_GC * (x + 0.044715 * x * x * x)))


def _gelu_and_grad(x):
    t = jnp.tanh(_GC * (x + 0.044715 * x * x * x))
    g = 0.5 * x * (1.0 + t)
    dg = 0.5 * (1.0 + t) + 0.5 * x * (1.0 - t * t) * _GC * (1.0 + 3 * 0.044715 * x * x)
    return g, dg


def _neg_expm1(y):
    series = -y * (1.0 + 0.5 * y * (1.0 + y * (1.0 / 3.0)))
    return jnp.where(y > -0.01, series, 1.0 - jnp.exp(y))


def _dot(a, b):
    return jnp.dot(a.astype(MXU_DT), b.astype(MXU_DT), preferred_element_type=F32)


def _dot_nt(a, b):
    return lax.dot_general(a.astype(MXU_DT), b.astype(MXU_DT), (((1,), (1,)), ((), ())),
                           preferred_element_type=F32)


def _dot_tn(a, b):
    return lax.dot_general(a.astype(MXU_DT), b.astype(MXU_DT), (((0,), (0,)), ((), ())),
                           preferred_element_type=F32)


def _split(a):
    hi = a.astype(BF16)
    return hi, (a - hi.astype(F32)).astype(BF16)


def _dot3(a, b, dims=(((1,), (0,)), ((), ()))):
    ah, al = _split(a)
    bh, bl = _split(b)
    d = lambda p, q: lax.dot_general(p, q, dims, preferred_element_type=F32)
    return d(ah, bh) + (d(al, bh) + d(ah, bl))


def _dot3_tn(a, b):
    return _dot3(a, b, (((0,), (0,)), ((), ())))


def _iota(shape, dim):
    return lax.broadcasted_iota(jnp.int32, shape, dim)


def _shift_down(x, before, j):
    if j == 0:
        return x
    xr = pltpu.roll(x, j, 0)
    br = pltpu.roll(before, j, 0)
    top = jnp.where(_iota(br.shape, 0) < j, br, xr[:8])
    return jnp.concatenate([top, xr[8:]], axis=0)


def _shift_up(x, after, j):
    if j == 0:
        return x
    t = x.shape[0]
    xr = pltpu.roll(x, t - j, 0)
    ar = pltpu.roll(after, 8 - j, 0)
    bot = jnp.where(_iota(ar.shape, 0) >= 8 - j, ar, xr[t - 8:])
    return jnp.concatenate([xr[:t - 8], bot], axis=0)


def _conv_causal(x, before, w_ref, k):
    y = w_ref[k - 1:k, :] * x
    for i in range(k - 1):
        y = y + w_ref[i:i + 1, :] * _shift_down(x, before, k - 1 - i)
    return y


def _conv_causal_bwd(dy, after, w_ref, k):
    dx = w_ref[k - 1:k, :] * dy
    for i in range(k - 1):
        dx = dx + w_ref[i:i + 1, :] * _shift_up(dy, after, k - 1 - i)
    return dx


def _scan_fwd(a, u):
    t = a.shape[0]
    rows = _iota(a.shape, 0)
    d = 1
    while d < t:
        m = rows >= d
        u = u + jnp.where(m, a * pltpu.roll(u, d, 0), 0.0)
        a = jnp.where(m, a * pltpu.roll(a, d, 0), a)
        d *= 2
    return a, u


def _scan_rev(a, u):
    t = a.shape[0]
    rows = _iota(a.shape, 0)
    d = 1
    while d < t:
        m = rows < t - d
        u = u + jnp.where(m, a * pltpu.roll(u, t - d, 0), 0.0)
        a = jnp.where(m, a * pltpu.roll(a, t - d, 0), a)
        d *= 2
    return a, u


def _chunk_cumsum(g, axis, rev=False):
    n = g.shape[axis]
    pos = _iota(g.shape, axis) & (CH - 1)
    d = 1
    while d < CH:
        if rev:
            g = g + jnp.where(pos < CH - d, pltpu.roll(g, n - d, axis), 0.0)
        else:
            g = g + jnp.where(pos >= d, pltpu.roll(g, d, axis), 0.0)
        d *= 2
    return g


def _ln_stats(r):
    mu = jnp.mean(r, axis=-1, keepdims=True)
    xc = r - mu
    var = jnp.mean(xc * xc, axis=-1, keepdims=True)
    rstd = lax.rsqrt(var + LN_EPS)
    return xc * rstd, rstd


def _ln_bwd(dy, xhat, rstd, g):
    dxh = dy * g
    return rstd * (dxh - jnp.mean(dxh, axis=-1, keepdims=True)
                   - xhat * jnp.mean(dxh * xhat, axis=-1, keepdims=True))


def _rsum(x):
    return jnp.sum(x, axis=0, keepdims=True)


def _params(sem):
    return pltpu.CompilerParams(dimension_semantics=sem, vmem_limit_bytes=VMEM_LIMIT)


def _pick(n, cands):
    for c in cands:
        if n % c == 0:
            return c
    return n


def _rows(tm, w, col=0, nt=None):
    if nt is None:
        return pl.BlockSpec((tm, w), lambda i: (i, col))
    return pl.BlockSpec((tm, w), lambda i: (nt - 1 - i, col))


def _before(tm, w, col=0, nt=None):
    r = tm // 8
    if nt is None:
        return pl.BlockSpec((8, w), lambda i: (jnp.maximum(i * r - 1, 0), col))
    return pl.BlockSpec((8, w), lambda i: (jnp.maximum((nt - 1 - i) * r - 1, 0), col))


def _whole(shape):
    return pl.BlockSpec(shape, lambda *_: (0,) * len(shape))


def _mm(a, b, *, name, trans_a=False, out_dtype=F32):
    if trans_a:
        return _mm(a.T, b, name=name, out_dtype=out_dtype)
    m, kk = a.shape
    _, n = b.shape
    tm = _pick(m, (512, 256, 128))
    tk = kk if kk <= 5632 else _pick(kk, (2176, 2048, 1024))
    nk = kk // tk

    def vmem_bytes(tn):
        blocks = tm * tk * a.dtype.itemsize + tk * tn * b.dtype.itemsize + tm * tn * jnp.dtype(out_dtype).itemsize
        return 2 * blocks + (tm * tn * 4 if nk > 1 else 0)

    cands = [t for t in (1408, 1280, 1024, 640, 512, 256, 128) if n % t == 0] or [n]
    tn = next((t for t in cands if vmem_bytes(t) <= MM_VMEM_BUDGET), cands[-1])

    if nk == 1:
        def body(a_ref, b_ref, o_ref):
            o_ref[...] = _dot(a_ref[...], b_ref[...]).astype(out_dtype)
        scratch = []
    else:
        def body(a_ref, b_ref, o_ref, acc):
            k = pl.program_id(2)

            @pl.when(k == 0)
            def _():
                acc[...] = jnp.zeros_like(acc)
            acc[...] += _dot(a_ref[...], b_ref[...])

            @pl.when(k == nk - 1)
            def _():
                o_ref[...] = acc[...].astype(out_dtype)
        scratch = [pltpu.VMEM((tm, tn), F32)]

    return pl.pallas_call(
        body, name=name, grid=(m // tm, n // tn, nk),
        in_specs=[pl.BlockSpec((tm, tk), lambda i, j, k: (i, k)),
                  pl.BlockSpec((tk, tn), lambda i, j, k: (k, j))],
        out_specs=pl.BlockSpec((tm, tn), lambda i, j, k: (i, j)),
        out_shape=jax.ShapeDtypeStruct((m, n), out_dtype),
        scratch_shapes=scratch,
        compiler_params=_params(("parallel", "parallel", "arbitrary")),
    )(a, b)


def _modulate(x, adam):
    s = x.shape[0]
    tm = _pick(s, (512, 256, 128))

    def body(x_ref, ada_ref, o_ref):
        o_ref[...] = (x_ref[...] * (1.0 + ada_ref[1:2, :]) + ada_ref[0:1, :]).astype(MXU_DT)

    return pl.pallas_call(
        body, name="modulate1", grid=(s // tm,),
        in_specs=[_rows(tm, D), _whole((8, D))], out_specs=_rows(tm, D),
        out_shape=jax.ShapeDtypeStruct((s, D), MXU_DT), compiler_params=_params(("parallel",)),
    )(x, adam)


def _rg_gates(xc, wa_ref, wx_ref, vec_ref):
    xb = xc.astype(MXU_DT)
    r = _sigmoid(jnp.dot(xb, wa_ref[...], preferred_element_type=F32) + vec_ref[1:2, :])
    ig = _sigmoid(jnp.dot(xb, wx_ref[...], preferred_element_type=F32) + vec_ref[2:3, :])
    sp = _softplus(-vec_ref[3:4, :])
    la = -RG_C * r * sp
    a = jnp.exp(la)
    mult = jnp.sqrt(_neg_expm1(2.0 * la))
    return r, ig, a, mult, sp


def _rg_fwd(px, cw, vec, wa, wx):
    s = px.shape[0]
    tm = _pick(s, (256, 128))
    w = D_RNN

    def body(xr_ref, gr_ref, cw_ref, vec_ref, wa_ref, wx_ref, h_ref, rec_ref, prev_x, prev_h):
        @pl.when(pl.program_id(0) == 0)
        def _():
            prev_x[...] = jnp.zeros_like(prev_x)
            prev_h[...] = jnp.zeros_like(prev_h)
        x = xr_ref[...]
        xc = _conv_causal(x, prev_x[...], cw_ref, 4) + vec_ref[0:1, :]
        prev_x[...] = x[tm - 8:, :]
        _, ig, a, mult, _ = _rg_gates(xc, wa_ref, wx_ref, vec_ref)
        acum, h = _scan_fwd(a, mult * ig * xc)
        h = h + acum * prev_h[7:8, :]
        prev_h[...] = h[tm - 8:, :]
        h_ref[...] = h
        rec_ref[...] = (h * _gelu(gr_ref[...])).astype(MXU_DT)

    return pl.pallas_call(
        body, name="rg_fwd", grid=(s // tm,),
        in_specs=[_rows(tm, w, 0), _rows(tm, w, 1), _whole((8, w)), _whole((8, w)),
                  _whole((w, w)), _whole((w, w))],
        out_specs=[_rows(tm, w), _rows(tm, w)],
        out_shape=[jax.ShapeDtypeStruct((s, w), F32), jax.ShapeDtypeStruct((s, w), MXU_DT)],
        scratch_shapes=[pltpu.VMEM((8, w), F32), pltpu.VMEM((8, w), F32)],
        compiler_params=_params(("arbitrary",)),
    )(px, px, cw, vec, wa, wx)


def _dn_scalars(ab, arow, drow):
    lane = _iota(ab.shape, 1)
    g = jnp.where(lane < NV, -jnp.exp(arow) * _softplus(ab + drow), 0.0)
    beta = _sigmoid(ab)
    return lane, g, beta


def _l2n_heads(c, out_ref, off, scale):
    for hh in range(NQ):
        x = c[:, off + hh * HD: off + (hh + 1) * HD]
        r = lax.rsqrt(jnp.sum(x * x, axis=-1, keepdims=True) + L2_EPS)
        out_ref[:, hh * HD:(hh + 1) * HD] = x * (r * scale)


def _dn_prep(pqkv, pab, cw, arow, drow):
    s = pqkv.shape[0]
    tm = 128
    wq = NQ * HD

    def body(x_ref, ab_ref, cw_ref, a_ref, d_ref, q_ref, k_ref, v_ref, rt_ref, prev_x):
        @pl.when(pl.program_id(0) == 0)
        def _():
            prev_x[...] = jnp.zeros_like(prev_x)
        x = x_ref[...]
        cp = _conv_causal(x, prev_x[...], cw_ref, 4)
        prev_x[...] = x[tm - 8:, :]
        c = cp * _sigmoid(cp)
        _l2n_heads(c, q_ref, 0, Q_SCALE)
        _l2n_heads(c, k_ref, wq, 1.0)
        v_ref[...] = c[:, 2 * wq:]
        lane, g, beta = _dn_scalars(ab_ref[...], a_ref[...], d_ref[...])
        gc = _chunk_cumsum(g, 0)
        gl = gc + _chunk_cumsum(g, 0, rev=True) - g
        pack = jnp.where(lane < NV, gc, jnp.where(lane < 2 * NV, beta,
                         jnp.where(lane < 3 * NV, pltpu.roll(gl, 2 * NV, 1), 0.0)))
        rt_ref[...] = pack.T[0:3 * NV, :]

    return pl.pallas_call(
        body, name="dn_prep", grid=(s // tm,),
        in_specs=[_rows(tm, 4 * wq), _rows(tm, 128), _whole((8, 4 * wq)), _whole((1, 128)), _whole((1, 128))],
        out_specs=[_rows(tm, wq), _rows(tm, wq), _rows(tm, 2 * wq),
                   pl.BlockSpec((3 * NV, tm), lambda i: (0, i))],
        out_shape=[jax.ShapeDtypeStruct((s, wq), F32), jax.ShapeDtypeStruct((s, wq), F32),
                   jax.ShapeDtypeStruct((s, 2 * wq), F32), jax.ShapeDtypeStruct((3 * NV, s), F32)],
        scratch_shapes=[pltpu.VMEM((8, 4 * wq), F32)],
        compiler_params=_params(("arbitrary",)),
    )(pqkv, pab, cw, arow, drow)


def _pair_masks():
    i = _iota((2 * CH, 2 * CH), 0)
    j = _iota((2 * CH, 2 * CH), 1)
    same = (i >> 6) == (j >> 6)
    return same & (i >= j), same & (i > j)


def _head_cols(rt_ref, h):
    shp = (2 * CH, 2 * CH)
    g_r = jnp.broadcast_to(rt_ref[pl.ds(h, 1), :], shp)
    b_r = jnp.broadcast_to(rt_ref[pl.ds(NV + h, 1), :], shp)
    l_r = jnp.broadcast_to(rt_ref[pl.ds(2 * NV + h, 1), :], shp)
    return g_r, g_r.T, b_r.T, l_r, l_r.T


def _inv_unit_lower(a):
    return _inv_unit_lower_many([a])[0]


def _inv_unit_lower_many(a_list):
    n = a_list[0].shape[0]
    eye = (_iota((n, n), 0) == _iota((n, n), 1)).astype(F32)
    bs = [-a for a in a_list]
    xs = [eye + b for b in bs]
    for _ in range(5):
        bs = [_dot(b, b) for b in bs]
        xs = [x + _dot(x, b) for x, b in zip(xs, bs)]
    rs = [(eye - x) - _dot3(a, x) for a, x in zip(a_list, xs)]
    return [x + _dot(x, r) for x, r in zip(xs, rs)]


def _gam_rows(l_r):
    lrow = l_r[0:1, :]
    lane = _iota(lrow.shape, 1)
    other = pltpu.roll(lrow, CH, 1)
    return jnp.exp(jnp.where(lane < CH, lrow, other)), jnp.exp(jnp.where(lane >= CH, lrow, other))


def _dn_intra(q, k, v, rt):
    s = q.shape[0]
    nb = s // (2 * CH)
    qps = 2
    blk = pl.BlockSpec((2 * CH, qps * HD), lambda i, h: (i, h))
    blk2 = pl.BlockSpec((2 * CH, 2 * qps * HD), lambda i, h: (i, h))

    def body(q_ref, k_ref, v_ref, rt_ref, u_ref, w_ref, qd_ref, kd_ref, p_ref, ti_ref, gam_ref):
        hstep = pl.program_id(1)
        mc, ms = _pair_masks()
        gam_ref[...] = jnp.zeros_like(gam_ref)
        heads = []
        for qh in range(qps):
            qq = q_ref[:, qh * HD:(qh + 1) * HD]
            kk_ = k_ref[:, qh * HD:(qh + 1) * HD]
            kk = _dot_nt(kk_, kk_)
            qk = _dot_nt(qq, kk_)
            for j in range(2):
                idx = 2 * qh + j
                cs = slice(idx * HD, (idx + 1) * HD)
                g_r, g_c, b_c, l_r, l_c = _head_cols(rt_ref, 2 * qps * hstep + idx)
                dec = jnp.where(mc, jnp.exp(jnp.where(mc, g_c - g_r, 0.0)), 0.0)
                eg = jnp.exp(g_c)
                p_ref[:, cs] = jnp.where(mc, qk * dec, 0.0)
                qd_ref[:, cs] = eg * qq
                kd_ref[:, cs] = jnp.exp(l_c - g_c) * kk_
                ga, gb = _gam_rows(l_r)
                gam_ref[0, qh, 2 * j:2 * j + 1, :] = ga
                gam_ref[0, qh, 2 * j + 1:2 * j + 2, :] = gb
                rhs = jnp.concatenate([b_c * v_ref[:, cs], b_c * eg * kk_], axis=1)
                heads.append((cs, jnp.where(ms, b_c * kk * dec, 0.0), rhs))
        tinvs = _inv_unit_lower_many([a for _, a, _ in heads])
        uws = [_dot3(t, rhs) for t, (_, _, rhs) in zip(tinvs, heads)]
        for t, uw, (cs, _, _) in zip(tinvs, uws, heads):
            ti_ref[:, cs] = t
            u_ref[:, cs] = uw[:, :HD]
            w_ref[:, cs] = uw[:, HD:]

    big = jax.ShapeDtypeStruct((s, NV * HD), F32)
    return pl.pallas_call(
        body, name="dn_intra", grid=(nb, NQ // qps),
        in_specs=[blk, blk, blk2, pl.BlockSpec((3 * NV, 2 * CH), lambda i, h: (0, i))],
        out_specs=[blk2] * 6 + [pl.BlockSpec((1, qps, 8, 128), lambda i, h: (i, h, 0, 0))],
        out_shape=[big] * 6 + [jax.ShapeDtypeStruct((nb, NQ, 8, 128), F32)],
        compiler_params=_params(("parallel", "parallel")),
    )(q, k, v, rt)


def _dn_seq(u, w, qd, kd, p, gam):
    s = u.shape[0]
    nb = s // (2 * CH)
    wide = pl.BlockSpec((2 * CH, NV * HD), lambda i: (i, 0))

    def body(u_ref, w_ref, qd_ref, kd_ref, p_ref, gam_ref, o_ref, vn_ref, ss_ref, st):
        @pl.when(pl.program_id(0) == 0)
        def _():
            st[...] = jnp.zeros_like(st)
        ra, rb = slice(0, CH), slice(CH, 2 * CH)
        for g0 in range(0, NV, SEQ_GROUP):
            hs = list(range(g0, g0 + SEQ_GROUP))
            cs = [slice(h * HD, (h + 1) * HD) for h in hs]
            ga = [gam_ref[0, h // 2, 2 * (h % 2):2 * (h % 2) + 1, :] for h in hs]
            gb = [gam_ref[0, h // 2, 2 * (h % 2) + 1:2 * (h % 2) + 2, :] for h in hs]
            s0 = [st[h] for h in hs]
            vna = [u_ref[ra, c] - _dot(w_ref[ra, c], s) for c, s in zip(cs, s0)]
            s1 = [g * s + _dot_tn(kd_ref[ra, c], v) for g, s, c, v in zip(ga, s0, cs, vna)]
            vnb = [u_ref[rb, c] - _dot(w_ref[rb, c], s) for c, s in zip(cs, s1)]
            s2 = [g * s + _dot_tn(kd_ref[rb, c], v) for g, s, c, v in zip(gb, s1, cs, vnb)]
            for h, s in zip(hs, s2):
                st[h] = s
            oa = [_dot(qd_ref[ra, c], s) for c, s in zip(cs, s0)]
            ob = [_dot(qd_ref[rb, c], s) for c, s in zip(cs, s1)]
            for i_, h in enumerate(hs):
                vn = jnp.concatenate([vna[i_], vnb[i_]], axis=0)
                o_ref[:, cs[i_]] = jnp.concatenate([oa[i_], ob[i_]], axis=0) + _dot(p_ref[:, cs[i_]], vn)
                vn_ref[:, cs[i_]] = vn
                ss_ref[h, 0:HD, :] = s0[i_]
                ss_ref[h, HD:2 * HD, :] = s1[i_]

    big = jax.ShapeDtypeStruct((s, NV * HD), F32)
    return pl.pallas_call(
        body, name="dn_seq", grid=(nb,),
        in_specs=[wide] * 5 + [pl.BlockSpec((1, NQ, 8, 128), lambda i: (i, 0, 0, 0))],
        out_specs=[wide, wide, pl.BlockSpec((NV, 2 * HD, HD), lambda i: (0, i, 0))],
        out_shape=[big, big, jax.ShapeDtypeStruct((NV, 2 * s, HD), F32)],
        scratch_shapes=[pltpu.VMEM((NV, HD, HD), F32)],
        compiler_params=_params(("arbitrary",)),
    )(u, w, qd, kd, p, gam)


def _dn_post(o, pz, nw):
    s = o.shape[0]
    tm = _pick(s, (256, 128))

    def body(o_ref, z_ref, nw_ref, y_ref):
        for h in range(NV):
            cs = slice(h * HD, (h + 1) * HD)
            oh = o_ref[:, cs]
            z = z_ref[:, cs]
            rs = lax.rsqrt(jnp.mean(oh * oh, axis=-1, keepdims=True) + RMS_EPS)
            y_ref[:, cs] = (oh * rs * nw_ref[...] * (z * _sigmoid(z))).astype(MXU_DT)

    return pl.pallas_call(
        body, name="dn_post", grid=(s // tm,),
        in_specs=[_rows(tm, NV * HD), _rows(tm, NV * HD), _whole((1, HD))],
        out_specs=_rows(tm, NV * HD),
        out_shape=jax.ShapeDtypeStruct((s, NV * HD), MXU_DT), compiler_params=_params(("parallel",)),
    )(o, pz, nw)


def _merge(pg, ya, yb):
    s = ya.shape[0]
    tm = _pick(s, (512, 256, 128))

    def body(ga_ref, gb_ref, ya_ref, yb_ref, o_ref):
        o_ref[...] = (_sigmoid(ga_ref[...]) * ya_ref[...] + _sigmoid(gb_ref[...]) * yb_ref[...]).astype(MXU_DT)

    return pl.pallas_call(
        body, name="merge", grid=(s // tm,),
        in_specs=[_rows(tm, D, 0), _rows(tm, D, 1), _rows(tm, D), _rows(tm, D)],
        out_specs=_rows(tm, D),
        out_shape=jax.ShapeDtypeStruct((s, D), MXU_DT), compiler_params=_params(("parallel",)),
    )(pg, pg, ya, yb)


def _ln1(x, mix, adam, lng, lnb):
    s = x.shape[0]
    tm = _pick(s, (512, 256, 128))

    def body(x_ref, m_ref, ada_ref, g_ref, b_ref, r_ref, x2_ref, h2_ref):
        r = ALPHA * x_ref[...] + (1.0 + ada_ref[2:3, :]) * m_ref[...]
        xhat, _ = _ln_stats(r)
        x2 = xhat * g_ref[...] + b_ref[...]
        r_ref[...] = r
        x2_ref[...] = x2
        h2_ref[...] = (x2 * (1.0 + ada_ref[4:5, :]) + ada_ref[3:4, :]).astype(MXU_DT)

    return pl.pallas_call(
        body, name="ln1", grid=(s // tm,),
        in_specs=[_rows(tm, D), _rows(tm, D), _whole((8, D)), _whole((1, D)), _whole((1, D))],
        out_specs=[_rows(tm, D)] * 3,
        out_shape=[jax.ShapeDtypeStruct((s, D), F32), jax.ShapeDtypeStruct((s, D), F32),
                   jax.ShapeDtypeStruct((s, D), MXU_DT)],
        compiler_params=_params(("parallel",)),
    )(x, mix, adam, lng, lnb)


def _ffn_act(gu, cw, cb):
    s = gu.shape[0]
    tm = _pick(s, (256, 128))
    w = D_FF

    def body(g_ref, u_ref, cw_ref, cb_ref, o_ref, prev):
        @pl.when(pl.program_id(0) == 0)
        def _():
            prev[...] = jnp.zeros_like(prev)
        g = g_ref[...]
        gc = _conv_causal(g, prev[...], cw_ref, 3) + cb_ref[...]
        prev[...] = g[tm - 8:, :]
        o_ref[...] = (_gelu(gc) * u_ref[...]).astype(MXU_DT)

    return pl.pallas_call(
        body, name="ffn_act", grid=(s // tm,),
        in_specs=[_rows(tm, w, 0), _rows(tm, w, 1), _whole((8, w)), _whole((1, w))],
        out_specs=_rows(tm, w),
        out_shape=jax.ShapeDtypeStruct((s, w), MXU_DT),
        scratch_shapes=[pltpu.VMEM((8, w), F32)],
        compiler_params=_params(("arbitrary",)),
    )(gu, gu, cw, cb)


def _ln2_loss(x2, ff, tgt, adam, lng, lnb):
    s = x2.shape[0]
    tm = _pick(s, (512, 256, 128))

    def body(x_ref, f_ref, t_ref, ada_ref, g_ref, b_ref, dff_ref, dx_ref, red_ref):
        @pl.when(pl.program_id(0) == 0)
        def _():
            red_ref[...] = jnp.zeros_like(red_ref)
        ff_ = f_ref[...]
        r = ALPHA * x_ref[...] + (1.0 + ada_ref[5:6, :]) * ff_
        xhat, rstd = _ln_stats(r)
        err = xhat * g_ref[...] + b_ref[...] - t_ref[...]
        dy = err * (1.0 / D)
        dr = _ln_bwd(dy, xhat, rstd, g_ref[...])
        dff_ref[...] = ((1.0 + ada_ref[5:6, :]) * dr).astype(MXU_DT)
        dx_ref[...] = ALPHA * dr
        red_ref[0:1, :] += _rsum(dy * xhat)
        red_ref[1:2, :] += _rsum(dy)
        red_ref[2:3, :] += _rsum(dr * ff_)
        red_ref[3:4, :] += jnp.sum(_rsum(err * err), axis=1, keepdims=True) * (0.5 / D)

    return pl.pallas_call(
        body, name="ln2_loss", grid=(s // tm,),
        in_specs=[_rows(tm, D)] * 3 + [_whole((8, D)), _whole((1, D)), _whole((1, D))],
        out_specs=[_rows(tm, D), _rows(tm, D), _whole((8, D))],
        out_shape=[jax.ShapeDtypeStruct((s, D), MXU_DT), jax.ShapeDtypeStruct((s, D), F32),
                   jax.ShapeDtypeStruct((8, D), F32)],
        compiler_params=_params(("arbitrary",)),
    )(x2, ff, tgt, adam, lng, lnb)


def _ffn_bwd(dact, gu, cw, cb):
    s = dact.shape[0]
    tm = _pick(s, (256, 128))
    nt = s // tm
    w = D_FF

    def body(da_ref, g_ref, gb_ref, u_ref, cw_ref, cb_ref, o_ref, gcw_ref, gcb_ref, nxt):
        i = pl.program_id(0)

        @pl.when(i == 0)
        def _():
            nxt[...] = jnp.zeros_like(nxt)
            gcw_ref[...] = jnp.zeros_like(gcw_ref)
            gcb_ref[...] = jnp.zeros_like(gcb_ref)
        g = g_ref[...]
        before = jnp.where(i < nt - 1, gb_ref[...], 0.0)
        gc = _conv_causal(g, before, cw_ref, 3) + cb_ref[...]
        gel, dgel = _gelu_and_grad(gc)
        da = da_ref[...]
        dgc = da * u_ref[...] * dgel
        o_ref[:, w:] = (da * gel).astype(MXU_DT)
        o_ref[:, :w] = _conv_causal_bwd(dgc, nxt[...], cw_ref, 3).astype(MXU_DT)
        nxt[...] = dgc[:8, :]
        for k in range(3):
            gcw_ref[k:k + 1, :] += _rsum(dgc * _shift_down(g, before, 2 - k))
        gcb_ref[...] += _rsum(dgc)

    return pl.pallas_call(
        body, name="ffn_bwd", grid=(nt,),
        in_specs=[_rows(tm, w, 0, nt), _rows(tm, w, 0, nt), _before(tm, w, 0, nt), _rows(tm, w, 1, nt),
                  _whole((8, w)), _whole((1, w))],
        out_specs=[_rows(tm, 2 * w, 0, nt), _whole((8, w)), _whole((1, w))],
        out_shape=[jax.ShapeDtypeStruct((s, 2 * w), MXU_DT), jax.ShapeDtypeStruct((8, w), F32),
                   jax.ShapeDtypeStruct((1, w), F32)],
        scratch_shapes=[pltpu.VMEM((8, w), F32)],
        compiler_params=_params(("arbitrary",)),
    )(dact, gu, gu, gu, cw, cb)


def _ln1_bwd(dh2, dx2a, x2, r1, mix, adam, lng):
    s = dh2.shape[0]
    tm = _pick(s, (512, 256, 128))

    def body(dh_ref, dxa_ref, x2_ref, r_ref, m_ref, ada_ref, g_ref, dm_ref, dx_ref, red_ref):
        @pl.when(pl.program_id(0) == 0)
        def _():
            red_ref[...] = jnp.zeros_like(red_ref)
        dh = dh_ref[...]
        dx2 = dxa_ref[...] + dh * (1.0 + ada_ref[4:5, :])
        xhat, rstd = _ln_stats(r_ref[...])
        dr = _ln_bwd(dx2, xhat, rstd, g_ref[...])
        dm_ref[...] = ((1.0 + ada_ref[2:3, :]) * dr).astype(MXU_DT)
        dx_ref[...] = ALPHA * dr
        red_ref[0:1, :] += _rsum(dh * x2_ref[...])
        red_ref[1:2, :] += _rsum(dh)
        red_ref[2:3, :] += _rsum(dx2 * xhat)
        red_ref[3:4, :] += _rsum(dx2)
        red_ref[4:5, :] += _rsum(dr * m_ref[...])

    return pl.pallas_call(
        body, name="ln1_bwd", grid=(s // tm,),
        in_specs=[_rows(tm, D)] * 5 + [_whole((8, D)), _whole((1, D))],
        out_specs=[_rows(tm, D), _rows(tm, D), _whole((8, D))],
        out_shape=[jax.ShapeDtypeStruct((s, D), MXU_DT), jax.ShapeDtypeStruct((s, D), F32),
                   jax.ShapeDtypeStruct((8, D), F32)],
        compiler_params=_params(("arbitrary",)),
    )(dh2, dx2a, x2, r1, mix, adam, lng)


def _merge_bwd(dmg, pg, ya, yb):
    s = dmg.shape[0]
    tm = _pick(s, (512, 256, 128))

    def body(d_ref, ga_ref, gb_ref, ya_ref, yb_ref, dya_ref, dyb_ref, dpg_ref):
        d = d_ref[...]
        sa = _sigmoid(ga_ref[...])
        sb = _sigmoid(gb_ref[...])
        dya_ref[...] = (d * sa).astype(MXU_DT)
        dyb_ref[...] = (d * sb).astype(MXU_DT)
        dpg_ref[:, :D] = (d * ya_ref[...] * sa * (1.0 - sa)).astype(MXU_DT)
        dpg_ref[:, D:] = (d * yb_ref[...] * sb * (1.0 - sb)).astype(MXU_DT)

    return pl.pallas_call(
        body, name="merge_bwd", grid=(s // tm,),
        in_specs=[_rows(tm, D), _rows(tm, D, 0), _rows(tm, D, 1), _rows(tm, D), _rows(tm, D)],
        out_specs=[_rows(tm, D), _rows(tm, D), _rows(tm, 2 * D)],
        out_shape=[jax.ShapeDtypeStruct((s, D), MXU_DT), jax.ShapeDtypeStruct((s, D), MXU_DT),
                   jax.ShapeDtypeStruct((s, 2 * D), MXU_DT)],
        compiler_params=_params(("parallel",)),
    )(dmg, pg, pg, ya, yb)


def _dn_post_bwd(ddn, o, pz, nw):
    s = o.shape[0]
    tm = _pick(s, (256, 128))

    def body(d_ref, o_ref, z_ref, nw_ref, do_ref, dz_ref, gnw_ref):
        @pl.when(pl.program_id(0) == 0)
        def _():
            gnw_ref[...] = jnp.zeros_like(gnw_ref)
        acc = jnp.zeros((1, HD), F32)
        for h in range(NV):
            cs = slice(h * HD, (h + 1) * HD)
            oh = o_ref[:, cs]
            z = z_ref[:, cs]
            d = d_ref[:, cs]
            sg = _sigmoid(z)
            rs = lax.rsqrt(jnp.mean(oh * oh, axis=-1, keepdims=True) + RMS_EPS)
            n = oh * rs
            dz_ref[:, cs] = (d * n * nw_ref[...] * sg * (1.0 + z * (1.0 - sg))).astype(MXU_DT)
            dn_ = d * (z * sg)
            acc = acc + _rsum(dn_ * n)
            dnn = dn_ * nw_ref[...]
            do_ref[:, cs] = rs * (dnn - n * jnp.mean(dnn * n, axis=-1, keepdims=True))
        gnw_ref[...] += acc

    return pl.pallas_call(
        body, name="dn_post_bwd", grid=(s // tm,),
        in_specs=[_rows(tm, NV * HD)] * 3 + [_whole((1, HD))],
        out_specs=[_rows(tm, NV * HD), _rows(tm, NV * HD), _whole((1, HD))],
        out_shape=[jax.ShapeDtypeStruct((s, NV * HD), F32), jax.ShapeDtypeStruct((s, NV * HD), MXU_DT),
                   jax.ShapeDtypeStruct((1, HD), F32)],
        compiler_params=_params(("arbitrary",)),
    )(ddn, o, pz, nw)


def _dn_seq_bwd(do, qd, kd, p, w, vn, ssave, gam):
    s = do.shape[0]
    nb = s // (2 * CH)
    wide = pl.BlockSpec((2 * CH, NV * HD), lambda i: (nb - 1 - i, 0))
    gspec = pl.BlockSpec((1, NQ, 8, 128), lambda i: (nb - 1 - i, 0, 0, 0))

    def body(do_ref, qd_ref, kd_ref, p_ref, w_ref, vn_ref, ss_ref, gam_ref, dvn_ref, dkd_ref, dgam_ref, dst):
        @pl.when(pl.program_id(0) == 0)
        def _():
            dst[...] = jnp.zeros_like(dst)
        dgam_ref[...] = jnp.zeros_like(dgam_ref)
        ra, rb = slice(0, CH), slice(CH, 2 * CH)
        tot = lambda t: jnp.sum(jnp.sum(t, axis=1, keepdims=True), axis=0, keepdims=True)
        for g0 in range(0, NV, SEQ_GROUP):
            hs = list(range(g0, g0 + SEQ_GROUP))
            cs = [slice(h * HD, (h + 1) * HD) for h in hs]
            ga = [gam_ref[0, h // 2, 2 * (h % 2):2 * (h % 2) + 1, :] for h in hs]
            gb = [gam_ref[0, h // 2, 2 * (h % 2) + 1:2 * (h % 2) + 2, :] for h in hs]
            ds2 = [dst[h] for h in hs]
            pdo = [_dot_tn(p_ref[:, c], do_ref[:, c]) for c in cs]
            qdo_b = [_dot_tn(qd_ref[rb, c], do_ref[rb, c]) for c in cs]
            qdo_a = [_dot_tn(qd_ref[ra, c], do_ref[ra, c]) for c in cs]
            dvb = [p_[rb] + _dot(kd_ref[rb, c], d_) for p_, c, d_ in zip(pdo, cs, ds2)]
            ds1 = [g * d_ + q_ - _dot_tn(w_ref[rb, c], v_)
                   for g, d_, q_, c, v_ in zip(gb, ds2, qdo_b, cs, dvb)]
            dva = [p_[ra] + _dot(kd_ref[ra, c], d_) for p_, c, d_ in zip(pdo, cs, ds1)]
            ds0 = [g * d_ + q_ - _dot_tn(w_ref[ra, c], v_)
                   for g, d_, q_, c, v_ in zip(ga, ds1, qdo_a, cs, dva)]
            for h, d_ in zip(hs, ds0):
                dst[h] = d_
            for i_, h in enumerate(hs):
                c = cs[i_]
                row = 2 * (h % 2)
                dkd_ref[rb, c] = _dot_nt(vn_ref[rb, c], ds2[i_])
                dkd_ref[ra, c] = _dot_nt(vn_ref[ra, c], ds1[i_])
                dvn_ref[ra, c] = dva[i_]
                dvn_ref[rb, c] = dvb[i_]
                dgam_ref[0, h // 2, row:row + 1, :] = jnp.broadcast_to(tot(ds1[i_] * ss_ref[h, 0:HD, :]), (1, 128))
                dgam_ref[0, h // 2, row + 1:row + 2, :] = jnp.broadcast_to(
                    tot(ds2[i_] * ss_ref[h, HD:2 * HD, :]), (1, 128))

    big = jax.ShapeDtypeStruct((s, NV * HD), F32)
    return pl.pallas_call(
        body, name="dn_seq_bwd", grid=(nb,),
        in_specs=[wide] * 6 + [pl.BlockSpec((NV, 2 * HD, HD), lambda i: (0, nb - 1 - i, 0)), gspec],
        out_specs=[wide, wide, gspec],
        out_shape=[big, big, jax.ShapeDtypeStruct((nb, NQ, 8, 128), F32)],
        scratch_shapes=[pltpu.VMEM((NV, HD, HD), F32)],
        compiler_params=_params(("arbitrary",)),
    )(do, qd, kd, p, w, vn, ssave, gam)


def _dn_intra_bwd(q, k, v, rt, do, dvn, dkd, gam, dgam, ssave, tinv, u, w, vn):
    s = q.shape[0]
    nb = s // (2 * CH)
    blk = pl.BlockSpec((2 * CH, HD), lambda i, h: (i, h))
    blk2 = pl.BlockSpec((2 * CH, 2 * HD), lambda i, h: (i, h))
    gspec = pl.BlockSpec((1, 1, 8, 128), lambda i, h: (i, h, 0, 0))
    rspec = pl.BlockSpec((3 * NV, 2 * CH), lambda i, h: (0, i))

    def body(q_ref, k_ref, v_ref, rt_ref, do_ref, dvn_ref, dkd_ref, gam_ref, dgam_ref, ss_ref,
             ti_ref, u_ref, w_ref, vn_ref, dq_ref, dk_ref, dv_ref, drt_ref, acc):
        hq = pl.program_id(1)

        @pl.when(hq == 0)
        def _():
            acc[...] = jnp.zeros_like(acc)
        qq = q_ref[...]
        kk_ = k_ref[...]
        kk = _dot_nt(kk_, kk_)
        qk = _dot_nt(qq, kk_)
        mc, ms = _pair_masks()
        ra, rb = slice(0, CH), slice(CH, 2 * CH)
        lane = _iota((1, 2 * CH), 1)
        js = (0, 1)
        cs = [slice(j * HD, (j + 1) * HD) for j in js]
        cols = [_head_cols(rt_ref, 2 * hq + j) for j in js]
        g_c = [c_[1] for c_ in cols]
        b_c = [c_[2] for c_ in cols]
        dec = [jnp.where(mc, jnp.exp(jnp.where(mc, c_[1] - c_[0], 0.0)), 0.0) for c_ in cols]
        eg = [jnp.exp(g) for g in g_c]
        egl = [jnp.exp(c_[4] - c_[1]) for c_ in cols]
        dob = [do_ref[:, c] for c in cs]
        dvb = [dvn_ref[:, c] for c in cs]
        dqd = [jnp.concatenate([_dot_nt(d_[ra], ss_ref[j, 0:HD, :]), _dot_nt(d_[rb], ss_ref[j, HD:2 * HD, :])], axis=0)
               for j, d_ in zip(js, dob)]
        dw = [-jnp.concatenate([_dot_nt(d_[ra], ss_ref[j, 0:HD, :]), _dot_nt(d_[rb], ss_ref[j, HD:2 * HD, :])], axis=0)
              for j, d_ in zip(js, dvb)]
        dp = [jnp.where(mc, _dot_nt(d_, vn_ref[:, c]), 0.0) for d_, c in zip(dob, cs)]
        dbuw = [_dot3_tn(ti_ref[:, c], jnp.concatenate([d_, w_], axis=1)) for c, d_, w_ in zip(cs, dvb, dw)]
        dbu = [t[:, :HD] for t in dbuw]
        dbw = [t[:, HD:] for t in dbuw]
        da = [jnp.where(ms, -(_dot_nt(bu, u_ref[:, c]) + _dot_nt(bw, w_ref[:, c])), 0.0)
              for bu, bw, c in zip(dbu, dbw, cs)]
        dm = [a_ * d_ for a_, d_ in zip(da, dec)]
        dn_ = [p_ * d_ for p_, d_ in zip(dp, dec)]
        dbk = [_dot(m_, kk_) for m_ in dm]
        dqs = [_dot(n_, kk_) + e_ * q_ for n_, e_, q_ in zip(dn_, eg, dqd)]
        dks = [_dot_tn(m_, b_ * kk_) + _dot_tn(n_, qq) + el * dkd_ref[:, c] + b_ * (e_ * bw + bk)
               for m_, b_, n_, el, c, e_, bw, bk in zip(dm, b_c, dn_, egl, cs, eg, dbw, dbk)]
        dq_ref[...] = dqs[0] + dqs[1]
        dk_ref[...] = dks[0] + dks[1]
        for j in js:
            c = cs[j]
            dv_ref[:, c] = b_c[j] * dbu[j]
            e = da[j] * (b_c[j] * kk * dec[j]) + dp[j] * (qk * dec[j])
            x = dkd_ref[:, c] * (egl[j] * kk_)
            egk = eg[j] * kk_
            z = e + dqd[j] * (eg[j] * qq) - x + dbw[j] * (b_c[j] * egk)
            zb = dbw[j] * egk + dbu[j] * v_ref[:, c] + dbk[j] * kk_
            sa = jnp.sum(jnp.sum(x[ra], axis=1, keepdims=True), axis=0, keepdims=True)
            sb = jnp.sum(jnp.sum(x[rb], axis=1, keepdims=True), axis=0, keepdims=True)
            la = sa + dgam_ref[0, 0, 2 * j:2 * j + 1, :] * gam_ref[0, 0, 2 * j:2 * j + 1, :]
            lb = sb + dgam_ref[0, 0, 2 * j + 1:2 * j + 2, :] * gam_ref[0, 0, 2 * j + 1:2 * j + 2, :]
            h = 2 * hq + j
            acc[pl.ds(h, 1), :] = _rsum(z.T - e)
            acc[pl.ds(NV + h, 1), :] = _rsum(zb.T)
            acc[pl.ds(2 * NV + h, 1), :] = jnp.where(lane < CH, la, lb)

        @pl.when(hq == NQ - 1)
        def _():
            drt_ref[...] = acc[...]

    return pl.pallas_call(
        body, name="dn_intra_bwd", grid=(nb, NQ),
        in_specs=[blk, blk, blk2, rspec, blk2, blk2, blk2, gspec, gspec,
                  pl.BlockSpec((2, 2 * HD, HD), lambda i, h: (h, i, 0)), blk2, blk2, blk2, blk2],
        out_specs=[blk, blk, blk2, rspec],
        out_shape=[jax.ShapeDtypeStruct((s, NQ * HD), F32), jax.ShapeDtypeStruct((s, NQ * HD), F32),
                   jax.ShapeDtypeStruct((s, NV * HD), F32), jax.ShapeDtypeStruct((3 * NV, s), F32)],
        scratch_shapes=[pltpu.VMEM((3 * NV, 2 * CH), F32)],
        compiler_params=_params(("parallel", "arbitrary")),
    )(q, k, v, rt, do, dvn, dkd, gam, dgam, ssave, tinv, u, w, vn)


def _l2n_heads_bwd(c, d_ref, dc_ref, off, scale):
    for hh in range(NQ):
        cs = slice(off + hh * HD, off + (hh + 1) * HD)
        x = c[:, cs]
        dy = d_ref[:, hh * HD:(hh + 1) * HD]
        r = lax.rsqrt(jnp.sum(x * x, axis=-1, keepdims=True) + L2_EPS)
        dc_ref[:, cs] = (scale * r) * (dy - x * (r * r) * jnp.sum(dy * x, axis=-1, keepdims=True))


def _dn_prep_bwd(dq, dk, dv, drt, pqkv, pab, cw, arow, drow, acol, dcol):
    s = pqkv.shape[0]
    tm = 128
    nt = s // tm
    wq = NQ * HD

    def body(dq_ref, dk_ref, dv_ref, drt_ref, x_ref, xb_ref, ab_ref, cw_ref, ar_ref, dr_ref, ac_ref,
             dc_ref, dx_ref, dab_ref, gcw_ref, gsc_ref, dcs, nxt):
        i = pl.program_id(0)

        @pl.when(i == 0)
        def _():
            nxt[...] = jnp.zeros_like(nxt)
            gcw_ref[...] = jnp.zeros_like(gcw_ref)
            gsc_ref[...] = jnp.zeros_like(gsc_ref)
        x = x_ref[...]
        before = jnp.where(i < nt - 1, xb_ref[...], 0.0)
        cp = _conv_causal(x, before, cw_ref, 4)
        sg = _sigmoid(cp)
        c = cp * sg
        _l2n_heads_bwd(c, dq_ref, dcs, 0, Q_SCALE)
        _l2n_heads_bwd(c, dk_ref, dcs, wq, 1.0)
        dcs[:, 2 * wq:] = dv_ref[...]
        dcp = dcs[...] * (sg * (1.0 + cp * (1.0 - sg)))
        dx_ref[...] = _conv_causal_bwd(dcp, nxt[...], cw_ref, 4).astype(MXU_DT)
        nxt[...] = dcp[:8, :]
        for kq in range(4):
            gcw_ref[kq:kq + 1, :] += _rsum(dcp * _shift_down(x, before, 3 - kq))
        lane = _iota((NV, tm), 1)
        dgt = drt_ref[0:NV, :] + jnp.where((lane & (CH - 1)) == CH - 1, drt_ref[2 * NV:3 * NV, :], 0.0)
        dg = _chunk_cumsum(dgt, 1, rev=True)
        abt = ab_ref[...].T
        zt = abt[0:NV, :] + dc_ref[...]
        gt = -jnp.exp(ac_ref[...]) * _softplus(zt)
        dat = dg * (-jnp.exp(ac_ref[...])) * _sigmoid(zt)
        bt = _sigmoid(abt[NV:2 * NV, :])
        dbt = drt_ref[NV:2 * NV, :] * bt * (1.0 - bt)
        full = jnp.concatenate([dat, dbt, jnp.zeros((128 - 2 * NV, tm), F32)], axis=0)
        dab_ref[...] = full.T.astype(MXU_DT)
        l2 = _iota((NV, 128), 1)
        gsc_ref[...] += jnp.where(l2 == 0, jnp.sum(dg * gt, axis=1, keepdims=True),
                                  jnp.where(l2 == 1, jnp.sum(dat, axis=1, keepdims=True), 0.0))

    return pl.pallas_call(
        body, name="dn_prep_bwd", grid=(nt,),
        in_specs=[_rows(tm, wq, 0, nt), _rows(tm, wq, 0, nt), _rows(tm, 2 * wq, 0, nt),
                  pl.BlockSpec((3 * NV, tm), lambda i: (0, nt - 1 - i)),
                  _rows(tm, 4 * wq, 0, nt), _before(tm, 4 * wq, 0, nt), _rows(tm, 128, 0, nt),
                  _whole((8, 4 * wq)), _whole((1, 128)), _whole((1, 128)), _whole((NV, 1)), _whole((NV, 1))],
        out_specs=[_rows(tm, 4 * wq, 0, nt), _rows(tm, 128, 0, nt), _whole((8, 4 * wq)), _whole((NV, 128))],
        out_shape=[jax.ShapeDtypeStruct((s, 4 * wq), MXU_DT), jax.ShapeDtypeStruct((s, 128), MXU_DT),
                   jax.ShapeDtypeStruct((8, 4 * wq), F32), jax.ShapeDtypeStruct((NV, 128), F32)],
        scratch_shapes=[pltpu.VMEM((tm, 4 * wq), F32), pltpu.VMEM((8, 4 * wq), F32)],
        compiler_params=_params(("arbitrary",)),
    )(dq, dk, dv, drt, pqkv, pqkv, pab, cw, arow, drow, acol, dcol)


def _rg_bwd(drec, px, h, cw, vec, wa, wx, wat, wxt):
    s = px.shape[0]
    tm = _pick(s, (256, 128))
    nt = s // tm
    w = D_RNN

    def body(dr_ref, xr_ref, xb_ref, gr_ref, h_ref, hb_ref, cw_ref, vec_ref, wa_ref, wx_ref, wat_ref, wxt_ref,
             o_ref, gwa_ref, gwx_ref, gcw_ref, gvec_ref, nxt_a, nxt_l, nxt_d):
        i = pl.program_id(0)

        @pl.when(i == 0)
        def _():
            nxt_a[...] = jnp.zeros_like(nxt_a)
            nxt_l[...] = jnp.zeros_like(nxt_l)
            nxt_d[...] = jnp.zeros_like(nxt_d)
            gwa_ref[...] = jnp.zeros_like(gwa_ref)
            gwx_ref[...] = jnp.zeros_like(gwx_ref)
            gcw_ref[...] = jnp.zeros_like(gcw_ref)
            gvec_ref[...] = jnp.zeros_like(gvec_ref)
        first = i == nt - 1
        x = xr_ref[...]
        xbefore = jnp.where(first, 0.0, xb_ref[...])
        hbefore = jnp.where(first, 0.0, hb_ref[...])
        xc = _conv_causal(x, xbefore, cw_ref, 4) + vec_ref[0:1, :]
        r, ig, a, mult, sp = _rg_gates(xc, wa_ref, wx_ref, vec_ref)
        hh = h_ref[...]
        gel, dgel = _gelu_and_grad(gr_ref[...])
        drec_ = dr_ref[...]
        o_ref[:, w:] = (drec_ * hh * dgel).astype(MXU_DT)
        acum, lam = _scan_rev(_shift_up(a, nxt_a[...], 1), drec_ * gel)
        lam = lam + acum * nxt_l[0:1, :]
        nxt_a[...] = a[:8, :]
        nxt_l[...] = lam[:8, :]
        da = lam * _shift_down(hh, hbefore, 1)
        dxc = lam * mult * ig
        dla = da * a - (lam * ig * xc) * (a * a) / mult
        dpr = dla * (-RG_C * sp) * r * (1.0 - r)
        dpi = (lam * mult * xc) * ig * (1.0 - ig)
        dprb = dpr.astype(MXU_DT)
        dpib = dpi.astype(MXU_DT)
        dxc = dxc + jnp.dot(dprb, wat_ref[...], preferred_element_type=F32) \
                  + jnp.dot(dpib, wxt_ref[...], preferred_element_type=F32)
        xcb = xc.astype(MXU_DT)
        gwa_ref[...] += _dot_tn(xcb, dprb)
        gwx_ref[...] += _dot_tn(xcb, dpib)
        o_ref[:, :w] = _conv_causal_bwd(dxc, nxt_d[...], cw_ref, 4).astype(MXU_DT)
        nxt_d[...] = dxc[:8, :]
        for kq in range(4):
            gcw_ref[kq:kq + 1, :] += _rsum(dxc * _shift_down(x, xbefore, 3 - kq))
        gvec_ref[0:1, :] += _rsum(dxc)
        gvec_ref[1:2, :] += _rsum(dpr)
        gvec_ref[2:3, :] += _rsum(dpi)
        gvec_ref[3:4, :] += _rsum(dla * (-RG_C * r)) * (-_sigmoid(-vec_ref[3:4, :]))

    return pl.pallas_call(
        body, name="rg_bwd", grid=(nt,),
        in_specs=[_rows(tm, w, 0, nt), _rows(tm, w, 0, nt), _before(tm, w, 0, nt), _rows(tm, w, 1, nt),
                  _rows(tm, w, 0, nt), _before(tm, w, 0, nt), _whole((8, w)), _whole((8, w)),
                  _whole((w, w)), _whole((w, w)), _whole((w, w)), _whole((w, w))],
        out_specs=[_rows(tm, 2 * w, 0, nt), _whole((w, w)), _whole((w, w)), _whole((8, w)), _whole((8, w))],
        out_shape=[jax.ShapeDtypeStruct((s, 2 * w), MXU_DT), jax.ShapeDtypeStruct((w, w), F32),
                   jax.ShapeDtypeStruct((w, w), F32), jax.ShapeDtypeStruct((8, w), F32),
                   jax.ShapeDtypeStruct((8, w), F32)],
        scratch_shapes=[pltpu.VMEM((8, w), F32)] * 3,
        compiler_params=_params(("arbitrary",)),
    )(drec, px, px, px, h, h, cw, vec, wa, wx, wat, wxt)


def _modulate_bwd(dh1, dxa, x, adam):
    s = x.shape[0]
    tm = _pick(s, (512, 256, 128))

    def body(dh_ref, dxa_ref, x_ref, ada_ref, gx_ref, red_ref):
        @pl.when(pl.program_id(0) == 0)
        def _():
            red_ref[...] = jnp.zeros_like(red_ref)
        dh = dh_ref[...]
        gx_ref[...] = dxa_ref[...] + dh * (1.0 + ada_ref[1:2, :])
        red_ref[0:1, :] += _rsum(dh * x_ref[...])
        red_ref[1:2, :] += _rsum(dh)

    return pl.pallas_call(
        body, name="modulate1_bwd", grid=(s // tm,),
        in_specs=[_rows(tm, D)] * 3 + [_whole((8, D))],
        out_specs=[_rows(tm, D), _whole((8, D))],
        out_shape=[jax.ShapeDtypeStruct((s, D), F32), jax.ShapeDtypeStruct((8, D), F32)],
        compiler_params=_params(("arbitrary",)),
    )(dh1, dxa, x, adam)


def _adamw(parts, w, m, v, name):
    r, c = w.shape
    n_parts = len(parts)
    tm = _row_tile(r, c * 4 * (n_parts + 7))
    c1 = 1.0 - ADAM_B1 ** ADAM_STEP
    c2 = 1.0 - ADAM_B2 ** ADAM_STEP

    def body(*refs):
        g = refs[0][...]
        for p_ref in refs[1:n_parts]:
            g = g + p_ref[...]
        w_ref, m_ref, v_ref, g_out, d_out, m_out, v_out = refs[n_parts:]
        mn = ADAM_B1 * m_ref[...] + (1.0 - ADAM_B1) * g
        vn = ADAM_B2 * v_ref[...] + (1.0 - ADAM_B2) * (g * g)
        g_out[...] = g
        m_out[...] = mn
        v_out[...] = vn
        d_out[...] = -ADAM_LR * ((mn / c1) / (jnp.sqrt(vn / c2) + ADAM_EPS) + ADAM_WD * w_ref[...])

    spec = pl.BlockSpec((tm, c), lambda i: (i, 0))
    return pl.pallas_call(
        body, name=name, grid=(r // tm,),
        in_specs=[spec] * (n_parts + 3), out_specs=[spec] * 4,
        out_shape=[jax.ShapeDtypeStruct((r, c), F32)] * 4, compiler_params=_params(("parallel",)),
    )(*parts, w, m, v)


def _row_tile(r, bytes_per_row):
    for t in (512, 256, 128, 64, 32, 16):
        if r % t == 0 and 2 * t * bytes_per_row <= 20 * 1024 * 1024:
            return t
    return _pick(r, (16, 8))


def _sum_partials(own, recv, name):
    r, c = own.shape
    tm = _row_tile(r, c * (4 + 4 + 3 * 2))

    def body(o_ref, r_ref, out_ref):
        out_ref[...] = ((o_ref[...] + r_ref[0].astype(F32)) + r_ref[1].astype(F32)) + r_ref[2].astype(F32)

    return pl.pallas_call(
        body, name=name, grid=(r // tm,),
        in_specs=[pl.BlockSpec((tm, c), lambda i: (i, 0)), pl.BlockSpec((3, tm, c), lambda i: (0, i, 0))],
        out_specs=pl.BlockSpec((tm, c), lambda i: (i, 0)),
        out_shape=jax.ShapeDtypeStruct((r, c), F32), compiler_params=_params(("parallel",)),
    )(own, recv)


def _sum8(g):
    _, r, c = g.shape
    tm = _pick(r, (256, 128, 64, 32, 16, 8))

    def body(g_ref, o_ref):
        acc = g_ref[0]
        for k in range(1, 8):
            acc = acc + g_ref[k]
        o_ref[...] = acc

    return pl.pallas_call(
        body, name="sum8", grid=(r // tm,),
        in_specs=[pl.BlockSpec((8, tm, c), lambda i: (0, i, 0))],
        out_specs=pl.BlockSpec((tm, c), lambda i: (i, 0)),
        out_shape=jax.ShapeDtypeStruct((r, c), F32), compiler_params=_params(("parallel",)),
    )(g)


def _silu_rows(x):
    def body(x_ref, o_ref):
        xx = x_ref[...]
        o_ref[...] = xx * _sigmoid(xx)

    return pl.pallas_call(body, name="silu_rows", out_shape=jax.ShapeDtypeStruct(x.shape, F32))(x)


def _coords():
    return lax.axis_index("x"), lax.axis_index("y"), lax.axis_index("c")


def _allgather8(v, name):
    r, n = v.shape

    def body(v_ref, out_ref, send_sems, recv_sems):
        x, y, c = _coords()
        me = 4 * x + 2 * y + c
        out_ref[me] = v_ref[...]
        peers = []
        for k in range(1, 8):
            px = 1 - x if k & 4 else x
            py = 1 - y if k & 2 else y
            pc = 1 - c if k & 1 else c
            peers.append((px, py, pc))

        def copy(k, slot, to):
            return pltpu.make_async_remote_copy(
                src_ref=v_ref, dst_ref=out_ref.at[slot], send_sem=send_sems.at[k], recv_sem=recv_sems.at[k],
                device_id=to, device_id_type=MESH)

        sends = [copy(k, me, p) for k, p in enumerate(peers)]
        for cp in sends:
            cp.start()
        for k, (px, py, pc) in enumerate(peers):
            copy(k, 4 * px + 2 * py + pc, (px, py, pc)).wait_recv()
        for cp in sends:
            cp.wait_send()

    return pl.pallas_call(
        body, name=name, out_shape=jax.ShapeDtypeStruct((8, r, n), v.dtype),
        in_specs=[pl.BlockSpec(memory_space=pltpu.VMEM)], out_specs=pl.BlockSpec(memory_space=pltpu.VMEM),
        scratch_shapes=[pltpu.SemaphoreType.DMA((7,)), pltpu.SemaphoreType.DMA((7,))],
        compiler_params=pltpu.CompilerParams(vmem_limit_bytes=VMEM_LIMIT),
    )(v)


def _other_chips(x, y):
    return [(1 - x, y), (x, 1 - y), (1 - x, 1 - y)]


_HBM = pl.BlockSpec(memory_space=pl.ANY)


def _gather_shards(shards):
    n = len(shards)

    def body(*refs):
        ins, outs = refs[:n], refs[n:2 * n]
        send_sems, recv_sems, local_sems = refs[2 * n:]
        x, y, c = _coords()
        s_me = 2 * x + y
        sibling = (x, y, 1 - c)
        chips = _other_chips(x, y)

        def half(i, slot, hc):
            rh = shards[i].shape[0] // 2
            return outs[i].at[slot, pl.ds(pl.multiple_of(hc * rh, 16), rh), :]

        def copy(i, k, src, dst, to):
            return pltpu.make_async_remote_copy(src_ref=src, dst_ref=dst, send_sem=send_sems.at[6 * i + k],
                                                recv_sem=recv_sems.at[6 * i + k], device_id=to, device_id_type=MESH)

        local = [pltpu.make_async_copy(ins[i], outs[i].at[s_me], local_sems.at[i]) for i in range(n)]
        for cp in local:
            cp.start()
        first = []
        for i in range(n):
            rh = shards[i].shape[0] // 2
            my_half = ins[i].at[pl.ds(pl.multiple_of(c * rh, 16), rh), :]
            first += [copy(i, j, my_half, half(i, s_me, c), (px, py, c)) for j, (px, py) in enumerate(chips)]
        for cp in first:
            cp.start()
        passed = []
        for i in range(n):
            for j, (px, py) in enumerate(chips):
                land = half(i, 2 * px + py, c)
                copy(i, j, land, land, (px, py, c)).wait_recv()
                fw = copy(i, 3 + j, land, land, sibling)
                fw.start()
                passed.append(fw)
        for i in range(n):
            for j, (px, py) in enumerate(chips):
                land = half(i, 2 * px + py, 1 - c)
                copy(i, 3 + j, land, land, sibling).wait_recv()
        for cp in first + passed:
            cp.wait_send()
        for cp in local:
            cp.wait()

    return pl.pallas_call(
        body, name="gather_shards",
        out_shape=[jax.ShapeDtypeStruct((4,) + v.shape, v.dtype) for v in shards],
        in_specs=[_HBM] * n, out_specs=[_HBM] * n,
        scratch_shapes=[pltpu.SemaphoreType.DMA((6 * n,)), pltpu.SemaphoreType.DMA((6 * n,)),
                        pltpu.SemaphoreType.DMA((n,))],
    )(*shards)


def _scatter_partials(gs):
    n = len(gs)

    def body(*refs):
        ins, outs = refs[:n], refs[n:2 * n]
        send_sems, recv_sems = refs[2 * n:]
        x, y, c = _coords()
        chips = _other_chips(x, y)
        cps = [pltpu.make_async_remote_copy(
                   src_ref=ins[i].at[2 * px + py], dst_ref=outs[i].at[j], send_sem=send_sems.at[3 * i + j],
                   recv_sem=recv_sems.at[3 * i + j], device_id=(px, py, c), device_id_type=MESH)
               for i in range(n) for j, (px, py) in enumerate(chips)]
        for cp in cps:
            cp.start()
        for cp in cps:
            cp.wait_recv()
        for cp in cps:
            cp.wait_send()

    return pl.pallas_call(
        body, name="scatter_partials",
        out_shape=[jax.ShapeDtypeStruct((3,) + g.shape[1:], g.dtype) for g in gs],
        in_specs=[_HBM] * n, out_specs=[_HBM] * n,
        scratch_shapes=[pltpu.SemaphoreType.DMA((3 * n,)), pltpu.SemaphoreType.DMA((3 * n,))],
    )(*gs)


def _swap_sibling(vs):
    n = len(vs)

    def body(*refs):
        ins, outs = refs[:n], refs[n:2 * n]
        send_sems, recv_sems = refs[2 * n:]
        x, y, c = _coords()
        cps = [pltpu.make_async_remote_copy(src_ref=ins[i], dst_ref=outs[i], send_sem=send_sems.at[i],
                                            recv_sem=recv_sems.at[i], device_id=(x, y, 1 - c), device_id_type=MESH)
               for i in range(n)]
        for cp in cps:
            cp.start()
        for cp in cps:
            cp.wait()

    return pl.pallas_call(
        body, name="swap_sibling", out_shape=[jax.ShapeDtypeStruct(v.shape, v.dtype) for v in vs],
        in_specs=[_HBM] * n, out_specs=[_HBM] * n,
        scratch_shapes=[pltpu.SemaphoreType.DMA((n,)), pltpu.SemaphoreType.DMA((n,))],
    )(*vs)


def _pad_rows(a, rows):
    return jnp.pad(a, ((0, rows - a.shape[0]), (0, 0)))


def _block_diag(w):
    eye = jnp.eye(RG_BLOCKS, dtype=w.dtype)
    return (eye[:, None, :, None] * w[:, :, None, :]).reshape(D_RNN, D_RNN)


def _diag_blocks(g):
    g4 = g.reshape(RG_BLOCKS, RG_BW, RG_BLOCKS, RG_BW)
    idx = jnp.arange(RG_BLOCKS)
    return g4[idx, :, idx, :]


def _prepare_weights(p):
    w = {}
    wi = p["w_in"].astype(MXU_DT)
    cat = jnp.concatenate([wi[:, 2560:6656], wi[:, 6656:8704], wi[:, 8736:10784], wi[:, 0:2560],
                           wi[:, 8704:8736], jnp.zeros((D, 96), MXU_DT)], axis=1)
    w["in_cat"], w["in_cat_t"] = cat, cat.T
    for k_, n_ in (("pa", "w_proj_a"), ("pb", "w_proj_b"), ("out", "w_out"), ("down", "ffn_w_down")):
        w[k_] = p[n_].astype(MXU_DT)
        w[k_ + "_t"] = w[k_].T
    w["gu"] = jnp.concatenate([p["ffn_w_gate"], p["ffn_w_up"]], axis=1).astype(MXU_DT)
    w["gu_t"] = w["gu"].T
    w["rg_cw"] = _pad_rows(p["rg_conv_w"], 8)
    w["dn_cw"] = _pad_rows(p["dn_conv_w"], 8)
    w["ffn_cw"] = _pad_rows(p["ffn_conv_w"], 8)
    w["rg_vec"] = _pad_rows(jnp.stack([p["rg_conv_b"], p["rg_b_a"], p["rg_b_x"], p["rg_lambda"]]), 8)
    w["wa"] = _block_diag(p["rg_w_a"]).astype(MXU_DT)
    w["wx"] = _block_diag(p["rg_w_x"]).astype(MXU_DT)
    w["wa_t"], w["wx_t"] = w["wa"].T, w["wx"].T
    w["arow"] = jnp.pad(p["dn_a_log"], (0, 128 - NV))[None, :]
    w["drow"] = jnp.pad(p["dn_dt_bias"], (0, 128 - NV))[None, :]
    w["acol"] = p["dn_a_log"][:, None]
    w["dcol"] = p["dn_dt_bias"][:, None]
    w["nw"] = p["dn_norm_w"][None, :]
    w["ffn_cb"] = p["ffn_conv_b"][None, :]
    for n_ in ("ln1_g", "ln1_b", "ln2_g", "ln2_b"):
        w[n_] = p[n_][None, :]
    return w


def _local_step(x, tgt, adam, w):
    h1 = _modulate(x, adam)
    cat = w["in_cat"]
    pqkv = _mm(h1, cat[:, 0:4096], name="proj_qkv")
    pz = _mm(h1, cat[:, 4096:6144], name="proj_z")
    pg = _mm(h1, cat[:, 6144:8192], name="proj_g")
    px = _mm(h1, cat[:, 8192:10752], name="proj_x")
    pab = _mm(h1, cat[:, 10752:10880], name="proj_ab")
    hrec, rec = _rg_fwd(px, w["rg_cw"], w["rg_vec"], w["wa"], w["wx"])
    q, k, v, rt = _dn_prep(pqkv, pab, w["dn_cw"], w["arow"], w["drow"])
    u, ww, qd, kd, pm, tinv, gam = _dn_intra(q, k, v, rt)
    o, vn, ssave = _dn_seq(u, ww, qd, kd, pm, gam)
    dn = _dn_post(o, pz, w["nw"])
    ya = _mm(rec, w["pa"], name="proj_a")
    yb = _mm(dn, w["pb"], name="proj_b")
    merged = _merge(pg, ya, yb)
    mix = _mm(merged, w["out"], name="proj_out")
    r1, x2, h2 = _ln1(x, mix, adam, w["ln1_g"], w["ln1_b"])
    gu = _mm(h2, w["gu"], name="ffn_gu")
    act = _ffn_act(gu, w["ffn_cw"], w["ffn_cb"])
    ff = _mm(act, w["down"], name="ffn_down")
    dff, dx2a, red2 = _ln2_loss(x2, ff, tgt, adam, w["ln2_g"], w["ln2_b"])
    g = {}
    dact = _mm(dff, w["down_t"], name="d_act")
    g["ffn_w_down"] = _mm(act, dff, name="g_down", trans_a=True)
    dgu, gcw_f, gcb_f = _ffn_bwd(dact, gu, w["ffn_cw"], w["ffn_cb"])
    dh2 = _mm(dgu, w["gu_t"], name="d_h2")
    ggu = _mm(h2, dgu, name="g_gu", trans_a=True)
    g["ffn_w_gate"], g["ffn_w_up"] = ggu[:, :D_FF], ggu[:, D_FF:]
    g["ffn_conv_w"], g["ffn_conv_b"] = gcw_f[0:3], gcb_f[0]
    dmix, dxa, red1 = _ln1_bwd(dh2, dx2a, x2, r1, mix, adam, w["ln1_g"])
    dmg = _mm(dmix, w["out_t"], name="d_merged")
    g["w_out"] = _mm(merged, dmix, name="g_out", trans_a=True)
    dya, dyb, dpg = _merge_bwd(dmg, pg, ya, yb)
    drec = _mm(dya, w["pa_t"], name="d_rec")
    g["w_proj_a"] = _mm(rec, dya, name="g_pa", trans_a=True)
    ddn = _mm(dyb, w["pb_t"], name="d_dn")
    g["w_proj_b"] = _mm(dn, dyb, name="g_pb", trans_a=True)
    do, dpz, gnw = _dn_post_bwd(ddn, o, pz, w["nw"])
    dvn, dkd, dgam = _dn_seq_bwd(do, qd, kd, pm, ww, vn, ssave, gam)
    dq, dk, dv, drt = _dn_intra_bwd(q, k, v, rt, do, dvn, dkd, gam, dgam, ssave, tinv, u, ww, vn)
    dpqkv, dpab, gcw_d, gsc = _dn_prep_bwd(dq, dk, dv, drt, pqkv, pab, w["dn_cw"], w["arow"], w["drow"],
                                           w["acol"], w["dcol"])
    dpx, gwa, gwx, gcw_r, gvec = _rg_bwd(drec, px, hrec, w["rg_cw"], w["rg_vec"], w["wa"], w["wx"],
                                         w["wa_t"], w["wx_t"])
    dproj = jnp.concatenate([dpqkv, dpz, dpg, dpx, dpab], axis=1)
    dh1 = _mm(dproj, w["in_cat_t"], name="d_h1")
    gc = _mm(h1, dproj, name="g_in", trans_a=True)
    g["w_in"] = jnp.concatenate([gc[:, 8192:10752], gc[:, 0:4096], gc[:, 4096:6144], gc[:, 10752:10784],
                                 gc[:, 6144:8192]], axis=1)
    gx, red0 = _modulate_bwd(dh1, dxa, x, adam)
    g["rg_conv_w"], g["rg_conv_b"] = gcw_r[0:4], gvec[0]
    g["rg_w_a"], g["rg_w_x"] = _diag_blocks(gwa), _diag_blocks(gwx)
    g["rg_b_a"], g["rg_b_x"], g["rg_lambda"] = gvec[1], gvec[2], gvec[3]
    g["dn_conv_w"] = gcw_d[0:4]
    g["dn_a_log"], g["dn_dt_bias"], g["dn_norm_w"] = gsc[:, 0], gsc[:, 1], gnw[0]
    g["ln1_g"], g["ln1_b"] = red1[2], red1[3]
    g["ln2_g"], g["ln2_b"] = red2[0], red2[1]
    d_ada = jnp.concatenate([red0[1], red0[0], red1[4], red1[1], red1[0], red2[2]])
    return red2[3, 0], gx, g, d_ada


_BIG = ("w_in", "w_proj_a", "w_proj_b", "w_out", "ffn_w_gate", "ffn_w_up", "ffn_w_down")
_COL_SHARDED = ("w_in", "ffn_w_gate", "ffn_w_up")
_CONV = ("rg_conv_w", "dn_conv_w", "ffn_conv_w")
_REPL = ("b_ada", "rg_conv_b", "rg_w_a", "rg_b_a", "rg_w_x", "rg_b_x", "rg_lambda", "dn_a_log",
         "dn_dt_bias", "dn_norm_w", "ln1_g", "ln1_b", "ffn_conv_b", "ln2_g", "ln2_b")
_NAMES = ("w_ada", "b_ada", "w_in", "rg_conv_w", "rg_conv_b", "rg_w_a", "rg_b_a", "rg_w_x", "rg_b_x",
          "rg_lambda", "dn_conv_w", "dn_a_log", "dn_dt_bias", "dn_norm_w", "w_proj_a", "w_proj_b", "w_out",
          "ln1_g", "ln1_b", "ffn_w_gate", "ffn_w_up", "ffn_conv_w", "ffn_conv_b", "ffn_w_down", "ln2_g", "ln2_b")


def _pack(arrs, width, row_mult):
    pieces = []
    for a in arrs:
        f = a.reshape(-1)
        pieces.append(jnp.pad(f, (0, (-f.shape[0]) % width)))
    flat = jnp.concatenate(pieces)
    rows = flat.shape[0] // width
    return jnp.pad(flat, (0, ((-rows) % row_mult) * width)).reshape(-1, width)


def _unpack(flat, shapes, width):
    out, row = [], 0
    for shp in shapes:
        n = 1
        for d_ in shp:
            n *= d_
        rows = -(-n // width)
        out.append(flat[row:row + rows].reshape(-1)[:n].reshape(shp))
        row += rows
    return out


def _stack_shards(name, full):
    if name in _COL_SHARDED or name in _CONV:
        r, ncol = full.shape
        return full.reshape(r, 4, ncol // 4).transpose(1, 0, 2)
    return full.reshape((4, full.shape[0] // 4) + full.shape[1:])


def _unstack_shards(name, st):
    if name in _COL_SHARDED or name in _CONV:
        return st.transpose(1, 0, 2).reshape(st.shape[1], 4 * st.shape[2])
    return st.reshape((4 * st.shape[1],) + st.shape[2:])


def kernel(x, c, w_ada, b_ada, w_in, rg_conv_w, rg_conv_b, rg_w_a, rg_b_a, rg_w_x, rg_b_x, rg_lambda, dn_conv_w, dn_a_log, dn_dt_bias, dn_norm_w, w_proj_a, w_proj_b, w_out, ln1_g, ln1_b, ffn_w_gate, ffn_w_up, ffn_conv_w, ffn_conv_b, ffn_w_down, ln2_g, ln2_b, loss_target, m_w_ada, m_b_ada, m_w_in, m_rg_conv_w, m_rg_conv_b, m_rg_w_a, m_rg_b_a, m_rg_w_x, m_rg_b_x, m_rg_lambda, m_dn_conv_w, m_dn_a_log, m_dn_dt_bias, m_dn_norm_w, m_w_proj_a, m_w_proj_b, m_w_out, m_ln1_g, m_ln1_b, m_ffn_w_gate, m_ffn_w_up, m_ffn_conv_w, m_ffn_conv_b, m_ffn_w_down, m_ln2_g, m_ln2_b, v_w_ada, v_b_ada, v_w_in, v_rg_conv_w, v_rg_conv_b, v_rg_w_a, v_rg_b_a, v_rg_w_x, v_rg_b_x, v_rg_lambda, v_dn_conv_w, v_dn_a_log, v_dn_dt_bias, v_dn_norm_w, v_w_proj_a, v_w_proj_b, v_w_out, v_ln1_g, v_ln1_b, v_ffn_w_gate, v_ffn_w_up, v_ffn_conv_w, v_ffn_conv_b, v_ffn_w_down, v_ln2_g, v_ln2_b):
    args = locals()
    wts = {n: args[n][0] for n in _NAMES}
    mom = {n: args["m_" + n][0] for n in _NAMES}
    var = {n: args["v_" + n][0] for n in _NAMES}
    xs, tgt = x[0], loss_target[0]
    ix, iy, ic = _coords()
    shard = 2 * ix + iy
    batch = 4 * ix + 2 * iy + ic

    c_all = _allgather8(_pad_rows(c, 8), "gather_c")[:, 0, :]
    sc16 = _pad_rows(_silu_rows(c_all), 16)
    ada_cols = _mm(sc16, w_ada[0], name="ada")[:8]
    ada_g = _allgather8(ada_cols, "gather_ada")
    ada_all = jnp.concatenate([ada_g[0], ada_g[2], ada_g[4], ada_g[6]], axis=1) + b_ada
    adam = _pad_rows(lax.dynamic_index_in_dim(ada_all, batch, 0, keepdims=False).reshape(6, D), 8)

    big_all = _gather_shards([wts[n].astype(MXU_DT) for n in _BIG])
    full = {n: _unstack_shards(n, a) for n, a in zip(_BIG, big_all)}
    conv_shard = _pack([wts[n] for n in _CONV], 128, 8)
    conv_all = _allgather8(conv_shard, "gather_conv")
    shapes_conv = [wts[n].shape for n in _CONV]
    per_shard = [_unpack(conv_all[2 * s], shapes_conv, 128) for s in range(4)]
    for i, n in enumerate(_CONV):
        full[n] = _unstack_shards(n, jnp.stack([per_shard[s][i] for s in range(4)]))
    for n in _REPL:
        full[n] = wts[n]

    loss_b, gx, g, d_ada = _local_step(xs, tgt, adam, _prepare_weights(full))

    recv = _scatter_partials([_stack_shards(n, g[n]).astype(MXU_DT) for n in _BIG])
    parts = []
    for n, r_ in zip(_BIG, recv):
        axis = 1 if n in _COL_SHARDED else 0
        width = wts[n].shape[axis]
        own = lax.dynamic_slice_in_dim(g[n], shard * width, width, axis=axis)
        parts.append(_sum_partials(own, r_, "sum_" + n))
    parts_sib = _swap_sibling(parts)
    out = {n: [] for n in _NAMES}
    for n, p_, q_ in zip(_BIG, parts, parts_sib):
        out[n] = list(_adamw([p_, q_], wts[n], mom[n], var[n], "adamw_" + n))

    small_names = _CONV + _REPL[1:]
    small = _pack([d_ada, jnp.full((1,), loss_b, F32)] + [g[n] for n in small_names], 128, 64)
    small_all = _allgather8(small, "gather_small")
    total = _sum8(small_all)
    shapes_small = [(6 * D,), (1,)] + [full[n].shape for n in small_names]
    tot = _unpack(total, shapes_small, 128)
    gsum = dict(zip(small_names, tot[2:]))
    gsum["b_ada"] = tot[0]
    loss = tot[1][0]
    for n in _CONV:
        gsum[n] = lax.dynamic_index_in_dim(_stack_shards(n, gsum[n]), shard, 0, keepdims=False)
    d_ada_all = small_all[:, :6 * D // 128, :].reshape(8, 6 * D)
    cols = lax.dynamic_slice_in_dim(d_ada_all, shard * (6 * D // 4), 6 * D // 4, axis=1)
    g_wada = _mm(sc16, _pad_rows(cols, 16), name="g_ada", trans_a=True)
    res = _adamw([g_wada], wts["w_ada"], mom["w_ada"], var["w_ada"], "adamw_ada")
    out["w_ada"] = list(res)
    names_s = _CONV + _REPL
    shapes_s = [wts[n].shape for n in names_s]
    pk = lambda d_: _pack([d_[n] for n in names_s], 128, 64)
    res_s = _adamw([pk(gsum)], pk(wts), pk(mom), pk(var), "adamw_small")
    for r_ in res_s:
        for n, a in zip(names_s, _unpack(r_, shapes_s, 128)):
            out[n].append(a)

    outs = [loss, gx[None]]
    for i in range(4):
        outs += [out[n][i][None] for n in _NAMES]
    return tuple(outs)
```

```python
import functools

import jax
import jax.numpy as jnp
from jax import lax
from jax.experimental import pallas as pl
from jax.experimental.pallas import tpu as pltpu

F32 = jnp.float32
BF16 = jnp.bfloat16
MXU_DT = BF16

D = 1024
D_RNN = 1280
RG_BLOCKS = 16
RG_BW = 80
RG_C = 8.0
NQ = 8
NV = 16
HD = 128
CH = 64
D_FF = 2816
LN_EPS = 1e-5
RMS_EPS = 1e-6
L2_EPS = 1e-6
ALPHA = 2.0 ** 0.25
Q_SCALE = HD ** -0.5
N_CAT = 10880
VMEM_LIMIT = 56 * 1024 * 1024
MM_VMEM_BUDGET = 36 * 1024 * 1024
SEQ_GROUP = 8
MESH = pl.DeviceIdType.MESH

ADAM_LR, ADAM_B1, ADAM_B2, ADAM_EPS, ADAM_WD, ADAM_STEP = 1e-3, 0.9, 0.999, 1e-8, 0.01, 10


def _sigmoid(x):
    return 0.5 * jnp.tanh(0.5 * x) + 0.5


def _softplus(x):
    return jnp.maximum(x, 0.0) + jnp.log1p(jnp.exp(-jnp.abs(x)))


_GC = 0.7978845608028654


def _gelu(x):
    return 0.5 * x * (1.0 + jnp.tanh(_GC * (x + 0.044715 * x * x * x)))


def _gelu_and_grad(x):
    t = jnp.tanh(_GC * (x + 0.044715 * x * x * x))
    g = 0.5 * x * (1.0 + t)
    dg = 0.5 * (1.0 + t) + 0.5 * x * (1.0 - t * t) * _GC * (1.0 + 3 * 0.044715 * x * x)
    return g, dg


def _neg_expm1(y):
    series = -y * (1.0 + 0.5 * y * (1.0 + y * (1.0 / 3.0)))
    return jnp.where(y > -0.01, series, 1.0 - jnp.exp(y))


def _dot(a, b):
    return jnp.dot(a.astype(MXU_DT), b.astype(MXU_DT), preferred_element_type=F32)


def _dot_nt(a, b):
    return lax.dot_general(a.astype(MXU_DT), b.astype(MXU_DT), (((1,), (1,)), ((), ())),
                           preferred_element_type=F32)


def _dot_tn(a, b):
    return lax.dot_general(a.astype(MXU_DT), b.astype(MXU_DT), (((0,), (0,)), ((), ())),
                           preferred_element_type=F32)


def _split(a):
    hi = a.astype(BF16)
    return hi, (a - hi.astype(F32)).astype(BF16)


def _dot3(a, b, dims=(((1,), (0,)), ((), ()))):
    ah, al = _split(a)
    bh, bl = _split(b)
    d = lambda p, q: lax.dot_general(p, q, dims, preferred_element_type=F32)
    return d(ah, bh) + (d(al, bh) + d(ah, bl))


def _dot3_tn(a, b):
    return _dot3(a, b, (((0,), (0,)), ((), ())))


def _iota(shape, dim):
    return lax.broadcasted_iota(jnp.int32, shape, dim)


def _shift_down(x, before, j):
    if j == 0:
        return x
    xr = pltpu.roll(x, j, 0)
    br = pltpu.roll(before, j, 0)
    top = jnp.where(_iota(br.shape, 0) < j, br, xr[:8])
    return jnp.concatenate([top, xr[8:]], axis=0)


def _shift_up(x, after, j):
    if j == 0:
        return x
    t = x.shape[0]
    xr = pltpu.roll(x, t - j, 0)
    ar = pltpu.roll(after, 8 - j, 0)
    bot = jnp.where(_iota(ar.shape, 0) >= 8 - j, ar, xr[t - 8:])
    return jnp.concatenate([xr[:t - 8], bot], axis=0)


def _taps(x, before, k):
    return [_shift_down(x, before, k - 1 - i) for i in range(k)]


def _conv_taps(taps, w_ref):
    y = w_ref[0:1, :] * taps[0]
    for i in range(1, len(taps)):
        y = y + w_ref[i:i + 1, :] * taps[i]
    return y


def _conv_causal(x, before, w_ref, k):
    return _conv_taps(_taps(x, before, k), w_ref)


def _conv_causal_bwd(dy, after, w_ref, k):
    dx = w_ref[k - 1:k, :] * dy
    for i in range(k - 1):
        dx = dx + w_ref[i:i + 1, :] * _shift_up(dy, after, k - 1 - i)
    return dx


def _scan_fwd(a, u):
    t = a.shape[0]
    rows = _iota(a.shape, 0)
    d = 1
    while d < t:
        m = rows >= d
        u = u + jnp.where(m, a * pltpu.roll(u, d, 0), 0.0)
        a = jnp.where(m, a * pltpu.roll(a, d, 0), a)
        d *= 2
    return a, u


def _scan_rev(a, u):
    t = a.shape[0]
    rows = _iota(a.shape, 0)
    d = 1
    while d < t:
        m = rows < t - d
        u = u + jnp.where(m, a * pltpu.roll(u, t - d, 0), 0.0)
        a = jnp.where(m, a * pltpu.roll(a, t - d, 0), a)
        d *= 2
    return a, u


def _chunk_cumsum(g, axis, rev=False):
    n = g.shape[axis]
    pos = _iota(g.shape, axis) & (CH - 1)
    d = 1
    while d < CH:
        if rev:
            g = g + jnp.where(pos < CH - d, pltpu.roll(g, n - d, axis), 0.0)
        else:
            g = g + jnp.where(pos >= d, pltpu.roll(g, d, axis), 0.0)
        d *= 2
    return g


def _ln_stats(r):
    mu = jnp.mean(r, axis=-1, keepdims=True)
    xc = r - mu
    var = jnp.mean(xc * xc, axis=-1, keepdims=True)
    rstd = lax.rsqrt(var + LN_EPS)
    return xc * rstd, rstd


def _ln_bwd(dy, xhat, rstd, g):
    dxh = dy * g
    return rstd * (dxh - jnp.mean(dxh, axis=-1, keepdims=True)
                   - xhat * jnp.mean(dxh * xhat, axis=-1, keepdims=True))


def _rsum(x):
    return jnp.sum(x, axis=0, keepdims=True)


def _params(sem):
    return pltpu.CompilerParams(dimension_semantics=sem, vmem_limit_bytes=VMEM_LIMIT)


def _pick(n, cands):
    for c in cands:
        if n % c == 0:
            return c
    return n


def _rows(tm, w, col=0, nt=None):
    if nt is None:
        return pl.BlockSpec((tm, w), lambda i: (i, col))
    return pl.BlockSpec((tm, w), lambda i: (nt - 1 - i, col))


def _before(tm, w, col=0, nt=None):
    r = tm // 8
    if nt is None:
        return pl.BlockSpec((8, w), lambda i: (jnp.maximum(i * r - 1, 0), col))
    return pl.BlockSpec((8, w), lambda i: (jnp.maximum((nt - 1 - i) * r - 1, 0), col))


def _whole(shape):
    return pl.BlockSpec(shape, lambda *_: (0,) * len(shape))


def _mm_plan(a, b, out_dtype):
    m, kk = a.shape
    _, n = b.shape
    tm = _pick(m, (512, 256, 128))
    tk = kk if kk <= 5632 else _pick(kk, (2176, 2048, 1024))
    nk = kk // tk

    def vmem_bytes(tn):
        blocks = tm * tk * a.dtype.itemsize + tk * tn * b.dtype.itemsize + tm * tn * jnp.dtype(out_dtype).itemsize
        return 2 * blocks + (tm * tn * 4 if nk > 1 else 0)

    cands = [t for t in (1408, 1280, 1024, 640, 512, 256, 128) if n % t == 0] or [n]
    tn = next((t for t in cands if vmem_bytes(t) <= MM_VMEM_BUDGET), cands[-1])
    return tm, tn, tk, nk


def _mm_side(a, b, name, out_dtype, side):
    m, _ = a.shape
    _, n = b.shape
    tm, tn, tk, nk = _mm_plan(a, b, out_dtype)
    ni, nj = m // tm, n // tn
    n_in, n_out = len(side.ins), len(side.out_shapes)

    def body(*refs):
        a_ref, b_ref = refs[0], refs[1]
        s_in = refs[2:2 + n_in]
        o_ref = refs[2 + n_in]
        s_out = refs[3 + n_in:3 + n_in + n_out]
        acc = refs[3 + n_in + n_out]
        sems = refs[4 + n_in + n_out:]
        i, j, k = pl.program_id(0), pl.program_id(1), pl.program_id(2)

        @pl.when((i == 0) & (j == 0) & (k == 0))
        def _():
            side.start(s_in, s_out, sems)

        @pl.when(k == 0)
        def _():
            acc[...] = jnp.zeros_like(acc)
        acc[...] += _dot(a_ref[...], b_ref[...])

        @pl.when(k == nk - 1)
        def _():
            o_ref[...] = acc[...].astype(out_dtype)

        @pl.when((i == ni - 1) & (j == nj - 1) & (k == nk - 1))
        def _():
            side.finish(s_in, s_out, sems)

    return pl.pallas_call(
        body, name=name, grid=(ni, nj, nk),
        in_specs=[pl.BlockSpec((tm, tk), lambda i, j, k: (i, k)),
                  pl.BlockSpec((tk, tn), lambda i, j, k: (k, j))] + [_HBM] * n_in,
        out_specs=[pl.BlockSpec((tm, tn), lambda i, j, k: (i, j))] + [_HBM] * n_out,
        out_shape=[jax.ShapeDtypeStruct((m, n), out_dtype)] + list(side.out_shapes),
        scratch_shapes=[pltpu.VMEM((tm, tn), F32)] + list(side.sems),
        compiler_params=_params(("arbitrary", "arbitrary", "arbitrary")),
    )(a, b, *side.ins)


def _mm(a, b, *, name, trans_a=False, out_dtype=F32, side=None):
    if trans_a:
        return _mm(a.T, b, name=name, out_dtype=out_dtype, side=side)
    if side is not None:
        return _mm_side(a, b, name, out_dtype, side)
    m, kk = a.shape
    _, n = b.shape
    tm, tn, tk, nk = _mm_plan(a, b, out_dtype)

    if nk == 1:
        def body(a_ref, b_ref, o_ref):
            o_ref[...] = _dot(a_ref[...], b_ref[...]).astype(out_dtype)
        scratch = []
    else:
        def body(a_ref, b_ref, o_ref, acc):
            k = pl.program_id(2)

            @pl.when(k == 0)
            def _():
                acc[...] = jnp.zeros_like(acc)
            acc[...] += _dot(a_ref[...], b_ref[...])

            @pl.when(k == nk - 1)
            def _():
                o_ref[...] = acc[...].astype(out_dtype)
        scratch = [pltpu.VMEM((tm, tn), F32)]

    return pl.pallas_call(
        body, name=name, grid=(m // tm, n // tn, nk),
        in_specs=[pl.BlockSpec((tm, tk), lambda i, j, k: (i, k)),
                  pl.BlockSpec((tk, tn), lambda i, j, k: (k, j))],
        out_specs=pl.BlockSpec((tm, tn), lambda i, j, k: (i, j)),
        out_shape=jax.ShapeDtypeStruct((m, n), out_dtype),
        scratch_shapes=scratch,
        compiler_params=_params(("parallel", "parallel", "arbitrary")),
    )(a, b)


def _modulate(x, adam):
    s = x.shape[0]
    tm = _pick(s, (512, 256, 128))

    def body(x_ref, ada_ref, o_ref):
        o_ref[...] = (x_ref[...] * (1.0 + ada_ref[1:2, :]) + ada_ref[0:1, :]).astype(MXU_DT)

    return pl.pallas_call(
        body, name="modulate1", grid=(s // tm,),
        in_specs=[_rows(tm, D), _whole((8, D))], out_specs=_rows(tm, D),
        out_shape=jax.ShapeDtypeStruct((s, D), MXU_DT), compiler_params=_params(("parallel",)),
    )(x, adam)


def _rg_gates(xc, wa_ref, wx_ref, vec_ref):
    xb = xc.astype(MXU_DT)
    r = _sigmoid(jnp.dot(xb, wa_ref[...], preferred_element_type=F32) + vec_ref[1:2, :])
    ig = _sigmoid(jnp.dot(xb, wx_ref[...], preferred_element_type=F32) + vec_ref[2:3, :])
    sp = _softplus(-vec_ref[3:4, :])
    la = -RG_C * r * sp
    a = jnp.exp(la)
    n1 = _neg_expm1(2.0 * la)
    rmult = lax.rsqrt(jnp.maximum(n1, 1e-20))
    return r, ig, a, n1 * rmult, sp, rmult


def _rg_fwd(px, cw, vec, wa, wx):
    s = px.shape[0]
    tm = _pick(s, (256, 128))
    w = D_RNN

    def body(xr_ref, gr_ref, cw_ref, vec_ref, wa_ref, wx_ref, h_ref, rec_ref, prev_x, prev_h):
        @pl.when(pl.program_id(0) == 0)
        def _():
            prev_x[...] = jnp.zeros_like(prev_x)
            prev_h[...] = jnp.zeros_like(prev_h)
        x = xr_ref[...]
        xc = _conv_causal(x, prev_x[...], cw_ref, 4) + vec_ref[0:1, :]
        prev_x[...] = x[tm - 8:, :]
        _, ig, a, mult, _, _ = _rg_gates(xc, wa_ref, wx_ref, vec_ref)
        acum, h = _scan_fwd(a, mult * ig * xc)
        h = h + acum * prev_h[7:8, :]
        prev_h[...] = h[tm - 8:, :]
        h_ref[...] = h
        rec_ref[...] = (h * _gelu(gr_ref[...])).astype(MXU_DT)

    return pl.pallas_call(
        body, name="rg_fwd", grid=(s // tm,),
        in_specs=[_rows(tm, w, 0), _rows(tm, w, 1), _whole((8, w)), _whole((8, w)),
                  _whole((w, w)), _whole((w, w))],
        out_specs=[_rows(tm, w), _rows(tm, w)],
        out_shape=[jax.ShapeDtypeStruct((s, w), F32), jax.ShapeDtypeStruct((s, w), MXU_DT)],
        scratch_shapes=[pltpu.VMEM((8, w), F32), pltpu.VMEM((8, w), F32)],
        compiler_params=_params(("arbitrary",)),
    )(px, px, cw, vec, wa, wx)


def _dn_scalars(ab, arow, drow):
    lane = _iota(ab.shape, 1)
    g = jnp.where(lane < NV, -jnp.exp(arow) * _softplus(ab + drow), 0.0)
    beta = _sigmoid(ab)
    return lane, g, beta


def _l2n_heads(c, out_ref, off, scale):
    for hh in range(NQ):
        x = c[:, off + hh * HD: off + (hh + 1) * HD]
        r = lax.rsqrt(jnp.sum(x * x, axis=-1, keepdims=True) + L2_EPS)
        out_ref[:, hh * HD:(hh + 1) * HD] = x * (r * scale)


def _dn_prep(pqkv, pab, cw, arow, drow):
    s = pqkv.shape[0]
    tm = 128
    wq = NQ * HD

    def body(x_ref, ab_ref, cw_ref, a_ref, d_ref, q_ref, k_ref, v_ref, rt_ref, prev_x):
        @pl.when(pl.program_id(0) == 0)
        def _():
            prev_x[...] = jnp.zeros_like(prev_x)
        x = x_ref[...]
        cp = _conv_causal(x, prev_x[...], cw_ref, 4)
        prev_x[...] = x[tm - 8:, :]
        c = cp * _sigmoid(cp)
        _l2n_heads(c, q_ref, 0, Q_SCALE)
        _l2n_heads(c, k_ref, wq, 1.0)
        v_ref[...] = c[:, 2 * wq:]
        lane, g, beta = _dn_scalars(ab_ref[...], a_ref[...], d_ref[...])
        gc = _chunk_cumsum(g, 0)
        gl = gc + _chunk_cumsum(g, 0, rev=True) - g
        pack = jnp.where(lane < NV, gc, jnp.where(lane < 2 * NV, beta,
                         jnp.where(lane < 3 * NV, pltpu.roll(gl, 2 * NV, 1), 0.0)))
        rt_ref[...] = pack.T[0:3 * NV, :]

    return pl.pallas_call(
        body, name="dn_prep", grid=(s // tm,),
        in_specs=[_rows(tm, 4 * wq), _rows(tm, 128), _whole((8, 4 * wq)), _whole((1, 128)), _whole((1, 128))],
        out_specs=[_rows(tm, wq), _rows(tm, wq), _rows(tm, 2 * wq),
                   pl.BlockSpec((3 * NV, tm), lambda i: (0, i))],
        out_shape=[jax.ShapeDtypeStruct((s, wq), F32), jax.ShapeDtypeStruct((s, wq), F32),
                   jax.ShapeDtypeStruct((s, 2 * wq), F32), jax.ShapeDtypeStruct((3 * NV, s), F32)],
        scratch_shapes=[pltpu.VMEM((8, 4 * wq), F32)],
        compiler_params=_params(("arbitrary",)),
    )(pqkv, pab, cw, arow, drow)


def _pair_masks():
    i = _iota((2 * CH, 2 * CH), 0)
    j = _iota((2 * CH, 2 * CH), 1)
    same = (i >> 6) == (j >> 6)
    return same & (i >= j), same & (i > j)


def _head_cols(rt_ref, h):
    shp = (2 * CH, 2 * CH)
    g_r = jnp.broadcast_to(rt_ref[pl.ds(h, 1), :], shp)
    b_r = jnp.broadcast_to(rt_ref[pl.ds(NV + h, 1), :], shp)
    l_r = jnp.broadcast_to(rt_ref[pl.ds(2 * NV + h, 1), :], shp)
    return g_r, g_r.T, b_r.T, l_r, l_r.T


def _inv_unit_lower(a):
    return _inv_unit_lower_many([a])[0]


def _inv_unit_lower_many(a_list):
    n = a_list[0].shape[0]
    eye = (_iota((n, n), 0) == _iota((n, n), 1)).astype(F32)
    bs = [-a for a in a_list]
    xs = [eye + b for b in bs]
    for _ in range(5):
        bs = [_dot(b, b) for b in bs]
        xs = [x + _dot(x, b) for x, b in zip(xs, bs)]
    rs = [(eye - x) - _dot3(a, x) for a, x in zip(a_list, xs)]
    return [x + _dot(x, r) for x, r in zip(xs, rs)]


def _gam_rows(l_r):
    lrow = l_r[0:1, :]
    lane = _iota(lrow.shape, 1)
    other = pltpu.roll(lrow, CH, 1)
    return jnp.exp(jnp.where(lane < CH, lrow, other)), jnp.exp(jnp.where(lane >= CH, lrow, other))


def _dn_intra(q, k, v, rt):
    s = q.shape[0]
    nb = s // (2 * CH)
    qps = 2
    blk = pl.BlockSpec((2 * CH, qps * HD), lambda i, h: (i, h))
    blk2 = pl.BlockSpec((2 * CH, 2 * qps * HD), lambda i, h: (i, h))

    def body(q_ref, k_ref, v_ref, rt_ref, u_ref, w_ref, qd_ref, kd_ref, p_ref, ti_ref, gam_ref):
        hstep = pl.program_id(1)
        mc, ms = _pair_masks()
        gam_ref[...] = jnp.zeros_like(gam_ref)
        heads = []
        for qh in range(qps):
            qq = q_ref[:, qh * HD:(qh + 1) * HD]
            kk_ = k_ref[:, qh * HD:(qh + 1) * HD]
            kk = _dot_nt(kk_, kk_)
            qk = _dot_nt(qq, kk_)
            for j in range(2):
                idx = 2 * qh + j
                cs = slice(idx * HD, (idx + 1) * HD)
                g_r, g_c, b_c, l_r, l_c = _head_cols(rt_ref, 2 * qps * hstep + idx)
                dec = jnp.where(mc, jnp.exp(jnp.where(mc, g_c - g_r, 0.0)), 0.0)
                eg = jnp.exp(g_c)
                p_ref[:, cs] = jnp.where(mc, qk * dec, 0.0)
                qd_ref[:, cs] = eg * qq
                kd_ref[:, cs] = jnp.exp(l_c - g_c) * kk_
                ga, gb = _gam_rows(l_r)
                gam_ref[0, qh, 2 * j:2 * j + 1, :] = ga
                gam_ref[0, qh, 2 * j + 1:2 * j + 2, :] = gb
                rhs = jnp.concatenate([b_c * v_ref[:, cs], b_c * eg * kk_], axis=1)
                heads.append((cs, jnp.where(ms, b_c * kk * dec, 0.0), rhs))
        tinvs = _inv_unit_lower_many([a for _, a, _ in heads])
        uws = [_dot3(t, rhs) for t, (_, _, rhs) in zip(tinvs, heads)]
        for t, uw, (cs, _, _) in zip(tinvs, uws, heads):
            ti_ref[:, cs] = t
            u_ref[:, cs] = uw[:, :HD]
            w_ref[:, cs] = uw[:, HD:]

    big = jax.ShapeDtypeStruct((s, NV * HD), F32)
    return pl.pallas_call(
        body, name="dn_intra", grid=(nb, NQ // qps),
        in_specs=[blk, blk, blk2, pl.BlockSpec((3 * NV, 2 * CH), lambda i, h: (0, i))],
        out_specs=[blk2] * 6 + [pl.BlockSpec((1, qps, 8, 128), lambda i, h: (i, h, 0, 0))],
        out_shape=[big] * 6 + [jax.ShapeDtypeStruct((nb, NQ, 8, 128), F32)],
        compiler_params=_params(("parallel", "parallel")),
    )(q, k, v, rt)


def _dn_seq(u, w, qd, kd, p, gam):
    s = u.shape[0]
    nb = s // (2 * CH)
    wide = pl.BlockSpec((2 * CH, NV * HD), lambda i: (i, 0))

    def body(u_ref, w_ref, qd_ref, kd_ref, p_ref, gam_ref, o_ref, vn_ref, ss_ref, st):
        @pl.when(pl.program_id(0) == 0)
        def _():
            st[...] = jnp.zeros_like(st)
        ra, rb = slice(0, CH), slice(CH, 2 * CH)
        for g0 in range(0, NV, SEQ_GROUP):
            hs = list(range(g0, g0 + SEQ_GROUP))
            cs = [slice(h * HD, (h + 1) * HD) for h in hs]
            ga = [gam_ref[0, h // 2, 2 * (h % 2):2 * (h % 2) + 1, :] for h in hs]
            gb = [gam_ref[0, h // 2, 2 * (h % 2) + 1:2 * (h % 2) + 2, :] for h in hs]
            s0 = [st[h] for h in hs]
            vna = [u_ref[ra, c] - _dot(w_ref[ra, c], s) for c, s in zip(cs, s0)]
            s1 = [g * s + _dot_tn(kd_ref[ra, c], v) for g, s, c, v in zip(ga, s0, cs, vna)]
            vnb = [u_ref[rb, c] - _dot(w_ref[rb, c], s) for c, s in zip(cs, s1)]
            s2 = [g * s + _dot_tn(kd_ref[rb, c], v) for g, s, c, v in zip(gb, s1, cs, vnb)]
            for h, s in zip(hs, s2):
                st[h] = s
            oa = [_dot(qd_ref[ra, c], s) for c, s in zip(cs, s0)]
            ob = [_dot(qd_ref[rb, c], s) for c, s in zip(cs, s1)]
            for i_, h in enumerate(hs):
                vn = jnp.concatenate([vna[i_], vnb[i_]], axis=0)
                o_ref[:, cs[i_]] = jnp.concatenate([oa[i_], ob[i_]], axis=0) + _dot(p_ref[:, cs[i_]], vn)
                vn_ref[:, cs[i_]] = vn
                ss_ref[h, 0:HD, :] = s0[i_]
                ss_ref[h, HD:2 * HD, :] = s1[i_]

    big = jax.ShapeDtypeStruct((s, NV * HD), F32)
    return pl.pallas_call(
        body, name="dn_seq", grid=(nb,),
        in_specs=[wide] * 5 + [pl.BlockSpec((1, NQ, 8, 128), lambda i: (i, 0, 0, 0))],
        out_specs=[wide, wide, pl.BlockSpec((NV, 2 * HD, HD), lambda i: (0, i, 0))],
        out_shape=[big, big, jax.ShapeDtypeStruct((NV, 2 * s, HD), F32)],
        scratch_shapes=[pltpu.VMEM((NV, HD, HD), F32)],
        compiler_params=_params(("arbitrary",)),
    )(u, w, qd, kd, p, gam)


def _dn_post(o, pz, nw):
    s = o.shape[0]
    tm = _pick(s, (256, 128))

    def body(o_ref, z_ref, nw_ref, y_ref):
        for h in range(NV):
            cs = slice(h * HD, (h + 1) * HD)
            oh = o_ref[:, cs]
            z = z_ref[:, cs]
            rs = lax.rsqrt(jnp.mean(oh * oh, axis=-1, keepdims=True) + RMS_EPS)
            y_ref[:, cs] = (oh * rs * nw_ref[...] * (z * _sigmoid(z))).astype(MXU_DT)

    return pl.pallas_call(
        body, name="dn_post", grid=(s // tm,),
        in_specs=[_rows(tm, NV * HD), _rows(tm, NV * HD), _whole((1, HD))],
        out_specs=_rows(tm, NV * HD),
        out_shape=jax.ShapeDtypeStruct((s, NV * HD), MXU_DT), compiler_params=_params(("parallel",)),
    )(o, pz, nw)


def _merge(pg, ya, yb):
    s = ya.shape[0]
    tm = _pick(s, (512, 256, 128))

    def body(ga_ref, gb_ref, ya_ref, yb_ref, o_ref):
        o_ref[...] = (_sigmoid(ga_ref[...]) * ya_ref[...] + _sigmoid(gb_ref[...]) * yb_ref[...]).astype(MXU_DT)

    return pl.pallas_call(
        body, name="merge", grid=(s // tm,),
        in_specs=[_rows(tm, D, 0), _rows(tm, D, 1), _rows(tm, D), _rows(tm, D)],
        out_specs=_rows(tm, D),
        out_shape=jax.ShapeDtypeStruct((s, D), MXU_DT), compiler_params=_params(("parallel",)),
    )(pg, pg, ya, yb)


def _ln1(x, mix, adam, lng, lnb):
    s = x.shape[0]
    tm = _pick(s, (512, 256, 128))

    def body(x_ref, m_ref, ada_ref, g_ref, b_ref, r_ref, x2_ref, h2_ref):
        r = ALPHA * x_ref[...] + (1.0 + ada_ref[2:3, :]) * m_ref[...]
        xhat, _ = _ln_stats(r)
        x2 = xhat * g_ref[...] + b_ref[...]
        r_ref[...] = r
        x2_ref[...] = x2
        h2_ref[...] = (x2 * (1.0 + ada_ref[4:5, :]) + ada_ref[3:4, :]).astype(MXU_DT)

    return pl.pallas_call(
        body, name="ln1", grid=(s // tm,),
        in_specs=[_rows(tm, D), _rows(tm, D), _whole((8, D)), _whole((1, D)), _whole((1, D))],
        out_specs=[_rows(tm, D)] * 3,
        out_shape=[jax.ShapeDtypeStruct((s, D), F32), jax.ShapeDtypeStruct((s, D), F32),
                   jax.ShapeDtypeStruct((s, D), MXU_DT)],
        compiler_params=_params(("parallel",)),
    )(x, mix, adam, lng, lnb)


def _ffn_act(gu, cw, cb):
    s = gu.shape[0]
    tm = _pick(s, (256, 128))
    w = D_FF

    def body(g_ref, u_ref, cw_ref, cb_ref, o_ref, prev):
        @pl.when(pl.program_id(0) == 0)
        def _():
            prev[...] = jnp.zeros_like(prev)
        g = g_ref[...]
        gc = _conv_causal(g, prev[...], cw_ref, 3) + cb_ref[...]
        prev[...] = g[tm - 8:, :]
        o_ref[...] = (_gelu(gc) * u_ref[...]).astype(MXU_DT)

    return pl.pallas_call(
        body, name="ffn_act", grid=(s // tm,),
        in_specs=[_rows(tm, w, 0), _rows(tm, w, 1), _whole((8, w)), _whole((1, w))],
        out_specs=_rows(tm, w),
        out_shape=jax.ShapeDtypeStruct((s, w), MXU_DT),
        scratch_shapes=[pltpu.VMEM((8, w), F32)],
        compiler_params=_params(("arbitrary",)),
    )(gu, gu, cw, cb)


def _ln2_loss(x2, ff, tgt, adam, lng, lnb):
    s = x2.shape[0]
    tm = _pick(s, (512, 256, 128))

    def body(x_ref, f_ref, t_ref, ada_ref, g_ref, b_ref, dff_ref, dx_ref, red_ref):
        @pl.when(pl.program_id(0) == 0)
        def _():
            red_ref[...] = jnp.zeros_like(red_ref)
        ff_ = f_ref[...]
        r = ALPHA * x_ref[...] + (1.0 + ada_ref[5:6, :]) * ff_
        xhat, rstd = _ln_stats(r)
        err = xhat * g_ref[...] + b_ref[...] - t_ref[...]
        dy = err * (1.0 / D)
        dr = _ln_bwd(dy, xhat, rstd, g_ref[...])
        dff_ref[...] = ((1.0 + ada_ref[5:6, :]) * dr).astype(MXU_DT)
        dx_ref[...] = ALPHA * dr
        red_ref[0:1, :] += _rsum(dy * xhat)
        red_ref[1:2, :] += _rsum(dy)
        red_ref[2:3, :] += _rsum(dr * ff_)
        red_ref[3:4, :] += jnp.sum(_rsum(err * err), axis=1, keepdims=True) * (0.5 / D)

    return pl.pallas_call(
        body, name="ln2_loss", grid=(s // tm,),
        in_specs=[_rows(tm, D)] * 3 + [_whole((8, D)), _whole((1, D)), _whole((1, D))],
        out_specs=[_rows(tm, D), _rows(tm, D), _whole((8, D))],
        out_shape=[jax.ShapeDtypeStruct((s, D), MXU_DT), jax.ShapeDtypeStruct((s, D), F32),
                   jax.ShapeDtypeStruct((8, D), F32)],
        compiler_params=_params(("arbitrary",)),
    )(x2, ff, tgt, adam, lng, lnb)


def _ffn_bwd(dact, gu, cw, cb):
    s = dact.shape[0]
    tm = _pick(s, (256, 128))
    nt = s // tm
    w = D_FF

    def body(da_ref, g_ref, gb_ref, u_ref, cw_ref, cb_ref, o_ref, gcw_ref, gcb_ref, nxt):
        i = pl.program_id(0)

        @pl.when(i == 0)
        def _():
            nxt[...] = jnp.zeros_like(nxt)
            gcw_ref[...] = jnp.zeros_like(gcw_ref)
            gcb_ref[...] = jnp.zeros_like(gcb_ref)
        g = g_ref[...]
        before = jnp.where(i < nt - 1, gb_ref[...], 0.0)
        taps = _taps(g, before, 3)
        gc = _conv_taps(taps, cw_ref) + cb_ref[...]
        gel, dgel = _gelu_and_grad(gc)
        da = da_ref[...]
        dgc = da * u_ref[...] * dgel
        o_ref[:, w:] = (da * gel).astype(MXU_DT)
        o_ref[:, :w] = _conv_causal_bwd(dgc, nxt[...], cw_ref, 3).astype(MXU_DT)
        nxt[...] = dgc[:8, :]
        for k in range(3):
            gcw_ref[k:k + 1, :] += _rsum(dgc * taps[k])
        gcb_ref[...] += _rsum(dgc)

    return pl.pallas_call(
        body, name="ffn_bwd", grid=(nt,),
        in_specs=[_rows(tm, w, 0, nt), _rows(tm, w, 0, nt), _before(tm, w, 0, nt), _rows(tm, w, 1, nt),
                  _whole((8, w)), _whole((1, w))],
        out_specs=[_rows(tm, 2 * w, 0, nt), _whole((8, w)), _whole((1, w))],
        out_shape=[jax.ShapeDtypeStruct((s, 2 * w), MXU_DT), jax.ShapeDtypeStruct((8, w), F32),
                   jax.ShapeDtypeStruct((1, w), F32)],
        scratch_shapes=[pltpu.VMEM((8, w), F32)],
        compiler_params=_params(("arbitrary",)),
    )(dact, gu, gu, gu, cw, cb)


def _ln1_bwd(dh2, dx2a, x2, r1, mix, adam, lng):
    s = dh2.shape[0]
    tm = _pick(s, (512, 256, 128))

    def body(dh_ref, dxa_ref, x2_ref, r_ref, m_ref, ada_ref, g_ref, dm_ref, dx_ref, red_ref):
        @pl.when(pl.program_id(0) == 0)
        def _():
            red_ref[...] = jnp.zeros_like(red_ref)
        dh = dh_ref[...]
        dx2 = dxa_ref[...] + dh * (1.0 + ada_ref[4:5, :])
        xhat, rstd = _ln_stats(r_ref[...])
        dr = _ln_bwd(dx2, xhat, rstd, g_ref[...])
        dm_ref[...] = ((1.0 + ada_ref[2:3, :]) * dr).astype(MXU_DT)
        dx_ref[...] = ALPHA * dr
        red_ref[0:1, :] += _rsum(dh * x2_ref[...])
        red_ref[1:2, :] += _rsum(dh)
        red_ref[2:3, :] += _rsum(dx2 * xhat)
        red_ref[3:4, :] += _rsum(dx2)
        red_ref[4:5, :] += _rsum(dr * m_ref[...])

    return pl.pallas_call(
        body, name="ln1_bwd", grid=(s // tm,),
        in_specs=[_rows(tm, D)] * 5 + [_whole((8, D)), _whole((1, D))],
        out_specs=[_rows(tm, D), _rows(tm, D), _whole((8, D))],
        out_shape=[jax.ShapeDtypeStruct((s, D), MXU_DT), jax.ShapeDtypeStruct((s, D), F32),
                   jax.ShapeDtypeStruct((8, D), F32)],
        compiler_params=_params(("arbitrary",)),
    )(dh2, dx2a, x2, r1, mix, adam, lng)


def _merge_bwd(dmg, pg, ya, yb):
    s = dmg.shape[0]
    tm = _pick(s, (512, 256, 128))

    def body(d_ref, ga_ref, gb_ref, ya_ref, yb_ref, dya_ref, dyb_ref, dpg_ref):
        d = d_ref[...]
        sa = _sigmoid(ga_ref[...])
        sb = _sigmoid(gb_ref[...])
        dya_ref[...] = (d * sa).astype(MXU_DT)
        dyb_ref[...] = (d * sb).astype(MXU_DT)
        dpg_ref[:, :D] = (d * ya_ref[...] * sa * (1.0 - sa)).astype(MXU_DT)
        dpg_ref[:, D:] = (d * yb_ref[...] * sb * (1.0 - sb)).astype(MXU_DT)

    return pl.pallas_call(
        body, name="merge_bwd", grid=(s // tm,),
        in_specs=[_rows(tm, D), _rows(tm, D, 0), _rows(tm, D, 1), _rows(tm, D), _rows(tm, D)],
        out_specs=[_rows(tm, D), _rows(tm, D), _rows(tm, 2 * D)],
        out_shape=[jax.ShapeDtypeStruct((s, D), MXU_DT), jax.ShapeDtypeStruct((s, D), MXU_DT),
                   jax.ShapeDtypeStruct((s, 2 * D), MXU_DT)],
        compiler_params=_params(("parallel",)),
    )(dmg, pg, pg, ya, yb)


def _dn_post_bwd(ddn, o, pz, nw):
    s = o.shape[0]
    tm = _pick(s, (256, 128))

    def body(d_ref, o_ref, z_ref, nw_ref, do_ref, dz_ref, gnw_ref):
        @pl.when(pl.program_id(0) == 0)
        def _():
            gnw_ref[...] = jnp.zeros_like(gnw_ref)
        acc = jnp.zeros((1, HD), F32)
        for h in range(NV):
            cs = slice(h * HD, (h + 1) * HD)
            oh = o_ref[:, cs]
            z = z_ref[:, cs]
            d = d_ref[:, cs]
            sg = _sigmoid(z)
            rs = lax.rsqrt(jnp.mean(oh * oh, axis=-1, keepdims=True) + RMS_EPS)
            n = oh * rs
            dz_ref[:, cs] = (d * n * nw_ref[...] * sg * (1.0 + z * (1.0 - sg))).astype(MXU_DT)
            dn_ = d * (z * sg)
            acc = acc + _rsum(dn_ * n)
            dnn = dn_ * nw_ref[...]
            do_ref[:, cs] = rs * (dnn - n * jnp.mean(dnn * n, axis=-1, keepdims=True))
        gnw_ref[...] += acc

    return pl.pallas_call(
        body, name="dn_post_bwd", grid=(s // tm,),
        in_specs=[_rows(tm, NV * HD)] * 3 + [_whole((1, HD))],
        out_specs=[_rows(tm, NV * HD), _rows(tm, NV * HD), _whole((1, HD))],
        out_shape=[jax.ShapeDtypeStruct((s, NV * HD), F32), jax.ShapeDtypeStruct((s, NV * HD), MXU_DT),
                   jax.ShapeDtypeStruct((1, HD), F32)],
        compiler_params=_params(("arbitrary",)),
    )(ddn, o, pz, nw)


def _dn_seq_bwd(do, qd, kd, p, w, vn, ssave, gam):
    s = do.shape[0]
    nb = s // (2 * CH)
    wide = pl.BlockSpec((2 * CH, NV * HD), lambda i: (nb - 1 - i, 0))
    gspec = pl.BlockSpec((1, NQ, 8, 128), lambda i: (nb - 1 - i, 0, 0, 0))

    def body(do_ref, qd_ref, kd_ref, p_ref, w_ref, vn_ref, ss_ref, gam_ref, dvn_ref, dkd_ref, dgam_ref, dst):
        @pl.when(pl.program_id(0) == 0)
        def _():
            dst[...] = jnp.zeros_like(dst)
        dgam_ref[...] = jnp.zeros_like(dgam_ref)
        ra, rb = slice(0, CH), slice(CH, 2 * CH)
        tot = lambda t: jnp.sum(jnp.sum(t, axis=1, keepdims=True), axis=0, keepdims=True)
        for g0 in range(0, NV, SEQ_GROUP):
            hs = list(range(g0, g0 + SEQ_GROUP))
            cs = [slice(h * HD, (h + 1) * HD) for h in hs]
            ga = [gam_ref[0, h // 2, 2 * (h % 2):2 * (h % 2) + 1, :] for h in hs]
            gb = [gam_ref[0, h // 2, 2 * (h % 2) + 1:2 * (h % 2) + 2, :] for h in hs]
            ds2 = [dst[h] for h in hs]
            pdo = [_dot_tn(p_ref[:, c], do_ref[:, c]) for c in cs]
            qdo_b = [_dot_tn(qd_ref[rb, c], do_ref[rb, c]) for c in cs]
            qdo_a = [_dot_tn(qd_ref[ra, c], do_ref[ra, c]) for c in cs]
            dvb = [p_[rb] + _dot(kd_ref[rb, c], d_) for p_, c, d_ in zip(pdo, cs, ds2)]
            ds1 = [g * d_ + q_ - _dot_tn(w_ref[rb, c], v_)
                   for g, d_, q_, c, v_ in zip(gb, ds2, qdo_b, cs, dvb)]
            dva = [p_[ra] + _dot(kd_ref[ra, c], d_) for p_, c, d_ in zip(pdo, cs, ds1)]
            ds0 = [g * d_ + q_ - _dot_tn(w_ref[ra, c], v_)
                   for g, d_, q_, c, v_ in zip(ga, ds1, qdo_a, cs, dva)]
            for h, d_ in zip(hs, ds0):
                dst[h] = d_
            for i_, h in enumerate(hs):
                c = cs[i_]
                row = 2 * (h % 2)
                dkd_ref[rb, c] = _dot_nt(vn_ref[rb, c], ds2[i_])
                dkd_ref[ra, c] = _dot_nt(vn_ref[ra, c], ds1[i_])
                dvn_ref[ra, c] = dva[i_]
                dvn_ref[rb, c] = dvb[i_]
                dgam_ref[0, h // 2, row:row + 1, :] = jnp.broadcast_to(tot(ds1[i_] * ss_ref[h, 0:HD, :]), (1, 128))
                dgam_ref[0, h // 2, row + 1:row + 2, :] = jnp.broadcast_to(
                    tot(ds2[i_] * ss_ref[h, HD:2 * HD, :]), (1, 128))

    big = jax.ShapeDtypeStruct((s, NV * HD), F32)
    return pl.pallas_call(
        body, name="dn_seq_bwd", grid=(nb,),
        in_specs=[wide] * 6 + [pl.BlockSpec((NV, 2 * HD, HD), lambda i: (0, nb - 1 - i, 0)), gspec],
        out_specs=[wide, wide, gspec],
        out_shape=[big, big, jax.ShapeDtypeStruct((nb, NQ, 8, 128), F32)],
        scratch_shapes=[pltpu.VMEM((NV, HD, HD), F32)],
        compiler_params=_params(("arbitrary",)),
    )(do, qd, kd, p, w, vn, ssave, gam)


def _dn_intra_bwd(q, k, v, rt, do, dvn, dkd, gam, dgam, ssave, tinv, u, w, vn):
    s = q.shape[0]
    nb = s // (2 * CH)
    blk = pl.BlockSpec((2 * CH, HD), lambda i, h: (i, h))
    blk2 = pl.BlockSpec((2 * CH, 2 * HD), lambda i, h: (i, h))
    gspec = pl.BlockSpec((1, 1, 8, 128), lambda i, h: (i, h, 0, 0))
    rspec = pl.BlockSpec((3 * NV, 2 * CH), lambda i, h: (0, i))

    def body(q_ref, k_ref, v_ref, rt_ref, do_ref, dvn_ref, dkd_ref, gam_ref, dgam_ref, ss_ref,
             ti_ref, u_ref, w_ref, vn_ref, dq_ref, dk_ref, dv_ref, drt_ref, acc):
        hq = pl.program_id(1)

        @pl.when(hq == 0)
        def _():
            acc[...] = jnp.zeros_like(acc)
        qq = q_ref[...]
        kk_ = k_ref[...]
        kk = _dot_nt(kk_, kk_)
        qk = _dot_nt(qq, kk_)
        mc, ms = _pair_masks()
        ra, rb = slice(0, CH), slice(CH, 2 * CH)
        lane = _iota((1, 2 * CH), 1)
        js = (0, 1)
        cs = [slice(j * HD, (j + 1) * HD) for j in js]
        cols = [_head_cols(rt_ref, 2 * hq + j) for j in js]
        g_c = [c_[1] for c_ in cols]
        b_c = [c_[2] for c_ in cols]
        dec = [jnp.where(mc, jnp.exp(jnp.where(mc, c_[1] - c_[0], 0.0)), 0.0) for c_ in cols]
        eg = [jnp.exp(g) for g in g_c]
        egl = [jnp.exp(c_[4] - c_[1]) for c_ in cols]
        dob = [do_ref[:, c] for c in cs]
        dvb = [dvn_ref[:, c] for c in cs]
        dqd = [jnp.concatenate([_dot_nt(d_[ra], ss_ref[j, 0:HD, :]), _dot_nt(d_[rb], ss_ref[j, HD:2 * HD, :])], axis=0)
               for j, d_ in zip(js, dob)]
        dw = [-jnp.concatenate([_dot_nt(d_[ra], ss_ref[j, 0:HD, :]), _dot_nt(d_[rb], ss_ref[j, HD:2 * HD, :])], axis=0)
              for j, d_ in zip(js, dvb)]
        dp = [jnp.where(mc, _dot_nt(d_, vn_ref[:, c]), 0.0) for d_, c in zip(dob, cs)]
        dbuw = [_dot3_tn(ti_ref[:, c], jnp.concatenate([d_, w_], axis=1)) for c, d_, w_ in zip(cs, dvb, dw)]
        dbu = [t[:, :HD] for t in dbuw]
        dbw = [t[:, HD:] for t in dbuw]
        da = [jnp.where(ms, -(_dot_nt(bu, u_ref[:, c]) + _dot_nt(bw, w_ref[:, c])), 0.0)
              for bu, bw, c in zip(dbu, dbw, cs)]
        dm = [a_ * d_ for a_, d_ in zip(da, dec)]
        dn_ = [p_ * d_ for p_, d_ in zip(dp, dec)]
        dbk = [_dot(m_, kk_) for m_ in dm]
        dqs = [_dot(n_, kk_) + e_ * q_ for n_, e_, q_ in zip(dn_, eg, dqd)]
        dks = [_dot_tn(m_, b_ * kk_) + _dot_tn(n_, qq) + el * dkd_ref[:, c] + b_ * (e_ * bw + bk)
               for m_, b_, n_, el, c, e_, bw, bk in zip(dm, b_c, dn_, egl, cs, eg, dbw, dbk)]
        dq_ref[...] = dqs[0] + dqs[1]
        dk_ref[...] = dks[0] + dks[1]
        for j in js:
            c = cs[j]
            dv_ref[:, c] = b_c[j] * dbu[j]
            e = da[j] * (b_c[j] * kk * dec[j]) + dp[j] * (qk * dec[j])
            x = dkd_ref[:, c] * (egl[j] * kk_)
            egk = eg[j] * kk_
            z = e + dqd[j] * (eg[j] * qq) - x + dbw[j] * (b_c[j] * egk)
            zb = dbw[j] * egk + dbu[j] * v_ref[:, c] + dbk[j] * kk_
            sa = jnp.sum(jnp.sum(x[ra], axis=1, keepdims=True), axis=0, keepdims=True)
            sb = jnp.sum(jnp.sum(x[rb], axis=1, keepdims=True), axis=0, keepdims=True)
            la = sa + dgam_ref[0, 0, 2 * j:2 * j + 1, :] * gam_ref[0, 0, 2 * j:2 * j + 1, :]
            lb = sb + dgam_ref[0, 0, 2 * j + 1:2 * j + 2, :] * gam_ref[0, 0, 2 * j + 1:2 * j + 2, :]
            h = 2 * hq + j
            acc[pl.ds(h, 1), :] = _rsum(z.T - e)
            acc[pl.ds(NV + h, 1), :] = _rsum(zb.T)
            acc[pl.ds(2 * NV + h, 1), :] = jnp.where(lane < CH, la, lb)

        @pl.when(hq == NQ - 1)
        def _():
            drt_ref[...] = acc[...]

    return pl.pallas_call(
        body, name="dn_intra_bwd", grid=(nb, NQ),
        in_specs=[blk, blk, blk2, rspec, blk2, blk2, blk2, gspec, gspec,
                  pl.BlockSpec((2, 2 * HD, HD), lambda i, h: (h, i, 0)), blk2, blk2, blk2, blk2],
        out_specs=[blk, blk, blk2, rspec],
        out_shape=[jax.ShapeDtypeStruct((s, NQ * HD), F32), jax.ShapeDtypeStruct((s, NQ * HD), F32),
                   jax.ShapeDtypeStruct((s, NV * HD), F32), jax.ShapeDtypeStruct((3 * NV, s), F32)],
        scratch_shapes=[pltpu.VMEM((3 * NV, 2 * CH), F32)],
        compiler_params=_params(("parallel", "arbitrary")),
    )(q, k, v, rt, do, dvn, dkd, gam, dgam, ssave, tinv, u, w, vn)


def _l2n_heads_bwd(c, d_ref, dc_ref, off, scale):
    for hh in range(NQ):
        cs = slice(off + hh * HD, off + (hh + 1) * HD)
        x = c[:, cs]
        dy = d_ref[:, hh * HD:(hh + 1) * HD]
        r = lax.rsqrt(jnp.sum(x * x, axis=-1, keepdims=True) + L2_EPS)
        dc_ref[:, cs] = (scale * r) * (dy - x * (r * r) * jnp.sum(dy * x, axis=-1, keepdims=True))


def _dn_prep_bwd(dq, dk, dv, drt, pqkv, pab, cw, arow, drow, acol, dcol):
    s = pqkv.shape[0]
    tm = 128
    nt = s // tm
    wq = NQ * HD

    def body(dq_ref, dk_ref, dv_ref, drt_ref, x_ref, xb_ref, ab_ref, cw_ref, ar_ref, dr_ref, ac_ref,
             dc_ref, dx_ref, dab_ref, gcw_ref, gsc_ref, dcs, nxt):
        i = pl.program_id(0)

        @pl.when(i == 0)
        def _():
            nxt[...] = jnp.zeros_like(nxt)
            gcw_ref[...] = jnp.zeros_like(gcw_ref)
            gsc_ref[...] = jnp.zeros_like(gsc_ref)
        x = x_ref[...]
        before = jnp.where(i < nt - 1, xb_ref[...], 0.0)
        taps = _taps(x, before, 4)
        cp = _conv_taps(taps, cw_ref)
        sg = _sigmoid(cp)
        c = cp * sg
        _l2n_heads_bwd(c, dq_ref, dcs, 0, Q_SCALE)
        _l2n_heads_bwd(c, dk_ref, dcs, wq, 1.0)
        dcs[:, 2 * wq:] = dv_ref[...]
        dcp = dcs[...] * (sg * (1.0 + cp * (1.0 - sg)))
        dx_ref[...] = _conv_causal_bwd(dcp, nxt[...], cw_ref, 4).astype(MXU_DT)
        nxt[...] = dcp[:8, :]
        for kq in range(4):
            gcw_ref[kq:kq + 1, :] += _rsum(dcp * taps[kq])
        lane = _iota((NV, tm), 1)
        dgt = drt_ref[0:NV, :] + jnp.where((lane & (CH - 1)) == CH - 1, drt_ref[2 * NV:3 * NV, :], 0.0)
        dg = _chunk_cumsum(dgt, 1, rev=True)
        abt = ab_ref[...].T
        zt = abt[0:NV, :] + dc_ref[...]
        gt = -jnp.exp(ac_ref[...]) * _softplus(zt)
        dat = dg * (-jnp.exp(ac_ref[...])) * _sigmoid(zt)
        bt = _sigmoid(abt[NV:2 * NV, :])
        dbt = drt_ref[NV:2 * NV, :] * bt * (1.0 - bt)
        full = jnp.concatenate([dat, dbt, jnp.zeros((128 - 2 * NV, tm), F32)], axis=0)
        dab_ref[...] = full.T.astype(MXU_DT)
        l2 = _iota((NV, 128), 1)
        gsc_ref[...] += jnp.where(l2 == 0, jnp.sum(dg * gt, axis=1, keepdims=True),
                                  jnp.where(l2 == 1, jnp.sum(dat, axis=1, keepdims=True), 0.0))

    return pl.pallas_call(
        body, name="dn_prep_bwd", grid=(nt,),
        in_specs=[_rows(tm, wq, 0, nt), _rows(tm, wq, 0, nt), _rows(tm, 2 * wq, 0, nt),
                  pl.BlockSpec((3 * NV, tm), lambda i: (0, nt - 1 - i)),
                  _rows(tm, 4 * wq, 0, nt), _before(tm, 4 * wq, 0, nt), _rows(tm, 128, 0, nt),
                  _whole((8, 4 * wq)), _whole((1, 128)), _whole((1, 128)), _whole((NV, 1)), _whole((NV, 1))],
        out_specs=[_rows(tm, 4 * wq, 0, nt), _rows(tm, 128, 0, nt), _whole((8, 4 * wq)), _whole((NV, 128))],
        out_shape=[jax.ShapeDtypeStruct((s, 4 * wq), MXU_DT), jax.ShapeDtypeStruct((s, 128), MXU_DT),
                   jax.ShapeDtypeStruct((8, 4 * wq), F32), jax.ShapeDtypeStruct((NV, 128), F32)],
        scratch_shapes=[pltpu.VMEM((tm, 4 * wq), F32), pltpu.VMEM((8, 4 * wq), F32)],
        compiler_params=_params(("arbitrary",)),
    )(dq, dk, dv, drt, pqkv, pqkv, pab, cw, arow, drow, acol, dcol)


def _rg_bwd(drec, px, h, cw, vec, wa, wx, wat, wxt):
    s = px.shape[0]
    tm = _pick(s, (256, 128))
    nt = s // tm
    w = D_RNN

    def body(dr_ref, xr_ref, xb_ref, gr_ref, h_ref, hb_ref, cw_ref, vec_ref, wa_ref, wx_ref, wat_ref, wxt_ref,
             o_ref, gwa_ref, gwx_ref, gcw_ref, gvec_ref, nxt_a, nxt_l, nxt_d):
        i = pl.program_id(0)

        @pl.when(i == 0)
        def _():
            nxt_a[...] = jnp.zeros_like(nxt_a)
            nxt_l[...] = jnp.zeros_like(nxt_l)
            nxt_d[...] = jnp.zeros_like(nxt_d)
            gwa_ref[...] = jnp.zeros_like(gwa_ref)
            gwx_ref[...] = jnp.zeros_like(gwx_ref)
            gcw_ref[...] = jnp.zeros_like(gcw_ref)
            gvec_ref[...] = jnp.zeros_like(gvec_ref)
        first = i == nt - 1
        x = xr_ref[...]
        xbefore = jnp.where(first, 0.0, xb_ref[...])
        hbefore = jnp.where(first, 0.0, hb_ref[...])
        taps = _taps(x, xbefore, 4)
        xc = _conv_taps(taps, cw_ref) + vec_ref[0:1, :]
        r, ig, a, mult, sp, rmult = _rg_gates(xc, wa_ref, wx_ref, vec_ref)
        hh = h_ref[...]
        gel, dgel = _gelu_and_grad(gr_ref[...])
        drec_ = dr_ref[...]
        o_ref[:, w:] = (drec_ * hh * dgel).astype(MXU_DT)
        acum, lam = _scan_rev(_shift_up(a, nxt_a[...], 1), drec_ * gel)
        lam = lam + acum * nxt_l[0:1, :]
        nxt_a[...] = a[:8, :]
        nxt_l[...] = lam[:8, :]
        da = lam * _shift_down(hh, hbefore, 1)
        dxc = lam * mult * ig
        dla = da * a - (lam * ig * xc) * (a * a) * rmult
        dpr = dla * (-RG_C * sp) * r * (1.0 - r)
        dpi = (lam * mult * xc) * ig * (1.0 - ig)
        dprb = dpr.astype(MXU_DT)
        dpib = dpi.astype(MXU_DT)
        dxc = dxc + jnp.dot(dprb, wat_ref[...], preferred_element_type=F32) \
                  + jnp.dot(dpib, wxt_ref[...], preferred_element_type=F32)
        xcb = xc.astype(MXU_DT)
        gwa_ref[...] += _dot_tn(xcb, dprb)
        gwx_ref[...] += _dot_tn(xcb, dpib)
        o_ref[:, :w] = _conv_causal_bwd(dxc, nxt_d[...], cw_ref, 4).astype(MXU_DT)
        nxt_d[...] = dxc[:8, :]
        for kq in range(4):
            gcw_ref[kq:kq + 1, :] += _rsum(dxc * taps[kq])
        gvec_ref[0:1, :] += _rsum(dxc)
        gvec_ref[1:2, :] += _rsum(dpr)
        gvec_ref[2:3, :] += _rsum(dpi)
        gvec_ref[3:4, :] += _rsum(dla * (-RG_C * r)) * (-_sigmoid(-vec_ref[3:4, :]))

    return pl.pallas_call(
        body, name="rg_bwd", grid=(nt,),
        in_specs=[_rows(tm, w, 0, nt), _rows(tm, w, 0, nt), _before(tm, w, 0, nt), _rows(tm, w, 1, nt),
                  _rows(tm, w, 0, nt), _before(tm, w, 0, nt), _whole((8, w)), _whole((8, w)),
                  _whole((w, w)), _whole((w, w)), _whole((w, w)), _whole((w, w))],
        out_specs=[_rows(tm, 2 * w, 0, nt), _whole((w, w)), _whole((w, w)), _whole((8, w)), _whole((8, w))],
        out_shape=[jax.ShapeDtypeStruct((s, 2 * w), MXU_DT), jax.ShapeDtypeStruct((w, w), F32),
                   jax.ShapeDtypeStruct((w, w), F32), jax.ShapeDtypeStruct((8, w), F32),
                   jax.ShapeDtypeStruct((8, w), F32)],
        scratch_shapes=[pltpu.VMEM((8, w), F32)] * 3,
        compiler_params=_params(("arbitrary",)),
    )(drec, px, px, px, h, h, cw, vec, wa, wx, wat, wxt)


def _modulate_bwd(dh1, dxa, x, adam):
    s = x.shape[0]
    tm = _pick(s, (512, 256, 128))

    def body(dh_ref, dxa_ref, x_ref, ada_ref, gx_ref, red_ref):
        @pl.when(pl.program_id(0) == 0)
        def _():
            red_ref[...] = jnp.zeros_like(red_ref)
        dh = dh_ref[...]
        gx_ref[...] = dxa_ref[...] + dh * (1.0 + ada_ref[1:2, :])
        red_ref[0:1, :] += _rsum(dh * x_ref[...])
        red_ref[1:2, :] += _rsum(dh)

    return pl.pallas_call(
        body, name="modulate1_bwd", grid=(s // tm,),
        in_specs=[_rows(tm, D)] * 3 + [_whole((8, D))],
        out_specs=[_rows(tm, D), _whole((8, D))],
        out_shape=[jax.ShapeDtypeStruct((s, D), F32), jax.ShapeDtypeStruct((8, D), F32)],
        compiler_params=_params(("arbitrary",)),
    )(dh1, dxa, x, adam)


def _adamw(parts, w, m, v, name):
    r, c = w.shape
    n_parts = len(parts)
    tm = _row_tile(r, c * 4 * (n_parts + 7))
    c1 = 1.0 - ADAM_B1 ** ADAM_STEP
    c2 = 1.0 - ADAM_B2 ** ADAM_STEP

    def body(*refs):
        g = refs[0][...]
        for p_ref in refs[1:n_parts]:
            g = g + p_ref[...]
        w_ref, m_ref, v_ref, g_out, d_out, m_out, v_out = refs[n_parts:]
        mn = ADAM_B1 * m_ref[...] + (1.0 - ADAM_B1) * g
        vn = ADAM_B2 * v_ref[...] + (1.0 - ADAM_B2) * (g * g)
        g_out[...] = g
        m_out[...] = mn
        v_out[...] = vn
        d_out[...] = -ADAM_LR * ((mn / c1) / (jnp.sqrt(vn / c2) + ADAM_EPS) + ADAM_WD * w_ref[...])

    spec = pl.BlockSpec((tm, c), lambda i: (i, 0))
    return pl.pallas_call(
        body, name=name, grid=(r // tm,),
        in_specs=[spec] * (n_parts + 3), out_specs=[spec] * 4,
        out_shape=[jax.ShapeDtypeStruct((r, c), F32)] * 4, compiler_params=_params(("parallel",)),
    )(*parts, w, m, v)


def _row_tile(r, bytes_per_row):
    for t in (512, 256, 128, 64, 32, 16):
        if r % t == 0 and 2 * t * bytes_per_row <= 20 * 1024 * 1024:
            return t
    return _pick(r, (16, 8))


def _sum_partials(own, recv, name):
    r, c = own.shape
    tm = _row_tile(r, c * (4 + 4 + 3 * 2))

    def body(o_ref, r_ref, out_ref):
        out_ref[...] = ((o_ref[...] + r_ref[0].astype(F32)) + r_ref[1].astype(F32)) + r_ref[2].astype(F32)

    return pl.pallas_call(
        body, name=name, grid=(r // tm,),
        in_specs=[pl.BlockSpec((tm, c), lambda i: (i, 0)), pl.BlockSpec((3, tm, c), lambda i: (0, i, 0))],
        out_specs=pl.BlockSpec((tm, c), lambda i: (i, 0)),
        out_shape=jax.ShapeDtypeStruct((r, c), F32), compiler_params=_params(("parallel",)),
    )(own, recv)


def _sum8(g):
    _, r, c = g.shape
    tm = _pick(r, (256, 128, 64, 32, 16, 8))

    def body(g_ref, o_ref):
        acc = g_ref[0]
        for k in range(1, 8):
            acc = acc + g_ref[k]
        o_ref[...] = acc

    return pl.pallas_call(
        body, name="sum8", grid=(r // tm,),
        in_specs=[pl.BlockSpec((8, tm, c), lambda i: (0, i, 0))],
        out_specs=pl.BlockSpec((tm, c), lambda i: (i, 0)),
        out_shape=jax.ShapeDtypeStruct((r, c), F32), compiler_params=_params(("parallel",)),
    )(g)


def _silu_rows(x):
    def body(x_ref, o_ref):
        xx = x_ref[...]
        o_ref[...] = xx * _sigmoid(xx)

    return pl.pallas_call(body, name="silu_rows", out_shape=jax.ShapeDtypeStruct(x.shape, F32))(x)


def _coords():
    return lax.axis_index("x"), lax.axis_index("y"), lax.axis_index("c")


def _allgather8(v, name):
    r, n = v.shape

    def body(v_ref, out_ref, send_sems, recv_sems):
        x, y, c = _coords()
        me = 4 * x + 2 * y + c
        out_ref[me] = v_ref[...]
        peers = []
        for k in range(1, 8):
            px = 1 - x if k & 4 else x
            py = 1 - y if k & 2 else y
            pc = 1 - c if k & 1 else c
            peers.append((px, py, pc))

        def copy(k, slot, to):
            return pltpu.make_async_remote_copy(
                src_ref=v_ref, dst_ref=out_ref.at[slot], send_sem=send_sems.at[k], recv_sem=recv_sems.at[k],
                device_id=to, device_id_type=MESH)

        sends = [copy(k, me, p) for k, p in enumerate(peers)]
        for cp in sends:
            cp.start()
        for k, (px, py, pc) in enumerate(peers):
            copy(k, 4 * px + 2 * py + pc, (px, py, pc)).wait_recv()
        for cp in sends:
            cp.wait_send()

    return pl.pallas_call(
        body, name=name, out_shape=jax.ShapeDtypeStruct((8, r, n), v.dtype),
        in_specs=[pl.BlockSpec(memory_space=pltpu.VMEM)], out_specs=pl.BlockSpec(memory_space=pltpu.VMEM),
        scratch_shapes=[pltpu.SemaphoreType.DMA((7,)), pltpu.SemaphoreType.DMA((7,))],
        compiler_params=pltpu.CompilerParams(vmem_limit_bytes=VMEM_LIMIT),
    )(v)


def _other_chips(x, y):
    return [(1 - x, y), (x, 1 - y), (1 - x, 1 - y)]


_HBM = pl.BlockSpec(memory_space=pl.ANY)


class _Exchange:
    def __init__(self, ins, out_shapes, sems, start, finish):
        self.ins, self.out_shapes, self.sems, self.start, self.finish = ins, out_shapes, sems, start, finish


def _run_exchange(ex, name):
    n_in, n_out = len(ex.ins), len(ex.out_shapes)

    def body(*refs):
        ins, outs, sems = refs[:n_in], refs[n_in:n_in + n_out], refs[n_in + n_out:]
        ex.start(ins, outs, sems)
        ex.finish(ins, outs, sems)

    return pl.pallas_call(body, name=name, out_shape=list(ex.out_shapes), in_specs=[_HBM] * n_in,
                          out_specs=[_HBM] * n_out, scratch_shapes=list(ex.sems))(*ex.ins)


def _gather_exchange(shards):
    n = len(shards)

    def plan(ins, outs, sems):
        send_sems, recv_sems, local_sems = sems
        x, y, c = _coords()
        s_me = 2 * x + y
        chips = _other_chips(x, y)

        def half(i, slot, hc):
            rh = shards[i].shape[0] // 2
            return outs[i].at[slot, pl.ds(pl.multiple_of(hc * rh, 16), rh), :]

        def copy(i, k, src, dst, to):
            return pltpu.make_async_remote_copy(src_ref=src, dst_ref=dst, send_sem=send_sems.at[6 * i + k],
                                                recv_sem=recv_sems.at[6 * i + k], device_id=to, device_id_type=MESH)

        local = [pltpu.make_async_copy(ins[i], outs[i].at[s_me], local_sems.at[i]) for i in range(n)]
        first = []
        for i in range(n):
            rh = shards[i].shape[0] // 2
            my_half = ins[i].at[pl.ds(pl.multiple_of(c * rh, 16), rh), :]
            first += [copy(i, j, my_half, half(i, s_me, c), (px, py, c)) for j, (px, py) in enumerate(chips)]
        return (x, y, c), chips, half, copy, local, first

    def start(ins, outs, sems):
        _, _, _, _, local, first = plan(ins, outs, sems)
        for cp in local + first:
            cp.start()

    def finish(ins, outs, sems):
        (x, y, c), chips, half, copy, local, first = plan(ins, outs, sems)
        sibling = (x, y, 1 - c)
        passed = []
        for i in range(n):
            for j, (px, py) in enumerate(chips):
                land = half(i, 2 * px + py, c)
                copy(i, j, land, land, (px, py, c)).wait_recv()
                fw = copy(i, 3 + j, land, land, sibling)
                fw.start()
                passed.append(fw)
        for i in range(n):
            for j, (px, py) in enumerate(chips):
                land = half(i, 2 * px + py, 1 - c)
                copy(i, 3 + j, land, land, sibling).wait_recv()
        for cp in first + passed:
            cp.wait_send()
        for cp in local:
            cp.wait()

    return _Exchange(list(shards), [jax.ShapeDtypeStruct((4,) + v.shape, v.dtype) for v in shards],
                     [pltpu.SemaphoreType.DMA((6 * n,)), pltpu.SemaphoreType.DMA((6 * n,)),
                      pltpu.SemaphoreType.DMA((n,))], start, finish)


def _scatter_exchange(gs):
    n = len(gs)

    def copies(ins, outs, sems):
        send_sems, recv_sems = sems
        x, y, c = _coords()
        return [pltpu.make_async_remote_copy(
                    src_ref=ins[i].at[2 * px + py], dst_ref=outs[i].at[j], send_sem=send_sems.at[3 * i + j],
                    recv_sem=recv_sems.at[3 * i + j], device_id=(px, py, c), device_id_type=MESH)
                for i in range(n) for j, (px, py) in enumerate(_other_chips(x, y))]

    def start(ins, outs, sems):
        for cp in copies(ins, outs, sems):
            cp.start()

    def finish(ins, outs, sems):
        cps = copies(ins, outs, sems)
        for cp in cps:
            cp.wait_recv()
        for cp in cps:
            cp.wait_send()

    return _Exchange(list(gs), [jax.ShapeDtypeStruct((3,) + g.shape[1:], g.dtype) for g in gs],
                     [pltpu.SemaphoreType.DMA((3 * n,)), pltpu.SemaphoreType.DMA((3 * n,))], start, finish)


def _swap_sibling(vs):
    n = len(vs)

    def body(*refs):
        ins, outs = refs[:n], refs[n:2 * n]
        send_sems, recv_sems = refs[2 * n:]
        x, y, c = _coords()
        cps = [pltpu.make_async_remote_copy(src_ref=ins[i], dst_ref=outs[i], send_sem=send_sems.at[i],
                                            recv_sem=recv_sems.at[i], device_id=(x, y, 1 - c), device_id_type=MESH)
               for i in range(n)]
        for cp in cps:
            cp.start()
        for cp in cps:
            cp.wait()

    return pl.pallas_call(
        body, name="swap_sibling", out_shape=[jax.ShapeDtypeStruct(v.shape, v.dtype) for v in vs],
        in_specs=[_HBM] * n, out_specs=[_HBM] * n,
        scratch_shapes=[pltpu.SemaphoreType.DMA((n,)), pltpu.SemaphoreType.DMA((n,))],
    )(*vs)


def _pad_rows(a, rows):
    return jnp.pad(a, ((0, rows - a.shape[0]), (0, 0)))


def _block_diag(w):
    eye = jnp.eye(RG_BLOCKS, dtype=w.dtype)
    return (eye[:, None, :, None] * w[:, :, None, :]).reshape(D_RNN, D_RNN)


def _diag_blocks(g):
    g4 = g.reshape(RG_BLOCKS, RG_BW, RG_BLOCKS, RG_BW)
    idx = jnp.arange(RG_BLOCKS)
    return g4[idx, :, idx, :]


def _prepare_rest(p):
    w = {}
    for k_, n_ in (("pa", "w_proj_a"), ("pb", "w_proj_b"), ("out", "w_out"), ("down", "ffn_w_down")):
        w[k_] = p[n_].astype(MXU_DT)
        w[k_ + "_t"] = w[k_].T
    w["gu"] = jnp.concatenate([p["ffn_w_gate"], p["ffn_w_up"]], axis=1).astype(MXU_DT)
    w["gu_t"] = w["gu"].T
    return w


def _prepare_first(p):
    w = {}
    wi = p["w_in"].astype(MXU_DT)
    cat = jnp.concatenate([wi[:, 2560:6656], wi[:, 6656:8704], wi[:, 8736:10784], wi[:, 0:2560],
                           wi[:, 8704:8736], jnp.zeros((D, 96), MXU_DT)], axis=1)
    w["in_cat"], w["in_cat_t"] = cat, cat.T
    w["rg_cw"] = _pad_rows(p["rg_conv_w"], 8)
    w["dn_cw"] = _pad_rows(p["dn_conv_w"], 8)
    w["ffn_cw"] = _pad_rows(p["ffn_conv_w"], 8)
    w["rg_vec"] = _pad_rows(jnp.stack([p["rg_conv_b"], p["rg_b_a"], p["rg_b_x"], p["rg_lambda"]]), 8)
    w["wa"] = _block_diag(p["rg_w_a"]).astype(MXU_DT)
    w["wx"] = _block_diag(p["rg_w_x"]).astype(MXU_DT)
    w["wa_t"], w["wx_t"] = w["wa"].T, w["wx"].T
    w["arow"] = jnp.pad(p["dn_a_log"], (0, 128 - NV))[None, :]
    w["drow"] = jnp.pad(p["dn_dt_bias"], (0, 128 - NV))[None, :]
    w["acol"] = p["dn_a_log"][:, None]
    w["dcol"] = p["dn_dt_bias"][:, None]
    w["nw"] = p["dn_norm_w"][None, :]
    w["ffn_cb"] = p["ffn_conv_b"][None, :]
    for n_ in ("ln1_g", "ln1_b", "ln2_g", "ln2_b"):
        w[n_] = p[n_][None, :]
    return w


def _mm_sided(a, b, side, **kw):
    if side is None:
        return _mm(a, b, **kw), []
    res = _mm(a, b, side=side, **kw)
    return res[0], res[1:]


_REST_A = ("w_proj_a", "w_proj_b", "w_out")
_REST_B = ("ffn_w_gate", "ffn_w_up", "ffn_w_down")


def _local_step(x, tgt, adam, p, shards=None):
    w = _prepare_first(p)
    side_a = side_b = None
    if shards is not None:
        side_a = _gather_exchange([shards[n] for n in _REST_A])
        side_b = _gather_exchange([shards[n] for n in _REST_B])
    h1 = _modulate(x, adam)
    cat = w["in_cat"]
    pqkv, got_a = _mm_sided(h1, cat[:, 0:4096], side_a, name="proj_qkv")
    px, got_b = _mm_sided(h1, cat[:, 8192:10752], side_b, name="proj_x")
    if shards is not None:
        p = dict(p, **{n: _unstack_shards(n, a_) for n, a_ in zip(_REST_A + _REST_B, got_a + got_b)})
    w.update(_prepare_rest(p))
    pz = _mm(h1, cat[:, 4096:6144], name="proj_z")
    pg = _mm(h1, cat[:, 6144:8192], name="proj_g")
    pab = _mm(h1, cat[:, 10752:10880], name="proj_ab")
    hrec, rec = _rg_fwd(px, w["rg_cw"], w["rg_vec"], w["wa"], w["wx"])
    q, k, v, rt = _dn_prep(pqkv, pab, w["dn_cw"], w["arow"], w["drow"])
    u, ww, qd, kd, pm, tinv, gam = _dn_intra(q, k, v, rt)
    o, vn, ssave = _dn_seq(u, ww, qd, kd, pm, gam)
    dn = _dn_post(o, pz, w["nw"])
    ya = _mm(rec, w["pa"], name="proj_a")
    yb = _mm(dn, w["pb"], name="proj_b")
    merged = _merge(pg, ya, yb)
    mix = _mm(merged, w["out"], name="proj_out")
    r1, x2, h2 = _ln1(x, mix, adam, w["ln1_g"], w["ln1_b"])
    gu = _mm(h2, w["gu"], name="ffn_gu")
    act = _ffn_act(gu, w["ffn_cw"], w["ffn_cb"])
    ff = _mm(act, w["down"], name="ffn_down")
    dff, dx2a, red2 = _ln2_loss(x2, ff, tgt, adam, w["ln2_g"], w["ln2_b"])
    g = {}
    dact = _mm(dff, w["down_t"], name="d_act")
    g["ffn_w_down"] = _mm(act, dff, name="g_down", trans_a=True)
    dgu, gcw_f, gcb_f = _ffn_bwd(dact, gu, w["ffn_cw"], w["ffn_cb"])
    dh2 = _mm(dgu, w["gu_t"], name="d_h2")
    ggu = _mm(h2, dgu, name="g_gu", trans_a=True)
    g["ffn_w_gate"], g["ffn_w_up"] = ggu[:, :D_FF], ggu[:, D_FF:]
    g["ffn_conv_w"], g["ffn_conv_b"] = gcw_f[0:3], gcb_f[0]
    dmix, dxa, red1 = _ln1_bwd(dh2, dx2a, x2, r1, mix, adam, w["ln1_g"])
    dmg = _mm(dmix, w["out_t"], name="d_merged")
    g["w_out"] = _mm(merged, dmix, name="g_out", trans_a=True)
    dya, dyb, dpg = _merge_bwd(dmg, pg, ya, yb)
    drec = _mm(dya, w["pa_t"], name="d_rec")
    g["w_proj_a"] = _mm(rec, dya, name="g_pa", trans_a=True)
    ddn = _mm(dyb, w["pb_t"], name="d_dn")
    g["w_proj_b"] = _mm(dn, dyb, name="g_pb", trans_a=True)
    do, dpz, gnw = _dn_post_bwd(ddn, o, pz, w["nw"])
    dvn, dkd, dgam = _dn_seq_bwd(do, qd, kd, pm, ww, vn, ssave, gam)
    dq, dk, dv, drt = _dn_intra_bwd(q, k, v, rt, do, dvn, dkd, gam, dgam, ssave, tinv, u, ww, vn)
    dpqkv, dpab, gcw_d, gsc = _dn_prep_bwd(dq, dk, dv, drt, pqkv, pab, w["dn_cw"], w["arow"], w["drow"],
                                           w["acol"], w["dcol"])
    dpx, gwa, gwx, gcw_r, gvec = _rg_bwd(drec, px, hrec, w["rg_cw"], w["rg_vec"], w["wa"], w["wx"],
                                         w["wa_t"], w["wx_t"])
    dproj = jnp.concatenate([dpqkv, dpz, dpg, dpx, dpab], axis=1)
    dh1 = _mm(dproj, w["in_cat_t"], name="d_h1")
    side_g = None
    if shards is not None:
        side_g = _scatter_exchange([_stack_shards(n, g[n]).astype(MXU_DT) for n in _REST_A + _REST_B])
    gc, got_g = _mm_sided(h1, dproj, side_g, name="g_in", trans_a=True)
    g["w_in"] = jnp.concatenate([gc[:, 8192:10752], gc[:, 0:4096], gc[:, 4096:6144], gc[:, 10752:10784],
                                 gc[:, 6144:8192]], axis=1)
    gx, red0 = _modulate_bwd(dh1, dxa, x, adam)
    g["rg_conv_w"], g["rg_conv_b"] = gcw_r[0:4], gvec[0]
    g["rg_w_a"], g["rg_w_x"] = _diag_blocks(gwa), _diag_blocks(gwx)
    g["rg_b_a"], g["rg_b_x"], g["rg_lambda"] = gvec[1], gvec[2], gvec[3]
    g["dn_conv_w"] = gcw_d[0:4]
    g["dn_a_log"], g["dn_dt_bias"], g["dn_norm_w"] = gsc[:, 0], gsc[:, 1], gnw[0]
    g["ln1_g"], g["ln1_b"] = red1[2], red1[3]
    g["ln2_g"], g["ln2_b"] = red2[0], red2[1]
    d_ada = jnp.concatenate([red0[1], red0[0], red1[4], red1[1], red1[0], red2[2]])
    return red2[3, 0], gx, g, d_ada, dict(zip(_REST_A + _REST_B, got_g))


_BIG = ("w_in", "w_proj_a", "w_proj_b", "w_out", "ffn_w_gate", "ffn_w_up", "ffn_w_down")
_COL_SHARDED = ("w_in", "ffn_w_gate", "ffn_w_up")
_CONV = ("rg_conv_w", "dn_conv_w", "ffn_conv_w")
_REPL = ("b_ada", "rg_conv_b", "rg_w_a", "rg_b_a", "rg_w_x", "rg_b_x", "rg_lambda", "dn_a_log",
         "dn_dt_bias", "dn_norm_w", "ln1_g", "ln1_b", "ffn_conv_b", "ln2_g", "ln2_b")
_NAMES = ("w_ada", "b_ada", "w_in", "rg_conv_w", "rg_conv_b", "rg_w_a", "rg_b_a", "rg_w_x", "rg_b_x",
          "rg_lambda", "dn_conv_w", "dn_a_log", "dn_dt_bias", "dn_norm_w", "w_proj_a", "w_proj_b", "w_out",
          "ln1_g", "ln1_b", "ffn_w_gate", "ffn_w_up", "ffn_conv_w", "ffn_conv_b", "ffn_w_down", "ln2_g", "ln2_b")


def _pack(arrs, width, row_mult):
    pieces = []
    for a in arrs:
        f = a.reshape(-1)
        pieces.append(jnp.pad(f, (0, (-f.shape[0]) % width)))
    flat = jnp.concatenate(pieces)
    rows = flat.shape[0] // width
    return jnp.pad(flat, (0, ((-rows) % row_mult) * width)).reshape(-1, width)


def _unpack(flat, shapes, width):
    out, row = [], 0
    for shp in shapes:
        n = 1
        for d_ in shp:
            n *= d_
        rows = -(-n // width)
        out.append(flat[row:row + rows].reshape(-1)[:n].reshape(shp))
        row += rows
    return out


def _stack_shards(name, full):
    if name in _COL_SHARDED or name in _CONV:
        r, ncol = full.shape
        return full.reshape(r, 4, ncol // 4).transpose(1, 0, 2)
    return full.reshape((4, full.shape[0] // 4) + full.shape[1:])


def _unstack_shards(name, st):
    if name in _COL_SHARDED or name in _CONV:
        return st.transpose(1, 0, 2).reshape(st.shape[1], 4 * st.shape[2])
    return st.reshape((4 * st.shape[1],) + st.shape[2:])


def kernel(x, c, w_ada, b_ada, w_in, rg_conv_w, rg_conv_b, rg_w_a, rg_b_a, rg_w_x, rg_b_x, rg_lambda, dn_conv_w, dn_a_log, dn_dt_bias, dn_norm_w, w_proj_a, w_proj_b, w_out, ln1_g, ln1_b, ffn_w_gate, ffn_w_up, ffn_conv_w, ffn_conv_b, ffn_w_down, ln2_g, ln2_b, loss_target, m_w_ada, m_b_ada, m_w_in, m_rg_conv_w, m_rg_conv_b, m_rg_w_a, m_rg_b_a, m_rg_w_x, m_rg_b_x, m_rg_lambda, m_dn_conv_w, m_dn_a_log, m_dn_dt_bias, m_dn_norm_w, m_w_proj_a, m_w_proj_b, m_w_out, m_ln1_g, m_ln1_b, m_ffn_w_gate, m_ffn_w_up, m_ffn_conv_w, m_ffn_conv_b, m_ffn_w_down, m_ln2_g, m_ln2_b, v_w_ada, v_b_ada, v_w_in, v_rg_conv_w, v_rg_conv_b, v_rg_w_a, v_rg_b_a, v_rg_w_x, v_rg_b_x, v_rg_lambda, v_dn_conv_w, v_dn_a_log, v_dn_dt_bias, v_dn_norm_w, v_w_proj_a, v_w_proj_b, v_w_out, v_ln1_g, v_ln1_b, v_ffn_w_gate, v_ffn_w_up, v_ffn_conv_w, v_ffn_conv_b, v_ffn_w_down, v_ln2_g, v_ln2_b):
    args = locals()
    wts = {n: args[n][0] for n in _NAMES}
    mom = {n: args["m_" + n][0] for n in _NAMES}
    var = {n: args["v_" + n][0] for n in _NAMES}
    xs, tgt = x[0], loss_target[0]
    ix, iy, ic = _coords()
    shard = 2 * ix + iy
    batch = 4 * ix + 2 * iy + ic

    c_all = _allgather8(_pad_rows(c, 8), "gather_c")[:, 0, :]
    sc16 = _pad_rows(_silu_rows(c_all), 16)
    ada_cols = _mm(sc16, w_ada[0], name="ada")[:8]
    ada_g = _allgather8(ada_cols, "gather_ada")
    ada_all = jnp.concatenate([ada_g[0], ada_g[2], ada_g[4], ada_g[6]], axis=1) + b_ada
    adam = _pad_rows(lax.dynamic_index_in_dim(ada_all, batch, 0, keepdims=False).reshape(6, D), 8)

    (w_in_all,) = _run_exchange(_gather_exchange([wts["w_in"].astype(MXU_DT)]), "gather_w_in")
    full = {"w_in": _unstack_shards("w_in", w_in_all)}
    conv_shard = _pack([wts[n] for n in _CONV], 128, 8)
    conv_all = _allgather8(conv_shard, "gather_conv")
    shapes_conv = [wts[n].shape for n in _CONV]
    per_shard = [_unpack(conv_all[2 * s], shapes_conv, 128) for s in range(4)]
    for i, n in enumerate(_CONV):
        full[n] = _unstack_shards(n, jnp.stack([per_shard[s][i] for s in range(4)]))
    for n in _REPL:
        full[n] = wts[n]

    shards = {n: wts[n].astype(MXU_DT) for n in _REST_A + _REST_B}
    loss_b, gx, g, d_ada, recv = _local_step(xs, tgt, adam, full, shards)

    (recv["w_in"],) = _run_exchange(_scatter_exchange([_stack_shards("w_in", g["w_in"]).astype(MXU_DT)]),
                                    "scatter_w_in")
    parts = []
    for n in _BIG:
        r_ = recv[n]
        axis = 1 if n in _COL_SHARDED else 0
        width = wts[n].shape[axis]
        own = lax.dynamic_slice_in_dim(g[n], shard * width, width, axis=axis)
        parts.append(_sum_partials(own, r_, "sum_" + n))
    parts_sib = _swap_sibling(parts)
    out = {n: [] for n in _NAMES}
    for n, p_, q_ in zip(_BIG, parts, parts_sib):
        out[n] = list(_adamw([p_, q_], wts[n], mom[n], var[n], "adamw_" + n))

    small_names = _CONV + _REPL[1:]
    small = _pack([d_ada, jnp.full((1,), loss_b, F32)] + [g[n] for n in small_names], 128, 64)
    small_all = _allgather8(small, "gather_small")
    total = _sum8(small_all)
    shapes_small = [(6 * D,), (1,)] + [full[n].shape for n in small_names]
    tot = _unpack(total, shapes_small, 128)
    gsum = dict(zip(small_names, tot[2:]))
    gsum["b_ada"] = tot[0]
    loss = tot[1][0]
    for n in _CONV:
        gsum[n] = lax.dynamic_index_in_dim(_stack_shards(n, gsum[n]), shard, 0, keepdims=False)
    d_ada_all = small_all[:, :6 * D // 128, :].reshape(8, 6 * D)
    cols = lax.dynamic_slice_in_dim(d_ada_all, shard * (6 * D // 4), 6 * D // 4, axis=1)
    g_wada = _mm(sc16, _pad_rows(cols, 16), name="g_ada", trans_a=True)
    res = _adamw([g_wada], wts["w_ada"], mom["w_ada"], var["w_ada"], "adamw_ada")
    out["w_ada"] = list(res)
    names_s = _CONV + _REPL
    shapes_s = [wts[n].shape for n in names_s]
    pk = lambda d_: _pack([d_[n] for n in names_s], 128, 64)
    res_s = _adamw([pk(gsum)], pk(wts), pk(mom), pk(var), "adamw_small")
    for r_ in res_s:
        for n, a in zip(names_s, _unpack(r_, shapes_s, 128)):
            out[n].append(a)

    outs = [loss, gx[None]]
    for i in range(4):
        outs += [out[n][i][None] for n in _NAMES]
    return tuple(outs)
```

```python
import functools

import jax
import jax.numpy as jnp
from jax import lax
from jax.experimental import pallas as pl
from jax.experimental.pallas import tpu as pltpu

F32 = jnp.float32
BF16 = jnp.bfloat16
MXU_DT = BF16

D = 1024
D_RNN = 1280
RG_BLOCKS = 16
RG_BW = 80
RG_C = 8.0
NQ = 8
NV = 16
HD = 128
CH = 64
D_FF = 2816
LN_EPS = 1e-5
RMS_EPS = 1e-6
L2_EPS = 1e-6
ALPHA = 2.0 ** 0.25
Q_SCALE = HD ** -0.5
N_CAT = 10880
VMEM_LIMIT = 56 * 1024 * 1024
MM_VMEM_BUDGET = 36 * 1024 * 1024
SEQ_GROUP = 8
MESH = pl.DeviceIdType.MESH

ADAM_LR, ADAM_B1, ADAM_B2, ADAM_EPS, ADAM_WD, ADAM_STEP = 1e-3, 0.9, 0.999, 1e-8, 0.01, 10


def _sigmoid(x):
    return 0.5 * jnp.tanh(0.5 * x) + 0.5


def _softplus(x):
    return jnp.maximum(x, 0.0) + jnp.log1p(jnp.exp(-jnp.abs(x)))


_GC = 0.7978845608028654


def _gelu(x):
    return 0.5 * x * (1.0 + jnp.tanh(_GC * (x + 0.044715 * x * x * x)))


def _gelu_and_grad(x):
    t = jnp.tanh(_GC * (x + 0.044715 * x * x * x))
    g = 0.5 * x * (1.0 + t)
    dg = 0.5 * (1.0 + t) + 0.5 * x * (1.0 - t * t) * _GC * (1.0 + 3 * 0.044715 * x * x)
    return g, dg


def _neg_expm1(y):
    series = -y * (1.0 + 0.5 * y * (1.0 + y * (1.0 / 3.0)))
    return jnp.where(y > -0.01, series, 1.0 - jnp.exp(y))


def _dot(a, b):
    return jnp.dot(a.astype(MXU_DT), b.astype(MXU_DT), preferred_element_type=F32)


def _dot_nt(a, b):
    return lax.dot_general(a.astype(MXU_DT), b.astype(MXU_DT), (((1,), (1,)), ((), ())),
                           preferred_element_type=F32)


def _dot_tn(a, b):
    return lax.dot_general(a.astype(MXU_DT), b.astype(MXU_DT), (((0,), (0,)), ((), ())),
                           preferred_element_type=F32)


def _split(a):
    hi = a.astype(BF16)
    return hi, (a - hi.astype(F32)).astype(BF16)


def _dot3(a, b, dims=(((1,), (0,)), ((), ()))):
    ah, al = _split(a)
    bh, bl = _split(b)
    d = lambda p, q: lax.dot_general(p, q, dims, preferred_element_type=F32)
    return d(ah, bh) + (d(al, bh) + d(ah, bl))


def _dot3_tn(a, b):
    return _dot3(a, b, (((0,), (0,)), ((), ())))


def _iota(shape, dim):
    return lax.broadcasted_iota(jnp.int32, shape, dim)


def _shift_down(x, before, j):
    if j == 0:
        return x
    xr = pltpu.roll(x, j, 0)
    br = pltpu.roll(before, j, 0)
    top = jnp.where(_iota(br.shape, 0) < j, br, xr[:8])
    return jnp.concatenate([top, xr[8:]], axis=0)


def _shift_up(x, after, j):
    if j == 0:
        return x
    t = x.shape[0]
    xr = pltpu.roll(x, t - j, 0)
    ar = pltpu.roll(after, 8 - j, 0)
    bot = jnp.where(_iota(ar.shape, 0) >= 8 - j, ar, xr[t - 8:])
    return jnp.concatenate([xr[:t - 8], bot], axis=0)


def _taps(x, before, k):
    return [_shift_down(x, before, k - 1 - i) for i in range(k)]


def _conv_taps(taps, w_ref):
    y = w_ref[0:1, :] * taps[0]
    for i in range(1, len(taps)):
        y = y + w_ref[i:i + 1, :] * taps[i]
    return y


def _conv_causal(x, before, w_ref, k):
    return _conv_taps(_taps(x, before, k), w_ref)


def _conv_causal_bwd(dy, after, w_ref, k):
    dx = w_ref[k - 1:k, :] * dy
    for i in range(k - 1):
        dx = dx + w_ref[i:i + 1, :] * _shift_up(dy, after, k - 1 - i)
    return dx


def _scan_fwd(a, u):
    t = a.shape[0]
    rows = _iota(a.shape, 0)
    d = 1
    while d < t:
        m = rows >= d
        u = u + jnp.where(m, a * pltpu.roll(u, d, 0), 0.0)
        a = jnp.where(m, a * pltpu.roll(a, d, 0), a)
        d *= 2
    return a, u


def _scan_rev(a, u):
    t = a.shape[0]
    rows = _iota(a.shape, 0)
    d = 1
    while d < t:
        m = rows < t - d
        u = u + jnp.where(m, a * pltpu.roll(u, t - d, 0), 0.0)
        a = jnp.where(m, a * pltpu.roll(a, t - d, 0), a)
        d *= 2
    return a, u


def _chunk_cumsum(g, axis, rev=False):
    n = g.shape[axis]
    pos = _iota(g.shape, axis) & (CH - 1)
    d = 1
    while d < CH:
        if rev:
            g = g + jnp.where(pos < CH - d, pltpu.roll(g, n - d, axis), 0.0)
        else:
            g = g + jnp.where(pos >= d, pltpu.roll(g, d, axis), 0.0)
        d *= 2
    return g


def _ln_stats(r):
    mu = jnp.mean(r, axis=-1, keepdims=True)
    xc = r - mu
    var = jnp.mean(xc * xc, axis=-1, keepdims=True)
    rstd = lax.rsqrt(var + LN_EPS)
    return xc * rstd, rstd


def _ln_bwd(dy, xhat, rstd, g):
    dxh = dy * g
    return rstd * (dxh - jnp.mean(dxh, axis=-1, keepdims=True)
                   - xhat * jnp.mean(dxh * xhat, axis=-1, keepdims=True))


def _rsum(x):
    return jnp.sum(x, axis=0, keepdims=True)


def _params(sem):
    return pltpu.CompilerParams(dimension_semantics=sem, vmem_limit_bytes=VMEM_LIMIT)


def _pick(n, cands):
    for c in cands:
        if n % c == 0:
            return c
    return n


def _rows(tm, w, col=0, nt=None):
    if nt is None:
        return pl.BlockSpec((tm, w), lambda i: (i, col))
    return pl.BlockSpec((tm, w), lambda i: (nt - 1 - i, col))


def _before(tm, w, col=0, nt=None):
    r = tm // 8
    if nt is None:
        return pl.BlockSpec((8, w), lambda i: (jnp.maximum(i * r - 1, 0), col))
    return pl.BlockSpec((8, w), lambda i: (jnp.maximum((nt - 1 - i) * r - 1, 0), col))


def _whole(shape):
    return pl.BlockSpec(shape, lambda *_: (0,) * len(shape))


def _mm_plan(a, b, out_dtype):
    m, kk = a.shape
    _, n = b.shape
    tm = _pick(m, (512, 256, 128))
    tk = kk if kk <= 5632 else _pick(kk, (2176, 2048, 1024))
    nk = kk // tk

    def vmem_bytes(tn):
        blocks = tm * tk * a.dtype.itemsize + tk * tn * b.dtype.itemsize + tm * tn * jnp.dtype(out_dtype).itemsize
        return 2 * blocks + (tm * tn * 4 if nk > 1 else 0)

    cands = [t for t in (1408, 1280, 1024, 640, 512, 256, 128) if n % t == 0] or [n]
    tn = next((t for t in cands if vmem_bytes(t) <= MM_VMEM_BUDGET), cands[-1])
    return tm, tn, tk, nk


def _mm_side(a, b, name, out_dtype, side):
    m, _ = a.shape
    _, n = b.shape
    tm, tn, tk, nk = _mm_plan(a, b, out_dtype)
    ni, nj = m // tm, n // tn
    n_in, n_out = len(side.ins), len(side.out_shapes)

    def body(*refs):
        a_ref, b_ref = refs[0], refs[1]
        s_in = refs[2:2 + n_in]
        o_ref = refs[2 + n_in]
        s_out = refs[3 + n_in:3 + n_in + n_out]
        acc = refs[3 + n_in + n_out]
        sems = refs[4 + n_in + n_out:]
        i, j, k = pl.program_id(0), pl.program_id(1), pl.program_id(2)

        @pl.when((i == 0) & (j == 0) & (k == 0))
        def _():
            side.start(s_in, s_out, sems)

        @pl.when(k == 0)
        def _():
            acc[...] = jnp.zeros_like(acc)
        acc[...] += _dot(a_ref[...], b_ref[...])

        @pl.when(k == nk - 1)
        def _():
            o_ref[...] = acc[...].astype(out_dtype)

        @pl.when((i == ni - 1) & (j == nj - 1) & (k == nk - 1))
        def _():
            side.finish(s_in, s_out, sems)

    return pl.pallas_call(
        body, name=name, grid=(ni, nj, nk),
        in_specs=[pl.BlockSpec((tm, tk), lambda i, j, k: (i, k)),
                  pl.BlockSpec((tk, tn), lambda i, j, k: (k, j))] + [_HBM] * n_in,
        out_specs=[pl.BlockSpec((tm, tn), lambda i, j, k: (i, j))] + [_HBM] * n_out,
        out_shape=[jax.ShapeDtypeStruct((m, n), out_dtype)] + list(side.out_shapes),
        scratch_shapes=[pltpu.VMEM((tm, tn), F32)] + list(side.sems),
        compiler_params=_params(("arbitrary", "arbitrary", "arbitrary")),
    )(a, b, *side.ins)


def _mm(a, b, *, name, trans_a=False, out_dtype=F32, side=None):
    if trans_a:
        return _mm(a.T, b, name=name, out_dtype=out_dtype, side=side)
    if side is not None:
        return _mm_side(a, b, name, out_dtype, side)
    m, kk = a.shape
    _, n = b.shape
    tm, tn, tk, nk = _mm_plan(a, b, out_dtype)

    if nk == 1:
        def body(a_ref, b_ref, o_ref):
            o_ref[...] = _dot(a_ref[...], b_ref[...]).astype(out_dtype)
        scratch = []
    else:
        def body(a_ref, b_ref, o_ref, acc):
            k = pl.program_id(2)

            @pl.when(k == 0)
            def _():
                acc[...] = jnp.zeros_like(acc)
            acc[...] += _dot(a_ref[...], b_ref[...])

            @pl.when(k == nk - 1)
            def _():
                o_ref[...] = acc[...].astype(out_dtype)
        scratch = [pltpu.VMEM((tm, tn), F32)]

    return pl.pallas_call(
        body, name=name, grid=(m // tm, n // tn, nk),
        in_specs=[pl.BlockSpec((tm, tk), lambda i, j, k: (i, k)),
                  pl.BlockSpec((tk, tn), lambda i, j, k: (k, j))],
        out_specs=pl.BlockSpec((tm, tn), lambda i, j, k: (i, j)),
        out_shape=jax.ShapeDtypeStruct((m, n), out_dtype),
        scratch_shapes=scratch,
        compiler_params=_params(("parallel", "parallel", "arbitrary")),
    )(a, b)


def _modulate(x, adam):
    s = x.shape[0]
    tm = _pick(s, (512, 256, 128))

    def body(x_ref, ada_ref, o_ref):
        o_ref[...] = (x_ref[...] * (1.0 + ada_ref[1:2, :]) + ada_ref[0:1, :]).astype(MXU_DT)

    return pl.pallas_call(
        body, name="modulate1", grid=(s // tm,),
        in_specs=[_rows(tm, D), _whole((8, D))], out_specs=_rows(tm, D),
        out_shape=jax.ShapeDtypeStruct((s, D), MXU_DT), compiler_params=_params(("parallel",)),
    )(x, adam)


def _rg_gates(xc, wa_ref, wx_ref, vec_ref):
    xb = xc.astype(MXU_DT)
    r = _sigmoid(jnp.dot(xb, wa_ref[...], preferred_element_type=F32) + vec_ref[1:2, :])
    ig = _sigmoid(jnp.dot(xb, wx_ref[...], preferred_element_type=F32) + vec_ref[2:3, :])
    sp = _softplus(-vec_ref[3:4, :])
    la = -RG_C * r * sp
    a = jnp.exp(la)
    n1 = _neg_expm1(2.0 * la)
    rmult = lax.rsqrt(jnp.maximum(n1, 1e-20))
    return r, ig, a, n1 * rmult, sp, rmult


def _rg_fwd(px, cw, vec, wa, wx):
    s = px.shape[0]
    tm = _pick(s, (256, 128))
    w = D_RNN

    def body(xr_ref, gr_ref, cw_ref, vec_ref, wa_ref, wx_ref, h_ref, rec_ref, prev_x, prev_h):
        @pl.when(pl.program_id(0) == 0)
        def _():
            prev_x[...] = jnp.zeros_like(prev_x)
            prev_h[...] = jnp.zeros_like(prev_h)
        x = xr_ref[...]
        xc = _conv_causal(x, prev_x[...], cw_ref, 4) + vec_ref[0:1, :]
        prev_x[...] = x[tm - 8:, :]
        _, ig, a, mult, _, _ = _rg_gates(xc, wa_ref, wx_ref, vec_ref)
        acum, h = _scan_fwd(a, mult * ig * xc)
        h = h + acum * prev_h[7:8, :]
        prev_h[...] = h[tm - 8:, :]
        h_ref[...] = h
        rec_ref[...] = (h * _gelu(gr_ref[...])).astype(MXU_DT)

    return pl.pallas_call(
        body, name="rg_fwd", grid=(s // tm,),
        in_specs=[_rows(tm, w, 0), _rows(tm, w, 1), _whole((8, w)), _whole((8, w)),
                  _whole((w, w)), _whole((w, w))],
        out_specs=[_rows(tm, w), _rows(tm, w)],
        out_shape=[jax.ShapeDtypeStruct((s, w), F32), jax.ShapeDtypeStruct((s, w), MXU_DT)],
        scratch_shapes=[pltpu.VMEM((8, w), F32), pltpu.VMEM((8, w), F32)],
        compiler_params=_params(("arbitrary",)),
    )(px, px, cw, vec, wa, wx)


def _dn_scalars(ab, arow, drow):
    lane = _iota(ab.shape, 1)
    g = jnp.where(lane < NV, -jnp.exp(arow) * _softplus(ab + drow), 0.0)
    beta = _sigmoid(ab)
    return lane, g, beta


def _l2n_heads(c, out_ref, off, scale):
    for hh in range(NQ):
        x = c[:, off + hh * HD: off + (hh + 1) * HD]
        r = lax.rsqrt(jnp.sum(x * x, axis=-1, keepdims=True) + L2_EPS)
        out_ref[:, hh * HD:(hh + 1) * HD] = x * (r * scale)


def _dn_prep(pqkv, pab, cw, arow, drow):
    s = pqkv.shape[0]
    tm = 128
    wq = NQ * HD

    def body(x_ref, ab_ref, cw_ref, a_ref, d_ref, q_ref, k_ref, v_ref, rt_ref, prev_x):
        @pl.when(pl.program_id(0) == 0)
        def _():
            prev_x[...] = jnp.zeros_like(prev_x)
        x = x_ref[...]
        cp = _conv_causal(x, prev_x[...], cw_ref, 4)
        prev_x[...] = x[tm - 8:, :]
        c = cp * _sigmoid(cp)
        _l2n_heads(c, q_ref, 0, Q_SCALE)
        _l2n_heads(c, k_ref, wq, 1.0)
        v_ref[...] = c[:, 2 * wq:]
        lane, g, beta = _dn_scalars(ab_ref[...], a_ref[...], d_ref[...])
        gc = _chunk_cumsum(g, 0)
        gl = gc + _chunk_cumsum(g, 0, rev=True) - g
        pack = jnp.where(lane < NV, gc, jnp.where(lane < 2 * NV, beta,
                         jnp.where(lane < 3 * NV, pltpu.roll(gl, 2 * NV, 1), 0.0)))
        rt_ref[...] = pack.T[0:3 * NV, :]

    return pl.pallas_call(
        body, name="dn_prep", grid=(s // tm,),
        in_specs=[_rows(tm, 4 * wq), _rows(tm, 128), _whole((8, 4 * wq)), _whole((1, 128)), _whole((1, 128))],
        out_specs=[_rows(tm, wq), _rows(tm, wq), _rows(tm, 2 * wq),
                   pl.BlockSpec((3 * NV, tm), lambda i: (0, i))],
        out_shape=[jax.ShapeDtypeStruct((s, wq), F32), jax.ShapeDtypeStruct((s, wq), F32),
                   jax.ShapeDtypeStruct((s, 2 * wq), F32), jax.ShapeDtypeStruct((3 * NV, s), F32)],
        scratch_shapes=[pltpu.VMEM((8, 4 * wq), F32)],
        compiler_params=_params(("arbitrary",)),
    )(pqkv, pab, cw, arow, drow)


def _pair_masks():
    i = _iota((2 * CH, 2 * CH), 0)
    j = _iota((2 * CH, 2 * CH), 1)
    same = (i >> 6) == (j >> 6)
    return same & (i >= j), same & (i > j)


def _head_cols(rt_ref, h):
    shp = (2 * CH, 2 * CH)
    g_r = jnp.broadcast_to(rt_ref[pl.ds(h, 1), :], shp)
    b_r = jnp.broadcast_to(rt_ref[pl.ds(NV + h, 1), :], shp)
    l_r = jnp.broadcast_to(rt_ref[pl.ds(2 * NV + h, 1), :], shp)
    return g_r, g_r.T, b_r.T, l_r, l_r.T


def _inv_unit_lower(a):
    return _inv_unit_lower_many([a])[0]


def _inv_unit_lower_many(a_list):
    n = a_list[0].shape[0]
    eye = (_iota((n, n), 0) == _iota((n, n), 1)).astype(F32)
    bs = [-a for a in a_list]
    xs = [eye + b for b in bs]
    for _ in range(5):
        bs = [_dot(b, b) for b in bs]
        xs = [x + _dot(x, b) for x, b in zip(xs, bs)]
    rs = [(eye - x) - _dot3(a, x) for a, x in zip(a_list, xs)]
    return [x + _dot(x, r) for x, r in zip(xs, rs)]


def _gam_rows(l_r):
    lrow = l_r[0:1, :]
    lane = _iota(lrow.shape, 1)
    other = pltpu.roll(lrow, CH, 1)
    return jnp.exp(jnp.where(lane < CH, lrow, other)), jnp.exp(jnp.where(lane >= CH, lrow, other))


def _dn_intra(q, k, v, rt):
    s = q.shape[0]
    nb = s // (2 * CH)
    qps = 8
    blk = pl.BlockSpec((2 * CH, qps * HD), lambda i, h: (i, h))
    blk2 = pl.BlockSpec((2 * CH, 2 * qps * HD), lambda i, h: (i, h))

    def body(q_ref, k_ref, v_ref, rt_ref, u_ref, w_ref, qd_ref, kd_ref, p_ref, ti_ref, gam_ref):
        hstep = pl.program_id(1)
        mc, ms = _pair_masks()
        gam_ref[...] = jnp.zeros_like(gam_ref)
        heads = []
        for qh in range(qps):
            qq = q_ref[:, qh * HD:(qh + 1) * HD]
            kk_ = k_ref[:, qh * HD:(qh + 1) * HD]
            kk = _dot_nt(kk_, kk_)
            qk = _dot_nt(qq, kk_)
            for j in range(2):
                idx = 2 * qh + j
                cs = slice(idx * HD, (idx + 1) * HD)
                g_r, g_c, b_c, l_r, l_c = _head_cols(rt_ref, 2 * qps * hstep + idx)
                dec = jnp.where(mc, jnp.exp(jnp.where(mc, g_c - g_r, 0.0)), 0.0)
                eg = jnp.exp(g_c)
                p_ref[:, cs] = jnp.where(mc, qk * dec, 0.0)
                qd_ref[:, cs] = eg * qq
                kd_ref[:, cs] = jnp.exp(l_c - g_c) * kk_
                ga, gb = _gam_rows(l_r)
                gam_ref[0, qh, 2 * j:2 * j + 1, :] = ga
                gam_ref[0, qh, 2 * j + 1:2 * j + 2, :] = gb
                rhs = jnp.concatenate([b_c * v_ref[:, cs], b_c * eg * kk_], axis=1)
                heads.append((cs, jnp.where(ms, b_c * kk * dec, 0.0), rhs))
        tinvs = _inv_unit_lower_many([a for _, a, _ in heads])
        uws = [_dot3(t, rhs) for t, (_, _, rhs) in zip(tinvs, heads)]
        for t, uw, (cs, _, _) in zip(tinvs, uws, heads):
            ti_ref[:, cs] = t
            u_ref[:, cs] = uw[:, :HD]
            w_ref[:, cs] = uw[:, HD:]

    big = jax.ShapeDtypeStruct((s, NV * HD), F32)
    return pl.pallas_call(
        body, name="dn_intra", grid=(nb, NQ // qps),
        in_specs=[blk, blk, blk2, pl.BlockSpec((3 * NV, 2 * CH), lambda i, h: (0, i))],
        out_specs=[blk2] * 6 + [pl.BlockSpec((1, qps, 8, 128), lambda i, h: (i, h, 0, 0))],
        out_shape=[big] * 6 + [jax.ShapeDtypeStruct((nb, NQ, 8, 128), F32)],
        compiler_params=_params(("parallel", "parallel")),
    )(q, k, v, rt)


def _dn_seq(u, w, qd, kd, p, gam):
    s = u.shape[0]
    nb = s // (2 * CH)
    wide = pl.BlockSpec((2 * CH, NV * HD), lambda i: (i, 0))

    def body(u_ref, w_ref, qd_ref, kd_ref, p_ref, gam_ref, o_ref, vn_ref, ss_ref, st):
        @pl.when(pl.program_id(0) == 0)
        def _():
            st[...] = jnp.zeros_like(st)
        ra, rb = slice(0, CH), slice(CH, 2 * CH)
        for g0 in range(0, NV, SEQ_GROUP):
            hs = list(range(g0, g0 + SEQ_GROUP))
            cs = [slice(h * HD, (h + 1) * HD) for h in hs]
            ga = [gam_ref[0, h // 2, 2 * (h % 2):2 * (h % 2) + 1, :] for h in hs]
            gb = [gam_ref[0, h // 2, 2 * (h % 2) + 1:2 * (h % 2) + 2, :] for h in hs]
            s0 = [st[h] for h in hs]
            vna = [u_ref[ra, c] - _dot(w_ref[ra, c], s) for c, s in zip(cs, s0)]
            s1 = [g * s + _dot_tn(kd_ref[ra, c], v) for g, s, c, v in zip(ga, s0, cs, vna)]
            vnb = [u_ref[rb, c] - _dot(w_ref[rb, c], s) for c, s in zip(cs, s1)]
            s2 = [g * s + _dot_tn(kd_ref[rb, c], v) for g, s, c, v in zip(gb, s1, cs, vnb)]
            for h, s in zip(hs, s2):
                st[h] = s
            oa = [_dot(qd_ref[ra, c], s) for c, s in zip(cs, s0)]
            ob = [_dot(qd_ref[rb, c], s) for c, s in zip(cs, s1)]
            for i_, h in enumerate(hs):
                vn = jnp.concatenate([vna[i_], vnb[i_]], axis=0)
                o_ref[:, cs[i_]] = jnp.concatenate([oa[i_], ob[i_]], axis=0) + _dot(p_ref[:, cs[i_]], vn)
                vn_ref[:, cs[i_]] = vn
                ss_ref[h, 0:HD, :] = s0[i_]
                ss_ref[h, HD:2 * HD, :] = s1[i_]

    big = jax.ShapeDtypeStruct((s, NV * HD), F32)
    return pl.pallas_call(
        body, name="dn_seq", grid=(nb,),
        in_specs=[wide] * 5 + [pl.BlockSpec((1, NQ, 8, 128), lambda i: (i, 0, 0, 0))],
        out_specs=[wide, wide, pl.BlockSpec((NV, 2 * HD, HD), lambda i: (0, i, 0))],
        out_shape=[big, big, jax.ShapeDtypeStruct((NV, 2 * s, HD), F32)],
        scratch_shapes=[pltpu.VMEM((NV, HD, HD), F32)],
        compiler_params=_params(("arbitrary",)),
    )(u, w, qd, kd, p, gam)


def _dn_post(o, pz, nw):
    s = o.shape[0]
    tm = _pick(s, (256, 128))

    def body(o_ref, z_ref, nw_ref, y_ref):
        for h in range(NV):
            cs = slice(h * HD, (h + 1) * HD)
            oh = o_ref[:, cs]
            z = z_ref[:, cs]
            rs = lax.rsqrt(jnp.mean(oh * oh, axis=-1, keepdims=True) + RMS_EPS)
            y_ref[:, cs] = (oh * rs * nw_ref[...] * (z * _sigmoid(z))).astype(MXU_DT)

    return pl.pallas_call(
        body, name="dn_post", grid=(s // tm,),
        in_specs=[_rows(tm, NV * HD), _rows(tm, NV * HD), _whole((1, HD))],
        out_specs=_rows(tm, NV * HD),
        out_shape=jax.ShapeDtypeStruct((s, NV * HD), MXU_DT), compiler_params=_params(("parallel",)),
    )(o, pz, nw)


def _merge(pg, ya, yb):
    s = ya.shape[0]
    tm = _pick(s, (512, 256, 128))

    def body(ga_ref, gb_ref, ya_ref, yb_ref, o_ref):
        o_ref[...] = (_sigmoid(ga_ref[...]) * ya_ref[...] + _sigmoid(gb_ref[...]) * yb_ref[...]).astype(MXU_DT)

    return pl.pallas_call(
        body, name="merge", grid=(s // tm,),
        in_specs=[_rows(tm, D, 0), _rows(tm, D, 1), _rows(tm, D), _rows(tm, D)],
        out_specs=_rows(tm, D),
        out_shape=jax.ShapeDtypeStruct((s, D), MXU_DT), compiler_params=_params(("parallel",)),
    )(pg, pg, ya, yb)


def _ln1(x, mix, adam, lng, lnb):
    s = x.shape[0]
    tm = _pick(s, (512, 256, 128))

    def body(x_ref, m_ref, ada_ref, g_ref, b_ref, r_ref, x2_ref, h2_ref):
        r = ALPHA * x_ref[...] + (1.0 + ada_ref[2:3, :]) * m_ref[...]
        xhat, _ = _ln_stats(r)
        x2 = xhat * g_ref[...] + b_ref[...]
        r_ref[...] = r
        x2_ref[...] = x2
        h2_ref[...] = (x2 * (1.0 + ada_ref[4:5, :]) + ada_ref[3:4, :]).astype(MXU_DT)

    return pl.pallas_call(
        body, name="ln1", grid=(s // tm,),
        in_specs=[_rows(tm, D), _rows(tm, D), _whole((8, D)), _whole((1, D)), _whole((1, D))],
        out_specs=[_rows(tm, D)] * 3,
        out_shape=[jax.ShapeDtypeStruct((s, D), F32), jax.ShapeDtypeStruct((s, D), F32),
                   jax.ShapeDtypeStruct((s, D), MXU_DT)],
        compiler_params=_params(("parallel",)),
    )(x, mix, adam, lng, lnb)


def _ffn_act(gu, cw, cb):
    s = gu.shape[0]
    tm = _pick(s, (256, 128))
    w = D_FF

    def body(g_ref, u_ref, cw_ref, cb_ref, o_ref, prev):
        @pl.when(pl.program_id(0) == 0)
        def _():
            prev[...] = jnp.zeros_like(prev)
        g = g_ref[...]
        gc = _conv_causal(g, prev[...], cw_ref, 3) + cb_ref[...]
        prev[...] = g[tm - 8:, :]
        o_ref[...] = (_gelu(gc) * u_ref[...]).astype(MXU_DT)

    return pl.pallas_call(
        body, name="ffn_act", grid=(s // tm,),
        in_specs=[_rows(tm, w, 0), _rows(tm, w, 1), _whole((8, w)), _whole((1, w))],
        out_specs=_rows(tm, w),
        out_shape=jax.ShapeDtypeStruct((s, w), MXU_DT),
        scratch_shapes=[pltpu.VMEM((8, w), F32)],
        compiler_params=_params(("arbitrary",)),
    )(gu, gu, cw, cb)


def _ln2_loss(x2, ff, tgt, adam, lng, lnb):
    s = x2.shape[0]
    tm = _pick(s, (512, 256, 128))

    def body(x_ref, f_ref, t_ref, ada_ref, g_ref, b_ref, dff_ref, dx_ref, red_ref):
        @pl.when(pl.program_id(0) == 0)
        def _():
            red_ref[...] = jnp.zeros_like(red_ref)
        ff_ = f_ref[...]
        r = ALPHA * x_ref[...] + (1.0 + ada_ref[5:6, :]) * ff_
        xhat, rstd = _ln_stats(r)
        err = xhat * g_ref[...] + b_ref[...] - t_ref[...]
        dy = err * (1.0 / D)
        dr = _ln_bwd(dy, xhat, rstd, g_ref[...])
        dff_ref[...] = ((1.0 + ada_ref[5:6, :]) * dr).astype(MXU_DT)
        dx_ref[...] = ALPHA * dr
        red_ref[0:1, :] += _rsum(dy * xhat)
        red_ref[1:2, :] += _rsum(dy)
        red_ref[2:3, :] += _rsum(dr * ff_)
        red_ref[3:4, :] += jnp.sum(_rsum(err * err), axis=1, keepdims=True) * (0.5 / D)

    return pl.pallas_call(
        body, name="ln2_loss", grid=(s // tm,),
        in_specs=[_rows(tm, D)] * 3 + [_whole((8, D)), _whole((1, D)), _whole((1, D))],
        out_specs=[_rows(tm, D), _rows(tm, D), _whole((8, D))],
        out_shape=[jax.ShapeDtypeStruct((s, D), MXU_DT), jax.ShapeDtypeStruct((s, D), F32),
                   jax.ShapeDtypeStruct((8, D), F32)],
        compiler_params=_params(("arbitrary",)),
    )(x2, ff, tgt, adam, lng, lnb)


def _ffn_bwd(dact, gu, cw, cb):
    s = dact.shape[0]
    tm = _pick(s, (256, 128))
    nt = s // tm
    w = D_FF

    def body(da_ref, g_ref, gb_ref, u_ref, cw_ref, cb_ref, o_ref, gcw_ref, gcb_ref, nxt):
        i = pl.program_id(0)

        @pl.when(i == 0)
        def _():
            nxt[...] = jnp.zeros_like(nxt)
            gcw_ref[...] = jnp.zeros_like(gcw_ref)
            gcb_ref[...] = jnp.zeros_like(gcb_ref)
        g = g_ref[...]
        before = jnp.where(i < nt - 1, gb_ref[...], 0.0)
        taps = _taps(g, before, 3)
        gc = _conv_taps(taps, cw_ref) + cb_ref[...]
        gel, dgel = _gelu_and_grad(gc)
        da = da_ref[...]
        dgc = da * u_ref[...] * dgel
        o_ref[:, w:] = (da * gel).astype(MXU_DT)
        o_ref[:, :w] = _conv_causal_bwd(dgc, nxt[...], cw_ref, 3).astype(MXU_DT)
        nxt[...] = dgc[:8, :]
        for k in range(3):
            gcw_ref[k:k + 1, :] += _rsum(dgc * taps[k])
        gcb_ref[...] += _rsum(dgc)

    return pl.pallas_call(
        body, name="ffn_bwd", grid=(nt,),
        in_specs=[_rows(tm, w, 0, nt), _rows(tm, w, 0, nt), _before(tm, w, 0, nt), _rows(tm, w, 1, nt),
                  _whole((8, w)), _whole((1, w))],
        out_specs=[_rows(tm, 2 * w, 0, nt), _whole((8, w)), _whole((1, w))],
        out_shape=[jax.ShapeDtypeStruct((s, 2 * w), MXU_DT), jax.ShapeDtypeStruct((8, w), F32),
                   jax.ShapeDtypeStruct((1, w), F32)],
        scratch_shapes=[pltpu.VMEM((8, w), F32)],
        compiler_params=_params(("arbitrary",)),
    )(dact, gu, gu, gu, cw, cb)


def _ln1_bwd(dh2, dx2a, x2, r1, mix, adam, lng):
    s = dh2.shape[0]
    tm = _pick(s, (512, 256, 128))

    def body(dh_ref, dxa_ref, x2_ref, r_ref, m_ref, ada_ref, g_ref, dm_ref, dx_ref, red_ref):
        @pl.when(pl.program_id(0) == 0)
        def _():
            red_ref[...] = jnp.zeros_like(red_ref)
        dh = dh_ref[...]
        dx2 = dxa_ref[...] + dh * (1.0 + ada_ref[4:5, :])
        xhat, rstd = _ln_stats(r_ref[...])
        dr = _ln_bwd(dx2, xhat, rstd, g_ref[...])
        dm_ref[...] = ((1.0 + ada_ref[2:3, :]) * dr).astype(MXU_DT)
        dx_ref[...] = ALPHA * dr
        red_ref[0:1, :] += _rsum(dh * x2_ref[...])
        red_ref[1:2, :] += _rsum(dh)
        red_ref[2:3, :] += _rsum(dx2 * xhat)
        red_ref[3:4, :] += _rsum(dx2)
        red_ref[4:5, :] += _rsum(dr * m_ref[...])

    return pl.pallas_call(
        body, name="ln1_bwd", grid=(s // tm,),
        in_specs=[_rows(tm, D)] * 5 + [_whole((8, D)), _whole((1, D))],
        out_specs=[_rows(tm, D), _rows(tm, D), _whole((8, D))],
        out_shape=[jax.ShapeDtypeStruct((s, D), MXU_DT), jax.ShapeDtypeStruct((s, D), F32),
                   jax.ShapeDtypeStruct((8, D), F32)],
        compiler_params=_params(("arbitrary",)),
    )(dh2, dx2a, x2, r1, mix, adam, lng)


def _merge_bwd(dmg, pg, ya, yb):
    s = dmg.shape[0]
    tm = _pick(s, (512, 256, 128))

    def body(d_ref, ga_ref, gb_ref, ya_ref, yb_ref, dya_ref, dyb_ref, dpg_ref):
        d = d_ref[...]
        sa = _sigmoid(ga_ref[...])
        sb = _sigmoid(gb_ref[...])
        dya_ref[...] = (d * sa).astype(MXU_DT)
        dyb_ref[...] = (d * sb).astype(MXU_DT)
        dpg_ref[:, :D] = (d * ya_ref[...] * sa * (1.0 - sa)).astype(MXU_DT)
        dpg_ref[:, D:] = (d * yb_ref[...] * sb * (1.0 - sb)).astype(MXU_DT)

    return pl.pallas_call(
        body, name="merge_bwd", grid=(s // tm,),
        in_specs=[_rows(tm, D), _rows(tm, D, 0), _rows(tm, D, 1), _rows(tm, D), _rows(tm, D)],
        out_specs=[_rows(tm, D), _rows(tm, D), _rows(tm, 2 * D)],
        out_shape=[jax.ShapeDtypeStruct((s, D), MXU_DT), jax.ShapeDtypeStruct((s, D), MXU_DT),
                   jax.ShapeDtypeStruct((s, 2 * D), MXU_DT)],
        compiler_params=_params(("parallel",)),
    )(dmg, pg, pg, ya, yb)


def _dn_post_bwd(ddn, o, pz, nw):
    s = o.shape[0]
    tm = _pick(s, (256, 128))

    def body(d_ref, o_ref, z_ref, nw_ref, do_ref, dz_ref, gnw_ref):
        @pl.when(pl.program_id(0) == 0)
        def _():
            gnw_ref[...] = jnp.zeros_like(gnw_ref)
        acc = jnp.zeros((1, HD), F32)
        for h in range(NV):
            cs = slice(h * HD, (h + 1) * HD)
            oh = o_ref[:, cs]
            z = z_ref[:, cs]
            d = d_ref[:, cs]
            sg = _sigmoid(z)
            rs = lax.rsqrt(jnp.mean(oh * oh, axis=-1, keepdims=True) + RMS_EPS)
            n = oh * rs
            dz_ref[:, cs] = (d * n * nw_ref[...] * sg * (1.0 + z * (1.0 - sg))).astype(MXU_DT)
            dn_ = d * (z * sg)
            acc = acc + _rsum(dn_ * n)
            dnn = dn_ * nw_ref[...]
            do_ref[:, cs] = rs * (dnn - n * jnp.mean(dnn * n, axis=-1, keepdims=True))
        gnw_ref[...] += acc

    return pl.pallas_call(
        body, name="dn_post_bwd", grid=(s // tm,),
        in_specs=[_rows(tm, NV * HD)] * 3 + [_whole((1, HD))],
        out_specs=[_rows(tm, NV * HD), _rows(tm, NV * HD), _whole((1, HD))],
        out_shape=[jax.ShapeDtypeStruct((s, NV * HD), F32), jax.ShapeDtypeStruct((s, NV * HD), MXU_DT),
                   jax.ShapeDtypeStruct((1, HD), F32)],
        compiler_params=_params(("arbitrary",)),
    )(ddn, o, pz, nw)


def _dn_seq_bwd(do, qd, kd, p, w, vn, ssave, gam):
    s = do.shape[0]
    nb = s // (2 * CH)
    wide = pl.BlockSpec((2 * CH, NV * HD), lambda i: (nb - 1 - i, 0))
    gspec = pl.BlockSpec((1, NQ, 8, 128), lambda i: (nb - 1 - i, 0, 0, 0))

    def body(do_ref, qd_ref, kd_ref, p_ref, w_ref, vn_ref, ss_ref, gam_ref, dvn_ref, dkd_ref, dgam_ref, dst):
        @pl.when(pl.program_id(0) == 0)
        def _():
            dst[...] = jnp.zeros_like(dst)
        dgam_ref[...] = jnp.zeros_like(dgam_ref)
        ra, rb = slice(0, CH), slice(CH, 2 * CH)
        tot = lambda t: jnp.sum(jnp.sum(t, axis=1, keepdims=True), axis=0, keepdims=True)
        for g0 in range(0, NV, SEQ_GROUP):
            hs = list(range(g0, g0 + SEQ_GROUP))
            cs = [slice(h * HD, (h + 1) * HD) for h in hs]
            ga = [gam_ref[0, h // 2, 2 * (h % 2):2 * (h % 2) + 1, :] for h in hs]
            gb = [gam_ref[0, h // 2, 2 * (h % 2) + 1:2 * (h % 2) + 2, :] for h in hs]
            ds2 = [dst[h] for h in hs]
            pdo = [_dot_tn(p_ref[:, c], do_ref[:, c]) for c in cs]
            qdo_b = [_dot_tn(qd_ref[rb, c], do_ref[rb, c]) for c in cs]
            qdo_a = [_dot_tn(qd_ref[ra, c], do_ref[ra, c]) for c in cs]
            dvb = [p_[rb] + _dot(kd_ref[rb, c], d_) for p_, c, d_ in zip(pdo, cs, ds2)]
            ds1 = [g * d_ + q_ - _dot_tn(w_ref[rb, c], v_)
                   for g, d_, q_, c, v_ in zip(gb, ds2, qdo_b, cs, dvb)]
            dva = [p_[ra] + _dot(kd_ref[ra, c], d_) for p_, c, d_ in zip(pdo, cs, ds1)]
            ds0 = [g * d_ + q_ - _dot_tn(w_ref[ra, c], v_)
                   for g, d_, q_, c, v_ in zip(ga, ds1, qdo_a, cs, dva)]
            for h, d_ in zip(hs, ds0):
                dst[h] = d_
            for i_, h in enumerate(hs):
                c = cs[i_]
                row = 2 * (h % 2)
                dkd_ref[rb, c] = _dot_nt(vn_ref[rb, c], ds2[i_])
                dkd_ref[ra, c] = _dot_nt(vn_ref[ra, c], ds1[i_])
                dvn_ref[ra, c] = dva[i_]
                dvn_ref[rb, c] = dvb[i_]
                dgam_ref[0, h // 2, row:row + 1, :] = jnp.broadcast_to(tot(ds1[i_] * ss_ref[h, 0:HD, :]), (1, 128))
                dgam_ref[0, h // 2, row + 1:row + 2, :] = jnp.broadcast_to(
                    tot(ds2[i_] * ss_ref[h, HD:2 * HD, :]), (1, 128))

    big = jax.ShapeDtypeStruct((s, NV * HD), F32)
    return pl.pallas_call(
        body, name="dn_seq_bwd", grid=(nb,),
        in_specs=[wide] * 6 + [pl.BlockSpec((NV, 2 * HD, HD), lambda i: (0, nb - 1 - i, 0)), gspec],
        out_specs=[wide, wide, gspec],
        out_shape=[big, big, jax.ShapeDtypeStruct((nb, NQ, 8, 128), F32)],
        scratch_shapes=[pltpu.VMEM((NV, HD, HD), F32)],
        compiler_params=_params(("arbitrary",)),
    )(do, qd, kd, p, w, vn, ssave, gam)


def _dn_intra_bwd(q, k, v, rt, do, dvn, dkd, gam, dgam, ssave, tinv, u, w, vn):
    s = q.shape[0]
    nb = s // (2 * CH)
    qps = 4
    nh = 2 * qps
    blk = pl.BlockSpec((2 * CH, qps * HD), lambda i, h: (i, h))
    blk2 = pl.BlockSpec((2 * CH, nh * HD), lambda i, h: (i, h))
    gspec = pl.BlockSpec((1, qps, 8, 128), lambda i, h: (i, h, 0, 0))
    rspec = pl.BlockSpec((3 * NV, 2 * CH), lambda i, h: (0, i))

    def body(q_ref, k_ref, v_ref, rt_ref, do_ref, dvn_ref, dkd_ref, gam_ref, dgam_ref, ss_ref,
             ti_ref, u_ref, w_ref, vn_ref, dq_ref, dk_ref, dv_ref, drt_ref, acc):
        hstep = pl.program_id(1)

        @pl.when(hstep == 0)
        def _():
            acc[...] = jnp.zeros_like(acc)
        mc, ms = _pair_masks()
        ra, rb = slice(0, CH), slice(CH, 2 * CH)
        lane = _iota((1, 2 * CH), 1)
        hs = list(range(nh))
        qh = [h // 2 for h in hs]
        cs = [slice(h * HD, (h + 1) * HD) for h in hs]
        qq_ = [q_ref[:, t * HD:(t + 1) * HD] for t in range(qps)]
        kk_ = [k_ref[:, t * HD:(t + 1) * HD] for t in range(qps)]
        kk = [_dot_nt(k_, k_) for k_ in kk_]
        qk = [_dot_nt(q_, k_) for q_, k_ in zip(qq_, kk_)]
        cols = [_head_cols(rt_ref, nh * hstep + h) for h in hs]
        b_c = [c_[2] for c_ in cols]
        dec = [jnp.where(mc, jnp.exp(jnp.where(mc, c_[1] - c_[0], 0.0)), 0.0) for c_ in cols]
        eg = [jnp.exp(c_[1]) for c_ in cols]
        egl = [jnp.exp(c_[4] - c_[1]) for c_ in cols]
        dob = [do_ref[:, c] for c in cs]
        dvb = [dvn_ref[:, c] for c in cs]
        dqd = [jnp.concatenate([_dot_nt(d_[ra], ss_ref[h, 0:HD, :]), _dot_nt(d_[rb], ss_ref[h, HD:2 * HD, :])], axis=0)
               for h, d_ in zip(hs, dob)]
        dw = [-jnp.concatenate([_dot_nt(d_[ra], ss_ref[h, 0:HD, :]), _dot_nt(d_[rb], ss_ref[h, HD:2 * HD, :])], axis=0)
              for h, d_ in zip(hs, dvb)]
        dp = [jnp.where(mc, _dot_nt(d_, vn_ref[:, c]), 0.0) for d_, c in zip(dob, cs)]
        dbuw = [_dot3_tn(ti_ref[:, c], jnp.concatenate([d_, w_], axis=1)) for c, d_, w_ in zip(cs, dvb, dw)]
        dbu = [t[:, :HD] for t in dbuw]
        dbw = [t[:, HD:] for t in dbuw]
        da = [jnp.where(ms, -(_dot_nt(bu, u_ref[:, c]) + _dot_nt(bw, w_ref[:, c])), 0.0)
              for bu, bw, c in zip(dbu, dbw, cs)]
        dm = [a_ * d_ for a_, d_ in zip(da, dec)]
        dn_ = [p_ * d_ for p_, d_ in zip(dp, dec)]
        dbk = [_dot(m_, kk_[t]) for m_, t in zip(dm, qh)]
        dqs = [_dot(n_, kk_[t]) + e_ * q_ for n_, t, e_, q_ in zip(dn_, qh, eg, dqd)]
        dks = [_dot_tn(m_, b_ * kk_[t]) + _dot_tn(n_, qq_[t]) + el * dkd_ref[:, c] + b_ * (e_ * bw + bk)
               for m_, b_, t, n_, el, c, e_, bw, bk in zip(dm, b_c, qh, dn_, egl, cs, eg, dbw, dbk)]
        for t in range(qps):
            dq_ref[:, t * HD:(t + 1) * HD] = dqs[2 * t] + dqs[2 * t + 1]
            dk_ref[:, t * HD:(t + 1) * HD] = dks[2 * t] + dks[2 * t + 1]
        for h in hs:
            c, t, j = cs[h], qh[h], h % 2
            dv_ref[:, c] = b_c[h] * dbu[h]
            e = da[h] * (b_c[h] * kk[t] * dec[h]) + dp[h] * (qk[t] * dec[h])
            x = dkd_ref[:, c] * (egl[h] * kk_[t])
            egk = eg[h] * kk_[t]
            z = e + dqd[h] * (eg[h] * qq_[t]) - x + dbw[h] * (b_c[h] * egk)
            zb = dbw[h] * egk + dbu[h] * v_ref[:, c] + dbk[h] * kk_[t]
            sa = jnp.sum(jnp.sum(x[ra], axis=1, keepdims=True), axis=0, keepdims=True)
            sb = jnp.sum(jnp.sum(x[rb], axis=1, keepdims=True), axis=0, keepdims=True)
            la = sa + dgam_ref[0, t, 2 * j:2 * j + 1, :] * gam_ref[0, t, 2 * j:2 * j + 1, :]
            lb = sb + dgam_ref[0, t, 2 * j + 1:2 * j + 2, :] * gam_ref[0, t, 2 * j + 1:2 * j + 2, :]
            hg = nh * hstep + h
            acc[pl.ds(hg, 1), :] = _rsum(z.T - e)
            acc[pl.ds(NV + hg, 1), :] = _rsum(zb.T)
            acc[pl.ds(2 * NV + hg, 1), :] = jnp.where(lane < CH, la, lb)

        @pl.when(hstep == NQ // qps - 1)
        def _():
            drt_ref[...] = acc[...]

    return pl.pallas_call(
        body, name="dn_intra_bwd", grid=(nb, NQ // qps),
        in_specs=[blk, blk, blk2, rspec, blk2, blk2, blk2, gspec, gspec,
                  pl.BlockSpec((nh, 2 * HD, HD), lambda i, h: (h, i, 0)), blk2, blk2, blk2, blk2],
        out_specs=[blk, blk, blk2, rspec],
        out_shape=[jax.ShapeDtypeStruct((s, NQ * HD), F32), jax.ShapeDtypeStruct((s, NQ * HD), F32),
                   jax.ShapeDtypeStruct((s, NV * HD), F32), jax.ShapeDtypeStruct((3 * NV, s), F32)],
        scratch_shapes=[pltpu.VMEM((3 * NV, 2 * CH), F32)],
        compiler_params=_params(("parallel", "arbitrary")),
    )(q, k, v, rt, do, dvn, dkd, gam, dgam, ssave, tinv, u, w, vn)


def _l2n_heads_bwd(c, d_ref, dc_ref, off, scale):
    for hh in range(NQ):
        cs = slice(off + hh * HD, off + (hh + 1) * HD)
        x = c[:, cs]
        dy = d_ref[:, hh * HD:(hh + 1) * HD]
        r = lax.rsqrt(jnp.sum(x * x, axis=-1, keepdims=True) + L2_EPS)
        dc_ref[:, cs] = (scale * r) * (dy - x * (r * r) * jnp.sum(dy * x, axis=-1, keepdims=True))


def _dn_prep_bwd(dq, dk, dv, drt, pqkv, pab, cw, arow, drow, acol, dcol):
    s = pqkv.shape[0]
    tm = 128
    nt = s // tm
    wq = NQ * HD

    def body(dq_ref, dk_ref, dv_ref, drt_ref, x_ref, xb_ref, ab_ref, cw_ref, ar_ref, dr_ref, ac_ref,
             dc_ref, dx_ref, dab_ref, gcw_ref, gsc_ref, dcs, nxt):
        i = pl.program_id(0)

        @pl.when(i == 0)
        def _():
            nxt[...] = jnp.zeros_like(nxt)
            gcw_ref[...] = jnp.zeros_like(gcw_ref)
            gsc_ref[...] = jnp.zeros_like(gsc_ref)
        x = x_ref[...]
        before = jnp.where(i < nt - 1, xb_ref[...], 0.0)
        taps = _taps(x, before, 4)
        cp = _conv_taps(taps, cw_ref)
        sg = _sigmoid(cp)
        c = cp * sg
        _l2n_heads_bwd(c, dq_ref, dcs, 0, Q_SCALE)
        _l2n_heads_bwd(c, dk_ref, dcs, wq, 1.0)
        dcs[:, 2 * wq:] = dv_ref[...]
        dcp = dcs[...] * (sg * (1.0 + cp * (1.0 - sg)))
        dx_ref[...] = _conv_causal_bwd(dcp, nxt[...], cw_ref, 4).astype(MXU_DT)
        nxt[...] = dcp[:8, :]
        for kq in range(4):
            gcw_ref[kq:kq + 1, :] += _rsum(dcp * taps[kq])
        lane = _iota((NV, tm), 1)
        dgt = drt_ref[0:NV, :] + jnp.where((lane & (CH - 1)) == CH - 1, drt_ref[2 * NV:3 * NV, :], 0.0)
        dg = _chunk_cumsum(dgt, 1, rev=True)
        abt = ab_ref[...].T
        zt = abt[0:NV, :] + dc_ref[...]
        gt = -jnp.exp(ac_ref[...]) * _softplus(zt)
        dat = dg * (-jnp.exp(ac_ref[...])) * _sigmoid(zt)
        bt = _sigmoid(abt[NV:2 * NV, :])
        dbt = drt_ref[NV:2 * NV, :] * bt * (1.0 - bt)
        full = jnp.concatenate([dat, dbt, jnp.zeros((128 - 2 * NV, tm), F32)], axis=0)
        dab_ref[...] = full.T.astype(MXU_DT)
        l2 = _iota((NV, 128), 1)
        gsc_ref[...] += jnp.where(l2 == 0, jnp.sum(dg * gt, axis=1, keepdims=True),
                                  jnp.where(l2 == 1, jnp.sum(dat, axis=1, keepdims=True), 0.0))

    return pl.pallas_call(
        body, name="dn_prep_bwd", grid=(nt,),
        in_specs=[_rows(tm, wq, 0, nt), _rows(tm, wq, 0, nt), _rows(tm, 2 * wq, 0, nt),
                  pl.BlockSpec((3 * NV, tm), lambda i: (0, nt - 1 - i)),
                  _rows(tm, 4 * wq, 0, nt), _before(tm, 4 * wq, 0, nt), _rows(tm, 128, 0, nt),
                  _whole((8, 4 * wq)), _whole((1, 128)), _whole((1, 128)), _whole((NV, 1)), _whole((NV, 1))],
        out_specs=[_rows(tm, 4 * wq, 0, nt), _rows(tm, 128, 0, nt), _whole((8, 4 * wq)), _whole((NV, 128))],
        out_shape=[jax.ShapeDtypeStruct((s, 4 * wq), MXU_DT), jax.ShapeDtypeStruct((s, 128), MXU_DT),
                   jax.ShapeDtypeStruct((8, 4 * wq), F32), jax.ShapeDtypeStruct((NV, 128), F32)],
        scratch_shapes=[pltpu.VMEM((tm, 4 * wq), F32), pltpu.VMEM((8, 4 * wq), F32)],
        compiler_params=_params(("arbitrary",)),
    )(dq, dk, dv, drt, pqkv, pqkv, pab, cw, arow, drow, acol, dcol)


def _rg_bwd(drec, px, h, cw, vec, wa, wx, wat, wxt, side=None):
    s = px.shape[0]
    tm = _pick(s, (256, 128))
    nt = s // tm
    w = D_RNN
    n_in = len(side.ins) if side else 0
    n_out = len(side.out_shapes) if side else 0

    def body(*refs):
        (dr_ref, xr_ref, xb_ref, gr_ref, h_ref, hb_ref, cw_ref, vec_ref, wa_ref, wx_ref, wat_ref,
         wxt_ref) = refs[:12]
        s_in = refs[12:12 + n_in]
        o_ref, gwa_ref, gwx_ref, gcw_ref, gvec_ref = refs[12 + n_in:17 + n_in]
        s_out = refs[17 + n_in:17 + n_in + n_out]
        nxt_a, nxt_l, nxt_d = refs[17 + n_in + n_out:20 + n_in + n_out]
        sems = refs[20 + n_in + n_out:]
        i = pl.program_id(0)
        if side:
            @pl.when(i == 0)
            def _():
                side.start(s_in, s_out, sems)

        @pl.when(i == 0)
        def _():
            nxt_a[...] = jnp.zeros_like(nxt_a)
            nxt_l[...] = jnp.zeros_like(nxt_l)
            nxt_d[...] = jnp.zeros_like(nxt_d)
            gwa_ref[...] = jnp.zeros_like(gwa_ref)
            gwx_ref[...] = jnp.zeros_like(gwx_ref)
            gcw_ref[...] = jnp.zeros_like(gcw_ref)
            gvec_ref[...] = jnp.zeros_like(gvec_ref)
        first = i == nt - 1
        x = xr_ref[...]
        xbefore = jnp.where(first, 0.0, xb_ref[...])
        hbefore = jnp.where(first, 0.0, hb_ref[...])
        taps = _taps(x, xbefore, 4)
        xc = _conv_taps(taps, cw_ref) + vec_ref[0:1, :]
        r, ig, a, mult, sp, rmult = _rg_gates(xc, wa_ref, wx_ref, vec_ref)
        hh = h_ref[...]
        gel, dgel = _gelu_and_grad(gr_ref[...])
        drec_ = dr_ref[...]
        o_ref[:, w:] = (drec_ * hh * dgel).astype(MXU_DT)
        acum, lam = _scan_rev(_shift_up(a, nxt_a[...], 1), drec_ * gel)
        lam = lam + acum * nxt_l[0:1, :]
        nxt_a[...] = a[:8, :]
        nxt_l[...] = lam[:8, :]
        da = lam * _shift_down(hh, hbefore, 1)
        dxc = lam * mult * ig
        dla = da * a - (lam * ig * xc) * (a * a) * rmult
        dpr = dla * (-RG_C * sp) * r * (1.0 - r)
        dpi = (lam * mult * xc) * ig * (1.0 - ig)
        dprb = dpr.astype(MXU_DT)
        dpib = dpi.astype(MXU_DT)
        dxc = dxc + jnp.dot(dprb, wat_ref[...], preferred_element_type=F32) \
                  + jnp.dot(dpib, wxt_ref[...], preferred_element_type=F32)
        xcb = xc.astype(MXU_DT)
        gwa_ref[...] += _dot_tn(xcb, dprb)
        gwx_ref[...] += _dot_tn(xcb, dpib)
        o_ref[:, :w] = _conv_causal_bwd(dxc, nxt_d[...], cw_ref, 4).astype(MXU_DT)
        nxt_d[...] = dxc[:8, :]
        for kq in range(4):
            gcw_ref[kq:kq + 1, :] += _rsum(dxc * taps[kq])
        gvec_ref[0:1, :] += _rsum(dxc)
        gvec_ref[1:2, :] += _rsum(dpr)
        gvec_ref[2:3, :] += _rsum(dpi)
        gvec_ref[3:4, :] += _rsum(dla * (-RG_C * r)) * (-_sigmoid(-vec_ref[3:4, :]))
        if side:
            @pl.when(i == nt - 1)
            def _():
                side.finish(s_in, s_out, sems)

    return pl.pallas_call(
        body, name="rg_bwd", grid=(nt,),
        in_specs=[_rows(tm, w, 0, nt), _rows(tm, w, 0, nt), _before(tm, w, 0, nt), _rows(tm, w, 1, nt),
                  _rows(tm, w, 0, nt), _before(tm, w, 0, nt), _whole((8, w)), _whole((8, w)),
                  _whole((w, w)), _whole((w, w)), _whole((w, w)), _whole((w, w))] + [_HBM] * n_in,
        out_specs=[_rows(tm, 2 * w, 0, nt), _whole((w, w)), _whole((w, w)), _whole((8, w)), _whole((8, w))]
        + [_HBM] * n_out,
        out_shape=[jax.ShapeDtypeStruct((s, 2 * w), MXU_DT), jax.ShapeDtypeStruct((w, w), F32),
                   jax.ShapeDtypeStruct((w, w), F32), jax.ShapeDtypeStruct((8, w), F32),
                   jax.ShapeDtypeStruct((8, w), F32)] + (list(side.out_shapes) if side else []),
        scratch_shapes=[pltpu.VMEM((8, w), F32)] * 3 + (list(side.sems) if side else []),
        compiler_params=_params(("arbitrary",)),
    )(drec, px, px, px, h, h, cw, vec, wa, wx, wat, wxt, *(side.ins if side else []))


def _modulate_bwd(dh1, dxa, x, adam):
    s = x.shape[0]
    tm = _pick(s, (512, 256, 128))

    def body(dh_ref, dxa_ref, x_ref, ada_ref, gx_ref, red_ref):
        @pl.when(pl.program_id(0) == 0)
        def _():
            red_ref[...] = jnp.zeros_like(red_ref)
        dh = dh_ref[...]
        gx_ref[...] = dxa_ref[...] + dh * (1.0 + ada_ref[1:2, :])
        red_ref[0:1, :] += _rsum(dh * x_ref[...])
        red_ref[1:2, :] += _rsum(dh)

    return pl.pallas_call(
        body, name="modulate1_bwd", grid=(s // tm,),
        in_specs=[_rows(tm, D)] * 3 + [_whole((8, D))],
        out_specs=[_rows(tm, D), _whole((8, D))],
        out_shape=[jax.ShapeDtypeStruct((s, D), F32), jax.ShapeDtypeStruct((8, D), F32)],
        compiler_params=_params(("arbitrary",)),
    )(dh1, dxa, x, adam)


def _adamw(parts, w, m, v, name):
    r, c = w.shape
    n_parts = len(parts)
    tm = _row_tile(r, c * 4 * (n_parts + 7))
    c1 = 1.0 - ADAM_B1 ** ADAM_STEP
    c2 = 1.0 - ADAM_B2 ** ADAM_STEP

    def body(*refs):
        g = refs[0][...]
        for p_ref in refs[1:n_parts]:
            g = g + p_ref[...]
        w_ref, m_ref, v_ref, g_out, d_out, m_out, v_out = refs[n_parts:]
        mn = ADAM_B1 * m_ref[...] + (1.0 - ADAM_B1) * g
        vn = ADAM_B2 * v_ref[...] + (1.0 - ADAM_B2) * (g * g)
        g_out[...] = g
        m_out[...] = mn
        v_out[...] = vn
        d_out[...] = -ADAM_LR * ((mn / c1) / (jnp.sqrt(vn / c2) + ADAM_EPS) + ADAM_WD * w_ref[...])

    spec = pl.BlockSpec((tm, c), lambda i: (i, 0))
    return pl.pallas_call(
        body, name=name, grid=(r // tm,),
        in_specs=[spec] * (n_parts + 3), out_specs=[spec] * 4,
        out_shape=[jax.ShapeDtypeStruct((r, c), F32)] * 4, compiler_params=_params(("parallel",)),
    )(*parts, w, m, v)


def _row_tile(r, bytes_per_row):
    for t in (512, 256, 128, 64, 32, 16):
        if r % t == 0 and 2 * t * bytes_per_row <= 20 * 1024 * 1024:
            return t
    return _pick(r, (16, 8))


def _sum_partials(own, recv, name):
    r, c = own.shape
    tm = _row_tile(r, c * (4 + 4 + 3 * 2))

    def body(o_ref, r_ref, out_ref):
        out_ref[...] = ((o_ref[...] + r_ref[0].astype(F32)) + r_ref[1].astype(F32)) + r_ref[2].astype(F32)

    return pl.pallas_call(
        body, name=name, grid=(r // tm,),
        in_specs=[pl.BlockSpec((tm, c), lambda i: (i, 0)), pl.BlockSpec((3, tm, c), lambda i: (0, i, 0))],
        out_specs=pl.BlockSpec((tm, c), lambda i: (i, 0)),
        out_shape=jax.ShapeDtypeStruct((r, c), F32), compiler_params=_params(("parallel",)),
    )(own, recv)


def _sum8(g):
    _, r, c = g.shape
    tm = _pick(r, (256, 128, 64, 32, 16, 8))

    def body(g_ref, o_ref):
        acc = g_ref[0]
        for k in range(1, 8):
            acc = acc + g_ref[k]
        o_ref[...] = acc

    return pl.pallas_call(
        body, name="sum8", grid=(r // tm,),
        in_specs=[pl.BlockSpec((8, tm, c), lambda i: (0, i, 0))],
        out_specs=pl.BlockSpec((tm, c), lambda i: (i, 0)),
        out_shape=jax.ShapeDtypeStruct((r, c), F32), compiler_params=_params(("parallel",)),
    )(g)


def _silu_rows(x):
    def body(x_ref, o_ref):
        xx = x_ref[...]
        o_ref[...] = xx * _sigmoid(xx)

    return pl.pallas_call(body, name="silu_rows", out_shape=jax.ShapeDtypeStruct(x.shape, F32))(x)


def _coords():
    return lax.axis_index("x"), lax.axis_index("y"), lax.axis_index("c")


def _allgather8(v, name):
    r, n = v.shape

    def body(v_ref, out_ref, send_sems, recv_sems):
        x, y, c = _coords()
        me = 4 * x + 2 * y + c
        out_ref[me] = v_ref[...]
        peers = []
        for k in range(1, 8):
            px = 1 - x if k & 4 else x
            py = 1 - y if k & 2 else y
            pc = 1 - c if k & 1 else c
            peers.append((px, py, pc))

        def copy(k, slot, to):
            return pltpu.make_async_remote_copy(
                src_ref=v_ref, dst_ref=out_ref.at[slot], send_sem=send_sems.at[k], recv_sem=recv_sems.at[k],
                device_id=to, device_id_type=MESH)

        sends = [copy(k, me, p) for k, p in enumerate(peers)]
        for cp in sends:
            cp.start()
        for k, (px, py, pc) in enumerate(peers):
            copy(k, 4 * px + 2 * py + pc, (px, py, pc)).wait_recv()
        for cp in sends:
            cp.wait_send()

    return pl.pallas_call(
        body, name=name, out_shape=jax.ShapeDtypeStruct((8, r, n), v.dtype),
        in_specs=[pl.BlockSpec(memory_space=pltpu.VMEM)], out_specs=pl.BlockSpec(memory_space=pltpu.VMEM),
        scratch_shapes=[pltpu.SemaphoreType.DMA((7,)), pltpu.SemaphoreType.DMA((7,))],
        compiler_params=pltpu.CompilerParams(vmem_limit_bytes=VMEM_LIMIT),
    )(v)


def _other_chips(x, y):
    return [(1 - x, y), (x, 1 - y), (1 - x, 1 - y)]


_HBM = pl.BlockSpec(memory_space=pl.ANY)


class _Exchange:
    def __init__(self, ins, out_shapes, sems, start, finish):
        self.ins, self.out_shapes, self.sems, self.start, self.finish = ins, out_shapes, sems, start, finish


def _run_exchange(ex, name):
    n_in, n_out = len(ex.ins), len(ex.out_shapes)

    def body(*refs):
        ins, outs, sems = refs[:n_in], refs[n_in:n_in + n_out], refs[n_in + n_out:]
        ex.start(ins, outs, sems)
        ex.finish(ins, outs, sems)

    return pl.pallas_call(body, name=name, out_shape=list(ex.out_shapes), in_specs=[_HBM] * n_in,
                          out_specs=[_HBM] * n_out, scratch_shapes=list(ex.sems))(*ex.ins)


def _gather_exchange(shards):
    n = len(shards)

    def plan(ins, outs, sems):
        send_sems, recv_sems, local_sems = sems
        x, y, c = _coords()
        s_me = 2 * x + y
        chips = _other_chips(x, y)

        def half(i, slot, hc):
            rh = shards[i].shape[0] // 2
            return outs[i].at[slot, pl.ds(pl.multiple_of(hc * rh, 16), rh), :]

        def copy(i, k, src, dst, to):
            return pltpu.make_async_remote_copy(src_ref=src, dst_ref=dst, send_sem=send_sems.at[6 * i + k],
                                                recv_sem=recv_sems.at[6 * i + k], device_id=to, device_id_type=MESH)

        local = [pltpu.make_async_copy(ins[i], outs[i].at[s_me], local_sems.at[i]) for i in range(n)]
        first = []
        for i in range(n):
            rh = shards[i].shape[0] // 2
            my_half = ins[i].at[pl.ds(pl.multiple_of(c * rh, 16), rh), :]
            first += [copy(i, j, my_half, half(i, s_me, c), (px, py, c)) for j, (px, py) in enumerate(chips)]
        return (x, y, c), chips, half, copy, local, first

    def start(ins, outs, sems):
        _, _, _, _, local, first = plan(ins, outs, sems)
        for cp in local + first:
            cp.start()

    def finish(ins, outs, sems):
        (x, y, c), chips, half, copy, local, first = plan(ins, outs, sems)
        sibling = (x, y, 1 - c)
        passed = []
        for i in range(n):
            for j, (px, py) in enumerate(chips):
                land = half(i, 2 * px + py, c)
                copy(i, j, land, land, (px, py, c)).wait_recv()
                fw = copy(i, 3 + j, land, land, sibling)
                fw.start()
                passed.append(fw)
        for i in range(n):
            for j, (px, py) in enumerate(chips):
                land = half(i, 2 * px + py, 1 - c)
                copy(i, 3 + j, land, land, sibling).wait_recv()
        for cp in first + passed:
            cp.wait_send()
        for cp in local:
            cp.wait()

    return _Exchange(list(shards), [jax.ShapeDtypeStruct((4,) + v.shape, v.dtype) for v in shards],
                     [pltpu.SemaphoreType.DMA((6 * n,)), pltpu.SemaphoreType.DMA((6 * n,)),
                      pltpu.SemaphoreType.DMA((n,))], start, finish)


def _scatter_exchange(gs):
    n = len(gs)

    def copies(ins, outs, sems):
        send_sems, recv_sems = sems
        x, y, c = _coords()
        return [pltpu.make_async_remote_copy(
                    src_ref=ins[i].at[2 * px + py], dst_ref=outs[i].at[j], send_sem=send_sems.at[3 * i + j],
                    recv_sem=recv_sems.at[3 * i + j], device_id=(px, py, c), device_id_type=MESH)
                for i in range(n) for j, (px, py) in enumerate(_other_chips(x, y))]

    def start(ins, outs, sems):
        for cp in copies(ins, outs, sems):
            cp.start()

    def finish(ins, outs, sems):
        cps = copies(ins, outs, sems)
        for cp in cps:
            cp.wait_recv()
        for cp in cps:
            cp.wait_send()

    return _Exchange(list(gs), [jax.ShapeDtypeStruct((3,) + g.shape[1:], g.dtype) for g in gs],
                     [pltpu.SemaphoreType.DMA((3 * n,)), pltpu.SemaphoreType.DMA((3 * n,))], start, finish)


def _swap_sibling(vs):
    n = len(vs)

    def body(*refs):
        ins, outs = refs[:n], refs[n:2 * n]
        send_sems, recv_sems = refs[2 * n:]
        x, y, c = _coords()
        cps = [pltpu.make_async_remote_copy(src_ref=ins[i], dst_ref=outs[i], send_sem=send_sems.at[i],
                                            recv_sem=recv_sems.at[i], device_id=(x, y, 1 - c), device_id_type=MESH)
               for i in range(n)]
        for cp in cps:
            cp.start()
        for cp in cps:
            cp.wait()

    return pl.pallas_call(
        body, name="swap_sibling", out_shape=[jax.ShapeDtypeStruct(v.shape, v.dtype) for v in vs],
        in_specs=[_HBM] * n, out_specs=[_HBM] * n,
        scratch_shapes=[pltpu.SemaphoreType.DMA((n,)), pltpu.SemaphoreType.DMA((n,))],
    )(*vs)


def _pad_rows(a, rows):
    return jnp.pad(a, ((0, rows - a.shape[0]), (0, 0)))


def _block_diag(w):
    eye = jnp.eye(RG_BLOCKS, dtype=w.dtype)
    return (eye[:, None, :, None] * w[:, :, None, :]).reshape(D_RNN, D_RNN)


def _diag_blocks(g):
    g4 = g.reshape(RG_BLOCKS, RG_BW, RG_BLOCKS, RG_BW)
    idx = jnp.arange(RG_BLOCKS)
    return g4[idx, :, idx, :]


def _prepare_rest(p):
    w = {}
    for k_, n_ in (("pa", "w_proj_a"), ("pb", "w_proj_b"), ("out", "w_out"), ("down", "ffn_w_down")):
        w[k_] = p[n_].astype(MXU_DT)
        w[k_ + "_t"] = w[k_].T
    w["gu"] = jnp.concatenate([p["ffn_w_gate"], p["ffn_w_up"]], axis=1).astype(MXU_DT)
    w["gu_t"] = w["gu"].T
    return w


def _prepare_first(p):
    w = {}
    wi = p["w_in"].astype(MXU_DT)
    cat = jnp.concatenate([wi[:, 2560:6656], wi[:, 6656:8704], wi[:, 8736:10784], wi[:, 0:2560],
                           wi[:, 8704:8736], jnp.zeros((D, 96), MXU_DT)], axis=1)
    w["in_cat"], w["in_cat_t"] = cat, cat.T
    w["rg_cw"] = _pad_rows(p["rg_conv_w"], 8)
    w["dn_cw"] = _pad_rows(p["dn_conv_w"], 8)
    w["ffn_cw"] = _pad_rows(p["ffn_conv_w"], 8)
    w["rg_vec"] = _pad_rows(jnp.stack([p["rg_conv_b"], p["rg_b_a"], p["rg_b_x"], p["rg_lambda"]]), 8)
    w["wa"] = _block_diag(p["rg_w_a"]).astype(MXU_DT)
    w["wx"] = _block_diag(p["rg_w_x"]).astype(MXU_DT)
    w["wa_t"], w["wx_t"] = w["wa"].T, w["wx"].T
    w["arow"] = jnp.pad(p["dn_a_log"], (0, 128 - NV))[None, :]
    w["drow"] = jnp.pad(p["dn_dt_bias"], (0, 128 - NV))[None, :]
    w["acol"] = p["dn_a_log"][:, None]
    w["dcol"] = p["dn_dt_bias"][:, None]
    w["nw"] = p["dn_norm_w"][None, :]
    w["ffn_cb"] = p["ffn_conv_b"][None, :]
    for n_ in ("ln1_g", "ln1_b", "ln2_g", "ln2_b"):
        w[n_] = p[n_][None, :]
    return w


def _mm_sided(a, b, side, **kw):
    if side is None:
        return _mm(a, b, **kw), []
    res = _mm(a, b, side=side, **kw)
    return res[0], res[1:]


_REST_A = ("w_proj_a", "w_proj_b", "w_out")
_REST_B = ("ffn_w_gate", "ffn_w_up", "ffn_w_down")


def _local_step(x, tgt, adam, p, shards=None):
    w = _prepare_first(p)
    side_a = side_b = None
    if shards is not None:
        side_a = _gather_exchange([shards[n] for n in _REST_A])
        side_b = _gather_exchange([shards[n] for n in _REST_B])
    h1 = _modulate(x, adam)
    cat = w["in_cat"]
    pqkv, got_a = _mm_sided(h1, cat[:, 0:4096], side_a, name="proj_qkv")
    px, got_b = _mm_sided(h1, cat[:, 8192:10752], side_b, name="proj_x")
    if shards is not None:
        p = dict(p, **{n: _unstack_shards(n, a_) for n, a_ in zip(_REST_A + _REST_B, got_a + got_b)})
    w.update(_prepare_rest(p))
    pz = _mm(h1, cat[:, 4096:6144], name="proj_z")
    pg = _mm(h1, cat[:, 6144:8192], name="proj_g")
    pab = _mm(h1, cat[:, 10752:10880], name="proj_ab")
    hrec, rec = _rg_fwd(px, w["rg_cw"], w["rg_vec"], w["wa"], w["wx"])
    q, k, v, rt = _dn_prep(pqkv, pab, w["dn_cw"], w["arow"], w["drow"])
    u, ww, qd, kd, pm, tinv, gam = _dn_intra(q, k, v, rt)
    o, vn, ssave = _dn_seq(u, ww, qd, kd, pm, gam)
    dn = _dn_post(o, pz, w["nw"])
    ya = _mm(rec, w["pa"], name="proj_a")
    yb = _mm(dn, w["pb"], name="proj_b")
    merged = _merge(pg, ya, yb)
    mix = _mm(merged, w["out"], name="proj_out")
    r1, x2, h2 = _ln1(x, mix, adam, w["ln1_g"], w["ln1_b"])
    gu = _mm(h2, w["gu"], name="ffn_gu")
    act = _ffn_act(gu, w["ffn_cw"], w["ffn_cb"])
    ff = _mm(act, w["down"], name="ffn_down")
    dff, dx2a, red2 = _ln2_loss(x2, ff, tgt, adam, w["ln2_g"], w["ln2_b"])
    g = {}
    dact = _mm(dff, w["down_t"], name="d_act")
    g["ffn_w_down"] = _mm(act, dff, name="g_down", trans_a=True)
    dgu, gcw_f, gcb_f = _ffn_bwd(dact, gu, w["ffn_cw"], w["ffn_cb"])
    dh2 = _mm(dgu, w["gu_t"], name="d_h2")
    ggu = _mm(h2, dgu, name="g_gu", trans_a=True)
    g["ffn_w_gate"], g["ffn_w_up"] = ggu[:, :D_FF], ggu[:, D_FF:]
    g["ffn_conv_w"], g["ffn_conv_b"] = gcw_f[0:3], gcb_f[0]
    dmix, dxa, red1 = _ln1_bwd(dh2, dx2a, x2, r1, mix, adam, w["ln1_g"])
    dmg = _mm(dmix, w["out_t"], name="d_merged")
    g["w_out"] = _mm(merged, dmix, name="g_out", trans_a=True)
    dya, dyb, dpg = _merge_bwd(dmg, pg, ya, yb)
    drec = _mm(dya, w["pa_t"], name="d_rec")
    g["w_proj_a"] = _mm(rec, dya, name="g_pa", trans_a=True)
    ddn = _mm(dyb, w["pb_t"], name="d_dn")
    g["w_proj_b"] = _mm(dn, dyb, name="g_pb", trans_a=True)
    do, dpz, gnw = _dn_post_bwd(ddn, o, pz, w["nw"])
    dvn, dkd, dgam = _dn_seq_bwd(do, qd, kd, pm, ww, vn, ssave, gam)
    dq, dk, dv, drt = _dn_intra_bwd(q, k, v, rt, do, dvn, dkd, gam, dgam, ssave, tinv, u, ww, vn)
    dpqkv, dpab, gcw_d, gsc = _dn_prep_bwd(dq, dk, dv, drt, pqkv, pab, w["dn_cw"], w["arow"], w["drow"],
                                           w["acol"], w["dcol"])
    side_r = None
    if shards is not None:
        side_r = _scatter_exchange([_stack_shards(n, g[n]).astype(MXU_DT) for n in _REST_A + _REST_B])
    dpx, gwa, gwx, gcw_r, gvec, *got_r = _rg_bwd(drec, px, hrec, w["rg_cw"], w["rg_vec"], w["wa"], w["wx"],
                                                 w["wa_t"], w["wx_t"], side_r)
    dproj = jnp.concatenate([dpqkv, dpz, dpg, dpx, dpab], axis=1)
    reorder = lambda gc: jnp.concatenate([gc[:, 8192:10752], gc[:, 0:4096], gc[:, 4096:6144], gc[:, 10752:10784],
                                          gc[:, 6144:8192]], axis=1)
    wire = lambda gh: _scatter_exchange([_stack_shards("w_in", gh).astype(MXU_DT)]) if shards is not None else None
    h1t = h1.T
    g_top = reorder(_mm(h1t[:D // 2], dproj, name="g_in_top"))
    g_bot, got_top = _mm_sided(h1t[D // 2:], dproj, wire(g_top), name="g_in_bot")
    g_bot = reorder(g_bot)
    dh1, got_bot = _mm_sided(dproj, w["in_cat_t"], wire(g_bot), name="d_h1")
    g["w_in"] = jnp.concatenate([g_top, g_bot], axis=0)
    got = dict(zip(_REST_A + _REST_B, got_r))
    if shards is not None:
        got["w_in"] = jnp.concatenate([got_top[0], got_bot[0]], axis=1)
    gx, red0 = _modulate_bwd(dh1, dxa, x, adam)
    g["rg_conv_w"], g["rg_conv_b"] = gcw_r[0:4], gvec[0]
    g["rg_w_a"], g["rg_w_x"] = _diag_blocks(gwa), _diag_blocks(gwx)
    g["rg_b_a"], g["rg_b_x"], g["rg_lambda"] = gvec[1], gvec[2], gvec[3]
    g["dn_conv_w"] = gcw_d[0:4]
    g["dn_a_log"], g["dn_dt_bias"], g["dn_norm_w"] = gsc[:, 0], gsc[:, 1], gnw[0]
    g["ln1_g"], g["ln1_b"] = red1[2], red1[3]
    g["ln2_g"], g["ln2_b"] = red2[0], red2[1]
    d_ada = jnp.concatenate([red0[1], red0[0], red1[4], red1[1], red1[0], red2[2]])
    return red2[3, 0], gx, g, d_ada, got


_BIG = ("w_in", "w_proj_a", "w_proj_b", "w_out", "ffn_w_gate", "ffn_w_up", "ffn_w_down")
_COL_SHARDED = ("w_in", "ffn_w_gate", "ffn_w_up")
_CONV = ("rg_conv_w", "dn_conv_w", "ffn_conv_w")
_REPL = ("b_ada", "rg_conv_b", "rg_w_a", "rg_b_a", "rg_w_x", "rg_b_x", "rg_lambda", "dn_a_log",
         "dn_dt_bias", "dn_norm_w", "ln1_g", "ln1_b", "ffn_conv_b", "ln2_g", "ln2_b")
_NAMES = ("w_ada", "b_ada", "w_in", "rg_conv_w", "rg_conv_b", "rg_w_a", "rg_b_a", "rg_w_x", "rg_b_x",
          "rg_lambda", "dn_conv_w", "dn_a_log", "dn_dt_bias", "dn_norm_w", "w_proj_a", "w_proj_b", "w_out",
          "ln1_g", "ln1_b", "ffn_w_gate", "ffn_w_up", "ffn_conv_w", "ffn_conv_b", "ffn_w_down", "ln2_g", "ln2_b")


def _pack(arrs, width, row_mult):
    pieces = []
    for a in arrs:
        f = a.reshape(-1)
        pieces.append(jnp.pad(f, (0, (-f.shape[0]) % width)))
    flat = jnp.concatenate(pieces)
    rows = flat.shape[0] // width
    return jnp.pad(flat, (0, ((-rows) % row_mult) * width)).reshape(-1, width)


def _unpack(flat, shapes, width):
    out, row = [], 0
    for shp in shapes:
        n = 1
        for d_ in shp:
            n *= d_
        rows = -(-n // width)
        out.append(flat[row:row + rows].reshape(-1)[:n].reshape(shp))
        row += rows
    return out


def _stack_shards(name, full):
    if name in _COL_SHARDED or name in _CONV:
        r, ncol = full.shape
        return full.reshape(r, 4, ncol // 4).transpose(1, 0, 2)
    return full.reshape((4, full.shape[0] // 4) + full.shape[1:])


def _unstack_shards(name, st):
    if name in _COL_SHARDED or name in _CONV:
        return st.transpose(1, 0, 2).reshape(st.shape[1], 4 * st.shape[2])
    return st.reshape((4 * st.shape[1],) + st.shape[2:])


def kernel(x, c, w_ada, b_ada, w_in, rg_conv_w, rg_conv_b, rg_w_a, rg_b_a, rg_w_x, rg_b_x, rg_lambda, dn_conv_w, dn_a_log, dn_dt_bias, dn_norm_w, w_proj_a, w_proj_b, w_out, ln1_g, ln1_b, ffn_w_gate, ffn_w_up, ffn_conv_w, ffn_conv_b, ffn_w_down, ln2_g, ln2_b, loss_target, m_w_ada, m_b_ada, m_w_in, m_rg_conv_w, m_rg_conv_b, m_rg_w_a, m_rg_b_a, m_rg_w_x, m_rg_b_x, m_rg_lambda, m_dn_conv_w, m_dn_a_log, m_dn_dt_bias, m_dn_norm_w, m_w_proj_a, m_w_proj_b, m_w_out, m_ln1_g, m_ln1_b, m_ffn_w_gate, m_ffn_w_up, m_ffn_conv_w, m_ffn_conv_b, m_ffn_w_down, m_ln2_g, m_ln2_b, v_w_ada, v_b_ada, v_w_in, v_rg_conv_w, v_rg_conv_b, v_rg_w_a, v_rg_b_a, v_rg_w_x, v_rg_b_x, v_rg_lambda, v_dn_conv_w, v_dn_a_log, v_dn_dt_bias, v_dn_norm_w, v_w_proj_a, v_w_proj_b, v_w_out, v_ln1_g, v_ln1_b, v_ffn_w_gate, v_ffn_w_up, v_ffn_conv_w, v_ffn_conv_b, v_ffn_w_down, v_ln2_g, v_ln2_b):
    args = locals()
    wts = {n: args[n][0] for n in _NAMES}
    mom = {n: args["m_" + n][0] for n in _NAMES}
    var = {n: args["v_" + n][0] for n in _NAMES}
    xs, tgt = x[0], loss_target[0]
    ix, iy, ic = _coords()
    shard = 2 * ix + iy
    batch = 4 * ix + 2 * iy + ic

    c_all = _allgather8(_pad_rows(c, 8), "gather_c")[:, 0, :]
    sc16 = _pad_rows(_silu_rows(c_all), 16)
    ada_cols = _mm(sc16, w_ada[0], name="ada")[:8]
    ada_g = _allgather8(ada_cols, "gather_ada")
    ada_all = jnp.concatenate([ada_g[0], ada_g[2], ada_g[4], ada_g[6]], axis=1) + b_ada
    adam = _pad_rows(lax.dynamic_index_in_dim(ada_all, batch, 0, keepdims=False).reshape(6, D), 8)

    (w_in_all,) = _run_exchange(_gather_exchange([wts["w_in"].astype(MXU_DT)]), "gather_w_in")
    full = {"w_in": _unstack_shards("w_in", w_in_all)}
    conv_shard = _pack([wts[n] for n in _CONV], 128, 8)
    conv_all = _allgather8(conv_shard, "gather_conv")
    shapes_conv = [wts[n].shape for n in _CONV]
    per_shard = [_unpack(conv_all[2 * s], shapes_conv, 128) for s in range(4)]
    for i, n in enumerate(_CONV):
        full[n] = _unstack_shards(n, jnp.stack([per_shard[s][i] for s in range(4)]))
    for n in _REPL:
        full[n] = wts[n]

    shards = {n: wts[n].astype(MXU_DT) for n in _REST_A + _REST_B}
    loss_b, gx, g, d_ada, recv = _local_step(xs, tgt, adam, full, shards)

    parts = []
    for n in _BIG:
        r_ = recv[n]
        axis = 1 if n in _COL_SHARDED else 0
        width = wts[n].shape[axis]
        own = lax.dynamic_slice_in_dim(g[n], shard * width, width, axis=axis)
        parts.append(_sum_partials(own, r_, "sum_" + n))
    parts_sib = _swap_sibling(parts)
    out = {n: [] for n in _NAMES}
    for n, p_, q_ in zip(_BIG, parts, parts_sib):
        out[n] = list(_adamw([p_, q_], wts[n], mom[n], var[n], "adamw_" + n))

    small_names = _CONV + _REPL[1:]
    small = _pack([d_ada, jnp.full((1,), loss_b, F32)] + [g[n] for n in small_names], 128, 64)
    small_all = _allgather8(small, "gather_small")
    total = _sum8(small_all)
    shapes_small = [(6 * D,), (1,)] + [full[n].shape for n in small_names]
    tot = _unpack(total, shapes_small, 128)
    gsum = dict(zip(small_names, tot[2:]))
    gsum["b_ada"] = tot[0]
    loss = tot[1][0]
    for n in _CONV:
        gsum[n] = lax.dynamic_index_in_dim(_stack_shards(n, gsum[n]), shard, 0, keepdims=False)
    d_ada_all = small_all[:, :6 * D // 128, :].reshape(8, 6 * D)
    cols = lax.dynamic_slice_in_dim(d_ada_all, shard * (6 * D // 4), 6 * D // 4, axis=1)
    g_wada = _mm(sc16, _pad_rows(cols, 16), name="g_ada", trans_a=True)
    res = _adamw([g_wada], wts["w_ada"], mom["w_ada"], var["w_ada"], "adamw_ada")
    out["w_ada"] = list(res)
    names_s = _CONV + _REPL
    shapes_s = [wts[n].shape for n in names_s]
    pk = lambda d_: _pack([d_[n] for n in names_s], 128, 64)
    res_s = _adamw([pk(gsum)], pk(wts), pk(mom), pk(var), "adamw_small")
    for r_ in res_s:
        for n, a in zip(names_s, _unpack(r_, shapes_s, 128)):
            out[n].append(a)

    outs = [loss, gx[None]]
    for i in range(4):
        outs += [out[n][i][None] for n in _NAMES]
    return tuple(outs)
```

```python
import functools

import jax
import jax.numpy as jnp
from jax import lax
from jax.experimental import pallas as pl
from jax.experimental.pallas import tpu as pltpu

F32 = jnp.float32
BF16 = jnp.bfloat16
MXU_DT = BF16

D = 1024
D_RNN = 1280
RG_BLOCKS = 16
RG_BW = 80
RG_C = 8.0
NQ = 8
NV = 16
HD = 128
CH = 64
D_FF = 2816
LN_EPS = 1e-5
RMS_EPS = 1e-6
L2_EPS = 1e-6
ALPHA = 2.0 ** 0.25
Q_SCALE = HD ** -0.5
N_CAT = 10880
VMEM_LIMIT = 56 * 1024 * 1024
MM_VMEM_BUDGET = 36 * 1024 * 1024
SEQ_GROUP = 8
MESH = pl.DeviceIdType.MESH

ADAM_LR, ADAM_B1, ADAM_B2, ADAM_EPS, ADAM_WD, ADAM_STEP = 1e-3, 0.9, 0.999, 1e-8, 0.01, 10


def _sigmoid(x):
    return 0.5 * jnp.tanh(0.5 * x) + 0.5


def _softplus(x):
    return jnp.maximum(x, 0.0) + jnp.log1p(jnp.exp(-jnp.abs(x)))


_GC = 0.7978845608028654


def _gelu(x):
    return 0.5 * x * (1.0 + jnp.tanh(_GC * (x + 0.044715 * x * x * x)))


def _gelu_and_grad(x):
    t = jnp.tanh(_GC * (x + 0.044715 * x * x * x))
    g = 0.5 * x * (1.0 + t)
    dg = 0.5 * (1.0 + t) + 0.5 * x * (1.0 - t * t) * _GC * (1.0 + 3 * 0.044715 * x * x)
    return g, dg


def _neg_expm1(y):
    series = -y * (1.0 + 0.5 * y * (1.0 + y * (1.0 / 3.0)))
    return jnp.where(y > -0.01, series, 1.0 - jnp.exp(y))


def _dot(a, b):
    return jnp.dot(a.astype(MXU_DT), b.astype(MXU_DT), preferred_element_type=F32)


def _dot_nt(a, b):
    return lax.dot_general(a.astype(MXU_DT), b.astype(MXU_DT), (((1,), (1,)), ((), ())),
                           preferred_element_type=F32)


def _dot_tn(a, b):
    return lax.dot_general(a.astype(MXU_DT), b.astype(MXU_DT), (((0,), (0,)), ((), ())),
                           preferred_element_type=F32)


def _split(a):
    hi = a.astype(BF16)
    return hi, (a - hi.astype(F32)).astype(BF16)


def _dot3(a, b, dims=(((1,), (0,)), ((), ()))):
    ah, al = _split(a)
    bh, bl = _split(b)
    d = lambda p, q: lax.dot_general(p, q, dims, preferred_element_type=F32)
    return d(ah, bh) + (d(al, bh) + d(ah, bl))


def _dot3_tn(a, b):
    return _dot3(a, b, (((0,), (0,)), ((), ())))


def _iota(shape, dim):
    return lax.broadcasted_iota(jnp.int32, shape, dim)


def _shift_down(x, before, j):
    if j == 0:
        return x
    xr = pltpu.roll(x, j, 0)
    br = pltpu.roll(before, j, 0)
    top = jnp.where(_iota(br.shape, 0) < j, br, xr[:8])
    return jnp.concatenate([top, xr[8:]], axis=0)


def _shift_up(x, after, j):
    if j == 0:
        return x
    t = x.shape[0]
    xr = pltpu.roll(x, t - j, 0)
    ar = pltpu.roll(after, 8 - j, 0)
    bot = jnp.where(_iota(ar.shape, 0) >= 8 - j, ar, xr[t - 8:])
    return jnp.concatenate([xr[:t - 8], bot], axis=0)


def _taps(x, before, k):
    return [_shift_down(x, before, k - 1 - i) for i in range(k)]


def _conv_taps(taps, w_ref):
    y = w_ref[0:1, :] * taps[0]
    for i in range(1, len(taps)):
        y = y + w_ref[i:i + 1, :] * taps[i]
    return y


def _conv_causal(x, before, w_ref, k):
    return _conv_taps(_taps(x, before, k), w_ref)


def _conv_causal_bwd(dy, after, w_ref, k, x=None, gw_ref=None):
    dx = None
    for i in range(k):
        sh = _shift_up(dy, after, k - 1 - i)
        term = w_ref[i:i + 1, :] * sh
        dx = term if dx is None else dx + term
        if x is not None:
            gw_ref[i:i + 1, :] += _rsum(x * sh)
    return dx


def _scan_fwd(a, u):
    t = a.shape[0]
    rows = _iota(a.shape, 0)
    d = 1
    while d < t:
        m = rows >= d
        u = u + jnp.where(m, a * pltpu.roll(u, d, 0), 0.0)
        a = jnp.where(m, a * pltpu.roll(a, d, 0), a)
        d *= 2
    return a, u


def _scan_rev(a, u):
    t = a.shape[0]
    rows = _iota(a.shape, 0)
    d = 1
    while d < t:
        m = rows < t - d
        u = u + jnp.where(m, a * pltpu.roll(u, t - d, 0), 0.0)
        a = jnp.where(m, a * pltpu.roll(a, t - d, 0), a)
        d *= 2
    return a, u


def _chunk_cumsum(g, axis, rev=False):
    n = g.shape[axis]
    pos = _iota(g.shape, axis) & (CH - 1)
    d = 1
    while d < CH:
        if rev:
            g = g + jnp.where(pos < CH - d, pltpu.roll(g, n - d, axis), 0.0)
        else:
            g = g + jnp.where(pos >= d, pltpu.roll(g, d, axis), 0.0)
        d *= 2
    return g


def _ln_stats(r):
    mu = jnp.mean(r, axis=-1, keepdims=True)
    xc = r - mu
    var = jnp.mean(xc * xc, axis=-1, keepdims=True)
    rstd = lax.rsqrt(var + LN_EPS)
    return xc * rstd, rstd


def _ln_bwd(dy, xhat, rstd, g):
    dxh = dy * g
    return rstd * (dxh - jnp.mean(dxh, axis=-1, keepdims=True)
                   - xhat * jnp.mean(dxh * xhat, axis=-1, keepdims=True))


def _rsum(x):
    return jnp.sum(x, axis=0, keepdims=True)


def _params(sem):
    return pltpu.CompilerParams(dimension_semantics=sem, vmem_limit_bytes=VMEM_LIMIT)


def _pick(n, cands):
    for c in cands:
        if n % c == 0:
            return c
    return n


def _rows(tm, w, col=0, nt=None):
    if nt is None:
        return pl.BlockSpec((tm, w), lambda i: (i, col))
    return pl.BlockSpec((tm, w), lambda i: (nt - 1 - i, col))


def _before(tm, w, col=0, nt=None):
    r = tm // 8
    if nt is None:
        return pl.BlockSpec((8, w), lambda i: (jnp.maximum(i * r - 1, 0), col))
    return pl.BlockSpec((8, w), lambda i: (jnp.maximum((nt - 1 - i) * r - 1, 0), col))


def _whole(shape):
    return pl.BlockSpec(shape, lambda *_: (0,) * len(shape))


def _mm_plan(a, b, out_dtype):
    m, kk = a.shape
    _, n = b.shape
    tm = _pick(m, (512, 256, 128))
    tk = kk if kk <= 5632 else _pick(kk, (2176, 2048, 1024))
    nk = kk // tk

    def vmem_bytes(tn):
        blocks = tm * tk * a.dtype.itemsize + tk * tn * b.dtype.itemsize + tm * tn * jnp.dtype(out_dtype).itemsize
        return 2 * blocks + (tm * tn * 4 if nk > 1 else 0)

    cands = [t for t in (1408, 1280, 1024, 640, 512, 256, 128) if n % t == 0] or [n]
    tn = next((t for t in cands if vmem_bytes(t) <= MM_VMEM_BUDGET), cands[-1])
    return tm, tn, tk, nk


def _mm_side(a, b, name, out_dtype, side):
    m, _ = a.shape
    _, n = b.shape
    tm, tn, tk, nk = _mm_plan(a, b, out_dtype)
    ni, nj = m // tm, n // tn
    n_in, n_out = len(side.ins), len(side.out_shapes)

    def body(*refs):
        a_ref, b_ref = refs[0], refs[1]
        s_in = refs[2:2 + n_in]
        o_ref = refs[2 + n_in]
        s_out = refs[3 + n_in:3 + n_in + n_out]
        acc = refs[3 + n_in + n_out]
        sems = refs[4 + n_in + n_out:]
        i, j, k = pl.program_id(0), pl.program_id(1), pl.program_id(2)

        @pl.when((i == 0) & (j == 0) & (k == 0))
        def _():
            side.start(s_in, s_out, sems)

        @pl.when(k == 0)
        def _():
            acc[...] = jnp.zeros_like(acc)
        acc[...] += _dot(a_ref[...], b_ref[...])

        @pl.when(k == nk - 1)
        def _():
            o_ref[...] = acc[...].astype(out_dtype)

        @pl.when((i == ni - 1) & (j == nj - 1) & (k == nk - 1))
        def _():
            side.finish(s_in, s_out, sems)

    return pl.pallas_call(
        body, name=name, grid=(ni, nj, nk),
        in_specs=[pl.BlockSpec((tm, tk), lambda i, j, k: (i, k)),
                  pl.BlockSpec((tk, tn), lambda i, j, k: (k, j))] + [_HBM] * n_in,
        out_specs=[pl.BlockSpec((tm, tn), lambda i, j, k: (i, j))] + [_HBM] * n_out,
        out_shape=[jax.ShapeDtypeStruct((m, n), out_dtype)] + list(side.out_shapes),
        scratch_shapes=[pltpu.VMEM((tm, tn), F32)] + list(side.sems),
        compiler_params=_params(("arbitrary", "arbitrary", "arbitrary")),
    )(a, b, *side.ins)


def _mm(a, b, *, name, trans_a=False, out_dtype=F32, side=None):
    if trans_a:
        return _mm(a.T, b, name=name, out_dtype=out_dtype, side=side)
    if side is not None:
        return _mm_side(a, b, name, out_dtype, side)
    m, kk = a.shape
    _, n = b.shape
    tm, tn, tk, nk = _mm_plan(a, b, out_dtype)

    if nk == 1:
        def body(a_ref, b_ref, o_ref):
            o_ref[...] = _dot(a_ref[...], b_ref[...]).astype(out_dtype)
        scratch = []
    else:
        def body(a_ref, b_ref, o_ref, acc):
            k = pl.program_id(2)

            @pl.when(k == 0)
            def _():
                acc[...] = jnp.zeros_like(acc)
            acc[...] += _dot(a_ref[...], b_ref[...])

            @pl.when(k == nk - 1)
            def _():
                o_ref[...] = acc[...].astype(out_dtype)
        scratch = [pltpu.VMEM((tm, tn), F32)]

    return pl.pallas_call(
        body, name=name, grid=(m // tm, n // tn, nk),
        in_specs=[pl.BlockSpec((tm, tk), lambda i, j, k: (i, k)),
                  pl.BlockSpec((tk, tn), lambda i, j, k: (k, j))],
        out_specs=pl.BlockSpec((tm, tn), lambda i, j, k: (i, j)),
        out_shape=jax.ShapeDtypeStruct((m, n), out_dtype),
        scratch_shapes=scratch,
        compiler_params=_params(("parallel", "parallel", "arbitrary")),
    )(a, b)


def _modulate(x, adam):
    s = x.shape[0]
    tm = _pick(s, (512, 256, 128))

    def body(x_ref, ada_ref, o_ref):
        o_ref[...] = (x_ref[...] * (1.0 + ada_ref[1:2, :]) + ada_ref[0:1, :]).astype(MXU_DT)

    return pl.pallas_call(
        body, name="modulate1", grid=(s // tm,),
        in_specs=[_rows(tm, D), _whole((8, D))], out_specs=_rows(tm, D),
        out_shape=jax.ShapeDtypeStruct((s, D), MXU_DT), compiler_params=_params(("parallel",)),
    )(x, adam)


def _rg_gates(xc, wa_ref, wx_ref, vec_ref):
    xb = xc.astype(MXU_DT)
    r = _sigmoid(jnp.dot(xb, wa_ref[...], preferred_element_type=F32) + vec_ref[1:2, :])
    ig = _sigmoid(jnp.dot(xb, wx_ref[...], preferred_element_type=F32) + vec_ref[2:3, :])
    sp = _softplus(-vec_ref[3:4, :])
    la = -RG_C * r * sp
    a = jnp.exp(la)
    n1 = _neg_expm1(2.0 * la)
    rmult = lax.rsqrt(jnp.maximum(n1, 1e-20))
    return r, ig, a, n1 * rmult, sp, rmult


def _rg_fwd(px, cw, vec, wa, wx):
    s = px.shape[0]
    tm = _pick(s, (256, 128))
    w = D_RNN

    def body(xr_ref, gr_ref, cw_ref, vec_ref, wa_ref, wx_ref, h_ref, xc_ref, rec_ref, prev_x, prev_h):
        @pl.when(pl.program_id(0) == 0)
        def _():
            prev_x[...] = jnp.zeros_like(prev_x)
            prev_h[...] = jnp.zeros_like(prev_h)
        x = xr_ref[...]
        xc = _conv_causal(x, prev_x[...], cw_ref, 4) + vec_ref[0:1, :]
        prev_x[...] = x[tm - 8:, :]
        xc_ref[...] = xc
        _, ig, a, mult, _, _ = _rg_gates(xc, wa_ref, wx_ref, vec_ref)
        acum, h = _scan_fwd(a, mult * ig * xc)
        h = h + acum * prev_h[7:8, :]
        prev_h[...] = h[tm - 8:, :]
        h_ref[...] = h
        rec_ref[...] = (h * _gelu(gr_ref[...])).astype(MXU_DT)

    return pl.pallas_call(
        body, name="rg_fwd", grid=(s // tm,),
        in_specs=[_rows(tm, w, 0), _rows(tm, w, 1), _whole((8, w)), _whole((8, w)),
                  _whole((w, w)), _whole((w, w))],
        out_specs=[_rows(tm, w), _rows(tm, w), _rows(tm, w)],
        out_shape=[jax.ShapeDtypeStruct((s, w), F32), jax.ShapeDtypeStruct((s, w), F32),
                   jax.ShapeDtypeStruct((s, w), MXU_DT)],
        scratch_shapes=[pltpu.VMEM((8, w), F32), pltpu.VMEM((8, w), F32)],
        compiler_params=_params(("arbitrary",)),
    )(px, px, cw, vec, wa, wx)


def _dn_scalars(ab, arow, drow):
    lane = _iota(ab.shape, 1)
    g = jnp.where(lane < NV, -jnp.exp(arow) * _softplus(ab + drow), 0.0)
    beta = _sigmoid(ab)
    return lane, g, beta


def _l2n_heads(c, out_ref, off, scale):
    for hh in range(NQ):
        x = c[:, off + hh * HD: off + (hh + 1) * HD]
        r = lax.rsqrt(jnp.sum(x * x, axis=-1, keepdims=True) + L2_EPS)
        out_ref[:, hh * HD:(hh + 1) * HD] = x * (r * scale)


def _dn_prep(pqkv, pab, cw, arow, drow):
    s = pqkv.shape[0]
    tm = 128
    wq = NQ * HD

    def body(x_ref, ab_ref, cw_ref, a_ref, d_ref, q_ref, k_ref, v_ref, rt_ref, cp_ref, prev_x):
        @pl.when(pl.program_id(0) == 0)
        def _():
            prev_x[...] = jnp.zeros_like(prev_x)
        x = x_ref[...]
        cp = _conv_causal(x, prev_x[...], cw_ref, 4)
        prev_x[...] = x[tm - 8:, :]
        cp_ref[...] = cp
        c = cp * _sigmoid(cp)
        _l2n_heads(c, q_ref, 0, Q_SCALE)
        _l2n_heads(c, k_ref, wq, 1.0)
        v_ref[...] = c[:, 2 * wq:]
        lane, g, beta = _dn_scalars(ab_ref[...], a_ref[...], d_ref[...])
        gc = _chunk_cumsum(g, 0)
        gl = gc + _chunk_cumsum(g, 0, rev=True) - g
        pack = jnp.where(lane < NV, gc, jnp.where(lane < 2 * NV, beta,
                         jnp.where(lane < 3 * NV, pltpu.roll(gl, 2 * NV, 1), 0.0)))
        rt_ref[...] = pack.T[0:3 * NV, :]

    return pl.pallas_call(
        body, name="dn_prep", grid=(s // tm,),
        in_specs=[_rows(tm, 4 * wq), _rows(tm, 128), _whole((8, 4 * wq)), _whole((1, 128)), _whole((1, 128))],
        out_specs=[_rows(tm, wq), _rows(tm, wq), _rows(tm, 2 * wq),
                   pl.BlockSpec((3 * NV, tm), lambda i: (0, i)), _rows(tm, 4 * wq)],
        out_shape=[jax.ShapeDtypeStruct((s, wq), F32), jax.ShapeDtypeStruct((s, wq), F32),
                   jax.ShapeDtypeStruct((s, 2 * wq), F32), jax.ShapeDtypeStruct((3 * NV, s), F32),
                   jax.ShapeDtypeStruct((s, 4 * wq), F32)],
        scratch_shapes=[pltpu.VMEM((8, 4 * wq), F32)],
        compiler_params=_params(("arbitrary",)),
    )(pqkv, pab, cw, arow, drow)


def _pair_masks():
    i = _iota((2 * CH, 2 * CH), 0)
    j = _iota((2 * CH, 2 * CH), 1)
    same = (i >> 6) == (j >> 6)
    return same & (i >= j), same & (i > j)


def _head_cols(rt_ref, h):
    shp = (2 * CH, 2 * CH)
    g_r = jnp.broadcast_to(rt_ref[pl.ds(h, 1), :], shp)
    b_r = jnp.broadcast_to(rt_ref[pl.ds(NV + h, 1), :], shp)
    l_r = jnp.broadcast_to(rt_ref[pl.ds(2 * NV + h, 1), :], shp)
    return g_r, g_r.T, b_r.T, l_r, l_r.T


def _inv_unit_lower(a):
    return _inv_unit_lower_many([a])[0]


def _inv_unit_lower_many(a_list):
    n = a_list[0].shape[0]
    eye = (_iota((n, n), 0) == _iota((n, n), 1)).astype(F32)
    bs = [-a for a in a_list]
    xs = [eye + b for b in bs]
    for _ in range(5):
        bs = [_dot(b, b) for b in bs]
        xs = [x + _dot(x, b) for x, b in zip(xs, bs)]
    rs = [(eye - x) - _dot3(a, x) for a, x in zip(a_list, xs)]
    return [x + _dot(x, r) for x, r in zip(xs, rs)]


def _gam_rows(l_r):
    lrow = l_r[0:1, :]
    lane = _iota(lrow.shape, 1)
    other = pltpu.roll(lrow, CH, 1)
    return jnp.exp(jnp.where(lane < CH, lrow, other)), jnp.exp(jnp.where(lane >= CH, lrow, other))


def _dn_intra(q, k, v, rt):
    s = q.shape[0]
    nb = s // (2 * CH)
    qps = 8
    blk = pl.BlockSpec((2 * CH, qps * HD), lambda i, h: (i, h))
    blk2 = pl.BlockSpec((2 * CH, 2 * qps * HD), lambda i, h: (i, h))

    def body(q_ref, k_ref, v_ref, rt_ref, u_ref, w_ref, qd_ref, kd_ref, p_ref, ti_ref, gam_ref):
        hstep = pl.program_id(1)
        mc, ms = _pair_masks()
        gam_ref[...] = jnp.zeros_like(gam_ref)
        heads = []
        for qh in range(qps):
            qq = q_ref[:, qh * HD:(qh + 1) * HD]
            kk_ = k_ref[:, qh * HD:(qh + 1) * HD]
            kk = _dot_nt(kk_, kk_)
            qk = _dot_nt(qq, kk_)
            for j in range(2):
                idx = 2 * qh + j
                cs = slice(idx * HD, (idx + 1) * HD)
                g_r, g_c, b_c, l_r, l_c = _head_cols(rt_ref, 2 * qps * hstep + idx)
                dec = jnp.where(mc, jnp.exp(jnp.where(mc, g_c - g_r, 0.0)), 0.0)
                eg = jnp.exp(g_c)
                p_ref[:, cs] = jnp.where(mc, qk * dec, 0.0)
                qd_ref[:, cs] = eg * qq
                kd_ref[:, cs] = jnp.exp(l_c - g_c) * kk_
                ga, gb = _gam_rows(l_r)
                gam_ref[0, qh, 2 * j:2 * j + 1, :] = ga
                gam_ref[0, qh, 2 * j + 1:2 * j + 2, :] = gb
                rhs = jnp.concatenate([b_c * v_ref[:, cs], b_c * eg * kk_], axis=1)
                heads.append((cs, jnp.where(ms, b_c * kk * dec, 0.0), rhs))
        tinvs = _inv_unit_lower_many([a for _, a, _ in heads])
        uws = [_dot3(t, rhs) for t, (_, _, rhs) in zip(tinvs, heads)]
        for t, uw, (cs, _, _) in zip(tinvs, uws, heads):
            ti_ref[:, cs] = t
            u_ref[:, cs] = uw[:, :HD]
            w_ref[:, cs] = uw[:, HD:]

    big = jax.ShapeDtypeStruct((s, NV * HD), F32)
    return pl.pallas_call(
        body, name="dn_intra", grid=(nb, NQ // qps),
        in_specs=[blk, blk, blk2, pl.BlockSpec((3 * NV, 2 * CH), lambda i, h: (0, i))],
        out_specs=[blk2] * 6 + [pl.BlockSpec((1, qps, 8, 128), lambda i, h: (i, h, 0, 0))],
        out_shape=[big] * 6 + [jax.ShapeDtypeStruct((nb, NQ, 8, 128), F32)],
        compiler_params=_params(("parallel", "parallel")),
    )(q, k, v, rt)


def _dn_seq(u, w, qd, kd, p, gam):
    s = u.shape[0]
    nb = s // (2 * CH)
    wide = pl.BlockSpec((2 * CH, NV * HD), lambda i: (i, 0))

    def body(u_ref, w_ref, qd_ref, kd_ref, p_ref, gam_ref, o_ref, vn_ref, ss_ref, st):
        @pl.when(pl.program_id(0) == 0)
        def _():
            st[...] = jnp.zeros_like(st)
        ra, rb = slice(0, CH), slice(CH, 2 * CH)
        for g0 in range(0, NV, SEQ_GROUP):
            hs = list(range(g0, g0 + SEQ_GROUP))
            cs = [slice(h * HD, (h + 1) * HD) for h in hs]
            ga = [gam_ref[0, h // 2, 2 * (h % 2):2 * (h % 2) + 1, :] for h in hs]
            gb = [gam_ref[0, h // 2, 2 * (h % 2) + 1:2 * (h % 2) + 2, :] for h in hs]
            s0 = [st[h] for h in hs]
            vna = [u_ref[ra, c] - _dot(w_ref[ra, c], s) for c, s in zip(cs, s0)]
            s1 = [g * s + _dot_tn(kd_ref[ra, c], v) for g, s, c, v in zip(ga, s0, cs, vna)]
            vnb = [u_ref[rb, c] - _dot(w_ref[rb, c], s) for c, s in zip(cs, s1)]
            s2 = [g * s + _dot_tn(kd_ref[rb, c], v) for g, s, c, v in zip(gb, s1, cs, vnb)]
            for h, s in zip(hs, s2):
                st[h] = s
            oa = [_dot(qd_ref[ra, c], s) for c, s in zip(cs, s0)]
            ob = [_dot(qd_ref[rb, c], s) for c, s in zip(cs, s1)]
            for i_, h in enumerate(hs):
                vn = jnp.concatenate([vna[i_], vnb[i_]], axis=0)
                o_ref[:, cs[i_]] = jnp.concatenate([oa[i_], ob[i_]], axis=0) + _dot(p_ref[:, cs[i_]], vn)
                vn_ref[:, cs[i_]] = vn
                ss_ref[h, 0:HD, :] = s0[i_]
                ss_ref[h, HD:2 * HD, :] = s1[i_]

    big = jax.ShapeDtypeStruct((s, NV * HD), F32)
    return pl.pallas_call(
        body, name="dn_seq", grid=(nb,),
        in_specs=[wide] * 5 + [pl.BlockSpec((1, NQ, 8, 128), lambda i: (i, 0, 0, 0))],
        out_specs=[wide, wide, pl.BlockSpec((NV, 2 * HD, HD), lambda i: (0, i, 0))],
        out_shape=[big, big, jax.ShapeDtypeStruct((NV, 2 * s, HD), F32)],
        scratch_shapes=[pltpu.VMEM((NV, HD, HD), F32)],
        compiler_params=_params(("arbitrary",)),
    )(u, w, qd, kd, p, gam)


def _dn_post(o, pz, nw):
    s = o.shape[0]
    tm = _pick(s, (256, 128))

    def body(o_ref, z_ref, nw_ref, y_ref):
        for h in range(NV):
            cs = slice(h * HD, (h + 1) * HD)
            oh = o_ref[:, cs]
            z = z_ref[:, cs]
            rs = lax.rsqrt(jnp.mean(oh * oh, axis=-1, keepdims=True) + RMS_EPS)
            y_ref[:, cs] = (oh * rs * nw_ref[...] * (z * _sigmoid(z))).astype(MXU_DT)

    return pl.pallas_call(
        body, name="dn_post", grid=(s // tm,),
        in_specs=[_rows(tm, NV * HD), _rows(tm, NV * HD), _whole((1, HD))],
        out_specs=_rows(tm, NV * HD),
        out_shape=jax.ShapeDtypeStruct((s, NV * HD), MXU_DT), compiler_params=_params(("parallel",)),
    )(o, pz, nw)


def _merge(pg, ya, yb):
    s = ya.shape[0]
    tm = _pick(s, (512, 256, 128))

    def body(ga_ref, gb_ref, ya_ref, yb_ref, o_ref):
        o_ref[...] = (_sigmoid(ga_ref[...]) * ya_ref[...] + _sigmoid(gb_ref[...]) * yb_ref[...]).astype(MXU_DT)

    return pl.pallas_call(
        body, name="merge", grid=(s // tm,),
        in_specs=[_rows(tm, D, 0), _rows(tm, D, 1), _rows(tm, D), _rows(tm, D)],
        out_specs=_rows(tm, D),
        out_shape=jax.ShapeDtypeStruct((s, D), MXU_DT), compiler_params=_params(("parallel",)),
    )(pg, pg, ya, yb)


def _ln1(x, mix, adam, lng, lnb):
    s = x.shape[0]
    tm = _pick(s, (512, 256, 128))

    def body(x_ref, m_ref, ada_ref, g_ref, b_ref, r_ref, x2_ref, h2_ref):
        r = ALPHA * x_ref[...] + (1.0 + ada_ref[2:3, :]) * m_ref[...]
        xhat, _ = _ln_stats(r)
        x2 = xhat * g_ref[...] + b_ref[...]
        r_ref[...] = r
        x2_ref[...] = x2
        h2_ref[...] = (x2 * (1.0 + ada_ref[4:5, :]) + ada_ref[3:4, :]).astype(MXU_DT)

    return pl.pallas_call(
        body, name="ln1", grid=(s // tm,),
        in_specs=[_rows(tm, D), _rows(tm, D), _whole((8, D)), _whole((1, D)), _whole((1, D))],
        out_specs=[_rows(tm, D)] * 3,
        out_shape=[jax.ShapeDtypeStruct((s, D), F32), jax.ShapeDtypeStruct((s, D), F32),
                   jax.ShapeDtypeStruct((s, D), MXU_DT)],
        compiler_params=_params(("parallel",)),
    )(x, mix, adam, lng, lnb)


def _ffn_act(gu, cw, cb):
    s = gu.shape[0]
    tm = _pick(s, (256, 128))
    w = D_FF

    def body(g_ref, u_ref, cw_ref, cb_ref, o_ref, gc_ref, prev):
        @pl.when(pl.program_id(0) == 0)
        def _():
            prev[...] = jnp.zeros_like(prev)
        g = g_ref[...]
        gc = _conv_causal(g, prev[...], cw_ref, 3) + cb_ref[...]
        prev[...] = g[tm - 8:, :]
        gc_ref[...] = gc
        o_ref[...] = (_gelu(gc) * u_ref[...]).astype(MXU_DT)

    return pl.pallas_call(
        body, name="ffn_act", grid=(s // tm,),
        in_specs=[_rows(tm, w, 0), _rows(tm, w, 1), _whole((8, w)), _whole((1, w))],
        out_specs=[_rows(tm, w), _rows(tm, w)],
        out_shape=[jax.ShapeDtypeStruct((s, w), MXU_DT), jax.ShapeDtypeStruct((s, w), F32)],
        scratch_shapes=[pltpu.VMEM((8, w), F32)],
        compiler_params=_params(("arbitrary",)),
    )(gu, gu, cw, cb)


def _ln2_loss(x2, ff, tgt, adam, lng, lnb):
    s = x2.shape[0]
    tm = _pick(s, (512, 256, 128))

    def body(x_ref, f_ref, t_ref, ada_ref, g_ref, b_ref, dff_ref, dx_ref, red_ref):
        @pl.when(pl.program_id(0) == 0)
        def _():
            red_ref[...] = jnp.zeros_like(red_ref)
        ff_ = f_ref[...]
        r = ALPHA * x_ref[...] + (1.0 + ada_ref[5:6, :]) * ff_
        xhat, rstd = _ln_stats(r)
        err = xhat * g_ref[...] + b_ref[...] - t_ref[...]
        dy = err * (1.0 / D)
        dr = _ln_bwd(dy, xhat, rstd, g_ref[...])
        dff_ref[...] = ((1.0 + ada_ref[5:6, :]) * dr).astype(MXU_DT)
        dx_ref[...] = ALPHA * dr
        red_ref[0:1, :] += _rsum(dy * xhat)
        red_ref[1:2, :] += _rsum(dy)
        red_ref[2:3, :] += _rsum(dr * ff_)
        red_ref[3:4, :] += jnp.sum(_rsum(err * err), axis=1, keepdims=True) * (0.5 / D)

    return pl.pallas_call(
        body, name="ln2_loss", grid=(s // tm,),
        in_specs=[_rows(tm, D)] * 3 + [_whole((8, D)), _whole((1, D)), _whole((1, D))],
        out_specs=[_rows(tm, D), _rows(tm, D), _whole((8, D))],
        out_shape=[jax.ShapeDtypeStruct((s, D), MXU_DT), jax.ShapeDtypeStruct((s, D), F32),
                   jax.ShapeDtypeStruct((8, D), F32)],
        compiler_params=_params(("arbitrary",)),
    )(x2, ff, tgt, adam, lng, lnb)


def _ffn_bwd(dact, gu, gc, cw):
    s = dact.shape[0]
    tm = _pick(s, (256, 128))
    nt = s // tm
    w = D_FF

    def body(da_ref, g_ref, u_ref, gc_ref, cw_ref, o_ref, gcw_ref, gcb_ref, nxt):
        @pl.when(pl.program_id(0) == 0)
        def _():
            nxt[...] = jnp.zeros_like(nxt)
            gcw_ref[...] = jnp.zeros_like(gcw_ref)
            gcb_ref[...] = jnp.zeros_like(gcb_ref)
        gel, dgel = _gelu_and_grad(gc_ref[...])
        da = da_ref[...]
        dgc = da * u_ref[...] * dgel
        o_ref[:, w:] = (da * gel).astype(MXU_DT)
        o_ref[:, :w] = _conv_causal_bwd(dgc, nxt[...], cw_ref, 3, g_ref[...], gcw_ref).astype(MXU_DT)
        nxt[...] = dgc[:8, :]
        gcb_ref[...] += _rsum(dgc)

    return pl.pallas_call(
        body, name="ffn_bwd", grid=(nt,),
        in_specs=[_rows(tm, w, 0, nt), _rows(tm, w, 0, nt), _rows(tm, w, 1, nt), _rows(tm, w, 0, nt),
                  _whole((8, w))],
        out_specs=[_rows(tm, 2 * w, 0, nt), _whole((8, w)), _whole((1, w))],
        out_shape=[jax.ShapeDtypeStruct((s, 2 * w), MXU_DT), jax.ShapeDtypeStruct((8, w), F32),
                   jax.ShapeDtypeStruct((1, w), F32)],
        scratch_shapes=[pltpu.VMEM((8, w), F32)],
        compiler_params=_params(("arbitrary",)),
    )(dact, gu, gu, gc, cw)


def _ln1_bwd(dh2, dx2a, x2, r1, mix, adam, lng):
    s = dh2.shape[0]
    tm = _pick(s, (512, 256, 128))

    def body(dh_ref, dxa_ref, x2_ref, r_ref, m_ref, ada_ref, g_ref, dm_ref, dx_ref, red_ref):
        @pl.when(pl.program_id(0) == 0)
        def _():
            red_ref[...] = jnp.zeros_like(red_ref)
        dh = dh_ref[...]
        dx2 = dxa_ref[...] + dh * (1.0 + ada_ref[4:5, :])
        xhat, rstd = _ln_stats(r_ref[...])
        dr = _ln_bwd(dx2, xhat, rstd, g_ref[...])
        dm_ref[...] = ((1.0 + ada_ref[2:3, :]) * dr).astype(MXU_DT)
        dx_ref[...] = ALPHA * dr
        red_ref[0:1, :] += _rsum(dh * x2_ref[...])
        red_ref[1:2, :] += _rsum(dh)
        red_ref[2:3, :] += _rsum(dx2 * xhat)
        red_ref[3:4, :] += _rsum(dx2)
        red_ref[4:5, :] += _rsum(dr * m_ref[...])

    return pl.pallas_call(
        body, name="ln1_bwd", grid=(s // tm,),
        in_specs=[_rows(tm, D)] * 5 + [_whole((8, D)), _whole((1, D))],
        out_specs=[_rows(tm, D), _rows(tm, D), _whole((8, D))],
        out_shape=[jax.ShapeDtypeStruct((s, D), MXU_DT), jax.ShapeDtypeStruct((s, D), F32),
                   jax.ShapeDtypeStruct((8, D), F32)],
        compiler_params=_params(("arbitrary",)),
    )(dh2, dx2a, x2, r1, mix, adam, lng)


def _merge_bwd(dmg, pg, ya, yb):
    s = dmg.shape[0]
    tm = _pick(s, (512, 256, 128))

    def body(d_ref, ga_ref, gb_ref, ya_ref, yb_ref, dya_ref, dyb_ref, dpg_ref):
        d = d_ref[...]
        sa = _sigmoid(ga_ref[...])
        sb = _sigmoid(gb_ref[...])
        dya_ref[...] = (d * sa).astype(MXU_DT)
        dyb_ref[...] = (d * sb).astype(MXU_DT)
        dpg_ref[:, :D] = (d * ya_ref[...] * sa * (1.0 - sa)).astype(MXU_DT)
        dpg_ref[:, D:] = (d * yb_ref[...] * sb * (1.0 - sb)).astype(MXU_DT)

    return pl.pallas_call(
        body, name="merge_bwd", grid=(s // tm,),
        in_specs=[_rows(tm, D), _rows(tm, D, 0), _rows(tm, D, 1), _rows(tm, D), _rows(tm, D)],
        out_specs=[_rows(tm, D), _rows(tm, D), _rows(tm, 2 * D)],
        out_shape=[jax.ShapeDtypeStruct((s, D), MXU_DT), jax.ShapeDtypeStruct((s, D), MXU_DT),
                   jax.ShapeDtypeStruct((s, 2 * D), MXU_DT)],
        compiler_params=_params(("parallel",)),
    )(dmg, pg, pg, ya, yb)


def _dn_post_bwd(ddn, o, pz, nw):
    s = o.shape[0]
    tm = _pick(s, (256, 128))

    def body(d_ref, o_ref, z_ref, nw_ref, do_ref, dz_ref, gnw_ref):
        @pl.when(pl.program_id(0) == 0)
        def _():
            gnw_ref[...] = jnp.zeros_like(gnw_ref)
        acc = jnp.zeros((1, HD), F32)
        for h in range(NV):
            cs = slice(h * HD, (h + 1) * HD)
            oh = o_ref[:, cs]
            z = z_ref[:, cs]
            d = d_ref[:, cs]
            sg = _sigmoid(z)
            rs = lax.rsqrt(jnp.mean(oh * oh, axis=-1, keepdims=True) + RMS_EPS)
            n = oh * rs
            dz_ref[:, cs] = (d * n * nw_ref[...] * sg * (1.0 + z * (1.0 - sg))).astype(MXU_DT)
            dn_ = d * (z * sg)
            acc = acc + _rsum(dn_ * n)
            dnn = dn_ * nw_ref[...]
            do_ref[:, cs] = rs * (dnn - n * jnp.mean(dnn * n, axis=-1, keepdims=True))
        gnw_ref[...] += acc

    return pl.pallas_call(
        body, name="dn_post_bwd", grid=(s // tm,),
        in_specs=[_rows(tm, NV * HD)] * 3 + [_whole((1, HD))],
        out_specs=[_rows(tm, NV * HD), _rows(tm, NV * HD), _whole((1, HD))],
        out_shape=[jax.ShapeDtypeStruct((s, NV * HD), F32), jax.ShapeDtypeStruct((s, NV * HD), MXU_DT),
                   jax.ShapeDtypeStruct((1, HD), F32)],
        compiler_params=_params(("arbitrary",)),
    )(ddn, o, pz, nw)


def _dn_seq_bwd(do, qd, kd, p, w, vn, ssave, gam):
    s = do.shape[0]
    nb = s // (2 * CH)
    wide = pl.BlockSpec((2 * CH, NV * HD), lambda i: (nb - 1 - i, 0))
    gspec = pl.BlockSpec((1, NQ, 8, 128), lambda i: (nb - 1 - i, 0, 0, 0))

    def body(do_ref, qd_ref, kd_ref, p_ref, w_ref, vn_ref, ss_ref, gam_ref, dvn_ref, dkd_ref, dgam_ref, dst):
        @pl.when(pl.program_id(0) == 0)
        def _():
            dst[...] = jnp.zeros_like(dst)
        dgam_ref[...] = jnp.zeros_like(dgam_ref)
        ra, rb = slice(0, CH), slice(CH, 2 * CH)
        tot = lambda t: jnp.sum(jnp.sum(t, axis=1, keepdims=True), axis=0, keepdims=True)
        for g0 in range(0, NV, SEQ_GROUP):
            hs = list(range(g0, g0 + SEQ_GROUP))
            cs = [slice(h * HD, (h + 1) * HD) for h in hs]
            ga = [gam_ref[0, h // 2, 2 * (h % 2):2 * (h % 2) + 1, :] for h in hs]
            gb = [gam_ref[0, h // 2, 2 * (h % 2) + 1:2 * (h % 2) + 2, :] for h in hs]
            ds2 = [dst[h] for h in hs]
            pdo = [_dot_tn(p_ref[:, c], do_ref[:, c]) for c in cs]
            qdo_b = [_dot_tn(qd_ref[rb, c], do_ref[rb, c]) for c in cs]
            qdo_a = [_dot_tn(qd_ref[ra, c], do_ref[ra, c]) for c in cs]
            dvb = [p_[rb] + _dot(kd_ref[rb, c], d_) for p_, c, d_ in zip(pdo, cs, ds2)]
            ds1 = [g * d_ + q_ - _dot_tn(w_ref[rb, c], v_)
                   for g, d_, q_, c, v_ in zip(gb, ds2, qdo_b, cs, dvb)]
            dva = [p_[ra] + _dot(kd_ref[ra, c], d_) for p_, c, d_ in zip(pdo, cs, ds1)]
            ds0 = [g * d_ + q_ - _dot_tn(w_ref[ra, c], v_)
                   for g, d_, q_, c, v_ in zip(ga, ds1, qdo_a, cs, dva)]
            for h, d_ in zip(hs, ds0):
                dst[h] = d_
            for i_, h in enumerate(hs):
                c = cs[i_]
                row = 2 * (h % 2)
                dkd_ref[rb, c] = _dot_nt(vn_ref[rb, c], ds2[i_])
                dkd_ref[ra, c] = _dot_nt(vn_ref[ra, c], ds1[i_])
                dvn_ref[ra, c] = dva[i_]
                dvn_ref[rb, c] = dvb[i_]
                dgam_ref[0, h // 2, row:row + 1, :] = jnp.broadcast_to(tot(ds1[i_] * ss_ref[h, 0:HD, :]), (1, 128))
                dgam_ref[0, h // 2, row + 1:row + 2, :] = jnp.broadcast_to(
                    tot(ds2[i_] * ss_ref[h, HD:2 * HD, :]), (1, 128))

    big = jax.ShapeDtypeStruct((s, NV * HD), F32)
    return pl.pallas_call(
        body, name="dn_seq_bwd", grid=(nb,),
        in_specs=[wide] * 6 + [pl.BlockSpec((NV, 2 * HD, HD), lambda i: (0, nb - 1 - i, 0)), gspec],
        out_specs=[wide, wide, gspec],
        out_shape=[big, big, jax.ShapeDtypeStruct((nb, NQ, 8, 128), F32)],
        scratch_shapes=[pltpu.VMEM((NV, HD, HD), F32)],
        compiler_params=_params(("arbitrary",)),
    )(do, qd, kd, p, w, vn, ssave, gam)


def _dn_intra_bwd(q, k, v, rt, do, dvn, dkd, gam, dgam, ssave, tinv, u, w, vn):
    s = q.shape[0]
    nb = s // (2 * CH)
    qps = 4
    nh = 2 * qps
    blk = pl.BlockSpec((2 * CH, qps * HD), lambda i, h: (i, h))
    blk2 = pl.BlockSpec((2 * CH, nh * HD), lambda i, h: (i, h))
    gspec = pl.BlockSpec((1, qps, 8, 128), lambda i, h: (i, h, 0, 0))
    rspec = pl.BlockSpec((3 * NV, 2 * CH), lambda i, h: (0, i))

    def body(q_ref, k_ref, v_ref, rt_ref, do_ref, dvn_ref, dkd_ref, gam_ref, dgam_ref, ss_ref,
             ti_ref, u_ref, w_ref, vn_ref, dq_ref, dk_ref, dv_ref, drt_ref, acc):
        hstep = pl.program_id(1)

        @pl.when(hstep == 0)
        def _():
            acc[...] = jnp.zeros_like(acc)
        mc, ms = _pair_masks()
        ra, rb = slice(0, CH), slice(CH, 2 * CH)
        lane = _iota((1, 2 * CH), 1)
        hs = list(range(nh))
        qh = [h // 2 for h in hs]
        cs = [slice(h * HD, (h + 1) * HD) for h in hs]
        qq_ = [q_ref[:, t * HD:(t + 1) * HD] for t in range(qps)]
        kk_ = [k_ref[:, t * HD:(t + 1) * HD] for t in range(qps)]
        kk = [_dot_nt(k_, k_) for k_ in kk_]
        qk = [_dot_nt(q_, k_) for q_, k_ in zip(qq_, kk_)]
        cols = [_head_cols(rt_ref, nh * hstep + h) for h in hs]
        b_c = [c_[2] for c_ in cols]
        dec = [jnp.where(mc, jnp.exp(jnp.where(mc, c_[1] - c_[0], 0.0)), 0.0) for c_ in cols]
        eg = [jnp.exp(c_[1]) for c_ in cols]
        egl = [jnp.exp(c_[4] - c_[1]) for c_ in cols]
        dob = [do_ref[:, c] for c in cs]
        dvb = [dvn_ref[:, c] for c in cs]
        dqd = [jnp.concatenate([_dot_nt(d_[ra], ss_ref[h, 0:HD, :]), _dot_nt(d_[rb], ss_ref[h, HD:2 * HD, :])], axis=0)
               for h, d_ in zip(hs, dob)]
        dw = [-jnp.concatenate([_dot_nt(d_[ra], ss_ref[h, 0:HD, :]), _dot_nt(d_[rb], ss_ref[h, HD:2 * HD, :])], axis=0)
              for h, d_ in zip(hs, dvb)]
        dp = [jnp.where(mc, _dot_nt(d_, vn_ref[:, c]), 0.0) for d_, c in zip(dob, cs)]
        dbuw = [_dot3_tn(ti_ref[:, c], jnp.concatenate([d_, w_], axis=1)) for c, d_, w_ in zip(cs, dvb, dw)]
        dbu = [t[:, :HD] for t in dbuw]
        dbw = [t[:, HD:] for t in dbuw]
        da = [jnp.where(ms, -(_dot_nt(bu, u_ref[:, c]) + _dot_nt(bw, w_ref[:, c])), 0.0)
              for bu, bw, c in zip(dbu, dbw, cs)]
        dm = [a_ * d_ for a_, d_ in zip(da, dec)]
        dn_ = [p_ * d_ for p_, d_ in zip(dp, dec)]
        dbk = [_dot(m_, kk_[t]) for m_, t in zip(dm, qh)]
        dqs = [_dot(n_, kk_[t]) + e_ * q_ for n_, t, e_, q_ in zip(dn_, qh, eg, dqd)]
        dks = [_dot_tn(m_, b_ * kk_[t]) + _dot_tn(n_, qq_[t]) + el * dkd_ref[:, c] + b_ * (e_ * bw + bk)
               for m_, b_, t, n_, el, c, e_, bw, bk in zip(dm, b_c, qh, dn_, egl, cs, eg, dbw, dbk)]
        for t in range(qps):
            dq_ref[:, t * HD:(t + 1) * HD] = dqs[2 * t] + dqs[2 * t + 1]
            dk_ref[:, t * HD:(t + 1) * HD] = dks[2 * t] + dks[2 * t + 1]
        for h in hs:
            c, t, j = cs[h], qh[h], h % 2
            dv_ref[:, c] = b_c[h] * dbu[h]
            e = da[h] * (b_c[h] * kk[t] * dec[h]) + dp[h] * (qk[t] * dec[h])
            x = dkd_ref[:, c] * (egl[h] * kk_[t])
            egk = eg[h] * kk_[t]
            z = e + dqd[h] * (eg[h] * qq_[t]) - x + dbw[h] * (b_c[h] * egk)
            zb = dbw[h] * egk + dbu[h] * v_ref[:, c] + dbk[h] * kk_[t]
            sa = jnp.sum(jnp.sum(x[ra], axis=1, keepdims=True), axis=0, keepdims=True)
            sb = jnp.sum(jnp.sum(x[rb], axis=1, keepdims=True), axis=0, keepdims=True)
            la = sa + dgam_ref[0, t, 2 * j:2 * j + 1, :] * gam_ref[0, t, 2 * j:2 * j + 1, :]
            lb = sb + dgam_ref[0, t, 2 * j + 1:2 * j + 2, :] * gam_ref[0, t, 2 * j + 1:2 * j + 2, :]
            hg = nh * hstep + h
            acc[pl.ds(hg, 1), :] = _rsum(z.T - e)
            acc[pl.ds(NV + hg, 1), :] = _rsum(zb.T)
            acc[pl.ds(2 * NV + hg, 1), :] = jnp.where(lane < CH, la, lb)

        @pl.when(hstep == NQ // qps - 1)
        def _():
            drt_ref[...] = acc[...]

    return pl.pallas_call(
        body, name="dn_intra_bwd", grid=(nb, NQ // qps),
        in_specs=[blk, blk, blk2, rspec, blk2, blk2, blk2, gspec, gspec,
                  pl.BlockSpec((nh, 2 * HD, HD), lambda i, h: (h, i, 0)), blk2, blk2, blk2, blk2],
        out_specs=[blk, blk, blk2, rspec],
        out_shape=[jax.ShapeDtypeStruct((s, NQ * HD), F32), jax.ShapeDtypeStruct((s, NQ * HD), F32),
                   jax.ShapeDtypeStruct((s, NV * HD), F32), jax.ShapeDtypeStruct((3 * NV, s), F32)],
        scratch_shapes=[pltpu.VMEM((3 * NV, 2 * CH), F32)],
        compiler_params=_params(("parallel", "arbitrary")),
    )(q, k, v, rt, do, dvn, dkd, gam, dgam, ssave, tinv, u, w, vn)


def _l2n_heads_bwd(c, d_ref, dc_ref, off, scale):
    for hh in range(NQ):
        cs = slice(off + hh * HD, off + (hh + 1) * HD)
        x = c[:, cs]
        dy = d_ref[:, hh * HD:(hh + 1) * HD]
        r = lax.rsqrt(jnp.sum(x * x, axis=-1, keepdims=True) + L2_EPS)
        dc_ref[:, cs] = (scale * r) * (dy - x * (r * r) * jnp.sum(dy * x, axis=-1, keepdims=True))


def _dn_prep_bwd(dq, dk, dv, drt, pqkv, cpre, pab, cw, acol, dcol):
    s = pqkv.shape[0]
    tm = 128
    nt = s // tm
    wq = NQ * HD

    def body(dq_ref, dk_ref, dv_ref, drt_ref, x_ref, cp_ref, ab_ref, cw_ref, ac_ref,
             dc_ref, dx_ref, dab_ref, gcw_ref, gsc_ref, dcs, nxt):
        @pl.when(pl.program_id(0) == 0)
        def _():
            nxt[...] = jnp.zeros_like(nxt)
            gcw_ref[...] = jnp.zeros_like(gcw_ref)
            gsc_ref[...] = jnp.zeros_like(gsc_ref)
        cp = cp_ref[...]
        sg = _sigmoid(cp)
        c = cp * sg
        _l2n_heads_bwd(c, dq_ref, dcs, 0, Q_SCALE)
        _l2n_heads_bwd(c, dk_ref, dcs, wq, 1.0)
        dcs[:, 2 * wq:] = dv_ref[...]
        dcp = dcs[...] * (sg * (1.0 + cp * (1.0 - sg)))
        dx_ref[...] = _conv_causal_bwd(dcp, nxt[...], cw_ref, 4, x_ref[...], gcw_ref).astype(MXU_DT)
        nxt[...] = dcp[:8, :]
        lane = _iota((NV, tm), 1)
        dgt = drt_ref[0:NV, :] + jnp.where((lane & (CH - 1)) == CH - 1, drt_ref[2 * NV:3 * NV, :], 0.0)
        dg = _chunk_cumsum(dgt, 1, rev=True)
        abt = ab_ref[...].T
        zt = abt[0:NV, :] + dc_ref[...]
        gt = -jnp.exp(ac_ref[...]) * _softplus(zt)
        dat = dg * (-jnp.exp(ac_ref[...])) * _sigmoid(zt)
        bt = _sigmoid(abt[NV:2 * NV, :])
        dbt = drt_ref[NV:2 * NV, :] * bt * (1.0 - bt)
        full = jnp.concatenate([dat, dbt, jnp.zeros((128 - 2 * NV, tm), F32)], axis=0)
        dab_ref[...] = full.T.astype(MXU_DT)
        l2 = _iota((NV, 128), 1)
        gsc_ref[...] += jnp.where(l2 == 0, jnp.sum(dg * gt, axis=1, keepdims=True),
                                  jnp.where(l2 == 1, jnp.sum(dat, axis=1, keepdims=True), 0.0))

    return pl.pallas_call(
        body, name="dn_prep_bwd", grid=(nt,),
        in_specs=[_rows(tm, wq, 0, nt), _rows(tm, wq, 0, nt), _rows(tm, 2 * wq, 0, nt),
                  pl.BlockSpec((3 * NV, tm), lambda i: (0, nt - 1 - i)),
                  _rows(tm, 4 * wq, 0, nt), _rows(tm, 4 * wq, 0, nt), _rows(tm, 128, 0, nt),
                  _whole((8, 4 * wq)), _whole((NV, 1)), _whole((NV, 1))],
        out_specs=[_rows(tm, 4 * wq, 0, nt), _rows(tm, 128, 0, nt), _whole((8, 4 * wq)), _whole((NV, 128))],
        out_shape=[jax.ShapeDtypeStruct((s, 4 * wq), MXU_DT), jax.ShapeDtypeStruct((s, 128), MXU_DT),
                   jax.ShapeDtypeStruct((8, 4 * wq), F32), jax.ShapeDtypeStruct((NV, 128), F32)],
        scratch_shapes=[pltpu.VMEM((tm, 4 * wq), F32), pltpu.VMEM((8, 4 * wq), F32)],
        compiler_params=_params(("arbitrary",)),
    )(dq, dk, dv, drt, pqkv, cpre, pab, cw, acol, dcol)


def _rg_bwd(drec, px, xcs, h, cw, vec, wa, wx, wat, wxt, side=None):
    s = px.shape[0]
    tm = _pick(s, (256, 128))
    nt = s // tm
    w = D_RNN
    n_in = len(side.ins) if side else 0
    n_out = len(side.out_shapes) if side else 0

    def body(*refs):
        (dr_ref, xr_ref, xc_ref, gr_ref, h_ref, hb_ref, cw_ref, vec_ref, wa_ref, wx_ref, wat_ref,
         wxt_ref) = refs[:12]
        s_in = refs[12:12 + n_in]
        o_ref, gwa_ref, gwx_ref, gcw_ref, gvec_ref = refs[12 + n_in:17 + n_in]
        s_out = refs[17 + n_in:17 + n_in + n_out]
        nxt_a, nxt_l, nxt_d = refs[17 + n_in + n_out:20 + n_in + n_out]
        sems = refs[20 + n_in + n_out:]
        i = pl.program_id(0)
        if side:
            @pl.when(i == 0)
            def _():
                side.start(s_in, s_out, sems)

        @pl.when(i == 0)
        def _():
            nxt_a[...] = jnp.zeros_like(nxt_a)
            nxt_l[...] = jnp.zeros_like(nxt_l)
            nxt_d[...] = jnp.zeros_like(nxt_d)
            gwa_ref[...] = jnp.zeros_like(gwa_ref)
            gwx_ref[...] = jnp.zeros_like(gwx_ref)
            gcw_ref[...] = jnp.zeros_like(gcw_ref)
            gvec_ref[...] = jnp.zeros_like(gvec_ref)
        hbefore = jnp.where(i == nt - 1, 0.0, hb_ref[...])
        xc = xc_ref[...]
        r, ig, a, mult, sp, rmult = _rg_gates(xc, wa_ref, wx_ref, vec_ref)
        hh = h_ref[...]
        gel, dgel = _gelu_and_grad(gr_ref[...])
        drec_ = dr_ref[...]
        o_ref[:, w:] = (drec_ * hh * dgel).astype(MXU_DT)
        acum, lam = _scan_rev(_shift_up(a, nxt_a[...], 1), drec_ * gel)
        lam = lam + acum * nxt_l[0:1, :]
        nxt_a[...] = a[:8, :]
        nxt_l[...] = lam[:8, :]
        da = lam * _shift_down(hh, hbefore, 1)
        dxc = lam * mult * ig
        dla = da * a - (lam * ig * xc) * (a * a) * rmult
        dpr = dla * (-RG_C * sp) * r * (1.0 - r)
        dpi = (lam * mult * xc) * ig * (1.0 - ig)
        dprb = dpr.astype(MXU_DT)
        dpib = dpi.astype(MXU_DT)
        dxc = dxc + jnp.dot(dprb, wat_ref[...], preferred_element_type=F32) \
                  + jnp.dot(dpib, wxt_ref[...], preferred_element_type=F32)
        xcb = xc.astype(MXU_DT)
        gwa_ref[...] += _dot_tn(xcb, dprb)
        gwx_ref[...] += _dot_tn(xcb, dpib)
        o_ref[:, :w] = _conv_causal_bwd(dxc, nxt_d[...], cw_ref, 4, xr_ref[...], gcw_ref).astype(MXU_DT)
        nxt_d[...] = dxc[:8, :]
        gvec_ref[0:1, :] += _rsum(dxc)
        gvec_ref[1:2, :] += _rsum(dpr)
        gvec_ref[2:3, :] += _rsum(dpi)
        gvec_ref[3:4, :] += _rsum(dla * (-RG_C * r)) * (-_sigmoid(-vec_ref[3:4, :]))
        if side:
            @pl.when(i == nt - 1)
            def _():
                side.finish(s_in, s_out, sems)

    return pl.pallas_call(
        body, name="rg_bwd", grid=(nt,),
        in_specs=[_rows(tm, w, 0, nt), _rows(tm, w, 0, nt), _rows(tm, w, 0, nt), _rows(tm, w, 1, nt),
                  _rows(tm, w, 0, nt), _before(tm, w, 0, nt), _whole((8, w)), _whole((8, w)),
                  _whole((w, w)), _whole((w, w)), _whole((w, w)), _whole((w, w))] + [_HBM] * n_in,
        out_specs=[_rows(tm, 2 * w, 0, nt), _whole((w, w)), _whole((w, w)), _whole((8, w)), _whole((8, w))]
        + [_HBM] * n_out,
        out_shape=[jax.ShapeDtypeStruct((s, 2 * w), MXU_DT), jax.ShapeDtypeStruct((w, w), F32),
                   jax.ShapeDtypeStruct((w, w), F32), jax.ShapeDtypeStruct((8, w), F32),
                   jax.ShapeDtypeStruct((8, w), F32)] + (list(side.out_shapes) if side else []),
        scratch_shapes=[pltpu.VMEM((8, w), F32)] * 3 + (list(side.sems) if side else []),
        compiler_params=_params(("arbitrary",)),
    )(drec, px, xcs, px, h, h, cw, vec, wa, wx, wat, wxt, *(side.ins if side else []))


def _modulate_bwd(dh1, dxa, x, adam):
    s = x.shape[0]
    tm = _pick(s, (512, 256, 128))

    def body(dh_ref, dxa_ref, x_ref, ada_ref, gx_ref, red_ref):
        @pl.when(pl.program_id(0) == 0)
        def _():
            red_ref[...] = jnp.zeros_like(red_ref)
        dh = dh_ref[...]
        gx_ref[...] = dxa_ref[...] + dh * (1.0 + ada_ref[1:2, :])
        red_ref[0:1, :] += _rsum(dh * x_ref[...])
        red_ref[1:2, :] += _rsum(dh)

    return pl.pallas_call(
        body, name="modulate1_bwd", grid=(s // tm,),
        in_specs=[_rows(tm, D)] * 3 + [_whole((8, D))],
        out_specs=[_rows(tm, D), _whole((8, D))],
        out_shape=[jax.ShapeDtypeStruct((s, D), F32), jax.ShapeDtypeStruct((8, D), F32)],
        compiler_params=_params(("arbitrary",)),
    )(dh1, dxa, x, adam)


def _adamw(parts, w, m, v, name):
    r, c = w.shape
    n_parts = len(parts)
    tm = _row_tile(r, c * 4 * (n_parts + 7))
    c1 = 1.0 - ADAM_B1 ** ADAM_STEP
    c2 = 1.0 - ADAM_B2 ** ADAM_STEP

    def body(*refs):
        g = refs[0][...]
        for p_ref in refs[1:n_parts]:
            g = g + p_ref[...]
        w_ref, m_ref, v_ref, g_out, d_out, m_out, v_out = refs[n_parts:]
        mn = ADAM_B1 * m_ref[...] + (1.0 - ADAM_B1) * g
        vn = ADAM_B2 * v_ref[...] + (1.0 - ADAM_B2) * (g * g)
        g_out[...] = g
        m_out[...] = mn
        v_out[...] = vn
        d_out[...] = -ADAM_LR * ((mn / c1) / (jnp.sqrt(vn / c2) + ADAM_EPS) + ADAM_WD * w_ref[...])

    spec = pl.BlockSpec((tm, c), lambda i: (i, 0))
    return pl.pallas_call(
        body, name=name, grid=(r // tm,),
        in_specs=[spec] * (n_parts + 3), out_specs=[spec] * 4,
        out_shape=[jax.ShapeDtypeStruct((r, c), F32)] * 4, compiler_params=_params(("parallel",)),
    )(*parts, w, m, v)


def _row_tile(r, bytes_per_row):
    for t in (512, 256, 128, 64, 32, 16):
        if r % t == 0 and 2 * t * bytes_per_row <= 20 * 1024 * 1024:
            return t
    return _pick(r, (16, 8))


def _sum_partials(own, recv, name):
    r, c = own.shape
    tm = _row_tile(r, c * (4 + 4 + 3 * 2))

    def body(o_ref, r_ref, out_ref):
        out_ref[...] = ((o_ref[...] + r_ref[0].astype(F32)) + r_ref[1].astype(F32)) + r_ref[2].astype(F32)

    return pl.pallas_call(
        body, name=name, grid=(r // tm,),
        in_specs=[pl.BlockSpec((tm, c), lambda i: (i, 0)), pl.BlockSpec((3, tm, c), lambda i: (0, i, 0))],
        out_specs=pl.BlockSpec((tm, c), lambda i: (i, 0)),
        out_shape=jax.ShapeDtypeStruct((r, c), F32), compiler_params=_params(("parallel",)),
    )(own, recv)


def _sum8(g):
    _, r, c = g.shape
    tm = _pick(r, (256, 128, 64, 32, 16, 8))

    def body(g_ref, o_ref):
        acc = g_ref[0]
        for k in range(1, 8):
            acc = acc + g_ref[k]
        o_ref[...] = acc

    return pl.pallas_call(
        body, name="sum8", grid=(r // tm,),
        in_specs=[pl.BlockSpec((8, tm, c), lambda i: (0, i, 0))],
        out_specs=pl.BlockSpec((tm, c), lambda i: (i, 0)),
        out_shape=jax.ShapeDtypeStruct((r, c), F32), compiler_params=_params(("parallel",)),
    )(g)


def _silu_rows(x):
    def body(x_ref, o_ref):
        xx = x_ref[...]
        o_ref[...] = xx * _sigmoid(xx)

    return pl.pallas_call(body, name="silu_rows", out_shape=jax.ShapeDtypeStruct(x.shape, F32))(x)


def _coords():
    return lax.axis_index("x"), lax.axis_index("y"), lax.axis_index("c")


def _allgather8(v, name):
    r, n = v.shape

    def body(v_ref, out_ref, send_sems, recv_sems):
        x, y, c = _coords()
        me = 4 * x + 2 * y + c
        out_ref[me] = v_ref[...]
        peers = []
        for k in range(1, 8):
            px = 1 - x if k & 4 else x
            py = 1 - y if k & 2 else y
            pc = 1 - c if k & 1 else c
            peers.append((px, py, pc))

        def copy(k, slot, to):
            return pltpu.make_async_remote_copy(
                src_ref=v_ref, dst_ref=out_ref.at[slot], send_sem=send_sems.at[k], recv_sem=recv_sems.at[k],
                device_id=to, device_id_type=MESH)

        sends = [copy(k, me, p) for k, p in enumerate(peers)]
        for cp in sends:
            cp.start()
        for k, (px, py, pc) in enumerate(peers):
            copy(k, 4 * px + 2 * py + pc, (px, py, pc)).wait_recv()
        for cp in sends:
            cp.wait_send()

    return pl.pallas_call(
        body, name=name, out_shape=jax.ShapeDtypeStruct((8, r, n), v.dtype),
        in_specs=[pl.BlockSpec(memory_space=pltpu.VMEM)], out_specs=pl.BlockSpec(memory_space=pltpu.VMEM),
        scratch_shapes=[pltpu.SemaphoreType.DMA((7,)), pltpu.SemaphoreType.DMA((7,))],
        compiler_params=pltpu.CompilerParams(vmem_limit_bytes=VMEM_LIMIT),
    )(v)


def _other_chips(x, y):
    return [(1 - x, y), (x, 1 - y), (1 - x, 1 - y)]


_HBM = pl.BlockSpec(memory_space=pl.ANY)


class _Exchange:
    def __init__(self, ins, out_shapes, sems, start, finish):
        self.ins, self.out_shapes, self.sems, self.start, self.finish = ins, out_shapes, sems, start, finish


def _run_exchange(ex, name):
    n_in, n_out = len(ex.ins), len(ex.out_shapes)

    def body(*refs):
        ins, outs, sems = refs[:n_in], refs[n_in:n_in + n_out], refs[n_in + n_out:]
        ex.start(ins, outs, sems)
        ex.finish(ins, outs, sems)

    return pl.pallas_call(body, name=name, out_shape=list(ex.out_shapes), in_specs=[_HBM] * n_in,
                          out_specs=[_HBM] * n_out, scratch_shapes=list(ex.sems))(*ex.ins)


def _gather_exchange(shards):
    n = len(shards)

    def plan(ins, outs, sems):
        send_sems, recv_sems, local_sems = sems
        x, y, c = _coords()
        s_me = 2 * x + y
        chips = _other_chips(x, y)

        def half(i, slot, hc):
            rh = shards[i].shape[0] // 2
            return outs[i].at[slot, pl.ds(pl.multiple_of(hc * rh, 16), rh), :]

        def copy(i, k, src, dst, to):
            return pltpu.make_async_remote_copy(src_ref=src, dst_ref=dst, send_sem=send_sems.at[6 * i + k],
                                                recv_sem=recv_sems.at[6 * i + k], device_id=to, device_id_type=MESH)

        local = [pltpu.make_async_copy(ins[i], outs[i].at[s_me], local_sems.at[i]) for i in range(n)]
        first = []
        for i in range(n):
            rh = shards[i].shape[0] // 2
            my_half = ins[i].at[pl.ds(pl.multiple_of(c * rh, 16), rh), :]
            first += [copy(i, j, my_half, half(i, s_me, c), (px, py, c)) for j, (px, py) in enumerate(chips)]
        return (x, y, c), chips, half, copy, local, first

    def start(ins, outs, sems):
        _, _, _, _, local, first = plan(ins, outs, sems)
        for cp in local + first:
            cp.start()

    def finish(ins, outs, sems):
        (x, y, c), chips, half, copy, local, first = plan(ins, outs, sems)
        sibling = (x, y, 1 - c)
        passed = []
        for i in range(n):
            for j, (px, py) in enumerate(chips):
                land = half(i, 2 * px + py, c)
                copy(i, j, land, land, (px, py, c)).wait_recv()
                fw = copy(i, 3 + j, land, land, sibling)
                fw.start()
                passed.append(fw)
        for i in range(n):
            for j, (px, py) in enumerate(chips):
                land = half(i, 2 * px + py, 1 - c)
                copy(i, 3 + j, land, land, sibling).wait_recv()
        for cp in first + passed:
            cp.wait_send()
        for cp in local:
            cp.wait()

    return _Exchange(list(shards), [jax.ShapeDtypeStruct((4,) + v.shape, v.dtype) for v in shards],
                     [pltpu.SemaphoreType.DMA((6 * n,)), pltpu.SemaphoreType.DMA((6 * n,)),
                      pltpu.SemaphoreType.DMA((n,))], start, finish)


def _scatter_exchange(gs):
    n = len(gs)

    def copies(ins, outs, sems):
        send_sems, recv_sems = sems
        x, y, c = _coords()
        return [pltpu.make_async_remote_copy(
                    src_ref=ins[i].at[2 * px + py], dst_ref=outs[i].at[j], send_sem=send_sems.at[3 * i + j],
                    recv_sem=recv_sems.at[3 * i + j], device_id=(px, py, c), device_id_type=MESH)
                for i in range(n) for j, (px, py) in enumerate(_other_chips(x, y))]

    def start(ins, outs, sems):
        for cp in copies(ins, outs, sems):
            cp.start()

    def finish(ins, outs, sems):
        cps = copies(ins, outs, sems)
        for cp in cps:
            cp.wait_recv()
        for cp in cps:
            cp.wait_send()

    return _Exchange(list(gs), [jax.ShapeDtypeStruct((3,) + g.shape[1:], g.dtype) for g in gs],
                     [pltpu.SemaphoreType.DMA((3 * n,)), pltpu.SemaphoreType.DMA((3 * n,))], start, finish)


def _swap_sibling(vs):
    n = len(vs)

    def body(*refs):
        ins, outs = refs[:n], refs[n:2 * n]
        send_sems, recv_sems = refs[2 * n:]
        x, y, c = _coords()
        cps = [pltpu.make_async_remote_copy(src_ref=ins[i], dst_ref=outs[i], send_sem=send_sems.at[i],
                                            recv_sem=recv_sems.at[i], device_id=(x, y, 1 - c), device_id_type=MESH)
               for i in range(n)]
        for cp in cps:
            cp.start()
        for cp in cps:
            cp.wait()

    return pl.pallas_call(
        body, name="swap_sibling", out_shape=[jax.ShapeDtypeStruct(v.shape, v.dtype) for v in vs],
        in_specs=[_HBM] * n, out_specs=[_HBM] * n,
        scratch_shapes=[pltpu.SemaphoreType.DMA((n,)), pltpu.SemaphoreType.DMA((n,))],
    )(*vs)


def _pad_rows(a, rows):
    return jnp.pad(a, ((0, rows - a.shape[0]), (0, 0)))


def _block_diag(w):
    eye = jnp.eye(RG_BLOCKS, dtype=w.dtype)
    return (eye[:, None, :, None] * w[:, :, None, :]).reshape(D_RNN, D_RNN)


def _diag_blocks(g):
    g4 = g.reshape(RG_BLOCKS, RG_BW, RG_BLOCKS, RG_BW)
    idx = jnp.arange(RG_BLOCKS)
    return g4[idx, :, idx, :]


def _prepare_rest(p):
    w = {}
    for k_, n_ in (("pa", "w_proj_a"), ("pb", "w_proj_b"), ("out", "w_out"), ("down", "ffn_w_down")):
        w[k_] = p[n_].astype(MXU_DT)
        w[k_ + "_t"] = w[k_].T
    w["gu"] = jnp.concatenate([p["ffn_w_gate"], p["ffn_w_up"]], axis=1).astype(MXU_DT)
    w["gu_t"] = w["gu"].T
    return w


def _prepare_first(p):
    w = {}
    wi = p["w_in"].astype(MXU_DT)
    cat = jnp.concatenate([wi[:, 2560:6656], wi[:, 6656:8704], wi[:, 8736:10784], wi[:, 0:2560],
                           wi[:, 8704:8736], jnp.zeros((D, 96), MXU_DT)], axis=1)
    w["in_cat"], w["in_cat_t"] = cat, cat.T
    w["rg_cw"] = _pad_rows(p["rg_conv_w"], 8)
    w["dn_cw"] = _pad_rows(p["dn_conv_w"], 8)
    w["ffn_cw"] = _pad_rows(p["ffn_conv_w"], 8)
    w["rg_vec"] = _pad_rows(jnp.stack([p["rg_conv_b"], p["rg_b_a"], p["rg_b_x"], p["rg_lambda"]]), 8)
    w["wa"] = _block_diag(p["rg_w_a"]).astype(MXU_DT)
    w["wx"] = _block_diag(p["rg_w_x"]).astype(MXU_DT)
    w["wa_t"], w["wx_t"] = w["wa"].T, w["wx"].T
    w["arow"] = jnp.pad(p["dn_a_log"], (0, 128 - NV))[None, :]
    w["drow"] = jnp.pad(p["dn_dt_bias"], (0, 128 - NV))[None, :]
    w["acol"] = p["dn_a_log"][:, None]
    w["dcol"] = p["dn_dt_bias"][:, None]
    w["nw"] = p["dn_norm_w"][None, :]
    w["ffn_cb"] = p["ffn_conv_b"][None, :]
    for n_ in ("ln1_g", "ln1_b", "ln2_g", "ln2_b"):
        w[n_] = p[n_][None, :]
    return w


def _mm_sided(a, b, side, **kw):
    if side is None:
        return _mm(a, b, **kw), []
    res = _mm(a, b, side=side, **kw)
    return res[0], res[1:]


_REST_A = ("w_proj_a", "w_proj_b", "w_out")
_REST_B = ("ffn_w_gate", "ffn_w_up", "ffn_w_down")


def _local_step(x, tgt, adam, p, shards=None):
    w = _prepare_first(p)
    side_a = side_b = None
    if shards is not None:
        side_a = _gather_exchange([shards[n] for n in _REST_A])
        side_b = _gather_exchange([shards[n] for n in _REST_B])
    h1 = _modulate(x, adam)
    cat = w["in_cat"]
    pqkv, got_a = _mm_sided(h1, cat[:, 0:4096], side_a, name="proj_qkv")
    px, got_b = _mm_sided(h1, cat[:, 8192:10752], side_b, name="proj_x")
    if shards is not None:
        p = dict(p, **{n: _unstack_shards(n, a_) for n, a_ in zip(_REST_A + _REST_B, got_a + got_b)})
    w.update(_prepare_rest(p))
    pz = _mm(h1, cat[:, 4096:6144], name="proj_z")
    pg = _mm(h1, cat[:, 6144:8192], name="proj_g")
    pab = _mm(h1, cat[:, 10752:10880], name="proj_ab")
    hrec, xcs, rec = _rg_fwd(px, w["rg_cw"], w["rg_vec"], w["wa"], w["wx"])
    q, k, v, rt, cpre = _dn_prep(pqkv, pab, w["dn_cw"], w["arow"], w["drow"])
    u, ww, qd, kd, pm, tinv, gam = _dn_intra(q, k, v, rt)
    o, vn, ssave = _dn_seq(u, ww, qd, kd, pm, gam)
    dn = _dn_post(o, pz, w["nw"])
    ya = _mm(rec, w["pa"], name="proj_a")
    yb = _mm(dn, w["pb"], name="proj_b")
    merged = _merge(pg, ya, yb)
    mix = _mm(merged, w["out"], name="proj_out")
    r1, x2, h2 = _ln1(x, mix, adam, w["ln1_g"], w["ln1_b"])
    gu = _mm(h2, w["gu"], name="ffn_gu")
    act, gcf = _ffn_act(gu, w["ffn_cw"], w["ffn_cb"])
    ff = _mm(act, w["down"], name="ffn_down")
    dff, dx2a, red2 = _ln2_loss(x2, ff, tgt, adam, w["ln2_g"], w["ln2_b"])
    g = {}
    dact = _mm(dff, w["down_t"], name="d_act")
    g["ffn_w_down"] = _mm(act, dff, name="g_down", trans_a=True)
    dgu, gcw_f, gcb_f = _ffn_bwd(dact, gu, gcf, w["ffn_cw"])
    dh2 = _mm(dgu, w["gu_t"], name="d_h2")
    ggu = _mm(h2, dgu, name="g_gu", trans_a=True)
    g["ffn_w_gate"], g["ffn_w_up"] = ggu[:, :D_FF], ggu[:, D_FF:]
    g["ffn_conv_w"], g["ffn_conv_b"] = gcw_f[0:3], gcb_f[0]
    dmix, dxa, red1 = _ln1_bwd(dh2, dx2a, x2, r1, mix, adam, w["ln1_g"])
    dmg = _mm(dmix, w["out_t"], name="d_merged")
    g["w_out"] = _mm(merged, dmix, name="g_out", trans_a=True)
    dya, dyb, dpg = _merge_bwd(dmg, pg, ya, yb)
    drec = _mm(dya, w["pa_t"], name="d_rec")
    g["w_proj_a"] = _mm(rec, dya, name="g_pa", trans_a=True)
    ddn = _mm(dyb, w["pb_t"], name="d_dn")
    g["w_proj_b"] = _mm(dn, dyb, name="g_pb", trans_a=True)
    do, dpz, gnw = _dn_post_bwd(ddn, o, pz, w["nw"])
    dvn, dkd, dgam = _dn_seq_bwd(do, qd, kd, pm, ww, vn, ssave, gam)
    dq, dk, dv, drt = _dn_intra_bwd(q, k, v, rt, do, dvn, dkd, gam, dgam, ssave, tinv, u, ww, vn)
    dpqkv, dpab, gcw_d, gsc = _dn_prep_bwd(dq, dk, dv, drt, pqkv, cpre, pab, w["dn_cw"], w["acol"], w["dcol"])
    side_r = None
    if shards is not None:
        side_r = _scatter_exchange([_stack_shards(n, g[n]).astype(MXU_DT) for n in _REST_A + _REST_B])
    dpx, gwa, gwx, gcw_r, gvec, *got_r = _rg_bwd(drec, px, xcs, hrec, w["rg_cw"], w["rg_vec"], w["wa"], w["wx"],
                                                 w["wa_t"], w["wx_t"], side_r)
    dproj = jnp.concatenate([dpqkv, dpz, dpg, dpx, dpab], axis=1)
    reorder = lambda gc: jnp.concatenate([gc[:, 8192:10752], gc[:, 0:4096], gc[:, 4096:6144], gc[:, 10752:10784],
                                          gc[:, 6144:8192]], axis=1)
    wire = lambda gh: _scatter_exchange([_stack_shards("w_in", gh).astype(MXU_DT)]) if shards is not None else None
    h1t = h1.T
    g_top = reorder(_mm(h1t[:D // 2], dproj, name="g_in_top"))
    g_bot, got_top = _mm_sided(h1t[D // 2:], dproj, wire(g_top), name="g_in_bot")
    g_bot = reorder(g_bot)
    dh1, got_bot = _mm_sided(dproj, w["in_cat_t"], wire(g_bot), name="d_h1")
    g["w_in"] = jnp.concatenate([g_top, g_bot], axis=0)
    got = dict(zip(_REST_A + _REST_B, got_r))
    if shards is not None:
        got["w_in"] = jnp.concatenate([got_top[0], got_bot[0]], axis=1)
    gx, red0 = _modulate_bwd(dh1, dxa, x, adam)
    g["rg_conv_w"], g["rg_conv_b"] = gcw_r[0:4], gvec[0]
    g["rg_w_a"], g["rg_w_x"] = _diag_blocks(gwa), _diag_blocks(gwx)
    g["rg_b_a"], g["rg_b_x"], g["rg_lambda"] = gvec[1], gvec[2], gvec[3]
    g["dn_conv_w"] = gcw_d[0:4]
    g["dn_a_log"], g["dn_dt_bias"], g["dn_norm_w"] = gsc[:, 0], gsc[:, 1], gnw[0]
    g["ln1_g"], g["ln1_b"] = red1[2], red1[3]
    g["ln2_g"], g["ln2_b"] = red2[0], red2[1]
    d_ada = jnp.concatenate([red0[1], red0[0], red1[4], red1[1], red1[0], red2[2]])
    return red2[3, 0], gx, g, d_ada, got


_BIG = ("w_in", "w_proj_a", "w_proj_b", "w_out", "ffn_w_gate", "ffn_w_up", "ffn_w_down")
_COL_SHARDED = ("w_in", "ffn_w_gate", "ffn_w_up")
_CONV = ("rg_conv_w", "dn_conv_w", "ffn_conv_w")
_REPL = ("b_ada", "rg_conv_b", "rg_w_a", "rg_b_a", "rg_w_x", "rg_b_x", "rg_lambda", "dn_a_log",
         "dn_dt_bias", "dn_norm_w", "ln1_g", "ln1_b", "ffn_conv_b", "ln2_g", "ln2_b")
_NAMES = ("w_ada", "b_ada", "w_in", "rg_conv_w", "rg_conv_b", "rg_w_a", "rg_b_a", "rg_w_x", "rg_b_x",
          "rg_lambda", "dn_conv_w", "dn_a_log", "dn_dt_bias", "dn_norm_w", "w_proj_a", "w_proj_b", "w_out",
          "ln1_g", "ln1_b", "ffn_w_gate", "ffn_w_up", "ffn_conv_w", "ffn_conv_b", "ffn_w_down", "ln2_g", "ln2_b")


def _pack(arrs, width, row_mult):
    pieces = []
    for a in arrs:
        f = a.reshape(-1)
        pieces.append(jnp.pad(f, (0, (-f.shape[0]) % (8 * width))).reshape(-1, width))
    rows = sum(p_.shape[0] for p_ in pieces)
    if rows % row_mult:
        pieces.append(jnp.zeros((row_mult - rows % row_mult, width), pieces[0].dtype))
    return jnp.concatenate(pieces, axis=0)


def _unpack(flat, shapes, width):
    out, row = [], 0
    for shp in shapes:
        n = 1
        for d_ in shp:
            n *= d_
        rows = -(-n // (8 * width)) * 8
        out.append(flat[row:row + rows].reshape(-1)[:n].reshape(shp))
        row += rows
    return out


def _stack_shards(name, full):
    if name in _COL_SHARDED or name in _CONV:
        r, ncol = full.shape
        return full.reshape(r, 4, ncol // 4).transpose(1, 0, 2)
    return full.reshape((4, full.shape[0] // 4) + full.shape[1:])


def _unstack_shards(name, st):
    if name in _COL_SHARDED or name in _CONV:
        return st.transpose(1, 0, 2).reshape(st.shape[1], 4 * st.shape[2])
    return st.reshape((4 * st.shape[1],) + st.shape[2:])


def kernel(x, c, w_ada, b_ada, w_in, rg_conv_w, rg_conv_b, rg_w_a, rg_b_a, rg_w_x, rg_b_x, rg_lambda, dn_conv_w, dn_a_log, dn_dt_bias, dn_norm_w, w_proj_a, w_proj_b, w_out, ln1_g, ln1_b, ffn_w_gate, ffn_w_up, ffn_conv_w, ffn_conv_b, ffn_w_down, ln2_g, ln2_b, loss_target, m_w_ada, m_b_ada, m_w_in, m_rg_conv_w, m_rg_conv_b, m_rg_w_a, m_rg_b_a, m_rg_w_x, m_rg_b_x, m_rg_lambda, m_dn_conv_w, m_dn_a_log, m_dn_dt_bias, m_dn_norm_w, m_w_proj_a, m_w_proj_b, m_w_out, m_ln1_g, m_ln1_b, m_ffn_w_gate, m_ffn_w_up, m_ffn_conv_w, m_ffn_conv_b, m_ffn_w_down, m_ln2_g, m_ln2_b, v_w_ada, v_b_ada, v_w_in, v_rg_conv_w, v_rg_conv_b, v_rg_w_a, v_rg_b_a, v_rg_w_x, v_rg_b_x, v_rg_lambda, v_dn_conv_w, v_dn_a_log, v_dn_dt_bias, v_dn_norm_w, v_w_proj_a, v_w_proj_b, v_w_out, v_ln1_g, v_ln1_b, v_ffn_w_gate, v_ffn_w_up, v_ffn_conv_w, v_ffn_conv_b, v_ffn_w_down, v_ln2_g, v_ln2_b):
    args = locals()
    wts = {n: args[n][0] for n in _NAMES}
    mom = {n: args["m_" + n][0] for n in _NAMES}
    var = {n: args["v_" + n][0] for n in _NAMES}
    xs, tgt = x[0], loss_target[0]
    ix, iy, ic = _coords()
    shard = 2 * ix + iy
    batch = 4 * ix + 2 * iy + ic

    c_all = _allgather8(_pad_rows(c, 8), "gather_c")[:, 0, :]
    sc16 = _pad_rows(_silu_rows(c_all), 16)
    ada_cols = _mm(sc16, w_ada[0], name="ada")[:8]
    ada_g = _allgather8(ada_cols, "gather_ada")
    ada_all = jnp.concatenate([ada_g[0], ada_g[2], ada_g[4], ada_g[6]], axis=1) + b_ada
    adam = _pad_rows(lax.dynamic_index_in_dim(ada_all, batch, 0, keepdims=False).reshape(6, D), 8)

    (w_in_all,) = _run_exchange(_gather_exchange([wts["w_in"].astype(MXU_DT)]), "gather_w_in")
    full = {"w_in": _unstack_shards("w_in", w_in_all)}
    conv_shard = _pack([wts[n] for n in _CONV], 128, 8)
    conv_all = _allgather8(conv_shard, "gather_conv")
    shapes_conv = [wts[n].shape for n in _CONV]
    per_shard = [_unpack(conv_all[2 * s], shapes_conv, 128) for s in range(4)]
    for i, n in enumerate(_CONV):
        full[n] = _unstack_shards(n, jnp.stack([per_shard[s][i] for s in range(4)]))
    for n in _REPL:
        full[n] = wts[n]

    shards = {n: wts[n].astype(MXU_DT) for n in _REST_A + _REST_B}
    loss_b, gx, g, d_ada, recv = _local_step(xs, tgt, adam, full, shards)

    parts = []
    for n in _BIG:
        r_ = recv[n]
        axis = 1 if n in _COL_SHARDED else 0
        width = wts[n].shape[axis]
        own = lax.dynamic_slice_in_dim(g[n], shard * width, width, axis=axis)
        parts.append(_sum_partials(own, r_, "sum_" + n))
    parts_sib = _swap_sibling(parts)
    out = {n: [] for n in _NAMES}
    for n, p_, q_ in zip(_BIG, parts, parts_sib):
        out[n] = list(_adamw([p_, q_], wts[n], mom[n], var[n], "adamw_" + n))

    small_names = _CONV + _REPL[1:]
    small = _pack([d_ada, jnp.full((1,), loss_b, F32)] + [g[n] for n in small_names], 128, 64)
    small_all = _allgather8(small, "gather_small")
    total = _sum8(small_all)
    shapes_small = [(6 * D,), (1,)] + [full[n].shape for n in small_names]
    tot = _unpack(total, shapes_small, 128)
    gsum = dict(zip(small_names, tot[2:]))
    gsum["b_ada"] = tot[0]
    loss = tot[1][0]
    for n in _CONV:
        gsum[n] = lax.dynamic_index_in_dim(_stack_shards(n, gsum[n]), shard, 0, keepdims=False)
    d_ada_all = small_all[:, :6 * D // 128, :].reshape(8, 6 * D)
    cols = lax.dynamic_slice_in_dim(d_ada_all, shard * (6 * D // 4), 6 * D // 4, axis=1)
    g_wada = _mm(sc16, _pad_rows(cols, 16), name="g_ada", trans_a=True)
    res = _adamw([g_wada], wts["w_ada"], mom["w_ada"], var["w_ada"], "adamw_ada")
    out["w_ada"] = list(res)
    names_s = _CONV + _REPL
    shapes_s = [wts[n].shape for n in names_s]
    pk = lambda d_: _pack([d_[n] for n in names_s], 128, 64)
    res_s = _adamw([pk(gsum)], pk(wts), pk(mom), pk(var), "adamw_small")
    for r_ in res_s:
        for n, a in zip(names_s, _unpack(r_, shapes_s, 128)):
            out[n].append(a)

    outs = [loss, gx[None]]
    for i in range(4):
        outs += [out[n][i][None] for n in _NAMES]
    return tuple(outs)
```

```python
import functools

import jax
import jax.numpy as jnp
from jax import lax
from jax.experimental import pallas as pl
from jax.experimental.pallas import tpu as pltpu

F32 = jnp.float32
BF16 = jnp.bfloat16
MXU_DT = BF16

D = 1024
D_RNN = 1280
RG_BLOCKS = 16
RG_BW = 80
RG_C = 8.0
NQ = 8
NV = 16
HD = 128
CH = 64
D_FF = 2816
LN_EPS = 1e-5
RMS_EPS = 1e-6
L2_EPS = 1e-6
ALPHA = 2.0 ** 0.25
Q_SCALE = HD ** -0.5
N_CAT = 10880
VMEM_LIMIT = 56 * 1024 * 1024
MM_VMEM_BUDGET = 36 * 1024 * 1024
SEQ_GROUP = 8
MESH = pl.DeviceIdType.MESH

ADAM_LR, ADAM_B1, ADAM_B2, ADAM_EPS, ADAM_WD, ADAM_STEP = 1e-3, 0.9, 0.999, 1e-8, 0.01, 10


def _sigmoid(x):
    return 0.5 * jnp.tanh(0.5 * x) + 0.5


def _softplus(x):
    return jnp.maximum(x, 0.0) + jnp.log1p(jnp.exp(-jnp.abs(x)))


_GC = 0.7978845608028654


def _gelu(x):
    return 0.5 * x * (1.0 + jnp.tanh(_GC * (x + 0.044715 * x * x * x)))


def _gelu_and_grad(x):
    t = jnp.tanh(_GC * (x + 0.044715 * x * x * x))
    g = 0.5 * x * (1.0 + t)
    dg = 0.5 * (1.0 + t) + 0.5 * x * (1.0 - t * t) * _GC * (1.0 + 3 * 0.044715 * x * x)
    return g, dg


def _neg_expm1(y):
    series = -y * (1.0 + 0.5 * y * (1.0 + y * (1.0 / 3.0)))
    return jnp.where(y > -0.01, series, 1.0 - jnp.exp(y))


def _dot(a, b):
    return jnp.dot(a.astype(MXU_DT), b.astype(MXU_DT), preferred_element_type=F32)


def _dot_nt(a, b):
    return lax.dot_general(a.astype(MXU_DT), b.astype(MXU_DT), (((1,), (1,)), ((), ())),
                           preferred_element_type=F32)


def _dot_tn(a, b):
    return lax.dot_general(a.astype(MXU_DT), b.astype(MXU_DT), (((0,), (0,)), ((), ())),
                           preferred_element_type=F32)


def _split(a):
    hi = a.astype(BF16)
    return hi, (a - hi.astype(F32)).astype(BF16)


def _dot3(a, b, dims=(((1,), (0,)), ((), ()))):
    ah, al = _split(a)
    bh, bl = _split(b)
    d = lambda p, q: lax.dot_general(p, q, dims, preferred_element_type=F32)
    return d(ah, bh) + (d(al, bh) + d(ah, bl))


def _dot3_tn(a, b):
    return _dot3(a, b, (((0,), (0,)), ((), ())))


def _iota(shape, dim):
    return lax.broadcasted_iota(jnp.int32, shape, dim)


def _shift_down(x, before, j):
    if j == 0:
        return x
    xr = pltpu.roll(x, j, 0)
    br = pltpu.roll(before, j, 0)
    top = jnp.where(_iota(br.shape, 0) < j, br, xr[:8])
    return jnp.concatenate([top, xr[8:]], axis=0)


def _shift_up(x, after, j):
    if j == 0:
        return x
    t = x.shape[0]
    xr = pltpu.roll(x, t - j, 0)
    ar = pltpu.roll(after, 8 - j, 0)
    bot = jnp.where(_iota(ar.shape, 0) >= 8 - j, ar, xr[t - 8:])
    return jnp.concatenate([xr[:t - 8], bot], axis=0)


def _taps(x, before, k):
    return [_shift_down(x, before, k - 1 - i) for i in range(k)]


def _conv_taps(taps, w_ref):
    y = w_ref[0:1, :] * taps[0]
    for i in range(1, len(taps)):
        y = y + w_ref[i:i + 1, :] * taps[i]
    return y


def _conv_causal(x, before, w_ref, k):
    return _conv_taps(_taps(x, before, k), w_ref)


def _conv_causal_bwd(dy, after, w_ref, k, x=None, gw_ref=None):
    dx = None
    for i in range(k):
        sh = _shift_up(dy, after, k - 1 - i)
        term = w_ref[i:i + 1, :] * sh
        dx = term if dx is None else dx + term
        if x is not None:
            gw_ref[i:i + 1, :] += _rsum(x * sh)
    return dx


def _scan_fwd(a, u):
    t = a.shape[0]
    rows = _iota(a.shape, 0)
    d = 1
    while d < t:
        m = rows >= d
        u = u + jnp.where(m, a * pltpu.roll(u, d, 0), 0.0)
        a = jnp.where(m, a * pltpu.roll(a, d, 0), a)
        d *= 2
    return a, u


def _scan_rev(a, u):
    t = a.shape[0]
    rows = _iota(a.shape, 0)
    d = 1
    while d < t:
        m = rows < t - d
        u = u + jnp.where(m, a * pltpu.roll(u, t - d, 0), 0.0)
        a = jnp.where(m, a * pltpu.roll(a, t - d, 0), a)
        d *= 2
    return a, u


def _chunk_cumsum(g, axis, rev=False):
    n = g.shape[axis]
    pos = _iota(g.shape, axis) & (CH - 1)
    d = 1
    while d < CH:
        if rev:
            g = g + jnp.where(pos < CH - d, pltpu.roll(g, n - d, axis), 0.0)
        else:
            g = g + jnp.where(pos >= d, pltpu.roll(g, d, axis), 0.0)
        d *= 2
    return g


def _ln_stats(r):
    mu = jnp.mean(r, axis=-1, keepdims=True)
    xc = r - mu
    var = jnp.mean(xc * xc, axis=-1, keepdims=True)
    rstd = lax.rsqrt(var + LN_EPS)
    return xc * rstd, rstd


def _ln_bwd(dy, xhat, rstd, g):
    dxh = dy * g
    return rstd * (dxh - jnp.mean(dxh, axis=-1, keepdims=True)
                   - xhat * jnp.mean(dxh * xhat, axis=-1, keepdims=True))


def _rsum(x):
    return jnp.sum(x, axis=0, keepdims=True)


def _params(sem):
    return pltpu.CompilerParams(dimension_semantics=sem, vmem_limit_bytes=VMEM_LIMIT)


def _pick(n, cands):
    for c in cands:
        if n % c == 0:
            return c
    return n


def _rows(tm, w, col=0, nt=None):
    if nt is None:
        return pl.BlockSpec((tm, w), lambda i: (i, col))
    return pl.BlockSpec((tm, w), lambda i: (nt - 1 - i, col))


def _before(tm, w, col=0, nt=None):
    r = tm // 8
    if nt is None:
        return pl.BlockSpec((8, w), lambda i: (jnp.maximum(i * r - 1, 0), col))
    return pl.BlockSpec((8, w), lambda i: (jnp.maximum((nt - 1 - i) * r - 1, 0), col))


def _cols(tm, w):
    return pl.BlockSpec((w, tm), lambda i: (0, i))


def _whole(shape):
    return pl.BlockSpec(shape, lambda *_: (0,) * len(shape))


def _mm_plan(a, b, out_dtype):
    m, kk = a.shape
    _, n = b.shape
    tm = _pick(m, (512, 256, 128))
    tk = kk if kk <= 5632 else _pick(kk, (2176, 2048, 1024))
    nk = kk // tk

    def vmem_bytes(tn):
        blocks = tm * tk * a.dtype.itemsize + tk * tn * b.dtype.itemsize + tm * tn * jnp.dtype(out_dtype).itemsize
        return 2 * blocks + (tm * tn * 4 if nk > 1 else 0)

    cands = [t for t in (1408, 1280, 1024, 640, 512, 256, 128) if n % t == 0] or [n]
    tn = next((t for t in cands if vmem_bytes(t) <= MM_VMEM_BUDGET), cands[-1])
    return tm, tn, tk, nk


def _mm_side(a, b, name, out_dtype, side):
    m, _ = a.shape
    _, n = b.shape
    tm, tn, tk, nk = _mm_plan(a, b, out_dtype)
    ni, nj = m // tm, n // tn
    n_in, n_out = len(side.ins), len(side.out_shapes)

    def body(*refs):
        a_ref, b_ref = refs[0], refs[1]
        s_in = refs[2:2 + n_in]
        o_ref = refs[2 + n_in]
        s_out = refs[3 + n_in:3 + n_in + n_out]
        acc = refs[3 + n_in + n_out]
        sems = refs[4 + n_in + n_out:]
        i, j, k = pl.program_id(0), pl.program_id(1), pl.program_id(2)

        @pl.when((i == 0) & (j == 0) & (k == 0))
        def _():
            side.start(s_in, s_out, sems)

        @pl.when(k == 0)
        def _():
            acc[...] = jnp.zeros_like(acc)
        acc[...] += _dot(a_ref[...], b_ref[...])

        @pl.when(k == nk - 1)
        def _():
            o_ref[...] = acc[...].astype(out_dtype)

        @pl.when((i == ni - 1) & (j == nj - 1) & (k == nk - 1))
        def _():
            side.finish(s_in, s_out, sems)

    return pl.pallas_call(
        body, name=name, grid=(ni, nj, nk),
        in_specs=[pl.BlockSpec((tm, tk), lambda i, j, k: (i, k)),
                  pl.BlockSpec((tk, tn), lambda i, j, k: (k, j))] + [_HBM] * n_in,
        out_specs=[pl.BlockSpec((tm, tn), lambda i, j, k: (i, j))] + [_HBM] * n_out,
        out_shape=[jax.ShapeDtypeStruct((m, n), out_dtype)] + list(side.out_shapes),
        scratch_shapes=[pltpu.VMEM((tm, tn), F32)] + list(side.sems),
        compiler_params=_params(("arbitrary", "arbitrary", "arbitrary")),
    )(a, b, *side.ins)


def _mm(a, b, *, name, trans_a=False, out_dtype=F32, side=None):
    if trans_a:
        return _mm(a.T, b, name=name, out_dtype=out_dtype, side=side)
    if side is not None:
        return _mm_side(a, b, name, out_dtype, side)
    m, kk = a.shape
    _, n = b.shape
    tm, tn, tk, nk = _mm_plan(a, b, out_dtype)

    if nk == 1:
        def body(a_ref, b_ref, o_ref):
            o_ref[...] = _dot(a_ref[...], b_ref[...]).astype(out_dtype)
        scratch = []
    else:
        def body(a_ref, b_ref, o_ref, acc):
            k = pl.program_id(2)

            @pl.when(k == 0)
            def _():
                acc[...] = jnp.zeros_like(acc)
            acc[...] += _dot(a_ref[...], b_ref[...])

            @pl.when(k == nk - 1)
            def _():
                o_ref[...] = acc[...].astype(out_dtype)
        scratch = [pltpu.VMEM((tm, tn), F32)]

    return pl.pallas_call(
        body, name=name, grid=(m // tm, n // tn, nk),
        in_specs=[pl.BlockSpec((tm, tk), lambda i, j, k: (i, k)),
                  pl.BlockSpec((tk, tn), lambda i, j, k: (k, j))],
        out_specs=pl.BlockSpec((tm, tn), lambda i, j, k: (i, j)),
        out_shape=jax.ShapeDtypeStruct((m, n), out_dtype),
        scratch_shapes=scratch,
        compiler_params=_params(("parallel", "parallel", "arbitrary")),
    )(a, b)


def _modulate(x, adam):
    s = x.shape[0]
    tm = _pick(s, (512, 256, 128))

    def body(x_ref, ada_ref, o_ref, ot_ref):
        h = x_ref[...] * (1.0 + ada_ref[1:2, :]) + ada_ref[0:1, :]
        o_ref[...] = h.astype(MXU_DT)
        ot_ref[...] = h.T.astype(MXU_DT)

    return pl.pallas_call(
        body, name="modulate1", grid=(s // tm,),
        in_specs=[_rows(tm, D), _whole((8, D))], out_specs=[_rows(tm, D), _cols(tm, D)],
        out_shape=[jax.ShapeDtypeStruct((s, D), MXU_DT), jax.ShapeDtypeStruct((D, s), MXU_DT)],
        compiler_params=_params(("parallel",)),
    )(x, adam)


def _rg_gates(xc, wa_ref, wx_ref, vec_ref):
    xb = xc.astype(MXU_DT)
    r = _sigmoid(jnp.dot(xb, wa_ref[...], preferred_element_type=F32) + vec_ref[1:2, :])
    ig = _sigmoid(jnp.dot(xb, wx_ref[...], preferred_element_type=F32) + vec_ref[2:3, :])
    sp = _softplus(-vec_ref[3:4, :])
    la = -RG_C * r * sp
    a = jnp.exp(la)
    n1 = _neg_expm1(2.0 * la)
    rmult = lax.rsqrt(jnp.maximum(n1, 1e-20))
    return r, ig, a, n1 * rmult, sp, rmult


def _rg_fwd(px, cw, vec, wa, wx):
    s = px.shape[0]
    tm = _pick(s, (256, 128))
    w = D_RNN

    def body(xr_ref, gr_ref, cw_ref, vec_ref, wa_ref, wx_ref, h_ref, xc_ref, rec_ref, rect_ref, prev_x, prev_h):
        @pl.when(pl.program_id(0) == 0)
        def _():
            prev_x[...] = jnp.zeros_like(prev_x)
            prev_h[...] = jnp.zeros_like(prev_h)
        x = xr_ref[...]
        xc = _conv_causal(x, prev_x[...], cw_ref, 4) + vec_ref[0:1, :]
        prev_x[...] = x[tm - 8:, :]
        xc_ref[...] = xc
        _, ig, a, mult, _, _ = _rg_gates(xc, wa_ref, wx_ref, vec_ref)
        acum, h = _scan_fwd(a, mult * ig * xc)
        h = h + acum * prev_h[7:8, :]
        prev_h[...] = h[tm - 8:, :]
        h_ref[...] = h
        rec = h * _gelu(gr_ref[...])
        rec_ref[...] = rec.astype(MXU_DT)
        rect_ref[...] = rec.T.astype(MXU_DT)

    return pl.pallas_call(
        body, name="rg_fwd", grid=(s // tm,),
        in_specs=[_rows(tm, w, 0), _rows(tm, w, 1), _whole((8, w)), _whole((8, w)),
                  _whole((w, w)), _whole((w, w))],
        out_specs=[_rows(tm, w), _rows(tm, w), _rows(tm, w), _cols(tm, w)],
        out_shape=[jax.ShapeDtypeStruct((s, w), F32), jax.ShapeDtypeStruct((s, w), F32),
                   jax.ShapeDtypeStruct((s, w), MXU_DT), jax.ShapeDtypeStruct((w, s), MXU_DT)],
        scratch_shapes=[pltpu.VMEM((8, w), F32), pltpu.VMEM((8, w), F32)],
        compiler_params=_params(("arbitrary",)),
    )(px, px, cw, vec, wa, wx)


def _dn_scalars(ab, arow, drow):
    lane = _iota(ab.shape, 1)
    g = jnp.where(lane < NV, -jnp.exp(arow) * _softplus(ab + drow), 0.0)
    beta = _sigmoid(ab)
    return lane, g, beta


def _l2n_heads(c, out_ref, off, scale):
    for hh in range(NQ):
        x = c[:, off + hh * HD: off + (hh + 1) * HD]
        r = lax.rsqrt(jnp.sum(x * x, axis=-1, keepdims=True) + L2_EPS)
        out_ref[:, hh * HD:(hh + 1) * HD] = x * (r * scale)


def _dn_prep(pqkv, pab, cw, arow, drow):
    s = pqkv.shape[0]
    tm = 128
    wq = NQ * HD

    def body(x_ref, ab_ref, cw_ref, a_ref, d_ref, q_ref, k_ref, v_ref, rt_ref, cp_ref, prev_x):
        @pl.when(pl.program_id(0) == 0)
        def _():
            prev_x[...] = jnp.zeros_like(prev_x)
        x = x_ref[...]
        cp = _conv_causal(x, prev_x[...], cw_ref, 4)
        prev_x[...] = x[tm - 8:, :]
        cp_ref[...] = cp
        c = cp * _sigmoid(cp)
        _l2n_heads(c, q_ref, 0, Q_SCALE)
        _l2n_heads(c, k_ref, wq, 1.0)
        v_ref[...] = c[:, 2 * wq:]
        lane, g, beta = _dn_scalars(ab_ref[...], a_ref[...], d_ref[...])
        gc = _chunk_cumsum(g, 0)
        gl = gc + _chunk_cumsum(g, 0, rev=True) - g
        pack = jnp.where(lane < NV, gc, jnp.where(lane < 2 * NV, beta,
                         jnp.where(lane < 3 * NV, pltpu.roll(gl, 2 * NV, 1), 0.0)))
        rt_ref[...] = pack.T[0:3 * NV, :]

    return pl.pallas_call(
        body, name="dn_prep", grid=(s // tm,),
        in_specs=[_rows(tm, 4 * wq), _rows(tm, 128), _whole((8, 4 * wq)), _whole((1, 128)), _whole((1, 128))],
        out_specs=[_rows(tm, wq), _rows(tm, wq), _rows(tm, 2 * wq),
                   pl.BlockSpec((3 * NV, tm), lambda i: (0, i)), _rows(tm, 4 * wq)],
        out_shape=[jax.ShapeDtypeStruct((s, wq), F32), jax.ShapeDtypeStruct((s, wq), F32),
                   jax.ShapeDtypeStruct((s, 2 * wq), F32), jax.ShapeDtypeStruct((3 * NV, s), F32),
                   jax.ShapeDtypeStruct((s, 4 * wq), F32)],
        scratch_shapes=[pltpu.VMEM((8, 4 * wq), F32)],
        compiler_params=_params(("arbitrary",)),
    )(pqkv, pab, cw, arow, drow)


def _pair_masks():
    i = _iota((2 * CH, 2 * CH), 0)
    j = _iota((2 * CH, 2 * CH), 1)
    same = (i >> 6) == (j >> 6)
    return same & (i >= j), same & (i > j)


def _head_cols(rt_ref, h):
    shp = (2 * CH, 2 * CH)
    g_r = jnp.broadcast_to(rt_ref[pl.ds(h, 1), :], shp)
    b_r = jnp.broadcast_to(rt_ref[pl.ds(NV + h, 1), :], shp)
    l_r = jnp.broadcast_to(rt_ref[pl.ds(2 * NV + h, 1), :], shp)
    return g_r, g_r.T, b_r.T, l_r, l_r.T


def _inv_unit_lower(a):
    return _inv_unit_lower_many([a])[0]


def _inv_unit_lower_many(a_list):
    n = a_list[0].shape[0]
    eye = (_iota((n, n), 0) == _iota((n, n), 1)).astype(F32)
    bs = [-a for a in a_list]
    xs = [eye + b for b in bs]
    for _ in range(5):
        bs = [_dot(b, b) for b in bs]
        xs = [x + _dot(x, b) for x, b in zip(xs, bs)]
    rs = [(eye - x) - _dot3(a, x) for a, x in zip(a_list, xs)]
    return [x + _dot(x, r) for x, r in zip(xs, rs)]


def _gam_rows(l_r):
    lrow = l_r[0:1, :]
    lane = _iota(lrow.shape, 1)
    other = pltpu.roll(lrow, CH, 1)
    return jnp.exp(jnp.where(lane < CH, lrow, other)), jnp.exp(jnp.where(lane >= CH, lrow, other))


def _dn_intra(q, k, v, rt):
    s = q.shape[0]
    nb = s // (2 * CH)
    qps = 8
    blk = pl.BlockSpec((2 * CH, qps * HD), lambda i, h: (i, h))
    blk2 = pl.BlockSpec((2 * CH, 2 * qps * HD), lambda i, h: (i, h))

    def body(q_ref, k_ref, v_ref, rt_ref, u_ref, w_ref, qd_ref, kd_ref, p_ref, ti_ref, gam_ref):
        hstep = pl.program_id(1)
        mc, ms = _pair_masks()
        gam_ref[...] = jnp.zeros_like(gam_ref)
        heads = []
        for qh in range(qps):
            qq = q_ref[:, qh * HD:(qh + 1) * HD]
            kk_ = k_ref[:, qh * HD:(qh + 1) * HD]
            kk = _dot_nt(kk_, kk_)
            qk = _dot_nt(qq, kk_)
            for j in range(2):
                idx = 2 * qh + j
                cs = slice(idx * HD, (idx + 1) * HD)
                g_r, g_c, b_c, l_r, l_c = _head_cols(rt_ref, 2 * qps * hstep + idx)
                dec = jnp.where(mc, jnp.exp(jnp.where(mc, g_c - g_r, 0.0)), 0.0)
                eg = jnp.exp(g_c)
                p_ref[:, cs] = jnp.where(mc, qk * dec, 0.0)
                qd_ref[:, cs] = eg * qq
                kd_ref[:, cs] = jnp.exp(l_c - g_c) * kk_
                ga, gb = _gam_rows(l_r)
                gam_ref[0, qh, 2 * j:2 * j + 1, :] = ga
                gam_ref[0, qh, 2 * j + 1:2 * j + 2, :] = gb
                rhs = jnp.concatenate([b_c * v_ref[:, cs], b_c * eg * kk_], axis=1)
                heads.append((cs, jnp.where(ms, b_c * kk * dec, 0.0), rhs))
        tinvs = _inv_unit_lower_many([a for _, a, _ in heads])
        uws = [_dot3(t, rhs) for t, (_, _, rhs) in zip(tinvs, heads)]
        for t, uw, (cs, _, _) in zip(tinvs, uws, heads):
            ti_ref[:, cs] = t
            u_ref[:, cs] = uw[:, :HD]
            w_ref[:, cs] = uw[:, HD:]

    big = jax.ShapeDtypeStruct((s, NV * HD), F32)
    return pl.pallas_call(
        body, name="dn_intra", grid=(nb, NQ // qps),
        in_specs=[blk, blk, blk2, pl.BlockSpec((3 * NV, 2 * CH), lambda i, h: (0, i))],
        out_specs=[blk2] * 6 + [pl.BlockSpec((1, qps, 8, 128), lambda i, h: (i, h, 0, 0))],
        out_shape=[big] * 6 + [jax.ShapeDtypeStruct((nb, NQ, 8, 128), F32)],
        compiler_params=_params(("parallel", "parallel")),
    )(q, k, v, rt)


def _dn_seq(u, w, qd, kd, p, gam):
    s = u.shape[0]
    nb = s // (2 * CH)
    wide = pl.BlockSpec((2 * CH, NV * HD), lambda i: (i, 0))

    def body(u_ref, w_ref, qd_ref, kd_ref, p_ref, gam_ref, o_ref, vn_ref, ss_ref, st):
        @pl.when(pl.program_id(0) == 0)
        def _():
            st[...] = jnp.zeros_like(st)
        ra, rb = slice(0, CH), slice(CH, 2 * CH)
        for g0 in range(0, NV, SEQ_GROUP):
            hs = list(range(g0, g0 + SEQ_GROUP))
            cs = [slice(h * HD, (h + 1) * HD) for h in hs]
            ga = [gam_ref[0, h // 2, 2 * (h % 2):2 * (h % 2) + 1, :] for h in hs]
            gb = [gam_ref[0, h // 2, 2 * (h % 2) + 1:2 * (h % 2) + 2, :] for h in hs]
            s0 = [st[h] for h in hs]
            vna = [u_ref[ra, c] - _dot(w_ref[ra, c], s) for c, s in zip(cs, s0)]
            s1 = [g * s + _dot_tn(kd_ref[ra, c], v) for g, s, c, v in zip(ga, s0, cs, vna)]
            vnb = [u_ref[rb, c] - _dot(w_ref[rb, c], s) for c, s in zip(cs, s1)]
            s2 = [g * s + _dot_tn(kd_ref[rb, c], v) for g, s, c, v in zip(gb, s1, cs, vnb)]
            for h, s in zip(hs, s2):
                st[h] = s
            oa = [_dot(qd_ref[ra, c], s) for c, s in zip(cs, s0)]
            ob = [_dot(qd_ref[rb, c], s) for c, s in zip(cs, s1)]
            for i_, h in enumerate(hs):
                vn = jnp.concatenate([vna[i_], vnb[i_]], axis=0)
                o_ref[:, cs[i_]] = jnp.concatenate([oa[i_], ob[i_]], axis=0) + _dot(p_ref[:, cs[i_]], vn)
                vn_ref[:, cs[i_]] = vn
                ss_ref[h, 0:HD, :] = s0[i_]
                ss_ref[h, HD:2 * HD, :] = s1[i_]

    big = jax.ShapeDtypeStruct((s, NV * HD), F32)
    return pl.pallas_call(
        body, name="dn_seq", grid=(nb,),
        in_specs=[wide] * 5 + [pl.BlockSpec((1, NQ, 8, 128), lambda i: (i, 0, 0, 0))],
        out_specs=[wide, wide, pl.BlockSpec((NV, 2 * HD, HD), lambda i: (0, i, 0))],
        out_shape=[big, big, jax.ShapeDtypeStruct((NV, 2 * s, HD), F32)],
        scratch_shapes=[pltpu.VMEM((NV, HD, HD), F32)],
        compiler_params=_params(("arbitrary",)),
    )(u, w, qd, kd, p, gam)


def _dn_post(o, pz, nw):
    s = o.shape[0]
    tm = _pick(s, (256, 128))

    def body(o_ref, z_ref, nw_ref, y_ref, yt_ref):
        for h in range(NV):
            cs = slice(h * HD, (h + 1) * HD)
            oh = o_ref[:, cs]
            z = z_ref[:, cs]
            rs = lax.rsqrt(jnp.mean(oh * oh, axis=-1, keepdims=True) + RMS_EPS)
            y = oh * rs * nw_ref[...] * (z * _sigmoid(z))
            y_ref[:, cs] = y.astype(MXU_DT)
            yt_ref[cs, :] = y.T.astype(MXU_DT)

    return pl.pallas_call(
        body, name="dn_post", grid=(s // tm,),
        in_specs=[_rows(tm, NV * HD), _rows(tm, NV * HD), _whole((1, HD))],
        out_specs=[_rows(tm, NV * HD), _cols(tm, NV * HD)],
        out_shape=[jax.ShapeDtypeStruct((s, NV * HD), MXU_DT), jax.ShapeDtypeStruct((NV * HD, s), MXU_DT)],
        compiler_params=_params(("parallel",)),
    )(o, pz, nw)


def _merge(pg, ya, yb):
    s = ya.shape[0]
    tm = _pick(s, (512, 256, 128))

    def body(ga_ref, gb_ref, ya_ref, yb_ref, o_ref, ot_ref):
        m = _sigmoid(ga_ref[...]) * ya_ref[...] + _sigmoid(gb_ref[...]) * yb_ref[...]
        o_ref[...] = m.astype(MXU_DT)
        ot_ref[...] = m.T.astype(MXU_DT)

    return pl.pallas_call(
        body, name="merge", grid=(s // tm,),
        in_specs=[_rows(tm, D, 0), _rows(tm, D, 1), _rows(tm, D), _rows(tm, D)],
        out_specs=[_rows(tm, D), _cols(tm, D)],
        out_shape=[jax.ShapeDtypeStruct((s, D), MXU_DT), jax.ShapeDtypeStruct((D, s), MXU_DT)],
        compiler_params=_params(("parallel",)),
    )(pg, pg, ya, yb)


def _ln1(x, mix, adam, lng, lnb):
    s = x.shape[0]
    tm = _pick(s, (512, 256, 128))

    def body(x_ref, m_ref, ada_ref, g_ref, b_ref, r_ref, x2_ref, h2_ref, h2t_ref):
        r = ALPHA * x_ref[...] + (1.0 + ada_ref[2:3, :]) * m_ref[...]
        xhat, _ = _ln_stats(r)
        x2 = xhat * g_ref[...] + b_ref[...]
        r_ref[...] = r
        x2_ref[...] = x2
        h2 = x2 * (1.0 + ada_ref[4:5, :]) + ada_ref[3:4, :]
        h2_ref[...] = h2.astype(MXU_DT)
        h2t_ref[...] = h2.T.astype(MXU_DT)

    return pl.pallas_call(
        body, name="ln1", grid=(s // tm,),
        in_specs=[_rows(tm, D), _rows(tm, D), _whole((8, D)), _whole((1, D)), _whole((1, D))],
        out_specs=[_rows(tm, D)] * 3 + [_cols(tm, D)],
        out_shape=[jax.ShapeDtypeStruct((s, D), F32), jax.ShapeDtypeStruct((s, D), F32),
                   jax.ShapeDtypeStruct((s, D), MXU_DT), jax.ShapeDtypeStruct((D, s), MXU_DT)],
        compiler_params=_params(("parallel",)),
    )(x, mix, adam, lng, lnb)


def _ffn_act(gu, cw, cb):
    s = gu.shape[0]
    tm = _pick(s, (256, 128))
    w = D_FF

    def body(g_ref, u_ref, cw_ref, cb_ref, o_ref, ot_ref, gc_ref, prev):
        @pl.when(pl.program_id(0) == 0)
        def _():
            prev[...] = jnp.zeros_like(prev)
        g = g_ref[...]
        gc = _conv_causal(g, prev[...], cw_ref, 3) + cb_ref[...]
        prev[...] = g[tm - 8:, :]
        gc_ref[...] = gc
        act = _gelu(gc) * u_ref[...]
        o_ref[...] = act.astype(MXU_DT)
        ot_ref[...] = act.T.astype(MXU_DT)

    return pl.pallas_call(
        body, name="ffn_act", grid=(s // tm,),
        in_specs=[_rows(tm, w, 0), _rows(tm, w, 1), _whole((8, w)), _whole((1, w))],
        out_specs=[_rows(tm, w), _cols(tm, w), _rows(tm, w)],
        out_shape=[jax.ShapeDtypeStruct((s, w), MXU_DT), jax.ShapeDtypeStruct((w, s), MXU_DT),
                   jax.ShapeDtypeStruct((s, w), F32)],
        scratch_shapes=[pltpu.VMEM((8, w), F32)],
        compiler_params=_params(("arbitrary",)),
    )(gu, gu, cw, cb)


def _ln2_loss(x2, ff, tgt, adam, lng, lnb):
    s = x2.shape[0]
    tm = _pick(s, (512, 256, 128))

    def body(x_ref, f_ref, t_ref, ada_ref, g_ref, b_ref, dff_ref, dx_ref, red_ref):
        @pl.when(pl.program_id(0) == 0)
        def _():
            red_ref[...] = jnp.zeros_like(red_ref)
        ff_ = f_ref[...]
        r = ALPHA * x_ref[...] + (1.0 + ada_ref[5:6, :]) * ff_
        xhat, rstd = _ln_stats(r)
        err = xhat * g_ref[...] + b_ref[...] - t_ref[...]
        dy = err * (1.0 / D)
        dr = _ln_bwd(dy, xhat, rstd, g_ref[...])
        dff_ref[...] = ((1.0 + ada_ref[5:6, :]) * dr).astype(MXU_DT)
        dx_ref[...] = ALPHA * dr
        red_ref[0:1, :] += _rsum(dy * xhat)
        red_ref[1:2, :] += _rsum(dy)
        red_ref[2:3, :] += _rsum(dr * ff_)
        red_ref[3:4, :] += jnp.sum(_rsum(err * err), axis=1, keepdims=True) * (0.5 / D)

    return pl.pallas_call(
        body, name="ln2_loss", grid=(s // tm,),
        in_specs=[_rows(tm, D)] * 3 + [_whole((8, D)), _whole((1, D)), _whole((1, D))],
        out_specs=[_rows(tm, D), _rows(tm, D), _whole((8, D))],
        out_shape=[jax.ShapeDtypeStruct((s, D), MXU_DT), jax.ShapeDtypeStruct((s, D), F32),
                   jax.ShapeDtypeStruct((8, D), F32)],
        compiler_params=_params(("arbitrary",)),
    )(x2, ff, tgt, adam, lng, lnb)


def _ffn_bwd(dact, gu, gc, cw):
    s = dact.shape[0]
    tm = _pick(s, (256, 128))
    nt = s // tm
    w = D_FF

    def body(da_ref, g_ref, u_ref, gc_ref, cw_ref, o_ref, gcw_ref, gcb_ref, nxt):
        @pl.when(pl.program_id(0) == 0)
        def _():
            nxt[...] = jnp.zeros_like(nxt)
            gcw_ref[...] = jnp.zeros_like(gcw_ref)
            gcb_ref[...] = jnp.zeros_like(gcb_ref)
        gel, dgel = _gelu_and_grad(gc_ref[...])
        da = da_ref[...]
        dgc = da * u_ref[...] * dgel
        o_ref[:, w:] = (da * gel).astype(MXU_DT)
        o_ref[:, :w] = _conv_causal_bwd(dgc, nxt[...], cw_ref, 3, g_ref[...], gcw_ref).astype(MXU_DT)
        nxt[...] = dgc[:8, :]
        gcb_ref[...] += _rsum(dgc)

    return pl.pallas_call(
        body, name="ffn_bwd", grid=(nt,),
        in_specs=[_rows(tm, w, 0, nt), _rows(tm, w, 0, nt), _rows(tm, w, 1, nt), _rows(tm, w, 0, nt),
                  _whole((8, w))],
        out_specs=[_rows(tm, 2 * w, 0, nt), _whole((8, w)), _whole((1, w))],
        out_shape=[jax.ShapeDtypeStruct((s, 2 * w), MXU_DT), jax.ShapeDtypeStruct((8, w), F32),
                   jax.ShapeDtypeStruct((1, w), F32)],
        scratch_shapes=[pltpu.VMEM((8, w), F32)],
        compiler_params=_params(("arbitrary",)),
    )(dact, gu, gu, gc, cw)


def _ln1_bwd(dh2, dx2a, x2, r1, mix, adam, lng):
    s = dh2.shape[0]
    tm = _pick(s, (512, 256, 128))

    def body(dh_ref, dxa_ref, x2_ref, r_ref, m_ref, ada_ref, g_ref, dm_ref, dx_ref, red_ref):
        @pl.when(pl.program_id(0) == 0)
        def _():
            red_ref[...] = jnp.zeros_like(red_ref)
        dh = dh_ref[...]
        dx2 = dxa_ref[...] + dh * (1.0 + ada_ref[4:5, :])
        xhat, rstd = _ln_stats(r_ref[...])
        dr = _ln_bwd(dx2, xhat, rstd, g_ref[...])
        dm_ref[...] = ((1.0 + ada_ref[2:3, :]) * dr).astype(MXU_DT)
        dx_ref[...] = ALPHA * dr
        red_ref[0:1, :] += _rsum(dh * x2_ref[...])
        red_ref[1:2, :] += _rsum(dh)
        red_ref[2:3, :] += _rsum(dx2 * xhat)
        red_ref[3:4, :] += _rsum(dx2)
        red_ref[4:5, :] += _rsum(dr * m_ref[...])

    return pl.pallas_call(
        body, name="ln1_bwd", grid=(s // tm,),
        in_specs=[_rows(tm, D)] * 5 + [_whole((8, D)), _whole((1, D))],
        out_specs=[_rows(tm, D), _rows(tm, D), _whole((8, D))],
        out_shape=[jax.ShapeDtypeStruct((s, D), MXU_DT), jax.ShapeDtypeStruct((s, D), F32),
                   jax.ShapeDtypeStruct((8, D), F32)],
        compiler_params=_params(("arbitrary",)),
    )(dh2, dx2a, x2, r1, mix, adam, lng)


def _merge_bwd(dmg, pg, ya, yb):
    s = dmg.shape[0]
    tm = _pick(s, (512, 256, 128))

    def body(d_ref, ga_ref, gb_ref, ya_ref, yb_ref, dya_ref, dyb_ref, dpg_ref):
        d = d_ref[...]
        sa = _sigmoid(ga_ref[...])
        sb = _sigmoid(gb_ref[...])
        dya_ref[...] = (d * sa).astype(MXU_DT)
        dyb_ref[...] = (d * sb).astype(MXU_DT)
        dpg_ref[:, :D] = (d * ya_ref[...] * sa * (1.0 - sa)).astype(MXU_DT)
        dpg_ref[:, D:] = (d * yb_ref[...] * sb * (1.0 - sb)).astype(MXU_DT)

    return pl.pallas_call(
        body, name="merge_bwd", grid=(s // tm,),
        in_specs=[_rows(tm, D), _rows(tm, D, 0), _rows(tm, D, 1), _rows(tm, D), _rows(tm, D)],
        out_specs=[_rows(tm, D), _rows(tm, D), _rows(tm, 2 * D)],
        out_shape=[jax.ShapeDtypeStruct((s, D), MXU_DT), jax.ShapeDtypeStruct((s, D), MXU_DT),
                   jax.ShapeDtypeStruct((s, 2 * D), MXU_DT)],
        compiler_params=_params(("parallel",)),
    )(dmg, pg, pg, ya, yb)


def _dn_post_bwd(ddn, o, pz, nw):
    s = o.shape[0]
    tm = _pick(s, (256, 128))

    def body(d_ref, o_ref, z_ref, nw_ref, do_ref, dz_ref, gnw_ref):
        @pl.when(pl.program_id(0) == 0)
        def _():
            gnw_ref[...] = jnp.zeros_like(gnw_ref)
        acc = jnp.zeros((1, HD), F32)
        for h in range(NV):
            cs = slice(h * HD, (h + 1) * HD)
            oh = o_ref[:, cs]
            z = z_ref[:, cs]
            d = d_ref[:, cs]
            sg = _sigmoid(z)
            rs = lax.rsqrt(jnp.mean(oh * oh, axis=-1, keepdims=True) + RMS_EPS)
            n = oh * rs
            dz_ref[:, cs] = (d * n * nw_ref[...] * sg * (1.0 + z * (1.0 - sg))).astype(MXU_DT)
            dn_ = d * (z * sg)
            acc = acc + _rsum(dn_ * n)
            dnn = dn_ * nw_ref[...]
            do_ref[:, cs] = rs * (dnn - n * jnp.mean(dnn * n, axis=-1, keepdims=True))
        gnw_ref[...] += acc

    return pl.pallas_call(
        body, name="dn_post_bwd", grid=(s // tm,),
        in_specs=[_rows(tm, NV * HD)] * 3 + [_whole((1, HD))],
        out_specs=[_rows(tm, NV * HD), _rows(tm, NV * HD), _whole((1, HD))],
        out_shape=[jax.ShapeDtypeStruct((s, NV * HD), F32), jax.ShapeDtypeStruct((s, NV * HD), MXU_DT),
                   jax.ShapeDtypeStruct((1, HD), F32)],
        compiler_params=_params(("arbitrary",)),
    )(ddn, o, pz, nw)


def _dn_seq_bwd(do, qd, kd, p, w, vn, ssave, gam):
    s = do.shape[0]
    nb = s // (2 * CH)
    wide = pl.BlockSpec((2 * CH, NV * HD), lambda i: (nb - 1 - i, 0))
    gspec = pl.BlockSpec((1, NQ, 8, 128), lambda i: (nb - 1 - i, 0, 0, 0))

    def body(do_ref, qd_ref, kd_ref, p_ref, w_ref, vn_ref, ss_ref, gam_ref, dvn_ref, dkd_ref, dgam_ref, dst):
        @pl.when(pl.program_id(0) == 0)
        def _():
            dst[...] = jnp.zeros_like(dst)
        dgam_ref[...] = jnp.zeros_like(dgam_ref)
        ra, rb = slice(0, CH), slice(CH, 2 * CH)
        tot = lambda t: jnp.sum(jnp.sum(t, axis=1, keepdims=True), axis=0, keepdims=True)
        for g0 in range(0, NV, SEQ_GROUP):
            hs = list(range(g0, g0 + SEQ_GROUP))
            cs = [slice(h * HD, (h + 1) * HD) for h in hs]
            ga = [gam_ref[0, h // 2, 2 * (h % 2):2 * (h % 2) + 1, :] for h in hs]
            gb = [gam_ref[0, h // 2, 2 * (h % 2) + 1:2 * (h % 2) + 2, :] for h in hs]
            ds2 = [dst[h] for h in hs]
            pdo = [_dot_tn(p_ref[:, c], do_ref[:, c]) for c in cs]
            qdo_b = [_dot_tn(qd_ref[rb, c], do_ref[rb, c]) for c in cs]
            qdo_a = [_dot_tn(qd_ref[ra, c], do_ref[ra, c]) for c in cs]
            dvb = [p_[rb] + _dot(kd_ref[rb, c], d_) for p_, c, d_ in zip(pdo, cs, ds2)]
            ds1 = [g * d_ + q_ - _dot_tn(w_ref[rb, c], v_)
                   for g, d_, q_, c, v_ in zip(gb, ds2, qdo_b, cs, dvb)]
            dva = [p_[ra] + _dot(kd_ref[ra, c], d_) for p_, c, d_ in zip(pdo, cs, ds1)]
            ds0 = [g * d_ + q_ - _dot_tn(w_ref[ra, c], v_)
                   for g, d_, q_, c, v_ in zip(ga, ds1, qdo_a, cs, dva)]
            for h, d_ in zip(hs, ds0):
                dst[h] = d_
            for i_, h in enumerate(hs):
                c = cs[i_]
                row = 2 * (h % 2)
                dkd_ref[rb, c] = _dot_nt(vn_ref[rb, c], ds2[i_])
                dkd_ref[ra, c] = _dot_nt(vn_ref[ra, c], ds1[i_])
                dvn_ref[ra, c] = dva[i_]
                dvn_ref[rb, c] = dvb[i_]
                dgam_ref[0, h // 2, row:row + 1, :] = jnp.broadcast_to(tot(ds1[i_] * ss_ref[h, 0:HD, :]), (1, 128))
                dgam_ref[0, h // 2, row + 1:row + 2, :] = jnp.broadcast_to(
                    tot(ds2[i_] * ss_ref[h, HD:2 * HD, :]), (1, 128))

    big = jax.ShapeDtypeStruct((s, NV * HD), F32)
    return pl.pallas_call(
        body, name="dn_seq_bwd", grid=(nb,),
        in_specs=[wide] * 6 + [pl.BlockSpec((NV, 2 * HD, HD), lambda i: (0, nb - 1 - i, 0)), gspec],
        out_specs=[wide, wide, gspec],
        out_shape=[big, big, jax.ShapeDtypeStruct((nb, NQ, 8, 128), F32)],
        scratch_shapes=[pltpu.VMEM((NV, HD, HD), F32)],
        compiler_params=_params(("arbitrary",)),
    )(do, qd, kd, p, w, vn, ssave, gam)


def _dn_intra_bwd(q, k, v, rt, do, dvn, dkd, gam, dgam, ssave, tinv, u, w, vn):
    s = q.shape[0]
    nb = s // (2 * CH)
    qps = 4
    nh = 2 * qps
    blk = pl.BlockSpec((2 * CH, qps * HD), lambda i, h: (i, h))
    blk2 = pl.BlockSpec((2 * CH, nh * HD), lambda i, h: (i, h))
    gspec = pl.BlockSpec((1, qps, 8, 128), lambda i, h: (i, h, 0, 0))
    rspec = pl.BlockSpec((3 * NV, 2 * CH), lambda i, h: (0, i))

    def body(q_ref, k_ref, v_ref, rt_ref, do_ref, dvn_ref, dkd_ref, gam_ref, dgam_ref, ss_ref,
             ti_ref, u_ref, w_ref, vn_ref, dq_ref, dk_ref, dv_ref, drt_ref, acc):
        hstep = pl.program_id(1)

        @pl.when(hstep == 0)
        def _():
            acc[...] = jnp.zeros_like(acc)
        mc, ms = _pair_masks()
        ra, rb = slice(0, CH), slice(CH, 2 * CH)
        lane = _iota((1, 2 * CH), 1)
        hs = list(range(nh))
        qh = [h // 2 for h in hs]
        cs = [slice(h * HD, (h + 1) * HD) for h in hs]
        qq_ = [q_ref[:, t * HD:(t + 1) * HD] for t in range(qps)]
        kk_ = [k_ref[:, t * HD:(t + 1) * HD] for t in range(qps)]
        kk = [_dot_nt(k_, k_) for k_ in kk_]
        qk = [_dot_nt(q_, k_) for q_, k_ in zip(qq_, kk_)]
        cols = [_head_cols(rt_ref, nh * hstep + h) for h in hs]
        b_c = [c_[2] for c_ in cols]
        dec = [jnp.where(mc, jnp.exp(jnp.where(mc, c_[1] - c_[0], 0.0)), 0.0) for c_ in cols]
        eg = [jnp.exp(c_[1]) for c_ in cols]
        egl = [jnp.exp(c_[4] - c_[1]) for c_ in cols]
        dob = [do_ref[:, c] for c in cs]
        dvb = [dvn_ref[:, c] for c in cs]
        dqd = [jnp.concatenate([_dot_nt(d_[ra], ss_ref[h, 0:HD, :]), _dot_nt(d_[rb], ss_ref[h, HD:2 * HD, :])], axis=0)
               for h, d_ in zip(hs, dob)]
        dw = [-jnp.concatenate([_dot_nt(d_[ra], ss_ref[h, 0:HD, :]), _dot_nt(d_[rb], ss_ref[h, HD:2 * HD, :])], axis=0)
              for h, d_ in zip(hs, dvb)]
        dp = [jnp.where(mc, _dot_nt(d_, vn_ref[:, c]), 0.0) for d_, c in zip(dob, cs)]
        dbuw = [_dot3_tn(ti_ref[:, c], jnp.concatenate([d_, w_], axis=1)) for c, d_, w_ in zip(cs, dvb, dw)]
        dbu = [t[:, :HD] for t in dbuw]
        dbw = [t[:, HD:] for t in dbuw]
        da = [jnp.where(ms, -(_dot_nt(bu, u_ref[:, c]) + _dot_nt(bw, w_ref[:, c])), 0.0)
              for bu, bw, c in zip(dbu, dbw, cs)]
        dm = [a_ * d_ for a_, d_ in zip(da, dec)]
        dn_ = [p_ * d_ for p_, d_ in zip(dp, dec)]
        dbk = [_dot(m_, kk_[t]) for m_, t in zip(dm, qh)]
        dqs = [_dot(n_, kk_[t]) + e_ * q_ for n_, t, e_, q_ in zip(dn_, qh, eg, dqd)]
        dks = [_dot_tn(m_, b_ * kk_[t]) + _dot_tn(n_, qq_[t]) + el * dkd_ref[:, c] + b_ * (e_ * bw + bk)
               for m_, b_, t, n_, el, c, e_, bw, bk in zip(dm, b_c, qh, dn_, egl, cs, eg, dbw, dbk)]
        for t in range(qps):
            dq_ref[:, t * HD:(t + 1) * HD] = dqs[2 * t] + dqs[2 * t + 1]
            dk_ref[:, t * HD:(t + 1) * HD] = dks[2 * t] + dks[2 * t + 1]
        for h in hs:
            c, t, j = cs[h], qh[h], h % 2
            dv_ref[:, c] = b_c[h] * dbu[h]
            e = da[h] * (b_c[h] * kk[t] * dec[h]) + dp[h] * (qk[t] * dec[h])
            x = dkd_ref[:, c] * (egl[h] * kk_[t])
            egk = eg[h] * kk_[t]
            z = e + dqd[h] * (eg[h] * qq_[t]) - x + dbw[h] * (b_c[h] * egk)
            zb = dbw[h] * egk + dbu[h] * v_ref[:, c] + dbk[h] * kk_[t]
            sa = jnp.sum(jnp.sum(x[ra], axis=1, keepdims=True), axis=0, keepdims=True)
            sb = jnp.sum(jnp.sum(x[rb], axis=1, keepdims=True), axis=0, keepdims=True)
            la = sa + dgam_ref[0, t, 2 * j:2 * j + 1, :] * gam_ref[0, t, 2 * j:2 * j + 1, :]
            lb = sb + dgam_ref[0, t, 2 * j + 1:2 * j + 2, :] * gam_ref[0, t, 2 * j + 1:2 * j + 2, :]
            hg = nh * hstep + h
            acc[pl.ds(hg, 1), :] = _rsum(z.T - e)
            acc[pl.ds(NV + hg, 1), :] = _rsum(zb.T)
            acc[pl.ds(2 * NV + hg, 1), :] = jnp.where(lane < CH, la, lb)

        @pl.when(hstep == NQ // qps - 1)
        def _():
            drt_ref[...] = acc[...]

    return pl.pallas_call(
        body, name="dn_intra_bwd", grid=(nb, NQ // qps),
        in_specs=[blk, blk, blk2, rspec, blk2, blk2, blk2, gspec, gspec,
                  pl.BlockSpec((nh, 2 * HD, HD), lambda i, h: (h, i, 0)), blk2, blk2, blk2, blk2],
        out_specs=[blk, blk, blk2, rspec],
        out_shape=[jax.ShapeDtypeStruct((s, NQ * HD), F32), jax.ShapeDtypeStruct((s, NQ * HD), F32),
                   jax.ShapeDtypeStruct((s, NV * HD), F32), jax.ShapeDtypeStruct((3 * NV, s), F32)],
        scratch_shapes=[pltpu.VMEM((3 * NV, 2 * CH), F32)],
        compiler_params=_params(("parallel", "arbitrary")),
    )(q, k, v, rt, do, dvn, dkd, gam, dgam, ssave, tinv, u, w, vn)


def _l2n_heads_bwd(c, d_ref, dc_ref, off, scale):
    for hh in range(NQ):
        cs = slice(off + hh * HD, off + (hh + 1) * HD)
        x = c[:, cs]
        dy = d_ref[:, hh * HD:(hh + 1) * HD]
        r = lax.rsqrt(jnp.sum(x * x, axis=-1, keepdims=True) + L2_EPS)
        dc_ref[:, cs] = (scale * r) * (dy - x * (r * r) * jnp.sum(dy * x, axis=-1, keepdims=True))


def _dn_prep_bwd(dq, dk, dv, drt, pqkv, cpre, pab, cw, acol, dcol):
    s = pqkv.shape[0]
    tm = 128
    nt = s // tm
    wq = NQ * HD

    def body(dq_ref, dk_ref, dv_ref, drt_ref, x_ref, cp_ref, ab_ref, cw_ref, ac_ref,
             dc_ref, dx_ref, dab_ref, gcw_ref, gsc_ref, dcs, nxt):
        @pl.when(pl.program_id(0) == 0)
        def _():
            nxt[...] = jnp.zeros_like(nxt)
            gcw_ref[...] = jnp.zeros_like(gcw_ref)
            gsc_ref[...] = jnp.zeros_like(gsc_ref)
        cp = cp_ref[...]
        sg = _sigmoid(cp)
        c = cp * sg
        _l2n_heads_bwd(c, dq_ref, dcs, 0, Q_SCALE)
        _l2n_heads_bwd(c, dk_ref, dcs, wq, 1.0)
        dcs[:, 2 * wq:] = dv_ref[...]
        dcp = dcs[...] * (sg * (1.0 + cp * (1.0 - sg)))
        dx_ref[...] = _conv_causal_bwd(dcp, nxt[...], cw_ref, 4, x_ref[...], gcw_ref).astype(MXU_DT)
        nxt[...] = dcp[:8, :]
        lane = _iota((NV, tm), 1)
        dgt = drt_ref[0:NV, :] + jnp.where((lane & (CH - 1)) == CH - 1, drt_ref[2 * NV:3 * NV, :], 0.0)
        dg = _chunk_cumsum(dgt, 1, rev=True)
        abt = ab_ref[...].T
        zt = abt[0:NV, :] + dc_ref[...]
        gt = -jnp.exp(ac_ref[...]) * _softplus(zt)
        dat = dg * (-jnp.exp(ac_ref[...])) * _sigmoid(zt)
        bt = _sigmoid(abt[NV:2 * NV, :])
        dbt = drt_ref[NV:2 * NV, :] * bt * (1.0 - bt)
        full = jnp.concatenate([dat, dbt, jnp.zeros((128 - 2 * NV, tm), F32)], axis=0)
        dab_ref[...] = full.T.astype(MXU_DT)
        l2 = _iota((NV, 128), 1)
        gsc_ref[...] += jnp.where(l2 == 0, jnp.sum(dg * gt, axis=1, keepdims=True),
                                  jnp.where(l2 == 1, jnp.sum(dat, axis=1, keepdims=True), 0.0))

    return pl.pallas_call(
        body, name="dn_prep_bwd", grid=(nt,),
        in_specs=[_rows(tm, wq, 0, nt), _rows(tm, wq, 0, nt), _rows(tm, 2 * wq, 0, nt),
                  pl.BlockSpec((3 * NV, tm), lambda i: (0, nt - 1 - i)),
                  _rows(tm, 4 * wq, 0, nt), _rows(tm, 4 * wq, 0, nt), _rows(tm, 128, 0, nt),
                  _whole((8, 4 * wq)), _whole((NV, 1)), _whole((NV, 1))],
        out_specs=[_rows(tm, 4 * wq, 0, nt), _rows(tm, 128, 0, nt), _whole((8, 4 * wq)), _whole((NV, 128))],
        out_shape=[jax.ShapeDtypeStruct((s, 4 * wq), MXU_DT), jax.ShapeDtypeStruct((s, 128), MXU_DT),
                   jax.ShapeDtypeStruct((8, 4 * wq), F32), jax.ShapeDtypeStruct((NV, 128), F32)],
        scratch_shapes=[pltpu.VMEM((tm, 4 * wq), F32), pltpu.VMEM((8, 4 * wq), F32)],
        compiler_params=_params(("arbitrary",)),
    )(dq, dk, dv, drt, pqkv, cpre, pab, cw, acol, dcol)


def _rg_bwd(drec, px, xcs, h, cw, vec, wa, wx, wat, wxt, side=None):
    s = px.shape[0]
    tm = _pick(s, (256, 128))
    nt = s // tm
    w = D_RNN
    n_in = len(side.ins) if side else 0
    n_out = len(side.out_shapes) if side else 0

    def body(*refs):
        (dr_ref, xr_ref, xc_ref, gr_ref, h_ref, hb_ref, cw_ref, vec_ref, wa_ref, wx_ref, wat_ref,
         wxt_ref) = refs[:12]
        s_in = refs[12:12 + n_in]
        o_ref, gwa_ref, gwx_ref, gcw_ref, gvec_ref = refs[12 + n_in:17 + n_in]
        s_out = refs[17 + n_in:17 + n_in + n_out]
        nxt_a, nxt_l, nxt_d = refs[17 + n_in + n_out:20 + n_in + n_out]
        sems = refs[20 + n_in + n_out:]
        i = pl.program_id(0)
        if side:
            @pl.when(i == 0)
            def _():
                side.start(s_in, s_out, sems)

        @pl.when(i == 0)
        def _():
            nxt_a[...] = jnp.zeros_like(nxt_a)
            nxt_l[...] = jnp.zeros_like(nxt_l)
            nxt_d[...] = jnp.zeros_like(nxt_d)
            gwa_ref[...] = jnp.zeros_like(gwa_ref)
            gwx_ref[...] = jnp.zeros_like(gwx_ref)
            gcw_ref[...] = jnp.zeros_like(gcw_ref)
            gvec_ref[...] = jnp.zeros_like(gvec_ref)
        hbefore = jnp.where(i == nt - 1, 0.0, hb_ref[...])
        xc = xc_ref[...]
        r, ig, a, mult, sp, rmult = _rg_gates(xc, wa_ref, wx_ref, vec_ref)
        hh = h_ref[...]
        gel, dgel = _gelu_and_grad(gr_ref[...])
        drec_ = dr_ref[...]
        o_ref[:, w:] = (drec_ * hh * dgel).astype(MXU_DT)
        acum, lam = _scan_rev(_shift_up(a, nxt_a[...], 1), drec_ * gel)
        lam = lam + acum * nxt_l[0:1, :]
        nxt_a[...] = a[:8, :]
        nxt_l[...] = lam[:8, :]
        da = lam * _shift_down(hh, hbefore, 1)
        dxc = lam * mult * ig
        dla = da * a - (lam * ig * xc) * (a * a) * rmult
        dpr = dla * (-RG_C * sp) * r * (1.0 - r)
        dpi = (lam * mult * xc) * ig * (1.0 - ig)
        dprb = dpr.astype(MXU_DT)
        dpib = dpi.astype(MXU_DT)
        dxc = dxc + jnp.dot(dprb, wat_ref[...], preferred_element_type=F32) \
                  + jnp.dot(dpib, wxt_ref[...], preferred_element_type=F32)
        xcb = xc.astype(MXU_DT)
        gwa_ref[...] += _dot_tn(xcb, dprb)
        gwx_ref[...] += _dot_tn(xcb, dpib)
        o_ref[:, :w] = _conv_causal_bwd(dxc, nxt_d[...], cw_ref, 4, xr_ref[...], gcw_ref).astype(MXU_DT)
        nxt_d[...] = dxc[:8, :]
        gvec_ref[0:1, :] += _rsum(dxc)
        gvec_ref[1:2, :] += _rsum(dpr)
        gvec_ref[2:3, :] += _rsum(dpi)
        gvec_ref[3:4, :] += _rsum(dla * (-RG_C * r)) * (-_sigmoid(-vec_ref[3:4, :]))
        if side:
            @pl.when(i == nt - 1)
            def _():
                side.finish(s_in, s_out, sems)

    return pl.pallas_call(
        body, name="rg_bwd", grid=(nt,),
        in_specs=[_rows(tm, w, 0, nt), _rows(tm, w, 0, nt), _rows(tm, w, 0, nt), _rows(tm, w, 1, nt),
                  _rows(tm, w, 0, nt), _before(tm, w, 0, nt), _whole((8, w)), _whole((8, w)),
                  _whole((w, w)), _whole((w, w)), _whole((w, w)), _whole((w, w))] + [_HBM] * n_in,
        out_specs=[_rows(tm, 2 * w, 0, nt), _whole((w, w)), _whole((w, w)), _whole((8, w)), _whole((8, w))]
        + [_HBM] * n_out,
        out_shape=[jax.ShapeDtypeStruct((s, 2 * w), MXU_DT), jax.ShapeDtypeStruct((w, w), F32),
                   jax.ShapeDtypeStruct((w, w), F32), jax.ShapeDtypeStruct((8, w), F32),
                   jax.ShapeDtypeStruct((8, w), F32)] + (list(side.out_shapes) if side else []),
        scratch_shapes=[pltpu.VMEM((8, w), F32)] * 3 + (list(side.sems) if side else []),
        compiler_params=_params(("arbitrary",)),
    )(drec, px, xcs, px, h, h, cw, vec, wa, wx, wat, wxt, *(side.ins if side else []))


def _modulate_bwd(dh1, dxa, x, adam):
    s = x.shape[0]
    tm = _pick(s, (512, 256, 128))

    def body(dh_ref, dxa_ref, x_ref, ada_ref, gx_ref, red_ref):
        @pl.when(pl.program_id(0) == 0)
        def _():
            red_ref[...] = jnp.zeros_like(red_ref)
        dh = dh_ref[...]
        gx_ref[...] = dxa_ref[...] + dh * (1.0 + ada_ref[1:2, :])
        red_ref[0:1, :] += _rsum(dh * x_ref[...])
        red_ref[1:2, :] += _rsum(dh)

    return pl.pallas_call(
        body, name="modulate1_bwd", grid=(s // tm,),
        in_specs=[_rows(tm, D)] * 3 + [_whole((8, D))],
        out_specs=[_rows(tm, D), _whole((8, D))],
        out_shape=[jax.ShapeDtypeStruct((s, D), F32), jax.ShapeDtypeStruct((8, D), F32)],
        compiler_params=_params(("arbitrary",)),
    )(dh1, dxa, x, adam)


def _adamw(parts, w, m, v, name):
    r, c = w.shape
    n_parts = len(parts)
    tm = _row_tile(r, c * 4 * (n_parts + 7))
    c1 = 1.0 - ADAM_B1 ** ADAM_STEP
    c2 = 1.0 - ADAM_B2 ** ADAM_STEP

    def body(*refs):
        g = refs[0][...]
        for p_ref in refs[1:n_parts]:
            g = g + p_ref[...]
        w_ref, m_ref, v_ref, g_out, d_out, m_out, v_out = refs[n_parts:]
        mn = ADAM_B1 * m_ref[...] + (1.0 - ADAM_B1) * g
        vn = ADAM_B2 * v_ref[...] + (1.0 - ADAM_B2) * (g * g)
        g_out[...] = g
        m_out[...] = mn
        v_out[...] = vn
        d_out[...] = -ADAM_LR * ((mn / c1) / (jnp.sqrt(vn / c2) + ADAM_EPS) + ADAM_WD * w_ref[...])

    spec = pl.BlockSpec((tm, c), lambda i: (i, 0))
    return pl.pallas_call(
        body, name=name, grid=(r // tm,),
        in_specs=[spec] * (n_parts + 3), out_specs=[spec] * 4,
        out_shape=[jax.ShapeDtypeStruct((r, c), F32)] * 4, compiler_params=_params(("parallel",)),
    )(*parts, w, m, v)


def _row_tile(r, bytes_per_row):
    for t in (512, 256, 128, 64, 32, 16):
        if r % t == 0 and 2 * t * bytes_per_row <= 20 * 1024 * 1024:
            return t
    return _pick(r, (16, 8))


def _sum_partials(own, recv, name):
    r, c = own.shape
    tm = _row_tile(r, c * (4 + 4 + 3 * 2))

    def body(o_ref, r_ref, out_ref):
        out_ref[...] = ((o_ref[...] + r_ref[0].astype(F32)) + r_ref[1].astype(F32)) + r_ref[2].astype(F32)

    return pl.pallas_call(
        body, name=name, grid=(r // tm,),
        in_specs=[pl.BlockSpec((tm, c), lambda i: (i, 0)), pl.BlockSpec((3, tm, c), lambda i: (0, i, 0))],
        out_specs=pl.BlockSpec((tm, c), lambda i: (i, 0)),
        out_shape=jax.ShapeDtypeStruct((r, c), F32), compiler_params=_params(("parallel",)),
    )(own, recv)


def _sum8(g, name):
    _, r, c = g.shape
    tm = _pick(r, (256, 128, 64, 32, 16, 8))

    def body(g_ref, o_ref):
        acc = g_ref[0]
        for k in range(1, 8):
            acc = acc + g_ref[k]
        o_ref[...] = acc

    return pl.pallas_call(
        body, name=name, grid=(r // tm,),
        in_specs=[pl.BlockSpec((8, tm, c), lambda i: (0, i, 0))],
        out_specs=pl.BlockSpec((tm, c), lambda i: (i, 0)),
        out_shape=jax.ShapeDtypeStruct((r, c), F32), compiler_params=_params(("parallel",)),
    )(g)


def _silu_rows(x):
    def body(x_ref, o_ref):
        xx = x_ref[...]
        o_ref[...] = xx * _sigmoid(xx)

    return pl.pallas_call(body, name="silu_rows", out_shape=jax.ShapeDtypeStruct(x.shape, F32))(x)


def _coords():
    return lax.axis_index("x"), lax.axis_index("y"), lax.axis_index("c")


def _allgather8(v, name):
    r, n = v.shape

    def body(v_ref, out_ref, send_sems, recv_sems):
        x, y, c = _coords()
        me = 4 * x + 2 * y + c
        out_ref[me] = v_ref[...]
        peers = []
        for k in range(1, 8):
            px = 1 - x if k & 4 else x
            py = 1 - y if k & 2 else y
            pc = 1 - c if k & 1 else c
            peers.append((px, py, pc))

        def copy(k, slot, to):
            return pltpu.make_async_remote_copy(
                src_ref=v_ref, dst_ref=out_ref.at[slot], send_sem=send_sems.at[k], recv_sem=recv_sems.at[k],
                device_id=to, device_id_type=MESH)

        sends = [copy(k, me, p) for k, p in enumerate(peers)]
        for cp in sends:
            cp.start()
        for k, (px, py, pc) in enumerate(peers):
            copy(k, 4 * px + 2 * py + pc, (px, py, pc)).wait_recv()
        for cp in sends:
            cp.wait_send()

    return pl.pallas_call(
        body, name=name, out_shape=jax.ShapeDtypeStruct((8, r, n), v.dtype),
        in_specs=[pl.BlockSpec(memory_space=pltpu.VMEM)], out_specs=pl.BlockSpec(memory_space=pltpu.VMEM),
        scratch_shapes=[pltpu.SemaphoreType.DMA((7,)), pltpu.SemaphoreType.DMA((7,))],
        compiler_params=pltpu.CompilerParams(vmem_limit_bytes=VMEM_LIMIT),
    )(v)


def _other_chips(x, y):
    return [(1 - x, y), (x, 1 - y), (1 - x, 1 - y)]


_HBM = pl.BlockSpec(memory_space=pl.ANY)


class _Exchange:
    def __init__(self, ins, out_shapes, sems, start, finish):
        self.ins, self.out_shapes, self.sems, self.start, self.finish = ins, out_shapes, sems, start, finish


def _run_exchange(ex, name):
    n_in, n_out = len(ex.ins), len(ex.out_shapes)

    def body(*refs):
        ins, outs, sems = refs[:n_in], refs[n_in:n_in + n_out], refs[n_in + n_out:]
        ex.start(ins, outs, sems)
        ex.finish(ins, outs, sems)

    return pl.pallas_call(body, name=name, out_shape=list(ex.out_shapes), in_specs=[_HBM] * n_in,
                          out_specs=[_HBM] * n_out, scratch_shapes=list(ex.sems))(*ex.ins)


def _gather_exchange(shards):
    n = len(shards)

    def plan(ins, outs, sems):
        send_sems, recv_sems, local_sems = sems
        x, y, c = _coords()
        s_me = 2 * x + y
        chips = _other_chips(x, y)

        def half(i, slot, hc):
            rh = shards[i].shape[0] // 2
            return outs[i].at[slot, pl.ds(pl.multiple_of(hc * rh, 16), rh), :]

        def copy(i, k, src, dst, to):
            return pltpu.make_async_remote_copy(src_ref=src, dst_ref=dst, send_sem=send_sems.at[6 * i + k],
                                                recv_sem=recv_sems.at[6 * i + k], device_id=to, device_id_type=MESH)

        local = [pltpu.make_async_copy(ins[i], outs[i].at[s_me], local_sems.at[i]) for i in range(n)]
        first = []
        for i in range(n):
            rh = shards[i].shape[0] // 2
            my_half = ins[i].at[pl.ds(pl.multiple_of(c * rh, 16), rh), :]
            first += [copy(i, j, my_half, half(i, s_me, c), (px, py, c)) for j, (px, py) in enumerate(chips)]
        return (x, y, c), chips, half, copy, local, first

    def start(ins, outs, sems):
        _, _, _, _, local, first = plan(ins, outs, sems)
        for cp in local + first:
            cp.start()

    def finish(ins, outs, sems):
        (x, y, c), chips, half, copy, local, first = plan(ins, outs, sems)
        sibling = (x, y, 1 - c)
        passed = []
        for i in range(n):
            for j, (px, py) in enumerate(chips):
                land = half(i, 2 * px + py, c)
                copy(i, j, land, land, (px, py, c)).wait_recv()
                fw = copy(i, 3 + j, land, land, sibling)
                fw.start()
                passed.append(fw)
        for i in range(n):
            for j, (px, py) in enumerate(chips):
                land = half(i, 2 * px + py, 1 - c)
                copy(i, 3 + j, land, land, sibling).wait_recv()
        for cp in first + passed:
            cp.wait_send()
        for cp in local:
            cp.wait()

    return _Exchange(list(shards), [jax.ShapeDtypeStruct((4,) + v.shape, v.dtype) for v in shards],
                     [pltpu.SemaphoreType.DMA((6 * n,)), pltpu.SemaphoreType.DMA((6 * n,)),
                      pltpu.SemaphoreType.DMA((n,))], start, finish)


def _scatter_exchange(gs):
    n = len(gs)

    def copies(ins, outs, sems):
        send_sems, recv_sems = sems
        x, y, c = _coords()
        return [pltpu.make_async_remote_copy(
                    src_ref=ins[i].at[2 * px + py], dst_ref=outs[i].at[j], send_sem=send_sems.at[3 * i + j],
                    recv_sem=recv_sems.at[3 * i + j], device_id=(px, py, c), device_id_type=MESH)
                for i in range(n) for j, (px, py) in enumerate(_other_chips(x, y))]

    def start(ins, outs, sems):
        for cp in copies(ins, outs, sems):
            cp.start()

    def finish(ins, outs, sems):
        cps = copies(ins, outs, sems)
        for cp in cps:
            cp.wait_recv()
        for cp in cps:
            cp.wait_send()

    return _Exchange(list(gs), [jax.ShapeDtypeStruct((3,) + g.shape[1:], g.dtype) for g in gs],
                     [pltpu.SemaphoreType.DMA((3 * n,)), pltpu.SemaphoreType.DMA((3 * n,))], start, finish)


def _allgather_exchange(v):
    def copies(ins, outs, sems):
        send_sems, recv_sems, _ = sems
        x, y, c = _coords()
        me = 4 * x + 2 * y + c
        cps = []
        for k in range(1, 8):
            peer = (1 - x if k & 4 else x, 1 - y if k & 2 else y, 1 - c if k & 1 else c)
            cps.append(pltpu.make_async_remote_copy(
                src_ref=ins[0], dst_ref=outs[0].at[me], send_sem=send_sems.at[k - 1], recv_sem=recv_sems.at[k - 1],
                device_id=peer, device_id_type=MESH))
        return me, cps

    def start(ins, outs, sems):
        me, cps = copies(ins, outs, sems)
        pltpu.make_async_copy(ins[0], outs[0].at[me], sems[2]).start()
        for cp in cps:
            cp.start()

    def finish(ins, outs, sems):
        me, cps = copies(ins, outs, sems)
        for cp in cps:
            cp.wait_recv()
        for cp in cps:
            cp.wait_send()
        pltpu.make_async_copy(ins[0], outs[0].at[me], sems[2]).wait()

    return _Exchange([v], [jax.ShapeDtypeStruct((8,) + v.shape, v.dtype)],
                     [pltpu.SemaphoreType.DMA((7,)), pltpu.SemaphoreType.DMA((7,)), pltpu.SemaphoreType.DMA],
                     start, finish)


def _swap_sibling(vs):
    n = len(vs)

    def body(*refs):
        ins, outs = refs[:n], refs[n:2 * n]
        send_sems, recv_sems = refs[2 * n:]
        x, y, c = _coords()
        cps = [pltpu.make_async_remote_copy(src_ref=ins[i], dst_ref=outs[i], send_sem=send_sems.at[i],
                                            recv_sem=recv_sems.at[i], device_id=(x, y, 1 - c), device_id_type=MESH)
               for i in range(n)]
        for cp in cps:
            cp.start()
        for cp in cps:
            cp.wait()

    return pl.pallas_call(
        body, name="swap_sibling", out_shape=[jax.ShapeDtypeStruct(v.shape, v.dtype) for v in vs],
        in_specs=[_HBM] * n, out_specs=[_HBM] * n,
        scratch_shapes=[pltpu.SemaphoreType.DMA((n,)), pltpu.SemaphoreType.DMA((n,))],
    )(*vs)


def _pad_rows(a, rows):
    return jnp.pad(a, ((0, rows - a.shape[0]), (0, 0)))


def _block_diag(w):
    eye = jnp.eye(RG_BLOCKS, dtype=w.dtype)
    return (eye[:, None, :, None] * w[:, :, None, :]).reshape(D_RNN, D_RNN)


def _diag_blocks(g):
    g4 = g.reshape(RG_BLOCKS, RG_BW, RG_BLOCKS, RG_BW)
    idx = jnp.arange(RG_BLOCKS)
    return g4[idx, :, idx, :]


def _prepare_rest(p):
    w = {}
    for k_, n_ in (("pa", "w_proj_a"), ("pb", "w_proj_b"), ("out", "w_out"), ("down", "ffn_w_down")):
        w[k_] = p[n_].astype(MXU_DT)
        w[k_ + "_t"] = w[k_].T
    w["gu"] = jnp.concatenate([p["ffn_w_gate"], p["ffn_w_up"]], axis=1).astype(MXU_DT)
    w["gu_t"] = w["gu"].T
    return w


def _prepare_first(p):
    w = {}
    wi = p["w_in"].astype(MXU_DT)
    cat = jnp.concatenate([wi[:, 2560:6656], wi[:, 6656:8704], wi[:, 8736:10784], wi[:, 0:2560],
                           wi[:, 8704:8736], jnp.zeros((D, 96), MXU_DT)], axis=1)
    w["in_cat"], w["in_cat_t"] = cat, cat.T
    w["rg_cw"] = _pad_rows(p["rg_conv_w"], 8)
    w["dn_cw"] = _pad_rows(p["dn_conv_w"], 8)
    w["ffn_cw"] = _pad_rows(p["ffn_conv_w"], 8)
    w["rg_vec"] = _pad_rows(jnp.stack([p["rg_conv_b"], p["rg_b_a"], p["rg_b_x"], p["rg_lambda"]]), 8)
    w["wa"] = _block_diag(p["rg_w_a"]).astype(MXU_DT)
    w["wx"] = _block_diag(p["rg_w_x"]).astype(MXU_DT)
    w["wa_t"], w["wx_t"] = w["wa"].T, w["wx"].T
    w["arow"] = jnp.pad(p["dn_a_log"], (0, 128 - NV))[None, :]
    w["drow"] = jnp.pad(p["dn_dt_bias"], (0, 128 - NV))[None, :]
    w["acol"] = p["dn_a_log"][:, None]
    w["dcol"] = p["dn_dt_bias"][:, None]
    w["nw"] = p["dn_norm_w"][None, :]
    w["ffn_cb"] = p["ffn_conv_b"][None, :]
    for n_ in ("ln1_g", "ln1_b", "ln2_g", "ln2_b"):
        w[n_] = p[n_][None, :]
    return w


def _mm_sided(a, b, side, **kw):
    if side is None:
        return _mm(a, b, **kw), []
    res = _mm(a, b, side=side, **kw)
    return res[0], res[1:]


_SMALL = ("rg_conv_w", "dn_conv_w", "ffn_conv_w", "rg_conv_b", "rg_w_a", "rg_b_a", "rg_w_x", "rg_b_x",
          "rg_lambda", "dn_a_log", "dn_dt_bias", "dn_norm_w", "ln1_g", "ln1_b", "ffn_conv_b", "ln2_g", "ln2_b")
_REST_A = ("w_proj_a", "w_proj_b", "w_out")
_REST_B = ("ffn_w_gate", "ffn_w_up", "ffn_w_down")


def _local_step(x, tgt, adam, p, shards=None):
    w = _prepare_first(p)
    side_a = side_b = None
    if shards is not None:
        side_a = _gather_exchange([shards[n] for n in _REST_A])
        side_b = _gather_exchange([shards[n] for n in _REST_B])
    h1, h1t = _modulate(x, adam)
    cat = w["in_cat"]
    pqkv, got_a = _mm_sided(h1, cat[:, 0:4096], side_a, name="proj_qkv")
    px, got_b = _mm_sided(h1, cat[:, 8192:10752], side_b, name="proj_x")
    if shards is not None:
        p = dict(p, **{n: _unstack_shards(n, a_) for n, a_ in zip(_REST_A + _REST_B, got_a + got_b)})
    w.update(_prepare_rest(p))
    pz = _mm(h1, cat[:, 4096:6144], name="proj_z")
    pg = _mm(h1, cat[:, 6144:8192], name="proj_g")
    pab = _mm(h1, cat[:, 10752:10880], name="proj_ab")
    hrec, xcs, rec, rec_t = _rg_fwd(px, w["rg_cw"], w["rg_vec"], w["wa"], w["wx"])
    q, k, v, rt, cpre = _dn_prep(pqkv, pab, w["dn_cw"], w["arow"], w["drow"])
    u, ww, qd, kd, pm, tinv, gam = _dn_intra(q, k, v, rt)
    o, vn, ssave = _dn_seq(u, ww, qd, kd, pm, gam)
    dn, dn_t = _dn_post(o, pz, w["nw"])
    ya = _mm(rec, w["pa"], name="proj_a")
    yb = _mm(dn, w["pb"], name="proj_b")
    merged, merged_t = _merge(pg, ya, yb)
    mix = _mm(merged, w["out"], name="proj_out")
    r1, x2, h2, h2_t = _ln1(x, mix, adam, w["ln1_g"], w["ln1_b"])
    gu = _mm(h2, w["gu"], name="ffn_gu")
    act, act_t, gcf = _ffn_act(gu, w["ffn_cw"], w["ffn_cb"])
    ff = _mm(act, w["down"], name="ffn_down")
    dff, dx2a, red2 = _ln2_loss(x2, ff, tgt, adam, w["ln2_g"], w["ln2_b"])
    g = {}
    dact = _mm(dff, w["down_t"], name="d_act")
    g["ffn_w_down"] = _mm(act_t, dff, name="g_down")
    dgu, gcw_f, gcb_f = _ffn_bwd(dact, gu, gcf, w["ffn_cw"])
    dh2 = _mm(dgu, w["gu_t"], name="d_h2")
    ggu = _mm(h2_t, dgu, name="g_gu")
    g["ffn_w_gate"], g["ffn_w_up"] = ggu[:, :D_FF], ggu[:, D_FF:]
    g["ffn_conv_w"], g["ffn_conv_b"] = gcw_f[0:3], gcb_f[0]
    dmix, dxa, red1 = _ln1_bwd(dh2, dx2a, x2, r1, mix, adam, w["ln1_g"])
    dmg = _mm(dmix, w["out_t"], name="d_merged")
    g["w_out"] = _mm(merged_t, dmix, name="g_out")
    dya, dyb, dpg = _merge_bwd(dmg, pg, ya, yb)
    drec = _mm(dya, w["pa_t"], name="d_rec")
    g["w_proj_a"] = _mm(rec_t, dya, name="g_pa")
    ddn = _mm(dyb, w["pb_t"], name="d_dn")
    g["w_proj_b"] = _mm(dn_t, dyb, name="g_pb")
    do, dpz, gnw = _dn_post_bwd(ddn, o, pz, w["nw"])
    dvn, dkd, dgam = _dn_seq_bwd(do, qd, kd, pm, ww, vn, ssave, gam)
    dq, dk, dv, drt = _dn_intra_bwd(q, k, v, rt, do, dvn, dkd, gam, dgam, ssave, tinv, u, ww, vn)
    dpqkv, dpab, gcw_d, gsc = _dn_prep_bwd(dq, dk, dv, drt, pqkv, cpre, pab, w["dn_cw"], w["acol"], w["dcol"])
    side_r = None
    if shards is not None:
        side_r = _scatter_exchange([_stack_shards(n, g[n]).astype(MXU_DT) for n in _REST_A + _REST_B])
    dpx, gwa, gwx, gcw_r, gvec, *got_r = _rg_bwd(drec, px, xcs, hrec, w["rg_cw"], w["rg_vec"], w["wa"], w["wx"],
                                                 w["wa_t"], w["wx_t"], side_r)
    dproj = jnp.concatenate([dpqkv, dpz, dpg, dpx, dpab], axis=1)
    reorder = lambda gc: jnp.concatenate([gc[:, 8192:10752], gc[:, 0:4096], gc[:, 4096:6144], gc[:, 10752:10784],
                                          gc[:, 6144:8192]], axis=1)
    wire = lambda gh: _scatter_exchange([_stack_shards("w_in", gh).astype(MXU_DT)]) if shards is not None else None
    g["rg_conv_w"], g["rg_conv_b"] = gcw_r[0:4], gvec[0]
    g["rg_w_a"], g["rg_w_x"] = _diag_blocks(gwa), _diag_blocks(gwx)
    g["rg_b_a"], g["rg_b_x"], g["rg_lambda"] = gvec[1], gvec[2], gvec[3]
    g["dn_conv_w"] = gcw_d[0:4]
    g["dn_a_log"], g["dn_dt_bias"], g["dn_norm_w"] = gsc[:, 0], gsc[:, 1], gnw[0]
    g["ln1_g"], g["ln1_b"] = red1[2], red1[3]
    g["ln2_g"], g["ln2_b"] = red2[0], red2[1]
    side_s = None
    if shards is not None:
        side_s = _allgather_exchange(_pack([jnp.full((1,), red2[3, 0], F32)] + [g[n] for n in _SMALL], 128, 64))
    g_top, got_s = _mm_sided(h1t[:D // 2], dproj, side_s, name="g_in_top")
    g_top = reorder(g_top)
    g_bot, got_top = _mm_sided(h1t[D // 2:], dproj, wire(g_top), name="g_in_bot")
    g_bot = reorder(g_bot)
    dh1, got_bot = _mm_sided(dproj, w["in_cat_t"], wire(g_bot), name="d_h1")
    g["w_in"] = jnp.concatenate([g_top, g_bot], axis=0)
    got = dict(zip(_REST_A + _REST_B, got_r))
    if shards is not None:
        got["w_in"] = jnp.concatenate([got_top[0], got_bot[0]], axis=1)
        got["small"] = got_s[0]
    gx, red0 = _modulate_bwd(dh1, dxa, x, adam)
    d_ada = jnp.concatenate([red0[1], red0[0], red1[4], red1[1], red1[0], red2[2]])
    return red2[3, 0], gx, g, d_ada, got


_BIG = ("w_in", "w_proj_a", "w_proj_b", "w_out", "ffn_w_gate", "ffn_w_up", "ffn_w_down")
_COL_SHARDED = ("w_in", "ffn_w_gate", "ffn_w_up")
_CONV = ("rg_conv_w", "dn_conv_w", "ffn_conv_w")
_REPL = ("b_ada", "rg_conv_b", "rg_w_a", "rg_b_a", "rg_w_x", "rg_b_x", "rg_lambda", "dn_a_log",
         "dn_dt_bias", "dn_norm_w", "ln1_g", "ln1_b", "ffn_conv_b", "ln2_g", "ln2_b")
_NAMES = ("w_ada", "b_ada", "w_in", "rg_conv_w", "rg_conv_b", "rg_w_a", "rg_b_a", "rg_w_x", "rg_b_x",
          "rg_lambda", "dn_conv_w", "dn_a_log", "dn_dt_bias", "dn_norm_w", "w_proj_a", "w_proj_b", "w_out",
          "ln1_g", "ln1_b", "ffn_w_gate", "ffn_w_up", "ffn_conv_w", "ffn_conv_b", "ffn_w_down", "ln2_g", "ln2_b")


def _pack(arrs, width, row_mult):
    pieces = []
    for a in arrs:
        f = a.reshape(-1)
        pieces.append(jnp.pad(f, (0, (-f.shape[0]) % (8 * width))).reshape(-1, width))
    rows = sum(p_.shape[0] for p_ in pieces)
    if rows % row_mult:
        pieces.append(jnp.zeros((row_mult - rows % row_mult, width), pieces[0].dtype))
    return jnp.concatenate(pieces, axis=0)


def _unpack(flat, shapes, width):
    out, row = [], 0
    for shp in shapes:
        n = 1
        for d_ in shp:
            n *= d_
        rows = -(-n // (8 * width)) * 8
        out.append(flat[row:row + rows].reshape(-1)[:n].reshape(shp))
        row += rows
    return out


def _stack_shards(name, full):
    if name in _COL_SHARDED or name in _CONV:
        r, ncol = full.shape
        return full.reshape(r, 4, ncol // 4).transpose(1, 0, 2)
    return full.reshape((4, full.shape[0] // 4) + full.shape[1:])


def _unstack_shards(name, st):
    if name in _COL_SHARDED or name in _CONV:
        return st.transpose(1, 0, 2).reshape(st.shape[1], 4 * st.shape[2])
    return st.reshape((4 * st.shape[1],) + st.shape[2:])


def kernel(x, c, w_ada, b_ada, w_in, rg_conv_w, rg_conv_b, rg_w_a, rg_b_a, rg_w_x, rg_b_x, rg_lambda, dn_conv_w, dn_a_log, dn_dt_bias, dn_norm_w, w_proj_a, w_proj_b, w_out, ln1_g, ln1_b, ffn_w_gate, ffn_w_up, ffn_conv_w, ffn_conv_b, ffn_w_down, ln2_g, ln2_b, loss_target, m_w_ada, m_b_ada, m_w_in, m_rg_conv_w, m_rg_conv_b, m_rg_w_a, m_rg_b_a, m_rg_w_x, m_rg_b_x, m_rg_lambda, m_dn_conv_w, m_dn_a_log, m_dn_dt_bias, m_dn_norm_w, m_w_proj_a, m_w_proj_b, m_w_out, m_ln1_g, m_ln1_b, m_ffn_w_gate, m_ffn_w_up, m_ffn_conv_w, m_ffn_conv_b, m_ffn_w_down, m_ln2_g, m_ln2_b, v_w_ada, v_b_ada, v_w_in, v_rg_conv_w, v_rg_conv_b, v_rg_w_a, v_rg_b_a, v_rg_w_x, v_rg_b_x, v_rg_lambda, v_dn_conv_w, v_dn_a_log, v_dn_dt_bias, v_dn_norm_w, v_w_proj_a, v_w_proj_b, v_w_out, v_ln1_g, v_ln1_b, v_ffn_w_gate, v_ffn_w_up, v_ffn_conv_w, v_ffn_conv_b, v_ffn_w_down, v_ln2_g, v_ln2_b):
    args = locals()
    wts = {n: args[n][0] for n in _NAMES}
    mom = {n: args["m_" + n][0] for n in _NAMES}
    var = {n: args["v_" + n][0] for n in _NAMES}
    xs, tgt = x[0], loss_target[0]
    ix, iy, ic = _coords()
    shard = 2 * ix + iy
    batch = 4 * ix + 2 * iy + ic

    c_all = _allgather8(_pad_rows(c, 8), "gather_c")[:, 0, :]
    sc16 = _pad_rows(_silu_rows(c_all), 16)
    ada_cols = _mm(sc16, w_ada[0], name="ada")[:8]
    ada_g = _allgather8(ada_cols, "gather_ada")
    ada_all = jnp.concatenate([ada_g[0], ada_g[2], ada_g[4], ada_g[6]], axis=1) + b_ada
    adam = _pad_rows(lax.dynamic_index_in_dim(ada_all, batch, 0, keepdims=False).reshape(6, D), 8)

    (w_in_all,) = _run_exchange(_gather_exchange([wts["w_in"].astype(MXU_DT)]), "gather_w_in")
    full = {"w_in": _unstack_shards("w_in", w_in_all)}
    conv_shard = _pack([wts[n] for n in _CONV], 128, 8)
    conv_all = _allgather8(conv_shard, "gather_conv")
    shapes_conv = [wts[n].shape for n in _CONV]
    per_shard = [_unpack(conv_all[2 * s], shapes_conv, 128) for s in range(4)]
    for i, n in enumerate(_CONV):
        full[n] = _unstack_shards(n, jnp.stack([per_shard[s][i] for s in range(4)]))
    for n in _REPL:
        full[n] = wts[n]

    shards = {n: wts[n].astype(MXU_DT) for n in _REST_A + _REST_B}
    loss_b, gx, g, d_ada, recv = _local_step(xs, tgt, adam, full, shards)

    parts = []
    for n in _BIG:
        r_ = recv[n]
        axis = 1 if n in _COL_SHARDED else 0
        width = wts[n].shape[axis]
        own = lax.dynamic_slice_in_dim(g[n], shard * width, width, axis=axis)
        parts.append(_sum_partials(own, r_, "sum_" + n))
    parts_sib = _swap_sibling(parts)
    out = {n: [] for n in _NAMES}
    for n, p_, q_ in zip(_BIG, parts, parts_sib):
        out[n] = list(_adamw([p_, q_], wts[n], mom[n], var[n], "adamw_" + n))

    tot = _unpack(_sum8(recv["small"], "sum_small"), [(1,)] + [full[n].shape for n in _SMALL], 128)
    gsum = dict(zip(_SMALL, tot[1:]))
    loss = tot[0][0]
    for n in _CONV:
        gsum[n] = lax.dynamic_index_in_dim(_stack_shards(n, gsum[n]), shard, 0, keepdims=False)
    d_ada_g = _allgather8(d_ada.reshape(6 * D // 128, 128), "gather_d_ada")
    gsum["b_ada"] = _sum8(d_ada_g, "sum_d_ada").reshape(6 * D)
    d_ada_all = d_ada_g.reshape(8, 6 * D)
    cols = lax.dynamic_slice_in_dim(d_ada_all, shard * (6 * D // 4), 6 * D // 4, axis=1)
    g_wada = _mm(sc16, _pad_rows(cols, 16), name="g_ada", trans_a=True)
    res = _adamw([g_wada], wts["w_ada"], mom["w_ada"], var["w_ada"], "adamw_ada")
    out["w_ada"] = list(res)
    names_s = _CONV + _REPL
    shapes_s = [wts[n].shape for n in names_s]
    pk = lambda d_: _pack([d_[n] for n in names_s], 128, 64)
    res_s = _adamw([pk(gsum)], pk(wts), pk(mom), pk(var), "adamw_small")
    for r_ in res_s:
        for n, a in zip(names_s, _unpack(r_, shapes_s, 128)):
            out[n].append(a)

    outs = [loss, gx[None]]
    for i in range(4):
        outs += [out[n][i][None] for n in _NAMES]
    return tuple(outs)
```

```python
import functools

import jax
import jax.numpy as jnp
from jax import lax
from jax.experimental import pallas as pl
from jax.experimental.pallas import tpu as pltpu

F32 = jnp.float32
BF16 = jnp.bfloat16
MXU_DT = BF16

D = 1024
D_RNN = 1280
RG_BLOCKS = 16
RG_BW = 80
RG_C = 8.0
NQ = 8
NV = 16
HD = 128
CH = 64
D_FF = 2816
LN_EPS = 1e-5
RMS_EPS = 1e-6
L2_EPS = 1e-6
ALPHA = 2.0 ** 0.25
Q_SCALE = HD ** -0.5
N_CAT = 10880
VMEM_LIMIT = 56 * 1024 * 1024
MM_VMEM_BUDGET = 36 * 1024 * 1024
SEQ_GROUP = 16
MESH = pl.DeviceIdType.MESH

ADAM_LR, ADAM_B1, ADAM_B2, ADAM_EPS, ADAM_WD, ADAM_STEP = 1e-3, 0.9, 0.999, 1e-8, 0.01, 10


def _sigmoid(x):
    return 0.5 * jnp.tanh(0.5 * x) + 0.5


def _softplus(x):
    return jnp.maximum(x, 0.0) + jnp.log1p(jnp.exp(-jnp.abs(x)))


_GC = 0.7978845608028654


def _gelu(x):
    return 0.5 * x * (1.0 + jnp.tanh(_GC * (x + 0.044715 * x * x * x)))


def _gelu_and_grad(x):
    t = jnp.tanh(_GC * (x + 0.044715 * x * x * x))
    g = 0.5 * x * (1.0 + t)
    dg = 0.5 * (1.0 + t) + 0.5 * x * (1.0 - t * t) * _GC * (1.0 + 3 * 0.044715 * x * x)
    return g, dg


def _neg_expm1(y):
    series = -y * (1.0 + 0.5 * y * (1.0 + y * (1.0 / 3.0)))
    return jnp.where(y > -0.01, series, 1.0 - jnp.exp(y))


def _dot(a, b):
    return jnp.dot(a.astype(MXU_DT), b.astype(MXU_DT), preferred_element_type=F32)


def _dot_nt(a, b):
    return lax.dot_general(a.astype(MXU_DT), b.astype(MXU_DT), (((1,), (1,)), ((), ())),
                           preferred_element_type=F32)


def _dot_tn(a, b):
    return lax.dot_general(a.astype(MXU_DT), b.astype(MXU_DT), (((0,), (0,)), ((), ())),
                           preferred_element_type=F32)


def _split(a):
    hi = a.astype(BF16)
    return hi, (a - hi.astype(F32)).astype(BF16)


def _dot3(a, b, dims=(((1,), (0,)), ((), ()))):
    ah, al = _split(a)
    bh, bl = _split(b)
    d = lambda p, q: lax.dot_general(p, q, dims, preferred_element_type=F32)
    return d(ah, bh) + (d(al, bh) + d(ah, bl))


def _dot3_tn(a, b):
    return _dot3(a, b, (((0,), (0,)), ((), ())))


def _iota(shape, dim):
    return lax.broadcasted_iota(jnp.int32, shape, dim)


def _shift_down(x, before, j):
    if j == 0:
        return x
    xr = pltpu.roll(x, j, 0)
    br = pltpu.roll(before, j, 0)
    top = jnp.where(_iota(br.shape, 0) < j, br, xr[:8])
    return jnp.concatenate([top, xr[8:]], axis=0)


def _shift_up(x, after, j):
    if j == 0:
        return x
    t = x.shape[0]
    xr = pltpu.roll(x, t - j, 0)
    ar = pltpu.roll(after, 8 - j, 0)
    bot = jnp.where(_iota(ar.shape, 0) >= 8 - j, ar, xr[t - 8:])
    return jnp.concatenate([xr[:t - 8], bot], axis=0)


def _taps(x, before, k):
    return [_shift_down(x, before, k - 1 - i) for i in range(k)]


def _conv_taps(taps, w_ref):
    y = w_ref[0:1, :] * taps[0]
    for i in range(1, len(taps)):
        y = y + w_ref[i:i + 1, :] * taps[i]
    return y


def _conv_causal(x, before, w_ref, k):
    return _conv_taps(_taps(x, before, k), w_ref)


def _conv_causal_bwd(dy, after, w_ref, k, x=None, gw_ref=None):
    dx = None
    for i in range(k):
        sh = _shift_up(dy, after, k - 1 - i)
        term = w_ref[i:i + 1, :] * sh
        dx = term if dx is None else dx + term
        if x is not None:
            gw_ref[i:i + 1, :] += _rsum(x * sh)
    return dx


def _scan_fwd(a, u):
    t = a.shape[0]
    rows = _iota(a.shape, 0)
    d = 1
    while d < t:
        m = rows >= d
        u = u + jnp.where(m, a * pltpu.roll(u, d, 0), 0.0)
        a = jnp.where(m, a * pltpu.roll(a, d, 0), a)
        d *= 2
    return a, u


def _scan_rev(a, u):
    t = a.shape[0]
    rows = _iota(a.shape, 0)
    d = 1
    while d < t:
        m = rows < t - d
        u = u + jnp.where(m, a * pltpu.roll(u, t - d, 0), 0.0)
        a = jnp.where(m, a * pltpu.roll(a, t - d, 0), a)
        d *= 2
    return a, u


def _chunk_cumsum(g, axis, rev=False):
    n = g.shape[axis]
    pos = _iota(g.shape, axis) & (CH - 1)
    d = 1
    while d < CH:
        if rev:
            g = g + jnp.where(pos < CH - d, pltpu.roll(g, n - d, axis), 0.0)
        else:
            g = g + jnp.where(pos >= d, pltpu.roll(g, d, axis), 0.0)
        d *= 2
    return g


def _ln_stats(r):
    mu = jnp.mean(r, axis=-1, keepdims=True)
    xc = r - mu
    var = jnp.mean(xc * xc, axis=-1, keepdims=True)
    rstd = lax.rsqrt(var + LN_EPS)
    return xc * rstd, rstd


def _ln_bwd(dy, xhat, rstd, g):
    dxh = dy * g
    return rstd * (dxh - jnp.mean(dxh, axis=-1, keepdims=True)
                   - xhat * jnp.mean(dxh * xhat, axis=-1, keepdims=True))


def _rsum(x):
    return jnp.sum(x, axis=0, keepdims=True)


def _params(sem):
    return pltpu.CompilerParams(dimension_semantics=sem, vmem_limit_bytes=VMEM_LIMIT)


def _pick(n, cands):
    for c in cands:
        if n % c == 0:
            return c
    return n


def _rows(tm, w, col=0, nt=None):
    if nt is None:
        return pl.BlockSpec((tm, w), lambda i: (i, col))
    return pl.BlockSpec((tm, w), lambda i: (nt - 1 - i, col))


def _before(tm, w, col=0, nt=None):
    r = tm // 8
    if nt is None:
        return pl.BlockSpec((8, w), lambda i: (jnp.maximum(i * r - 1, 0), col))
    return pl.BlockSpec((8, w), lambda i: (jnp.maximum((nt - 1 - i) * r - 1, 0), col))


def _cols(tm, w):
    return pl.BlockSpec((w, tm), lambda i: (0, i))


def _whole(shape):
    return pl.BlockSpec(shape, lambda *_: (0,) * len(shape))


def _mm_plan(a, b, out_dtype):
    m, kk = a.shape
    _, n = b.shape
    tm = _pick(m, (512, 256, 128))
    tk = kk if kk <= 5632 else _pick(kk, (2176, 2048, 1024))
    nk = kk // tk

    def vmem_bytes(tn):
        blocks = tm * tk * a.dtype.itemsize + tk * tn * b.dtype.itemsize + tm * tn * jnp.dtype(out_dtype).itemsize
        return 2 * blocks + (tm * tn * 4 if nk > 1 else 0)

    cands = [t for t in (1408, 1280, 1024, 640, 512, 256, 128) if n % t == 0] or [n]
    tn = next((t for t in cands if vmem_bytes(t) <= MM_VMEM_BUDGET), cands[-1])
    return tm, tn, tk, nk


def _mm_side(a, b, name, out_dtype, side):
    m, _ = a.shape
    _, n = b.shape
    tm, tn, tk, nk = _mm_plan(a, b, out_dtype)
    ni, nj = m // tm, n // tn
    n_in, n_out = len(side.ins), len(side.out_shapes)

    def body(*refs):
        a_ref, b_ref = refs[0], refs[1]
        s_in = refs[2:2 + n_in]
        o_ref = refs[2 + n_in]
        s_out = refs[3 + n_in:3 + n_in + n_out]
        acc = refs[3 + n_in + n_out]
        sems = refs[4 + n_in + n_out:]
        i, j, k = pl.program_id(0), pl.program_id(1), pl.program_id(2)

        @pl.when((i == 0) & (j == 0) & (k == 0))
        def _():
            side.start(s_in, s_out, sems)

        @pl.when(k == 0)
        def _():
            acc[...] = jnp.zeros_like(acc)
        acc[...] += _dot(a_ref[...], b_ref[...])

        @pl.when(k == nk - 1)
        def _():
            o_ref[...] = acc[...].astype(out_dtype)

        @pl.when((i == ni - 1) & (j == nj - 1) & (k == nk - 1))
        def _():
            side.finish(s_in, s_out, sems)

    return pl.pallas_call(
        body, name=name, grid=(ni, nj, nk),
        in_specs=[pl.BlockSpec((tm, tk), lambda i, j, k: (i, k)),
                  pl.BlockSpec((tk, tn), lambda i, j, k: (k, j))] + [_HBM] * n_in,
        out_specs=[pl.BlockSpec((tm, tn), lambda i, j, k: (i, j))] + [_HBM] * n_out,
        out_shape=[jax.ShapeDtypeStruct((m, n), out_dtype)] + list(side.out_shapes),
        scratch_shapes=[pltpu.VMEM((tm, tn), F32)] + list(side.sems),
        compiler_params=_params(("arbitrary", "arbitrary", "arbitrary")),
    )(a, b, *side.ins)


def _mm(a, b, *, name, trans_a=False, out_dtype=F32, side=None):
    if trans_a:
        return _mm(a.T, b, name=name, out_dtype=out_dtype, side=side)
    if side is not None:
        return _mm_side(a, b, name, out_dtype, side)
    m, kk = a.shape
    _, n = b.shape
    tm, tn, tk, nk = _mm_plan(a, b, out_dtype)

    if nk == 1:
        def body(a_ref, b_ref, o_ref):
            o_ref[...] = _dot(a_ref[...], b_ref[...]).astype(out_dtype)
        scratch = []
    else:
        def body(a_ref, b_ref, o_ref, acc):
            k = pl.program_id(2)

            @pl.when(k == 0)
            def _():
                acc[...] = jnp.zeros_like(acc)
            acc[...] += _dot(a_ref[...], b_ref[...])

            @pl.when(k == nk - 1)
            def _():
                o_ref[...] = acc[...].astype(out_dtype)
        scratch = [pltpu.VMEM((tm, tn), F32)]

    return pl.pallas_call(
        body, name=name, grid=(m // tm, n // tn, nk),
        in_specs=[pl.BlockSpec((tm, tk), lambda i, j, k: (i, k)),
                  pl.BlockSpec((tk, tn), lambda i, j, k: (k, j))],
        out_specs=pl.BlockSpec((tm, tn), lambda i, j, k: (i, j)),
        out_shape=jax.ShapeDtypeStruct((m, n), out_dtype),
        scratch_shapes=scratch,
        compiler_params=_params(("parallel", "parallel", "arbitrary")),
    )(a, b)


def _modulate(x, adam):
    s = x.shape[0]
    tm = _pick(s, (512, 256, 128))

    def body(x_ref, ada_ref, o_ref, ot_ref):
        h = x_ref[...] * (1.0 + ada_ref[1:2, :]) + ada_ref[0:1, :]
        o_ref[...] = h.astype(MXU_DT)
        ot_ref[...] = h.T.astype(MXU_DT)

    return pl.pallas_call(
        body, name="modulate1", grid=(s // tm,),
        in_specs=[_rows(tm, D), _whole((8, D))], out_specs=[_rows(tm, D), _cols(tm, D)],
        out_shape=[jax.ShapeDtypeStruct((s, D), MXU_DT), jax.ShapeDtypeStruct((D, s), MXU_DT)],
        compiler_params=_params(("parallel",)),
    )(x, adam)


def _rg_gates(xc, wa_ref, wx_ref, vec_ref):
    xb = xc.astype(MXU_DT)
    r = _sigmoid(jnp.dot(xb, wa_ref[...], preferred_element_type=F32) + vec_ref[1:2, :])
    ig = _sigmoid(jnp.dot(xb, wx_ref[...], preferred_element_type=F32) + vec_ref[2:3, :])
    sp = _softplus(-vec_ref[3:4, :])
    la = -RG_C * r * sp
    a = jnp.exp(la)
    n1 = _neg_expm1(2.0 * la)
    rmult = lax.rsqrt(jnp.maximum(n1, 1e-20))
    return r, ig, a, n1 * rmult, sp, rmult


def _rg_fwd(px, cw, vec, wa, wx):
    s = px.shape[0]
    tm = _pick(s, (256, 128))
    w = D_RNN

    def body(xr_ref, gr_ref, cw_ref, vec_ref, wa_ref, wx_ref, h_ref, xc_ref, rec_ref, rect_ref, prev_x, prev_h):
        @pl.when(pl.program_id(0) == 0)
        def _():
            prev_x[...] = jnp.zeros_like(prev_x)
            prev_h[...] = jnp.zeros_like(prev_h)
        x = xr_ref[...]
        xc = _conv_causal(x, prev_x[...], cw_ref, 4) + vec_ref[0:1, :]
        prev_x[...] = x[tm - 8:, :]
        xc_ref[...] = xc
        _, ig, a, mult, _, _ = _rg_gates(xc, wa_ref, wx_ref, vec_ref)
        acum, h = _scan_fwd(a, mult * ig * xc)
        h = h + acum * prev_h[7:8, :]
        prev_h[...] = h[tm - 8:, :]
        h_ref[...] = h
        rec = h * _gelu(gr_ref[...])
        rec_ref[...] = rec.astype(MXU_DT)
        rect_ref[...] = rec.T.astype(MXU_DT)

    return pl.pallas_call(
        body, name="rg_fwd", grid=(s // tm,),
        in_specs=[_rows(tm, w, 0), _rows(tm, w, 1), _whole((8, w)), _whole((8, w)),
                  _whole((w, w)), _whole((w, w))],
        out_specs=[_rows(tm, w), _rows(tm, w), _rows(tm, w), _cols(tm, w)],
        out_shape=[jax.ShapeDtypeStruct((s, w), F32), jax.ShapeDtypeStruct((s, w), F32),
                   jax.ShapeDtypeStruct((s, w), MXU_DT), jax.ShapeDtypeStruct((w, s), MXU_DT)],
        scratch_shapes=[pltpu.VMEM((8, w), F32), pltpu.VMEM((8, w), F32)],
        compiler_params=_params(("arbitrary",)),
    )(px, px, cw, vec, wa, wx)


def _dn_scalars(ab, arow, drow):
    lane = _iota(ab.shape, 1)
    g = jnp.where(lane < NV, -jnp.exp(arow) * _softplus(ab + drow), 0.0)
    beta = _sigmoid(ab)
    return lane, g, beta


def _l2n_heads(c, out_ref, off, scale):
    for hh in range(NQ):
        x = c[:, off + hh * HD: off + (hh + 1) * HD]
        r = lax.rsqrt(jnp.sum(x * x, axis=-1, keepdims=True) + L2_EPS)
        out_ref[:, hh * HD:(hh + 1) * HD] = x * (r * scale)


def _dn_prep(pqkv, pab, cw, arow, drow):
    s = pqkv.shape[0]
    tm = 128
    wq = NQ * HD

    def body(x_ref, ab_ref, cw_ref, a_ref, d_ref, q_ref, k_ref, v_ref, rt_ref, cp_ref, prev_x):
        @pl.when(pl.program_id(0) == 0)
        def _():
            prev_x[...] = jnp.zeros_like(prev_x)
        x = x_ref[...]
        cp = _conv_causal(x, prev_x[...], cw_ref, 4)
        prev_x[...] = x[tm - 8:, :]
        cp_ref[...] = cp
        c = cp * _sigmoid(cp)
        _l2n_heads(c, q_ref, 0, Q_SCALE)
        _l2n_heads(c, k_ref, wq, 1.0)
        v_ref[...] = c[:, 2 * wq:]
        lane, g, beta = _dn_scalars(ab_ref[...], a_ref[...], d_ref[...])
        gc = _chunk_cumsum(g, 0)
        gl = gc + _chunk_cumsum(g, 0, rev=True) - g
        pack = jnp.where(lane < NV, gc, jnp.where(lane < 2 * NV, beta,
                         jnp.where(lane < 3 * NV, pltpu.roll(gl, 2 * NV, 1), 0.0)))
        rt_ref[...] = pack.T[0:3 * NV, :]

    return pl.pallas_call(
        body, name="dn_prep", grid=(s // tm,),
        in_specs=[_rows(tm, 4 * wq), _rows(tm, 128), _whole((8, 4 * wq)), _whole((1, 128)), _whole((1, 128))],
        out_specs=[_rows(tm, wq), _rows(tm, wq), _rows(tm, 2 * wq),
                   pl.BlockSpec((3 * NV, tm), lambda i: (0, i)), _rows(tm, 4 * wq)],
        out_shape=[jax.ShapeDtypeStruct((s, wq), F32), jax.ShapeDtypeStruct((s, wq), F32),
                   jax.ShapeDtypeStruct((s, 2 * wq), F32), jax.ShapeDtypeStruct((3 * NV, s), F32),
                   jax.ShapeDtypeStruct((s, 4 * wq), F32)],
        scratch_shapes=[pltpu.VMEM((8, 4 * wq), F32)],
        compiler_params=_params(("arbitrary",)),
    )(pqkv, pab, cw, arow, drow)


def _pair_masks():
    i = _iota((2 * CH, 2 * CH), 0)
    j = _iota((2 * CH, 2 * CH), 1)
    same = (i >> 6) == (j >> 6)
    return same & (i >= j), same & (i > j)


def _head_cols(rt_ref, h):
    shp = (2 * CH, 2 * CH)
    g_r = jnp.broadcast_to(rt_ref[pl.ds(h, 1), :], shp)
    b_r = jnp.broadcast_to(rt_ref[pl.ds(NV + h, 1), :], shp)
    l_r = jnp.broadcast_to(rt_ref[pl.ds(2 * NV + h, 1), :], shp)
    return g_r, g_r.T, b_r.T, l_r, l_r.T


def _inv_unit_lower(a):
    return _inv_unit_lower_many([a])[0]


def _inv_unit_lower_many(a_list):
    n = a_list[0].shape[0]
    eye = (_iota((n, n), 0) == _iota((n, n), 1)).astype(F32)
    bs = [-a for a in a_list]
    xs = [eye + b for b in bs]
    for _ in range(5):
        bs = [_dot(b, b) for b in bs]
        xs = [x + _dot(x, b) for x, b in zip(xs, bs)]
    rs = [(eye - x) - _dot3(a, x) for a, x in zip(a_list, xs)]
    return [x + _dot(x, r) for x, r in zip(xs, rs)]


def _gam_rows(l_r):
    lrow = l_r[0:1, :]
    lane = _iota(lrow.shape, 1)
    other = pltpu.roll(lrow, CH, 1)
    return jnp.exp(jnp.where(lane < CH, lrow, other)), jnp.exp(jnp.where(lane >= CH, lrow, other))


def _dn_intra(q, k, v, rt):
    s = q.shape[0]
    nb = s // (2 * CH)
    qps = 8
    blk = pl.BlockSpec((2 * CH, qps * HD), lambda i, h: (i, h))
    blk2 = pl.BlockSpec((2 * CH, 2 * qps * HD), lambda i, h: (i, h))

    def body(q_ref, k_ref, v_ref, rt_ref, u_ref, w_ref, qd_ref, kd_ref, p_ref, ti_ref, gam_ref):
        hstep = pl.program_id(1)
        mc, ms = _pair_masks()
        gam_ref[...] = jnp.zeros_like(gam_ref)
        heads = []
        for qh in range(qps):
            qq = q_ref[:, qh * HD:(qh + 1) * HD]
            kk_ = k_ref[:, qh * HD:(qh + 1) * HD]
            kk = _dot_nt(kk_, kk_)
            qk = _dot_nt(qq, kk_)
            for j in range(2):
                idx = 2 * qh + j
                cs = slice(idx * HD, (idx + 1) * HD)
                g_r, g_c, b_c, l_r, l_c = _head_cols(rt_ref, 2 * qps * hstep + idx)
                dec = jnp.where(mc, jnp.exp(jnp.where(mc, g_c - g_r, 0.0)), 0.0)
                eg = jnp.exp(g_c)
                p_ref[:, cs] = jnp.where(mc, qk * dec, 0.0)
                qd_ref[:, cs] = eg * qq
                kd_ref[:, cs] = jnp.exp(l_c - g_c) * kk_
                ga, gb = _gam_rows(l_r)
                gam_ref[0, qh, 2 * j:2 * j + 1, :] = ga
                gam_ref[0, qh, 2 * j + 1:2 * j + 2, :] = gb
                rhs = jnp.concatenate([b_c * v_ref[:, cs], b_c * eg * kk_], axis=1)
                heads.append((cs, jnp.where(ms, b_c * kk * dec, 0.0), rhs))
        tinvs = _inv_unit_lower_many([a for _, a, _ in heads])
        uws = [_dot3(t, rhs) for t, (_, _, rhs) in zip(tinvs, heads)]
        for t, uw, (cs, _, _) in zip(tinvs, uws, heads):
            ti_ref[:, cs] = t
            u_ref[:, cs] = uw[:, :HD]
            w_ref[:, cs] = uw[:, HD:]

    big = jax.ShapeDtypeStruct((s, NV * HD), F32)
    return pl.pallas_call(
        body, name="dn_intra", grid=(nb, NQ // qps),
        in_specs=[blk, blk, blk2, pl.BlockSpec((3 * NV, 2 * CH), lambda i, h: (0, i))],
        out_specs=[blk2] * 6 + [pl.BlockSpec((1, qps, 8, 128), lambda i, h: (i, h, 0, 0))],
        out_shape=[big] * 6 + [jax.ShapeDtypeStruct((nb, NQ, 8, 128), F32)],
        compiler_params=_params(("parallel", "parallel")),
    )(q, k, v, rt)


def _dn_seq(u, w, qd, kd, p, gam):
    s = u.shape[0]
    nb = s // (2 * CH)
    wide = pl.BlockSpec((2 * CH, NV * HD), lambda i: (i, 0))

    def body(u_ref, w_ref, qd_ref, kd_ref, p_ref, gam_ref, o_ref, vn_ref, ss_ref, st):
        @pl.when(pl.program_id(0) == 0)
        def _():
            st[...] = jnp.zeros_like(st)
        ra, rb = slice(0, CH), slice(CH, 2 * CH)
        for g0 in range(0, NV, SEQ_GROUP):
            hs = list(range(g0, g0 + SEQ_GROUP))
            cs = [slice(h * HD, (h + 1) * HD) for h in hs]
            ga = [gam_ref[0, h // 2, 2 * (h % 2):2 * (h % 2) + 1, :] for h in hs]
            gb = [gam_ref[0, h // 2, 2 * (h % 2) + 1:2 * (h % 2) + 2, :] for h in hs]
            s0 = [st[h] for h in hs]
            vna = [u_ref[ra, c] - _dot(w_ref[ra, c], s) for c, s in zip(cs, s0)]
            s1 = [g * s + _dot_tn(kd_ref[ra, c], v) for g, s, c, v in zip(ga, s0, cs, vna)]
            vnb = [u_ref[rb, c] - _dot(w_ref[rb, c], s) for c, s in zip(cs, s1)]
            s2 = [g * s + _dot_tn(kd_ref[rb, c], v) for g, s, c, v in zip(gb, s1, cs, vnb)]
            for h, s in zip(hs, s2):
                st[h] = s
            oa = [_dot(qd_ref[ra, c], s) for c, s in zip(cs, s0)]
            ob = [_dot(qd_ref[rb, c], s) for c, s in zip(cs, s1)]
            for i_, h in enumerate(hs):
                vn = jnp.concatenate([vna[i_], vnb[i_]], axis=0)
                o_ref[:, cs[i_]] = jnp.concatenate([oa[i_], ob[i_]], axis=0) + _dot(p_ref[:, cs[i_]], vn)
                vn_ref[:, cs[i_]] = vn
                ss_ref[h, 0:HD, :] = s0[i_]
                ss_ref[h, HD:2 * HD, :] = s1[i_]

    big = jax.ShapeDtypeStruct((s, NV * HD), F32)
    return pl.pallas_call(
        body, name="dn_seq", grid=(nb,),
        in_specs=[wide] * 5 + [pl.BlockSpec((1, NQ, 8, 128), lambda i: (i, 0, 0, 0))],
        out_specs=[wide, wide, pl.BlockSpec((NV, 2 * HD, HD), lambda i: (0, i, 0))],
        out_shape=[big, big, jax.ShapeDtypeStruct((NV, 2 * s, HD), F32)],
        scratch_shapes=[pltpu.VMEM((NV, HD, HD), F32)],
        compiler_params=_params(("arbitrary",)),
    )(u, w, qd, kd, p, gam)


def _dn_post(o, pz, nw):
    s = o.shape[0]
    tm = _pick(s, (256, 128))

    def body(o_ref, z_ref, nw_ref, y_ref, yt_ref):
        for h in range(NV):
            cs = slice(h * HD, (h + 1) * HD)
            oh = o_ref[:, cs]
            z = z_ref[:, cs]
            rs = lax.rsqrt(jnp.mean(oh * oh, axis=-1, keepdims=True) + RMS_EPS)
            y = oh * rs * nw_ref[...] * (z * _sigmoid(z))
            y_ref[:, cs] = y.astype(MXU_DT)
            yt_ref[cs, :] = y.T.astype(MXU_DT)

    return pl.pallas_call(
        body, name="dn_post", grid=(s // tm,),
        in_specs=[_rows(tm, NV * HD), _rows(tm, NV * HD), _whole((1, HD))],
        out_specs=[_rows(tm, NV * HD), _cols(tm, NV * HD)],
        out_shape=[jax.ShapeDtypeStruct((s, NV * HD), MXU_DT), jax.ShapeDtypeStruct((NV * HD, s), MXU_DT)],
        compiler_params=_params(("parallel",)),
    )(o, pz, nw)


def _merge(pg, ya, yb):
    s = ya.shape[0]
    tm = _pick(s, (512, 256, 128))

    def body(ga_ref, gb_ref, ya_ref, yb_ref, o_ref, ot_ref):
        m = _sigmoid(ga_ref[...]) * ya_ref[...] + _sigmoid(gb_ref[...]) * yb_ref[...]
        o_ref[...] = m.astype(MXU_DT)
        ot_ref[...] = m.T.astype(MXU_DT)

    return pl.pallas_call(
        body, name="merge", grid=(s // tm,),
        in_specs=[_rows(tm, D, 0), _rows(tm, D, 1), _rows(tm, D), _rows(tm, D)],
        out_specs=[_rows(tm, D), _cols(tm, D)],
        out_shape=[jax.ShapeDtypeStruct((s, D), MXU_DT), jax.ShapeDtypeStruct((D, s), MXU_DT)],
        compiler_params=_params(("parallel",)),
    )(pg, pg, ya, yb)


def _ln1(x, mix, adam, lng, lnb):
    s = x.shape[0]
    tm = _pick(s, (512, 256, 128))

    def body(x_ref, m_ref, ada_ref, g_ref, b_ref, r_ref, x2_ref, h2_ref, h2t_ref):
        r = ALPHA * x_ref[...] + (1.0 + ada_ref[2:3, :]) * m_ref[...]
        xhat, _ = _ln_stats(r)
        x2 = xhat * g_ref[...] + b_ref[...]
        r_ref[...] = r
        x2_ref[...] = x2
        h2 = x2 * (1.0 + ada_ref[4:5, :]) + ada_ref[3:4, :]
        h2_ref[...] = h2.astype(MXU_DT)
        h2t_ref[...] = h2.T.astype(MXU_DT)

    return pl.pallas_call(
        body, name="ln1", grid=(s // tm,),
        in_specs=[_rows(tm, D), _rows(tm, D), _whole((8, D)), _whole((1, D)), _whole((1, D))],
        out_specs=[_rows(tm, D)] * 3 + [_cols(tm, D)],
        out_shape=[jax.ShapeDtypeStruct((s, D), F32), jax.ShapeDtypeStruct((s, D), F32),
                   jax.ShapeDtypeStruct((s, D), MXU_DT), jax.ShapeDtypeStruct((D, s), MXU_DT)],
        compiler_params=_params(("parallel",)),
    )(x, mix, adam, lng, lnb)


def _ffn_act(gu, cw, cb):
    s = gu.shape[0]
    tm = _pick(s, (256, 128))
    w = D_FF

    def body(g_ref, u_ref, cw_ref, cb_ref, o_ref, ot_ref, gc_ref, prev):
        @pl.when(pl.program_id(0) == 0)
        def _():
            prev[...] = jnp.zeros_like(prev)
        g = g_ref[...]
        gc = _conv_causal(g, prev[...], cw_ref, 3) + cb_ref[...]
        prev[...] = g[tm - 8:, :]
        gc_ref[...] = gc
        act = _gelu(gc) * u_ref[...]
        o_ref[...] = act.astype(MXU_DT)
        ot_ref[...] = act.T.astype(MXU_DT)

    return pl.pallas_call(
        body, name="ffn_act", grid=(s // tm,),
        in_specs=[_rows(tm, w, 0), _rows(tm, w, 1), _whole((8, w)), _whole((1, w))],
        out_specs=[_rows(tm, w), _cols(tm, w), _rows(tm, w)],
        out_shape=[jax.ShapeDtypeStruct((s, w), MXU_DT), jax.ShapeDtypeStruct((w, s), MXU_DT),
                   jax.ShapeDtypeStruct((s, w), F32)],
        scratch_shapes=[pltpu.VMEM((8, w), F32)],
        compiler_params=_params(("arbitrary",)),
    )(gu, gu, cw, cb)


def _ln2_loss(x2, ff, tgt, adam, lng, lnb):
    s = x2.shape[0]
    tm = _pick(s, (512, 256, 128))

    def body(x_ref, f_ref, t_ref, ada_ref, g_ref, b_ref, dff_ref, dx_ref, red_ref):
        @pl.when(pl.program_id(0) == 0)
        def _():
            red_ref[...] = jnp.zeros_like(red_ref)
        ff_ = f_ref[...]
        r = ALPHA * x_ref[...] + (1.0 + ada_ref[5:6, :]) * ff_
        xhat, rstd = _ln_stats(r)
        err = xhat * g_ref[...] + b_ref[...] - t_ref[...]
        dy = err * (1.0 / D)
        dr = _ln_bwd(dy, xhat, rstd, g_ref[...])
        dff_ref[...] = ((1.0 + ada_ref[5:6, :]) * dr).astype(MXU_DT)
        dx_ref[...] = ALPHA * dr
        red_ref[0:1, :] += _rsum(dy * xhat)
        red_ref[1:2, :] += _rsum(dy)
        red_ref[2:3, :] += _rsum(dr * ff_)
        red_ref[3:4, :] += jnp.sum(_rsum(err * err), axis=1, keepdims=True) * (0.5 / D)

    return pl.pallas_call(
        body, name="ln2_loss", grid=(s // tm,),
        in_specs=[_rows(tm, D)] * 3 + [_whole((8, D)), _whole((1, D)), _whole((1, D))],
        out_specs=[_rows(tm, D), _rows(tm, D), _whole((8, D))],
        out_shape=[jax.ShapeDtypeStruct((s, D), MXU_DT), jax.ShapeDtypeStruct((s, D), F32),
                   jax.ShapeDtypeStruct((8, D), F32)],
        compiler_params=_params(("arbitrary",)),
    )(x2, ff, tgt, adam, lng, lnb)


def _ffn_bwd(dact, gu, gc, cw):
    s = dact.shape[0]
    tm = _pick(s, (256, 128))
    nt = s // tm
    w = D_FF

    def body(da_ref, g_ref, u_ref, gc_ref, cw_ref, o_ref, gcw_ref, gcb_ref, nxt):
        @pl.when(pl.program_id(0) == 0)
        def _():
            nxt[...] = jnp.zeros_like(nxt)
            gcw_ref[...] = jnp.zeros_like(gcw_ref)
            gcb_ref[...] = jnp.zeros_like(gcb_ref)
        gel, dgel = _gelu_and_grad(gc_ref[...])
        da = da_ref[...]
        dgc = da * u_ref[...] * dgel
        o_ref[:, w:] = (da * gel).astype(MXU_DT)
        o_ref[:, :w] = _conv_causal_bwd(dgc, nxt[...], cw_ref, 3, g_ref[...], gcw_ref).astype(MXU_DT)
        nxt[...] = dgc[:8, :]
        gcb_ref[...] += _rsum(dgc)

    return pl.pallas_call(
        body, name="ffn_bwd", grid=(nt,),
        in_specs=[_rows(tm, w, 0, nt), _rows(tm, w, 0, nt), _rows(tm, w, 1, nt), _rows(tm, w, 0, nt),
                  _whole((8, w))],
        out_specs=[_rows(tm, 2 * w, 0, nt), _whole((8, w)), _whole((1, w))],
        out_shape=[jax.ShapeDtypeStruct((s, 2 * w), MXU_DT), jax.ShapeDtypeStruct((8, w), F32),
                   jax.ShapeDtypeStruct((1, w), F32)],
        scratch_shapes=[pltpu.VMEM((8, w), F32)],
        compiler_params=_params(("arbitrary",)),
    )(dact, gu, gu, gc, cw)


def _ln1_bwd(dh2, dx2a, x2, r1, mix, adam, lng):
    s = dh2.shape[0]
    tm = _pick(s, (512, 256, 128))

    def body(dh_ref, dxa_ref, x2_ref, r_ref, m_ref, ada_ref, g_ref, dm_ref, dx_ref, red_ref):
        @pl.when(pl.program_id(0) == 0)
        def _():
            red_ref[...] = jnp.zeros_like(red_ref)
        dh = dh_ref[...]
        dx2 = dxa_ref[...] + dh * (1.0 + ada_ref[4:5, :])
        xhat, rstd = _ln_stats(r_ref[...])
        dr = _ln_bwd(dx2, xhat, rstd, g_ref[...])
        dm_ref[...] = ((1.0 + ada_ref[2:3, :]) * dr).astype(MXU_DT)
        dx_ref[...] = ALPHA * dr
        red_ref[0:1, :] += _rsum(dh * x2_ref[...])
        red_ref[1:2, :] += _rsum(dh)
        red_ref[2:3, :] += _rsum(dx2 * xhat)
        red_ref[3:4, :] += _rsum(dx2)
        red_ref[4:5, :] += _rsum(dr * m_ref[...])

    return pl.pallas_call(
        body, name="ln1_bwd", grid=(s // tm,),
        in_specs=[_rows(tm, D)] * 5 + [_whole((8, D)), _whole((1, D))],
        out_specs=[_rows(tm, D), _rows(tm, D), _whole((8, D))],
        out_shape=[jax.ShapeDtypeStruct((s, D), MXU_DT), jax.ShapeDtypeStruct((s, D), F32),
                   jax.ShapeDtypeStruct((8, D), F32)],
        compiler_params=_params(("arbitrary",)),
    )(dh2, dx2a, x2, r1, mix, adam, lng)


def _merge_bwd(dmg, pg, ya, yb):
    s = dmg.shape[0]
    tm = _pick(s, (512, 256, 128))

    def body(d_ref, ga_ref, gb_ref, ya_ref, yb_ref, dya_ref, dyb_ref, dpg_ref):
        d = d_ref[...]
        sa = _sigmoid(ga_ref[...])
        sb = _sigmoid(gb_ref[...])
        dya_ref[...] = (d * sa).astype(MXU_DT)
        dyb_ref[...] = (d * sb).astype(MXU_DT)
        dpg_ref[:, :D] = (d * ya_ref[...] * sa * (1.0 - sa)).astype(MXU_DT)
        dpg_ref[:, D:] = (d * yb_ref[...] * sb * (1.0 - sb)).astype(MXU_DT)

    return pl.pallas_call(
        body, name="merge_bwd", grid=(s // tm,),
        in_specs=[_rows(tm, D), _rows(tm, D, 0), _rows(tm, D, 1), _rows(tm, D), _rows(tm, D)],
        out_specs=[_rows(tm, D), _rows(tm, D), _rows(tm, 2 * D)],
        out_shape=[jax.ShapeDtypeStruct((s, D), MXU_DT), jax.ShapeDtypeStruct((s, D), MXU_DT),
                   jax.ShapeDtypeStruct((s, 2 * D), MXU_DT)],
        compiler_params=_params(("parallel",)),
    )(dmg, pg, pg, ya, yb)


def _dn_post_bwd(ddn, o, pz, nw):
    s = o.shape[0]
    tm = _pick(s, (256, 128))

    def body(d_ref, o_ref, z_ref, nw_ref, do_ref, dz_ref, gnw_ref):
        @pl.when(pl.program_id(0) == 0)
        def _():
            gnw_ref[...] = jnp.zeros_like(gnw_ref)
        acc = jnp.zeros((1, HD), F32)
        for h in range(NV):
            cs = slice(h * HD, (h + 1) * HD)
            oh = o_ref[:, cs]
            z = z_ref[:, cs]
            d = d_ref[:, cs]
            sg = _sigmoid(z)
            rs = lax.rsqrt(jnp.mean(oh * oh, axis=-1, keepdims=True) + RMS_EPS)
            n = oh * rs
            dz_ref[:, cs] = (d * n * nw_ref[...] * sg * (1.0 + z * (1.0 - sg))).astype(MXU_DT)
            dn_ = d * (z * sg)
            acc = acc + _rsum(dn_ * n)
            dnn = dn_ * nw_ref[...]
            do_ref[:, cs] = rs * (dnn - n * jnp.mean(dnn * n, axis=-1, keepdims=True))
        gnw_ref[...] += acc

    return pl.pallas_call(
        body, name="dn_post_bwd", grid=(s // tm,),
        in_specs=[_rows(tm, NV * HD)] * 3 + [_whole((1, HD))],
        out_specs=[_rows(tm, NV * HD), _rows(tm, NV * HD), _whole((1, HD))],
        out_shape=[jax.ShapeDtypeStruct((s, NV * HD), F32), jax.ShapeDtypeStruct((s, NV * HD), MXU_DT),
                   jax.ShapeDtypeStruct((1, HD), F32)],
        compiler_params=_params(("arbitrary",)),
    )(ddn, o, pz, nw)


def _dn_seq_bwd(do, qd, kd, p, w, vn, ssave, gam):
    s = do.shape[0]
    nb = s // (2 * CH)
    wide = pl.BlockSpec((2 * CH, NV * HD), lambda i: (nb - 1 - i, 0))
    gspec = pl.BlockSpec((1, NQ, 8, 128), lambda i: (nb - 1 - i, 0, 0, 0))

    def body(do_ref, qd_ref, kd_ref, p_ref, w_ref, vn_ref, ss_ref, gam_ref, dvn_ref, dkd_ref, dgam_ref, dst):
        @pl.when(pl.program_id(0) == 0)
        def _():
            dst[...] = jnp.zeros_like(dst)
        dgam_ref[...] = jnp.zeros_like(dgam_ref)
        ra, rb = slice(0, CH), slice(CH, 2 * CH)
        tot = lambda t: jnp.sum(jnp.sum(t, axis=1, keepdims=True), axis=0, keepdims=True)
        for g0 in range(0, NV, SEQ_GROUP):
            hs = list(range(g0, g0 + SEQ_GROUP))
            cs = [slice(h * HD, (h + 1) * HD) for h in hs]
            ga = [gam_ref[0, h // 2, 2 * (h % 2):2 * (h % 2) + 1, :] for h in hs]
            gb = [gam_ref[0, h // 2, 2 * (h % 2) + 1:2 * (h % 2) + 2, :] for h in hs]
            ds2 = [dst[h] for h in hs]
            pdo = [_dot_tn(p_ref[:, c], do_ref[:, c]) for c in cs]
            qdo_b = [_dot_tn(qd_ref[rb, c], do_ref[rb, c]) for c in cs]
            qdo_a = [_dot_tn(qd_ref[ra, c], do_ref[ra, c]) for c in cs]
            dvb = [p_[rb] + _dot(kd_ref[rb, c], d_) for p_, c, d_ in zip(pdo, cs, ds2)]
            ds1 = [g * d_ + q_ - _dot_tn(w_ref[rb, c], v_)
                   for g, d_, q_, c, v_ in zip(gb, ds2, qdo_b, cs, dvb)]
            dva = [p_[ra] + _dot(kd_ref[ra, c], d_) for p_, c, d_ in zip(pdo, cs, ds1)]
            ds0 = [g * d_ + q_ - _dot_tn(w_ref[ra, c], v_)
                   for g, d_, q_, c, v_ in zip(ga, ds1, qdo_a, cs, dva)]
            for h, d_ in zip(hs, ds0):
                dst[h] = d_
            for i_, h in enumerate(hs):
                c = cs[i_]
                row = 2 * (h % 2)
                dkd_ref[rb, c] = _dot_nt(vn_ref[rb, c], ds2[i_])
                dkd_ref[ra, c] = _dot_nt(vn_ref[ra, c], ds1[i_])
                dvn_ref[ra, c] = dva[i_]
                dvn_ref[rb, c] = dvb[i_]
                dgam_ref[0, h // 2, row:row + 1, :] = jnp.broadcast_to(tot(ds1[i_] * ss_ref[h, 0:HD, :]), (1, 128))
                dgam_ref[0, h // 2, row + 1:row + 2, :] = jnp.broadcast_to(
                    tot(ds2[i_] * ss_ref[h, HD:2 * HD, :]), (1, 128))

    big = jax.ShapeDtypeStruct((s, NV * HD), F32)
    return pl.pallas_call(
        body, name="dn_seq_bwd", grid=(nb,),
        in_specs=[wide] * 6 + [pl.BlockSpec((NV, 2 * HD, HD), lambda i: (0, nb - 1 - i, 0)), gspec],
        out_specs=[wide, wide, gspec],
        out_shape=[big, big, jax.ShapeDtypeStruct((nb, NQ, 8, 128), F32)],
        scratch_shapes=[pltpu.VMEM((NV, HD, HD), F32)],
        compiler_params=_params(("arbitrary",)),
    )(do, qd, kd, p, w, vn, ssave, gam)


def _dn_intra_bwd(q, k, v, rt, do, dvn, dkd, gam, dgam, ssave, tinv, u, w, vn):
    s = q.shape[0]
    nb = s // (2 * CH)
    qps = 4
    nh = 2 * qps
    blk = pl.BlockSpec((2 * CH, qps * HD), lambda i, h: (i, h))
    blk2 = pl.BlockSpec((2 * CH, nh * HD), lambda i, h: (i, h))
    gspec = pl.BlockSpec((1, qps, 8, 128), lambda i, h: (i, h, 0, 0))
    rspec = pl.BlockSpec((3 * NV, 2 * CH), lambda i, h: (0, i))

    def body(q_ref, k_ref, v_ref, rt_ref, do_ref, dvn_ref, dkd_ref, gam_ref, dgam_ref, ss_ref,
             ti_ref, u_ref, w_ref, vn_ref, dq_ref, dk_ref, dv_ref, drt_ref, acc):
        hstep = pl.program_id(1)

        @pl.when(hstep == 0)
        def _():
            acc[...] = jnp.zeros_like(acc)
        mc, ms = _pair_masks()
        ra, rb = slice(0, CH), slice(CH, 2 * CH)
        lane = _iota((1, 2 * CH), 1)
        hs = list(range(nh))
        qh = [h // 2 for h in hs]
        cs = [slice(h * HD, (h + 1) * HD) for h in hs]
        qq_ = [q_ref[:, t * HD:(t + 1) * HD] for t in range(qps)]
        kk_ = [k_ref[:, t * HD:(t + 1) * HD] for t in range(qps)]
        kk = [_dot_nt(k_, k_) for k_ in kk_]
        qk = [_dot_nt(q_, k_) for q_, k_ in zip(qq_, kk_)]
        cols = [_head_cols(rt_ref, nh * hstep + h) for h in hs]
        b_c = [c_[2] for c_ in cols]
        dec = [jnp.where(mc, jnp.exp(jnp.where(mc, c_[1] - c_[0], 0.0)), 0.0) for c_ in cols]
        eg = [jnp.exp(c_[1]) for c_ in cols]
        egl = [jnp.exp(c_[4] - c_[1]) for c_ in cols]
        dob = [do_ref[:, c] for c in cs]
        dvb = [dvn_ref[:, c] for c in cs]
        dqd = [jnp.concatenate([_dot_nt(d_[ra], ss_ref[h, 0:HD, :]), _dot_nt(d_[rb], ss_ref[h, HD:2 * HD, :])], axis=0)
               for h, d_ in zip(hs, dob)]
        dw = [-jnp.concatenate([_dot_nt(d_[ra], ss_ref[h, 0:HD, :]), _dot_nt(d_[rb], ss_ref[h, HD:2 * HD, :])], axis=0)
              for h, d_ in zip(hs, dvb)]
        dp = [jnp.where(mc, _dot_nt(d_, vn_ref[:, c]), 0.0) for d_, c in zip(dob, cs)]
        dbuw = [_dot3_tn(ti_ref[:, c], jnp.concatenate([d_, w_], axis=1)) for c, d_, w_ in zip(cs, dvb, dw)]
        dbu = [t[:, :HD] for t in dbuw]
        dbw = [t[:, HD:] for t in dbuw]
        da = [jnp.where(ms, -(_dot_nt(bu, u_ref[:, c]) + _dot_nt(bw, w_ref[:, c])), 0.0)
              for bu, bw, c in zip(dbu, dbw, cs)]
        dm = [a_ * d_ for a_, d_ in zip(da, dec)]
        dn_ = [p_ * d_ for p_, d_ in zip(dp, dec)]
        dbk = [_dot(m_, kk_[t]) for m_, t in zip(dm, qh)]
        dqs = [_dot(n_, kk_[t]) + e_ * q_ for n_, t, e_, q_ in zip(dn_, qh, eg, dqd)]
        dks = [_dot_tn(m_, b_ * kk_[t]) + _dot_tn(n_, qq_[t]) + el * dkd_ref[:, c] + b_ * (e_ * bw + bk)
               for m_, b_, t, n_, el, c, e_, bw, bk in zip(dm, b_c, qh, dn_, egl, cs, eg, dbw, dbk)]
        for t in range(qps):
            dq_ref[:, t * HD:(t + 1) * HD] = dqs[2 * t] + dqs[2 * t + 1]
            dk_ref[:, t * HD:(t + 1) * HD] = dks[2 * t] + dks[2 * t + 1]
        for h in hs:
            c, t, j = cs[h], qh[h], h % 2
            dv_ref[:, c] = b_c[h] * dbu[h]
            e = da[h] * (b_c[h] * kk[t] * dec[h]) + dp[h] * (qk[t] * dec[h])
            x = dkd_ref[:, c] * (egl[h] * kk_[t])
            egk = eg[h] * kk_[t]
            z = e + dqd[h] * (eg[h] * qq_[t]) - x + dbw[h] * (b_c[h] * egk)
            zb = dbw[h] * egk + dbu[h] * v_ref[:, c] + dbk[h] * kk_[t]
            sa = jnp.sum(jnp.sum(x[ra], axis=1, keepdims=True), axis=0, keepdims=True)
            sb = jnp.sum(jnp.sum(x[rb], axis=1, keepdims=True), axis=0, keepdims=True)
            la = sa + dgam_ref[0, t, 2 * j:2 * j + 1, :] * gam_ref[0, t, 2 * j:2 * j + 1, :]
            lb = sb + dgam_ref[0, t, 2 * j + 1:2 * j + 2, :] * gam_ref[0, t, 2 * j + 1:2 * j + 2, :]
            hg = nh * hstep + h
            acc[pl.ds(hg, 1), :] = _rsum(z.T - e)
            acc[pl.ds(NV + hg, 1), :] = _rsum(zb.T)
            acc[pl.ds(2 * NV + hg, 1), :] = jnp.where(lane < CH, la, lb)

        @pl.when(hstep == NQ // qps - 1)
        def _():
            drt_ref[...] = acc[...]

    return pl.pallas_call(
        body, name="dn_intra_bwd", grid=(nb, NQ // qps),
        in_specs=[blk, blk, blk2, rspec, blk2, blk2, blk2, gspec, gspec,
                  pl.BlockSpec((nh, 2 * HD, HD), lambda i, h: (h, i, 0)), blk2, blk2, blk2, blk2],
        out_specs=[blk, blk, blk2, rspec],
        out_shape=[jax.ShapeDtypeStruct((s, NQ * HD), F32), jax.ShapeDtypeStruct((s, NQ * HD), F32),
                   jax.ShapeDtypeStruct((s, NV * HD), F32), jax.ShapeDtypeStruct((3 * NV, s), F32)],
        scratch_shapes=[pltpu.VMEM((3 * NV, 2 * CH), F32)],
        compiler_params=_params(("parallel", "arbitrary")),
    )(q, k, v, rt, do, dvn, dkd, gam, dgam, ssave, tinv, u, w, vn)


def _l2n_heads_bwd(c, d_ref, dc_ref, off, scale):
    for hh in range(NQ):
        cs = slice(off + hh * HD, off + (hh + 1) * HD)
        x = c[:, cs]
        dy = d_ref[:, hh * HD:(hh + 1) * HD]
        r = lax.rsqrt(jnp.sum(x * x, axis=-1, keepdims=True) + L2_EPS)
        dc_ref[:, cs] = (scale * r) * (dy - x * (r * r) * jnp.sum(dy * x, axis=-1, keepdims=True))


def _dn_prep_bwd(dq, dk, dv, drt, pqkv, cpre, pab, cw, acol, dcol):
    s = pqkv.shape[0]
    tm = 128
    nt = s // tm
    wq = NQ * HD

    def body(dq_ref, dk_ref, dv_ref, drt_ref, x_ref, cp_ref, ab_ref, cw_ref, ac_ref,
             dc_ref, dx_ref, dab_ref, gcw_ref, gsc_ref, dcs, nxt):
        @pl.when(pl.program_id(0) == 0)
        def _():
            nxt[...] = jnp.zeros_like(nxt)
            gcw_ref[...] = jnp.zeros_like(gcw_ref)
            gsc_ref[...] = jnp.zeros_like(gsc_ref)
        cp = cp_ref[...]
        sg = _sigmoid(cp)
        c = cp * sg
        _l2n_heads_bwd(c, dq_ref, dcs, 0, Q_SCALE)
        _l2n_heads_bwd(c, dk_ref, dcs, wq, 1.0)
        dcs[:, 2 * wq:] = dv_ref[...]
        dcp = dcs[...] * (sg * (1.0 + cp * (1.0 - sg)))
        dx_ref[...] = _conv_causal_bwd(dcp, nxt[...], cw_ref, 4, x_ref[...], gcw_ref).astype(MXU_DT)
        nxt[...] = dcp[:8, :]
        lane = _iota((NV, tm), 1)
        dgt = drt_ref[0:NV, :] + jnp.where((lane & (CH - 1)) == CH - 1, drt_ref[2 * NV:3 * NV, :], 0.0)
        dg = _chunk_cumsum(dgt, 1, rev=True)
        abt = ab_ref[...].T
        zt = abt[0:NV, :] + dc_ref[...]
        gt = -jnp.exp(ac_ref[...]) * _softplus(zt)
        dat = dg * (-jnp.exp(ac_ref[...])) * _sigmoid(zt)
        bt = _sigmoid(abt[NV:2 * NV, :])
        dbt = drt_ref[NV:2 * NV, :] * bt * (1.0 - bt)
        full = jnp.concatenate([dat, dbt, jnp.zeros((128 - 2 * NV, tm), F32)], axis=0)
        dab_ref[...] = full.T.astype(MXU_DT)
        l2 = _iota((NV, 128), 1)
        gsc_ref[...] += jnp.where(l2 == 0, jnp.sum(dg * gt, axis=1, keepdims=True),
                                  jnp.where(l2 == 1, jnp.sum(dat, axis=1, keepdims=True), 0.0))

    return pl.pallas_call(
        body, name="dn_prep_bwd", grid=(nt,),
        in_specs=[_rows(tm, wq, 0, nt), _rows(tm, wq, 0, nt), _rows(tm, 2 * wq, 0, nt),
                  pl.BlockSpec((3 * NV, tm), lambda i: (0, nt - 1 - i)),
                  _rows(tm, 4 * wq, 0, nt), _rows(tm, 4 * wq, 0, nt), _rows(tm, 128, 0, nt),
                  _whole((8, 4 * wq)), _whole((NV, 1)), _whole((NV, 1))],
        out_specs=[_rows(tm, 4 * wq, 0, nt), _rows(tm, 128, 0, nt), _whole((8, 4 * wq)), _whole((NV, 128))],
        out_shape=[jax.ShapeDtypeStruct((s, 4 * wq), MXU_DT), jax.ShapeDtypeStruct((s, 128), MXU_DT),
                   jax.ShapeDtypeStruct((8, 4 * wq), F32), jax.ShapeDtypeStruct((NV, 128), F32)],
        scratch_shapes=[pltpu.VMEM((tm, 4 * wq), F32), pltpu.VMEM((8, 4 * wq), F32)],
        compiler_params=_params(("arbitrary",)),
    )(dq, dk, dv, drt, pqkv, cpre, pab, cw, acol, dcol)


def _rg_bwd(drec, px, xcs, h, cw, vec, wa, wx, wat, wxt, side=None):
    s = px.shape[0]
    tm = _pick(s, (256, 128))
    nt = s // tm
    w = D_RNN
    n_in = len(side.ins) if side else 0
    n_out = len(side.out_shapes) if side else 0

    def body(*refs):
        (dr_ref, xr_ref, xc_ref, gr_ref, h_ref, hb_ref, cw_ref, vec_ref, wa_ref, wx_ref, wat_ref,
         wxt_ref) = refs[:12]
        s_in = refs[12:12 + n_in]
        o_ref, gwa_ref, gwx_ref, gcw_ref, gvec_ref = refs[12 + n_in:17 + n_in]
        s_out = refs[17 + n_in:17 + n_in + n_out]
        nxt_a, nxt_l, nxt_d = refs[17 + n_in + n_out:20 + n_in + n_out]
        sems = refs[20 + n_in + n_out:]
        i = pl.program_id(0)
        if side:
            @pl.when(i == 0)
            def _():
                side.start(s_in, s_out, sems)

        @pl.when(i == 0)
        def _():
            nxt_a[...] = jnp.zeros_like(nxt_a)
            nxt_l[...] = jnp.zeros_like(nxt_l)
            nxt_d[...] = jnp.zeros_like(nxt_d)
            gwa_ref[...] = jnp.zeros_like(gwa_ref)
            gwx_ref[...] = jnp.zeros_like(gwx_ref)
            gcw_ref[...] = jnp.zeros_like(gcw_ref)
            gvec_ref[...] = jnp.zeros_like(gvec_ref)
        hbefore = jnp.where(i == nt - 1, 0.0, hb_ref[...])
        xc = xc_ref[...]
        r, ig, a, mult, sp, rmult = _rg_gates(xc, wa_ref, wx_ref, vec_ref)
        hh = h_ref[...]
        gel, dgel = _gelu_and_grad(gr_ref[...])
        drec_ = dr_ref[...]
        o_ref[:, w:] = (drec_ * hh * dgel).astype(MXU_DT)
        acum, lam = _scan_rev(_shift_up(a, nxt_a[...], 1), drec_ * gel)
        lam = lam + acum * nxt_l[0:1, :]
        nxt_a[...] = a[:8, :]
        nxt_l[...] = lam[:8, :]
        da = lam * _shift_down(hh, hbefore, 1)
        dxc = lam * mult * ig
        dla = da * a - (lam * ig * xc) * (a * a) * rmult
        dpr = dla * (-RG_C * sp) * r * (1.0 - r)
        dpi = (lam * mult * xc) * ig * (1.0 - ig)
        dprb = dpr.astype(MXU_DT)
        dpib = dpi.astype(MXU_DT)
        dxc = dxc + jnp.dot(dprb, wat_ref[...], preferred_element_type=F32) \
                  + jnp.dot(dpib, wxt_ref[...], preferred_element_type=F32)
        xcb = xc.astype(MXU_DT)
        gwa_ref[...] += _dot_tn(xcb, dprb)
        gwx_ref[...] += _dot_tn(xcb, dpib)
        o_ref[:, :w] = _conv_causal_bwd(dxc, nxt_d[...], cw_ref, 4, xr_ref[...], gcw_ref).astype(MXU_DT)
        nxt_d[...] = dxc[:8, :]
        gvec_ref[0:1, :] += _rsum(dxc)
        gvec_ref[1:2, :] += _rsum(dpr)
        gvec_ref[2:3, :] += _rsum(dpi)
        gvec_ref[3:4, :] += _rsum(dla * (-RG_C * r)) * (-_sigmoid(-vec_ref[3:4, :]))
        if side:
            @pl.when(i == nt - 1)
            def _():
                side.finish(s_in, s_out, sems)

    return pl.pallas_call(
        body, name="rg_bwd", grid=(nt,),
        in_specs=[_rows(tm, w, 0, nt), _rows(tm, w, 0, nt), _rows(tm, w, 0, nt), _rows(tm, w, 1, nt),
                  _rows(tm, w, 0, nt), _before(tm, w, 0, nt), _whole((8, w)), _whole((8, w)),
                  _whole((w, w)), _whole((w, w)), _whole((w, w)), _whole((w, w))] + [_HBM] * n_in,
        out_specs=[_rows(tm, 2 * w, 0, nt), _whole((w, w)), _whole((w, w)), _whole((8, w)), _whole((8, w))]
        + [_HBM] * n_out,
        out_shape=[jax.ShapeDtypeStruct((s, 2 * w), MXU_DT), jax.ShapeDtypeStruct((w, w), F32),
                   jax.ShapeDtypeStruct((w, w), F32), jax.ShapeDtypeStruct((8, w), F32),
                   jax.ShapeDtypeStruct((8, w), F32)] + (list(side.out_shapes) if side else []),
        scratch_shapes=[pltpu.VMEM((8, w), F32)] * 3 + (list(side.sems) if side else []),
        compiler_params=_params(("arbitrary",)),
    )(drec, px, xcs, px, h, h, cw, vec, wa, wx, wat, wxt, *(side.ins if side else []))


def _modulate_bwd(dh1, dxa, x, adam):
    s = x.shape[0]
    tm = _pick(s, (512, 256, 128))

    def body(dh_ref, dxa_ref, x_ref, ada_ref, gx_ref, red_ref):
        @pl.when(pl.program_id(0) == 0)
        def _():
            red_ref[...] = jnp.zeros_like(red_ref)
        dh = dh_ref[...]
        gx_ref[...] = dxa_ref[...] + dh * (1.0 + ada_ref[1:2, :])
        red_ref[0:1, :] += _rsum(dh * x_ref[...])
        red_ref[1:2, :] += _rsum(dh)

    return pl.pallas_call(
        body, name="modulate1_bwd", grid=(s // tm,),
        in_specs=[_rows(tm, D)] * 3 + [_whole((8, D))],
        out_specs=[_rows(tm, D), _whole((8, D))],
        out_shape=[jax.ShapeDtypeStruct((s, D), F32), jax.ShapeDtypeStruct((8, D), F32)],
        compiler_params=_params(("arbitrary",)),
    )(dh1, dxa, x, adam)


def _mm_epi(a, b, *, name, epi, extras=(), wholes=(), row_outs=(), acc_outs=(), tm=None, side=None):
    m, kk = a.shape
    n = b.shape[1]
    tm = tm or _pick(m, (512, 256, 128))
    tk = kk if kk <= 2816 else _pick(kk, (2816, 2176, 2048, 1024))
    nk, ni = kk // tk, m // tm
    n_ex, n_wh, n_ro, n_ao = len(extras), len(wholes), len(row_outs), len(acc_outs)
    n_si = len(side.ins) if side else 0
    n_so = len(side.out_shapes) if side else 0

    def body(*refs):
        a_ref, b_ref = refs[:2]
        p = 2
        ex, p = refs[p:p + n_ex], p + n_ex
        wh, p = refs[p:p + n_wh], p + n_wh
        s_in, p = refs[p:p + n_si], p + n_si
        ro, p = refs[p:p + n_ro], p + n_ro
        ao, p = refs[p:p + n_ao], p + n_ao
        s_out, p = refs[p:p + n_so], p + n_so
        i, k = pl.program_id(0), pl.program_id(1)
        first = (i == 0) & (k == 0)
        if side:
            sems = refs[p + (1 if nk > 1 else 0):]

            @pl.when(first)
            def _():
                side.start(s_in, s_out, sems)
        if n_ao:
            @pl.when(first)
            def _():
                for r_ in ao:
                    r_[...] = jnp.zeros_like(r_)
        if nk == 1:
            epi(_dot(a_ref[...], b_ref[...]), ex, wh, ro, ao)
        else:
            acc = refs[p]

            @pl.when(k == 0)
            def _():
                acc[...] = jnp.zeros_like(acc)
            acc[...] += _dot(a_ref[...], b_ref[...])

            @pl.when(k == nk - 1)
            def _():
                epi(acc[...], ex, wh, ro, ao)
        if side:
            @pl.when((i == ni - 1) & (k == nk - 1))
            def _():
                side.finish(s_in, s_out, sems)

    in_specs = [pl.BlockSpec((tm, tk), lambda i, k: (i, k)), pl.BlockSpec((tk, n), lambda i, k: (k, 0))]
    in_specs += [pl.BlockSpec((tm, w_), functools.partial(lambda i, k, c_: (i, c_), c_=c_)) for _, w_, c_ in extras]
    in_specs += [_whole(x_.shape) for x_ in wholes] + [_HBM] * n_si
    out_specs = [pl.BlockSpec((tm, w_), lambda i, k: (i, 0)) for w_, _ in row_outs]
    out_specs += [_whole(sh) for sh in acc_outs] + [_HBM] * n_so
    out_shape = [jax.ShapeDtypeStruct((m, w_), dt) for w_, dt in row_outs]
    out_shape += [jax.ShapeDtypeStruct(sh, F32) for sh in acc_outs] + (list(side.out_shapes) if side else [])
    scratch = ([pltpu.VMEM((tm, n), F32)] if nk > 1 else []) + (list(side.sems) if side else [])
    return pl.pallas_call(
        body, name=name, grid=(ni, nk), in_specs=in_specs, out_specs=out_specs, out_shape=out_shape,
        scratch_shapes=scratch, compiler_params=_params(("arbitrary", "arbitrary")),
    )(a, b, *[x_ for x_, _, _ in extras], *wholes, *(side.ins if side else []))


def _ffn_down_ln2_loss(act, w_down, x2, tgt, adam, lng, lnb):
    def epi(ff_, ex, wh, ro, ao):
        x_ref, t_ref = ex
        ada_ref, g_ref, b_ref = wh
        dff_ref, dx_ref = ro
        red_ref, = ao
        r = ALPHA * x_ref[...] + (1.0 + ada_ref[5:6, :]) * ff_
        xhat, rstd = _ln_stats(r)
        err = xhat * g_ref[...] + b_ref[...] - t_ref[...]
        dy = err * (1.0 / D)
        dr = _ln_bwd(dy, xhat, rstd, g_ref[...])
        dff_ref[...] = ((1.0 + ada_ref[5:6, :]) * dr).astype(MXU_DT)
        dx_ref[...] = ALPHA * dr
        red_ref[0:1, :] += _rsum(dy * xhat)
        red_ref[1:2, :] += _rsum(dy)
        red_ref[2:3, :] += _rsum(dr * ff_)
        red_ref[3:4, :] += jnp.sum(_rsum(err * err), axis=1, keepdims=True) * (0.5 / D)

    return _mm_epi(act, w_down, name="ffn_down_ln2_loss", epi=epi, extras=[(x2, D, 0), (tgt, D, 0)],
                   wholes=[adam, lng, lnb], row_outs=[(D, MXU_DT), (D, F32)], acc_outs=[(8, D)])


def _d_h1_modulate_bwd(dproj, w_t, dxa, x, adam, side):
    def epi(dh, ex, wh, ro, ao):
        dxa_ref, x_ref = ex
        ada_ref, = wh
        ro[0][...] = dxa_ref[...] + dh * (1.0 + ada_ref[1:2, :])
        ao[0][0:1, :] += _rsum(dh * x_ref[...])
        ao[0][1:2, :] += _rsum(dh)

    return _mm_epi(dproj, w_t, name="d_h1", epi=epi, extras=[(dxa, D, 0), (x, D, 0)], wholes=[adam],
                   row_outs=[(D, F32)], acc_outs=[(8, D)], side=side)


def _d_merged_bwd(dmix, w_out_t, pg, ya, yb):
    def epi(d, ex, wh, ro, ao):
        ga_ref, gb_ref, ya_ref, yb_ref = ex
        dya_ref, dyb_ref, dpg_ref = ro
        sa = _sigmoid(ga_ref[...])
        sb = _sigmoid(gb_ref[...])
        dya_ref[...] = (d * sa).astype(MXU_DT)
        dyb_ref[...] = (d * sb).astype(MXU_DT)
        dpg_ref[:, :D] = (d * ya_ref[...] * sa * (1.0 - sa)).astype(MXU_DT)
        dpg_ref[:, D:] = (d * yb_ref[...] * sb * (1.0 - sb)).astype(MXU_DT)

    return _mm_epi(dmix, w_out_t, name="d_merged", epi=epi,
                   extras=[(pg, D, 0), (pg, D, 1), (ya, D, 0), (yb, D, 0)],
                   row_outs=[(D, MXU_DT), (D, MXU_DT), (2 * D, MXU_DT)])


def _d_dn_post_bwd(dyb, w_pb_t, o, pz, nw):
    def epi(d_all, ex, wh, ro, ao):
        o_ref, z_ref = ex
        nw_ref, = wh
        do_ref, dz_ref = ro
        acc = jnp.zeros((1, HD), F32)
        for h in range(NV):
            cs = slice(h * HD, (h + 1) * HD)
            oh = o_ref[:, cs]
            z = z_ref[:, cs]
            d = d_all[:, cs]
            sg = _sigmoid(z)
            rs = lax.rsqrt(jnp.mean(oh * oh, axis=-1, keepdims=True) + RMS_EPS)
            n = oh * rs
            dz_ref[:, cs] = (d * n * nw_ref[...] * sg * (1.0 + z * (1.0 - sg))).astype(MXU_DT)
            dn_ = d * (z * sg)
            acc = acc + _rsum(dn_ * n)
            dnn = dn_ * nw_ref[...]
            do_ref[:, cs] = rs * (dnn - n * jnp.mean(dnn * n, axis=-1, keepdims=True))
        ao[0][...] += acc

    return _mm_epi(dyb, w_pb_t, name="d_dn", epi=epi, extras=[(o, NV * HD, 0), (pz, NV * HD, 0)], wholes=[nw],
                   row_outs=[(NV * HD, F32), (NV * HD, MXU_DT)], acc_outs=[(1, HD)], tm=256)


def _d_h2_ln1_bwd(dgu, w_gu_t, dx2a, x2, r1, mix, adam, lng):
    def epi(dh, ex, wh, ro, ao):
        dxa_ref, x2_ref, r_ref, m_ref = ex
        ada_ref, g_ref = wh
        dm_ref, dx_ref = ro
        red_ref, = ao
        dx2 = dxa_ref[...] + dh * (1.0 + ada_ref[4:5, :])
        xhat, rstd = _ln_stats(r_ref[...])
        dr = _ln_bwd(dx2, xhat, rstd, g_ref[...])
        dm_ref[...] = ((1.0 + ada_ref[2:3, :]) * dr).astype(MXU_DT)
        dx_ref[...] = ALPHA * dr
        red_ref[0:1, :] += _rsum(dh * x2_ref[...])
        red_ref[1:2, :] += _rsum(dh)
        red_ref[2:3, :] += _rsum(dx2 * xhat)
        red_ref[3:4, :] += _rsum(dx2)
        red_ref[4:5, :] += _rsum(dr * m_ref[...])

    return _mm_epi(dgu, w_gu_t, name="d_h2", epi=epi,
                   extras=[(dx2a, D, 0), (x2, D, 0), (r1, D, 0), (mix, D, 0)], wholes=[adam, lng],
                   row_outs=[(D, MXU_DT), (D, F32)], acc_outs=[(8, D)], tm=256)


def _adamw(parts, w, m, v, name):
    r, c = w.shape
    n_parts = len(parts)
    tm = _row_tile(r, c * 4 * (n_parts + 7))
    c1 = 1.0 - ADAM_B1 ** ADAM_STEP
    c2 = 1.0 - ADAM_B2 ** ADAM_STEP

    def body(*refs):
        g = refs[0][...]
        for p_ref in refs[1:n_parts]:
            g = g + p_ref[...]
        w_ref, m_ref, v_ref, g_out, d_out, m_out, v_out = refs[n_parts:]
        mn = ADAM_B1 * m_ref[...] + (1.0 - ADAM_B1) * g
        vn = ADAM_B2 * v_ref[...] + (1.0 - ADAM_B2) * (g * g)
        g_out[...] = g
        m_out[...] = mn
        v_out[...] = vn
        d_out[...] = -ADAM_LR * ((mn / c1) / (jnp.sqrt(vn / c2) + ADAM_EPS) + ADAM_WD * w_ref[...])

    spec = pl.BlockSpec((tm, c), lambda i: (i, 0))
    return pl.pallas_call(
        body, name=name, grid=(r // tm,),
        in_specs=[spec] * (n_parts + 3), out_specs=[spec] * 4,
        out_shape=[jax.ShapeDtypeStruct((r, c), F32)] * 4, compiler_params=_params(("parallel",)),
    )(*parts, w, m, v)


def _row_tile(r, bytes_per_row):
    for t in (512, 256, 128, 64, 32, 16):
        if r % t == 0 and 2 * t * bytes_per_row <= 20 * 1024 * 1024:
            return t
    return _pick(r, (16, 8))


def _sum_partials(own, recv, name):
    r, c = own.shape
    tm = _row_tile(r, c * (4 + 4 + 3 * 2))

    def body(o_ref, r_ref, out_ref):
        out_ref[...] = ((o_ref[...] + r_ref[0].astype(F32)) + r_ref[1].astype(F32)) + r_ref[2].astype(F32)

    return pl.pallas_call(
        body, name=name, grid=(r // tm,),
        in_specs=[pl.BlockSpec((tm, c), lambda i: (i, 0)), pl.BlockSpec((3, tm, c), lambda i: (0, i, 0))],
        out_specs=pl.BlockSpec((tm, c), lambda i: (i, 0)),
        out_shape=jax.ShapeDtypeStruct((r, c), F32), compiler_params=_params(("parallel",)),
    )(own, recv)


def _sum8(g, name):
    _, r, c = g.shape
    tm = _pick(r, (256, 128, 64, 32, 16, 8))

    def body(g_ref, o_ref):
        acc = g_ref[0]
        for k in range(1, 8):
            acc = acc + g_ref[k]
        o_ref[...] = acc

    return pl.pallas_call(
        body, name=name, grid=(r // tm,),
        in_specs=[pl.BlockSpec((8, tm, c), lambda i: (0, i, 0))],
        out_specs=pl.BlockSpec((tm, c), lambda i: (i, 0)),
        out_shape=jax.ShapeDtypeStruct((r, c), F32), compiler_params=_params(("parallel",)),
    )(g)


def _silu_rows(x):
    def body(x_ref, o_ref):
        xx = x_ref[...]
        o_ref[...] = xx * _sigmoid(xx)

    return pl.pallas_call(body, name="silu_rows", out_shape=jax.ShapeDtypeStruct(x.shape, F32))(x)


def _coords():
    return lax.axis_index("x"), lax.axis_index("y"), lax.axis_index("c")


def _allgather8(v, name):
    r, n = v.shape

    def body(v_ref, out_ref, send_sems, recv_sems):
        x, y, c = _coords()
        me = 4 * x + 2 * y + c
        out_ref[me] = v_ref[...]
        peers = []
        for k in range(1, 8):
            px = 1 - x if k & 4 else x
            py = 1 - y if k & 2 else y
            pc = 1 - c if k & 1 else c
            peers.append((px, py, pc))

        def copy(k, slot, to):
            return pltpu.make_async_remote_copy(
                src_ref=v_ref, dst_ref=out_ref.at[slot], send_sem=send_sems.at[k], recv_sem=recv_sems.at[k],
                device_id=to, device_id_type=MESH)

        sends = [copy(k, me, p) for k, p in enumerate(peers)]
        for cp in sends:
            cp.start()
        for k, (px, py, pc) in enumerate(peers):
            copy(k, 4 * px + 2 * py + pc, (px, py, pc)).wait_recv()
        for cp in sends:
            cp.wait_send()

    return pl.pallas_call(
        body, name=name, out_shape=jax.ShapeDtypeStruct((8, r, n), v.dtype),
        in_specs=[pl.BlockSpec(memory_space=pltpu.VMEM)], out_specs=pl.BlockSpec(memory_space=pltpu.VMEM),
        scratch_shapes=[pltpu.SemaphoreType.DMA((7,)), pltpu.SemaphoreType.DMA((7,))],
        compiler_params=pltpu.CompilerParams(vmem_limit_bytes=VMEM_LIMIT),
    )(v)


def _other_chips(x, y):
    return [(1 - x, y), (x, 1 - y), (1 - x, 1 - y)]


_HBM = pl.BlockSpec(memory_space=pl.ANY)


class _Exchange:
    def __init__(self, ins, out_shapes, sems, start, finish):
        self.ins, self.out_shapes, self.sems, self.start, self.finish = ins, out_shapes, sems, start, finish


def _run_exchange(ex, name):
    n_in, n_out = len(ex.ins), len(ex.out_shapes)

    def body(*refs):
        ins, outs, sems = refs[:n_in], refs[n_in:n_in + n_out], refs[n_in + n_out:]
        ex.start(ins, outs, sems)
        ex.finish(ins, outs, sems)

    return pl.pallas_call(body, name=name, out_shape=list(ex.out_shapes), in_specs=[_HBM] * n_in,
                          out_specs=[_HBM] * n_out, scratch_shapes=list(ex.sems))(*ex.ins)


def _gather_exchange(shards):
    n = len(shards)

    def plan(ins, outs, sems):
        send_sems, recv_sems, local_sems = sems
        x, y, c = _coords()
        s_me = 2 * x + y
        chips = _other_chips(x, y)

        def half(i, slot, hc):
            rh = shards[i].shape[0] // 2
            return outs[i].at[slot, pl.ds(pl.multiple_of(hc * rh, 16), rh), :]

        def copy(i, k, src, dst, to):
            return pltpu.make_async_remote_copy(src_ref=src, dst_ref=dst, send_sem=send_sems.at[6 * i + k],
                                                recv_sem=recv_sems.at[6 * i + k], device_id=to, device_id_type=MESH)

        local = [pltpu.make_async_copy(ins[i], outs[i].at[s_me], local_sems.at[i]) for i in range(n)]
        first = []
        for i in range(n):
            rh = shards[i].shape[0] // 2
            my_half = ins[i].at[pl.ds(pl.multiple_of(c * rh, 16), rh), :]
            first += [copy(i, j, my_half, half(i, s_me, c), (px, py, c)) for j, (px, py) in enumerate(chips)]
        return (x, y, c), chips, half, copy, local, first

    def start(ins, outs, sems):
        _, _, _, _, local, first = plan(ins, outs, sems)
        for cp in local + first:
            cp.start()

    def finish(ins, outs, sems):
        (x, y, c), chips, half, copy, local, first = plan(ins, outs, sems)
        sibling = (x, y, 1 - c)
        passed = []
        for i in range(n):
            for j, (px, py) in enumerate(chips):
                land = half(i, 2 * px + py, c)
                copy(i, j, land, land, (px, py, c)).wait_recv()
                fw = copy(i, 3 + j, land, land, sibling)
                fw.start()
                passed.append(fw)
        for i in range(n):
            for j, (px, py) in enumerate(chips):
                land = half(i, 2 * px + py, 1 - c)
                copy(i, 3 + j, land, land, sibling).wait_recv()
        for cp in first + passed:
            cp.wait_send()
        for cp in local:
            cp.wait()

    return _Exchange(list(shards), [jax.ShapeDtypeStruct((4,) + v.shape, v.dtype) for v in shards],
                     [pltpu.SemaphoreType.DMA((6 * n,)), pltpu.SemaphoreType.DMA((6 * n,)),
                      pltpu.SemaphoreType.DMA((n,))], start, finish)


def _scatter_exchange(gs):
    n = len(gs)

    def copies(ins, outs, sems):
        send_sems, recv_sems = sems
        x, y, c = _coords()
        return [pltpu.make_async_remote_copy(
                    src_ref=ins[i].at[2 * px + py], dst_ref=outs[i].at[j], send_sem=send_sems.at[3 * i + j],
                    recv_sem=recv_sems.at[3 * i + j], device_id=(px, py, c), device_id_type=MESH)
                for i in range(n) for j, (px, py) in enumerate(_other_chips(x, y))]

    def start(ins, outs, sems):
        for cp in copies(ins, outs, sems):
            cp.start()

    def finish(ins, outs, sems):
        cps = copies(ins, outs, sems)
        for cp in cps:
            cp.wait_recv()
        for cp in cps:
            cp.wait_send()

    return _Exchange(list(gs), [jax.ShapeDtypeStruct((3,) + g.shape[1:], g.dtype) for g in gs],
                     [pltpu.SemaphoreType.DMA((3 * n,)), pltpu.SemaphoreType.DMA((3 * n,))], start, finish)


def _allgather_exchange(v):
    def copies(ins, outs, sems):
        send_sems, recv_sems, _ = sems
        x, y, c = _coords()
        me = 4 * x + 2 * y + c
        cps = []
        for k in range(1, 8):
            peer = (1 - x if k & 4 else x, 1 - y if k & 2 else y, 1 - c if k & 1 else c)
            cps.append(pltpu.make_async_remote_copy(
                src_ref=ins[0], dst_ref=outs[0].at[me], send_sem=send_sems.at[k - 1], recv_sem=recv_sems.at[k - 1],
                device_id=peer, device_id_type=MESH))
        return me, cps

    def start(ins, outs, sems):
        me, cps = copies(ins, outs, sems)
        pltpu.make_async_copy(ins[0], outs[0].at[me], sems[2]).start()
        for cp in cps:
            cp.start()

    def finish(ins, outs, sems):
        me, cps = copies(ins, outs, sems)
        for cp in cps:
            cp.wait_recv()
        for cp in cps:
            cp.wait_send()
        pltpu.make_async_copy(ins[0], outs[0].at[me], sems[2]).wait()

    return _Exchange([v], [jax.ShapeDtypeStruct((8,) + v.shape, v.dtype)],
                     [pltpu.SemaphoreType.DMA((7,)), pltpu.SemaphoreType.DMA((7,)), pltpu.SemaphoreType.DMA],
                     start, finish)


def _swap_sibling(vs):
    n = len(vs)

    def body(*refs):
        ins, outs = refs[:n], refs[n:2 * n]
        send_sems, recv_sems = refs[2 * n:]
        x, y, c = _coords()
        cps = [pltpu.make_async_remote_copy(src_ref=ins[i], dst_ref=outs[i], send_sem=send_sems.at[i],
                                            recv_sem=recv_sems.at[i], device_id=(x, y, 1 - c), device_id_type=MESH)
               for i in range(n)]
        for cp in cps:
            cp.start()
        for cp in cps:
            cp.wait()

    return pl.pallas_call(
        body, name="swap_sibling", out_shape=[jax.ShapeDtypeStruct(v.shape, v.dtype) for v in vs],
        in_specs=[_HBM] * n, out_specs=[_HBM] * n,
        scratch_shapes=[pltpu.SemaphoreType.DMA((n,)), pltpu.SemaphoreType.DMA((n,))],
    )(*vs)


def _pad_rows(a, rows):
    return jnp.pad(a, ((0, rows - a.shape[0]), (0, 0)))


def _block_diag(w):
    eye = jnp.eye(RG_BLOCKS, dtype=w.dtype)
    return (eye[:, None, :, None] * w[:, :, None, :]).reshape(D_RNN, D_RNN)


def _diag_blocks(g):
    g4 = g.reshape(RG_BLOCKS, RG_BW, RG_BLOCKS, RG_BW)
    idx = jnp.arange(RG_BLOCKS)
    return g4[idx, :, idx, :]


def _prepare_rest(p):
    w = {}
    for k_, n_ in (("pa", "w_proj_a"), ("pb", "w_proj_b"), ("out", "w_out"), ("down", "ffn_w_down")):
        w[k_] = p[n_].astype(MXU_DT)
        w[k_ + "_t"] = w[k_].T
    w["gu"] = jnp.concatenate([p["ffn_w_gate"], p["ffn_w_up"]], axis=1).astype(MXU_DT)
    w["gu_t"] = w["gu"].T
    return w


def _prepare_first(p):
    w = {}
    wi = p["w_in"].astype(MXU_DT)
    cat = jnp.concatenate([wi[:, 2560:6656], wi[:, 6656:8704], wi[:, 8736:10784], wi[:, 0:2560],
                           wi[:, 8704:8736], jnp.zeros((D, 96), MXU_DT)], axis=1)
    w["in_cat"], w["in_cat_t"] = cat, cat.T
    w["rg_cw"] = _pad_rows(p["rg_conv_w"], 8)
    w["dn_cw"] = _pad_rows(p["dn_conv_w"], 8)
    w["ffn_cw"] = _pad_rows(p["ffn_conv_w"], 8)
    w["rg_vec"] = _pad_rows(jnp.stack([p["rg_conv_b"], p["rg_b_a"], p["rg_b_x"], p["rg_lambda"]]), 8)
    w["wa"] = _block_diag(p["rg_w_a"]).astype(MXU_DT)
    w["wx"] = _block_diag(p["rg_w_x"]).astype(MXU_DT)
    w["wa_t"], w["wx_t"] = w["wa"].T, w["wx"].T
    w["arow"] = jnp.pad(p["dn_a_log"], (0, 128 - NV))[None, :]
    w["drow"] = jnp.pad(p["dn_dt_bias"], (0, 128 - NV))[None, :]
    w["acol"] = p["dn_a_log"][:, None]
    w["dcol"] = p["dn_dt_bias"][:, None]
    w["nw"] = p["dn_norm_w"][None, :]
    w["ffn_cb"] = p["ffn_conv_b"][None, :]
    for n_ in ("ln1_g", "ln1_b", "ln2_g", "ln2_b"):
        w[n_] = p[n_][None, :]
    return w


def _mm_sided(a, b, side, **kw):
    if side is None:
        return _mm(a, b, **kw), []
    res = _mm(a, b, side=side, **kw)
    return res[0], res[1:]


_SMALL = ("rg_conv_w", "dn_conv_w", "ffn_conv_w", "rg_conv_b", "rg_w_a", "rg_b_a", "rg_w_x", "rg_b_x",
          "rg_lambda", "dn_a_log", "dn_dt_bias", "dn_norm_w", "ln1_g", "ln1_b", "ffn_conv_b", "ln2_g", "ln2_b")
_REST_A = ("w_proj_a", "w_proj_b", "w_out")
_REST_B = ("ffn_w_gate", "ffn_w_up", "ffn_w_down")


def _local_step(x, tgt, adam, p, shards=None):
    w = _prepare_first(p)
    side_a = side_b = None
    if shards is not None:
        side_a = _gather_exchange([shards[n] for n in _REST_A])
        side_b = _gather_exchange([shards[n] for n in _REST_B])
    h1, h1t = _modulate(x, adam)
    cat = w["in_cat"]
    pqkv, got_a = _mm_sided(h1, cat[:, 0:4096], side_a, name="proj_qkv")
    px, got_b = _mm_sided(h1, cat[:, 8192:10752], side_b, name="proj_x")
    if shards is not None:
        p = dict(p, **{n: _unstack_shards(n, a_) for n, a_ in zip(_REST_A + _REST_B, got_a + got_b)})
    w.update(_prepare_rest(p))
    pz = _mm(h1, cat[:, 4096:6144], name="proj_z")
    pg = _mm(h1, cat[:, 6144:8192], name="proj_g")
    pab = _mm(h1, cat[:, 10752:10880], name="proj_ab")
    hrec, xcs, rec, rec_t = _rg_fwd(px, w["rg_cw"], w["rg_vec"], w["wa"], w["wx"])
    q, k, v, rt, cpre = _dn_prep(pqkv, pab, w["dn_cw"], w["arow"], w["drow"])
    u, ww, qd, kd, pm, tinv, gam = _dn_intra(q, k, v, rt)
    o, vn, ssave = _dn_seq(u, ww, qd, kd, pm, gam)
    dn, dn_t = _dn_post(o, pz, w["nw"])
    ya = _mm(rec, w["pa"], name="proj_a")
    yb = _mm(dn, w["pb"], name="proj_b")
    merged, merged_t = _merge(pg, ya, yb)
    mix = _mm(merged, w["out"], name="proj_out")
    r1, x2, h2, h2_t = _ln1(x, mix, adam, w["ln1_g"], w["ln1_b"])
    gu = _mm(h2, w["gu"], name="ffn_gu")
    act, act_t, gcf = _ffn_act(gu, w["ffn_cw"], w["ffn_cb"])
    dff, dx2a, red2 = _ffn_down_ln2_loss(act, w["down"], x2, tgt, adam, w["ln2_g"], w["ln2_b"])
    g = {}
    dact = _mm(dff, w["down_t"], name="d_act")
    g["ffn_w_down"] = _mm(act_t, dff, name="g_down")
    dgu, gcw_f, gcb_f = _ffn_bwd(dact, gu, gcf, w["ffn_cw"])
    ggu = _mm(h2_t, dgu, name="g_gu")
    g["ffn_w_gate"], g["ffn_w_up"] = ggu[:, :D_FF], ggu[:, D_FF:]
    g["ffn_conv_w"], g["ffn_conv_b"] = gcw_f[0:3], gcb_f[0]
    dmix, dxa, red1 = _d_h2_ln1_bwd(dgu, w["gu_t"], dx2a, x2, r1, mix, adam, w["ln1_g"])
    g["w_out"] = _mm(merged_t, dmix, name="g_out")
    dya, dyb, dpg = _d_merged_bwd(dmix, w["out_t"], pg, ya, yb)
    drec = _mm(dya, w["pa_t"], name="d_rec")
    g["w_proj_a"] = _mm(rec_t, dya, name="g_pa")
    g["w_proj_b"] = _mm(dn_t, dyb, name="g_pb")
    do, dpz, gnw = _d_dn_post_bwd(dyb, w["pb_t"], o, pz, w["nw"])
    dvn, dkd, dgam = _dn_seq_bwd(do, qd, kd, pm, ww, vn, ssave, gam)
    dq, dk, dv, drt = _dn_intra_bwd(q, k, v, rt, do, dvn, dkd, gam, dgam, ssave, tinv, u, ww, vn)
    dpqkv, dpab, gcw_d, gsc = _dn_prep_bwd(dq, dk, dv, drt, pqkv, cpre, pab, w["dn_cw"], w["acol"], w["dcol"])
    side_r = None
    if shards is not None:
        side_r = _scatter_exchange([_stack_shards(n, g[n]).astype(MXU_DT) for n in _REST_A + _REST_B])
    dpx, gwa, gwx, gcw_r, gvec, *got_r = _rg_bwd(drec, px, xcs, hrec, w["rg_cw"], w["rg_vec"], w["wa"], w["wx"],
                                                 w["wa_t"], w["wx_t"], side_r)
    dproj = jnp.concatenate([dpqkv, dpz, dpg, dpx, dpab], axis=1)
    reorder = lambda gc: jnp.concatenate([gc[:, 8192:10752], gc[:, 0:4096], gc[:, 4096:6144], gc[:, 10752:10784],
                                          gc[:, 6144:8192]], axis=1)
    wire = lambda gh: _scatter_exchange([_stack_shards("w_in", gh).astype(MXU_DT)]) if shards is not None else None
    g["rg_conv_w"], g["rg_conv_b"] = gcw_r[0:4], gvec[0]
    g["rg_w_a"], g["rg_w_x"] = _diag_blocks(gwa), _diag_blocks(gwx)
    g["rg_b_a"], g["rg_b_x"], g["rg_lambda"] = gvec[1], gvec[2], gvec[3]
    g["dn_conv_w"] = gcw_d[0:4]
    g["dn_a_log"], g["dn_dt_bias"], g["dn_norm_w"] = gsc[:, 0], gsc[:, 1], gnw[0]
    g["ln1_g"], g["ln1_b"] = red1[2], red1[3]
    g["ln2_g"], g["ln2_b"] = red2[0], red2[1]
    side_s = None
    if shards is not None:
        side_s = _allgather_exchange(_pack([jnp.full((1,), red2[3, 0], F32)] + [g[n] for n in _SMALL], 128, 64))
    g_top, got_s = _mm_sided(h1t[:D // 2], dproj, side_s, name="g_in_top")
    g_top = reorder(g_top)
    g_bot, got_top = _mm_sided(h1t[D // 2:], dproj, wire(g_top), name="g_in_bot")
    g_bot = reorder(g_bot)
    gx, red0, *got_bot = _d_h1_modulate_bwd(dproj, w["in_cat_t"], dxa, x, adam, wire(g_bot))
    g["w_in"] = jnp.concatenate([g_top, g_bot], axis=0)
    got = dict(zip(_REST_A + _REST_B, got_r))
    if shards is not None:
        got["w_in"] = jnp.concatenate([got_top[0], got_bot[0]], axis=1)
        got["small"] = got_s[0]
    d_ada = jnp.concatenate([red0[1], red0[0], red1[4], red1[1], red1[0], red2[2]])
    return red2[3, 0], gx, g, d_ada, got


_BIG = ("w_in", "w_proj_a", "w_proj_b", "w_out", "ffn_w_gate", "ffn_w_up", "ffn_w_down")
_COL_SHARDED = ("w_in", "ffn_w_gate", "ffn_w_up")
_CONV = ("rg_conv_w", "dn_conv_w", "ffn_conv_w")
_REPL = ("b_ada", "rg_conv_b", "rg_w_a", "rg_b_a", "rg_w_x", "rg_b_x", "rg_lambda", "dn_a_log",
         "dn_dt_bias", "dn_norm_w", "ln1_g", "ln1_b", "ffn_conv_b", "ln2_g", "ln2_b")
_NAMES = ("w_ada", "b_ada", "w_in", "rg_conv_w", "rg_conv_b", "rg_w_a", "rg_b_a", "rg_w_x", "rg_b_x",
          "rg_lambda", "dn_conv_w", "dn_a_log", "dn_dt_bias", "dn_norm_w", "w_proj_a", "w_proj_b", "w_out",
          "ln1_g", "ln1_b", "ffn_w_gate", "ffn_w_up", "ffn_conv_w", "ffn_conv_b", "ffn_w_down", "ln2_g", "ln2_b")


def _pack(arrs, width, row_mult):
    pieces = []
    for a in arrs:
        f = a.reshape(-1)
        pieces.append(jnp.pad(f, (0, (-f.shape[0]) % (8 * width))).reshape(-1, width))
    rows = sum(p_.shape[0] for p_ in pieces)
    if rows % row_mult:
        pieces.append(jnp.zeros((row_mult - rows % row_mult, width), pieces[0].dtype))
    return jnp.concatenate(pieces, axis=0)


def _unpack(flat, shapes, width):
    out, row = [], 0
    for shp in shapes:
        n = 1
        for d_ in shp:
            n *= d_
        rows = -(-n // (8 * width)) * 8
        out.append(flat[row:row + rows].reshape(-1)[:n].reshape(shp))
        row += rows
    return out


def _stack_shards(name, full):
    if name in _COL_SHARDED or name in _CONV:
        r, ncol = full.shape
        return full.reshape(r, 4, ncol // 4).transpose(1, 0, 2)
    return full.reshape((4, full.shape[0] // 4) + full.shape[1:])


def _unstack_shards(name, st):
    if name in _COL_SHARDED or name in _CONV:
        return st.transpose(1, 0, 2).reshape(st.shape[1], 4 * st.shape[2])
    return st.reshape((4 * st.shape[1],) + st.shape[2:])


def kernel(x, c, w_ada, b_ada, w_in, rg_conv_w, rg_conv_b, rg_w_a, rg_b_a, rg_w_x, rg_b_x, rg_lambda, dn_conv_w, dn_a_log, dn_dt_bias, dn_norm_w, w_proj_a, w_proj_b, w_out, ln1_g, ln1_b, ffn_w_gate, ffn_w_up, ffn_conv_w, ffn_conv_b, ffn_w_down, ln2_g, ln2_b, loss_target, m_w_ada, m_b_ada, m_w_in, m_rg_conv_w, m_rg_conv_b, m_rg_w_a, m_rg_b_a, m_rg_w_x, m_rg_b_x, m_rg_lambda, m_dn_conv_w, m_dn_a_log, m_dn_dt_bias, m_dn_norm_w, m_w_proj_a, m_w_proj_b, m_w_out, m_ln1_g, m_ln1_b, m_ffn_w_gate, m_ffn_w_up, m_ffn_conv_w, m_ffn_conv_b, m_ffn_w_down, m_ln2_g, m_ln2_b, v_w_ada, v_b_ada, v_w_in, v_rg_conv_w, v_rg_conv_b, v_rg_w_a, v_rg_b_a, v_rg_w_x, v_rg_b_x, v_rg_lambda, v_dn_conv_w, v_dn_a_log, v_dn_dt_bias, v_dn_norm_w, v_w_proj_a, v_w_proj_b, v_w_out, v_ln1_g, v_ln1_b, v_ffn_w_gate, v_ffn_w_up, v_ffn_conv_w, v_ffn_conv_b, v_ffn_w_down, v_ln2_g, v_ln2_b):
    args = locals()
    wts = {n: args[n][0] for n in _NAMES}
    mom = {n: args["m_" + n][0] for n in _NAMES}
    var = {n: args["v_" + n][0] for n in _NAMES}
    xs, tgt = x[0], loss_target[0]
    ix, iy, ic = _coords()
    shard = 2 * ix + iy
    batch = 4 * ix + 2 * iy + ic

    c_all = _allgather8(_pad_rows(c, 8), "gather_c")[:, 0, :]
    sc16 = _pad_rows(_silu_rows(c_all), 16)
    ada_cols = _mm(sc16, w_ada[0], name="ada")[:8]
    ada_g = _allgather8(ada_cols, "gather_ada")
    ada_all = jnp.concatenate([ada_g[0], ada_g[2], ada_g[4], ada_g[6]], axis=1) + b_ada
    adam = _pad_rows(lax.dynamic_index_in_dim(ada_all, batch, 0, keepdims=False).reshape(6, D), 8)

    (w_in_all,) = _run_exchange(_gather_exchange([wts["w_in"].astype(MXU_DT)]), "gather_w_in")
    full = {"w_in": _unstack_shards("w_in", w_in_all)}
    conv_shard = _pack([wts[n] for n in _CONV], 128, 8)
    conv_all = _allgather8(conv_shard, "gather_conv")
    shapes_conv = [wts[n].shape for n in _CONV]
    per_shard = [_unpack(conv_all[2 * s], shapes_conv, 128) for s in range(4)]
    for i, n in enumerate(_CONV):
        full[n] = _unstack_shards(n, jnp.stack([per_shard[s][i] for s in range(4)]))
    for n in _REPL:
        full[n] = wts[n]

    shards = {n: wts[n].astype(MXU_DT) for n in _REST_A + _REST_B}
    loss_b, gx, g, d_ada, recv = _local_step(xs, tgt, adam, full, shards)

    parts = []
    for n in _BIG:
        r_ = recv[n]
        axis = 1 if n in _COL_SHARDED else 0
        width = wts[n].shape[axis]
        own = lax.dynamic_slice_in_dim(g[n], shard * width, width, axis=axis)
        parts.append(_sum_partials(own, r_, "sum_" + n))
    parts_sib = _swap_sibling(parts)
    out = {n: [] for n in _NAMES}
    for n, p_, q_ in zip(_BIG, parts, parts_sib):
        out[n] = list(_adamw([p_, q_], wts[n], mom[n], var[n], "adamw_" + n))

    tot = _unpack(_sum8(recv["small"], "sum_small"), [(1,)] + [full[n].shape for n in _SMALL], 128)
    gsum = dict(zip(_SMALL, tot[1:]))
    loss = tot[0][0]
    for n in _CONV:
        gsum[n] = lax.dynamic_index_in_dim(_stack_shards(n, gsum[n]), shard, 0, keepdims=False)
    d_ada_g = _allgather8(d_ada.reshape(6 * D // 128, 128), "gather_d_ada")
    gsum["b_ada"] = _sum8(d_ada_g, "sum_d_ada").reshape(6 * D)
    d_ada_all = d_ada_g.reshape(8, 6 * D)
    cols = lax.dynamic_slice_in_dim(d_ada_all, shard * (6 * D // 4), 6 * D // 4, axis=1)
    g_wada = _mm(sc16, _pad_rows(cols, 16), name="g_ada", trans_a=True)
    res = _adamw([g_wada], wts["w_ada"], mom["w_ada"], var["w_ada"], "adamw_ada")
    out["w_ada"] = list(res)
    names_s = _CONV + _REPL
    shapes_s = [wts[n].shape for n in names_s]
    pk = lambda d_: _pack([d_[n] for n in names_s], 128, 64)
    res_s = _adamw([pk(gsum)], pk(wts), pk(mom), pk(var), "adamw_small")
    for r_ in res_s:
        for n, a in zip(names_s, _unpack(r_, shapes_s, 128)):
            out[n].append(a)

    outs = [loss, gx[None]]
    for i in range(4):
        outs += [out[n][i][None] for n in _NAMES]
    return tuple(outs)
```

```python
import functools

import jax
import jax.numpy as jnp
from jax import lax
from jax.experimental import pallas as pl
from jax.experimental.pallas import tpu as pltpu

F32 = jnp.float32
BF16 = jnp.bfloat16
MXU_DT = BF16

D = 1024
D_RNN = 1280
RG_BLOCKS = 16
RG_BW = 80
RG_C = 8.0
NQ = 8
NV = 16
HD = 128
CH = 64
D_FF = 2816
LN_EPS = 1e-5
RMS_EPS = 1e-6
L2_EPS = 1e-6
ALPHA = 2.0 ** 0.25
Q_SCALE = HD ** -0.5
N_CAT = 10880
VMEM_LIMIT = 56 * 1024 * 1024
MM_VMEM_BUDGET = 36 * 1024 * 1024
SEQ_GROUP = 16
MESH = pl.DeviceIdType.MESH

ADAM_LR, ADAM_B1, ADAM_B2, ADAM_EPS, ADAM_WD, ADAM_STEP = 1e-3, 0.9, 0.999, 1e-8, 0.01, 10


def _sigmoid(x):
    return 0.5 * jnp.tanh(0.5 * x) + 0.5


def _softplus(x):
    return jnp.maximum(x, 0.0) + jnp.log1p(jnp.exp(-jnp.abs(x)))


_GC = 0.7978845608028654


def _gelu(x):
    return 0.5 * x * (1.0 + jnp.tanh(_GC * (x + 0.044715 * x * x * x)))


def _gelu_and_grad(x):
    t = jnp.tanh(_GC * (x + 0.044715 * x * x * x))
    g = 0.5 * x * (1.0 + t)
    dg = 0.5 * (1.0 + t) + 0.5 * x * (1.0 - t * t) * _GC * (1.0 + 3 * 0.044715 * x * x)
    return g, dg


def _neg_expm1(y):
    series = -y * (1.0 + 0.5 * y * (1.0 + y * (1.0 / 3.0)))
    return jnp.where(y > -0.01, series, 1.0 - jnp.exp(y))


def _dot(a, b):
    return jnp.dot(a.astype(MXU_DT), b.astype(MXU_DT), preferred_element_type=F32)


def _dot_nt(a, b):
    return lax.dot_general(a.astype(MXU_DT), b.astype(MXU_DT), (((1,), (1,)), ((), ())),
                           preferred_element_type=F32)


def _dot_tn(a, b):
    return lax.dot_general(a.astype(MXU_DT), b.astype(MXU_DT), (((0,), (0,)), ((), ())),
                           preferred_element_type=F32)


def _split(a):
    hi = a.astype(BF16)
    return hi, (a - hi.astype(F32)).astype(BF16)


def _dot3(a, b, dims=(((1,), (0,)), ((), ()))):
    ah, al = _split(a)
    bh, bl = _split(b)
    d = lambda p, q: lax.dot_general(p, q, dims, preferred_element_type=F32)
    return d(ah, bh) + (d(al, bh) + d(ah, bl))


def _dot3_tn(a, b):
    return _dot3(a, b, (((0,), (0,)), ((), ())))


def _iota(shape, dim):
    return lax.broadcasted_iota(jnp.int32, shape, dim)


def _shift_down(x, before, j):
    if j == 0:
        return x
    xr = pltpu.roll(x, j, 0)
    br = pltpu.roll(before, j, 0)
    top = jnp.where(_iota(br.shape, 0) < j, br, xr[:8])
    return jnp.concatenate([top, xr[8:]], axis=0)


def _shift_up(x, after, j):
    if j == 0:
        return x
    t = x.shape[0]
    xr = pltpu.roll(x, t - j, 0)
    ar = pltpu.roll(after, 8 - j, 0)
    bot = jnp.where(_iota(ar.shape, 0) >= 8 - j, ar, xr[t - 8:])
    return jnp.concatenate([xr[:t - 8], bot], axis=0)


def _taps(x, before, k):
    return [_shift_down(x, before, k - 1 - i) for i in range(k)]


def _conv_taps(taps, w_ref):
    y = w_ref[0:1, :] * taps[0]
    for i in range(1, len(taps)):
        y = y + w_ref[i:i + 1, :] * taps[i]
    return y


def _conv_causal(x, before, w_ref, k):
    return _conv_taps(_taps(x, before, k), w_ref)


def _conv_causal_bwd(dy, after, w_ref, k, x=None, gw_ref=None):
    dx = None
    for i in range(k):
        sh = _shift_up(dy, after, k - 1 - i)
        term = w_ref[i:i + 1, :] * sh
        dx = term if dx is None else dx + term
        if x is not None:
            gw_ref[i:i + 1, :] += _rsum(x * sh)
    return dx


def _scan_fwd(a, u):
    t = a.shape[0]
    rows = _iota(a.shape, 0)
    d = 1
    while d < t:
        m = rows >= d
        u = u + jnp.where(m, a * pltpu.roll(u, d, 0), 0.0)
        a = jnp.where(m, a * pltpu.roll(a, d, 0), a)
        d *= 2
    return a, u


def _scan_rev(a, u):
    t = a.shape[0]
    rows = _iota(a.shape, 0)
    d = 1
    while d < t:
        m = rows < t - d
        u = u + jnp.where(m, a * pltpu.roll(u, t - d, 0), 0.0)
        a = jnp.where(m, a * pltpu.roll(a, t - d, 0), a)
        d *= 2
    return a, u


def _chunk_cumsum(g, axis, rev=False):
    n = g.shape[axis]
    pos = _iota(g.shape, axis) & (CH - 1)
    d = 1
    while d < CH:
        if rev:
            g = g + jnp.where(pos < CH - d, pltpu.roll(g, n - d, axis), 0.0)
        else:
            g = g + jnp.where(pos >= d, pltpu.roll(g, d, axis), 0.0)
        d *= 2
    return g


def _ln_stats(r):
    mu = jnp.mean(r, axis=-1, keepdims=True)
    xc = r - mu
    var = jnp.mean(xc * xc, axis=-1, keepdims=True)
    rstd = lax.rsqrt(var + LN_EPS)
    return xc * rstd, rstd


def _ln_bwd(dy, xhat, rstd, g):
    dxh = dy * g
    return rstd * (dxh - jnp.mean(dxh, axis=-1, keepdims=True)
                   - xhat * jnp.mean(dxh * xhat, axis=-1, keepdims=True))


def _rsum(x):
    return jnp.sum(x, axis=0, keepdims=True)


def _params(sem):
    return pltpu.CompilerParams(dimension_semantics=sem, vmem_limit_bytes=VMEM_LIMIT)


def _pick(n, cands):
    for c in cands:
        if n % c == 0:
            return c
    return n


def _rows(tm, w, col=0, nt=None):
    if nt is None:
        return pl.BlockSpec((tm, w), lambda i: (i, col))
    return pl.BlockSpec((tm, w), lambda i: (nt - 1 - i, col))


def _before(tm, w, col=0, nt=None):
    r = tm // 8
    if nt is None:
        return pl.BlockSpec((8, w), lambda i: (jnp.maximum(i * r - 1, 0), col))
    return pl.BlockSpec((8, w), lambda i: (jnp.maximum((nt - 1 - i) * r - 1, 0), col))


def _cols(tm, w):
    return pl.BlockSpec((w, tm), lambda i: (0, i))


def _whole(shape):
    return pl.BlockSpec(shape, lambda *_: (0,) * len(shape))


def _mm_plan(a, b, out_dtype):
    m, kk = a.shape
    _, n = b.shape
    tm = _pick(m, (512, 256, 128))
    tk = kk if kk <= 5632 else _pick(kk, (2176, 2048, 1024))
    nk = kk // tk

    def vmem_bytes(tn):
        blocks = tm * tk * a.dtype.itemsize + tk * tn * b.dtype.itemsize + tm * tn * jnp.dtype(out_dtype).itemsize
        return 2 * blocks + (tm * tn * 4 if nk > 1 else 0)

    cands = [t for t in (1408, 1280, 1024, 640, 512, 256, 128) if n % t == 0] or [n]
    tn = next((t for t in cands if vmem_bytes(t) <= MM_VMEM_BUDGET), cands[-1])
    return tm, tn, tk, nk


def _mm_side(a, b, name, out_dtype, side):
    m, _ = a.shape
    _, n = b.shape
    tm, tn, tk, nk = _mm_plan(a, b, out_dtype)
    ni, nj = m // tm, n // tn
    n_in, n_out = len(side.ins), len(side.out_shapes)

    def body(*refs):
        a_ref, b_ref = refs[0], refs[1]
        s_in = refs[2:2 + n_in]
        o_ref = refs[2 + n_in]
        s_out = refs[3 + n_in:3 + n_in + n_out]
        acc = refs[3 + n_in + n_out]
        sems = refs[4 + n_in + n_out:]
        i, j, k = pl.program_id(0), pl.program_id(1), pl.program_id(2)

        @pl.when((i == 0) & (j == 0) & (k == 0))
        def _():
            side.start(s_in, s_out, sems)

        @pl.when(k == 0)
        def _():
            acc[...] = jnp.zeros_like(acc)
        acc[...] += _dot(a_ref[...], b_ref[...])

        @pl.when(k == nk - 1)
        def _():
            o_ref[...] = acc[...].astype(out_dtype)

        @pl.when((i == ni - 1) & (j == nj - 1) & (k == nk - 1))
        def _():
            side.finish(s_in, s_out, sems)

    return pl.pallas_call(
        body, name=name, grid=(ni, nj, nk),
        in_specs=[pl.BlockSpec((tm, tk), lambda i, j, k: (i, k)),
                  pl.BlockSpec((tk, tn), lambda i, j, k: (k, j))] + [_HBM] * n_in,
        out_specs=[pl.BlockSpec((tm, tn), lambda i, j, k: (i, j))] + [_HBM] * n_out,
        out_shape=[jax.ShapeDtypeStruct((m, n), out_dtype)] + list(side.out_shapes),
        scratch_shapes=[pltpu.VMEM((tm, tn), F32)] + list(side.sems),
        compiler_params=_params(("arbitrary", "arbitrary", "arbitrary")),
    )(a, b, *side.ins)


def _mm(a, b, *, name, trans_a=False, out_dtype=F32, side=None):
    if trans_a:
        return _mm(a.T, b, name=name, out_dtype=out_dtype, side=side)
    if side is not None:
        return _mm_side(a, b, name, out_dtype, side)
    m, kk = a.shape
    _, n = b.shape
    tm, tn, tk, nk = _mm_plan(a, b, out_dtype)

    if nk == 1:
        def body(a_ref, b_ref, o_ref):
            o_ref[...] = _dot(a_ref[...], b_ref[...]).astype(out_dtype)
        scratch = []
    else:
        def body(a_ref, b_ref, o_ref, acc):
            k = pl.program_id(2)

            @pl.when(k == 0)
            def _():
                acc[...] = jnp.zeros_like(acc)
            acc[...] += _dot(a_ref[...], b_ref[...])

            @pl.when(k == nk - 1)
            def _():
                o_ref[...] = acc[...].astype(out_dtype)
        scratch = [pltpu.VMEM((tm, tn), F32)]

    return pl.pallas_call(
        body, name=name, grid=(m // tm, n // tn, nk),
        in_specs=[pl.BlockSpec((tm, tk), lambda i, j, k: (i, k)),
                  pl.BlockSpec((tk, tn), lambda i, j, k: (k, j))],
        out_specs=pl.BlockSpec((tm, tn), lambda i, j, k: (i, j)),
        out_shape=jax.ShapeDtypeStruct((m, n), out_dtype),
        scratch_shapes=scratch,
        compiler_params=_params(("parallel", "parallel", "arbitrary")),
    )(a, b)


def _modulate(x, adam):
    s = x.shape[0]
    tm = _pick(s, (512, 256, 128))

    def body(x_ref, ada_ref, o_ref, ot_ref):
        h = x_ref[...] * (1.0 + ada_ref[1:2, :]) + ada_ref[0:1, :]
        o_ref[...] = h.astype(MXU_DT)
        ot_ref[...] = h.T.astype(MXU_DT)

    return pl.pallas_call(
        body, name="modulate1", grid=(s // tm,),
        in_specs=[_rows(tm, D), _whole((8, D))], out_specs=[_rows(tm, D), _cols(tm, D)],
        out_shape=[jax.ShapeDtypeStruct((s, D), MXU_DT), jax.ShapeDtypeStruct((D, s), MXU_DT)],
        compiler_params=_params(("parallel",)),
    )(x, adam)


def _rg_gates(xc, wa_ref, wx_ref, vec_ref):
    xb = xc.astype(MXU_DT)
    r = _sigmoid(jnp.dot(xb, wa_ref[...], preferred_element_type=F32) + vec_ref[1:2, :])
    ig = _sigmoid(jnp.dot(xb, wx_ref[...], preferred_element_type=F32) + vec_ref[2:3, :])
    sp = _softplus(-vec_ref[3:4, :])
    la = -RG_C * r * sp
    a = jnp.exp(la)
    n1 = _neg_expm1(2.0 * la)
    rmult = lax.rsqrt(jnp.maximum(n1, 1e-20))
    return r, ig, a, n1 * rmult, sp, rmult


def _rg_fwd(px, cw, vec, wa, wx):
    s = px.shape[0]
    tm = _pick(s, (256, 128))
    w = D_RNN

    def body(xr_ref, gr_ref, cw_ref, vec_ref, wa_ref, wx_ref, h_ref, xc_ref, rec_ref, rect_ref, prev_x, prev_h):
        @pl.when(pl.program_id(0) == 0)
        def _():
            prev_x[...] = jnp.zeros_like(prev_x)
            prev_h[...] = jnp.zeros_like(prev_h)
        x = xr_ref[...]
        xc = _conv_causal(x, prev_x[...], cw_ref, 4) + vec_ref[0:1, :]
        prev_x[...] = x[tm - 8:, :]
        xc_ref[...] = xc
        _, ig, a, mult, _, _ = _rg_gates(xc, wa_ref, wx_ref, vec_ref)
        acum, h = _scan_fwd(a, mult * ig * xc)
        h = h + acum * prev_h[7:8, :]
        prev_h[...] = h[tm - 8:, :]
        h_ref[...] = h
        rec = h * _gelu(gr_ref[...])
        rec_ref[...] = rec.astype(MXU_DT)
        rect_ref[...] = rec.T.astype(MXU_DT)

    return pl.pallas_call(
        body, name="rg_fwd", grid=(s // tm,),
        in_specs=[_rows(tm, w, 0), _rows(tm, w, 1), _whole((8, w)), _whole((8, w)),
                  _whole((w, w)), _whole((w, w))],
        out_specs=[_rows(tm, w), _rows(tm, w), _rows(tm, w), _cols(tm, w)],
        out_shape=[jax.ShapeDtypeStruct((s, w), F32), jax.ShapeDtypeStruct((s, w), F32),
                   jax.ShapeDtypeStruct((s, w), MXU_DT), jax.ShapeDtypeStruct((w, s), MXU_DT)],
        scratch_shapes=[pltpu.VMEM((8, w), F32), pltpu.VMEM((8, w), F32)],
        compiler_params=_params(("arbitrary",)),
    )(px, px, cw, vec, wa, wx)


def _dn_scalars(ab, arow, drow):
    lane = _iota(ab.shape, 1)
    g = jnp.where(lane < NV, -jnp.exp(arow) * _softplus(ab + drow), 0.0)
    beta = _sigmoid(ab)
    return lane, g, beta


def _l2n_heads(c, out_ref, off, scale):
    for hh in range(NQ):
        x = c[:, off + hh * HD: off + (hh + 1) * HD]
        r = lax.rsqrt(jnp.sum(x * x, axis=-1, keepdims=True) + L2_EPS)
        out_ref[:, hh * HD:(hh + 1) * HD] = x * (r * scale)


def _dn_prep(pqkv, pab, cw, arow, drow):
    s = pqkv.shape[0]
    tm = 128
    wq = NQ * HD

    def body(x_ref, ab_ref, cw_ref, a_ref, d_ref, q_ref, k_ref, v_ref, rt_ref, cp_ref, prev_x):
        @pl.when(pl.program_id(0) == 0)
        def _():
            prev_x[...] = jnp.zeros_like(prev_x)
        x = x_ref[...]
        cp = _conv_causal(x, prev_x[...], cw_ref, 4)
        prev_x[...] = x[tm - 8:, :]
        cp_ref[...] = cp
        c = cp * _sigmoid(cp)
        _l2n_heads(c, q_ref, 0, Q_SCALE)
        _l2n_heads(c, k_ref, wq, 1.0)
        v_ref[...] = c[:, 2 * wq:]
        lane, g, beta = _dn_scalars(ab_ref[...], a_ref[...], d_ref[...])
        gc = _chunk_cumsum(g, 0)
        gl = gc + _chunk_cumsum(g, 0, rev=True) - g
        pack = jnp.where(lane < NV, gc, jnp.where(lane < 2 * NV, beta,
                         jnp.where(lane < 3 * NV, pltpu.roll(gl, 2 * NV, 1), 0.0)))
        rt_ref[...] = pack.T[0:3 * NV, :]

    return pl.pallas_call(
        body, name="dn_prep", grid=(s // tm,),
        in_specs=[_rows(tm, 4 * wq), _rows(tm, 128), _whole((8, 4 * wq)), _whole((1, 128)), _whole((1, 128))],
        out_specs=[_rows(tm, wq), _rows(tm, wq), _rows(tm, 2 * wq),
                   pl.BlockSpec((3 * NV, tm), lambda i: (0, i)), _rows(tm, 4 * wq)],
        out_shape=[jax.ShapeDtypeStruct((s, wq), F32), jax.ShapeDtypeStruct((s, wq), F32),
                   jax.ShapeDtypeStruct((s, 2 * wq), F32), jax.ShapeDtypeStruct((3 * NV, s), F32),
                   jax.ShapeDtypeStruct((s, 4 * wq), F32)],
        scratch_shapes=[pltpu.VMEM((8, 4 * wq), F32)],
        compiler_params=_params(("arbitrary",)),
    )(pqkv, pab, cw, arow, drow)


def _pair_masks():
    i = _iota((2 * CH, 2 * CH), 0)
    j = _iota((2 * CH, 2 * CH), 1)
    same = (i >> 6) == (j >> 6)
    return same & (i >= j), same & (i > j)


def _head_cols(rt_ref, h):
    shp = (2 * CH, 2 * CH)
    g_r = jnp.broadcast_to(rt_ref[pl.ds(h, 1), :], shp)
    b_r = jnp.broadcast_to(rt_ref[pl.ds(NV + h, 1), :], shp)
    l_r = jnp.broadcast_to(rt_ref[pl.ds(2 * NV + h, 1), :], shp)
    return g_r, g_r.T, b_r.T, l_r, l_r.T


def _inv_unit_lower(a):
    return _inv_unit_lower_many([a])[0]


def _inv_unit_lower_many(a_list):
    n = a_list[0].shape[0]
    eye = (_iota((n, n), 0) == _iota((n, n), 1)).astype(F32)
    bs = [-a for a in a_list]
    xs = [eye + b for b in bs]
    for _ in range(5):
        bs = [_dot(b, b) for b in bs]
        xs = [x + _dot(x, b) for x, b in zip(xs, bs)]
    rs = [(eye - x) - _dot3(a, x) for a, x in zip(a_list, xs)]
    return [x + _dot(x, r) for x, r in zip(xs, rs)]


def _gam_rows(l_r):
    lrow = l_r[0:1, :]
    lane = _iota(lrow.shape, 1)
    other = pltpu.roll(lrow, CH, 1)
    return jnp.exp(jnp.where(lane < CH, lrow, other)), jnp.exp(jnp.where(lane >= CH, lrow, other))


def _dn_intra(q, k, v, rt):
    s = q.shape[0]
    nb = s // (2 * CH)
    qps = 8
    blk = pl.BlockSpec((2 * CH, qps * HD), lambda i, h: (i, h))
    blk2 = pl.BlockSpec((2 * CH, 2 * qps * HD), lambda i, h: (i, h))

    def body(q_ref, k_ref, v_ref, rt_ref, u_ref, w_ref, qd_ref, kd_ref, p_ref, ti_ref, gam_ref):
        hstep = pl.program_id(1)
        mc, ms = _pair_masks()
        gam_ref[...] = jnp.zeros_like(gam_ref)
        heads = []
        for qh in range(qps):
            qq = q_ref[:, qh * HD:(qh + 1) * HD]
            kk_ = k_ref[:, qh * HD:(qh + 1) * HD]
            kk = _dot_nt(kk_, kk_)
            qk = _dot_nt(qq, kk_)
            for j in range(2):
                idx = 2 * qh + j
                cs = slice(idx * HD, (idx + 1) * HD)
                g_r, g_c, b_c, l_r, l_c = _head_cols(rt_ref, 2 * qps * hstep + idx)
                dec = jnp.where(mc, jnp.exp(jnp.where(mc, g_c - g_r, 0.0)), 0.0)
                eg = jnp.exp(g_c)
                p_ref[:, cs] = jnp.where(mc, qk * dec, 0.0)
                qd_ref[:, cs] = eg * qq
                kd_ref[:, cs] = jnp.exp(l_c - g_c) * kk_
                ga, gb = _gam_rows(l_r)
                gam_ref[0, qh, 2 * j:2 * j + 1, :] = ga
                gam_ref[0, qh, 2 * j + 1:2 * j + 2, :] = gb
                rhs = jnp.concatenate([b_c * v_ref[:, cs], b_c * eg * kk_], axis=1)
                heads.append((cs, jnp.where(ms, b_c * kk * dec, 0.0), rhs))
        tinvs = _inv_unit_lower_many([a for _, a, _ in heads])
        uws = [_dot3(t, rhs) for t, (_, _, rhs) in zip(tinvs, heads)]
        for t, uw, (cs, _, _) in zip(tinvs, uws, heads):
            ti_ref[:, cs] = t
            u_ref[:, cs] = uw[:, :HD]
            w_ref[:, cs] = uw[:, HD:]

    big = jax.ShapeDtypeStruct((s, NV * HD), F32)
    return pl.pallas_call(
        body, name="dn_intra", grid=(nb, NQ // qps),
        in_specs=[blk, blk, blk2, pl.BlockSpec((3 * NV, 2 * CH), lambda i, h: (0, i))],
        out_specs=[blk2] * 6 + [pl.BlockSpec((1, qps, 8, 128), lambda i, h: (i, h, 0, 0))],
        out_shape=[big] * 6 + [jax.ShapeDtypeStruct((nb, NQ, 8, 128), F32)],
        compiler_params=_params(("parallel", "parallel")),
    )(q, k, v, rt)


def _dn_seq(u, w, qd, kd, p, gam, pz, nw):
    s = u.shape[0]
    nb = s // (2 * CH)
    wide = pl.BlockSpec((2 * CH, NV * HD), lambda i: (i, 0))

    def body(u_ref, w_ref, qd_ref, kd_ref, p_ref, gam_ref, z_ref, nw_ref, o_ref, vn_ref, ss_ref, y_ref, yt_ref, st):
        @pl.when(pl.program_id(0) == 0)
        def _():
            st[...] = jnp.zeros_like(st)
        ra, rb = slice(0, CH), slice(CH, 2 * CH)
        for g0 in range(0, NV, SEQ_GROUP):
            hs = list(range(g0, g0 + SEQ_GROUP))
            cs = [slice(h * HD, (h + 1) * HD) for h in hs]
            ga = [gam_ref[0, h // 2, 2 * (h % 2):2 * (h % 2) + 1, :] for h in hs]
            gb = [gam_ref[0, h // 2, 2 * (h % 2) + 1:2 * (h % 2) + 2, :] for h in hs]
            s0 = [st[h] for h in hs]
            vna = [u_ref[ra, c] - _dot(w_ref[ra, c], s) for c, s in zip(cs, s0)]
            s1 = [g * s + _dot_tn(kd_ref[ra, c], v) for g, s, c, v in zip(ga, s0, cs, vna)]
            vnb = [u_ref[rb, c] - _dot(w_ref[rb, c], s) for c, s in zip(cs, s1)]
            s2 = [g * s + _dot_tn(kd_ref[rb, c], v) for g, s, c, v in zip(gb, s1, cs, vnb)]
            for h, s in zip(hs, s2):
                st[h] = s
            oa = [_dot(qd_ref[ra, c], s) for c, s in zip(cs, s0)]
            ob = [_dot(qd_ref[rb, c], s) for c, s in zip(cs, s1)]
            for i_, h in enumerate(hs):
                vn = jnp.concatenate([vna[i_], vnb[i_]], axis=0)
                oh = jnp.concatenate([oa[i_], ob[i_]], axis=0) + _dot(p_ref[:, cs[i_]], vn)
                o_ref[:, cs[i_]] = oh
                vn_ref[:, cs[i_]] = vn
                ss_ref[h, 0:HD, :] = s0[i_]
                ss_ref[h, HD:2 * HD, :] = s1[i_]
                z = z_ref[:, cs[i_]]
                rs = lax.rsqrt(jnp.mean(oh * oh, axis=-1, keepdims=True) + RMS_EPS)
                y = oh * rs * nw_ref[...] * (z * _sigmoid(z))
                y_ref[:, cs[i_]] = y.astype(MXU_DT)
                yt_ref[cs[i_], :] = y.T.astype(MXU_DT)

    big = jax.ShapeDtypeStruct((s, NV * HD), F32)
    return pl.pallas_call(
        body, name="dn_seq", grid=(nb,),
        in_specs=[wide] * 5 + [pl.BlockSpec((1, NQ, 8, 128), lambda i: (i, 0, 0, 0)), wide, _whole((1, HD))],
        out_specs=[wide, wide, pl.BlockSpec((NV, 2 * HD, HD), lambda i: (0, i, 0)), wide, _cols(2 * CH, NV * HD)],
        out_shape=[big, big, jax.ShapeDtypeStruct((NV, 2 * s, HD), F32),
                   jax.ShapeDtypeStruct((s, NV * HD), MXU_DT), jax.ShapeDtypeStruct((NV * HD, s), MXU_DT)],
        scratch_shapes=[pltpu.VMEM((NV, HD, HD), F32)],
        compiler_params=_params(("arbitrary",)),
    )(u, w, qd, kd, p, gam, pz, nw)


def _dn_post(o, pz, nw):
    s = o.shape[0]
    tm = _pick(s, (256, 128))

    def body(o_ref, z_ref, nw_ref, y_ref, yt_ref):
        for h in range(NV):
            cs = slice(h * HD, (h + 1) * HD)
            oh = o_ref[:, cs]
            z = z_ref[:, cs]
            rs = lax.rsqrt(jnp.mean(oh * oh, axis=-1, keepdims=True) + RMS_EPS)
            y = oh * rs * nw_ref[...] * (z * _sigmoid(z))
            y_ref[:, cs] = y.astype(MXU_DT)
            yt_ref[cs, :] = y.T.astype(MXU_DT)

    return pl.pallas_call(
        body, name="dn_post", grid=(s // tm,),
        in_specs=[_rows(tm, NV * HD), _rows(tm, NV * HD), _whole((1, HD))],
        out_specs=[_rows(tm, NV * HD), _cols(tm, NV * HD)],
        out_shape=[jax.ShapeDtypeStruct((s, NV * HD), MXU_DT), jax.ShapeDtypeStruct((NV * HD, s), MXU_DT)],
        compiler_params=_params(("parallel",)),
    )(o, pz, nw)


def _merge(pg, ya, yb):
    s = ya.shape[0]
    tm = _pick(s, (512, 256, 128))

    def body(ga_ref, gb_ref, ya_ref, yb_ref, o_ref, ot_ref):
        m = _sigmoid(ga_ref[...]) * ya_ref[...] + _sigmoid(gb_ref[...]) * yb_ref[...]
        o_ref[...] = m.astype(MXU_DT)
        ot_ref[...] = m.T.astype(MXU_DT)

    return pl.pallas_call(
        body, name="merge", grid=(s // tm,),
        in_specs=[_rows(tm, D, 0), _rows(tm, D, 1), _rows(tm, D), _rows(tm, D)],
        out_specs=[_rows(tm, D), _cols(tm, D)],
        out_shape=[jax.ShapeDtypeStruct((s, D), MXU_DT), jax.ShapeDtypeStruct((D, s), MXU_DT)],
        compiler_params=_params(("parallel",)),
    )(pg, pg, ya, yb)


def _ln1(x, mix, adam, lng, lnb):
    s = x.shape[0]
    tm = _pick(s, (512, 256, 128))

    def body(x_ref, m_ref, ada_ref, g_ref, b_ref, r_ref, x2_ref, h2_ref, h2t_ref):
        r = ALPHA * x_ref[...] + (1.0 + ada_ref[2:3, :]) * m_ref[...]
        xhat, _ = _ln_stats(r)
        x2 = xhat * g_ref[...] + b_ref[...]
        r_ref[...] = r
        x2_ref[...] = x2
        h2 = x2 * (1.0 + ada_ref[4:5, :]) + ada_ref[3:4, :]
        h2_ref[...] = h2.astype(MXU_DT)
        h2t_ref[...] = h2.T.astype(MXU_DT)

    return pl.pallas_call(
        body, name="ln1", grid=(s // tm,),
        in_specs=[_rows(tm, D), _rows(tm, D), _whole((8, D)), _whole((1, D)), _whole((1, D))],
        out_specs=[_rows(tm, D)] * 3 + [_cols(tm, D)],
        out_shape=[jax.ShapeDtypeStruct((s, D), F32), jax.ShapeDtypeStruct((s, D), F32),
                   jax.ShapeDtypeStruct((s, D), MXU_DT), jax.ShapeDtypeStruct((D, s), MXU_DT)],
        compiler_params=_params(("parallel",)),
    )(x, mix, adam, lng, lnb)


def _ffn_act(gu, cw, cb):
    s = gu.shape[0]
    tm = _pick(s, (256, 128))
    w = D_FF

    def body(g_ref, u_ref, cw_ref, cb_ref, o_ref, ot_ref, gc_ref, prev):
        @pl.when(pl.program_id(0) == 0)
        def _():
            prev[...] = jnp.zeros_like(prev)
        g = g_ref[...]
        gc = _conv_causal(g, prev[...], cw_ref, 3) + cb_ref[...]
        prev[...] = g[tm - 8:, :]
        gc_ref[...] = gc
        act = _gelu(gc) * u_ref[...]
        o_ref[...] = act.astype(MXU_DT)
        ot_ref[...] = act.T.astype(MXU_DT)

    return pl.pallas_call(
        body, name="ffn_act", grid=(s // tm,),
        in_specs=[_rows(tm, w, 0), _rows(tm, w, 1), _whole((8, w)), _whole((1, w))],
        out_specs=[_rows(tm, w), _cols(tm, w), _rows(tm, w)],
        out_shape=[jax.ShapeDtypeStruct((s, w), MXU_DT), jax.ShapeDtypeStruct((w, s), MXU_DT),
                   jax.ShapeDtypeStruct((s, w), F32)],
        scratch_shapes=[pltpu.VMEM((8, w), F32)],
        compiler_params=_params(("arbitrary",)),
    )(gu, gu, cw, cb)


def _ln2_loss(x2, ff, tgt, adam, lng, lnb):
    s = x2.shape[0]
    tm = _pick(s, (512, 256, 128))

    def body(x_ref, f_ref, t_ref, ada_ref, g_ref, b_ref, dff_ref, dx_ref, red_ref):
        @pl.when(pl.program_id(0) == 0)
        def _():
            red_ref[...] = jnp.zeros_like(red_ref)
        ff_ = f_ref[...]
        r = ALPHA * x_ref[...] + (1.0 + ada_ref[5:6, :]) * ff_
        xhat, rstd = _ln_stats(r)
        err = xhat * g_ref[...] + b_ref[...] - t_ref[...]
        dy = err * (1.0 / D)
        dr = _ln_bwd(dy, xhat, rstd, g_ref[...])
        dff_ref[...] = ((1.0 + ada_ref[5:6, :]) * dr).astype(MXU_DT)
        dx_ref[...] = ALPHA * dr
        red_ref[0:1, :] += _rsum(dy * xhat)
        red_ref[1:2, :] += _rsum(dy)
        red_ref[2:3, :] += _rsum(dr * ff_)
        red_ref[3:4, :] += jnp.sum(_rsum(err * err), axis=1, keepdims=True) * (0.5 / D)

    return pl.pallas_call(
        body, name="ln2_loss", grid=(s // tm,),
        in_specs=[_rows(tm, D)] * 3 + [_whole((8, D)), _whole((1, D)), _whole((1, D))],
        out_specs=[_rows(tm, D), _rows(tm, D), _whole((8, D))],
        out_shape=[jax.ShapeDtypeStruct((s, D), MXU_DT), jax.ShapeDtypeStruct((s, D), F32),
                   jax.ShapeDtypeStruct((8, D), F32)],
        compiler_params=_params(("arbitrary",)),
    )(x2, ff, tgt, adam, lng, lnb)


def _ffn_bwd(dact, gu, gc, cw):
    s = dact.shape[0]
    tm = _pick(s, (256, 128))
    nt = s // tm
    w = D_FF

    def body(da_ref, g_ref, u_ref, gc_ref, cw_ref, o_ref, gcw_ref, gcb_ref, nxt):
        @pl.when(pl.program_id(0) == 0)
        def _():
            nxt[...] = jnp.zeros_like(nxt)
            gcw_ref[...] = jnp.zeros_like(gcw_ref)
            gcb_ref[...] = jnp.zeros_like(gcb_ref)
        gel, dgel = _gelu_and_grad(gc_ref[...])
        da = da_ref[...]
        dgc = da * u_ref[...] * dgel
        o_ref[:, w:] = (da * gel).astype(MXU_DT)
        o_ref[:, :w] = _conv_causal_bwd(dgc, nxt[...], cw_ref, 3, g_ref[...], gcw_ref).astype(MXU_DT)
        nxt[...] = dgc[:8, :]
        gcb_ref[...] += _rsum(dgc)

    return pl.pallas_call(
        body, name="ffn_bwd", grid=(nt,),
        in_specs=[_rows(tm, w, 0, nt), _rows(tm, w, 0, nt), _rows(tm, w, 1, nt), _rows(tm, w, 0, nt),
                  _whole((8, w))],
        out_specs=[_rows(tm, 2 * w, 0, nt), _whole((8, w)), _whole((1, w))],
        out_shape=[jax.ShapeDtypeStruct((s, 2 * w), MXU_DT), jax.ShapeDtypeStruct((8, w), F32),
                   jax.ShapeDtypeStruct((1, w), F32)],
        scratch_shapes=[pltpu.VMEM((8, w), F32)],
        compiler_params=_params(("arbitrary",)),
    )(dact, gu, gu, gc, cw)


def _ln1_bwd(dh2, dx2a, x2, r1, mix, adam, lng):
    s = dh2.shape[0]
    tm = _pick(s, (512, 256, 128))

    def body(dh_ref, dxa_ref, x2_ref, r_ref, m_ref, ada_ref, g_ref, dm_ref, dx_ref, red_ref):
        @pl.when(pl.program_id(0) == 0)
        def _():
            red_ref[...] = jnp.zeros_like(red_ref)
        dh = dh_ref[...]
        dx2 = dxa_ref[...] + dh * (1.0 + ada_ref[4:5, :])
        xhat, rstd = _ln_stats(r_ref[...])
        dr = _ln_bwd(dx2, xhat, rstd, g_ref[...])
        dm_ref[...] = ((1.0 + ada_ref[2:3, :]) * dr).astype(MXU_DT)
        dx_ref[...] = ALPHA * dr
        red_ref[0:1, :] += _rsum(dh * x2_ref[...])
        red_ref[1:2, :] += _rsum(dh)
        red_ref[2:3, :] += _rsum(dx2 * xhat)
        red_ref[3:4, :] += _rsum(dx2)
        red_ref[4:5, :] += _rsum(dr * m_ref[...])

    return pl.pallas_call(
        body, name="ln1_bwd", grid=(s // tm,),
        in_specs=[_rows(tm, D)] * 5 + [_whole((8, D)), _whole((1, D))],
        out_specs=[_rows(tm, D), _rows(tm, D), _whole((8, D))],
        out_shape=[jax.ShapeDtypeStruct((s, D), MXU_DT), jax.ShapeDtypeStruct((s, D), F32),
                   jax.ShapeDtypeStruct((8, D), F32)],
        compiler_params=_params(("arbitrary",)),
    )(dh2, dx2a, x2, r1, mix, adam, lng)


def _merge_bwd(dmg, pg, ya, yb):
    s = dmg.shape[0]
    tm = _pick(s, (512, 256, 128))

    def body(d_ref, ga_ref, gb_ref, ya_ref, yb_ref, dya_ref, dyb_ref, dpg_ref):
        d = d_ref[...]
        sa = _sigmoid(ga_ref[...])
        sb = _sigmoid(gb_ref[...])
        dya_ref[...] = (d * sa).astype(MXU_DT)
        dyb_ref[...] = (d * sb).astype(MXU_DT)
        dpg_ref[:, :D] = (d * ya_ref[...] * sa * (1.0 - sa)).astype(MXU_DT)
        dpg_ref[:, D:] = (d * yb_ref[...] * sb * (1.0 - sb)).astype(MXU_DT)

    return pl.pallas_call(
        body, name="merge_bwd", grid=(s // tm,),
        in_specs=[_rows(tm, D), _rows(tm, D, 0), _rows(tm, D, 1), _rows(tm, D), _rows(tm, D)],
        out_specs=[_rows(tm, D), _rows(tm, D), _rows(tm, 2 * D)],
        out_shape=[jax.ShapeDtypeStruct((s, D), MXU_DT), jax.ShapeDtypeStruct((s, D), MXU_DT),
                   jax.ShapeDtypeStruct((s, 2 * D), MXU_DT)],
        compiler_params=_params(("parallel",)),
    )(dmg, pg, pg, ya, yb)


def _dn_post_bwd(ddn, o, pz, nw):
    s = o.shape[0]
    tm = _pick(s, (256, 128))

    def body(d_ref, o_ref, z_ref, nw_ref, do_ref, dz_ref, gnw_ref):
        @pl.when(pl.program_id(0) == 0)
        def _():
            gnw_ref[...] = jnp.zeros_like(gnw_ref)
        acc = jnp.zeros((1, HD), F32)
        for h in range(NV):
            cs = slice(h * HD, (h + 1) * HD)
            oh = o_ref[:, cs]
            z = z_ref[:, cs]
            d = d_ref[:, cs]
            sg = _sigmoid(z)
            rs = lax.rsqrt(jnp.mean(oh * oh, axis=-1, keepdims=True) + RMS_EPS)
            n = oh * rs
            dz_ref[:, cs] = (d * n * nw_ref[...] * sg * (1.0 + z * (1.0 - sg))).astype(MXU_DT)
            dn_ = d * (z * sg)
            acc = acc + _rsum(dn_ * n)
            dnn = dn_ * nw_ref[...]
            do_ref[:, cs] = rs * (dnn - n * jnp.mean(dnn * n, axis=-1, keepdims=True))
        gnw_ref[...] += acc

    return pl.pallas_call(
        body, name="dn_post_bwd", grid=(s // tm,),
        in_specs=[_rows(tm, NV * HD)] * 3 + [_whole((1, HD))],
        out_specs=[_rows(tm, NV * HD), _rows(tm, NV * HD), _whole((1, HD))],
        out_shape=[jax.ShapeDtypeStruct((s, NV * HD), F32), jax.ShapeDtypeStruct((s, NV * HD), MXU_DT),
                   jax.ShapeDtypeStruct((1, HD), F32)],
        compiler_params=_params(("arbitrary",)),
    )(ddn, o, pz, nw)


def _dn_seq_bwd(do, qd, kd, p, w, vn, ssave, gam):
    s = do.shape[0]
    nb = s // (2 * CH)
    wide = pl.BlockSpec((2 * CH, NV * HD), lambda i: (nb - 1 - i, 0))
    gspec = pl.BlockSpec((1, NQ, 8, 128), lambda i: (nb - 1 - i, 0, 0, 0))

    def body(do_ref, qd_ref, kd_ref, p_ref, w_ref, vn_ref, ss_ref, gam_ref, dvn_ref, dkd_ref, dgam_ref, dst):
        @pl.when(pl.program_id(0) == 0)
        def _():
            dst[...] = jnp.zeros_like(dst)
        dgam_ref[...] = jnp.zeros_like(dgam_ref)
        ra, rb = slice(0, CH), slice(CH, 2 * CH)
        tot = lambda t: jnp.sum(jnp.sum(t, axis=1, keepdims=True), axis=0, keepdims=True)
        for g0 in range(0, NV, SEQ_GROUP):
            hs = list(range(g0, g0 + SEQ_GROUP))
            cs = [slice(h * HD, (h + 1) * HD) for h in hs]
            ga = [gam_ref[0, h // 2, 2 * (h % 2):2 * (h % 2) + 1, :] for h in hs]
            gb = [gam_ref[0, h // 2, 2 * (h % 2) + 1:2 * (h % 2) + 2, :] for h in hs]
            ds2 = [dst[h] for h in hs]
            pdo = [_dot_tn(p_ref[:, c], do_ref[:, c]) for c in cs]
            qdo_b = [_dot_tn(qd_ref[rb, c], do_ref[rb, c]) for c in cs]
            qdo_a = [_dot_tn(qd_ref[ra, c], do_ref[ra, c]) for c in cs]
            dvb = [p_[rb] + _dot(kd_ref[rb, c], d_) for p_, c, d_ in zip(pdo, cs, ds2)]
            ds1 = [g * d_ + q_ - _dot_tn(w_ref[rb, c], v_)
                   for g, d_, q_, c, v_ in zip(gb, ds2, qdo_b, cs, dvb)]
            dva = [p_[ra] + _dot(kd_ref[ra, c], d_) for p_, c, d_ in zip(pdo, cs, ds1)]
            ds0 = [g * d_ + q_ - _dot_tn(w_ref[ra, c], v_)
                   for g, d_, q_, c, v_ in zip(ga, ds1, qdo_a, cs, dva)]
            for h, d_ in zip(hs, ds0):
                dst[h] = d_
            for i_, h in enumerate(hs):
                c = cs[i_]
                row = 2 * (h % 2)
                dkd_ref[rb, c] = _dot_nt(vn_ref[rb, c], ds2[i_])
                dkd_ref[ra, c] = _dot_nt(vn_ref[ra, c], ds1[i_])
                dvn_ref[ra, c] = dva[i_]
                dvn_ref[rb, c] = dvb[i_]
                dgam_ref[0, h // 2, row:row + 1, :] = jnp.broadcast_to(tot(ds1[i_] * ss_ref[h, 0:HD, :]), (1, 128))
                dgam_ref[0, h // 2, row + 1:row + 2, :] = jnp.broadcast_to(
                    tot(ds2[i_] * ss_ref[h, HD:2 * HD, :]), (1, 128))

    big = jax.ShapeDtypeStruct((s, NV * HD), F32)
    return pl.pallas_call(
        body, name="dn_seq_bwd", grid=(nb,),
        in_specs=[wide] * 6 + [pl.BlockSpec((NV, 2 * HD, HD), lambda i: (0, nb - 1 - i, 0)), gspec],
        out_specs=[wide, wide, gspec],
        out_shape=[big, big, jax.ShapeDtypeStruct((nb, NQ, 8, 128), F32)],
        scratch_shapes=[pltpu.VMEM((NV, HD, HD), F32)],
        compiler_params=_params(("arbitrary",)),
    )(do, qd, kd, p, w, vn, ssave, gam)


def _dn_intra_bwd(q, k, v, rt, do, dvn, dkd, gam, dgam, ssave, tinv, u, w, vn):
    s = q.shape[0]
    nb = s // (2 * CH)
    qps = 4
    nh = 2 * qps
    blk = pl.BlockSpec((2 * CH, qps * HD), lambda i, h: (i, h))
    blk2 = pl.BlockSpec((2 * CH, nh * HD), lambda i, h: (i, h))
    gspec = pl.BlockSpec((1, qps, 8, 128), lambda i, h: (i, h, 0, 0))
    rspec = pl.BlockSpec((3 * NV, 2 * CH), lambda i, h: (0, i))

    def body(q_ref, k_ref, v_ref, rt_ref, do_ref, dvn_ref, dkd_ref, gam_ref, dgam_ref, ss_ref,
             ti_ref, u_ref, w_ref, vn_ref, dq_ref, dk_ref, dv_ref, drt_ref, acc):
        hstep = pl.program_id(1)

        @pl.when(hstep == 0)
        def _():
            acc[...] = jnp.zeros_like(acc)
        mc, ms = _pair_masks()
        ra, rb = slice(0, CH), slice(CH, 2 * CH)
        lane = _iota((1, 2 * CH), 1)
        hs = list(range(nh))
        qh = [h // 2 for h in hs]
        cs = [slice(h * HD, (h + 1) * HD) for h in hs]
        qq_ = [q_ref[:, t * HD:(t + 1) * HD] for t in range(qps)]
        kk_ = [k_ref[:, t * HD:(t + 1) * HD] for t in range(qps)]
        kk = [_dot_nt(k_, k_) for k_ in kk_]
        qk = [_dot_nt(q_, k_) for q_, k_ in zip(qq_, kk_)]
        cols = [_head_cols(rt_ref, nh * hstep + h) for h in hs]
        b_c = [c_[2] for c_ in cols]
        dec = [jnp.where(mc, jnp.exp(jnp.where(mc, c_[1] - c_[0], 0.0)), 0.0) for c_ in cols]
        eg = [jnp.exp(c_[1]) for c_ in cols]
        egl = [jnp.exp(c_[4] - c_[1]) for c_ in cols]
        dob = [do_ref[:, c] for c in cs]
        dvb = [dvn_ref[:, c] for c in cs]
        dqd = [jnp.concatenate([_dot_nt(d_[ra], ss_ref[h, 0:HD, :]), _dot_nt(d_[rb], ss_ref[h, HD:2 * HD, :])], axis=0)
               for h, d_ in zip(hs, dob)]
        dw = [-jnp.concatenate([_dot_nt(d_[ra], ss_ref[h, 0:HD, :]), _dot_nt(d_[rb], ss_ref[h, HD:2 * HD, :])], axis=0)
              for h, d_ in zip(hs, dvb)]
        dp = [jnp.where(mc, _dot_nt(d_, vn_ref[:, c]), 0.0) for d_, c in zip(dob, cs)]
        dbuw = [_dot3_tn(ti_ref[:, c], jnp.concatenate([d_, w_], axis=1)) for c, d_, w_ in zip(cs, dvb, dw)]
        dbu = [t[:, :HD] for t in dbuw]
        dbw = [t[:, HD:] for t in dbuw]
        da = [jnp.where(ms, -(_dot_nt(bu, u_ref[:, c]) + _dot_nt(bw, w_ref[:, c])), 0.0)
              for bu, bw, c in zip(dbu, dbw, cs)]
        dm = [a_ * d_ for a_, d_ in zip(da, dec)]
        dn_ = [p_ * d_ for p_, d_ in zip(dp, dec)]
        dbk = [_dot(m_, kk_[t]) for m_, t in zip(dm, qh)]
        dqs = [_dot(n_, kk_[t]) + e_ * q_ for n_, t, e_, q_ in zip(dn_, qh, eg, dqd)]
        dks = [_dot_tn(m_, b_ * kk_[t]) + _dot_tn(n_, qq_[t]) + el * dkd_ref[:, c] + b_ * (e_ * bw + bk)
               for m_, b_, t, n_, el, c, e_, bw, bk in zip(dm, b_c, qh, dn_, egl, cs, eg, dbw, dbk)]
        for t in range(qps):
            dq_ref[:, t * HD:(t + 1) * HD] = dqs[2 * t] + dqs[2 * t + 1]
            dk_ref[:, t * HD:(t + 1) * HD] = dks[2 * t] + dks[2 * t + 1]
        for h in hs:
            c, t, j = cs[h], qh[h], h % 2
            dv_ref[:, c] = b_c[h] * dbu[h]
            e = da[h] * (b_c[h] * kk[t] * dec[h]) + dp[h] * (qk[t] * dec[h])
            x = dkd_ref[:, c] * (egl[h] * kk_[t])
            egk = eg[h] * kk_[t]
            z = e + dqd[h] * (eg[h] * qq_[t]) - x + dbw[h] * (b_c[h] * egk)
            zb = dbw[h] * egk + dbu[h] * v_ref[:, c] + dbk[h] * kk_[t]
            sa = jnp.sum(jnp.sum(x[ra], axis=1, keepdims=True), axis=0, keepdims=True)
            sb = jnp.sum(jnp.sum(x[rb], axis=1, keepdims=True), axis=0, keepdims=True)
            la = sa + dgam_ref[0, t, 2 * j:2 * j + 1, :] * gam_ref[0, t, 2 * j:2 * j + 1, :]
            lb = sb + dgam_ref[0, t, 2 * j + 1:2 * j + 2, :] * gam_ref[0, t, 2 * j + 1:2 * j + 2, :]
            hg = nh * hstep + h
            acc[pl.ds(hg, 1), :] = _rsum(z.T - e)
            acc[pl.ds(NV + hg, 1), :] = _rsum(zb.T)
            acc[pl.ds(2 * NV + hg, 1), :] = jnp.where(lane < CH, la, lb)

        @pl.when(hstep == NQ // qps - 1)
        def _():
            drt_ref[...] = acc[...]

    return pl.pallas_call(
        body, name="dn_intra_bwd", grid=(nb, NQ // qps),
        in_specs=[blk, blk, blk2, rspec, blk2, blk2, blk2, gspec, gspec,
                  pl.BlockSpec((nh, 2 * HD, HD), lambda i, h: (h, i, 0)), blk2, blk2, blk2, blk2],
        out_specs=[blk, blk, blk2, rspec],
        out_shape=[jax.ShapeDtypeStruct((s, NQ * HD), F32), jax.ShapeDtypeStruct((s, NQ * HD), F32),
                   jax.ShapeDtypeStruct((s, NV * HD), F32), jax.ShapeDtypeStruct((3 * NV, s), F32)],
        scratch_shapes=[pltpu.VMEM((3 * NV, 2 * CH), F32)],
        compiler_params=_params(("parallel", "arbitrary")),
    )(q, k, v, rt, do, dvn, dkd, gam, dgam, ssave, tinv, u, w, vn)


def _l2n_heads_bwd(c, d_ref, dc_ref, off, scale):
    for hh in range(NQ):
        cs = slice(off + hh * HD, off + (hh + 1) * HD)
        x = c[:, cs]
        dy = d_ref[:, hh * HD:(hh + 1) * HD]
        r = lax.rsqrt(jnp.sum(x * x, axis=-1, keepdims=True) + L2_EPS)
        dc_ref[:, cs] = (scale * r) * (dy - x * (r * r) * jnp.sum(dy * x, axis=-1, keepdims=True))


def _dn_prep_bwd(dq, dk, dv, drt, pqkv, cpre, pab, cw, acol, dcol):
    s = pqkv.shape[0]
    tm = 128
    nt = s // tm
    wq = NQ * HD

    def body(dq_ref, dk_ref, dv_ref, drt_ref, x_ref, cp_ref, ab_ref, cw_ref, ac_ref,
             dc_ref, dx_ref, dab_ref, gcw_ref, gsc_ref, dcs, nxt):
        @pl.when(pl.program_id(0) == 0)
        def _():
            nxt[...] = jnp.zeros_like(nxt)
            gcw_ref[...] = jnp.zeros_like(gcw_ref)
            gsc_ref[...] = jnp.zeros_like(gsc_ref)
        cp = cp_ref[...]
        sg = _sigmoid(cp)
        c = cp * sg
        _l2n_heads_bwd(c, dq_ref, dcs, 0, Q_SCALE)
        _l2n_heads_bwd(c, dk_ref, dcs, wq, 1.0)
        dcs[:, 2 * wq:] = dv_ref[...]
        dcp = dcs[...] * (sg * (1.0 + cp * (1.0 - sg)))
        dx_ref[...] = _conv_causal_bwd(dcp, nxt[...], cw_ref, 4, x_ref[...], gcw_ref).astype(MXU_DT)
        nxt[...] = dcp[:8, :]
        lane = _iota((NV, tm), 1)
        dgt = drt_ref[0:NV, :] + jnp.where((lane & (CH - 1)) == CH - 1, drt_ref[2 * NV:3 * NV, :], 0.0)
        dg = _chunk_cumsum(dgt, 1, rev=True)
        abt = ab_ref[...].T
        zt = abt[0:NV, :] + dc_ref[...]
        gt = -jnp.exp(ac_ref[...]) * _softplus(zt)
        dat = dg * (-jnp.exp(ac_ref[...])) * _sigmoid(zt)
        bt = _sigmoid(abt[NV:2 * NV, :])
        dbt = drt_ref[NV:2 * NV, :] * bt * (1.0 - bt)
        full = jnp.concatenate([dat, dbt, jnp.zeros((128 - 2 * NV, tm), F32)], axis=0)
        dab_ref[...] = full.T.astype(MXU_DT)
        l2 = _iota((NV, 128), 1)
        gsc_ref[...] += jnp.where(l2 == 0, jnp.sum(dg * gt, axis=1, keepdims=True),
                                  jnp.where(l2 == 1, jnp.sum(dat, axis=1, keepdims=True), 0.0))

    return pl.pallas_call(
        body, name="dn_prep_bwd", grid=(nt,),
        in_specs=[_rows(tm, wq, 0, nt), _rows(tm, wq, 0, nt), _rows(tm, 2 * wq, 0, nt),
                  pl.BlockSpec((3 * NV, tm), lambda i: (0, nt - 1 - i)),
                  _rows(tm, 4 * wq, 0, nt), _rows(tm, 4 * wq, 0, nt), _rows(tm, 128, 0, nt),
                  _whole((8, 4 * wq)), _whole((NV, 1)), _whole((NV, 1))],
        out_specs=[_rows(tm, 4 * wq, 0, nt), _rows(tm, 128, 0, nt), _whole((8, 4 * wq)), _whole((NV, 128))],
        out_shape=[jax.ShapeDtypeStruct((s, 4 * wq), MXU_DT), jax.ShapeDtypeStruct((s, 128), MXU_DT),
                   jax.ShapeDtypeStruct((8, 4 * wq), F32), jax.ShapeDtypeStruct((NV, 128), F32)],
        scratch_shapes=[pltpu.VMEM((tm, 4 * wq), F32), pltpu.VMEM((8, 4 * wq), F32)],
        compiler_params=_params(("arbitrary",)),
    )(dq, dk, dv, drt, pqkv, cpre, pab, cw, acol, dcol)


def _rg_bwd(drec, px, xcs, h, cw, vec, wa, wx, wat, wxt, side=None):
    s = px.shape[0]
    tm = _pick(s, (256, 128))
    nt = s // tm
    w = D_RNN
    n_in = len(side.ins) if side else 0
    n_out = len(side.out_shapes) if side else 0

    def body(*refs):
        (dr_ref, xr_ref, xc_ref, gr_ref, h_ref, hb_ref, cw_ref, vec_ref, wa_ref, wx_ref, wat_ref,
         wxt_ref) = refs[:12]
        s_in = refs[12:12 + n_in]
        o_ref, gwa_ref, gwx_ref, gcw_ref, gvec_ref = refs[12 + n_in:17 + n_in]
        s_out = refs[17 + n_in:17 + n_in + n_out]
        nxt_a, nxt_l, nxt_d = refs[17 + n_in + n_out:20 + n_in + n_out]
        sems = refs[20 + n_in + n_out:]
        i = pl.program_id(0)
        if side:
            @pl.when(i == 0)
            def _():
                side.start(s_in, s_out, sems)

        @pl.when(i == 0)
        def _():
            nxt_a[...] = jnp.zeros_like(nxt_a)
            nxt_l[...] = jnp.zeros_like(nxt_l)
            nxt_d[...] = jnp.zeros_like(nxt_d)
            gwa_ref[...] = jnp.zeros_like(gwa_ref)
            gwx_ref[...] = jnp.zeros_like(gwx_ref)
            gcw_ref[...] = jnp.zeros_like(gcw_ref)
            gvec_ref[...] = jnp.zeros_like(gvec_ref)
        hbefore = jnp.where(i == nt - 1, 0.0, hb_ref[...])
        xc = xc_ref[...]
        r, ig, a, mult, sp, rmult = _rg_gates(xc, wa_ref, wx_ref, vec_ref)
        hh = h_ref[...]
        gel, dgel = _gelu_and_grad(gr_ref[...])
        drec_ = dr_ref[...]
        o_ref[:, w:] = (drec_ * hh * dgel).astype(MXU_DT)
        acum, lam = _scan_rev(_shift_up(a, nxt_a[...], 1), drec_ * gel)
        lam = lam + acum * nxt_l[0:1, :]
        nxt_a[...] = a[:8, :]
        nxt_l[...] = lam[:8, :]
        da = lam * _shift_down(hh, hbefore, 1)
        dxc = lam * mult * ig
        dla = da * a - (lam * ig * xc) * (a * a) * rmult
        dpr = dla * (-RG_C * sp) * r * (1.0 - r)
        dpi = (lam * mult * xc) * ig * (1.0 - ig)
        dprb = dpr.astype(MXU_DT)
        dpib = dpi.astype(MXU_DT)
        dxc = dxc + jnp.dot(dprb, wat_ref[...], preferred_element_type=F32) \
                  + jnp.dot(dpib, wxt_ref[...], preferred_element_type=F32)
        xcb = xc.astype(MXU_DT)
        gwa_ref[...] += _dot_tn(xcb, dprb)
        gwx_ref[...] += _dot_tn(xcb, dpib)
        o_ref[:, :w] = _conv_causal_bwd(dxc, nxt_d[...], cw_ref, 4, xr_ref[...], gcw_ref).astype(MXU_DT)
        nxt_d[...] = dxc[:8, :]
        gvec_ref[0:1, :] += _rsum(dxc)
        gvec_ref[1:2, :] += _rsum(dpr)
        gvec_ref[2:3, :] += _rsum(dpi)
        gvec_ref[3:4, :] += _rsum(dla * (-RG_C * r)) * (-_sigmoid(-vec_ref[3:4, :]))
        if side:
            @pl.when(i == nt - 1)
            def _():
                side.finish(s_in, s_out, sems)

    return pl.pallas_call(
        body, name="rg_bwd", grid=(nt,),
        in_specs=[_rows(tm, w, 0, nt), _rows(tm, w, 0, nt), _rows(tm, w, 0, nt), _rows(tm, w, 1, nt),
                  _rows(tm, w, 0, nt), _before(tm, w, 0, nt), _whole((8, w)), _whole((8, w)),
                  _whole((w, w)), _whole((w, w)), _whole((w, w)), _whole((w, w))] + [_HBM] * n_in,
        out_specs=[_rows(tm, 2 * w, 0, nt), _whole((w, w)), _whole((w, w)), _whole((8, w)), _whole((8, w))]
        + [_HBM] * n_out,
        out_shape=[jax.ShapeDtypeStruct((s, 2 * w), MXU_DT), jax.ShapeDtypeStruct((w, w), F32),
                   jax.ShapeDtypeStruct((w, w), F32), jax.ShapeDtypeStruct((8, w), F32),
                   jax.ShapeDtypeStruct((8, w), F32)] + (list(side.out_shapes) if side else []),
        scratch_shapes=[pltpu.VMEM((8, w), F32)] * 3 + (list(side.sems) if side else []),
        compiler_params=_params(("arbitrary",)),
    )(drec, px, xcs, px, h, h, cw, vec, wa, wx, wat, wxt, *(side.ins if side else []))


def _modulate_bwd(dh1, dxa, x, adam):
    s = x.shape[0]
    tm = _pick(s, (512, 256, 128))

    def body(dh_ref, dxa_ref, x_ref, ada_ref, gx_ref, red_ref):
        @pl.when(pl.program_id(0) == 0)
        def _():
            red_ref[...] = jnp.zeros_like(red_ref)
        dh = dh_ref[...]
        gx_ref[...] = dxa_ref[...] + dh * (1.0 + ada_ref[1:2, :])
        red_ref[0:1, :] += _rsum(dh * x_ref[...])
        red_ref[1:2, :] += _rsum(dh)

    return pl.pallas_call(
        body, name="modulate1_bwd", grid=(s // tm,),
        in_specs=[_rows(tm, D)] * 3 + [_whole((8, D))],
        out_specs=[_rows(tm, D), _whole((8, D))],
        out_shape=[jax.ShapeDtypeStruct((s, D), F32), jax.ShapeDtypeStruct((8, D), F32)],
        compiler_params=_params(("arbitrary",)),
    )(dh1, dxa, x, adam)


def _mm_epi(a, b, *, name, epi, extras=(), wholes=(), row_outs=(), acc_outs=(), tm=None, side=None):
    m, kk = a.shape
    n = b.shape[1]
    tm = tm or _pick(m, (512, 256, 128))
    tk = kk if kk <= 2816 else _pick(kk, (2816, 2176, 2048, 1024))
    nk, ni = kk // tk, m // tm
    n_ex, n_wh, n_ro, n_ao = len(extras), len(wholes), len(row_outs), len(acc_outs)
    n_si = len(side.ins) if side else 0
    n_so = len(side.out_shapes) if side else 0

    def body(*refs):
        a_ref, b_ref = refs[:2]
        p = 2
        ex, p = refs[p:p + n_ex], p + n_ex
        wh, p = refs[p:p + n_wh], p + n_wh
        s_in, p = refs[p:p + n_si], p + n_si
        ro, p = refs[p:p + n_ro], p + n_ro
        ao, p = refs[p:p + n_ao], p + n_ao
        s_out, p = refs[p:p + n_so], p + n_so
        i, k = pl.program_id(0), pl.program_id(1)
        first = (i == 0) & (k == 0)
        if side:
            sems = refs[p + (1 if nk > 1 else 0):]

            @pl.when(first)
            def _():
                side.start(s_in, s_out, sems)
        if n_ao:
            @pl.when(first)
            def _():
                for r_ in ao:
                    r_[...] = jnp.zeros_like(r_)
        if nk == 1:
            epi(_dot(a_ref[...], b_ref[...]), ex, wh, ro, ao)
        else:
            acc = refs[p]

            @pl.when(k == 0)
            def _():
                acc[...] = jnp.zeros_like(acc)
            acc[...] += _dot(a_ref[...], b_ref[...])

            @pl.when(k == nk - 1)
            def _():
                epi(acc[...], ex, wh, ro, ao)
        if side:
            @pl.when((i == ni - 1) & (k == nk - 1))
            def _():
                side.finish(s_in, s_out, sems)

    in_specs = [pl.BlockSpec((tm, tk), lambda i, k: (i, k)), pl.BlockSpec((tk, n), lambda i, k: (k, 0))]
    in_specs += [pl.BlockSpec((tm, w_), functools.partial(lambda i, k, c_: (i, c_), c_=c_)) for _, w_, c_ in extras]
    in_specs += [_whole(x_.shape) for x_ in wholes] + [_HBM] * n_si
    out_specs = [pl.BlockSpec((tm, w_), lambda i, k: (i, 0)) for w_, _ in row_outs]
    out_specs += [_whole(sh) for sh in acc_outs] + [_HBM] * n_so
    out_shape = [jax.ShapeDtypeStruct((m, w_), dt) for w_, dt in row_outs]
    out_shape += [jax.ShapeDtypeStruct(sh, F32) for sh in acc_outs] + (list(side.out_shapes) if side else [])
    scratch = ([pltpu.VMEM((tm, n), F32)] if nk > 1 else []) + (list(side.sems) if side else [])
    return pl.pallas_call(
        body, name=name, grid=(ni, nk), in_specs=in_specs, out_specs=out_specs, out_shape=out_shape,
        scratch_shapes=scratch, compiler_params=_params(("arbitrary", "arbitrary")),
    )(a, b, *[x_ for x_, _, _ in extras], *wholes, *(side.ins if side else []))


def _mm_pro(b, *, name, pro, m, extras=(), wholes=(), row_outs=(), col_outs=(), tm=256):
    kk, n = b.shape
    tn = _pick(n, (1408, 1024, 512, 256, 128))
    n_ex, n_wh, n_ro, n_co = len(extras), len(wholes), len(row_outs), len(col_outs)

    def body(*refs):
        ex = refs[:n_ex]
        wh = refs[n_ex:n_ex + n_wh]
        b_ref, o_ref = refs[n_ex + n_wh], refs[n_ex + n_wh + 1]
        ro = refs[n_ex + n_wh + 2:n_ex + n_wh + 2 + n_ro]
        co = refs[n_ex + n_wh + 2 + n_ro:n_ex + n_wh + 2 + n_ro + n_co]
        a_scr = refs[-1]

        @pl.when(pl.program_id(1) == 0)
        def _():
            a_scr[...] = pro(ex, wh, ro, co).astype(MXU_DT)
        o_ref[...] = jnp.dot(a_scr[...], b_ref[...], preferred_element_type=F32)

    in_specs = [pl.BlockSpec((tm, w_), functools.partial(lambda i, j, c_: (i, c_), c_=c_)) for _, w_, c_ in extras]
    in_specs += [_whole(x_.shape) for x_ in wholes] + [pl.BlockSpec((kk, tn), lambda i, j: (0, j))]
    out_specs = [pl.BlockSpec((tm, tn), lambda i, j: (i, j))]
    out_specs += [pl.BlockSpec((tm, w_), lambda i, j: (i, 0)) for w_, _ in row_outs]
    out_specs += [pl.BlockSpec((w_, tm), lambda i, j: (0, i)) for w_, _ in col_outs]
    out_shape = [jax.ShapeDtypeStruct((m, n), F32)] + [jax.ShapeDtypeStruct((m, w_), dt) for w_, dt in row_outs]
    out_shape += [jax.ShapeDtypeStruct((w_, m), dt) for w_, dt in col_outs]
    return pl.pallas_call(
        body, name=name, grid=(m // tm, n // tn), in_specs=in_specs, out_specs=out_specs, out_shape=out_shape,
        scratch_shapes=[pltpu.VMEM((tm, kk), MXU_DT)], compiler_params=_params(("arbitrary", "arbitrary")),
    )(*[x_ for x_, _, _ in extras], *wholes, b.astype(MXU_DT))


def _ln1_ffn_gu(x, mix, adam, lng, lnb, w_gu):
    def pro(ex, wh, ro, co):
        x_ref, m_ref = ex
        ada_ref, g_ref, b_ref = wh
        r = ALPHA * x_ref[...] + (1.0 + ada_ref[2:3, :]) * m_ref[...]
        xhat, _ = _ln_stats(r)
        x2 = xhat * g_ref[...] + b_ref[...]
        ro[0][...] = r
        ro[1][...] = x2
        h2 = x2 * (1.0 + ada_ref[4:5, :]) + ada_ref[3:4, :]
        co[0][...] = h2.T.astype(MXU_DT)
        return h2

    return _mm_pro(w_gu, name="ln1_ffn_gu", pro=pro, m=x.shape[0], extras=[(x, D, 0), (mix, D, 0)],
                   wholes=[adam, lng, lnb], row_outs=[(D, F32), (D, F32)], col_outs=[(D, MXU_DT)])


def _merge_proj_out(pg, ya, yb, w_out):
    def pro(ex, wh, ro, co):
        ga_ref, gb_ref, ya_ref, yb_ref = ex
        mg = _sigmoid(ga_ref[...]) * ya_ref[...] + _sigmoid(gb_ref[...]) * yb_ref[...]
        co[0][...] = mg.T.astype(MXU_DT)
        return mg

    return _mm_pro(w_out, name="merge_proj_out", pro=pro, m=ya.shape[0],
                   extras=[(pg, D, 0), (pg, D, 1), (ya, D, 0), (yb, D, 0)], col_outs=[(D, MXU_DT)])


def _ffn_down_ln2_loss(act, w_down, x2, tgt, adam, lng, lnb):
    def epi(ff_, ex, wh, ro, ao):
        x_ref, t_ref = ex
        ada_ref, g_ref, b_ref = wh
        dff_ref, dx_ref = ro
        red_ref, = ao
        r = ALPHA * x_ref[...] + (1.0 + ada_ref[5:6, :]) * ff_
        xhat, rstd = _ln_stats(r)
        err = xhat * g_ref[...] + b_ref[...] - t_ref[...]
        dy = err * (1.0 / D)
        dr = _ln_bwd(dy, xhat, rstd, g_ref[...])
        dff_ref[...] = ((1.0 + ada_ref[5:6, :]) * dr).astype(MXU_DT)
        dx_ref[...] = ALPHA * dr
        red_ref[0:1, :] += _rsum(dy * xhat)
        red_ref[1:2, :] += _rsum(dy)
        red_ref[2:3, :] += _rsum(dr * ff_)
        red_ref[3:4, :] += jnp.sum(_rsum(err * err), axis=1, keepdims=True) * (0.5 / D)

    return _mm_epi(act, w_down, name="ffn_down_ln2_loss", epi=epi, extras=[(x2, D, 0), (tgt, D, 0)],
                   wholes=[adam, lng, lnb], row_outs=[(D, MXU_DT), (D, F32)], acc_outs=[(8, D)])


def _d_h1_modulate_bwd(dproj, w_t, dxa, x, adam, side):
    def epi(dh, ex, wh, ro, ao):
        dxa_ref, x_ref = ex
        ada_ref, = wh
        ro[0][...] = dxa_ref[...] + dh * (1.0 + ada_ref[1:2, :])
        ao[0][0:1, :] += _rsum(dh * x_ref[...])
        ao[0][1:2, :] += _rsum(dh)

    return _mm_epi(dproj, w_t, name="d_h1", epi=epi, extras=[(dxa, D, 0), (x, D, 0)], wholes=[adam],
                   row_outs=[(D, F32)], acc_outs=[(8, D)], side=side)


def _d_merged_bwd(dmix, w_out_t, pg, ya, yb):
    def epi(d, ex, wh, ro, ao):
        ga_ref, gb_ref, ya_ref, yb_ref = ex
        dya_ref, dyb_ref, dpg_ref = ro
        sa = _sigmoid(ga_ref[...])
        sb = _sigmoid(gb_ref[...])
        dya_ref[...] = (d * sa).astype(MXU_DT)
        dyb_ref[...] = (d * sb).astype(MXU_DT)
        dpg_ref[:, :D] = (d * ya_ref[...] * sa * (1.0 - sa)).astype(MXU_DT)
        dpg_ref[:, D:] = (d * yb_ref[...] * sb * (1.0 - sb)).astype(MXU_DT)

    return _mm_epi(dmix, w_out_t, name="d_merged", epi=epi,
                   extras=[(pg, D, 0), (pg, D, 1), (ya, D, 0), (yb, D, 0)],
                   row_outs=[(D, MXU_DT), (D, MXU_DT), (2 * D, MXU_DT)])


def _d_dn_post_bwd(dyb, w_pb_t, o, pz, nw):
    def epi(d_all, ex, wh, ro, ao):
        o_ref, z_ref = ex
        nw_ref, = wh
        do_ref, dz_ref = ro
        acc = jnp.zeros((1, HD), F32)
        for h in range(NV):
            cs = slice(h * HD, (h + 1) * HD)
            oh = o_ref[:, cs]
            z = z_ref[:, cs]
            d = d_all[:, cs]
            sg = _sigmoid(z)
            rs = lax.rsqrt(jnp.mean(oh * oh, axis=-1, keepdims=True) + RMS_EPS)
            n = oh * rs
            dz_ref[:, cs] = (d * n * nw_ref[...] * sg * (1.0 + z * (1.0 - sg))).astype(MXU_DT)
            dn_ = d * (z * sg)
            acc = acc + _rsum(dn_ * n)
            dnn = dn_ * nw_ref[...]
            do_ref[:, cs] = rs * (dnn - n * jnp.mean(dnn * n, axis=-1, keepdims=True))
        ao[0][...] += acc

    return _mm_epi(dyb, w_pb_t, name="d_dn", epi=epi, extras=[(o, NV * HD, 0), (pz, NV * HD, 0)], wholes=[nw],
                   row_outs=[(NV * HD, F32), (NV * HD, MXU_DT)], acc_outs=[(1, HD)], tm=256)


def _d_h2_ln1_bwd(dgu, w_gu_t, dx2a, x2, r1, mix, adam, lng):
    def epi(dh, ex, wh, ro, ao):
        dxa_ref, x2_ref, r_ref, m_ref = ex
        ada_ref, g_ref = wh
        dm_ref, dx_ref = ro
        red_ref, = ao
        dx2 = dxa_ref[...] + dh * (1.0 + ada_ref[4:5, :])
        xhat, rstd = _ln_stats(r_ref[...])
        dr = _ln_bwd(dx2, xhat, rstd, g_ref[...])
        dm_ref[...] = ((1.0 + ada_ref[2:3, :]) * dr).astype(MXU_DT)
        dx_ref[...] = ALPHA * dr
        red_ref[0:1, :] += _rsum(dh * x2_ref[...])
        red_ref[1:2, :] += _rsum(dh)
        red_ref[2:3, :] += _rsum(dx2 * xhat)
        red_ref[3:4, :] += _rsum(dx2)
        red_ref[4:5, :] += _rsum(dr * m_ref[...])

    return _mm_epi(dgu, w_gu_t, name="d_h2", epi=epi,
                   extras=[(dx2a, D, 0), (x2, D, 0), (r1, D, 0), (mix, D, 0)], wholes=[adam, lng],
                   row_outs=[(D, MXU_DT), (D, F32)], acc_outs=[(8, D)], tm=256)


def _adamw(parts, w, m, v, name):
    r, c = w.shape
    n_parts = len(parts)
    tm = _row_tile(r, c * 4 * (n_parts + 7))
    c1 = 1.0 - ADAM_B1 ** ADAM_STEP
    c2 = 1.0 - ADAM_B2 ** ADAM_STEP

    def body(*refs):
        g = refs[0][...]
        for p_ref in refs[1:n_parts]:
            g = g + p_ref[...]
        w_ref, m_ref, v_ref, g_out, d_out, m_out, v_out = refs[n_parts:]
        mn = ADAM_B1 * m_ref[...] + (1.0 - ADAM_B1) * g
        vn = ADAM_B2 * v_ref[...] + (1.0 - ADAM_B2) * (g * g)
        g_out[...] = g
        m_out[...] = mn
        v_out[...] = vn
        d_out[...] = -ADAM_LR * ((mn / c1) / (jnp.sqrt(vn / c2) + ADAM_EPS) + ADAM_WD * w_ref[...])

    spec = pl.BlockSpec((tm, c), lambda i: (i, 0))
    return pl.pallas_call(
        body, name=name, grid=(r // tm,),
        in_specs=[spec] * (n_parts + 3), out_specs=[spec] * 4,
        out_shape=[jax.ShapeDtypeStruct((r, c), F32)] * 4, compiler_params=_params(("parallel",)),
    )(*parts, w, m, v)


def _row_tile(r, bytes_per_row):
    for t in (512, 256, 128, 64, 32, 16):
        if r % t == 0 and 2 * t * bytes_per_row <= 20 * 1024 * 1024:
            return t
    return _pick(r, (16, 8))


def _sum_partials(own, recv, name):
    r, c = own.shape
    tm = _row_tile(r, c * (4 + 4 + 3 * 2))

    def body(o_ref, r_ref, out_ref):
        out_ref[...] = ((o_ref[...] + r_ref[0].astype(F32)) + r_ref[1].astype(F32)) + r_ref[2].astype(F32)

    return pl.pallas_call(
        body, name=name, grid=(r // tm,),
        in_specs=[pl.BlockSpec((tm, c), lambda i: (i, 0)), pl.BlockSpec((3, tm, c), lambda i: (0, i, 0))],
        out_specs=pl.BlockSpec((tm, c), lambda i: (i, 0)),
        out_shape=jax.ShapeDtypeStruct((r, c), F32), compiler_params=_params(("parallel",)),
    )(own, recv)


def _sum8(g, name):
    _, r, c = g.shape
    tm = _pick(r, (256, 128, 64, 32, 16, 8))

    def body(g_ref, o_ref):
        acc = g_ref[0]
        for k in range(1, 8):
            acc = acc + g_ref[k]
        o_ref[...] = acc

    return pl.pallas_call(
        body, name=name, grid=(r // tm,),
        in_specs=[pl.BlockSpec((8, tm, c), lambda i: (0, i, 0))],
        out_specs=pl.BlockSpec((tm, c), lambda i: (i, 0)),
        out_shape=jax.ShapeDtypeStruct((r, c), F32), compiler_params=_params(("parallel",)),
    )(g)


def _silu_rows(x):
    def body(x_ref, o_ref):
        xx = x_ref[...]
        o_ref[...] = xx * _sigmoid(xx)

    return pl.pallas_call(body, name="silu_rows", out_shape=jax.ShapeDtypeStruct(x.shape, F32))(x)


def _coords():
    return lax.axis_index("x"), lax.axis_index("y"), lax.axis_index("c")


def _allgather8(v, name):
    r, n = v.shape

    def body(v_ref, out_ref, send_sems, recv_sems):
        x, y, c = _coords()
        me = 4 * x + 2 * y + c
        out_ref[me] = v_ref[...]
        peers = []
        for k in range(1, 8):
            px = 1 - x if k & 4 else x
            py = 1 - y if k & 2 else y
            pc = 1 - c if k & 1 else c
            peers.append((px, py, pc))

        def copy(k, slot, to):
            return pltpu.make_async_remote_copy(
                src_ref=v_ref, dst_ref=out_ref.at[slot], send_sem=send_sems.at[k], recv_sem=recv_sems.at[k],
                device_id=to, device_id_type=MESH)

        sends = [copy(k, me, p) for k, p in enumerate(peers)]
        for cp in sends:
            cp.start()
        for k, (px, py, pc) in enumerate(peers):
            copy(k, 4 * px + 2 * py + pc, (px, py, pc)).wait_recv()
        for cp in sends:
            cp.wait_send()

    return pl.pallas_call(
        body, name=name, out_shape=jax.ShapeDtypeStruct((8, r, n), v.dtype),
        in_specs=[pl.BlockSpec(memory_space=pltpu.VMEM)], out_specs=pl.BlockSpec(memory_space=pltpu.VMEM),
        scratch_shapes=[pltpu.SemaphoreType.DMA((7,)), pltpu.SemaphoreType.DMA((7,))],
        compiler_params=pltpu.CompilerParams(vmem_limit_bytes=VMEM_LIMIT),
    )(v)


def _other_chips(x, y):
    return [(1 - x, y), (x, 1 - y), (1 - x, 1 - y)]


_HBM = pl.BlockSpec(memory_space=pl.ANY)


class _Exchange:
    def __init__(self, ins, out_shapes, sems, start, finish):
        self.ins, self.out_shapes, self.sems, self.start, self.finish = ins, out_shapes, sems, start, finish


def _run_exchange(ex, name):
    n_in, n_out = len(ex.ins), len(ex.out_shapes)

    def body(*refs):
        ins, outs, sems = refs[:n_in], refs[n_in:n_in + n_out], refs[n_in + n_out:]
        ex.start(ins, outs, sems)
        ex.finish(ins, outs, sems)

    return pl.pallas_call(body, name=name, out_shape=list(ex.out_shapes), in_specs=[_HBM] * n_in,
                          out_specs=[_HBM] * n_out, scratch_shapes=list(ex.sems))(*ex.ins)


def _gather_exchange(shards):
    n = len(shards)

    def plan(ins, outs, sems):
        send_sems, recv_sems, local_sems = sems
        x, y, c = _coords()
        s_me = 2 * x + y
        chips = _other_chips(x, y)

        def half(i, slot, hc):
            rh = shards[i].shape[0] // 2
            return outs[i].at[slot, pl.ds(pl.multiple_of(hc * rh, 16), rh), :]

        def copy(i, k, src, dst, to):
            return pltpu.make_async_remote_copy(src_ref=src, dst_ref=dst, send_sem=send_sems.at[6 * i + k],
                                                recv_sem=recv_sems.at[6 * i + k], device_id=to, device_id_type=MESH)

        local = [pltpu.make_async_copy(ins[i], outs[i].at[s_me], local_sems.at[i]) for i in range(n)]
        first = []
        for i in range(n):
            rh = shards[i].shape[0] // 2
            my_half = ins[i].at[pl.ds(pl.multiple_of(c * rh, 16), rh), :]
            first += [copy(i, j, my_half, half(i, s_me, c), (px, py, c)) for j, (px, py) in enumerate(chips)]
        return (x, y, c), chips, half, copy, local, first

    def start(ins, outs, sems):
        _, _, _, _, local, first = plan(ins, outs, sems)
        for cp in local + first:
            cp.start()

    def finish(ins, outs, sems):
        (x, y, c), chips, half, copy, local, first = plan(ins, outs, sems)
        sibling = (x, y, 1 - c)
        passed = []
        for i in range(n):
            for j, (px, py) in enumerate(chips):
                land = half(i, 2 * px + py, c)
                copy(i, j, land, land, (px, py, c)).wait_recv()
                fw = copy(i, 3 + j, land, land, sibling)
                fw.start()
                passed.append(fw)
        for i in range(n):
            for j, (px, py) in enumerate(chips):
                land = half(i, 2 * px + py, 1 - c)
                copy(i, 3 + j, land, land, sibling).wait_recv()
        for cp in first + passed:
            cp.wait_send()
        for cp in local:
            cp.wait()

    return _Exchange(list(shards), [jax.ShapeDtypeStruct((4,) + v.shape, v.dtype) for v in shards],
                     [pltpu.SemaphoreType.DMA((6 * n,)), pltpu.SemaphoreType.DMA((6 * n,)),
                      pltpu.SemaphoreType.DMA((n,))], start, finish)


def _scatter_exchange(gs):
    n = len(gs)

    def copies(ins, outs, sems):
        send_sems, recv_sems = sems
        x, y, c = _coords()
        return [pltpu.make_async_remote_copy(
                    src_ref=ins[i].at[2 * px + py], dst_ref=outs[i].at[j], send_sem=send_sems.at[3 * i + j],
                    recv_sem=recv_sems.at[3 * i + j], device_id=(px, py, c), device_id_type=MESH)
                for i in range(n) for j, (px, py) in enumerate(_other_chips(x, y))]

    def start(ins, outs, sems):
        for cp in copies(ins, outs, sems):
            cp.start()

    def finish(ins, outs, sems):
        cps = copies(ins, outs, sems)
        for cp in cps:
            cp.wait_recv()
        for cp in cps:
            cp.wait_send()

    return _Exchange(list(gs), [jax.ShapeDtypeStruct((3,) + g.shape[1:], g.dtype) for g in gs],
                     [pltpu.SemaphoreType.DMA((3 * n,)), pltpu.SemaphoreType.DMA((3 * n,))], start, finish)


def _allgather_exchange(v):
    def copies(ins, outs, sems):
        send_sems, recv_sems, _ = sems
        x, y, c = _coords()
        me = 4 * x + 2 * y + c
        cps = []
        for k in range(1, 8):
            peer = (1 - x if k & 4 else x, 1 - y if k & 2 else y, 1 - c if k & 1 else c)
            cps.append(pltpu.make_async_remote_copy(
                src_ref=ins[0], dst_ref=outs[0].at[me], send_sem=send_sems.at[k - 1], recv_sem=recv_sems.at[k - 1],
                device_id=peer, device_id_type=MESH))
        return me, cps

    def start(ins, outs, sems):
        me, cps = copies(ins, outs, sems)
        pltpu.make_async_copy(ins[0], outs[0].at[me], sems[2]).start()
        for cp in cps:
            cp.start()

    def finish(ins, outs, sems):
        me, cps = copies(ins, outs, sems)
        for cp in cps:
            cp.wait_recv()
        for cp in cps:
            cp.wait_send()
        pltpu.make_async_copy(ins[0], outs[0].at[me], sems[2]).wait()

    return _Exchange([v], [jax.ShapeDtypeStruct((8,) + v.shape, v.dtype)],
                     [pltpu.SemaphoreType.DMA((7,)), pltpu.SemaphoreType.DMA((7,)), pltpu.SemaphoreType.DMA],
                     start, finish)


def _swap_sibling(vs):
    n = len(vs)

    def body(*refs):
        ins, outs = refs[:n], refs[n:2 * n]
        send_sems, recv_sems = refs[2 * n:]
        x, y, c = _coords()
        cps = [pltpu.make_async_remote_copy(src_ref=ins[i], dst_ref=outs[i], send_sem=send_sems.at[i],
                                            recv_sem=recv_sems.at[i], device_id=(x, y, 1 - c), device_id_type=MESH)
               for i in range(n)]
        for cp in cps:
            cp.start()
        for cp in cps:
            cp.wait()

    return pl.pallas_call(
        body, name="swap_sibling", out_shape=[jax.ShapeDtypeStruct(v.shape, v.dtype) for v in vs],
        in_specs=[_HBM] * n, out_specs=[_HBM] * n,
        scratch_shapes=[pltpu.SemaphoreType.DMA((n,)), pltpu.SemaphoreType.DMA((n,))],
    )(*vs)


def _pad_rows(a, rows):
    return jnp.pad(a, ((0, rows - a.shape[0]), (0, 0)))


def _block_diag(w):
    eye = jnp.eye(RG_BLOCKS, dtype=w.dtype)
    return (eye[:, None, :, None] * w[:, :, None, :]).reshape(D_RNN, D_RNN)


def _diag_blocks(g):
    g4 = g.reshape(RG_BLOCKS, RG_BW, RG_BLOCKS, RG_BW)
    idx = jnp.arange(RG_BLOCKS)
    return g4[idx, :, idx, :]


def _prepare_rest(p):
    w = {}
    for k_, n_ in (("pa", "w_proj_a"), ("pb", "w_proj_b"), ("out", "w_out"), ("down", "ffn_w_down")):
        w[k_] = p[n_].astype(MXU_DT)
        w[k_ + "_t"] = w[k_].T
    w["gu"] = jnp.concatenate([p["ffn_w_gate"], p["ffn_w_up"]], axis=1).astype(MXU_DT)
    w["gu_t"] = w["gu"].T
    return w


def _prepare_first(p):
    w = {}
    wi = p["w_in"].astype(MXU_DT)
    cat = jnp.concatenate([wi[:, 2560:6656], wi[:, 6656:8704], wi[:, 8736:10784], wi[:, 0:2560],
                           wi[:, 8704:8736], jnp.zeros((D, 96), MXU_DT)], axis=1)
    w["in_cat"], w["in_cat_t"] = cat, cat.T
    w["rg_cw"] = _pad_rows(p["rg_conv_w"], 8)
    w["dn_cw"] = _pad_rows(p["dn_conv_w"], 8)
    w["ffn_cw"] = _pad_rows(p["ffn_conv_w"], 8)
    w["rg_vec"] = _pad_rows(jnp.stack([p["rg_conv_b"], p["rg_b_a"], p["rg_b_x"], p["rg_lambda"]]), 8)
    w["wa"] = _block_diag(p["rg_w_a"]).astype(MXU_DT)
    w["wx"] = _block_diag(p["rg_w_x"]).astype(MXU_DT)
    w["wa_t"], w["wx_t"] = w["wa"].T, w["wx"].T
    w["arow"] = jnp.pad(p["dn_a_log"], (0, 128 - NV))[None, :]
    w["drow"] = jnp.pad(p["dn_dt_bias"], (0, 128 - NV))[None, :]
    w["acol"] = p["dn_a_log"][:, None]
    w["dcol"] = p["dn_dt_bias"][:, None]
    w["nw"] = p["dn_norm_w"][None, :]
    w["ffn_cb"] = p["ffn_conv_b"][None, :]
    for n_ in ("ln1_g", "ln1_b", "ln2_g", "ln2_b"):
        w[n_] = p[n_][None, :]
    return w


def _mm_sided(a, b, side, **kw):
    if side is None:
        return _mm(a, b, **kw), []
    res = _mm(a, b, side=side, **kw)
    return res[0], res[1:]


_SMALL = ("rg_conv_w", "dn_conv_w", "ffn_conv_w", "rg_conv_b", "rg_w_a", "rg_b_a", "rg_w_x", "rg_b_x",
          "rg_lambda", "dn_a_log", "dn_dt_bias", "dn_norm_w", "ln1_g", "ln1_b", "ffn_conv_b", "ln2_g", "ln2_b")
_REST_A = ("w_proj_a", "w_proj_b", "w_out")
_REST_B = ("ffn_w_gate", "ffn_w_up", "ffn_w_down")


def _local_step(x, tgt, adam, p, shards=None):
    w = _prepare_first(p)
    side_a = side_b = None
    if shards is not None:
        side_a = _gather_exchange([shards[n] for n in _REST_A])
        side_b = _gather_exchange([shards[n] for n in _REST_B])
    h1, h1t = _modulate(x, adam)
    cat = w["in_cat"]
    pqkv, got_a = _mm_sided(h1, cat[:, 0:4096], side_a, name="proj_qkv")
    px, got_b = _mm_sided(h1, cat[:, 8192:10752], side_b, name="proj_x")
    if shards is not None:
        p = dict(p, **{n: _unstack_shards(n, a_) for n, a_ in zip(_REST_A + _REST_B, got_a + got_b)})
    w.update(_prepare_rest(p))
    pz = _mm(h1, cat[:, 4096:6144], name="proj_z")
    pg = _mm(h1, cat[:, 6144:8192], name="proj_g")
    pab = _mm(h1, cat[:, 10752:10880], name="proj_ab")
    hrec, xcs, rec, rec_t = _rg_fwd(px, w["rg_cw"], w["rg_vec"], w["wa"], w["wx"])
    q, k, v, rt, cpre = _dn_prep(pqkv, pab, w["dn_cw"], w["arow"], w["drow"])
    u, ww, qd, kd, pm, tinv, gam = _dn_intra(q, k, v, rt)
    o, vn, ssave, dn, dn_t = _dn_seq(u, ww, qd, kd, pm, gam, pz, w["nw"])
    ya = _mm(rec, w["pa"], name="proj_a")
    yb = _mm(dn, w["pb"], name="proj_b")
    mix, merged_t = _merge_proj_out(pg, ya, yb, w["out"])
    gu, r1, x2, h2_t = _ln1_ffn_gu(x, mix, adam, w["ln1_g"], w["ln1_b"], w["gu"])
    act, act_t, gcf = _ffn_act(gu, w["ffn_cw"], w["ffn_cb"])
    dff, dx2a, red2 = _ffn_down_ln2_loss(act, w["down"], x2, tgt, adam, w["ln2_g"], w["ln2_b"])
    g = {}
    dact = _mm(dff, w["down_t"], name="d_act")
    g["ffn_w_down"] = _mm(act_t, dff, name="g_down")
    dgu, gcw_f, gcb_f = _ffn_bwd(dact, gu, gcf, w["ffn_cw"])
    ggu = _mm(h2_t, dgu, name="g_gu")
    g["ffn_w_gate"], g["ffn_w_up"] = ggu[:, :D_FF], ggu[:, D_FF:]
    g["ffn_conv_w"], g["ffn_conv_b"] = gcw_f[0:3], gcb_f[0]
    dmix, dxa, red1 = _d_h2_ln1_bwd(dgu, w["gu_t"], dx2a, x2, r1, mix, adam, w["ln1_g"])
    g["w_out"] = _mm(merged_t, dmix, name="g_out")
    dya, dyb, dpg = _d_merged_bwd(dmix, w["out_t"], pg, ya, yb)
    drec = _mm(dya, w["pa_t"], name="d_rec")
    g["w_proj_a"] = _mm(rec_t, dya, name="g_pa")
    g["w_proj_b"] = _mm(dn_t, dyb, name="g_pb")
    do, dpz, gnw = _d_dn_post_bwd(dyb, w["pb_t"], o, pz, w["nw"])
    dvn, dkd, dgam = _dn_seq_bwd(do, qd, kd, pm, ww, vn, ssave, gam)
    dq, dk, dv, drt = _dn_intra_bwd(q, k, v, rt, do, dvn, dkd, gam, dgam, ssave, tinv, u, ww, vn)
    dpqkv, dpab, gcw_d, gsc = _dn_prep_bwd(dq, dk, dv, drt, pqkv, cpre, pab, w["dn_cw"], w["acol"], w["dcol"])
    side_r = None
    if shards is not None:
        side_r = _scatter_exchange([_stack_shards(n, g[n]).astype(MXU_DT) for n in _REST_A + _REST_B])
    dpx, gwa, gwx, gcw_r, gvec, *got_r = _rg_bwd(drec, px, xcs, hrec, w["rg_cw"], w["rg_vec"], w["wa"], w["wx"],
                                                 w["wa_t"], w["wx_t"], side_r)
    dproj = jnp.concatenate([dpqkv, dpz, dpg, dpx, dpab], axis=1)
    reorder = lambda gc: jnp.concatenate([gc[:, 8192:10752], gc[:, 0:4096], gc[:, 4096:6144], gc[:, 10752:10784],
                                          gc[:, 6144:8192]], axis=1)
    wire = lambda gh: _scatter_exchange([_stack_shards("w_in", gh).astype(MXU_DT)]) if shards is not None else None
    g["rg_conv_w"], g["rg_conv_b"] = gcw_r[0:4], gvec[0]
    g["rg_w_a"], g["rg_w_x"] = _diag_blocks(gwa), _diag_blocks(gwx)
    g["rg_b_a"], g["rg_b_x"], g["rg_lambda"] = gvec[1], gvec[2], gvec[3]
    g["dn_conv_w"] = gcw_d[0:4]
    g["dn_a_log"], g["dn_dt_bias"], g["dn_norm_w"] = gsc[:, 0], gsc[:, 1], gnw[0]
    g["ln1_g"], g["ln1_b"] = red1[2], red1[3]
    g["ln2_g"], g["ln2_b"] = red2[0], red2[1]
    side_s = None
    if shards is not None:
        side_s = _allgather_exchange(_pack([jnp.full((1,), red2[3, 0], F32)] + [g[n] for n in _SMALL], 128, 64))
    g_top, got_s = _mm_sided(h1t[:D // 2], dproj, side_s, name="g_in_top")
    g_top = reorder(g_top)
    g_bot, got_top = _mm_sided(h1t[D // 2:], dproj, wire(g_top), name="g_in_bot")
    g_bot = reorder(g_bot)
    gx, red0, *got_bot = _d_h1_modulate_bwd(dproj, w["in_cat_t"], dxa, x, adam, wire(g_bot))
    g["w_in"] = jnp.concatenate([g_top, g_bot], axis=0)
    got = dict(zip(_REST_A + _REST_B, got_r))
    if shards is not None:
        got["w_in"] = jnp.concatenate([got_top[0], got_bot[0]], axis=1)
        got["small"] = got_s[0]
    d_ada = jnp.concatenate([red0[1], red0[0], red1[4], red1[1], red1[0], red2[2]])
    return red2[3, 0], gx, g, d_ada, got


_BIG = ("w_in", "w_proj_a", "w_proj_b", "w_out", "ffn_w_gate", "ffn_w_up", "ffn_w_down")
_COL_SHARDED = ("w_in", "ffn_w_gate", "ffn_w_up")
_CONV = ("rg_conv_w", "dn_conv_w", "ffn_conv_w")
_REPL = ("b_ada", "rg_conv_b", "rg_w_a", "rg_b_a", "rg_w_x", "rg_b_x", "rg_lambda", "dn_a_log",
         "dn_dt_bias", "dn_norm_w", "ln1_g", "ln1_b", "ffn_conv_b", "ln2_g", "ln2_b")
_NAMES = ("w_ada", "b_ada", "w_in", "rg_conv_w", "rg_conv_b", "rg_w_a", "rg_b_a", "rg_w_x", "rg_b_x",
          "rg_lambda", "dn_conv_w", "dn_a_log", "dn_dt_bias", "dn_norm_w", "w_proj_a", "w_proj_b", "w_out",
          "ln1_g", "ln1_b", "ffn_w_gate", "ffn_w_up", "ffn_conv_w", "ffn_conv_b", "ffn_w_down", "ln2_g", "ln2_b")


def _pack(arrs, width, row_mult):
    pieces = []
    for a in arrs:
        f = a.reshape(-1)
        pieces.append(jnp.pad(f, (0, (-f.shape[0]) % (8 * width))).reshape(-1, width))
    rows = sum(p_.shape[0] for p_ in pieces)
    if rows % row_mult:
        pieces.append(jnp.zeros((row_mult - rows % row_mult, width), pieces[0].dtype))
    return jnp.concatenate(pieces, axis=0)


def _unpack(flat, shapes, width):
    out, row = [], 0
    for shp in shapes:
        n = 1
        for d_ in shp:
            n *= d_
        rows = -(-n // (8 * width)) * 8
        out.append(flat[row:row + rows].reshape(-1)[:n].reshape(shp))
        row += rows
    return out


def _stack_shards(name, full):
    if name in _COL_SHARDED or name in _CONV:
        r, ncol = full.shape
        return full.reshape(r, 4, ncol // 4).transpose(1, 0, 2)
    return full.reshape((4, full.shape[0] // 4) + full.shape[1:])


def _unstack_shards(name, st):
    if name in _COL_SHARDED or name in _CONV:
        return st.transpose(1, 0, 2).reshape(st.shape[1], 4 * st.shape[2])
    return st.reshape((4 * st.shape[1],) + st.shape[2:])


def kernel(x, c, w_ada, b_ada, w_in, rg_conv_w, rg_conv_b, rg_w_a, rg_b_a, rg_w_x, rg_b_x, rg_lambda, dn_conv_w, dn_a_log, dn_dt_bias, dn_norm_w, w_proj_a, w_proj_b, w_out, ln1_g, ln1_b, ffn_w_gate, ffn_w_up, ffn_conv_w, ffn_conv_b, ffn_w_down, ln2_g, ln2_b, loss_target, m_w_ada, m_b_ada, m_w_in, m_rg_conv_w, m_rg_conv_b, m_rg_w_a, m_rg_b_a, m_rg_w_x, m_rg_b_x, m_rg_lambda, m_dn_conv_w, m_dn_a_log, m_dn_dt_bias, m_dn_norm_w, m_w_proj_a, m_w_proj_b, m_w_out, m_ln1_g, m_ln1_b, m_ffn_w_gate, m_ffn_w_up, m_ffn_conv_w, m_ffn_conv_b, m_ffn_w_down, m_ln2_g, m_ln2_b, v_w_ada, v_b_ada, v_w_in, v_rg_conv_w, v_rg_conv_b, v_rg_w_a, v_rg_b_a, v_rg_w_x, v_rg_b_x, v_rg_lambda, v_dn_conv_w, v_dn_a_log, v_dn_dt_bias, v_dn_norm_w, v_w_proj_a, v_w_proj_b, v_w_out, v_ln1_g, v_ln1_b, v_ffn_w_gate, v_ffn_w_up, v_ffn_conv_w, v_ffn_conv_b, v_ffn_w_down, v_ln2_g, v_ln2_b):
    args = locals()
    wts = {n: args[n][0] for n in _NAMES}
    mom = {n: args["m_" + n][0] for n in _NAMES}
    var = {n: args["v_" + n][0] for n in _NAMES}
    xs, tgt = x[0], loss_target[0]
    ix, iy, ic = _coords()
    shard = 2 * ix + iy
    batch = 4 * ix + 2 * iy + ic

    c_all = _allgather8(_pad_rows(c, 8), "gather_c")[:, 0, :]
    sc16 = _pad_rows(_silu_rows(c_all), 16)
    ada_cols = _mm(sc16, w_ada[0], name="ada")[:8]
    ada_g = _allgather8(ada_cols, "gather_ada")
    ada_all = jnp.concatenate([ada_g[0], ada_g[2], ada_g[4], ada_g[6]], axis=1) + b_ada
    adam = _pad_rows(lax.dynamic_index_in_dim(ada_all, batch, 0, keepdims=False).reshape(6, D), 8)

    (w_in_all,) = _run_exchange(_gather_exchange([wts["w_in"].astype(MXU_DT)]), "gather_w_in")
    full = {"w_in": _unstack_shards("w_in", w_in_all)}
    conv_shard = _pack([wts[n] for n in _CONV], 128, 8)
    conv_all = _allgather8(conv_shard, "gather_conv")
    shapes_conv = [wts[n].shape for n in _CONV]
    per_shard = [_unpack(conv_all[2 * s], shapes_conv, 128) for s in range(4)]
    for i, n in enumerate(_CONV):
        full[n] = _unstack_shards(n, jnp.stack([per_shard[s][i] for s in range(4)]))
    for n in _REPL:
        full[n] = wts[n]

    shards = {n: wts[n].astype(MXU_DT) for n in _REST_A + _REST_B}
    loss_b, gx, g, d_ada, recv = _local_step(xs, tgt, adam, full, shards)

    parts = []
    for n in _BIG:
        r_ = recv[n]
        axis = 1 if n in _COL_SHARDED else 0
        width = wts[n].shape[axis]
        own = lax.dynamic_slice_in_dim(g[n], shard * width, width, axis=axis)
        parts.append(_sum_partials(own, r_, "sum_" + n))
    parts_sib = _swap_sibling(parts)
    out = {n: [] for n in _NAMES}
    for n, p_, q_ in zip(_BIG, parts, parts_sib):
        out[n] = list(_adamw([p_, q_], wts[n], mom[n], var[n], "adamw_" + n))

    tot = _unpack(_sum8(recv["small"], "sum_small"), [(1,)] + [full[n].shape for n in _SMALL], 128)
    gsum = dict(zip(_SMALL, tot[1:]))
    loss = tot[0][0]
    for n in _CONV:
        gsum[n] = lax.dynamic_index_in_dim(_stack_shards(n, gsum[n]), shard, 0, keepdims=False)
    d_ada_g = _allgather8(d_ada.reshape(6 * D // 128, 128), "gather_d_ada")
    gsum["b_ada"] = _sum8(d_ada_g, "sum_d_ada").reshape(6 * D)
    d_ada_all = d_ada_g.reshape(8, 6 * D)
    cols = lax.dynamic_slice_in_dim(d_ada_all, shard * (6 * D // 4), 6 * D // 4, axis=1)
    g_wada = _mm(sc16, _pad_rows(cols, 16), name="g_ada", trans_a=True)
    res = _adamw([g_wada], wts["w_ada"], mom["w_ada"], var["w_ada"], "adamw_ada")
    out["w_ada"] = list(res)
    names_s = _CONV + _REPL
    shapes_s = [wts[n].shape for n in names_s]
    pk = lambda d_: _pack([d_[n] for n in names_s], 128, 64)
    res_s = _adamw([pk(gsum)], pk(wts), pk(mom), pk(var), "adamw_small")
    for r_ in res_s:
        for n, a in zip(names_s, _unpack(r_, shapes_s, 128)):
            out[n].append(a)

    outs = [loss, gx[None]]
    for i in range(4):
        outs += [out[n][i][None] for n in _NAMES]
    return tuple(outs)
```

```python
import functools

import jax
import jax.numpy as jnp
from jax import lax
from jax.experimental import pallas as pl
from jax.experimental.pallas import tpu as pltpu

F32 = jnp.float32
BF16 = jnp.bfloat16
MXU_DT = BF16

D = 1024
D_RNN = 1280
RG_BLOCKS = 16
RG_BW = 80
RG_C = 8.0
NQ = 8
NV = 16
HD = 128
CH = 64
D_FF = 2816
LN_EPS = 1e-5
RMS_EPS = 1e-6
L2_EPS = 1e-6
ALPHA = 2.0 ** 0.25
Q_SCALE = HD ** -0.5
N_CAT = 10880
VMEM_LIMIT = 56 * 1024 * 1024
MM_VMEM_BUDGET = 36 * 1024 * 1024
SEQ_GROUP = 16
MESH = pl.DeviceIdType.MESH

ADAM_LR, ADAM_B1, ADAM_B2, ADAM_EPS, ADAM_WD, ADAM_STEP = 1e-3, 0.9, 0.999, 1e-8, 0.01, 10


def _sigmoid(x):
    return 0.5 * jnp.tanh(0.5 * x) + 0.5


def _softplus(x):
    return jnp.maximum(x, 0.0) + jnp.log1p(jnp.exp(-jnp.abs(x)))


_GC = 0.7978845608028654


def _gelu(x):
    return 0.5 * x * (1.0 + jnp.tanh(_GC * (x + 0.044715 * x * x * x)))


def _gelu_and_grad(x):
    t = jnp.tanh(_GC * (x + 0.044715 * x * x * x))
    g = 0.5 * x * (1.0 + t)
    dg = 0.5 * (1.0 + t) + 0.5 * x * (1.0 - t * t) * _GC * (1.0 + 3 * 0.044715 * x * x)
    return g, dg


def _neg_expm1(y):
    series = -y * (1.0 + 0.5 * y * (1.0 + y * (1.0 / 3.0)))
    return jnp.where(y > -0.01, series, 1.0 - jnp.exp(y))


def _dot(a, b):
    return jnp.dot(a.astype(MXU_DT), b.astype(MXU_DT), preferred_element_type=F32)


def _dot_nt(a, b):
    return lax.dot_general(a.astype(MXU_DT), b.astype(MXU_DT), (((1,), (1,)), ((), ())),
                           preferred_element_type=F32)


def _dot_tn(a, b):
    return lax.dot_general(a.astype(MXU_DT), b.astype(MXU_DT), (((0,), (0,)), ((), ())),
                           preferred_element_type=F32)


def _split(a):
    hi = a.astype(BF16)
    return hi, (a - hi.astype(F32)).astype(BF16)


def _dot3(a, b, dims=(((1,), (0,)), ((), ()))):
    ah, al = _split(a)
    bh, bl = _split(b)
    d = lambda p, q: lax.dot_general(p, q, dims, preferred_element_type=F32)
    return d(ah, bh) + (d(al, bh) + d(ah, bl))


def _dot3_tn(a, b):
    return _dot3(a, b, (((0,), (0,)), ((), ())))


def _iota(shape, dim):
    return lax.broadcasted_iota(jnp.int32, shape, dim)


def _shift_down(x, before, j):
    if j == 0:
        return x
    xr = pltpu.roll(x, j, 0)
    br = pltpu.roll(before, j, 0)
    top = jnp.where(_iota(br.shape, 0) < j, br, xr[:8])
    return jnp.concatenate([top, xr[8:]], axis=0)


def _shift_up(x, after, j):
    if j == 0:
        return x
    t = x.shape[0]
    xr = pltpu.roll(x, t - j, 0)
    ar = pltpu.roll(after, 8 - j, 0)
    bot = jnp.where(_iota(ar.shape, 0) >= 8 - j, ar, xr[t - 8:])
    return jnp.concatenate([xr[:t - 8], bot], axis=0)


def _taps(x, before, k):
    return [_shift_down(x, before, k - 1 - i) for i in range(k)]


def _conv_taps(taps, w_ref):
    y = w_ref[0:1, :] * taps[0]
    for i in range(1, len(taps)):
        y = y + w_ref[i:i + 1, :] * taps[i]
    return y


def _conv_causal(x, before, w_ref, k):
    return _conv_taps(_taps(x, before, k), w_ref)


def _conv_causal_bwd(dy, after, w_ref, k, x=None, gw_ref=None):
    dx = None
    for i in range(k):
        sh = _shift_up(dy, after, k - 1 - i)
        term = w_ref[i:i + 1, :] * sh
        dx = term if dx is None else dx + term
        if x is not None:
            gw_ref[i:i + 1, :] += _rsum(x * sh)
    return dx


def _scan_fwd(a, u):
    t = a.shape[0]
    rows = _iota(a.shape, 0)
    d = 1
    while d < t:
        m = rows >= d
        u = u + jnp.where(m, a * pltpu.roll(u, d, 0), 0.0)
        a = jnp.where(m, a * pltpu.roll(a, d, 0), a)
        d *= 2
    return a, u


def _scan_rev(a, u):
    t = a.shape[0]
    rows = _iota(a.shape, 0)
    d = 1
    while d < t:
        m = rows < t - d
        u = u + jnp.where(m, a * pltpu.roll(u, t - d, 0), 0.0)
        a = jnp.where(m, a * pltpu.roll(a, t - d, 0), a)
        d *= 2
    return a, u


def _chunk_cumsum(g, axis, rev=False):
    n = g.shape[axis]
    pos = _iota(g.shape, axis) & (CH - 1)
    d = 1
    while d < CH:
        if rev:
            g = g + jnp.where(pos < CH - d, pltpu.roll(g, n - d, axis), 0.0)
        else:
            g = g + jnp.where(pos >= d, pltpu.roll(g, d, axis), 0.0)
        d *= 2
    return g


def _ln_stats(r):
    mu = jnp.mean(r, axis=-1, keepdims=True)
    xc = r - mu
    var = jnp.mean(xc * xc, axis=-1, keepdims=True)
    rstd = lax.rsqrt(var + LN_EPS)
    return xc * rstd, rstd


def _ln_bwd(dy, xhat, rstd, g):
    dxh = dy * g
    return rstd * (dxh - jnp.mean(dxh, axis=-1, keepdims=True)
                   - xhat * jnp.mean(dxh * xhat, axis=-1, keepdims=True))


def _rsum(x):
    return jnp.sum(x, axis=0, keepdims=True)


def _params(sem):
    return pltpu.CompilerParams(dimension_semantics=sem, vmem_limit_bytes=VMEM_LIMIT)


def _pick(n, cands):
    for c in cands:
        if n % c == 0:
            return c
    return n


def _rows(tm, w, col=0, nt=None):
    if nt is None:
        return pl.BlockSpec((tm, w), lambda i: (i, col))
    return pl.BlockSpec((tm, w), lambda i: (nt - 1 - i, col))


def _before(tm, w, col=0, nt=None):
    r = tm // 8
    if nt is None:
        return pl.BlockSpec((8, w), lambda i: (jnp.maximum(i * r - 1, 0), col))
    return pl.BlockSpec((8, w), lambda i: (jnp.maximum((nt - 1 - i) * r - 1, 0), col))


def _cols(tm, w):
    return pl.BlockSpec((w, tm), lambda i: (0, i))


def _whole(shape):
    return pl.BlockSpec(shape, lambda *_: (0,) * len(shape))


def _mm_plan(a, b, out_dtype):
    m, kk = a.shape
    _, n = b.shape
    tm = _pick(m, (512, 256, 128))
    tk = kk if kk <= 5632 else _pick(kk, (2176, 2048, 1024))
    nk = kk // tk

    def vmem_bytes(tn):
        blocks = tm * tk * a.dtype.itemsize + tk * tn * b.dtype.itemsize + tm * tn * jnp.dtype(out_dtype).itemsize
        return 2 * blocks + (tm * tn * 4 if nk > 1 else 0)

    cands = [t for t in (1408, 1280, 1024, 640, 512, 256, 128) if n % t == 0] or [n]
    tn = next((t for t in cands if vmem_bytes(t) <= MM_VMEM_BUDGET), cands[-1])
    return tm, tn, tk, nk


def _mm_side(a, b, name, out_dtype, side):
    m, _ = a.shape
    _, n = b.shape
    tm, tn, tk, nk = _mm_plan(a, b, out_dtype)
    ni, nj = m // tm, n // tn
    n_in, n_out = len(side.ins), len(side.out_shapes)

    def body(*refs):
        a_ref, b_ref = refs[0], refs[1]
        s_in = refs[2:2 + n_in]
        o_ref = refs[2 + n_in]
        s_out = refs[3 + n_in:3 + n_in + n_out]
        acc = refs[3 + n_in + n_out]
        sems = refs[4 + n_in + n_out:]
        i, j, k = pl.program_id(0), pl.program_id(1), pl.program_id(2)

        @pl.when((i == 0) & (j == 0) & (k == 0))
        def _():
            side.start(s_in, s_out, sems)

        @pl.when(k == 0)
        def _():
            acc[...] = jnp.zeros_like(acc)
        acc[...] += _dot(a_ref[...], b_ref[...])

        @pl.when(k == nk - 1)
        def _():
            o_ref[...] = acc[...].astype(out_dtype)

        @pl.when((i == ni - 1) & (j == nj - 1) & (k == nk - 1))
        def _():
            side.finish(s_in, s_out, sems)

    return pl.pallas_call(
        body, name=name, grid=(ni, nj, nk),
        in_specs=[pl.BlockSpec((tm, tk), lambda i, j, k: (i, k)),
                  pl.BlockSpec((tk, tn), lambda i, j, k: (k, j))] + [_HBM] * n_in,
        out_specs=[pl.BlockSpec((tm, tn), lambda i, j, k: (i, j))] + [_HBM] * n_out,
        out_shape=[jax.ShapeDtypeStruct((m, n), out_dtype)] + list(side.out_shapes),
        scratch_shapes=[pltpu.VMEM((tm, tn), F32)] + list(side.sems),
        compiler_params=_params(("arbitrary", "arbitrary", "arbitrary")),
    )(a, b, *side.ins)


def _mm(a, b, *, name, trans_a=False, out_dtype=F32, side=None):
    if trans_a:
        return _mm(a.T, b, name=name, out_dtype=out_dtype, side=side)
    if side is not None:
        return _mm_side(a, b, name, out_dtype, side)
    m, kk = a.shape
    _, n = b.shape
    tm, tn, tk, nk = _mm_plan(a, b, out_dtype)

    if nk == 1:
        def body(a_ref, b_ref, o_ref):
            o_ref[...] = _dot(a_ref[...], b_ref[...]).astype(out_dtype)
        scratch = []
    else:
        def body(a_ref, b_ref, o_ref, acc):
            k = pl.program_id(2)

            @pl.when(k == 0)
            def _():
                acc[...] = jnp.zeros_like(acc)
            acc[...] += _dot(a_ref[...], b_ref[...])

            @pl.when(k == nk - 1)
            def _():
                o_ref[...] = acc[...].astype(out_dtype)
        scratch = [pltpu.VMEM((tm, tn), F32)]

    return pl.pallas_call(
        body, name=name, grid=(m // tm, n // tn, nk),
        in_specs=[pl.BlockSpec((tm, tk), lambda i, j, k: (i, k)),
                  pl.BlockSpec((tk, tn), lambda i, j, k: (k, j))],
        out_specs=pl.BlockSpec((tm, tn), lambda i, j, k: (i, j)),
        out_shape=jax.ShapeDtypeStruct((m, n), out_dtype),
        scratch_shapes=scratch,
        compiler_params=_params(("parallel", "parallel", "arbitrary")),
    )(a, b)


def _modulate(x, adam):
    s = x.shape[0]
    tm = _pick(s, (512, 256, 128))

    def body(x_ref, ada_ref, o_ref, ot_ref):
        h = x_ref[...] * (1.0 + ada_ref[1:2, :]) + ada_ref[0:1, :]
        o_ref[...] = h.astype(MXU_DT)
        ot_ref[...] = h.T.astype(MXU_DT)

    return pl.pallas_call(
        body, name="modulate1", grid=(s // tm,),
        in_specs=[_rows(tm, D), _whole((8, D))], out_specs=[_rows(tm, D), _cols(tm, D)],
        out_shape=[jax.ShapeDtypeStruct((s, D), MXU_DT), jax.ShapeDtypeStruct((D, s), MXU_DT)],
        compiler_params=_params(("parallel",)),
    )(x, adam)


def _rg_gates(xc, wa_ref, wx_ref, vec_ref):
    xb = xc.astype(MXU_DT)
    r = _sigmoid(jnp.dot(xb, wa_ref[...], preferred_element_type=F32) + vec_ref[1:2, :])
    ig = _sigmoid(jnp.dot(xb, wx_ref[...], preferred_element_type=F32) + vec_ref[2:3, :])
    sp = _softplus(-vec_ref[3:4, :])
    la = -RG_C * r * sp
    a = jnp.exp(la)
    n1 = _neg_expm1(2.0 * la)
    rmult = lax.rsqrt(jnp.maximum(n1, 1e-20))
    return r, ig, a, n1 * rmult, sp, rmult


def _rg_fwd(px, cw, vec, wa, wx):
    s = px.shape[0]
    tm = _pick(s, (256, 128))
    w = D_RNN

    def body(xr_ref, gr_ref, cw_ref, vec_ref, wa_ref, wx_ref, h_ref, xc_ref, rec_ref, rect_ref, prev_x, prev_h):
        @pl.when(pl.program_id(0) == 0)
        def _():
            prev_x[...] = jnp.zeros_like(prev_x)
            prev_h[...] = jnp.zeros_like(prev_h)
        x = xr_ref[...]
        xc = _conv_causal(x, prev_x[...], cw_ref, 4) + vec_ref[0:1, :]
        prev_x[...] = x[tm - 8:, :]
        xc_ref[...] = xc
        _, ig, a, mult, _, _ = _rg_gates(xc, wa_ref, wx_ref, vec_ref)
        acum, h = _scan_fwd(a, mult * ig * xc)
        h = h + acum * prev_h[7:8, :]
        prev_h[...] = h[tm - 8:, :]
        h_ref[...] = h
        rec = h * _gelu(gr_ref[...])
        rec_ref[...] = rec.astype(MXU_DT)
        rect_ref[...] = rec.T.astype(MXU_DT)

    return pl.pallas_call(
        body, name="rg_fwd", grid=(s // tm,),
        in_specs=[_rows(tm, w, 0), _rows(tm, w, 1), _whole((8, w)), _whole((8, w)),
                  _whole((w, w)), _whole((w, w))],
        out_specs=[_rows(tm, w), _rows(tm, w), _rows(tm, w), _cols(tm, w)],
        out_shape=[jax.ShapeDtypeStruct((s, w), F32), jax.ShapeDtypeStruct((s, w), F32),
                   jax.ShapeDtypeStruct((s, w), MXU_DT), jax.ShapeDtypeStruct((w, s), MXU_DT)],
        scratch_shapes=[pltpu.VMEM((8, w), F32), pltpu.VMEM((8, w), F32)],
        compiler_params=_params(("arbitrary",)),
    )(px, px, cw, vec, wa, wx)


def _dn_scalars(ab, arow, drow):
    lane = _iota(ab.shape, 1)
    g = jnp.where(lane < NV, -jnp.exp(arow) * _softplus(ab + drow), 0.0)
    beta = _sigmoid(ab)
    return lane, g, beta


def _l2n_heads(c, out_ref, off, scale):
    for hh in range(NQ):
        x = c[:, off + hh * HD: off + (hh + 1) * HD]
        r = lax.rsqrt(jnp.sum(x * x, axis=-1, keepdims=True) + L2_EPS)
        out_ref[:, hh * HD:(hh + 1) * HD] = x * (r * scale)


def _dn_prep(pqkv, pab, cw, arow, drow):
    s = pqkv.shape[0]
    tm = 128
    wq = NQ * HD

    def body(x_ref, ab_ref, cw_ref, a_ref, d_ref, q_ref, k_ref, v_ref, rt_ref, cp_ref, prev_x):
        @pl.when(pl.program_id(0) == 0)
        def _():
            prev_x[...] = jnp.zeros_like(prev_x)
        x = x_ref[...]
        cp = _conv_causal(x, prev_x[...], cw_ref, 4)
        prev_x[...] = x[tm - 8:, :]
        cp_ref[...] = cp
        c = cp * _sigmoid(cp)
        _l2n_heads(c, q_ref, 0, Q_SCALE)
        _l2n_heads(c, k_ref, wq, 1.0)
        v_ref[...] = c[:, 2 * wq:]
        lane, g, beta = _dn_scalars(ab_ref[...], a_ref[...], d_ref[...])
        gc = _chunk_cumsum(g, 0)
        gl = gc + _chunk_cumsum(g, 0, rev=True) - g
        pack = jnp.where(lane < NV, gc, jnp.where(lane < 2 * NV, beta,
                         jnp.where(lane < 3 * NV, pltpu.roll(gl, 2 * NV, 1), 0.0)))
        rt_ref[...] = pack.T[0:3 * NV, :]

    return pl.pallas_call(
        body, name="dn_prep", grid=(s // tm,),
        in_specs=[_rows(tm, 4 * wq), _rows(tm, 128), _whole((8, 4 * wq)), _whole((1, 128)), _whole((1, 128))],
        out_specs=[_rows(tm, wq), _rows(tm, wq), _rows(tm, 2 * wq),
                   pl.BlockSpec((3 * NV, tm), lambda i: (0, i)), _rows(tm, 4 * wq)],
        out_shape=[jax.ShapeDtypeStruct((s, wq), F32), jax.ShapeDtypeStruct((s, wq), F32),
                   jax.ShapeDtypeStruct((s, 2 * wq), F32), jax.ShapeDtypeStruct((3 * NV, s), F32),
                   jax.ShapeDtypeStruct((s, 4 * wq), F32)],
        scratch_shapes=[pltpu.VMEM((8, 4 * wq), F32)],
        compiler_params=_params(("arbitrary",)),
    )(pqkv, pab, cw, arow, drow)


def _pair_masks():
    i = _iota((2 * CH, 2 * CH), 0)
    j = _iota((2 * CH, 2 * CH), 1)
    same = (i >> 6) == (j >> 6)
    return same & (i >= j), same & (i > j)


def _head_cols(rt_ref, h):
    shp = (2 * CH, 2 * CH)
    g_r = jnp.broadcast_to(rt_ref[pl.ds(h, 1), :], shp)
    b_r = jnp.broadcast_to(rt_ref[pl.ds(NV + h, 1), :], shp)
    l_r = jnp.broadcast_to(rt_ref[pl.ds(2 * NV + h, 1), :], shp)
    return g_r, g_r.T, b_r.T, l_r, l_r.T


def _inv_unit_lower_many(a_list):
    n = a_list[0].shape[0]
    eye = (_iota((n, n), 0) == _iota((n, n), 1)).astype(F32)
    bs = [-a for a in a_list]
    xs = [eye + b for b in bs]
    for _ in range(5):
        bs = [_dot(b, b) for b in bs]
        xs = [x + _dot(x, b) for x, b in zip(xs, bs)]
    rs = [(eye - x) - _dot3(a, x) for a, x in zip(a_list, xs)]
    return [x + _dot(x, r) for x, r in zip(xs, rs)]


def _gam_rows(l_r):
    lrow = l_r[0:1, :]
    lane = _iota(lrow.shape, 1)
    other = pltpu.roll(lrow, CH, 1)
    return jnp.exp(jnp.where(lane < CH, lrow, other)), jnp.exp(jnp.where(lane >= CH, lrow, other))


def _dn_intra(q, k, v, rt):
    s = q.shape[0]
    nb = s // (2 * CH)
    qps = 8
    blk = pl.BlockSpec((2 * CH, qps * HD), lambda i, h: (i, h))
    blk2 = pl.BlockSpec((2 * CH, 2 * qps * HD), lambda i, h: (i, h))

    def body(q_ref, k_ref, v_ref, rt_ref, u_ref, w_ref, qd_ref, kd_ref, p_ref, ti_ref, gam_ref):
        hstep = pl.program_id(1)
        mc, ms = _pair_masks()
        gam_ref[...] = jnp.zeros_like(gam_ref)
        heads = []
        for qh in range(qps):
            qq = q_ref[:, qh * HD:(qh + 1) * HD]
            kk_ = k_ref[:, qh * HD:(qh + 1) * HD]
            kk = _dot_nt(kk_, kk_)
            qk = _dot_nt(qq, kk_)
            for j in range(2):
                idx = 2 * qh + j
                cs = slice(idx * HD, (idx + 1) * HD)
                g_r, g_c, b_c, l_r, l_c = _head_cols(rt_ref, 2 * qps * hstep + idx)
                dec = jnp.where(mc, jnp.exp(jnp.where(mc, g_c - g_r, 0.0)), 0.0)
                eg = jnp.exp(g_c)
                p_ref[:, cs] = jnp.where(mc, qk * dec, 0.0)
                qd_ref[:, cs] = eg * qq
                kd_ref[:, cs] = jnp.exp(l_c - g_c) * kk_
                ga, gb = _gam_rows(l_r)
                gam_ref[0, qh, 2 * j:2 * j + 1, :] = ga
                gam_ref[0, qh, 2 * j + 1:2 * j + 2, :] = gb
                rhs = jnp.concatenate([b_c * v_ref[:, cs], b_c * eg * kk_], axis=1)
                heads.append((cs, jnp.where(ms, b_c * kk * dec, 0.0), rhs))
        tinvs = _inv_unit_lower_many([a for _, a, _ in heads])
        uws = [_dot3(t, rhs) for t, (_, _, rhs) in zip(tinvs, heads)]
        for t, uw, (cs, _, _) in zip(tinvs, uws, heads):
            ti_ref[:, cs] = t
            u_ref[:, cs] = uw[:, :HD]
            w_ref[:, cs] = uw[:, HD:]

    big = jax.ShapeDtypeStruct((s, NV * HD), F32)
    return pl.pallas_call(
        body, name="dn_intra", grid=(nb, NQ // qps),
        in_specs=[blk, blk, blk2, pl.BlockSpec((3 * NV, 2 * CH), lambda i, h: (0, i))],
        out_specs=[blk2] * 6 + [pl.BlockSpec((1, qps, 8, 128), lambda i, h: (i, h, 0, 0))],
        out_shape=[big] * 6 + [jax.ShapeDtypeStruct((nb, NQ, 8, 128), F32)],
        compiler_params=_params(("parallel", "parallel")),
    )(q, k, v, rt)


def _dn_seq(u, w, qd, kd, p, gam, pz, nw):
    s = u.shape[0]
    nb = s // (2 * CH)
    wide = pl.BlockSpec((2 * CH, NV * HD), lambda i: (i, 0))

    def body(u_ref, w_ref, qd_ref, kd_ref, p_ref, gam_ref, z_ref, nw_ref, o_ref, vn_ref, ss_ref, y_ref, yt_ref, st):
        @pl.when(pl.program_id(0) == 0)
        def _():
            st[...] = jnp.zeros_like(st)
        ra, rb = slice(0, CH), slice(CH, 2 * CH)
        for g0 in range(0, NV, SEQ_GROUP):
            hs = list(range(g0, g0 + SEQ_GROUP))
            cs = [slice(h * HD, (h + 1) * HD) for h in hs]
            ga = [gam_ref[0, h // 2, 2 * (h % 2):2 * (h % 2) + 1, :] for h in hs]
            gb = [gam_ref[0, h // 2, 2 * (h % 2) + 1:2 * (h % 2) + 2, :] for h in hs]
            s0 = [st[h] for h in hs]
            vna = [u_ref[ra, c] - _dot(w_ref[ra, c], s) for c, s in zip(cs, s0)]
            s1 = [g * s + _dot_tn(kd_ref[ra, c], v) for g, s, c, v in zip(ga, s0, cs, vna)]
            vnb = [u_ref[rb, c] - _dot(w_ref[rb, c], s) for c, s in zip(cs, s1)]
            s2 = [g * s + _dot_tn(kd_ref[rb, c], v) for g, s, c, v in zip(gb, s1, cs, vnb)]
            for h, s in zip(hs, s2):
                st[h] = s
            oa = [_dot(qd_ref[ra, c], s) for c, s in zip(cs, s0)]
            ob = [_dot(qd_ref[rb, c], s) for c, s in zip(cs, s1)]
            for i_, h in enumerate(hs):
                vn = jnp.concatenate([vna[i_], vnb[i_]], axis=0)
                oh = jnp.concatenate([oa[i_], ob[i_]], axis=0) + _dot(p_ref[:, cs[i_]], vn)
                o_ref[:, cs[i_]] = oh
                vn_ref[:, cs[i_]] = vn
                ss_ref[h, 0:HD, :] = s0[i_]
                ss_ref[h, HD:2 * HD, :] = s1[i_]
                z = z_ref[:, cs[i_]]
                rs = lax.rsqrt(jnp.mean(oh * oh, axis=-1, keepdims=True) + RMS_EPS)
                y = oh * rs * nw_ref[...] * (z * _sigmoid(z))
                y_ref[:, cs[i_]] = y.astype(MXU_DT)
                yt_ref[cs[i_], :] = y.T.astype(MXU_DT)

    big = jax.ShapeDtypeStruct((s, NV * HD), F32)
    return pl.pallas_call(
        body, name="dn_seq", grid=(nb,),
        in_specs=[wide] * 5 + [pl.BlockSpec((1, NQ, 8, 128), lambda i: (i, 0, 0, 0)), wide, _whole((1, HD))],
        out_specs=[wide, wide, pl.BlockSpec((NV, 2 * HD, HD), lambda i: (0, i, 0)), wide, _cols(2 * CH, NV * HD)],
        out_shape=[big, big, jax.ShapeDtypeStruct((NV, 2 * s, HD), F32),
                   jax.ShapeDtypeStruct((s, NV * HD), MXU_DT), jax.ShapeDtypeStruct((NV * HD, s), MXU_DT)],
        scratch_shapes=[pltpu.VMEM((NV, HD, HD), F32)],
        compiler_params=_params(("arbitrary",)),
    )(u, w, qd, kd, p, gam, pz, nw)


def _ln1(x, mix, adam, lng, lnb):
    s = x.shape[0]
    tm = _pick(s, (512, 256, 128))

    def body(x_ref, m_ref, ada_ref, g_ref, b_ref, r_ref, x2_ref, h2_ref, h2t_ref):
        r = ALPHA * x_ref[...] + (1.0 + ada_ref[2:3, :]) * m_ref[...]
        xhat, _ = _ln_stats(r)
        x2 = xhat * g_ref[...] + b_ref[...]
        r_ref[...] = r
        x2_ref[...] = x2
        h2 = x2 * (1.0 + ada_ref[4:5, :]) + ada_ref[3:4, :]
        h2_ref[...] = h2.astype(MXU_DT)
        h2t_ref[...] = h2.T.astype(MXU_DT)

    return pl.pallas_call(
        body, name="ln1", grid=(s // tm,),
        in_specs=[_rows(tm, D), _rows(tm, D), _whole((8, D)), _whole((1, D)), _whole((1, D))],
        out_specs=[_rows(tm, D)] * 3 + [_cols(tm, D)],
        out_shape=[jax.ShapeDtypeStruct((s, D), F32), jax.ShapeDtypeStruct((s, D), F32),
                   jax.ShapeDtypeStruct((s, D), MXU_DT), jax.ShapeDtypeStruct((D, s), MXU_DT)],
        compiler_params=_params(("parallel",)),
    )(x, mix, adam, lng, lnb)


def _ffn_act(gu, cw, cb):
    s = gu.shape[0]
    tm = _pick(s, (256, 128))
    w = D_FF

    def body(g_ref, u_ref, cw_ref, cb_ref, o_ref, ot_ref, gc_ref, prev):
        @pl.when(pl.program_id(0) == 0)
        def _():
            prev[...] = jnp.zeros_like(prev)
        g = g_ref[...]
        gc = _conv_causal(g, prev[...], cw_ref, 3) + cb_ref[...]
        prev[...] = g[tm - 8:, :]
        gc_ref[...] = gc
        act = _gelu(gc) * u_ref[...]
        o_ref[...] = act.astype(MXU_DT)
        ot_ref[...] = act.T.astype(MXU_DT)

    return pl.pallas_call(
        body, name="ffn_act", grid=(s // tm,),
        in_specs=[_rows(tm, w, 0), _rows(tm, w, 1), _whole((8, w)), _whole((1, w))],
        out_specs=[_rows(tm, w), _cols(tm, w), _rows(tm, w)],
        out_shape=[jax.ShapeDtypeStruct((s, w), MXU_DT), jax.ShapeDtypeStruct((w, s), MXU_DT),
                   jax.ShapeDtypeStruct((s, w), F32)],
        scratch_shapes=[pltpu.VMEM((8, w), F32)],
        compiler_params=_params(("arbitrary",)),
    )(gu, gu, cw, cb)


def _ffn_bwd(dact, gu, gc, cw):
    s = dact.shape[0]
    tm = _pick(s, (256, 128))
    nt = s // tm
    w = D_FF

    def body(da_ref, g_ref, u_ref, gc_ref, cw_ref, o_ref, gcw_ref, gcb_ref, nxt):
        @pl.when(pl.program_id(0) == 0)
        def _():
            nxt[...] = jnp.zeros_like(nxt)
            gcw_ref[...] = jnp.zeros_like(gcw_ref)
            gcb_ref[...] = jnp.zeros_like(gcb_ref)
        gel, dgel = _gelu_and_grad(gc_ref[...])
        da = da_ref[...]
        dgc = da * u_ref[...] * dgel
        o_ref[:, w:] = (da * gel).astype(MXU_DT)
        o_ref[:, :w] = _conv_causal_bwd(dgc, nxt[...], cw_ref, 3, g_ref[...], gcw_ref).astype(MXU_DT)
        nxt[...] = dgc[:8, :]
        gcb_ref[...] += _rsum(dgc)

    return pl.pallas_call(
        body, name="ffn_bwd", grid=(nt,),
        in_specs=[_rows(tm, w, 0, nt), _rows(tm, w, 0, nt), _rows(tm, w, 1, nt), _rows(tm, w, 0, nt),
                  _whole((8, w))],
        out_specs=[_rows(tm, 2 * w, 0, nt), _whole((8, w)), _whole((1, w))],
        out_shape=[jax.ShapeDtypeStruct((s, 2 * w), MXU_DT), jax.ShapeDtypeStruct((8, w), F32),
                   jax.ShapeDtypeStruct((1, w), F32)],
        scratch_shapes=[pltpu.VMEM((8, w), F32)],
        compiler_params=_params(("arbitrary",)),
    )(dact, gu, gu, gc, cw)


def _dn_seq_bwd(do, qd, kd, p, w, vn, ssave, gam):
    s = do.shape[0]
    nb = s // (2 * CH)
    wide = pl.BlockSpec((2 * CH, NV * HD), lambda i: (nb - 1 - i, 0))
    gspec = pl.BlockSpec((1, NQ, 8, 128), lambda i: (nb - 1 - i, 0, 0, 0))

    def body(do_ref, qd_ref, kd_ref, p_ref, w_ref, vn_ref, ss_ref, gam_ref, dvn_ref, dkd_ref, dgam_ref, dst):
        @pl.when(pl.program_id(0) == 0)
        def _():
            dst[...] = jnp.zeros_like(dst)
        dgam_ref[...] = jnp.zeros_like(dgam_ref)
        ra, rb = slice(0, CH), slice(CH, 2 * CH)
        tot = lambda t: jnp.sum(jnp.sum(t, axis=1, keepdims=True), axis=0, keepdims=True)
        for g0 in range(0, NV, SEQ_GROUP):
            hs = list(range(g0, g0 + SEQ_GROUP))
            cs = [slice(h * HD, (h + 1) * HD) for h in hs]
            ga = [gam_ref[0, h // 2, 2 * (h % 2):2 * (h % 2) + 1, :] for h in hs]
            gb = [gam_ref[0, h // 2, 2 * (h % 2) + 1:2 * (h % 2) + 2, :] for h in hs]
            ds2 = [dst[h] for h in hs]
            pdo = [_dot_tn(p_ref[:, c], do_ref[:, c]) for c in cs]
            qdo_b = [_dot_tn(qd_ref[rb, c], do_ref[rb, c]) for c in cs]
            qdo_a = [_dot_tn(qd_ref[ra, c], do_ref[ra, c]) for c in cs]
            dvb = [p_[rb] + _dot(kd_ref[rb, c], d_) for p_, c, d_ in zip(pdo, cs, ds2)]
            ds1 = [g * d_ + q_ - _dot_tn(w_ref[rb, c], v_)
                   for g, d_, q_, c, v_ in zip(gb, ds2, qdo_b, cs, dvb)]
            dva = [p_[ra] + _dot(kd_ref[ra, c], d_) for p_, c, d_ in zip(pdo, cs, ds1)]
            ds0 = [g * d_ + q_ - _dot_tn(w_ref[ra, c], v_)
                   for g, d_, q_, c, v_ in zip(ga, ds1, qdo_a, cs, dva)]
            for h, d_ in zip(hs, ds0):
                dst[h] = d_
            for i_, h in enumerate(hs):
                c = cs[i_]
                row = 2 * (h % 2)
                dkd_ref[rb, c] = _dot_nt(vn_ref[rb, c], ds2[i_])
                dkd_ref[ra, c] = _dot_nt(vn_ref[ra, c], ds1[i_])
                dvn_ref[ra, c] = dva[i_]
                dvn_ref[rb, c] = dvb[i_]
                dgam_ref[0, h // 2, row:row + 1, :] = jnp.broadcast_to(tot(ds1[i_] * ss_ref[h, 0:HD, :]), (1, 128))
                dgam_ref[0, h // 2, row + 1:row + 2, :] = jnp.broadcast_to(
                    tot(ds2[i_] * ss_ref[h, HD:2 * HD, :]), (1, 128))

    big = jax.ShapeDtypeStruct((s, NV * HD), F32)
    return pl.pallas_call(
        body, name="dn_seq_bwd", grid=(nb,),
        in_specs=[wide] * 6 + [pl.BlockSpec((NV, 2 * HD, HD), lambda i: (0, nb - 1 - i, 0)), gspec],
        out_specs=[wide, wide, gspec],
        out_shape=[big, big, jax.ShapeDtypeStruct((nb, NQ, 8, 128), F32)],
        scratch_shapes=[pltpu.VMEM((NV, HD, HD), F32)],
        compiler_params=_params(("arbitrary",)),
    )(do, qd, kd, p, w, vn, ssave, gam)


def _dn_intra_bwd(q, k, v, rt, do, dvn, dkd, gam, dgam, ssave, tinv, u, w, vn):
    s = q.shape[0]
    nb = s // (2 * CH)
    qps = 8
    nh = 2 * qps
    blk = pl.BlockSpec((2 * CH, qps * HD), lambda i, h: (i, h))
    blk2 = pl.BlockSpec((2 * CH, nh * HD), lambda i, h: (i, h))
    gspec = pl.BlockSpec((1, qps, 8, 128), lambda i, h: (i, h, 0, 0))
    rspec = pl.BlockSpec((3 * NV, 2 * CH), lambda i, h: (0, i))

    def body(q_ref, k_ref, v_ref, rt_ref, do_ref, dvn_ref, dkd_ref, gam_ref, dgam_ref, ss_ref,
             ti_ref, u_ref, w_ref, vn_ref, dq_ref, dk_ref, dv_ref, drt_ref, acc):
        hstep = pl.program_id(1)

        @pl.when(hstep == 0)
        def _():
            acc[...] = jnp.zeros_like(acc)
        mc, ms = _pair_masks()
        ra, rb = slice(0, CH), slice(CH, 2 * CH)
        lane = _iota((1, 2 * CH), 1)
        hs = list(range(nh))
        qh = [h // 2 for h in hs]
        cs = [slice(h * HD, (h + 1) * HD) for h in hs]
        qq_ = [q_ref[:, t * HD:(t + 1) * HD] for t in range(qps)]
        kk_ = [k_ref[:, t * HD:(t + 1) * HD] for t in range(qps)]
        kk = [_dot_nt(k_, k_) for k_ in kk_]
        qk = [_dot_nt(q_, k_) for q_, k_ in zip(qq_, kk_)]
        cols = [_head_cols(rt_ref, nh * hstep + h) for h in hs]
        b_c = [c_[2] for c_ in cols]
        dec = [jnp.where(mc, jnp.exp(jnp.where(mc, c_[1] - c_[0], 0.0)), 0.0) for c_ in cols]
        eg = [jnp.exp(c_[1]) for c_ in cols]
        egl = [jnp.exp(c_[4] - c_[1]) for c_ in cols]
        dob = [do_ref[:, c] for c in cs]
        dvb = [dvn_ref[:, c] for c in cs]
        dqd = [jnp.concatenate([_dot_nt(d_[ra], ss_ref[h, 0:HD, :]), _dot_nt(d_[rb], ss_ref[h, HD:2 * HD, :])], axis=0)
               for h, d_ in zip(hs, dob)]
        dw = [-jnp.concatenate([_dot_nt(d_[ra], ss_ref[h, 0:HD, :]), _dot_nt(d_[rb], ss_ref[h, HD:2 * HD, :])], axis=0)
              for h, d_ in zip(hs, dvb)]
        dp = [jnp.where(mc, _dot_nt(d_, vn_ref[:, c]), 0.0) for d_, c in zip(dob, cs)]
        dbuw = [_dot3_tn(ti_ref[:, c], jnp.concatenate([d_, w_], axis=1)) for c, d_, w_ in zip(cs, dvb, dw)]
        dbu = [t[:, :HD] for t in dbuw]
        dbw = [t[:, HD:] for t in dbuw]
        da = [jnp.where(ms, -(_dot_nt(bu, u_ref[:, c]) + _dot_nt(bw, w_ref[:, c])), 0.0)
              for bu, bw, c in zip(dbu, dbw, cs)]
        dm = [a_ * d_ for a_, d_ in zip(da, dec)]
        dn_ = [p_ * d_ for p_, d_ in zip(dp, dec)]
        dbk = [_dot(m_, kk_[t]) for m_, t in zip(dm, qh)]
        dqs = [_dot(n_, kk_[t]) + e_ * q_ for n_, t, e_, q_ in zip(dn_, qh, eg, dqd)]
        dks = [_dot_tn(m_, b_ * kk_[t]) + _dot_tn(n_, qq_[t]) + el * dkd_ref[:, c] + b_ * (e_ * bw + bk)
               for m_, b_, t, n_, el, c, e_, bw, bk in zip(dm, b_c, qh, dn_, egl, cs, eg, dbw, dbk)]
        for t in range(qps):
            dq_ref[:, t * HD:(t + 1) * HD] = dqs[2 * t] + dqs[2 * t + 1]
            dk_ref[:, t * HD:(t + 1) * HD] = dks[2 * t] + dks[2 * t + 1]
        for h in hs:
            c, t, j = cs[h], qh[h], h % 2
            dv_ref[:, c] = b_c[h] * dbu[h]
            e = da[h] * (b_c[h] * kk[t] * dec[h]) + dp[h] * (qk[t] * dec[h])
            x = dkd_ref[:, c] * (egl[h] * kk_[t])
            egk = eg[h] * kk_[t]
            z = e + dqd[h] * (eg[h] * qq_[t]) - x + dbw[h] * (b_c[h] * egk)
            zb = dbw[h] * egk + dbu[h] * v_ref[:, c] + dbk[h] * kk_[t]
            sa = jnp.sum(jnp.sum(x[ra], axis=1, keepdims=True), axis=0, keepdims=True)
            sb = jnp.sum(jnp.sum(x[rb], axis=1, keepdims=True), axis=0, keepdims=True)
            la = sa + dgam_ref[0, t, 2 * j:2 * j + 1, :] * gam_ref[0, t, 2 * j:2 * j + 1, :]
            lb = sb + dgam_ref[0, t, 2 * j + 1:2 * j + 2, :] * gam_ref[0, t, 2 * j + 1:2 * j + 2, :]
            hg = nh * hstep + h
            acc[pl.ds(hg, 1), :] = _rsum(z.T - e)
            acc[pl.ds(NV + hg, 1), :] = _rsum(zb.T)
            acc[pl.ds(2 * NV + hg, 1), :] = jnp.where(lane < CH, la, lb)

        @pl.when(hstep == NQ // qps - 1)
        def _():
            drt_ref[...] = acc[...]

    return pl.pallas_call(
        body, name="dn_intra_bwd", grid=(nb, NQ // qps),
        in_specs=[blk, blk, blk2, rspec, blk2, blk2, blk2, gspec, gspec,
                  pl.BlockSpec((nh, 2 * HD, HD), lambda i, h: (h, i, 0)), blk2, blk2, blk2, blk2],
        out_specs=[blk, blk, blk2, rspec],
        out_shape=[jax.ShapeDtypeStruct((s, NQ * HD), F32), jax.ShapeDtypeStruct((s, NQ * HD), F32),
                   jax.ShapeDtypeStruct((s, NV * HD), F32), jax.ShapeDtypeStruct((3 * NV, s), F32)],
        scratch_shapes=[pltpu.VMEM((3 * NV, 2 * CH), F32)],
        compiler_params=_params(("parallel", "arbitrary")),
    )(q, k, v, rt, do, dvn, dkd, gam, dgam, ssave, tinv, u, w, vn)


def _l2n_heads_bwd(c, d_ref, dc_ref, off, scale):
    for hh in range(NQ):
        cs = slice(off + hh * HD, off + (hh + 1) * HD)
        x = c[:, cs]
        dy = d_ref[:, hh * HD:(hh + 1) * HD]
        r = lax.rsqrt(jnp.sum(x * x, axis=-1, keepdims=True) + L2_EPS)
        dc_ref[:, cs] = (scale * r) * (dy - x * (r * r) * jnp.sum(dy * x, axis=-1, keepdims=True))


def _dn_prep_bwd(dq, dk, dv, drt, pqkv, cpre, pab, cw, acol, dcol):
    s = pqkv.shape[0]
    tm = 128
    nt = s // tm
    wq = NQ * HD

    def body(dq_ref, dk_ref, dv_ref, drt_ref, x_ref, cp_ref, ab_ref, cw_ref, ac_ref,
             dc_ref, dx_ref, dab_ref, gcw_ref, gsc_ref, dcs, nxt):
        @pl.when(pl.program_id(0) == 0)
        def _():
            nxt[...] = jnp.zeros_like(nxt)
            gcw_ref[...] = jnp.zeros_like(gcw_ref)
            gsc_ref[...] = jnp.zeros_like(gsc_ref)
        cp = cp_ref[...]
        sg = _sigmoid(cp)
        c = cp * sg
        _l2n_heads_bwd(c, dq_ref, dcs, 0, Q_SCALE)
        _l2n_heads_bwd(c, dk_ref, dcs, wq, 1.0)
        dcs[:, 2 * wq:] = dv_ref[...]
        dcp = dcs[...] * (sg * (1.0 + cp * (1.0 - sg)))
        dx_ref[...] = _conv_causal_bwd(dcp, nxt[...], cw_ref, 4, x_ref[...], gcw_ref).astype(MXU_DT)
        nxt[...] = dcp[:8, :]
        lane = _iota((NV, tm), 1)
        dgt = drt_ref[0:NV, :] + jnp.where((lane & (CH - 1)) == CH - 1, drt_ref[2 * NV:3 * NV, :], 0.0)
        dg = _chunk_cumsum(dgt, 1, rev=True)
        abt = ab_ref[...].T
        zt = abt[0:NV, :] + dc_ref[...]
        gt = -jnp.exp(ac_ref[...]) * _softplus(zt)
        dat = dg * (-jnp.exp(ac_ref[...])) * _sigmoid(zt)
        bt = _sigmoid(abt[NV:2 * NV, :])
        dbt = drt_ref[NV:2 * NV, :] * bt * (1.0 - bt)
        full = jnp.concatenate([dat, dbt, jnp.zeros((128 - 2 * NV, tm), F32)], axis=0)
        dab_ref[...] = full.T.astype(MXU_DT)
        l2 = _iota((NV, 128), 1)
        gsc_ref[...] += jnp.where(l2 == 0, jnp.sum(dg * gt, axis=1, keepdims=True),
                                  jnp.where(l2 == 1, jnp.sum(dat, axis=1, keepdims=True), 0.0))

    return pl.pallas_call(
        body, name="dn_prep_bwd", grid=(nt,),
        in_specs=[_rows(tm, wq, 0, nt), _rows(tm, wq, 0, nt), _rows(tm, 2 * wq, 0, nt),
                  pl.BlockSpec((3 * NV, tm), lambda i: (0, nt - 1 - i)),
                  _rows(tm, 4 * wq, 0, nt), _rows(tm, 4 * wq, 0, nt), _rows(tm, 128, 0, nt),
                  _whole((8, 4 * wq)), _whole((NV, 1)), _whole((NV, 1))],
        out_specs=[_rows(tm, 4 * wq, 0, nt), _rows(tm, 128, 0, nt), _whole((8, 4 * wq)), _whole((NV, 128))],
        out_shape=[jax.ShapeDtypeStruct((s, 4 * wq), MXU_DT), jax.ShapeDtypeStruct((s, 128), MXU_DT),
                   jax.ShapeDtypeStruct((8, 4 * wq), F32), jax.ShapeDtypeStruct((NV, 128), F32)],
        scratch_shapes=[pltpu.VMEM((tm, 4 * wq), F32), pltpu.VMEM((8, 4 * wq), F32)],
        compiler_params=_params(("arbitrary",)),
    )(dq, dk, dv, drt, pqkv, cpre, pab, cw, acol, dcol)


def _rg_bwd(drec, px, xcs, h, cw, vec, wa, wx, wat, wxt, side=None):
    s = px.shape[0]
    tm = _pick(s, (256, 128))
    nt = s // tm
    w = D_RNN
    n_in = len(side.ins) if side else 0
    n_out = len(side.out_shapes) if side else 0

    def body(*refs):
        (dr_ref, xr_ref, xc_ref, gr_ref, h_ref, hb_ref, cw_ref, vec_ref, wa_ref, wx_ref, wat_ref,
         wxt_ref) = refs[:12]
        s_in = refs[12:12 + n_in]
        o_ref, gwa_ref, gwx_ref, gcw_ref, gvec_ref = refs[12 + n_in:17 + n_in]
        s_out = refs[17 + n_in:17 + n_in + n_out]
        nxt_a, nxt_l, nxt_d = refs[17 + n_in + n_out:20 + n_in + n_out]
        sems = refs[20 + n_in + n_out:]
        i = pl.program_id(0)
        if side:
            @pl.when(i == 0)
            def _():
                side.start(s_in, s_out, sems)

        @pl.when(i == 0)
        def _():
            nxt_a[...] = jnp.zeros_like(nxt_a)
            nxt_l[...] = jnp.zeros_like(nxt_l)
            nxt_d[...] = jnp.zeros_like(nxt_d)
            gwa_ref[...] = jnp.zeros_like(gwa_ref)
            gwx_ref[...] = jnp.zeros_like(gwx_ref)
            gcw_ref[...] = jnp.zeros_like(gcw_ref)
            gvec_ref[...] = jnp.zeros_like(gvec_ref)
        hbefore = jnp.where(i == nt - 1, 0.0, hb_ref[...])
        xc = xc_ref[...]
        r, ig, a, mult, sp, rmult = _rg_gates(xc, wa_ref, wx_ref, vec_ref)
        hh = h_ref[...]
        gel, dgel = _gelu_and_grad(gr_ref[...])
        drec_ = dr_ref[...]
        o_ref[:, w:] = (drec_ * hh * dgel).astype(MXU_DT)
        acum, lam = _scan_rev(_shift_up(a, nxt_a[...], 1), drec_ * gel)
        lam = lam + acum * nxt_l[0:1, :]
        nxt_a[...] = a[:8, :]
        nxt_l[...] = lam[:8, :]
        da = lam * _shift_down(hh, hbefore, 1)
        dxc = lam * mult * ig
        dla = da * a - (lam * ig * xc) * (a * a) * rmult
        dpr = dla * (-RG_C * sp) * r * (1.0 - r)
        dpi = (lam * mult * xc) * ig * (1.0 - ig)
        dprb = dpr.astype(MXU_DT)
        dpib = dpi.astype(MXU_DT)
        dxc = dxc + jnp.dot(dprb, wat_ref[...], preferred_element_type=F32) \
                  + jnp.dot(dpib, wxt_ref[...], preferred_element_type=F32)
        xcb = xc.astype(MXU_DT)
        gwa_ref[...] += _dot_tn(xcb, dprb)
        gwx_ref[...] += _dot_tn(xcb, dpib)
        o_ref[:, :w] = _conv_causal_bwd(dxc, nxt_d[...], cw_ref, 4, xr_ref[...], gcw_ref).astype(MXU_DT)
        nxt_d[...] = dxc[:8, :]
        gvec_ref[0:1, :] += _rsum(dxc)
        gvec_ref[1:2, :] += _rsum(dpr)
        gvec_ref[2:3, :] += _rsum(dpi)
        gvec_ref[3:4, :] += _rsum(dla * (-RG_C * r)) * (-_sigmoid(-vec_ref[3:4, :]))
        if side:
            @pl.when(i == nt - 1)
            def _():
                side.finish(s_in, s_out, sems)

    return pl.pallas_call(
        body, name="rg_bwd", grid=(nt,),
        in_specs=[_rows(tm, w, 0, nt), _rows(tm, w, 0, nt), _rows(tm, w, 0, nt), _rows(tm, w, 1, nt),
                  _rows(tm, w, 0, nt), _before(tm, w, 0, nt), _whole((8, w)), _whole((8, w)),
                  _whole((w, w)), _whole((w, w)), _whole((w, w)), _whole((w, w))] + [_HBM] * n_in,
        out_specs=[_rows(tm, 2 * w, 0, nt), _whole((w, w)), _whole((w, w)), _whole((8, w)), _whole((8, w))]
        + [_HBM] * n_out,
        out_shape=[jax.ShapeDtypeStruct((s, 2 * w), MXU_DT), jax.ShapeDtypeStruct((w, w), F32),
                   jax.ShapeDtypeStruct((w, w), F32), jax.ShapeDtypeStruct((8, w), F32),
                   jax.ShapeDtypeStruct((8, w), F32)] + (list(side.out_shapes) if side else []),
        scratch_shapes=[pltpu.VMEM((8, w), F32)] * 3 + (list(side.sems) if side else []),
        compiler_params=_params(("arbitrary",)),
    )(drec, px, xcs, px, h, h, cw, vec, wa, wx, wat, wxt, *(side.ins if side else []))


def _mm_epi(a, b, *, name, epi, extras=(), wholes=(), row_outs=(), acc_outs=(), tm=None, side=None):
    m, kk = a.shape
    n = b.shape[1]
    tm = tm or _pick(m, (512, 256, 128))
    tk = kk if kk <= 2816 else _pick(kk, (2816, 2176, 2048, 1024))
    nk, ni = kk // tk, m // tm
    n_ex, n_wh, n_ro, n_ao = len(extras), len(wholes), len(row_outs), len(acc_outs)
    n_si = len(side.ins) if side else 0
    n_so = len(side.out_shapes) if side else 0

    def body(*refs):
        a_ref, b_ref = refs[:2]
        p = 2
        ex, p = refs[p:p + n_ex], p + n_ex
        wh, p = refs[p:p + n_wh], p + n_wh
        s_in, p = refs[p:p + n_si], p + n_si
        ro, p = refs[p:p + n_ro], p + n_ro
        ao, p = refs[p:p + n_ao], p + n_ao
        s_out, p = refs[p:p + n_so], p + n_so
        i, k = pl.program_id(0), pl.program_id(1)
        first = (i == 0) & (k == 0)
        if side:
            sems = refs[p + (1 if nk > 1 else 0):]

            @pl.when(first)
            def _():
                side.start(s_in, s_out, sems)
        if n_ao:
            @pl.when(first)
            def _():
                for r_ in ao:
                    r_[...] = jnp.zeros_like(r_)
        if nk == 1:
            epi(_dot(a_ref[...], b_ref[...]), ex, wh, ro, ao)
        else:
            acc = refs[p]

            @pl.when(k == 0)
            def _():
                acc[...] = jnp.zeros_like(acc)
            acc[...] += _dot(a_ref[...], b_ref[...])

            @pl.when(k == nk - 1)
            def _():
                epi(acc[...], ex, wh, ro, ao)
        if side:
            @pl.when((i == ni - 1) & (k == nk - 1))
            def _():
                side.finish(s_in, s_out, sems)

    in_specs = [pl.BlockSpec((tm, tk), lambda i, k: (i, k)), pl.BlockSpec((tk, n), lambda i, k: (k, 0))]
    in_specs += [pl.BlockSpec((tm, w_), functools.partial(lambda i, k, c_: (i, c_), c_=c_)) for _, w_, c_ in extras]
    in_specs += [_whole(x_.shape) for x_ in wholes] + [_HBM] * n_si
    out_specs = [pl.BlockSpec((tm, w_), lambda i, k: (i, 0)) for w_, _ in row_outs]
    out_specs += [_whole(sh) for sh in acc_outs] + [_HBM] * n_so
    out_shape = [jax.ShapeDtypeStruct((m, w_), dt) for w_, dt in row_outs]
    out_shape += [jax.ShapeDtypeStruct(sh, F32) for sh in acc_outs] + (list(side.out_shapes) if side else [])
    scratch = ([pltpu.VMEM((tm, n), F32)] if nk > 1 else []) + (list(side.sems) if side else [])
    return pl.pallas_call(
        body, name=name, grid=(ni, nk), in_specs=in_specs, out_specs=out_specs, out_shape=out_shape,
        scratch_shapes=scratch, compiler_params=_params(("arbitrary", "arbitrary")),
    )(a, b, *[x_ for x_, _, _ in extras], *wholes, *(side.ins if side else []))


def _mm_pro(b, *, name, pro, m, extras=(), wholes=(), row_outs=(), col_outs=(), tm=256):
    kk, n = b.shape
    tn = _pick(n, (1408, 1024, 512, 256, 128))
    n_ex, n_wh, n_ro, n_co = len(extras), len(wholes), len(row_outs), len(col_outs)

    def body(*refs):
        ex = refs[:n_ex]
        wh = refs[n_ex:n_ex + n_wh]
        b_ref, o_ref = refs[n_ex + n_wh], refs[n_ex + n_wh + 1]
        ro = refs[n_ex + n_wh + 2:n_ex + n_wh + 2 + n_ro]
        co = refs[n_ex + n_wh + 2 + n_ro:n_ex + n_wh + 2 + n_ro + n_co]
        a_scr = refs[-1]

        @pl.when(pl.program_id(1) == 0)
        def _():
            a_scr[...] = pro(ex, wh, ro, co).astype(MXU_DT)
        o_ref[...] = jnp.dot(a_scr[...], b_ref[...], preferred_element_type=F32)

    in_specs = [pl.BlockSpec((tm, w_), functools.partial(lambda i, j, c_: (i, c_), c_=c_)) for _, w_, c_ in extras]
    in_specs += [_whole(x_.shape) for x_ in wholes] + [pl.BlockSpec((kk, tn), lambda i, j: (0, j))]
    out_specs = [pl.BlockSpec((tm, tn), lambda i, j: (i, j))]
    out_specs += [pl.BlockSpec((tm, w_), lambda i, j: (i, 0)) for w_, _ in row_outs]
    out_specs += [pl.BlockSpec((w_, tm), lambda i, j: (0, i)) for w_, _ in col_outs]
    out_shape = [jax.ShapeDtypeStruct((m, n), F32)] + [jax.ShapeDtypeStruct((m, w_), dt) for w_, dt in row_outs]
    out_shape += [jax.ShapeDtypeStruct((w_, m), dt) for w_, dt in col_outs]
    return pl.pallas_call(
        body, name=name, grid=(m // tm, n // tn), in_specs=in_specs, out_specs=out_specs, out_shape=out_shape,
        scratch_shapes=[pltpu.VMEM((tm, kk), MXU_DT)], compiler_params=_params(("arbitrary", "arbitrary")),
    )(*[x_ for x_, _, _ in extras], *wholes, b.astype(MXU_DT))


def _merge_proj_out(pg, ya, yb, w_out):
    def pro(ex, wh, ro, co):
        ga_ref, gb_ref, ya_ref, yb_ref = ex
        mg = _sigmoid(ga_ref[...]) * ya_ref[...] + _sigmoid(gb_ref[...]) * yb_ref[...]
        co[0][...] = mg.T.astype(MXU_DT)
        return mg

    return _mm_pro(w_out, name="merge_proj_out", pro=pro, m=ya.shape[0],
                   extras=[(pg, D, 0), (pg, D, 1), (ya, D, 0), (yb, D, 0)], col_outs=[(D, MXU_DT)])


def _ffn_down_ln2_loss(act, w_down, x2, tgt, adam, lng, lnb):
    def epi(ff_, ex, wh, ro, ao):
        x_ref, t_ref = ex
        ada_ref, g_ref, b_ref = wh
        dff_ref, dx_ref = ro
        red_ref, = ao
        r = ALPHA * x_ref[...] + (1.0 + ada_ref[5:6, :]) * ff_
        xhat, rstd = _ln_stats(r)
        err = xhat * g_ref[...] + b_ref[...] - t_ref[...]
        dy = err * (1.0 / D)
        dr = _ln_bwd(dy, xhat, rstd, g_ref[...])
        dff_ref[...] = ((1.0 + ada_ref[5:6, :]) * dr).astype(MXU_DT)
        dx_ref[...] = ALPHA * dr
        red_ref[0:1, :] += _rsum(dy * xhat)
        red_ref[1:2, :] += _rsum(dy)
        red_ref[2:3, :] += _rsum(dr * ff_)
        red_ref[3:4, :] += jnp.sum(_rsum(err * err), axis=1, keepdims=True) * (0.5 / D)

    return _mm_epi(act, w_down, name="ffn_down_ln2_loss", epi=epi, extras=[(x2, D, 0), (tgt, D, 0)],
                   wholes=[adam, lng, lnb], row_outs=[(D, MXU_DT), (D, F32)], acc_outs=[(8, D)])


def _d_h1_modulate_bwd(dproj, w_t, dxa, x, adam, side):
    def epi(dh, ex, wh, ro, ao):
        dxa_ref, x_ref = ex
        ada_ref, = wh
        ro[0][...] = dxa_ref[...] + dh * (1.0 + ada_ref[1:2, :])
        ao[0][0:1, :] += _rsum(dh * x_ref[...])
        ao[0][1:2, :] += _rsum(dh)

    return _mm_epi(dproj, w_t, name="d_h1", epi=epi, extras=[(dxa, D, 0), (x, D, 0)], wholes=[adam],
                   row_outs=[(D, F32)], acc_outs=[(8, D)], side=side)


def _d_merged_bwd(dmix, w_out_t, pg, ya, yb):
    def epi(d, ex, wh, ro, ao):
        ga_ref, gb_ref, ya_ref, yb_ref = ex
        dya_ref, dyb_ref, dpg_ref = ro
        sa = _sigmoid(ga_ref[...])
        sb = _sigmoid(gb_ref[...])
        dya_ref[...] = (d * sa).astype(MXU_DT)
        dyb_ref[...] = (d * sb).astype(MXU_DT)
        dpg_ref[:, :D] = (d * ya_ref[...] * sa * (1.0 - sa)).astype(MXU_DT)
        dpg_ref[:, D:] = (d * yb_ref[...] * sb * (1.0 - sb)).astype(MXU_DT)

    return _mm_epi(dmix, w_out_t, name="d_merged", epi=epi,
                   extras=[(pg, D, 0), (pg, D, 1), (ya, D, 0), (yb, D, 0)],
                   row_outs=[(D, MXU_DT), (D, MXU_DT), (2 * D, MXU_DT)])


def _d_dn_post_bwd(dyb, w_pb_t, o, pz, nw):
    def epi(d_all, ex, wh, ro, ao):
        o_ref, z_ref = ex
        nw_ref, = wh
        do_ref, dz_ref = ro
        acc = jnp.zeros((1, HD), F32)
        for h in range(NV):
            cs = slice(h * HD, (h + 1) * HD)
            oh = o_ref[:, cs]
            z = z_ref[:, cs]
            d = d_all[:, cs]
            sg = _sigmoid(z)
            rs = lax.rsqrt(jnp.mean(oh * oh, axis=-1, keepdims=True) + RMS_EPS)
            n = oh * rs
            dz_ref[:, cs] = (d * n * nw_ref[...] * sg * (1.0 + z * (1.0 - sg))).astype(MXU_DT)
            dn_ = d * (z * sg)
            acc = acc + _rsum(dn_ * n)
            dnn = dn_ * nw_ref[...]
            do_ref[:, cs] = rs * (dnn - n * jnp.mean(dnn * n, axis=-1, keepdims=True))
        ao[0][...] += acc

    return _mm_epi(dyb, w_pb_t, name="d_dn", epi=epi, extras=[(o, NV * HD, 0), (pz, NV * HD, 0)], wholes=[nw],
                   row_outs=[(NV * HD, F32), (NV * HD, MXU_DT)], acc_outs=[(1, HD)], tm=256)


def _d_h2_ln1_bwd(dgu, w_gu_t, dx2a, x2, r1, mix, adam, lng):
    def epi(dh, ex, wh, ro, ao):
        dxa_ref, x2_ref, r_ref, m_ref = ex
        ada_ref, g_ref = wh
        dm_ref, dx_ref = ro
        red_ref, = ao
        dx2 = dxa_ref[...] + dh * (1.0 + ada_ref[4:5, :])
        xhat, rstd = _ln_stats(r_ref[...])
        dr = _ln_bwd(dx2, xhat, rstd, g_ref[...])
        dm_ref[...] = ((1.0 + ada_ref[2:3, :]) * dr).astype(MXU_DT)
        dx_ref[...] = ALPHA * dr
        red_ref[0:1, :] += _rsum(dh * x2_ref[...])
        red_ref[1:2, :] += _rsum(dh)
        red_ref[2:3, :] += _rsum(dx2 * xhat)
        red_ref[3:4, :] += _rsum(dx2)
        red_ref[4:5, :] += _rsum(dr * m_ref[...])

    return _mm_epi(dgu, w_gu_t, name="d_h2", epi=epi,
                   extras=[(dx2a, D, 0), (x2, D, 0), (r1, D, 0), (mix, D, 0)], wholes=[adam, lng],
                   row_outs=[(D, MXU_DT), (D, F32)], acc_outs=[(8, D)], tm=256)


def _adamw(parts, w, m, v, name):
    r, c = w.shape
    n_parts = len(parts)
    tm = _row_tile(r, c * 4 * (n_parts + 7))
    c1 = 1.0 - ADAM_B1 ** ADAM_STEP
    c2 = 1.0 - ADAM_B2 ** ADAM_STEP

    def body(*refs):
        g = refs[0][...]
        for p_ref in refs[1:n_parts]:
            g = g + p_ref[...]
        w_ref, m_ref, v_ref, g_out, d_out, m_out, v_out = refs[n_parts:]
        mn = ADAM_B1 * m_ref[...] + (1.0 - ADAM_B1) * g
        vn = ADAM_B2 * v_ref[...] + (1.0 - ADAM_B2) * (g * g)
        g_out[...] = g
        m_out[...] = mn
        v_out[...] = vn
        d_out[...] = -ADAM_LR * ((mn / c1) / (jnp.sqrt(vn / c2) + ADAM_EPS) + ADAM_WD * w_ref[...])

    spec = pl.BlockSpec((tm, c), lambda i: (i, 0))
    return pl.pallas_call(
        body, name=name, grid=(r // tm,),
        in_specs=[spec] * (n_parts + 3), out_specs=[spec] * 4,
        out_shape=[jax.ShapeDtypeStruct((r, c), F32)] * 4, compiler_params=_params(("parallel",)),
    )(*parts, w, m, v)


def _row_tile(r, bytes_per_row):
    for t in (512, 256, 128, 64, 32, 16):
        if r % t == 0 and 2 * t * bytes_per_row <= 20 * 1024 * 1024:
            return t
    return _pick(r, (16, 8))


def _sum_partials(own, recv, name):
    r, c = own.shape
    tm = _row_tile(r, c * (4 + 4 + 3 * 2))

    def body(o_ref, r_ref, out_ref):
        out_ref[...] = ((o_ref[...] + r_ref[0].astype(F32)) + r_ref[1].astype(F32)) + r_ref[2].astype(F32)

    return pl.pallas_call(
        body, name=name, grid=(r // tm,),
        in_specs=[pl.BlockSpec((tm, c), lambda i: (i, 0)), pl.BlockSpec((3, tm, c), lambda i: (0, i, 0))],
        out_specs=pl.BlockSpec((tm, c), lambda i: (i, 0)),
        out_shape=jax.ShapeDtypeStruct((r, c), F32), compiler_params=_params(("parallel",)),
    )(own, recv)


def _sum8(g, name):
    _, r, c = g.shape
    tm = _pick(r, (256, 128, 64, 32, 16, 8))

    def body(g_ref, o_ref):
        acc = g_ref[0]
        for k in range(1, 8):
            acc = acc + g_ref[k]
        o_ref[...] = acc

    return pl.pallas_call(
        body, name=name, grid=(r // tm,),
        in_specs=[pl.BlockSpec((8, tm, c), lambda i: (0, i, 0))],
        out_specs=pl.BlockSpec((tm, c), lambda i: (i, 0)),
        out_shape=jax.ShapeDtypeStruct((r, c), F32), compiler_params=_params(("parallel",)),
    )(g)


def _silu_rows(x):
    def body(x_ref, o_ref):
        xx = x_ref[...]
        o_ref[...] = xx * _sigmoid(xx)

    return pl.pallas_call(body, name="silu_rows", out_shape=jax.ShapeDtypeStruct(x.shape, F32))(x)


def _coords():
    return lax.axis_index("x"), lax.axis_index("y"), lax.axis_index("c")


def _allgather8(v, name):
    r, n = v.shape

    def body(v_ref, out_ref, send_sems, recv_sems):
        x, y, c = _coords()
        me = 4 * x + 2 * y + c
        out_ref[me] = v_ref[...]
        peers = []
        for k in range(1, 8):
            px = 1 - x if k & 4 else x
            py = 1 - y if k & 2 else y
            pc = 1 - c if k & 1 else c
            peers.append((px, py, pc))

        def copy(k, slot, to):
            return pltpu.make_async_remote_copy(
                src_ref=v_ref, dst_ref=out_ref.at[slot], send_sem=send_sems.at[k], recv_sem=recv_sems.at[k],
                device_id=to, device_id_type=MESH)

        sends = [copy(k, me, p) for k, p in enumerate(peers)]
        for cp in sends:
            cp.start()
        for k, (px, py, pc) in enumerate(peers):
            copy(k, 4 * px + 2 * py + pc, (px, py, pc)).wait_recv()
        for cp in sends:
            cp.wait_send()

    return pl.pallas_call(
        body, name=name, out_shape=jax.ShapeDtypeStruct((8, r, n), v.dtype),
        in_specs=[pl.BlockSpec(memory_space=pltpu.VMEM)], out_specs=pl.BlockSpec(memory_space=pltpu.VMEM),
        scratch_shapes=[pltpu.SemaphoreType.DMA((7,)), pltpu.SemaphoreType.DMA((7,))],
        compiler_params=pltpu.CompilerParams(vmem_limit_bytes=VMEM_LIMIT),
    )(v)


def _other_chips(x, y):
    return [(1 - x, y), (x, 1 - y), (1 - x, 1 - y)]


_HBM = pl.BlockSpec(memory_space=pl.ANY)


class _Exchange:
    def __init__(self, ins, out_shapes, sems, start, finish):
        self.ins, self.out_shapes, self.sems, self.start, self.finish = ins, out_shapes, sems, start, finish


def _run_exchange(ex, name):
    n_in, n_out = len(ex.ins), len(ex.out_shapes)

    def body(*refs):
        ins, outs, sems = refs[:n_in], refs[n_in:n_in + n_out], refs[n_in + n_out:]
        ex.start(ins, outs, sems)
        ex.finish(ins, outs, sems)

    return pl.pallas_call(body, name=name, out_shape=list(ex.out_shapes), in_specs=[_HBM] * n_in,
                          out_specs=[_HBM] * n_out, scratch_shapes=list(ex.sems))(*ex.ins)


def _gather_exchange(shards):
    n = len(shards)

    def plan(ins, outs, sems):
        send_sems, recv_sems, local_sems = sems
        x, y, c = _coords()
        s_me = 2 * x + y
        chips = _other_chips(x, y)

        def half(i, slot, hc):
            rh = shards[i].shape[0] // 2
            return outs[i].at[slot, pl.ds(pl.multiple_of(hc * rh, 16), rh), :]

        def copy(i, k, src, dst, to):
            return pltpu.make_async_remote_copy(src_ref=src, dst_ref=dst, send_sem=send_sems.at[6 * i + k],
                                                recv_sem=recv_sems.at[6 * i + k], device_id=to, device_id_type=MESH)

        local = [pltpu.make_async_copy(ins[i], outs[i].at[s_me], local_sems.at[i]) for i in range(n)]
        first = []
        for i in range(n):
            rh = shards[i].shape[0] // 2
            my_half = ins[i].at[pl.ds(pl.multiple_of(c * rh, 16), rh), :]
            first += [copy(i, j, my_half, half(i, s_me, c), (px, py, c)) for j, (px, py) in enumerate(chips)]
        return (x, y, c), chips, half, copy, local, first

    def start(ins, outs, sems):
        _, _, _, _, local, first = plan(ins, outs, sems)
        for cp in local + first:
            cp.start()

    def finish(ins, outs, sems):
        (x, y, c), chips, half, copy, local, first = plan(ins, outs, sems)
        sibling = (x, y, 1 - c)
        passed = []
        for i in range(n):
            for j, (px, py) in enumerate(chips):
                land = half(i, 2 * px + py, c)
                copy(i, j, land, land, (px, py, c)).wait_recv()
                fw = copy(i, 3 + j, land, land, sibling)
                fw.start()
                passed.append(fw)
        for i in range(n):
            for j, (px, py) in enumerate(chips):
                land = half(i, 2 * px + py, 1 - c)
                copy(i, 3 + j, land, land, sibling).wait_recv()
        for cp in first + passed:
            cp.wait_send()
        for cp in local:
            cp.wait()

    return _Exchange(list(shards), [jax.ShapeDtypeStruct((4,) + v.shape, v.dtype) for v in shards],
                     [pltpu.SemaphoreType.DMA((6 * n,)), pltpu.SemaphoreType.DMA((6 * n,)),
                      pltpu.SemaphoreType.DMA((n,))], start, finish)


def _scatter_exchange(gs):
    n = len(gs)

    def copies(ins, outs, sems):
        send_sems, recv_sems = sems
        x, y, c = _coords()
        return [pltpu.make_async_remote_copy(
                    src_ref=ins[i].at[2 * px + py], dst_ref=outs[i].at[j], send_sem=send_sems.at[3 * i + j],
                    recv_sem=recv_sems.at[3 * i + j], device_id=(px, py, c), device_id_type=MESH)
                for i in range(n) for j, (px, py) in enumerate(_other_chips(x, y))]

    def start(ins, outs, sems):
        for cp in copies(ins, outs, sems):
            cp.start()

    def finish(ins, outs, sems):
        cps = copies(ins, outs, sems)
        for cp in cps:
            cp.wait_recv()
        for cp in cps:
            cp.wait_send()

    return _Exchange(list(gs), [jax.ShapeDtypeStruct((3,) + g.shape[1:], g.dtype) for g in gs],
                     [pltpu.SemaphoreType.DMA((3 * n,)), pltpu.SemaphoreType.DMA((3 * n,))], start, finish)


def _allgather_exchange(v):
    def copies(ins, outs, sems):
        send_sems, recv_sems, _ = sems
        x, y, c = _coords()
        me = 4 * x + 2 * y + c
        cps = []
        for k in range(1, 8):
            peer = (1 - x if k & 4 else x, 1 - y if k & 2 else y, 1 - c if k & 1 else c)
            cps.append(pltpu.make_async_remote_copy(
                src_ref=ins[0], dst_ref=outs[0].at[me], send_sem=send_sems.at[k - 1], recv_sem=recv_sems.at[k - 1],
                device_id=peer, device_id_type=MESH))
        return me, cps

    def start(ins, outs, sems):
        me, cps = copies(ins, outs, sems)
        pltpu.make_async_copy(ins[0], outs[0].at[me], sems[2]).start()
        for cp in cps:
            cp.start()

    def finish(ins, outs, sems):
        me, cps = copies(ins, outs, sems)
        for cp in cps:
            cp.wait_recv()
        for cp in cps:
            cp.wait_send()
        pltpu.make_async_copy(ins[0], outs[0].at[me], sems[2]).wait()

    return _Exchange([v], [jax.ShapeDtypeStruct((8,) + v.shape, v.dtype)],
                     [pltpu.SemaphoreType.DMA((7,)), pltpu.SemaphoreType.DMA((7,)), pltpu.SemaphoreType.DMA],
                     start, finish)


def _swap_sibling(vs):
    n = len(vs)

    def body(*refs):
        ins, outs = refs[:n], refs[n:2 * n]
        send_sems, recv_sems = refs[2 * n:]
        x, y, c = _coords()
        cps = [pltpu.make_async_remote_copy(src_ref=ins[i], dst_ref=outs[i], send_sem=send_sems.at[i],
                                            recv_sem=recv_sems.at[i], device_id=(x, y, 1 - c), device_id_type=MESH)
               for i in range(n)]
        for cp in cps:
            cp.start()
        for cp in cps:
            cp.wait()

    return pl.pallas_call(
        body, name="swap_sibling", out_shape=[jax.ShapeDtypeStruct(v.shape, v.dtype) for v in vs],
        in_specs=[_HBM] * n, out_specs=[_HBM] * n,
        scratch_shapes=[pltpu.SemaphoreType.DMA((n,)), pltpu.SemaphoreType.DMA((n,))],
    )(*vs)


def _pad_rows(a, rows):
    return jnp.pad(a, ((0, rows - a.shape[0]), (0, 0)))


def _block_diag(w):
    eye = jnp.eye(RG_BLOCKS, dtype=w.dtype)
    return (eye[:, None, :, None] * w[:, :, None, :]).reshape(D_RNN, D_RNN)


def _diag_blocks(g):
    g4 = g.reshape(RG_BLOCKS, RG_BW, RG_BLOCKS, RG_BW)
    idx = jnp.arange(RG_BLOCKS)
    return g4[idx, :, idx, :]


def _prepare_rest(p):
    w = {}
    for k_, n_ in (("pa", "w_proj_a"), ("pb", "w_proj_b"), ("out", "w_out"), ("down", "ffn_w_down")):
        w[k_] = p[n_].astype(MXU_DT)
        w[k_ + "_t"] = w[k_].T
    w["gu"] = jnp.concatenate([p["ffn_w_gate"], p["ffn_w_up"]], axis=1).astype(MXU_DT)
    w["gu_t"] = w["gu"].T
    return w


def _prepare_first(p):
    w = {}
    wi = p["w_in"].astype(MXU_DT)
    cat = jnp.concatenate([wi[:, 2560:6656], wi[:, 6656:8704], wi[:, 8736:10784], wi[:, 0:2560],
                           wi[:, 8704:8736], jnp.zeros((D, 96), MXU_DT)], axis=1)
    w["in_cat"], w["in_cat_t"] = cat, cat.T
    w["rg_cw"] = _pad_rows(p["rg_conv_w"], 8)
    w["dn_cw"] = _pad_rows(p["dn_conv_w"], 8)
    w["ffn_cw"] = _pad_rows(p["ffn_conv_w"], 8)
    w["rg_vec"] = _pad_rows(jnp.stack([p["rg_conv_b"], p["rg_b_a"], p["rg_b_x"], p["rg_lambda"]]), 8)
    w["wa"] = _block_diag(p["rg_w_a"]).astype(MXU_DT)
    w["wx"] = _block_diag(p["rg_w_x"]).astype(MXU_DT)
    w["wa_t"], w["wx_t"] = w["wa"].T, w["wx"].T
    w["arow"] = jnp.pad(p["dn_a_log"], (0, 128 - NV))[None, :]
    w["drow"] = jnp.pad(p["dn_dt_bias"], (0, 128 - NV))[None, :]
    w["acol"] = p["dn_a_log"][:, None]
    w["dcol"] = p["dn_dt_bias"][:, None]
    w["nw"] = p["dn_norm_w"][None, :]
    w["ffn_cb"] = p["ffn_conv_b"][None, :]
    for n_ in ("ln1_g", "ln1_b", "ln2_g", "ln2_b"):
        w[n_] = p[n_][None, :]
    return w


def _mm_sided(a, b, side, **kw):
    if side is None:
        return _mm(a, b, **kw), []
    res = _mm(a, b, side=side, **kw)
    return res[0], res[1:]


_SMALL = ("rg_conv_w", "dn_conv_w", "ffn_conv_w", "rg_conv_b", "rg_w_a", "rg_b_a", "rg_w_x", "rg_b_x",
          "rg_lambda", "dn_a_log", "dn_dt_bias", "dn_norm_w", "ln1_g", "ln1_b", "ffn_conv_b", "ln2_g", "ln2_b")
_REST_A = ("w_proj_a", "w_proj_b", "w_out")
_REST_B = ("ffn_w_gate", "ffn_w_up", "ffn_w_down")


def _local_step(x, tgt, adam, p, shards=None):
    w = _prepare_first(p)
    side_a = side_b = None
    if shards is not None:
        side_a = _gather_exchange([shards[n] for n in _REST_A])
        side_b = _gather_exchange([shards[n] for n in _REST_B])
    h1, h1t = _modulate(x, adam)
    cat = w["in_cat"]
    pqkv, got_a = _mm_sided(h1, cat[:, 0:4096], side_a, name="proj_qkv")
    px, got_b = _mm_sided(h1, cat[:, 8192:10752], side_b, name="proj_x")
    if shards is not None:
        p = dict(p, **{n: _unstack_shards(n, a_) for n, a_ in zip(_REST_A + _REST_B, got_a + got_b)})
    w.update(_prepare_rest(p))
    pz = _mm(h1, cat[:, 4096:6144], name="proj_z")
    pg = _mm(h1, cat[:, 6144:8192], name="proj_g")
    pab = _mm(h1, cat[:, 10752:10880], name="proj_ab")
    hrec, xcs, rec, rec_t = _rg_fwd(px, w["rg_cw"], w["rg_vec"], w["wa"], w["wx"])
    q, k, v, rt, cpre = _dn_prep(pqkv, pab, w["dn_cw"], w["arow"], w["drow"])
    u, ww, qd, kd, pm, tinv, gam = _dn_intra(q, k, v, rt)
    o, vn, ssave, dn, dn_t = _dn_seq(u, ww, qd, kd, pm, gam, pz, w["nw"])
    ya = _mm(rec, w["pa"], name="proj_a")
    yb = _mm(dn, w["pb"], name="proj_b")
    mix, merged_t = _merge_proj_out(pg, ya, yb, w["out"])
    r1, x2, h2, h2_t = _ln1(x, mix, adam, w["ln1_g"], w["ln1_b"])
    gu = _mm(h2, w["gu"], name="ffn_gu")
    act, act_t, gcf = _ffn_act(gu, w["ffn_cw"], w["ffn_cb"])
    dff, dx2a, red2 = _ffn_down_ln2_loss(act, w["down"], x2, tgt, adam, w["ln2_g"], w["ln2_b"])
    g = {}
    dact = _mm(dff, w["down_t"], name="d_act")
    g["ffn_w_down"] = _mm(act_t, dff, name="g_down")
    dgu, gcw_f, gcb_f = _ffn_bwd(dact, gu, gcf, w["ffn_cw"])
    ggu = _mm(h2_t, dgu, name="g_gu")
    g["ffn_w_gate"], g["ffn_w_up"] = ggu[:, :D_FF], ggu[:, D_FF:]
    g["ffn_conv_w"], g["ffn_conv_b"] = gcw_f[0:3], gcb_f[0]
    dmix, dxa, red1 = _d_h2_ln1_bwd(dgu, w["gu_t"], dx2a, x2, r1, mix, adam, w["ln1_g"])
    g["w_out"] = _mm(merged_t, dmix, name="g_out")
    dya, dyb, dpg = _d_merged_bwd(dmix, w["out_t"], pg, ya, yb)
    drec = _mm(dya, w["pa_t"], name="d_rec")
    g["w_proj_a"] = _mm(rec_t, dya, name="g_pa")
    g["w_proj_b"] = _mm(dn_t, dyb, name="g_pb")
    do, dpz, gnw = _d_dn_post_bwd(dyb, w["pb_t"], o, pz, w["nw"])
    dvn, dkd, dgam = _dn_seq_bwd(do, qd, kd, pm, ww, vn, ssave, gam)
    dq, dk, dv, drt = _dn_intra_bwd(q, k, v, rt, do, dvn, dkd, gam, dgam, ssave, tinv, u, ww, vn)
    dpqkv, dpab, gcw_d, gsc = _dn_prep_bwd(dq, dk, dv, drt, pqkv, cpre, pab, w["dn_cw"], w["acol"], w["dcol"])
    side_r = None
    if shards is not None:
        side_r = _scatter_exchange([_stack_shards(n, g[n]).astype(MXU_DT) for n in _REST_A + _REST_B])
    dpx, gwa, gwx, gcw_r, gvec, *got_r = _rg_bwd(drec, px, xcs, hrec, w["rg_cw"], w["rg_vec"], w["wa"], w["wx"],
                                                 w["wa_t"], w["wx_t"], side_r)
    dproj = jnp.concatenate([dpqkv, dpz, dpg, dpx, dpab], axis=1)
    reorder = lambda gc: jnp.concatenate([gc[:, 8192:10752], gc[:, 0:4096], gc[:, 4096:6144], gc[:, 10752:10784],
                                          gc[:, 6144:8192]], axis=1)
    wire = lambda gh: _scatter_exchange([_stack_shards("w_in", gh).astype(MXU_DT)]) if shards is not None else None
    g["rg_conv_w"], g["rg_conv_b"] = gcw_r[0:4], gvec[0]
    g["rg_w_a"], g["rg_w_x"] = _diag_blocks(gwa), _diag_blocks(gwx)
    g["rg_b_a"], g["rg_b_x"], g["rg_lambda"] = gvec[1], gvec[2], gvec[3]
    g["dn_conv_w"] = gcw_d[0:4]
    g["dn_a_log"], g["dn_dt_bias"], g["dn_norm_w"] = gsc[:, 0], gsc[:, 1], gnw[0]
    g["ln1_g"], g["ln1_b"] = red1[2], red1[3]
    g["ln2_g"], g["ln2_b"] = red2[0], red2[1]
    side_s = None
    if shards is not None:
        side_s = _allgather_exchange(_pack([jnp.full((1,), red2[3, 0], F32)] + [g[n] for n in _SMALL], 128, 64))
    g_top, got_s = _mm_sided(h1t[:D // 2], dproj, side_s, name="g_in_top")
    g_top = reorder(g_top)
    g_bot, got_top = _mm_sided(h1t[D // 2:], dproj, wire(g_top), name="g_in_bot")
    g_bot = reorder(g_bot)
    gx, red0, *got_bot = _d_h1_modulate_bwd(dproj, w["in_cat_t"], dxa, x, adam, wire(g_bot))
    g["w_in"] = jnp.concatenate([g_top, g_bot], axis=0)
    got = dict(zip(_REST_A + _REST_B, got_r))
    if shards is not None:
        got["w_in"] = jnp.concatenate([got_top[0], got_bot[0]], axis=1)
        got["small"] = got_s[0]
    d_ada = jnp.concatenate([red0[1], red0[0], red1[4], red1[1], red1[0], red2[2]])
    return red2[3, 0], gx, g, d_ada, got


_BIG = ("w_in", "w_proj_a", "w_proj_b", "w_out", "ffn_w_gate", "ffn_w_up", "ffn_w_down")
_COL_SHARDED = ("w_in", "ffn_w_gate", "ffn_w_up")
_CONV = ("rg_conv_w", "dn_conv_w", "ffn_conv_w")
_REPL = ("b_ada", "rg_conv_b", "rg_w_a", "rg_b_a", "rg_w_x", "rg_b_x", "rg_lambda", "dn_a_log",
         "dn_dt_bias", "dn_norm_w", "ln1_g", "ln1_b", "ffn_conv_b", "ln2_g", "ln2_b")
_NAMES = ("w_ada", "b_ada", "w_in", "rg_conv_w", "rg_conv_b", "rg_w_a", "rg_b_a", "rg_w_x", "rg_b_x",
          "rg_lambda", "dn_conv_w", "dn_a_log", "dn_dt_bias", "dn_norm_w", "w_proj_a", "w_proj_b", "w_out",
          "ln1_g", "ln1_b", "ffn_w_gate", "ffn_w_up", "ffn_conv_w", "ffn_conv_b", "ffn_w_down", "ln2_g", "ln2_b")


def _pack(arrs, width, row_mult):
    pieces = []
    for a in arrs:
        f = a.reshape(-1)
        pieces.append(jnp.pad(f, (0, (-f.shape[0]) % (8 * width))).reshape(-1, width))
    rows = sum(p_.shape[0] for p_ in pieces)
    if rows % row_mult:
        pieces.append(jnp.zeros((row_mult - rows % row_mult, width), pieces[0].dtype))
    return jnp.concatenate(pieces, axis=0)


def _unpack(flat, shapes, width):
    out, row = [], 0
    for shp in shapes:
        n = 1
        for d_ in shp:
            n *= d_
        rows = -(-n // (8 * width)) * 8
        out.append(flat[row:row + rows].reshape(-1)[:n].reshape(shp))
        row += rows
    return out


def _stack_shards(name, full):
    if name in _COL_SHARDED or name in _CONV:
        r, ncol = full.shape
        return full.reshape(r, 4, ncol // 4).transpose(1, 0, 2)
    return full.reshape((4, full.shape[0] // 4) + full.shape[1:])


def _unstack_shards(name, st):
    if name in _COL_SHARDED or name in _CONV:
        return st.transpose(1, 0, 2).reshape(st.shape[1], 4 * st.shape[2])
    return st.reshape((4 * st.shape[1],) + st.shape[2:])


def kernel(x, c, w_ada, b_ada, w_in, rg_conv_w, rg_conv_b, rg_w_a, rg_b_a, rg_w_x, rg_b_x, rg_lambda, dn_conv_w, dn_a_log, dn_dt_bias, dn_norm_w, w_proj_a, w_proj_b, w_out, ln1_g, ln1_b, ffn_w_gate, ffn_w_up, ffn_conv_w, ffn_conv_b, ffn_w_down, ln2_g, ln2_b, loss_target, m_w_ada, m_b_ada, m_w_in, m_rg_conv_w, m_rg_conv_b, m_rg_w_a, m_rg_b_a, m_rg_w_x, m_rg_b_x, m_rg_lambda, m_dn_conv_w, m_dn_a_log, m_dn_dt_bias, m_dn_norm_w, m_w_proj_a, m_w_proj_b, m_w_out, m_ln1_g, m_ln1_b, m_ffn_w_gate, m_ffn_w_up, m_ffn_conv_w, m_ffn_conv_b, m_ffn_w_down, m_ln2_g, m_ln2_b, v_w_ada, v_b_ada, v_w_in, v_rg_conv_w, v_rg_conv_b, v_rg_w_a, v_rg_b_a, v_rg_w_x, v_rg_b_x, v_rg_lambda, v_dn_conv_w, v_dn_a_log, v_dn_dt_bias, v_dn_norm_w, v_w_proj_a, v_w_proj_b, v_w_out, v_ln1_g, v_ln1_b, v_ffn_w_gate, v_ffn_w_up, v_ffn_conv_w, v_ffn_conv_b, v_ffn_w_down, v_ln2_g, v_ln2_b):
    args = locals()
    wts = {n: args[n][0] for n in _NAMES}
    mom = {n: args["m_" + n][0] for n in _NAMES}
    var = {n: args["v_" + n][0] for n in _NAMES}
    xs, tgt = x[0], loss_target[0]
    ix, iy, ic = _coords()
    shard = 2 * ix + iy
    batch = 4 * ix + 2 * iy + ic

    c_all = _allgather8(_pad_rows(c, 8), "gather_c")[:, 0, :]
    sc16 = _pad_rows(_silu_rows(c_all), 16)
    ada_cols = _mm(sc16, w_ada[0], name="ada")[:8]
    ada_g = _allgather8(ada_cols, "gather_ada")
    ada_all = jnp.concatenate([ada_g[0], ada_g[2], ada_g[4], ada_g[6]], axis=1) + b_ada
    adam = _pad_rows(lax.dynamic_index_in_dim(ada_all, batch, 0, keepdims=False).reshape(6, D), 8)

    (w_in_all,) = _run_exchange(_gather_exchange([wts["w_in"].astype(MXU_DT)]), "gather_w_in")
    full = {"w_in": _unstack_shards("w_in", w_in_all)}
    conv_shard = _pack([wts[n] for n in _CONV], 128, 8)
    conv_all = _allgather8(conv_shard, "gather_conv")
    shapes_conv = [wts[n].shape for n in _CONV]
    per_shard = [_unpack(conv_all[2 * s], shapes_conv, 128) for s in range(4)]
    for i, n in enumerate(_CONV):
        full[n] = _unstack_shards(n, jnp.stack([per_shard[s][i] for s in range(4)]))
    for n in _REPL:
        full[n] = wts[n]

    shards = {n: wts[n].astype(MXU_DT) for n in _REST_A + _REST_B}
    loss_b, gx, g, d_ada, recv = _local_step(xs, tgt, adam, full, shards)

    parts = []
    for n in _BIG:
        r_ = recv[n]
        axis = 1 if n in _COL_SHARDED else 0
        width = wts[n].shape[axis]
        own = lax.dynamic_slice_in_dim(g[n], shard * width, width, axis=axis)
        parts.append(_sum_partials(own, r_, "sum_" + n))
    parts_sib = _swap_sibling(parts)
    out = {n: [] for n in _NAMES}
    for n, p_, q_ in zip(_BIG, parts, parts_sib):
        out[n] = list(_adamw([p_, q_], wts[n], mom[n], var[n], "adamw_" + n))

    tot = _unpack(_sum8(recv["small"], "sum_small"), [(1,)] + [full[n].shape for n in _SMALL], 128)
    gsum = dict(zip(_SMALL, tot[1:]))
    loss = tot[0][0]
    for n in _CONV:
        gsum[n] = lax.dynamic_index_in_dim(_stack_shards(n, gsum[n]), shard, 0, keepdims=False)
    d_ada_g = _allgather8(d_ada.reshape(6 * D // 128, 128), "gather_d_ada")
    gsum["b_ada"] = _sum8(d_ada_g, "sum_d_ada").reshape(6 * D)
    d_ada_all = d_ada_g.reshape(8, 6 * D)
    cols = lax.dynamic_slice_in_dim(d_ada_all, shard * (6 * D // 4), 6 * D // 4, axis=1)
    g_wada = _mm(sc16, _pad_rows(cols, 16), name="g_ada", trans_a=True)
    res = _adamw([g_wada], wts["w_ada"], mom["w_ada"], var["w_ada"], "adamw_ada")
    out["w_ada"] = list(res)
    names_s = _CONV + _REPL
    shapes_s = [wts[n].shape for n in names_s]
    pk = lambda d_: _pack([d_[n] for n in names_s], 128, 64)
    res_s = _adamw([pk(gsum)], pk(wts), pk(mom), pk(var), "adamw_small")
    for r_ in res_s:
        for n, a in zip(names_s, _unpack(r_, shapes_s, 128)):
            out[n].append(a)

    outs = [loss, gx[None]]
    for i in range(4):
        outs += [out[n][i][None] for n in _NAMES]
    return tuple(outs)
```

```python
import functools

import jax
import jax.numpy as jnp
from jax import lax
from jax.experimental import pallas as pl
from jax.experimental.pallas import tpu as pltpu

F32 = jnp.float32
BF16 = jnp.bfloat16
MXU_DT = BF16

D = 1024
D_RNN = 1280
RG_BLOCKS = 16
RG_BW = 80
RG_C = 8.0
NQ = 8
NV = 16
HD = 128
CH = 64
D_FF = 2816
LN_EPS = 1e-5
RMS_EPS = 1e-6
L2_EPS = 1e-6
ALPHA = 2.0 ** 0.25
Q_SCALE = HD ** -0.5
N_CAT = 10880
VMEM_LIMIT = 56 * 1024 * 1024
MM_VMEM_BUDGET = 36 * 1024 * 1024
SEQ_GROUP = 16
MESH = pl.DeviceIdType.MESH

ADAM_LR, ADAM_B1, ADAM_B2, ADAM_EPS, ADAM_WD, ADAM_STEP = 1e-3, 0.9, 0.999, 1e-8, 0.01, 10


def _sigmoid(x):
    return 0.5 * jnp.tanh(0.5 * x) + 0.5


def _softplus(x):
    return jnp.maximum(x, 0.0) + jnp.log1p(jnp.exp(-jnp.abs(x)))


_GC = 0.7978845608028654


def _gelu(x):
    return 0.5 * x * (1.0 + jnp.tanh(_GC * (x + 0.044715 * x * x * x)))


def _gelu_and_grad(x):
    t = jnp.tanh(_GC * (x + 0.044715 * x * x * x))
    g = 0.5 * x * (1.0 + t)
    dg = 0.5 * (1.0 + t) + 0.5 * x * (1.0 - t * t) * _GC * (1.0 + 3 * 0.044715 * x * x)
    return g, dg


def _neg_expm1(y):
    series = -y * (1.0 + 0.5 * y * (1.0 + y * (1.0 / 3.0)))
    return jnp.where(y > -0.01, series, 1.0 - jnp.exp(y))


def _dot(a, b):
    return jnp.dot(a.astype(MXU_DT), b.astype(MXU_DT), preferred_element_type=F32)


def _dot_nt(a, b):
    return lax.dot_general(a.astype(MXU_DT), b.astype(MXU_DT), (((1,), (1,)), ((), ())),
                           preferred_element_type=F32)


def _dot_tn(a, b):
    return lax.dot_general(a.astype(MXU_DT), b.astype(MXU_DT), (((0,), (0,)), ((), ())),
                           preferred_element_type=F32)


def _split(a):
    hi = a.astype(BF16)
    return hi, (a - hi.astype(F32)).astype(BF16)


def _dot3(a, b, dims=(((1,), (0,)), ((), ()))):
    ah, al = _split(a)
    bh, bl = _split(b)
    d = lambda p, q: lax.dot_general(p, q, dims, preferred_element_type=F32)
    return d(ah, bh) + (d(al, bh) + d(ah, bl))


def _dot3_tn(a, b):
    return _dot3(a, b, (((0,), (0,)), ((), ())))


def _iota(shape, dim):
    return lax.broadcasted_iota(jnp.int32, shape, dim)


def _shift_down(x, before, j):
    if j == 0:
        return x
    xr = pltpu.roll(x, j, 0)
    br = pltpu.roll(before, j, 0)
    top = jnp.where(_iota(br.shape, 0) < j, br, xr[:8])
    return jnp.concatenate([top, xr[8:]], axis=0)


def _shift_up(x, after, j):
    if j == 0:
        return x
    t = x.shape[0]
    xr = pltpu.roll(x, t - j, 0)
    ar = pltpu.roll(after, 8 - j, 0)
    bot = jnp.where(_iota(ar.shape, 0) >= 8 - j, ar, xr[t - 8:])
    return jnp.concatenate([xr[:t - 8], bot], axis=0)


def _taps(x, before, k):
    return [_shift_down(x, before, k - 1 - i) for i in range(k)]


def _conv_taps(taps, w_ref):
    y = w_ref[0:1, :] * taps[0]
    for i in range(1, len(taps)):
        y = y + w_ref[i:i + 1, :] * taps[i]
    return y


def _conv_causal(x, before, w_ref, k):
    return _conv_taps(_taps(x, before, k), w_ref)


def _conv_causal_bwd(dy, after, w_ref, k, x=None, gw_ref=None):
    dx = None
    for i in range(k):
        sh = _shift_up(dy, after, k - 1 - i)
        term = w_ref[i:i + 1, :] * sh
        dx = term if dx is None else dx + term
        if x is not None:
            gw_ref[i:i + 1, :] += _rsum(x * sh)
    return dx


def _scan_fwd(a, u, carry):
    t = a.shape[0]
    pos = _iota(a.shape, 0) & 7
    for d in (1, 2, 4):
        m = pos >= d
        u = u + jnp.where(m, a * pltpu.roll(u, d, 0), 0.0)
        a = jnp.where(m, a * pltpu.roll(a, d, 0), a)
    out = []
    for g in range(t // 8):
        hg = u[8 * g:8 * g + 8] + a[8 * g:8 * g + 8] * carry
        out.append(hg)
        carry = hg[7:8]
    return jnp.concatenate(out, axis=0)


def _scan_rev(a, u, carry):
    t = a.shape[0]
    pos = _iota(a.shape, 0) & 7
    for d in (1, 2, 4):
        m = pos < 8 - d
        u = u + jnp.where(m, a * pltpu.roll(u, t - d, 0), 0.0)
        a = jnp.where(m, a * pltpu.roll(a, t - d, 0), a)
    out = [None] * (t // 8)
    for g in reversed(range(t // 8)):
        lg = u[8 * g:8 * g + 8] + a[8 * g:8 * g + 8] * carry
        out[g] = lg
        carry = lg[0:1]
    return jnp.concatenate(out, axis=0)


def _chunk_cumsum(g, axis, rev=False):
    n = g.shape[axis]
    pos = _iota(g.shape, axis) & (CH - 1)
    d = 1
    while d < CH:
        if rev:
            g = g + jnp.where(pos < CH - d, pltpu.roll(g, n - d, axis), 0.0)
        else:
            g = g + jnp.where(pos >= d, pltpu.roll(g, d, axis), 0.0)
        d *= 2
    return g


def _ln_stats(r):
    mu = jnp.mean(r, axis=-1, keepdims=True)
    xc = r - mu
    var = jnp.mean(xc * xc, axis=-1, keepdims=True)
    rstd = lax.rsqrt(var + LN_EPS)
    return xc * rstd, rstd


def _ln_bwd(dy, xhat, rstd, g):
    dxh = dy * g
    return rstd * (dxh - jnp.mean(dxh, axis=-1, keepdims=True)
                   - xhat * jnp.mean(dxh * xhat, axis=-1, keepdims=True))


def _rsum(x):
    return jnp.sum(x, axis=0, keepdims=True)


def _params(sem):
    return pltpu.CompilerParams(dimension_semantics=sem, vmem_limit_bytes=VMEM_LIMIT)


def _pick(n, cands):
    for c in cands:
        if n % c == 0:
            return c
    return n


def _rows(tm, w, col=0, nt=None):
    if nt is None:
        return pl.BlockSpec((tm, w), lambda i: (i, col))
    return pl.BlockSpec((tm, w), lambda i: (nt - 1 - i, col))


def _before(tm, w, col=0, nt=None):
    r = tm // 8
    if nt is None:
        return pl.BlockSpec((8, w), lambda i: (jnp.maximum(i * r - 1, 0), col))
    return pl.BlockSpec((8, w), lambda i: (jnp.maximum((nt - 1 - i) * r - 1, 0), col))


def _cols(tm, w):
    return pl.BlockSpec((w, tm), lambda i: (0, i))


def _whole(shape):
    return pl.BlockSpec(shape, lambda *_: (0,) * len(shape))


def _mm_plan(a, b, out_dtype):
    m, kk = a.shape
    _, n = b.shape
    tm = _pick(m, (512, 256, 128))
    tk = kk if kk <= 5632 else _pick(kk, (2176, 2048, 1024))
    nk = kk // tk

    def vmem_bytes(tn):
        blocks = tm * tk * a.dtype.itemsize + tk * tn * b.dtype.itemsize + tm * tn * jnp.dtype(out_dtype).itemsize
        return 2 * blocks + (tm * tn * 4 if nk > 1 else 0)

    cands = [t for t in (1408, 1280, 1024, 640, 512, 256, 128) if n % t == 0] or [n]
    tn = next((t for t in cands if vmem_bytes(t) <= MM_VMEM_BUDGET), cands[-1])
    return tm, tn, tk, nk


def _mm_side(a, b, name, out_dtype, side):
    m, _ = a.shape
    _, n = b.shape
    tm, tn, tk, nk = _mm_plan(a, b, out_dtype)
    ni, nj = m // tm, n // tn
    n_in, n_out = len(side.ins), len(side.out_shapes)

    def body(*refs):
        a_ref, b_ref = refs[0], refs[1]
        s_in = refs[2:2 + n_in]
        o_ref = refs[2 + n_in]
        s_out = refs[3 + n_in:3 + n_in + n_out]
        acc = refs[3 + n_in + n_out]
        sems = refs[4 + n_in + n_out:]
        i, j, k = pl.program_id(0), pl.program_id(1), pl.program_id(2)

        @pl.when((i == 0) & (j == 0) & (k == 0))
        def _():
            side.start(s_in, s_out, sems)

        @pl.when(k == 0)
        def _():
            acc[...] = jnp.zeros_like(acc)
        acc[...] += _dot(a_ref[...], b_ref[...])

        @pl.when(k == nk - 1)
        def _():
            o_ref[...] = acc[...].astype(out_dtype)

        @pl.when((i == ni - 1) & (j == nj - 1) & (k == nk - 1))
        def _():
            side.finish(s_in, s_out, sems)

    return pl.pallas_call(
        body, name=name, grid=(ni, nj, nk),
        in_specs=[pl.BlockSpec((tm, tk), lambda i, j, k: (i, k)),
                  pl.BlockSpec((tk, tn), lambda i, j, k: (k, j))] + [_HBM] * n_in,
        out_specs=[pl.BlockSpec((tm, tn), lambda i, j, k: (i, j))] + [_HBM] * n_out,
        out_shape=[jax.ShapeDtypeStruct((m, n), out_dtype)] + list(side.out_shapes),
        scratch_shapes=[pltpu.VMEM((tm, tn), F32)] + list(side.sems),
        compiler_params=_params(("arbitrary", "arbitrary", "arbitrary")),
    )(a, b, *side.ins)


def _mm(a, b, *, name, trans_a=False, out_dtype=F32, side=None):
    if trans_a:
        return _mm(a.T, b, name=name, out_dtype=out_dtype, side=side)
    if side is not None:
        return _mm_side(a, b, name, out_dtype, side)
    m, kk = a.shape
    _, n = b.shape
    tm, tn, tk, nk = _mm_plan(a, b, out_dtype)

    if nk == 1:
        def body(a_ref, b_ref, o_ref):
            o_ref[...] = _dot(a_ref[...], b_ref[...]).astype(out_dtype)
        scratch = []
    else:
        def body(a_ref, b_ref, o_ref, acc):
            k = pl.program_id(2)

            @pl.when(k == 0)
            def _():
                acc[...] = jnp.zeros_like(acc)
            acc[...] += _dot(a_ref[...], b_ref[...])

            @pl.when(k == nk - 1)
            def _():
                o_ref[...] = acc[...].astype(out_dtype)
        scratch = [pltpu.VMEM((tm, tn), F32)]

    return pl.pallas_call(
        body, name=name, grid=(m // tm, n // tn, nk),
        in_specs=[pl.BlockSpec((tm, tk), lambda i, j, k: (i, k)),
                  pl.BlockSpec((tk, tn), lambda i, j, k: (k, j))],
        out_specs=pl.BlockSpec((tm, tn), lambda i, j, k: (i, j)),
        out_shape=jax.ShapeDtypeStruct((m, n), out_dtype),
        scratch_shapes=scratch,
        compiler_params=_params(("parallel", "parallel", "arbitrary")),
    )(a, b)


def _modulate(x, adam):
    s = x.shape[0]
    tm = _pick(s, (512, 256, 128))

    def body(x_ref, ada_ref, o_ref, ot_ref):
        h = x_ref[...] * (1.0 + ada_ref[1:2, :]) + ada_ref[0:1, :]
        o_ref[...] = h.astype(MXU_DT)
        ot_ref[...] = h.T.astype(MXU_DT)

    return pl.pallas_call(
        body, name="modulate1", grid=(s // tm,),
        in_specs=[_rows(tm, D), _whole((8, D))], out_specs=[_rows(tm, D), _cols(tm, D)],
        out_shape=[jax.ShapeDtypeStruct((s, D), MXU_DT), jax.ShapeDtypeStruct((D, s), MXU_DT)],
        compiler_params=_params(("parallel",)),
    )(x, adam)


def _rg_gates(xc, wa_ref, wx_ref, vec_ref):
    xb = xc.astype(MXU_DT)
    r = _sigmoid(jnp.dot(xb, wa_ref[...], preferred_element_type=F32) + vec_ref[1:2, :])
    ig = _sigmoid(jnp.dot(xb, wx_ref[...], preferred_element_type=F32) + vec_ref[2:3, :])
    sp = _softplus(-vec_ref[3:4, :])
    la = -RG_C * r * sp
    a = jnp.exp(la)
    n1 = _neg_expm1(2.0 * la)
    rmult = lax.rsqrt(jnp.maximum(n1, 1e-20))
    return r, ig, a, n1 * rmult, sp, rmult


def _rg_fwd(px, cw, vec, wa, wx):
    s = px.shape[0]
    tm = _pick(s, (256, 128))
    w = D_RNN

    def body(xr_ref, gr_ref, cw_ref, vec_ref, wa_ref, wx_ref, h_ref, xc_ref, rec_ref, rect_ref, prev_x, prev_h):
        @pl.when(pl.program_id(0) == 0)
        def _():
            prev_x[...] = jnp.zeros_like(prev_x)
            prev_h[...] = jnp.zeros_like(prev_h)
        x = xr_ref[...]
        xc = _conv_causal(x, prev_x[...], cw_ref, 4) + vec_ref[0:1, :]
        prev_x[...] = x[tm - 8:, :]
        xc_ref[...] = xc
        _, ig, a, mult, _, _ = _rg_gates(xc, wa_ref, wx_ref, vec_ref)
        h = _scan_fwd(a, mult * ig * xc, prev_h[7:8, :])
        prev_h[...] = h[tm - 8:, :]
        h_ref[...] = h
        rec = h * _gelu(gr_ref[...])
        rec_ref[...] = rec.astype(MXU_DT)
        rect_ref[...] = rec.T.astype(MXU_DT)

    return pl.pallas_call(
        body, name="rg_fwd", grid=(s // tm,),
        in_specs=[_rows(tm, w, 0), _rows(tm, w, 1), _whole((8, w)), _whole((8, w)),
                  _whole((w, w)), _whole((w, w))],
        out_specs=[_rows(tm, w), _rows(tm, w), _rows(tm, w), _cols(tm, w)],
        out_shape=[jax.ShapeDtypeStruct((s, w), F32), jax.ShapeDtypeStruct((s, w), F32),
                   jax.ShapeDtypeStruct((s, w), MXU_DT), jax.ShapeDtypeStruct((w, s), MXU_DT)],
        scratch_shapes=[pltpu.VMEM((8, w), F32), pltpu.VMEM((8, w), F32)],
        compiler_params=_params(("arbitrary",)),
    )(px, px, cw, vec, wa, wx)


def _dn_scalars(ab, arow, drow):
    lane = _iota(ab.shape, 1)
    g = jnp.where(lane < NV, -jnp.exp(arow) * _softplus(ab + drow), 0.0)
    beta = _sigmoid(ab)
    return lane, g, beta


def _l2n_heads(c, out_ref, off, scale):
    for hh in range(NQ):
        x = c[:, off + hh * HD: off + (hh + 1) * HD]
        r = lax.rsqrt(jnp.sum(x * x, axis=-1, keepdims=True) + L2_EPS)
        out_ref[:, hh * HD:(hh + 1) * HD] = x * (r * scale)


def _dn_prep(pqkv, pab, cw, arow, drow):
    s = pqkv.shape[0]
    tm = 128
    wq = NQ * HD

    def body(x_ref, ab_ref, cw_ref, a_ref, d_ref, q_ref, k_ref, v_ref, rt_ref, cp_ref, prev_x):
        @pl.when(pl.program_id(0) == 0)
        def _():
            prev_x[...] = jnp.zeros_like(prev_x)
        x = x_ref[...]
        cp = _conv_causal(x, prev_x[...], cw_ref, 4)
        prev_x[...] = x[tm - 8:, :]
        cp_ref[...] = cp
        c = cp * _sigmoid(cp)
        _l2n_heads(c, q_ref, 0, Q_SCALE)
        _l2n_heads(c, k_ref, wq, 1.0)
        v_ref[...] = c[:, 2 * wq:]
        lane, g, beta = _dn_scalars(ab_ref[...], a_ref[...], d_ref[...])
        gc = _chunk_cumsum(g, 0)
        gl = gc + _chunk_cumsum(g, 0, rev=True) - g
        pack = jnp.where(lane < NV, gc, jnp.where(lane < 2 * NV, beta,
                         jnp.where(lane < 3 * NV, pltpu.roll(gl, 2 * NV, 1), 0.0)))
        rt_ref[...] = pack.T[0:3 * NV, :]

    return pl.pallas_call(
        body, name="dn_prep", grid=(s // tm,),
        in_specs=[_rows(tm, 4 * wq), _rows(tm, 128), _whole((8, 4 * wq)), _whole((1, 128)), _whole((1, 128))],
        out_specs=[_rows(tm, wq), _rows(tm, wq), _rows(tm, 2 * wq),
                   pl.BlockSpec((3 * NV, tm), lambda i: (0, i)), _rows(tm, 4 * wq)],
        out_shape=[jax.ShapeDtypeStruct((s, wq), F32), jax.ShapeDtypeStruct((s, wq), F32),
                   jax.ShapeDtypeStruct((s, 2 * wq), F32), jax.ShapeDtypeStruct((3 * NV, s), F32),
                   jax.ShapeDtypeStruct((s, 4 * wq), F32)],
        scratch_shapes=[pltpu.VMEM((8, 4 * wq), F32)],
        compiler_params=_params(("arbitrary",)),
    )(pqkv, pab, cw, arow, drow)


def _pair_masks():
    i = _iota((2 * CH, 2 * CH), 0)
    j = _iota((2 * CH, 2 * CH), 1)
    same = (i >> 6) == (j >> 6)
    return same & (i >= j), same & (i > j)


def _head_cols(rt_ref, h):
    shp = (2 * CH, 2 * CH)
    g_r = jnp.broadcast_to(rt_ref[pl.ds(h, 1), :], shp)
    b_r = jnp.broadcast_to(rt_ref[pl.ds(NV + h, 1), :], shp)
    l_r = jnp.broadcast_to(rt_ref[pl.ds(2 * NV + h, 1), :], shp)
    return g_r, g_r.T, b_r.T, l_r, l_r.T


def _inv_unit_lower_many(a_list):
    n = a_list[0].shape[0]
    eye = (_iota((n, n), 0) == _iota((n, n), 1)).astype(F32)
    bs = [-a for a in a_list]
    xs = [eye + b for b in bs]
    for _ in range(5):
        bs = [_dot(b, b) for b in bs]
        xs = [x + _dot(x, b) for x, b in zip(xs, bs)]
    rs = [(eye - x) - _dot3(a, x) for a, x in zip(a_list, xs)]
    return [x + _dot(x, r) for x, r in zip(xs, rs)]


def _gam_rows(l_r):
    lrow = l_r[0:1, :]
    lane = _iota(lrow.shape, 1)
    other = pltpu.roll(lrow, CH, 1)
    return jnp.exp(jnp.where(lane < CH, lrow, other)), jnp.exp(jnp.where(lane >= CH, lrow, other))


def _dn_intra(q, k, v, rt):
    s = q.shape[0]
    nb = s // (2 * CH)
    qps = 8
    blk = pl.BlockSpec((2 * CH, qps * HD), lambda i, h: (i, h))
    blk2 = pl.BlockSpec((2 * CH, 2 * qps * HD), lambda i, h: (i, h))

    def body(q_ref, k_ref, v_ref, rt_ref, u_ref, w_ref, qd_ref, kd_ref, p_ref, ti_ref, gam_ref):
        hstep = pl.program_id(1)
        mc, ms = _pair_masks()
        gam_ref[...] = jnp.zeros_like(gam_ref)
        heads = []
        for qh in range(qps):
            qq = q_ref[:, qh * HD:(qh + 1) * HD]
            kk_ = k_ref[:, qh * HD:(qh + 1) * HD]
            kk = _dot_nt(kk_, kk_)
            qk = _dot_nt(qq, kk_)
            for j in range(2):
                idx = 2 * qh + j
                cs = slice(idx * HD, (idx + 1) * HD)
                g_r, g_c, b_c, l_r, l_c = _head_cols(rt_ref, 2 * qps * hstep + idx)
                dec = jnp.where(mc, jnp.exp(jnp.where(mc, g_c - g_r, 0.0)), 0.0)
                eg = jnp.exp(g_c)
                p_ref[:, cs] = jnp.where(mc, qk * dec, 0.0)
                qd_ref[:, cs] = eg * qq
                kd_ref[:, cs] = jnp.exp(l_c - g_c) * kk_
                ga, gb = _gam_rows(l_r)
                gam_ref[0, qh, 2 * j:2 * j + 1, :] = ga
                gam_ref[0, qh, 2 * j + 1:2 * j + 2, :] = gb
                rhs = jnp.concatenate([b_c * v_ref[:, cs], b_c * eg * kk_], axis=1)
                heads.append((cs, jnp.where(ms, b_c * kk * dec, 0.0), rhs))
        tinvs = _inv_unit_lower_many([a for _, a, _ in heads])
        uws = [_dot3(t, rhs) for t, (_, _, rhs) in zip(tinvs, heads)]
        for t, uw, (cs, _, _) in zip(tinvs, uws, heads):
            ti_ref[:, cs] = t
            u_ref[:, cs] = uw[:, :HD]
            w_ref[:, cs] = uw[:, HD:]

    big = jax.ShapeDtypeStruct((s, NV * HD), F32)
    return pl.pallas_call(
        body, name="dn_intra", grid=(nb, NQ // qps),
        in_specs=[blk, blk, blk2, pl.BlockSpec((3 * NV, 2 * CH), lambda i, h: (0, i))],
        out_specs=[blk2] * 6 + [pl.BlockSpec((1, qps, 8, 128), lambda i, h: (i, h, 0, 0))],
        out_shape=[big] * 6 + [jax.ShapeDtypeStruct((nb, NQ, 8, 128), F32)],
        compiler_params=_params(("parallel", "parallel")),
    )(q, k, v, rt)


def _dn_seq(u, w, qd, kd, p, gam, pz, nw):
    s = u.shape[0]
    nb = s // (2 * CH)
    wide = pl.BlockSpec((2 * CH, NV * HD), lambda i: (i, 0))

    def body(u_ref, w_ref, qd_ref, kd_ref, p_ref, gam_ref, z_ref, nw_ref, o_ref, vn_ref, ss_ref, y_ref, yt_ref, st):
        @pl.when(pl.program_id(0) == 0)
        def _():
            st[...] = jnp.zeros_like(st)
        ra, rb = slice(0, CH), slice(CH, 2 * CH)
        for g0 in range(0, NV, SEQ_GROUP):
            hs = list(range(g0, g0 + SEQ_GROUP))
            cs = [slice(h * HD, (h + 1) * HD) for h in hs]
            ga = [gam_ref[0, h // 2, 2 * (h % 2):2 * (h % 2) + 1, :] for h in hs]
            gb = [gam_ref[0, h // 2, 2 * (h % 2) + 1:2 * (h % 2) + 2, :] for h in hs]
            s0 = [st[h] for h in hs]
            vna = [u_ref[ra, c] - _dot(w_ref[ra, c], s) for c, s in zip(cs, s0)]
            s1 = [g * s + _dot_tn(kd_ref[ra, c], v) for g, s, c, v in zip(ga, s0, cs, vna)]
            vnb = [u_ref[rb, c] - _dot(w_ref[rb, c], s) for c, s in zip(cs, s1)]
            s2 = [g * s + _dot_tn(kd_ref[rb, c], v) for g, s, c, v in zip(gb, s1, cs, vnb)]
            for h, s in zip(hs, s2):
                st[h] = s
            oa = [_dot(qd_ref[ra, c], s) for c, s in zip(cs, s0)]
            ob = [_dot(qd_ref[rb, c], s) for c, s in zip(cs, s1)]
            for i_, h in enumerate(hs):
                vn = jnp.concatenate([vna[i_], vnb[i_]], axis=0)
                oh = jnp.concatenate([oa[i_], ob[i_]], axis=0) + _dot(p_ref[:, cs[i_]], vn)
                o_ref[:, cs[i_]] = oh
                vn_ref[:, cs[i_]] = vn
                ss_ref[h, 0:HD, :] = s0[i_]
                ss_ref[h, HD:2 * HD, :] = s1[i_]
                z = z_ref[:, cs[i_]]
                rs = lax.rsqrt(jnp.mean(oh * oh, axis=-1, keepdims=True) + RMS_EPS)
                y = oh * rs * nw_ref[...] * (z * _sigmoid(z))
                y_ref[:, cs[i_]] = y.astype(MXU_DT)
                yt_ref[cs[i_], :] = y.T.astype(MXU_DT)

    big = jax.ShapeDtypeStruct((s, NV * HD), F32)
    return pl.pallas_call(
        body, name="dn_seq", grid=(nb,),
        in_specs=[wide] * 5 + [pl.BlockSpec((1, NQ, 8, 128), lambda i: (i, 0, 0, 0)), wide, _whole((1, HD))],
        out_specs=[wide, wide, pl.BlockSpec((NV, 2 * HD, HD), lambda i: (0, i, 0)), wide, _cols(2 * CH, NV * HD)],
        out_shape=[big, big, jax.ShapeDtypeStruct((NV, 2 * s, HD), F32),
                   jax.ShapeDtypeStruct((s, NV * HD), MXU_DT), jax.ShapeDtypeStruct((NV * HD, s), MXU_DT)],
        scratch_shapes=[pltpu.VMEM((NV, HD, HD), F32)],
        compiler_params=_params(("arbitrary",)),
    )(u, w, qd, kd, p, gam, pz, nw)


def _ln1(x, mix, adam, lng, lnb):
    s = x.shape[0]
    tm = _pick(s, (512, 256, 128))

    def body(x_ref, m_ref, ada_ref, g_ref, b_ref, r_ref, x2_ref, h2_ref, h2t_ref):
        r = ALPHA * x_ref[...] + (1.0 + ada_ref[2:3, :]) * m_ref[...]
        xhat, _ = _ln_stats(r)
        x2 = xhat * g_ref[...] + b_ref[...]
        r_ref[...] = r
        x2_ref[...] = x2
        h2 = x2 * (1.0 + ada_ref[4:5, :]) + ada_ref[3:4, :]
        h2_ref[...] = h2.astype(MXU_DT)
        h2t_ref[...] = h2.T.astype(MXU_DT)

    return pl.pallas_call(
        body, name="ln1", grid=(s // tm,),
        in_specs=[_rows(tm, D), _rows(tm, D), _whole((8, D)), _whole((1, D)), _whole((1, D))],
        out_specs=[_rows(tm, D)] * 3 + [_cols(tm, D)],
        out_shape=[jax.ShapeDtypeStruct((s, D), F32), jax.ShapeDtypeStruct((s, D), F32),
                   jax.ShapeDtypeStruct((s, D), MXU_DT), jax.ShapeDtypeStruct((D, s), MXU_DT)],
        compiler_params=_params(("parallel",)),
    )(x, mix, adam, lng, lnb)


def _ffn_act(gu, cw, cb):
    s = gu.shape[0]
    tm = _pick(s, (256, 128))
    w = D_FF

    def body(g_ref, u_ref, cw_ref, cb_ref, o_ref, ot_ref, gc_ref, prev):
        @pl.when(pl.program_id(0) == 0)
        def _():
            prev[...] = jnp.zeros_like(prev)
        g = g_ref[...]
        gc = _conv_causal(g, prev[...], cw_ref, 3) + cb_ref[...]
        prev[...] = g[tm - 8:, :]
        gc_ref[...] = gc
        act = _gelu(gc) * u_ref[...]
        o_ref[...] = act.astype(MXU_DT)
        ot_ref[...] = act.T.astype(MXU_DT)

    return pl.pallas_call(
        body, name="ffn_act", grid=(s // tm,),
        in_specs=[_rows(tm, w, 0), _rows(tm, w, 1), _whole((8, w)), _whole((1, w))],
        out_specs=[_rows(tm, w), _cols(tm, w), _rows(tm, w)],
        out_shape=[jax.ShapeDtypeStruct((s, w), MXU_DT), jax.ShapeDtypeStruct((w, s), MXU_DT),
                   jax.ShapeDtypeStruct((s, w), F32)],
        scratch_shapes=[pltpu.VMEM((8, w), F32)],
        compiler_params=_params(("arbitrary",)),
    )(gu, gu, cw, cb)


def _ffn_bwd(dact, gu, gc, cw):
    s = dact.shape[0]
    tm = _pick(s, (256, 128))
    nt = s // tm
    w = D_FF

    def body(da_ref, g_ref, u_ref, gc_ref, cw_ref, o_ref, gcw_ref, gcb_ref, nxt):
        @pl.when(pl.program_id(0) == 0)
        def _():
            nxt[...] = jnp.zeros_like(nxt)
            gcw_ref[...] = jnp.zeros_like(gcw_ref)
            gcb_ref[...] = jnp.zeros_like(gcb_ref)
        gel, dgel = _gelu_and_grad(gc_ref[...])
        da = da_ref[...]
        dgc = da * u_ref[...] * dgel
        o_ref[:, w:] = (da * gel).astype(MXU_DT)
        o_ref[:, :w] = _conv_causal_bwd(dgc, nxt[...], cw_ref, 3, g_ref[...], gcw_ref).astype(MXU_DT)
        nxt[...] = dgc[:8, :]
        gcb_ref[...] += _rsum(dgc)

    return pl.pallas_call(
        body, name="ffn_bwd", grid=(nt,),
        in_specs=[_rows(tm, w, 0, nt), _rows(tm, w, 0, nt), _rows(tm, w, 1, nt), _rows(tm, w, 0, nt),
                  _whole((8, w))],
        out_specs=[_rows(tm, 2 * w, 0, nt), _whole((8, w)), _whole((1, w))],
        out_shape=[jax.ShapeDtypeStruct((s, 2 * w), MXU_DT), jax.ShapeDtypeStruct((8, w), F32),
                   jax.ShapeDtypeStruct((1, w), F32)],
        scratch_shapes=[pltpu.VMEM((8, w), F32)],
        compiler_params=_params(("arbitrary",)),
    )(dact, gu, gu, gc, cw)


def _dn_seq_bwd(do, qd, kd, p, w, vn, ssave, gam):
    s = do.shape[0]
    nb = s // (2 * CH)
    wide = pl.BlockSpec((2 * CH, NV * HD), lambda i: (nb - 1 - i, 0))
    gspec = pl.BlockSpec((1, NQ, 8, 128), lambda i: (nb - 1 - i, 0, 0, 0))

    def body(do_ref, qd_ref, kd_ref, p_ref, w_ref, vn_ref, ss_ref, gam_ref, dvn_ref, dkd_ref, dgam_ref, dst):
        @pl.when(pl.program_id(0) == 0)
        def _():
            dst[...] = jnp.zeros_like(dst)
        dgam_ref[...] = jnp.zeros_like(dgam_ref)
        ra, rb = slice(0, CH), slice(CH, 2 * CH)
        tot = lambda t: jnp.sum(jnp.sum(t, axis=1, keepdims=True), axis=0, keepdims=True)
        for g0 in range(0, NV, SEQ_GROUP):
            hs = list(range(g0, g0 + SEQ_GROUP))
            cs = [slice(h * HD, (h + 1) * HD) for h in hs]
            ga = [gam_ref[0, h // 2, 2 * (h % 2):2 * (h % 2) + 1, :] for h in hs]
            gb = [gam_ref[0, h // 2, 2 * (h % 2) + 1:2 * (h % 2) + 2, :] for h in hs]
            ds2 = [dst[h] for h in hs]
            pdo = [_dot_tn(p_ref[:, c], do_ref[:, c]) for c in cs]
            qdo_b = [_dot_tn(qd_ref[rb, c], do_ref[rb, c]) for c in cs]
            qdo_a = [_dot_tn(qd_ref[ra, c], do_ref[ra, c]) for c in cs]
            dvb = [p_[rb] + _dot(kd_ref[rb, c], d_) for p_, c, d_ in zip(pdo, cs, ds2)]
            ds1 = [g * d_ + q_ - _dot_tn(w_ref[rb, c], v_)
                   for g, d_, q_, c, v_ in zip(gb, ds2, qdo_b, cs, dvb)]
            dva = [p_[ra] + _dot(kd_ref[ra, c], d_) for p_, c, d_ in zip(pdo, cs, ds1)]
            ds0 = [g * d_ + q_ - _dot_tn(w_ref[ra, c], v_)
                   for g, d_, q_, c, v_ in zip(ga, ds1, qdo_a, cs, dva)]
            for h, d_ in zip(hs, ds0):
                dst[h] = d_
            for i_, h in enumerate(hs):
                c = cs[i_]
                row = 2 * (h % 2)
                dkd_ref[rb, c] = _dot_nt(vn_ref[rb, c], ds2[i_])
                dkd_ref[ra, c] = _dot_nt(vn_ref[ra, c], ds1[i_])
                dvn_ref[ra, c] = dva[i_]
                dvn_ref[rb, c] = dvb[i_]
                dgam_ref[0, h // 2, row:row + 1, :] = jnp.broadcast_to(tot(ds1[i_] * ss_ref[h, 0:HD, :]), (1, 128))
                dgam_ref[0, h // 2, row + 1:row + 2, :] = jnp.broadcast_to(
                    tot(ds2[i_] * ss_ref[h, HD:2 * HD, :]), (1, 128))

    big = jax.ShapeDtypeStruct((s, NV * HD), F32)
    return pl.pallas_call(
        body, name="dn_seq_bwd", grid=(nb,),
        in_specs=[wide] * 6 + [pl.BlockSpec((NV, 2 * HD, HD), lambda i: (0, nb - 1 - i, 0)), gspec],
        out_specs=[wide, wide, gspec],
        out_shape=[big, big, jax.ShapeDtypeStruct((nb, NQ, 8, 128), F32)],
        scratch_shapes=[pltpu.VMEM((NV, HD, HD), F32)],
        compiler_params=_params(("arbitrary",)),
    )(do, qd, kd, p, w, vn, ssave, gam)


def _dn_intra_bwd(q, k, v, rt, do, dvn, dkd, gam, dgam, ssave, tinv, u, w, vn):
    s = q.shape[0]
    nb = s // (2 * CH)
    qps = 8
    nh = 2 * qps
    blk = pl.BlockSpec((2 * CH, qps * HD), lambda i, h: (i, h))
    blk2 = pl.BlockSpec((2 * CH, nh * HD), lambda i, h: (i, h))
    gspec = pl.BlockSpec((1, qps, 8, 128), lambda i, h: (i, h, 0, 0))
    rspec = pl.BlockSpec((3 * NV, 2 * CH), lambda i, h: (0, i))

    def body(q_ref, k_ref, v_ref, rt_ref, do_ref, dvn_ref, dkd_ref, gam_ref, dgam_ref, ss_ref,
             ti_ref, u_ref, w_ref, vn_ref, dq_ref, dk_ref, dv_ref, drt_ref, acc):
        hstep = pl.program_id(1)

        @pl.when(hstep == 0)
        def _():
            acc[...] = jnp.zeros_like(acc)
        mc, ms = _pair_masks()
        ra, rb = slice(0, CH), slice(CH, 2 * CH)
        lane = _iota((1, 2 * CH), 1)
        hs = list(range(nh))
        qh = [h // 2 for h in hs]
        cs = [slice(h * HD, (h + 1) * HD) for h in hs]
        qq_ = [q_ref[:, t * HD:(t + 1) * HD] for t in range(qps)]
        kk_ = [k_ref[:, t * HD:(t + 1) * HD] for t in range(qps)]
        kk = [_dot_nt(k_, k_) for k_ in kk_]
        qk = [_dot_nt(q_, k_) for q_, k_ in zip(qq_, kk_)]
        cols = [_head_cols(rt_ref, nh * hstep + h) for h in hs]
        b_c = [c_[2] for c_ in cols]
        dec = [jnp.where(mc, jnp.exp(jnp.where(mc, c_[1] - c_[0], 0.0)), 0.0) for c_ in cols]
        eg = [jnp.exp(c_[1]) for c_ in cols]
        egl = [jnp.exp(c_[4] - c_[1]) for c_ in cols]
        dob = [do_ref[:, c] for c in cs]
        dvb = [dvn_ref[:, c] for c in cs]
        dqd = [jnp.concatenate([_dot_nt(d_[ra], ss_ref[h, 0:HD, :]), _dot_nt(d_[rb], ss_ref[h, HD:2 * HD, :])], axis=0)
               for h, d_ in zip(hs, dob)]
        dw = [-jnp.concatenate([_dot_nt(d_[ra], ss_ref[h, 0:HD, :]), _dot_nt(d_[rb], ss_ref[h, HD:2 * HD, :])], axis=0)
              for h, d_ in zip(hs, dvb)]
        dp = [jnp.where(mc, _dot_nt(d_, vn_ref[:, c]), 0.0) for d_, c in zip(dob, cs)]
        dbuw = [_dot3_tn(ti_ref[:, c], jnp.concatenate([d_, w_], axis=1)) for c, d_, w_ in zip(cs, dvb, dw)]
        dbu = [t[:, :HD] for t in dbuw]
        dbw = [t[:, HD:] for t in dbuw]
        da = [jnp.where(ms, -(_dot_nt(bu, u_ref[:, c]) + _dot_nt(bw, w_ref[:, c])), 0.0)
              for bu, bw, c in zip(dbu, dbw, cs)]
        dm = [a_ * d_ for a_, d_ in zip(da, dec)]
        dn_ = [p_ * d_ for p_, d_ in zip(dp, dec)]
        dbk = [_dot(m_, kk_[t]) for m_, t in zip(dm, qh)]
        dqs = [_dot(n_, kk_[t]) + e_ * q_ for n_, t, e_, q_ in zip(dn_, qh, eg, dqd)]
        dks = [_dot_tn(m_, b_ * kk_[t]) + _dot_tn(n_, qq_[t]) + el * dkd_ref[:, c] + b_ * (e_ * bw + bk)
               for m_, b_, t, n_, el, c, e_, bw, bk in zip(dm, b_c, qh, dn_, egl, cs, eg, dbw, dbk)]
        for t in range(qps):
            dq_ref[:, t * HD:(t + 1) * HD] = dqs[2 * t] + dqs[2 * t + 1]
            dk_ref[:, t * HD:(t + 1) * HD] = dks[2 * t] + dks[2 * t + 1]
        for h in hs:
            c, t, j = cs[h], qh[h], h % 2
            dv_ref[:, c] = b_c[h] * dbu[h]
            e = da[h] * (b_c[h] * kk[t] * dec[h]) + dp[h] * (qk[t] * dec[h])
            x = dkd_ref[:, c] * (egl[h] * kk_[t])
            egk = eg[h] * kk_[t]
            z = e + dqd[h] * (eg[h] * qq_[t]) - x + dbw[h] * (b_c[h] * egk)
            zb = dbw[h] * egk + dbu[h] * v_ref[:, c] + dbk[h] * kk_[t]
            sa = jnp.sum(jnp.sum(x[ra], axis=1, keepdims=True), axis=0, keepdims=True)
            sb = jnp.sum(jnp.sum(x[rb], axis=1, keepdims=True), axis=0, keepdims=True)
            la = sa + dgam_ref[0, t, 2 * j:2 * j + 1, :] * gam_ref[0, t, 2 * j:2 * j + 1, :]
            lb = sb + dgam_ref[0, t, 2 * j + 1:2 * j + 2, :] * gam_ref[0, t, 2 * j + 1:2 * j + 2, :]
            hg = nh * hstep + h
            acc[pl.ds(hg, 1), :] = _rsum(z.T - e)
            acc[pl.ds(NV + hg, 1), :] = _rsum(zb.T)
            acc[pl.ds(2 * NV + hg, 1), :] = jnp.where(lane < CH, la, lb)

        @pl.when(hstep == NQ // qps - 1)
        def _():
            drt_ref[...] = acc[...]

    return pl.pallas_call(
        body, name="dn_intra_bwd", grid=(nb, NQ // qps),
        in_specs=[blk, blk, blk2, rspec, blk2, blk2, blk2, gspec, gspec,
                  pl.BlockSpec((nh, 2 * HD, HD), lambda i, h: (h, i, 0)), blk2, blk2, blk2, blk2],
        out_specs=[blk, blk, blk2, rspec],
        out_shape=[jax.ShapeDtypeStruct((s, NQ * HD), F32), jax.ShapeDtypeStruct((s, NQ * HD), F32),
                   jax.ShapeDtypeStruct((s, NV * HD), F32), jax.ShapeDtypeStruct((3 * NV, s), F32)],
        scratch_shapes=[pltpu.VMEM((3 * NV, 2 * CH), F32)],
        compiler_params=_params(("parallel", "arbitrary")),
    )(q, k, v, rt, do, dvn, dkd, gam, dgam, ssave, tinv, u, w, vn)


def _l2n_heads_bwd(c, d_ref, dc_ref, off, scale):
    for hh in range(NQ):
        cs = slice(off + hh * HD, off + (hh + 1) * HD)
        x = c[:, cs]
        dy = d_ref[:, hh * HD:(hh + 1) * HD]
        r = lax.rsqrt(jnp.sum(x * x, axis=-1, keepdims=True) + L2_EPS)
        dc_ref[:, cs] = (scale * r) * (dy - x * (r * r) * jnp.sum(dy * x, axis=-1, keepdims=True))


def _dn_prep_bwd(dq, dk, dv, drt, pqkv, cpre, pab, cw, acol, dcol, dproj):
    s = pqkv.shape[0]
    tm = 128
    nt = s // tm
    wq = NQ * HD

    def body(dq_ref, dk_ref, dv_ref, drt_ref, x_ref, cp_ref, ab_ref, cw_ref, ac_ref,
             dc_ref, _, dx_ref, dab_ref, gcw_ref, gsc_ref, dcs, nxt):
        @pl.when(pl.program_id(0) == 0)
        def _():
            nxt[...] = jnp.zeros_like(nxt)
            gcw_ref[...] = jnp.zeros_like(gcw_ref)
            gsc_ref[...] = jnp.zeros_like(gsc_ref)
        cp = cp_ref[...]
        sg = _sigmoid(cp)
        c = cp * sg
        _l2n_heads_bwd(c, dq_ref, dcs, 0, Q_SCALE)
        _l2n_heads_bwd(c, dk_ref, dcs, wq, 1.0)
        dcs[:, 2 * wq:] = dv_ref[...]
        dcp = dcs[...] * (sg * (1.0 + cp * (1.0 - sg)))
        dx_ref[...] = _conv_causal_bwd(dcp, nxt[...], cw_ref, 4, x_ref[...], gcw_ref).astype(MXU_DT)
        nxt[...] = dcp[:8, :]
        lane = _iota((NV, tm), 1)
        dgt = drt_ref[0:NV, :] + jnp.where((lane & (CH - 1)) == CH - 1, drt_ref[2 * NV:3 * NV, :], 0.0)
        dg = _chunk_cumsum(dgt, 1, rev=True)
        abt = ab_ref[...].T
        zt = abt[0:NV, :] + dc_ref[...]
        gt = -jnp.exp(ac_ref[...]) * _softplus(zt)
        dat = dg * (-jnp.exp(ac_ref[...])) * _sigmoid(zt)
        bt = _sigmoid(abt[NV:2 * NV, :])
        dbt = drt_ref[NV:2 * NV, :] * bt * (1.0 - bt)
        full = jnp.concatenate([dat, dbt, jnp.zeros((128 - 2 * NV, tm), F32)], axis=0)
        dab_ref[...] = full.T.astype(MXU_DT)
        l2 = _iota((NV, 128), 1)
        gsc_ref[...] += jnp.where(l2 == 0, jnp.sum(dg * gt, axis=1, keepdims=True),
                                  jnp.where(l2 == 1, jnp.sum(dat, axis=1, keepdims=True), 0.0))

    return pl.pallas_call(
        body, name="dn_prep_bwd", grid=(nt,),
        in_specs=[_rows(tm, wq, 0, nt), _rows(tm, wq, 0, nt), _rows(tm, 2 * wq, 0, nt),
                  pl.BlockSpec((3 * NV, tm), lambda i: (0, nt - 1 - i)),
                  _rows(tm, 4 * wq, 0, nt), _rows(tm, 4 * wq, 0, nt), _rows(tm, 128, 0, nt),
                  _whole((8, 4 * wq)), _whole((NV, 1)), _whole((NV, 1)), _HBM],
        out_specs=[_rows(tm, 4 * wq, 0, nt), _rows(tm, 128, 0, nt), _whole((8, 4 * wq)), _whole((NV, 128))],
        out_shape=[jax.ShapeDtypeStruct(dproj.shape, MXU_DT), jax.ShapeDtypeStruct((s, 128), MXU_DT),
                   jax.ShapeDtypeStruct((8, 4 * wq), F32), jax.ShapeDtypeStruct((NV, 128), F32)],
        scratch_shapes=[pltpu.VMEM((tm, 4 * wq), F32), pltpu.VMEM((8, 4 * wq), F32)],
        compiler_params=_params(("arbitrary",)), input_output_aliases={10: 0},
    )(dq, dk, dv, drt, pqkv, cpre, pab, cw, acol, dcol, dproj)


def _rg_bwd(drec, px, xcs, h, cw, vec, wa, wx, wat, wxt, side=None):
    s = px.shape[0]
    tm = _pick(s, (256, 128))
    nt = s // tm
    w = D_RNN
    n_in = len(side.ins) if side else 0
    n_out = len(side.out_shapes) if side else 0

    def body(*refs):
        (dr_ref, xr_ref, xc_ref, gr_ref, h_ref, hb_ref, cw_ref, vec_ref, wa_ref, wx_ref, wat_ref,
         wxt_ref) = refs[:12]
        s_in = refs[12:12 + n_in]
        o_ref, gwa_ref, gwx_ref, gcw_ref, gvec_ref = refs[12 + n_in:17 + n_in]
        s_out = refs[17 + n_in:17 + n_in + n_out]
        nxt_a, nxt_l, nxt_d = refs[17 + n_in + n_out:20 + n_in + n_out]
        sems = refs[20 + n_in + n_out:]
        i = pl.program_id(0)
        if side:
            @pl.when(i == 0)
            def _():
                side.start(s_in, s_out, sems)

        @pl.when(i == 0)
        def _():
            nxt_a[...] = jnp.zeros_like(nxt_a)
            nxt_l[...] = jnp.zeros_like(nxt_l)
            nxt_d[...] = jnp.zeros_like(nxt_d)
            gwa_ref[...] = jnp.zeros_like(gwa_ref)
            gwx_ref[...] = jnp.zeros_like(gwx_ref)
            gcw_ref[...] = jnp.zeros_like(gcw_ref)
            gvec_ref[...] = jnp.zeros_like(gvec_ref)
        hbefore = jnp.where(i == nt - 1, 0.0, hb_ref[...])
        xc = xc_ref[...]
        r, ig, a, mult, sp, rmult = _rg_gates(xc, wa_ref, wx_ref, vec_ref)
        hh = h_ref[...]
        gel, dgel = _gelu_and_grad(gr_ref[...])
        drec_ = dr_ref[...]
        o_ref[:, w:] = (drec_ * hh * dgel).astype(MXU_DT)
        lam = _scan_rev(_shift_up(a, nxt_a[...], 1), drec_ * gel, nxt_l[0:1, :])
        nxt_a[...] = a[:8, :]
        nxt_l[...] = lam[:8, :]
        da = lam * _shift_down(hh, hbefore, 1)
        dxc = lam * mult * ig
        dla = da * a - (lam * ig * xc) * (a * a) * rmult
        dpr = dla * (-RG_C * sp) * r * (1.0 - r)
        dpi = (lam * mult * xc) * ig * (1.0 - ig)
        dprb = dpr.astype(MXU_DT)
        dpib = dpi.astype(MXU_DT)
        dxc = dxc + jnp.dot(dprb, wat_ref[...], preferred_element_type=F32) \
                  + jnp.dot(dpib, wxt_ref[...], preferred_element_type=F32)
        xcb = xc.astype(MXU_DT)
        gwa_ref[...] += _dot_tn(xcb, dprb)
        gwx_ref[...] += _dot_tn(xcb, dpib)
        o_ref[:, :w] = _conv_causal_bwd(dxc, nxt_d[...], cw_ref, 4, xr_ref[...], gcw_ref).astype(MXU_DT)
        nxt_d[...] = dxc[:8, :]
        gvec_ref[0:1, :] += _rsum(dxc)
        gvec_ref[1:2, :] += _rsum(dpr)
        gvec_ref[2:3, :] += _rsum(dpi)
        gvec_ref[3:4, :] += _rsum(dla * (-RG_C * r)) * (-_sigmoid(-vec_ref[3:4, :]))
        if side:
            @pl.when(i == nt - 1)
            def _():
                side.finish(s_in, s_out, sems)

    return pl.pallas_call(
        body, name="rg_bwd", grid=(nt,),
        in_specs=[_rows(tm, w, 0, nt), _rows(tm, w, 0, nt), _rows(tm, w, 0, nt), _rows(tm, w, 1, nt),
                  _rows(tm, w, 0, nt), _before(tm, w, 0, nt), _whole((8, w)), _whole((8, w)),
                  _whole((w, w)), _whole((w, w)), _whole((w, w)), _whole((w, w))] + [_HBM] * n_in,
        out_specs=[_rows(tm, 2 * w, 0, nt), _whole((w, w)), _whole((w, w)), _whole((8, w)), _whole((8, w))]
        + [_HBM] * n_out,
        out_shape=[jax.ShapeDtypeStruct((s, 2 * w), MXU_DT), jax.ShapeDtypeStruct((w, w), F32),
                   jax.ShapeDtypeStruct((w, w), F32), jax.ShapeDtypeStruct((8, w), F32),
                   jax.ShapeDtypeStruct((8, w), F32)] + (list(side.out_shapes) if side else []),
        scratch_shapes=[pltpu.VMEM((8, w), F32)] * 3 + (list(side.sems) if side else []),
        compiler_params=_params(("arbitrary",)),
    )(drec, px, xcs, px, h, h, cw, vec, wa, wx, wat, wxt, *(side.ins if side else []))


def _mm_epi(a, b, *, name, epi, extras=(), wholes=(), row_outs=(), acc_outs=(), tm=None, side=None):
    m, kk = a.shape
    n = b.shape[1]
    tm = tm or _pick(m, (512, 256, 128))
    tk = kk if kk <= 2816 else _pick(kk, (2816, 2176, 2048, 1024))
    nk, ni = kk // tk, m // tm
    n_ex, n_wh, n_ro, n_ao = len(extras), len(wholes), len(row_outs), len(acc_outs)
    n_si = len(side.ins) if side else 0
    n_so = len(side.out_shapes) if side else 0
    placed = [(j_, r_[2]) for j_, r_ in enumerate(row_outs) if len(r_) == 3]

    def body(*refs):
        a_ref, b_ref = refs[:2]
        p = 2
        ex, p = refs[p:p + n_ex], p + n_ex
        wh, p = refs[p:p + n_wh], p + n_wh
        s_in, p = refs[p:p + n_si], p + n_si
        p += len(placed)
        ro, p = refs[p:p + n_ro], p + n_ro
        ao, p = refs[p:p + n_ao], p + n_ao
        s_out, p = refs[p:p + n_so], p + n_so
        i, k = pl.program_id(0), pl.program_id(1)
        first = (i == 0) & (k == 0)
        if side:
            sems = refs[p + (1 if nk > 1 else 0):]

            @pl.when(first)
            def _():
                side.start(s_in, s_out, sems)
        if n_ao:
            @pl.when(first)
            def _():
                for r_ in ao:
                    r_[...] = jnp.zeros_like(r_)
        if nk == 1:
            epi(_dot(a_ref[...], b_ref[...]), ex, wh, ro, ao)
        else:
            acc = refs[p]

            @pl.when(k == 0)
            def _():
                acc[...] = jnp.zeros_like(acc)
            acc[...] += _dot(a_ref[...], b_ref[...])

            @pl.when(k == nk - 1)
            def _():
                epi(acc[...], ex, wh, ro, ao)
        if side:
            @pl.when((i == ni - 1) & (k == nk - 1))
            def _():
                side.finish(s_in, s_out, sems)

    in_specs = [pl.BlockSpec((tm, tk), lambda i, k: (i, k)), pl.BlockSpec((tk, n), lambda i, k: (k, 0))]
    in_specs += [pl.BlockSpec((tm, w_), functools.partial(lambda i, k, c_: (i, c_), c_=c_)) for _, w_, c_ in extras]
    in_specs += [_whole(x_.shape) for x_ in wholes] + [_HBM] * (n_si + len(placed))
    out_specs, out_shape = [], []
    for r_ in row_outs:
        blk = r_[2][1] if len(r_) == 3 else 0
        out_specs.append(pl.BlockSpec((tm, r_[0]), functools.partial(lambda i, k, c_: (i, c_), c_=blk)))
        out_shape.append(jax.ShapeDtypeStruct(r_[2][0].shape if len(r_) == 3 else (m, r_[0]), r_[1]))
    out_specs += [_whole(sh) for sh in acc_outs] + [_HBM] * n_so
    out_shape += [jax.ShapeDtypeStruct(sh, F32) for sh in acc_outs] + (list(side.out_shapes) if side else [])
    scratch = ([pltpu.VMEM((tm, n), F32)] if nk > 1 else []) + (list(side.sems) if side else [])
    first_placed = 2 + n_ex + n_wh + n_si
    return pl.pallas_call(
        body, name=name, grid=(ni, nk), in_specs=in_specs, out_specs=out_specs, out_shape=out_shape,
        scratch_shapes=scratch, compiler_params=_params(("arbitrary", "arbitrary")),
        input_output_aliases={first_placed + q_: j_ for q_, (j_, _) in enumerate(placed)},
    )(a, b, *[x_ for x_, _, _ in extras], *wholes, *(side.ins if side else []), *[pb_[0] for _, pb_ in placed])


def _mm_pro(b, *, name, pro, m, extras=(), wholes=(), row_outs=(), col_outs=(), tm=256):
    kk, n = b.shape
    tn = _pick(n, (1408, 1024, 512, 256, 128))
    n_ex, n_wh, n_ro, n_co = len(extras), len(wholes), len(row_outs), len(col_outs)

    def body(*refs):
        ex = refs[:n_ex]
        wh = refs[n_ex:n_ex + n_wh]
        b_ref, o_ref = refs[n_ex + n_wh], refs[n_ex + n_wh + 1]
        ro = refs[n_ex + n_wh + 2:n_ex + n_wh + 2 + n_ro]
        co = refs[n_ex + n_wh + 2 + n_ro:n_ex + n_wh + 2 + n_ro + n_co]
        a_scr = refs[-1]

        @pl.when(pl.program_id(1) == 0)
        def _():
            a_scr[...] = pro(ex, wh, ro, co).astype(MXU_DT)
        o_ref[...] = jnp.dot(a_scr[...], b_ref[...], preferred_element_type=F32)

    in_specs = [pl.BlockSpec((tm, w_), functools.partial(lambda i, j, c_: (i, c_), c_=c_)) for _, w_, c_ in extras]
    in_specs += [_whole(x_.shape) for x_ in wholes] + [pl.BlockSpec((kk, tn), lambda i, j: (0, j))]
    out_specs = [pl.BlockSpec((tm, tn), lambda i, j: (i, j))]
    out_specs += [pl.BlockSpec((tm, w_), lambda i, j: (i, 0)) for w_, _ in row_outs]
    out_specs += [pl.BlockSpec((w_, tm), lambda i, j: (0, i)) for w_, _ in col_outs]
    out_shape = [jax.ShapeDtypeStruct((m, n), F32)] + [jax.ShapeDtypeStruct((m, w_), dt) for w_, dt in row_outs]
    out_shape += [jax.ShapeDtypeStruct((w_, m), dt) for w_, dt in col_outs]
    return pl.pallas_call(
        body, name=name, grid=(m // tm, n // tn), in_specs=in_specs, out_specs=out_specs, out_shape=out_shape,
        scratch_shapes=[pltpu.VMEM((tm, kk), MXU_DT)], compiler_params=_params(("arbitrary", "arbitrary")),
    )(*[x_ for x_, _, _ in extras], *wholes, b.astype(MXU_DT))


def _merge_proj_out(pg, ya, yb, w_out):
    def pro(ex, wh, ro, co):
        ga_ref, gb_ref, ya_ref, yb_ref = ex
        mg = _sigmoid(ga_ref[...]) * ya_ref[...] + _sigmoid(gb_ref[...]) * yb_ref[...]
        co[0][...] = mg.T.astype(MXU_DT)
        return mg

    return _mm_pro(w_out, name="merge_proj_out", pro=pro, m=ya.shape[0],
                   extras=[(pg, D, 0), (pg, D, 1), (ya, D, 0), (yb, D, 0)], col_outs=[(D, MXU_DT)])


def _ffn_down_ln2_loss(act, w_down, x2, tgt, adam, lng, lnb):
    def epi(ff_, ex, wh, ro, ao):
        x_ref, t_ref = ex
        ada_ref, g_ref, b_ref = wh
        dff_ref, dx_ref = ro
        red_ref, = ao
        r = ALPHA * x_ref[...] + (1.0 + ada_ref[5:6, :]) * ff_
        xhat, rstd = _ln_stats(r)
        err = xhat * g_ref[...] + b_ref[...] - t_ref[...]
        dy = err * (1.0 / D)
        dr = _ln_bwd(dy, xhat, rstd, g_ref[...])
        dff_ref[...] = ((1.0 + ada_ref[5:6, :]) * dr).astype(MXU_DT)
        dx_ref[...] = ALPHA * dr
        red_ref[0:1, :] += _rsum(dy * xhat)
        red_ref[1:2, :] += _rsum(dy)
        red_ref[2:3, :] += _rsum(dr * ff_)
        red_ref[3:4, :] += jnp.sum(_rsum(err * err), axis=1, keepdims=True) * (0.5 / D)

    return _mm_epi(act, w_down, name="ffn_down_ln2_loss", epi=epi, extras=[(x2, D, 0), (tgt, D, 0)],
                   wholes=[adam, lng, lnb], row_outs=[(D, MXU_DT), (D, F32)], acc_outs=[(8, D)])


def _d_h1_modulate_bwd(dproj, w_t, dxa, x, adam, side):
    def epi(dh, ex, wh, ro, ao):
        dxa_ref, x_ref = ex
        ada_ref, = wh
        ro[0][...] = dxa_ref[...] + dh * (1.0 + ada_ref[1:2, :])
        ao[0][0:1, :] += _rsum(dh * x_ref[...])
        ao[0][1:2, :] += _rsum(dh)

    return _mm_epi(dproj, w_t, name="d_h1", epi=epi, extras=[(dxa, D, 0), (x, D, 0)], wholes=[adam],
                   row_outs=[(D, F32)], acc_outs=[(8, D)], side=side)


def _d_merged_bwd(dmix, w_out_t, pg, ya, yb, dproj):
    def epi(d, ex, wh, ro, ao):
        ga_ref, gb_ref, ya_ref, yb_ref = ex
        dya_ref, dyb_ref, dpg_ref = ro
        sa = _sigmoid(ga_ref[...])
        sb = _sigmoid(gb_ref[...])
        dya_ref[...] = (d * sa).astype(MXU_DT)
        dyb_ref[...] = (d * sb).astype(MXU_DT)
        dpg_ref[:, :D] = (d * ya_ref[...] * sa * (1.0 - sa)).astype(MXU_DT)
        dpg_ref[:, D:] = (d * yb_ref[...] * sb * (1.0 - sb)).astype(MXU_DT)

    return _mm_epi(dmix, w_out_t, name="d_merged", epi=epi,
                   extras=[(pg, D, 0), (pg, D, 1), (ya, D, 0), (yb, D, 0)],
                   row_outs=[(D, MXU_DT), (D, MXU_DT), (2 * D, MXU_DT, (dproj, 3))])


def _d_dn_post_bwd(dyb, w_pb_t, o, pz, nw, dproj):
    def epi(d_all, ex, wh, ro, ao):
        o_ref, z_ref = ex
        nw_ref, = wh
        do_ref, dz_ref = ro
        acc = jnp.zeros((1, HD), F32)
        for h in range(NV):
            cs = slice(h * HD, (h + 1) * HD)
            oh = o_ref[:, cs]
            z = z_ref[:, cs]
            d = d_all[:, cs]
            sg = _sigmoid(z)
            rs = lax.rsqrt(jnp.mean(oh * oh, axis=-1, keepdims=True) + RMS_EPS)
            n = oh * rs
            dz_ref[:, cs] = (d * n * nw_ref[...] * sg * (1.0 + z * (1.0 - sg))).astype(MXU_DT)
            dn_ = d * (z * sg)
            acc = acc + _rsum(dn_ * n)
            dnn = dn_ * nw_ref[...]
            do_ref[:, cs] = rs * (dnn - n * jnp.mean(dnn * n, axis=-1, keepdims=True))
        ao[0][...] += acc

    return _mm_epi(dyb, w_pb_t, name="d_dn", epi=epi, extras=[(o, NV * HD, 0), (pz, NV * HD, 0)], wholes=[nw],
                   row_outs=[(NV * HD, F32), (NV * HD, MXU_DT, (dproj, 2))], acc_outs=[(1, HD)], tm=256)


def _d_h2_ln1_bwd(dgu, w_gu_t, dx2a, x2, r1, mix, adam, lng):
    def epi(dh, ex, wh, ro, ao):
        dxa_ref, x2_ref, r_ref, m_ref = ex
        ada_ref, g_ref = wh
        dm_ref, dx_ref = ro
        red_ref, = ao
        dx2 = dxa_ref[...] + dh * (1.0 + ada_ref[4:5, :])
        xhat, rstd = _ln_stats(r_ref[...])
        dr = _ln_bwd(dx2, xhat, rstd, g_ref[...])
        dm_ref[...] = ((1.0 + ada_ref[2:3, :]) * dr).astype(MXU_DT)
        dx_ref[...] = ALPHA * dr
        red_ref[0:1, :] += _rsum(dh * x2_ref[...])
        red_ref[1:2, :] += _rsum(dh)
        red_ref[2:3, :] += _rsum(dx2 * xhat)
        red_ref[3:4, :] += _rsum(dx2)
        red_ref[4:5, :] += _rsum(dr * m_ref[...])

    return _mm_epi(dgu, w_gu_t, name="d_h2", epi=epi,
                   extras=[(dx2a, D, 0), (x2, D, 0), (r1, D, 0), (mix, D, 0)], wholes=[adam, lng],
                   row_outs=[(D, MXU_DT), (D, F32)], acc_outs=[(8, D)], tm=512)


def _adamw(parts, w, m, v, name):
    r, c = w.shape
    n_parts = len(parts)
    tm = _row_tile(r, c * 4 * (n_parts + 7))
    c1 = 1.0 - ADAM_B1 ** ADAM_STEP
    c2 = 1.0 - ADAM_B2 ** ADAM_STEP

    def body(*refs):
        g = refs[0][...]
        for p_ref in refs[1:n_parts]:
            g = g + p_ref[...]
        w_ref, m_ref, v_ref, g_out, d_out, m_out, v_out = refs[n_parts:]
        mn = ADAM_B1 * m_ref[...] + (1.0 - ADAM_B1) * g
        vn = ADAM_B2 * v_ref[...] + (1.0 - ADAM_B2) * (g * g)
        g_out[...] = g
        m_out[...] = mn
        v_out[...] = vn
        d_out[...] = -ADAM_LR * ((mn / c1) / (jnp.sqrt(vn / c2) + ADAM_EPS) + ADAM_WD * w_ref[...])

    spec = pl.BlockSpec((tm, c), lambda i: (i, 0))
    return pl.pallas_call(
        body, name=name, grid=(r // tm,),
        in_specs=[spec] * (n_parts + 3), out_specs=[spec] * 4,
        out_shape=[jax.ShapeDtypeStruct((r, c), F32)] * 4, compiler_params=_params(("parallel",)),
    )(*parts, w, m, v)


def _row_tile(r, bytes_per_row):
    for t in (512, 256, 128, 64, 32, 16):
        if r % t == 0 and 2 * t * bytes_per_row <= 20 * 1024 * 1024:
            return t
    return _pick(r, (16, 8))


def _sum_partials(own, recv, name):
    r, c = own.shape
    tm = _row_tile(r, c * (4 + 4 + 3 * 2))

    def body(o_ref, r_ref, out_ref):
        out_ref[...] = ((o_ref[...] + r_ref[0].astype(F32)) + r_ref[1].astype(F32)) + r_ref[2].astype(F32)

    return pl.pallas_call(
        body, name=name, grid=(r // tm,),
        in_specs=[pl.BlockSpec((tm, c), lambda i: (i, 0)), pl.BlockSpec((3, tm, c), lambda i: (0, i, 0))],
        out_specs=pl.BlockSpec((tm, c), lambda i: (i, 0)),
        out_shape=jax.ShapeDtypeStruct((r, c), F32), compiler_params=_params(("parallel",)),
    )(own, recv)


def _sum8(g, name):
    _, r, c = g.shape
    tm = _pick(r, (256, 128, 64, 32, 16, 8))

    def body(g_ref, o_ref):
        acc = g_ref[0]
        for k in range(1, 8):
            acc = acc + g_ref[k]
        o_ref[...] = acc

    return pl.pallas_call(
        body, name=name, grid=(r // tm,),
        in_specs=[pl.BlockSpec((8, tm, c), lambda i: (0, i, 0))],
        out_specs=pl.BlockSpec((tm, c), lambda i: (i, 0)),
        out_shape=jax.ShapeDtypeStruct((r, c), F32), compiler_params=_params(("parallel",)),
    )(g)


def _silu_rows(x):
    def body(x_ref, o_ref):
        xx = x_ref[...]
        o_ref[...] = xx * _sigmoid(xx)

    return pl.pallas_call(body, name="silu_rows", out_shape=jax.ShapeDtypeStruct(x.shape, F32))(x)


def _coords():
    return lax.axis_index("x"), lax.axis_index("y"), lax.axis_index("c")


def _allgather8(v, name):
    r, n = v.shape

    def body(v_ref, out_ref, send_sems, recv_sems):
        x, y, c = _coords()
        me = 4 * x + 2 * y + c
        out_ref[me] = v_ref[...]
        peers = []
        for k in range(1, 8):
            px = 1 - x if k & 4 else x
            py = 1 - y if k & 2 else y
            pc = 1 - c if k & 1 else c
            peers.append((px, py, pc))

        def copy(k, slot, to):
            return pltpu.make_async_remote_copy(
                src_ref=v_ref, dst_ref=out_ref.at[slot], send_sem=send_sems.at[k], recv_sem=recv_sems.at[k],
                device_id=to, device_id_type=MESH)

        sends = [copy(k, me, p) for k, p in enumerate(peers)]
        for cp in sends:
            cp.start()
        for k, (px, py, pc) in enumerate(peers):
            copy(k, 4 * px + 2 * py + pc, (px, py, pc)).wait_recv()
        for cp in sends:
            cp.wait_send()

    return pl.pallas_call(
        body, name=name, out_shape=jax.ShapeDtypeStruct((8, r, n), v.dtype),
        in_specs=[pl.BlockSpec(memory_space=pltpu.VMEM)], out_specs=pl.BlockSpec(memory_space=pltpu.VMEM),
        scratch_shapes=[pltpu.SemaphoreType.DMA((7,)), pltpu.SemaphoreType.DMA((7,))],
        compiler_params=pltpu.CompilerParams(vmem_limit_bytes=VMEM_LIMIT),
    )(v)


def _other_chips(x, y):
    return [(1 - x, y), (x, 1 - y), (1 - x, 1 - y)]


_HBM = pl.BlockSpec(memory_space=pl.ANY)


class _Exchange:
    def __init__(self, ins, out_shapes, sems, start, finish):
        self.ins, self.out_shapes, self.sems, self.start, self.finish = ins, out_shapes, sems, start, finish


def _run_exchange(ex, name):
    n_in, n_out = len(ex.ins), len(ex.out_shapes)

    def body(*refs):
        ins, outs, sems = refs[:n_in], refs[n_in:n_in + n_out], refs[n_in + n_out:]
        ex.start(ins, outs, sems)
        ex.finish(ins, outs, sems)

    return pl.pallas_call(body, name=name, out_shape=list(ex.out_shapes), in_specs=[_HBM] * n_in,
                          out_specs=[_HBM] * n_out, scratch_shapes=list(ex.sems))(*ex.ins)


def _gather_exchange(shards):
    n = len(shards)

    def plan(ins, outs, sems):
        send_sems, recv_sems, local_sems = sems
        x, y, c = _coords()
        s_me = 2 * x + y
        chips = _other_chips(x, y)

        def half(i, slot, hc):
            rh = shards[i].shape[0] // 2
            return outs[i].at[slot, pl.ds(pl.multiple_of(hc * rh, 16), rh), :]

        def copy(i, k, src, dst, to):
            return pltpu.make_async_remote_copy(src_ref=src, dst_ref=dst, send_sem=send_sems.at[6 * i + k],
                                                recv_sem=recv_sems.at[6 * i + k], device_id=to, device_id_type=MESH)

        local = [pltpu.make_async_copy(ins[i], outs[i].at[s_me], local_sems.at[i]) for i in range(n)]
        first = []
        for i in range(n):
            rh = shards[i].shape[0] // 2
            my_half = ins[i].at[pl.ds(pl.multiple_of(c * rh, 16), rh), :]
            first += [copy(i, j, my_half, half(i, s_me, c), (px, py, c)) for j, (px, py) in enumerate(chips)]
        return (x, y, c), chips, half, copy, local, first

    def start(ins, outs, sems):
        _, _, _, _, local, first = plan(ins, outs, sems)
        for cp in local + first:
            cp.start()

    def finish(ins, outs, sems):
        (x, y, c), chips, half, copy, local, first = plan(ins, outs, sems)
        sibling = (x, y, 1 - c)
        passed = []
        for i in range(n):
            for j, (px, py) in enumerate(chips):
                land = half(i, 2 * px + py, c)
                copy(i, j, land, land, (px, py, c)).wait_recv()
                fw = copy(i, 3 + j, land, land, sibling)
                fw.start()
                passed.append(fw)
        for i in range(n):
            for j, (px, py) in enumerate(chips):
                land = half(i, 2 * px + py, 1 - c)
                copy(i, 3 + j, land, land, sibling).wait_recv()
        for cp in first + passed:
            cp.wait_send()
        for cp in local:
            cp.wait()

    return _Exchange(list(shards), [jax.ShapeDtypeStruct((4,) + v.shape, v.dtype) for v in shards],
                     [pltpu.SemaphoreType.DMA((6 * n,)), pltpu.SemaphoreType.DMA((6 * n,)),
                      pltpu.SemaphoreType.DMA((n,))], start, finish)


def _scatter_exchange(gs):
    n = len(gs)

    def copies(ins, outs, sems):
        send_sems, recv_sems = sems
        x, y, c = _coords()
        return [pltpu.make_async_remote_copy(
                    src_ref=ins[i].at[2 * px + py], dst_ref=outs[i].at[j], send_sem=send_sems.at[3 * i + j],
                    recv_sem=recv_sems.at[3 * i + j], device_id=(px, py, c), device_id_type=MESH)
                for i in range(n) for j, (px, py) in enumerate(_other_chips(x, y))]

    def start(ins, outs, sems):
        for cp in copies(ins, outs, sems):
            cp.start()

    def finish(ins, outs, sems):
        cps = copies(ins, outs, sems)
        for cp in cps:
            cp.wait_recv()
        for cp in cps:
            cp.wait_send()

    return _Exchange(list(gs), [jax.ShapeDtypeStruct((3,) + g.shape[1:], g.dtype) for g in gs],
                     [pltpu.SemaphoreType.DMA((3 * n,)), pltpu.SemaphoreType.DMA((3 * n,))], start, finish)


def _allgather_exchange(v):
    def copies(ins, outs, sems):
        send_sems, recv_sems, _ = sems
        x, y, c = _coords()
        me = 4 * x + 2 * y + c
        cps = []
        for k in range(1, 8):
            peer = (1 - x if k & 4 else x, 1 - y if k & 2 else y, 1 - c if k & 1 else c)
            cps.append(pltpu.make_async_remote_copy(
                src_ref=ins[0], dst_ref=outs[0].at[me], send_sem=send_sems.at[k - 1], recv_sem=recv_sems.at[k - 1],
                device_id=peer, device_id_type=MESH))
        return me, cps

    def start(ins, outs, sems):
        me, cps = copies(ins, outs, sems)
        pltpu.make_async_copy(ins[0], outs[0].at[me], sems[2]).start()
        for cp in cps:
            cp.start()

    def finish(ins, outs, sems):
        me, cps = copies(ins, outs, sems)
        for cp in cps:
            cp.wait_recv()
        for cp in cps:
            cp.wait_send()
        pltpu.make_async_copy(ins[0], outs[0].at[me], sems[2]).wait()

    return _Exchange([v], [jax.ShapeDtypeStruct((8,) + v.shape, v.dtype)],
                     [pltpu.SemaphoreType.DMA((7,)), pltpu.SemaphoreType.DMA((7,)), pltpu.SemaphoreType.DMA],
                     start, finish)


def _swap_sibling(vs):
    n = len(vs)

    def body(*refs):
        ins, outs = refs[:n], refs[n:2 * n]
        send_sems, recv_sems = refs[2 * n:]
        x, y, c = _coords()
        cps = [pltpu.make_async_remote_copy(src_ref=ins[i], dst_ref=outs[i], send_sem=send_sems.at[i],
                                            recv_sem=recv_sems.at[i], device_id=(x, y, 1 - c), device_id_type=MESH)
               for i in range(n)]
        for cp in cps:
            cp.start()
        for cp in cps:
            cp.wait()

    return pl.pallas_call(
        body, name="swap_sibling", out_shape=[jax.ShapeDtypeStruct(v.shape, v.dtype) for v in vs],
        in_specs=[_HBM] * n, out_specs=[_HBM] * n,
        scratch_shapes=[pltpu.SemaphoreType.DMA((n,)), pltpu.SemaphoreType.DMA((n,))],
    )(*vs)


def _pad_rows(a, rows):
    return jnp.pad(a, ((0, rows - a.shape[0]), (0, 0)))


def _block_diag(w):
    eye = jnp.eye(RG_BLOCKS, dtype=w.dtype)
    return (eye[:, None, :, None] * w[:, :, None, :]).reshape(D_RNN, D_RNN)


def _diag_blocks(g):
    g4 = g.reshape(RG_BLOCKS, RG_BW, RG_BLOCKS, RG_BW)
    idx = jnp.arange(RG_BLOCKS)
    return g4[idx, :, idx, :]


def _prepare_rest(p):
    w = {}
    for k_, n_ in (("pa", "w_proj_a"), ("pb", "w_proj_b"), ("out", "w_out"), ("down", "ffn_w_down")):
        w[k_] = p[n_].astype(MXU_DT)
        w[k_ + "_t"] = w[k_].T
    w["gu"] = jnp.concatenate([p["ffn_w_gate"], p["ffn_w_up"]], axis=1).astype(MXU_DT)
    w["gu_t"] = w["gu"].T
    return w


def _prepare_first(p):
    w = {}
    wi = p["w_in"].astype(MXU_DT)
    cat = jnp.concatenate([wi[:, 2560:6656], wi[:, 6656:8704], wi[:, 8736:10784], wi[:, 0:2560],
                           wi[:, 8704:8736], jnp.zeros((D, 96), MXU_DT)], axis=1)
    w["in_cat"], w["in_cat_t"] = cat, cat.T
    w["rg_cw"] = _pad_rows(p["rg_conv_w"], 8)
    w["dn_cw"] = _pad_rows(p["dn_conv_w"], 8)
    w["ffn_cw"] = _pad_rows(p["ffn_conv_w"], 8)
    w["rg_vec"] = _pad_rows(jnp.stack([p["rg_conv_b"], p["rg_b_a"], p["rg_b_x"], p["rg_lambda"]]), 8)
    w["wa"] = _block_diag(p["rg_w_a"]).astype(MXU_DT)
    w["wx"] = _block_diag(p["rg_w_x"]).astype(MXU_DT)
    w["wa_t"], w["wx_t"] = w["wa"].T, w["wx"].T
    w["arow"] = jnp.pad(p["dn_a_log"], (0, 128 - NV))[None, :]
    w["drow"] = jnp.pad(p["dn_dt_bias"], (0, 128 - NV))[None, :]
    w["acol"] = p["dn_a_log"][:, None]
    w["dcol"] = p["dn_dt_bias"][:, None]
    w["nw"] = p["dn_norm_w"][None, :]
    w["ffn_cb"] = p["ffn_conv_b"][None, :]
    for n_ in ("ln1_g", "ln1_b", "ln2_g", "ln2_b"):
        w[n_] = p[n_][None, :]
    return w


def _mm_sided(a, b, side, **kw):
    if side is None:
        return _mm(a, b, **kw), []
    res = _mm(a, b, side=side, **kw)
    return res[0], res[1:]


_SMALL = ("rg_conv_w", "dn_conv_w", "ffn_conv_w", "rg_conv_b", "rg_w_a", "rg_b_a", "rg_w_x", "rg_b_x",
          "rg_lambda", "dn_a_log", "dn_dt_bias", "dn_norm_w", "ln1_g", "ln1_b", "ffn_conv_b", "ln2_g", "ln2_b")
_REST_A = ("w_proj_a", "w_proj_b", "w_out")
_REST_B = ("ffn_w_gate", "ffn_w_up", "ffn_w_down")


def _local_step(x, tgt, adam, p, shards=None):
    w = _prepare_first(p)
    side_a = side_b = None
    if shards is not None:
        side_a = _gather_exchange([shards[n] for n in _REST_A])
        side_b = _gather_exchange([shards[n] for n in _REST_B])
    h1, h1t = _modulate(x, adam)
    cat = w["in_cat"]
    pqkv, got_a = _mm_sided(h1, cat[:, 0:4096], side_a, name="proj_qkv")
    px, got_b = _mm_sided(h1, cat[:, 8192:10752], side_b, name="proj_x")
    if shards is not None:
        p = dict(p, **{n: _unstack_shards(n, a_) for n, a_ in zip(_REST_A + _REST_B, got_a + got_b)})
    w.update(_prepare_rest(p))
    pz = _mm(h1, cat[:, 4096:6144], name="proj_z")
    pg = _mm(h1, cat[:, 6144:8192], name="proj_g")
    pab = _mm(h1, cat[:, 10752:10880], name="proj_ab")
    hrec, xcs, rec, rec_t = _rg_fwd(px, w["rg_cw"], w["rg_vec"], w["wa"], w["wx"])
    q, k, v, rt, cpre = _dn_prep(pqkv, pab, w["dn_cw"], w["arow"], w["drow"])
    u, ww, qd, kd, pm, tinv, gam = _dn_intra(q, k, v, rt)
    o, vn, ssave, dn, dn_t = _dn_seq(u, ww, qd, kd, pm, gam, pz, w["nw"])
    ya = _mm(rec, w["pa"], name="proj_a")
    yb = _mm(dn, w["pb"], name="proj_b")
    mix, merged_t = _merge_proj_out(pg, ya, yb, w["out"])
    r1, x2, h2, h2_t = _ln1(x, mix, adam, w["ln1_g"], w["ln1_b"])
    gu = _mm(h2, w["gu"], name="ffn_gu")
    act, act_t, gcf = _ffn_act(gu, w["ffn_cw"], w["ffn_cb"])
    dff, dx2a, red2 = _ffn_down_ln2_loss(act, w["down"], x2, tgt, adam, w["ln2_g"], w["ln2_b"])
    g = {}
    dact = _mm(dff, w["down_t"], name="d_act")
    g["ffn_w_down"] = _mm(act_t, dff, name="g_down")
    dgu, gcw_f, gcb_f = _ffn_bwd(dact, gu, gcf, w["ffn_cw"])
    ggu = _mm(h2_t, dgu, name="g_gu")
    g["ffn_w_gate"], g["ffn_w_up"] = ggu[:, :D_FF], ggu[:, D_FF:]
    g["ffn_conv_w"], g["ffn_conv_b"] = gcw_f[0:3], gcb_f[0]
    dmix, dxa, red1 = _d_h2_ln1_bwd(dgu, w["gu_t"], dx2a, x2, r1, mix, adam, w["ln1_g"])
    g["w_out"] = _mm(merged_t, dmix, name="g_out")
    dproj = lax.empty((x.shape[0], N_CAT), MXU_DT)
    dya, dyb, dproj = _d_merged_bwd(dmix, w["out_t"], pg, ya, yb, dproj)
    drec = _mm(dya, w["pa_t"], name="d_rec")
    g["w_proj_a"] = _mm(rec_t, dya, name="g_pa")
    g["w_proj_b"] = _mm(dn_t, dyb, name="g_pb")
    do, dproj, gnw = _d_dn_post_bwd(dyb, w["pb_t"], o, pz, w["nw"], dproj)
    dvn, dkd, dgam = _dn_seq_bwd(do, qd, kd, pm, ww, vn, ssave, gam)
    dq, dk, dv, drt = _dn_intra_bwd(q, k, v, rt, do, dvn, dkd, gam, dgam, ssave, tinv, u, ww, vn)
    dproj, dpab, gcw_d, gsc = _dn_prep_bwd(dq, dk, dv, drt, pqkv, cpre, pab, w["dn_cw"], w["acol"], w["dcol"],
                                           dproj)
    side_r = None
    if shards is not None:
        side_r = _scatter_exchange([_stack_shards(n, g[n]).astype(MXU_DT) for n in _REST_A + _REST_B])
    dpx, gwa, gwx, gcw_r, gvec, *got_r = _rg_bwd(drec, px, xcs, hrec, w["rg_cw"], w["rg_vec"], w["wa"], w["wx"],
                                                 w["wa_t"], w["wx_t"], side_r)
    dproj = lax.dynamic_update_slice(dproj, jnp.concatenate([dpx, dpab], axis=1), (0, 8192))
    reorder = lambda gc: jnp.concatenate([gc[:, 8192:10752], gc[:, 0:4096], gc[:, 4096:6144], gc[:, 10752:10784],
                                          gc[:, 6144:8192]], axis=1)
    wire = lambda gh: _scatter_exchange([_stack_shards("w_in", gh).astype(MXU_DT)]) if shards is not None else None
    g["rg_conv_w"], g["rg_conv_b"] = gcw_r[0:4], gvec[0]
    g["rg_w_a"], g["rg_w_x"] = _diag_blocks(gwa), _diag_blocks(gwx)
    g["rg_b_a"], g["rg_b_x"], g["rg_lambda"] = gvec[1], gvec[2], gvec[3]
    g["dn_conv_w"] = gcw_d[0:4]
    g["dn_a_log"], g["dn_dt_bias"], g["dn_norm_w"] = gsc[:, 0], gsc[:, 1], gnw[0]
    g["ln1_g"], g["ln1_b"] = red1[2], red1[3]
    g["ln2_g"], g["ln2_b"] = red2[0], red2[1]
    side_s = None
    if shards is not None:
        side_s = _allgather_exchange(_pack([jnp.full((1,), red2[3, 0], F32)] + [g[n] for n in _SMALL], 128, 64))
    g_top, got_s = _mm_sided(h1t[:D // 2], dproj, side_s, name="g_in_top")
    g_top = reorder(g_top)
    g_bot, got_top = _mm_sided(h1t[D // 2:], dproj, wire(g_top), name="g_in_bot")
    g_bot = reorder(g_bot)
    gx, red0, *got_bot = _d_h1_modulate_bwd(dproj, w["in_cat_t"], dxa, x, adam, wire(g_bot))
    g["w_in"] = jnp.concatenate([g_top, g_bot], axis=0)
    got = dict(zip(_REST_A + _REST_B, got_r))
    if shards is not None:
        got["w_in"] = jnp.concatenate([got_top[0], got_bot[0]], axis=1)
        got["small"] = got_s[0]
    d_ada = jnp.concatenate([red0[1], red0[0], red1[4], red1[1], red1[0], red2[2]])
    return red2[3, 0], gx, g, d_ada, got


_BIG = ("w_in", "w_proj_a", "w_proj_b", "w_out", "ffn_w_gate", "ffn_w_up", "ffn_w_down")
_COL_SHARDED = ("w_in", "ffn_w_gate", "ffn_w_up")
_CONV = ("rg_conv_w", "dn_conv_w", "ffn_conv_w")
_REPL = ("b_ada", "rg_conv_b", "rg_w_a", "rg_b_a", "rg_w_x", "rg_b_x", "rg_lambda", "dn_a_log",
         "dn_dt_bias", "dn_norm_w", "ln1_g", "ln1_b", "ffn_conv_b", "ln2_g", "ln2_b")
_NAMES = ("w_ada", "b_ada", "w_in", "rg_conv_w", "rg_conv_b", "rg_w_a", "rg_b_a", "rg_w_x", "rg_b_x",
          "rg_lambda", "dn_conv_w", "dn_a_log", "dn_dt_bias", "dn_norm_w", "w_proj_a", "w_proj_b", "w_out",
          "ln1_g", "ln1_b", "ffn_w_gate", "ffn_w_up", "ffn_conv_w", "ffn_conv_b", "ffn_w_down", "ln2_g", "ln2_b")


def _pack(arrs, width, row_mult):
    pieces = []
    for a in arrs:
        f = a.reshape(-1)
        pieces.append(jnp.pad(f, (0, (-f.shape[0]) % (8 * width))).reshape(-1, width))
    rows = sum(p_.shape[0] for p_ in pieces)
    if rows % row_mult:
        pieces.append(jnp.zeros((row_mult - rows % row_mult, width), pieces[0].dtype))
    return jnp.concatenate(pieces, axis=0)


def _unpack(flat, shapes, width):
    out, row = [], 0
    for shp in shapes:
        n = 1
        for d_ in shp:
            n *= d_
        rows = -(-n // (8 * width)) * 8
        out.append(flat[row:row + rows].reshape(-1)[:n].reshape(shp))
        row += rows
    return out


def _stack_shards(name, full):
    if name in _COL_SHARDED or name in _CONV:
        r, ncol = full.shape
        return full.reshape(r, 4, ncol // 4).transpose(1, 0, 2)
    return full.reshape((4, full.shape[0] // 4) + full.shape[1:])


def _unstack_shards(name, st):
    if name in _COL_SHARDED or name in _CONV:
        return st.transpose(1, 0, 2).reshape(st.shape[1], 4 * st.shape[2])
    return st.reshape((4 * st.shape[1],) + st.shape[2:])


def kernel(x, c, w_ada, b_ada, w_in, rg_conv_w, rg_conv_b, rg_w_a, rg_b_a, rg_w_x, rg_b_x, rg_lambda, dn_conv_w, dn_a_log, dn_dt_bias, dn_norm_w, w_proj_a, w_proj_b, w_out, ln1_g, ln1_b, ffn_w_gate, ffn_w_up, ffn_conv_w, ffn_conv_b, ffn_w_down, ln2_g, ln2_b, loss_target, m_w_ada, m_b_ada, m_w_in, m_rg_conv_w, m_rg_conv_b, m_rg_w_a, m_rg_b_a, m_rg_w_x, m_rg_b_x, m_rg_lambda, m_dn_conv_w, m_dn_a_log, m_dn_dt_bias, m_dn_norm_w, m_w_proj_a, m_w_proj_b, m_w_out, m_ln1_g, m_ln1_b, m_ffn_w_gate, m_ffn_w_up, m_ffn_conv_w, m_ffn_conv_b, m_ffn_w_down, m_ln2_g, m_ln2_b, v_w_ada, v_b_ada, v_w_in, v_rg_conv_w, v_rg_conv_b, v_rg_w_a, v_rg_b_a, v_rg_w_x, v_rg_b_x, v_rg_lambda, v_dn_conv_w, v_dn_a_log, v_dn_dt_bias, v_dn_norm_w, v_w_proj_a, v_w_proj_b, v_w_out, v_ln1_g, v_ln1_b, v_ffn_w_gate, v_ffn_w_up, v_ffn_conv_w, v_ffn_conv_b, v_ffn_w_down, v_ln2_g, v_ln2_b):
    args = locals()
    wts = {n: args[n][0] for n in _NAMES}
    mom = {n: args["m_" + n][0] for n in _NAMES}
    var = {n: args["v_" + n][0] for n in _NAMES}
    xs, tgt = x[0], loss_target[0]
    ix, iy, ic = _coords()
    shard = 2 * ix + iy
    batch = 4 * ix + 2 * iy + ic

    c_all = _allgather8(_pad_rows(c, 8), "gather_c")[:, 0, :]
    sc16 = _pad_rows(_silu_rows(c_all), 16)
    ada_cols = _mm(sc16, w_ada[0], name="ada")[:8]
    ada_g = _allgather8(ada_cols, "gather_ada")
    ada_all = jnp.concatenate([ada_g[0], ada_g[2], ada_g[4], ada_g[6]], axis=1) + b_ada
    adam = _pad_rows(lax.dynamic_index_in_dim(ada_all, batch, 0, keepdims=False).reshape(6, D), 8)

    (w_in_all,) = _run_exchange(_gather_exchange([wts["w_in"].astype(MXU_DT)]), "gather_w_in")
    full = {"w_in": _unstack_shards("w_in", w_in_all)}
    conv_shard = _pack([wts[n] for n in _CONV], 128, 8)
    conv_all = _allgather8(conv_shard, "gather_conv")
    shapes_conv = [wts[n].shape for n in _CONV]
    per_shard = [_unpack(conv_all[2 * s], shapes_conv, 128) for s in range(4)]
    for i, n in enumerate(_CONV):
        full[n] = _unstack_shards(n, jnp.stack([per_shard[s][i] for s in range(4)]))
    for n in _REPL:
        full[n] = wts[n]

    shards = {n: wts[n].astype(MXU_DT) for n in _REST_A + _REST_B}
    loss_b, gx, g, d_ada, recv = _local_step(xs, tgt, adam, full, shards)

    parts = []
    for n in _BIG:
        r_ = recv[n]
        axis = 1 if n in _COL_SHARDED else 0
        width = wts[n].shape[axis]
        own = lax.dynamic_slice_in_dim(g[n], shard * width, width, axis=axis)
        parts.append(_sum_partials(own, r_, "sum_" + n))
    parts_sib = _swap_sibling(parts)
    out = {n: [] for n in _NAMES}
    for n, p_, q_ in zip(_BIG, parts, parts_sib):
        out[n] = list(_adamw([p_, q_], wts[n], mom[n], var[n], "adamw_" + n))

    tot = _unpack(_sum8(recv["small"], "sum_small"), [(1,)] + [full[n].shape for n in _SMALL], 128)
    gsum = dict(zip(_SMALL, tot[1:]))
    loss = tot[0][0]
    for n in _CONV:
        gsum[n] = lax.dynamic_index_in_dim(_stack_shards(n, gsum[n]), shard, 0, keepdims=False)
    d_ada_g = _allgather8(d_ada.reshape(6 * D // 128, 128), "gather_d_ada")
    gsum["b_ada"] = _sum8(d_ada_g, "sum_d_ada").reshape(6 * D)
    d_ada_all = d_ada_g.reshape(8, 6 * D)
    cols = lax.dynamic_slice_in_dim(d_ada_all, shard * (6 * D // 4), 6 * D // 4, axis=1)
    g_wada = _mm(sc16, _pad_rows(cols, 16), name="g_ada", trans_a=True)
    res = _adamw([g_wada], wts["w_ada"], mom["w_ada"], var["w_ada"], "adamw_ada")
    out["w_ada"] = list(res)
    names_s = _CONV + _REPL
    shapes_s = [wts[n].shape for n in names_s]
    pk = lambda d_: _pack([d_[n] for n in names_s], 128, 64)
    res_s = _adamw([pk(gsum)], pk(wts), pk(mom), pk(var), "adamw_small")
    for r_ in res_s:
        for n, a in zip(names_s, _unpack(r_, shapes_s, 128)):
            out[n].append(a)

    outs = [loss, gx[None]]
    for i in range(4):
        outs += [out[n][i][None] for n in _NAMES]
    return tuple(outs)
```

```python
import functools

import jax
import jax.numpy as jnp
from jax import lax
from jax.experimental import pallas as pl
from jax.experimental.pallas import tpu as pltpu

F32 = jnp.float32
BF16 = jnp.bfloat16
MXU_DT = BF16

D = 1024
D_RNN = 1280
RG_BLOCKS = 16
RG_BW = 80
RG_C = 8.0
NQ = 8
NV = 16
HD = 128
CH = 64
D_FF = 2816
LN_EPS = 1e-5
RMS_EPS = 1e-6
L2_EPS = 1e-6
ALPHA = 2.0 ** 0.25
Q_SCALE = HD ** -0.5
N_CAT = 10880
VMEM_LIMIT = 56 * 1024 * 1024
MM_VMEM_BUDGET = 36 * 1024 * 1024
SEQ_GROUP = 16
MESH = pl.DeviceIdType.MESH

ADAM_LR, ADAM_B1, ADAM_B2, ADAM_EPS, ADAM_WD, ADAM_STEP = 1e-3, 0.9, 0.999, 1e-8, 0.01, 10


def _sigmoid(x):
    return 0.5 * jnp.tanh(0.5 * x) + 0.5


def _softplus(x):
    return jnp.maximum(x, 0.0) + jnp.log1p(jnp.exp(-jnp.abs(x)))


_GC = 0.7978845608028654


def _gelu(x):
    return 0.5 * x * (1.0 + jnp.tanh(_GC * (x + 0.044715 * x * x * x)))


def _gelu_and_grad(x):
    t = jnp.tanh(_GC * (x + 0.044715 * x * x * x))
    g = 0.5 * x * (1.0 + t)
    dg = 0.5 * (1.0 + t) + 0.5 * x * (1.0 - t * t) * _GC * (1.0 + 3 * 0.044715 * x * x)
    return g, dg


def _neg_expm1(y):
    series = -y * (1.0 + 0.5 * y * (1.0 + y * (1.0 / 3.0)))
    return jnp.where(y > -0.01, series, 1.0 - jnp.exp(y))


def _dot(a, b):
    return jnp.dot(a.astype(MXU_DT), b.astype(MXU_DT), preferred_element_type=F32)


def _dot_nt(a, b):
    return lax.dot_general(a.astype(MXU_DT), b.astype(MXU_DT), (((1,), (1,)), ((), ())),
                           preferred_element_type=F32)


def _dot_tn(a, b):
    return lax.dot_general(a.astype(MXU_DT), b.astype(MXU_DT), (((0,), (0,)), ((), ())),
                           preferred_element_type=F32)


def _split(a):
    hi = a.astype(BF16)
    return hi, (a - hi.astype(F32)).astype(BF16)


def _dot3(a, b, dims=(((1,), (0,)), ((), ()))):
    ah, al = _split(a)
    bh, bl = _split(b)
    d = lambda p, q: lax.dot_general(p, q, dims, preferred_element_type=F32)
    return d(ah, bh) + (d(al, bh) + d(ah, bl))


def _dot3_tn(a, b):
    return _dot3(a, b, (((0,), (0,)), ((), ())))


def _iota(shape, dim):
    return lax.broadcasted_iota(jnp.int32, shape, dim)


def _shift_down(x, before, j):
    if j == 0:
        return x
    xr = pltpu.roll(x, j, 0)
    br = pltpu.roll(before, j, 0)
    top = jnp.where(_iota(br.shape, 0) < j, br, xr[:8])
    return jnp.concatenate([top, xr[8:]], axis=0)


def _shift_up(x, after, j):
    if j == 0:
        return x
    t = x.shape[0]
    xr = pltpu.roll(x, t - j, 0)
    ar = pltpu.roll(after, 8 - j, 0)
    bot = jnp.where(_iota(ar.shape, 0) >= 8 - j, ar, xr[t - 8:])
    return jnp.concatenate([xr[:t - 8], bot], axis=0)


def _taps(x, before, k):
    return [_shift_down(x, before, k - 1 - i) for i in range(k)]


def _conv_taps(taps, w_ref):
    y = w_ref[0:1, :] * taps[0]
    for i in range(1, len(taps)):
        y = y + w_ref[i:i + 1, :] * taps[i]
    return y


def _conv_causal(x, before, w_ref, k):
    return _conv_taps(_taps(x, before, k), w_ref)


def _conv_causal_bwd(dy, after, w_ref, k, x=None, gw_ref=None):
    dx = None
    for i in range(k):
        sh = _shift_up(dy, after, k - 1 - i)
        term = w_ref[i:i + 1, :] * sh
        dx = term if dx is None else dx + term
        if x is not None:
            gw_ref[i:i + 1, :] += _rsum(x * sh)
    return dx


def _scan_fwd(a, u, carry):
    t = a.shape[0]
    pos = _iota(a.shape, 0) & 7
    for d in (1, 2, 4):
        m = pos >= d
        u = u + jnp.where(m, a * pltpu.roll(u, d, 0), 0.0)
        a = jnp.where(m, a * pltpu.roll(a, d, 0), a)
    out = []
    for g in range(t // 8):
        hg = u[8 * g:8 * g + 8] + a[8 * g:8 * g + 8] * carry
        out.append(hg)
        carry = hg[7:8]
    return jnp.concatenate(out, axis=0)


def _scan_rev(a, u, carry):
    t = a.shape[0]
    pos = _iota(a.shape, 0) & 7
    for d in (1, 2, 4):
        m = pos < 8 - d
        u = u + jnp.where(m, a * pltpu.roll(u, t - d, 0), 0.0)
        a = jnp.where(m, a * pltpu.roll(a, t - d, 0), a)
    out = [None] * (t // 8)
    for g in reversed(range(t // 8)):
        lg = u[8 * g:8 * g + 8] + a[8 * g:8 * g + 8] * carry
        out[g] = lg
        carry = lg[0:1]
    return jnp.concatenate(out, axis=0)


def _chunk_cumsum(g, axis, rev=False):
    n = g.shape[axis]
    pos = _iota(g.shape, axis) & (CH - 1)
    d = 1
    while d < CH:
        if rev:
            g = g + jnp.where(pos < CH - d, pltpu.roll(g, n - d, axis), 0.0)
        else:
            g = g + jnp.where(pos >= d, pltpu.roll(g, d, axis), 0.0)
        d *= 2
    return g


def _ln_stats(r):
    mu = jnp.mean(r, axis=-1, keepdims=True)
    xc = r - mu
    var = jnp.mean(xc * xc, axis=-1, keepdims=True)
    rstd = lax.rsqrt(var + LN_EPS)
    return xc * rstd, rstd


def _ln_bwd(dy, xhat, rstd, g):
    dxh = dy * g
    return rstd * (dxh - jnp.mean(dxh, axis=-1, keepdims=True)
                   - xhat * jnp.mean(dxh * xhat, axis=-1, keepdims=True))


def _rsum(x):
    return jnp.sum(x, axis=0, keepdims=True)


def _params(sem):
    return pltpu.CompilerParams(dimension_semantics=sem, vmem_limit_bytes=VMEM_LIMIT)


def _pick(n, cands):
    for c in cands:
        if n % c == 0:
            return c
    return n


def _rows(tm, w, col=0, nt=None):
    if nt is None:
        return pl.BlockSpec((tm, w), lambda i: (i, col))
    return pl.BlockSpec((tm, w), lambda i: (nt - 1 - i, col))


def _before(tm, w, col=0, nt=None):
    r = tm // 8
    if nt is None:
        return pl.BlockSpec((8, w), lambda i: (jnp.maximum(i * r - 1, 0), col))
    return pl.BlockSpec((8, w), lambda i: (jnp.maximum((nt - 1 - i) * r - 1, 0), col))


def _cols(tm, w):
    return pl.BlockSpec((w, tm), lambda i: (0, i))


def _whole(shape):
    return pl.BlockSpec(shape, lambda *_: (0,) * len(shape))


def _mm_plan(a, b, out_dtype):
    m, kk = a.shape
    _, n = b.shape
    tm = _pick(m, (512, 256, 128))
    tk = kk if kk <= 5632 else _pick(kk, (2176, 2048, 1024))
    nk = kk // tk

    def vmem_bytes(tn):
        blocks = tm * tk * a.dtype.itemsize + tk * tn * b.dtype.itemsize + tm * tn * jnp.dtype(out_dtype).itemsize
        return 2 * blocks + (tm * tn * 4 if nk > 1 else 0)

    cands = [t for t in (1408, 1280, 1024, 640, 512, 256, 128) if n % t == 0] or [n]
    tn = next((t for t in cands if vmem_bytes(t) <= MM_VMEM_BUDGET), cands[-1])
    return tm, tn, tk, nk


def _mm_side(a, b, name, out_dtype, side):
    m, _ = a.shape
    _, n = b.shape
    tm, tn, tk, nk = _mm_plan(a, b, out_dtype)
    ni, nj = m // tm, n // tn
    n_in, n_out = len(side.ins), len(side.out_shapes)

    def body(*refs):
        a_ref, b_ref = refs[0], refs[1]
        s_in = refs[2:2 + n_in]
        o_ref = refs[2 + n_in]
        s_out = refs[3 + n_in:3 + n_in + n_out]
        acc = refs[3 + n_in + n_out]
        sems = refs[4 + n_in + n_out:]
        i, j, k = pl.program_id(0), pl.program_id(1), pl.program_id(2)

        @pl.when((i == 0) & (j == 0) & (k == 0))
        def _():
            side.start(s_in, s_out, sems)

        @pl.when(k == 0)
        def _():
            acc[...] = jnp.zeros_like(acc)
        acc[...] += _dot(a_ref[...], b_ref[...])

        @pl.when(k == nk - 1)
        def _():
            o_ref[...] = acc[...].astype(out_dtype)

        @pl.when((i == ni - 1) & (j == nj - 1) & (k == nk - 1))
        def _():
            side.finish(s_in, s_out, sems)

    return pl.pallas_call(
        body, name=name, grid=(ni, nj, nk),
        in_specs=[pl.BlockSpec((tm, tk), lambda i, j, k: (i, k)),
                  pl.BlockSpec((tk, tn), lambda i, j, k: (k, j))] + [_HBM] * n_in,
        out_specs=[pl.BlockSpec((tm, tn), lambda i, j, k: (i, j))] + [_HBM] * n_out,
        out_shape=[jax.ShapeDtypeStruct((m, n), out_dtype)] + list(side.out_shapes),
        scratch_shapes=[pltpu.VMEM((tm, tn), F32)] + list(side.sems),
        compiler_params=_params(("arbitrary", "arbitrary", "arbitrary")),
    )(a, b, *side.ins)


def _mm(a, b, *, name, trans_a=False, out_dtype=F32, side=None):
    if trans_a:
        return _mm(a.T, b, name=name, out_dtype=out_dtype, side=side)
    if side is not None:
        return _mm_side(a, b, name, out_dtype, side)
    m, kk = a.shape
    _, n = b.shape
    tm, tn, tk, nk = _mm_plan(a, b, out_dtype)

    if nk == 1:
        def body(a_ref, b_ref, o_ref):
            o_ref[...] = _dot(a_ref[...], b_ref[...]).astype(out_dtype)
        scratch = []
    else:
        def body(a_ref, b_ref, o_ref, acc):
            k = pl.program_id(2)

            @pl.when(k == 0)
            def _():
                acc[...] = jnp.zeros_like(acc)
            acc[...] += _dot(a_ref[...], b_ref[...])

            @pl.when(k == nk - 1)
            def _():
                o_ref[...] = acc[...].astype(out_dtype)
        scratch = [pltpu.VMEM((tm, tn), F32)]

    return pl.pallas_call(
        body, name=name, grid=(m // tm, n // tn, nk),
        in_specs=[pl.BlockSpec((tm, tk), lambda i, j, k: (i, k)),
                  pl.BlockSpec((tk, tn), lambda i, j, k: (k, j))],
        out_specs=pl.BlockSpec((tm, tn), lambda i, j, k: (i, j)),
        out_shape=jax.ShapeDtypeStruct((m, n), out_dtype),
        scratch_shapes=scratch,
        compiler_params=_params(("parallel", "parallel", "arbitrary")),
    )(a, b)


def _modulate(x, adam):
    s = x.shape[0]
    tm = _pick(s, (512, 256, 128))

    def body(x_ref, ada_ref, o_ref, ot_ref):
        h = x_ref[...] * (1.0 + ada_ref[1:2, :]) + ada_ref[0:1, :]
        o_ref[...] = h.astype(MXU_DT)
        ot_ref[...] = h.T.astype(MXU_DT)

    return pl.pallas_call(
        body, name="modulate1", grid=(s // tm,),
        in_specs=[_rows(tm, D), _whole((8, D))], out_specs=[_rows(tm, D), _cols(tm, D)],
        out_shape=[jax.ShapeDtypeStruct((s, D), MXU_DT), jax.ShapeDtypeStruct((D, s), MXU_DT)],
        compiler_params=_params(("parallel",)),
    )(x, adam)


def _rg_gates(xc, wa_ref, wx_ref, vec_ref):
    xb = xc.astype(MXU_DT)
    r = _sigmoid(jnp.dot(xb, wa_ref[...], preferred_element_type=F32) + vec_ref[1:2, :])
    ig = _sigmoid(jnp.dot(xb, wx_ref[...], preferred_element_type=F32) + vec_ref[2:3, :])
    sp = _softplus(-vec_ref[3:4, :])
    la = -RG_C * r * sp
    a = jnp.exp(la)
    n1 = _neg_expm1(2.0 * la)
    rmult = lax.rsqrt(jnp.maximum(n1, 1e-20))
    return r, ig, a, n1 * rmult, sp, rmult


def _rg_fwd(px, cw, vec, wa, wx):
    s = px.shape[0]
    tm = _pick(s, (256, 128))
    w = D_RNN

    def body(xr_ref, gr_ref, cw_ref, vec_ref, wa_ref, wx_ref, h_ref, xc_ref, rec_ref, rect_ref, prev_x, prev_h):
        @pl.when(pl.program_id(0) == 0)
        def _():
            prev_x[...] = jnp.zeros_like(prev_x)
            prev_h[...] = jnp.zeros_like(prev_h)
        x = xr_ref[...]
        xc = _conv_causal(x, prev_x[...], cw_ref, 4) + vec_ref[0:1, :]
        prev_x[...] = x[tm - 8:, :]
        xc_ref[...] = xc
        _, ig, a, mult, _, _ = _rg_gates(xc, wa_ref, wx_ref, vec_ref)
        h = _scan_fwd(a, mult * ig * xc, prev_h[7:8, :])
        prev_h[...] = h[tm - 8:, :]
        h_ref[...] = h
        rec = h * _gelu(gr_ref[...])
        rec_ref[...] = rec.astype(MXU_DT)
        rect_ref[...] = rec.T.astype(MXU_DT)

    return pl.pallas_call(
        body, name="rg_fwd", grid=(s // tm,),
        in_specs=[_rows(tm, w, 0), _rows(tm, w, 1), _whole((8, w)), _whole((8, w)),
                  _whole((w, w)), _whole((w, w))],
        out_specs=[_rows(tm, w), _rows(tm, w), _rows(tm, w), _cols(tm, w)],
        out_shape=[jax.ShapeDtypeStruct((s, w), F32), jax.ShapeDtypeStruct((s, w), F32),
                   jax.ShapeDtypeStruct((s, w), MXU_DT), jax.ShapeDtypeStruct((w, s), MXU_DT)],
        scratch_shapes=[pltpu.VMEM((8, w), F32), pltpu.VMEM((8, w), F32)],
        compiler_params=_params(("arbitrary",)),
    )(px, px, cw, vec, wa, wx)


def _dn_scalars(ab, arow, drow):
    lane = _iota(ab.shape, 1)
    g = jnp.where(lane < NV, -jnp.exp(arow) * _softplus(ab + drow), 0.0)
    beta = _sigmoid(ab)
    return lane, g, beta


def _l2n_heads(c, out_ref, off, scale):
    for hh in range(NQ):
        x = c[:, off + hh * HD: off + (hh + 1) * HD]
        r = lax.rsqrt(jnp.sum(x * x, axis=-1, keepdims=True) + L2_EPS)
        out_ref[:, hh * HD:(hh + 1) * HD] = x * (r * scale)


def _dn_prep(pqkv, pab, cw, arow, drow):
    s = pqkv.shape[0]
    tm = 128
    wq = NQ * HD

    def body(x_ref, ab_ref, cw_ref, a_ref, d_ref, q_ref, k_ref, v_ref, rt_ref, cp_ref, prev_x):
        @pl.when(pl.program_id(0) == 0)
        def _():
            prev_x[...] = jnp.zeros_like(prev_x)
        x = x_ref[...]
        cp = _conv_causal(x, prev_x[...], cw_ref, 4)
        prev_x[...] = x[tm - 8:, :]
        cp_ref[...] = cp
        c = cp * _sigmoid(cp)
        _l2n_heads(c, q_ref, 0, Q_SCALE)
        _l2n_heads(c, k_ref, wq, 1.0)
        v_ref[...] = c[:, 2 * wq:]
        lane, g, beta = _dn_scalars(ab_ref[...], a_ref[...], d_ref[...])
        gc = _chunk_cumsum(g, 0)
        gl = gc + _chunk_cumsum(g, 0, rev=True) - g
        pack = jnp.where(lane < NV, gc, jnp.where(lane < 2 * NV, beta,
                         jnp.where(lane < 3 * NV, pltpu.roll(gl, 2 * NV, 1), 0.0)))
        rt_ref[...] = pack.T[0:3 * NV, :]

    return pl.pallas_call(
        body, name="dn_prep", grid=(s // tm,),
        in_specs=[_rows(tm, 4 * wq), _rows(tm, 128), _whole((8, 4 * wq)), _whole((1, 128)), _whole((1, 128))],
        out_specs=[_rows(tm, wq), _rows(tm, wq), _rows(tm, 2 * wq),
                   pl.BlockSpec((3 * NV, tm), lambda i: (0, i)), _rows(tm, 4 * wq)],
        out_shape=[jax.ShapeDtypeStruct((s, wq), F32), jax.ShapeDtypeStruct((s, wq), F32),
                   jax.ShapeDtypeStruct((s, 2 * wq), F32), jax.ShapeDtypeStruct((3 * NV, s), F32),
                   jax.ShapeDtypeStruct((s, 4 * wq), F32)],
        scratch_shapes=[pltpu.VMEM((8, 4 * wq), F32)],
        compiler_params=_params(("arbitrary",)),
    )(pqkv, pab, cw, arow, drow)


def _pair_masks():
    i = _iota((2 * CH, 2 * CH), 0)
    j = _iota((2 * CH, 2 * CH), 1)
    same = (i >> 6) == (j >> 6)
    return same & (i >= j), same & (i > j)


def _head_cols(rt_ref, h):
    shp = (2 * CH, 2 * CH)
    g_r = jnp.broadcast_to(rt_ref[pl.ds(h, 1), :], shp)
    b_r = jnp.broadcast_to(rt_ref[pl.ds(NV + h, 1), :], shp)
    l_r = jnp.broadcast_to(rt_ref[pl.ds(2 * NV + h, 1), :], shp)
    return g_r, g_r.T, b_r.T, l_r, l_r.T


def _inv_unit_lower_many(a_list):
    n = a_list[0].shape[0]
    eye = (_iota((n, n), 0) == _iota((n, n), 1)).astype(F32)
    bs = [-a for a in a_list]
    xs = [eye + b for b in bs]
    for _ in range(5):
        bs = [_dot(b, b) for b in bs]
        xs = [x + _dot(x, b) for x, b in zip(xs, bs)]
    rs = [(eye - x) - _dot3(a, x) for a, x in zip(a_list, xs)]
    return [x + _dot(x, r) for x, r in zip(xs, rs)]


def _gam_rows(l_r):
    lrow = l_r[0:1, :]
    lane = _iota(lrow.shape, 1)
    other = pltpu.roll(lrow, CH, 1)
    return jnp.exp(jnp.where(lane < CH, lrow, other)), jnp.exp(jnp.where(lane >= CH, lrow, other))


def _dn_intra(q, k, v, rt):
    s = q.shape[0]
    nb = s // (2 * CH)
    qps = 8
    blk = pl.BlockSpec((2 * CH, qps * HD), lambda i, h: (i, h))
    blk2 = pl.BlockSpec((2 * CH, 2 * qps * HD), lambda i, h: (i, h))

    def body(q_ref, k_ref, v_ref, rt_ref, u_ref, w_ref, qd_ref, kd_ref, p_ref, ti_ref, gam_ref):
        hstep = pl.program_id(1)
        mc, ms = _pair_masks()
        gam_ref[...] = jnp.zeros_like(gam_ref)
        heads = []
        for qh in range(qps):
            qq = q_ref[:, qh * HD:(qh + 1) * HD]
            kk_ = k_ref[:, qh * HD:(qh + 1) * HD]
            kk = _dot_nt(kk_, kk_)
            qk = _dot_nt(qq, kk_)
            for j in range(2):
                idx = 2 * qh + j
                cs = slice(idx * HD, (idx + 1) * HD)
                g_r, g_c, b_c, l_r, l_c = _head_cols(rt_ref, 2 * qps * hstep + idx)
                dec = jnp.where(mc, jnp.exp(jnp.where(mc, g_c - g_r, 0.0)), 0.0)
                eg = jnp.exp(g_c)
                p_ref[:, cs] = jnp.where(mc, qk * dec, 0.0)
                qd_ref[:, cs] = eg * qq
                kd_ref[:, cs] = jnp.exp(l_c - g_c) * kk_
                ga, gb = _gam_rows(l_r)
                gam_ref[0, qh, 2 * j:2 * j + 1, :] = ga
                gam_ref[0, qh, 2 * j + 1:2 * j + 2, :] = gb
                rhs = jnp.concatenate([b_c * v_ref[:, cs], b_c * eg * kk_], axis=1)
                heads.append((cs, jnp.where(ms, b_c * kk * dec, 0.0), rhs))
        tinvs = _inv_unit_lower_many([a for _, a, _ in heads])
        uws = [_dot3(t, rhs) for t, (_, _, rhs) in zip(tinvs, heads)]
        for t, uw, (cs, _, _) in zip(tinvs, uws, heads):
            ti_ref[:, cs] = t
            u_ref[:, cs] = uw[:, :HD]
            w_ref[:, cs] = uw[:, HD:]

    big = jax.ShapeDtypeStruct((s, NV * HD), F32)
    return pl.pallas_call(
        body, name="dn_intra", grid=(nb, NQ // qps),
        in_specs=[blk, blk, blk2, pl.BlockSpec((3 * NV, 2 * CH), lambda i, h: (0, i))],
        out_specs=[blk2] * 6 + [pl.BlockSpec((1, qps, 8, 128), lambda i, h: (i, h, 0, 0))],
        out_shape=[big] * 6 + [jax.ShapeDtypeStruct((nb, NQ, 8, 128), F32)],
        compiler_params=_params(("parallel", "parallel")),
    )(q, k, v, rt)


def _dn_seq(u, w, qd, kd, p, gam, pz, nw):
    s = u.shape[0]
    nb = s // (2 * CH)
    wide = pl.BlockSpec((2 * CH, NV * HD), lambda i: (i, 0))

    def body(u_ref, w_ref, qd_ref, kd_ref, p_ref, gam_ref, z_ref, nw_ref, o_ref, vn_ref, ss_ref, y_ref, yt_ref, st):
        @pl.when(pl.program_id(0) == 0)
        def _():
            st[...] = jnp.zeros_like(st)
        ra, rb = slice(0, CH), slice(CH, 2 * CH)
        for g0 in range(0, NV, SEQ_GROUP):
            hs = list(range(g0, g0 + SEQ_GROUP))
            cs = [slice(h * HD, (h + 1) * HD) for h in hs]
            ga = [gam_ref[0, h // 2, 2 * (h % 2):2 * (h % 2) + 1, :] for h in hs]
            gb = [gam_ref[0, h // 2, 2 * (h % 2) + 1:2 * (h % 2) + 2, :] for h in hs]
            s0 = [st[h] for h in hs]
            vna = [u_ref[ra, c] - _dot(w_ref[ra, c], s) for c, s in zip(cs, s0)]
            s1 = [g * s + _dot_tn(kd_ref[ra, c], v) for g, s, c, v in zip(ga, s0, cs, vna)]
            vnb = [u_ref[rb, c] - _dot(w_ref[rb, c], s) for c, s in zip(cs, s1)]
            s2 = [g * s + _dot_tn(kd_ref[rb, c], v) for g, s, c, v in zip(gb, s1, cs, vnb)]
            for h, s in zip(hs, s2):
                st[h] = s
            oa = [_dot(qd_ref[ra, c], s) for c, s in zip(cs, s0)]
            ob = [_dot(qd_ref[rb, c], s) for c, s in zip(cs, s1)]
            for i_, h in enumerate(hs):
                vn = jnp.concatenate([vna[i_], vnb[i_]], axis=0)
                oh = jnp.concatenate([oa[i_], ob[i_]], axis=0) + _dot(p_ref[:, cs[i_]], vn)
                o_ref[:, cs[i_]] = oh
                vn_ref[:, cs[i_]] = vn
                ss_ref[h, 0:HD, :] = s0[i_]
                ss_ref[h, HD:2 * HD, :] = s1[i_]
                z = z_ref[:, cs[i_]]
                rs = lax.rsqrt(jnp.mean(oh * oh, axis=-1, keepdims=True) + RMS_EPS)
                y = oh * rs * nw_ref[...] * (z * _sigmoid(z))
                y_ref[:, cs[i_]] = y.astype(MXU_DT)
                yt_ref[cs[i_], :] = y.T.astype(MXU_DT)

    big = jax.ShapeDtypeStruct((s, NV * HD), F32)
    return pl.pallas_call(
        body, name="dn_seq", grid=(nb,),
        in_specs=[wide] * 5 + [pl.BlockSpec((1, NQ, 8, 128), lambda i: (i, 0, 0, 0)), wide, _whole((1, HD))],
        out_specs=[wide, wide, pl.BlockSpec((NV, 2 * HD, HD), lambda i: (0, i, 0)), wide, _cols(2 * CH, NV * HD)],
        out_shape=[big, big, jax.ShapeDtypeStruct((NV, 2 * s, HD), F32),
                   jax.ShapeDtypeStruct((s, NV * HD), MXU_DT), jax.ShapeDtypeStruct((NV * HD, s), MXU_DT)],
        scratch_shapes=[pltpu.VMEM((NV, HD, HD), F32)],
        compiler_params=_params(("arbitrary",)),
    )(u, w, qd, kd, p, gam, pz, nw)


def _ln1(x, mix, adam, lng, lnb):
    s = x.shape[0]
    tm = _pick(s, (512, 256, 128))

    def body(x_ref, m_ref, ada_ref, g_ref, b_ref, r_ref, x2_ref, h2_ref, h2t_ref):
        r = ALPHA * x_ref[...] + (1.0 + ada_ref[2:3, :]) * m_ref[...]
        xhat, _ = _ln_stats(r)
        x2 = xhat * g_ref[...] + b_ref[...]
        r_ref[...] = r
        x2_ref[...] = x2
        h2 = x2 * (1.0 + ada_ref[4:5, :]) + ada_ref[3:4, :]
        h2_ref[...] = h2.astype(MXU_DT)
        h2t_ref[...] = h2.T.astype(MXU_DT)

    return pl.pallas_call(
        body, name="ln1", grid=(s // tm,),
        in_specs=[_rows(tm, D), _rows(tm, D), _whole((8, D)), _whole((1, D)), _whole((1, D))],
        out_specs=[_rows(tm, D)] * 3 + [_cols(tm, D)],
        out_shape=[jax.ShapeDtypeStruct((s, D), F32), jax.ShapeDtypeStruct((s, D), F32),
                   jax.ShapeDtypeStruct((s, D), MXU_DT), jax.ShapeDtypeStruct((D, s), MXU_DT)],
        compiler_params=_params(("parallel",)),
    )(x, mix, adam, lng, lnb)


def _ffn_act(gu, cw, cb):
    s = gu.shape[0]
    tm = _pick(s, (256, 128))
    w = D_FF

    def body(g_ref, u_ref, cw_ref, cb_ref, o_ref, ot_ref, gc_ref, prev):
        @pl.when(pl.program_id(0) == 0)
        def _():
            prev[...] = jnp.zeros_like(prev)
        g = g_ref[...]
        gc = _conv_causal(g, prev[...], cw_ref, 3) + cb_ref[...]
        prev[...] = g[tm - 8:, :]
        gc_ref[...] = gc
        act = _gelu(gc) * u_ref[...]
        o_ref[...] = act.astype(MXU_DT)
        ot_ref[...] = act.T.astype(MXU_DT)

    return pl.pallas_call(
        body, name="ffn_act", grid=(s // tm,),
        in_specs=[_rows(tm, w, 0), _rows(tm, w, 1), _whole((8, w)), _whole((1, w))],
        out_specs=[_rows(tm, w), _cols(tm, w), _rows(tm, w)],
        out_shape=[jax.ShapeDtypeStruct((s, w), MXU_DT), jax.ShapeDtypeStruct((w, s), MXU_DT),
                   jax.ShapeDtypeStruct((s, w), F32)],
        scratch_shapes=[pltpu.VMEM((8, w), F32)],
        compiler_params=_params(("arbitrary",)),
    )(gu, gu, cw, cb)


def _ffn_bwd(dact, gu, gc, cw):
    s = dact.shape[0]
    tm = _pick(s, (256, 128))
    nt = s // tm
    w = D_FF

    def body(da_ref, g_ref, u_ref, gc_ref, cw_ref, o_ref, gcw_ref, gcb_ref, nxt):
        @pl.when(pl.program_id(0) == 0)
        def _():
            nxt[...] = jnp.zeros_like(nxt)
            gcw_ref[...] = jnp.zeros_like(gcw_ref)
            gcb_ref[...] = jnp.zeros_like(gcb_ref)
        gel, dgel = _gelu_and_grad(gc_ref[...])
        da = da_ref[...]
        dgc = da * u_ref[...] * dgel
        o_ref[:, w:] = (da * gel).astype(MXU_DT)
        o_ref[:, :w] = _conv_causal_bwd(dgc, nxt[...], cw_ref, 3, g_ref[...], gcw_ref).astype(MXU_DT)
        nxt[...] = dgc[:8, :]
        gcb_ref[...] += _rsum(dgc)

    return pl.pallas_call(
        body, name="ffn_bwd", grid=(nt,),
        in_specs=[_rows(tm, w, 0, nt), _rows(tm, w, 0, nt), _rows(tm, w, 1, nt), _rows(tm, w, 0, nt),
                  _whole((8, w))],
        out_specs=[_rows(tm, 2 * w, 0, nt), _whole((8, w)), _whole((1, w))],
        out_shape=[jax.ShapeDtypeStruct((s, 2 * w), MXU_DT), jax.ShapeDtypeStruct((8, w), F32),
                   jax.ShapeDtypeStruct((1, w), F32)],
        scratch_shapes=[pltpu.VMEM((8, w), F32)],
        compiler_params=_params(("arbitrary",)),
    )(dact, gu, gu, gc, cw)


def _dn_seq_bwd(do, qd, kd, p, w, vn, ssave, gam):
    s = do.shape[0]
    nb = s // (2 * CH)
    wide = pl.BlockSpec((2 * CH, NV * HD), lambda i: (nb - 1 - i, 0))
    gspec = pl.BlockSpec((1, NQ, 8, 128), lambda i: (nb - 1 - i, 0, 0, 0))

    def body(do_ref, qd_ref, kd_ref, p_ref, w_ref, vn_ref, ss_ref, gam_ref, dvn_ref, dkd_ref, dgam_ref, dst):
        @pl.when(pl.program_id(0) == 0)
        def _():
            dst[...] = jnp.zeros_like(dst)
        dgam_ref[...] = jnp.zeros_like(dgam_ref)
        ra, rb = slice(0, CH), slice(CH, 2 * CH)
        tot = lambda t: jnp.sum(jnp.sum(t, axis=1, keepdims=True), axis=0, keepdims=True)
        for g0 in range(0, NV, SEQ_GROUP):
            hs = list(range(g0, g0 + SEQ_GROUP))
            cs = [slice(h * HD, (h + 1) * HD) for h in hs]
            ga = [gam_ref[0, h // 2, 2 * (h % 2):2 * (h % 2) + 1, :] for h in hs]
            gb = [gam_ref[0, h // 2, 2 * (h % 2) + 1:2 * (h % 2) + 2, :] for h in hs]
            ds2 = [dst[h] for h in hs]
            pdo = [_dot_tn(p_ref[:, c], do_ref[:, c]) for c in cs]
            qdo_b = [_dot_tn(qd_ref[rb, c], do_ref[rb, c]) for c in cs]
            qdo_a = [_dot_tn(qd_ref[ra, c], do_ref[ra, c]) for c in cs]
            dvb = [p_[rb] + _dot(kd_ref[rb, c], d_) for p_, c, d_ in zip(pdo, cs, ds2)]
            ds1 = [g * d_ + q_ - _dot_tn(w_ref[rb, c], v_)
                   for g, d_, q_, c, v_ in zip(gb, ds2, qdo_b, cs, dvb)]
            dva = [p_[ra] + _dot(kd_ref[ra, c], d_) for p_, c, d_ in zip(pdo, cs, ds1)]
            ds0 = [g * d_ + q_ - _dot_tn(w_ref[ra, c], v_)
                   for g, d_, q_, c, v_ in zip(ga, ds1, qdo_a, cs, dva)]
            for h, d_ in zip(hs, ds0):
                dst[h] = d_
            for i_, h in enumerate(hs):
                c = cs[i_]
                row = 2 * (h % 2)
                dkd_ref[rb, c] = _dot_nt(vn_ref[rb, c], ds2[i_])
                dkd_ref[ra, c] = _dot_nt(vn_ref[ra, c], ds1[i_])
                dvn_ref[ra, c] = dva[i_]
                dvn_ref[rb, c] = dvb[i_]
                dgam_ref[0, h // 2, row:row + 1, :] = jnp.broadcast_to(tot(ds1[i_] * ss_ref[h, 0:HD, :]), (1, 128))
                dgam_ref[0, h // 2, row + 1:row + 2, :] = jnp.broadcast_to(
                    tot(ds2[i_] * ss_ref[h, HD:2 * HD, :]), (1, 128))

    big = jax.ShapeDtypeStruct((s, NV * HD), F32)
    return pl.pallas_call(
        body, name="dn_seq_bwd", grid=(nb,),
        in_specs=[wide] * 6 + [pl.BlockSpec((NV, 2 * HD, HD), lambda i: (0, nb - 1 - i, 0)), gspec],
        out_specs=[wide, wide, gspec],
        out_shape=[big, big, jax.ShapeDtypeStruct((nb, NQ, 8, 128), F32)],
        scratch_shapes=[pltpu.VMEM((NV, HD, HD), F32)],
        compiler_params=_params(("arbitrary",)),
    )(do, qd, kd, p, w, vn, ssave, gam)


def _dn_intra_bwd(q, k, v, rt, do, dvn, dkd, gam, dgam, ssave, tinv, u, w, vn):
    s = q.shape[0]
    nb = s // (2 * CH)
    qps = 8
    nh = 2 * qps
    blk = pl.BlockSpec((2 * CH, qps * HD), lambda i, h: (i, h))
    blk2 = pl.BlockSpec((2 * CH, nh * HD), lambda i, h: (i, h))
    gspec = pl.BlockSpec((1, qps, 8, 128), lambda i, h: (i, h, 0, 0))
    rspec = pl.BlockSpec((3 * NV, 2 * CH), lambda i, h: (0, i))

    def body(q_ref, k_ref, v_ref, rt_ref, do_ref, dvn_ref, dkd_ref, gam_ref, dgam_ref, ss_ref,
             ti_ref, u_ref, w_ref, vn_ref, dq_ref, dk_ref, dv_ref, drt_ref, acc):
        hstep = pl.program_id(1)

        @pl.when(hstep == 0)
        def _():
            acc[...] = jnp.zeros_like(acc)
        mc, ms = _pair_masks()
        ra, rb = slice(0, CH), slice(CH, 2 * CH)
        lane = _iota((1, 2 * CH), 1)
        hs = list(range(nh))
        qh = [h // 2 for h in hs]
        cs = [slice(h * HD, (h + 1) * HD) for h in hs]
        qq_ = [q_ref[:, t * HD:(t + 1) * HD] for t in range(qps)]
        kk_ = [k_ref[:, t * HD:(t + 1) * HD] for t in range(qps)]
        kk = [_dot_nt(k_, k_) for k_ in kk_]
        qk = [_dot_nt(q_, k_) for q_, k_ in zip(qq_, kk_)]
        cols = [_head_cols(rt_ref, nh * hstep + h) for h in hs]
        b_c = [c_[2] for c_ in cols]
        dec = [jnp.where(mc, jnp.exp(jnp.where(mc, c_[1] - c_[0], 0.0)), 0.0) for c_ in cols]
        eg = [jnp.exp(c_[1]) for c_ in cols]
        egl = [jnp.exp(c_[4] - c_[1]) for c_ in cols]
        dob = [do_ref[:, c] for c in cs]
        dvb = [dvn_ref[:, c] for c in cs]
        dqd = [jnp.concatenate([_dot_nt(d_[ra], ss_ref[h, 0:HD, :]), _dot_nt(d_[rb], ss_ref[h, HD:2 * HD, :])], axis=0)
               for h, d_ in zip(hs, dob)]
        dw = [-jnp.concatenate([_dot_nt(d_[ra], ss_ref[h, 0:HD, :]), _dot_nt(d_[rb], ss_ref[h, HD:2 * HD, :])], axis=0)
              for h, d_ in zip(hs, dvb)]
        dp = [jnp.where(mc, _dot_nt(d_, vn_ref[:, c]), 0.0) for d_, c in zip(dob, cs)]
        dbuw = [_dot3_tn(ti_ref[:, c], jnp.concatenate([d_, w_], axis=1)) for c, d_, w_ in zip(cs, dvb, dw)]
        dbu = [t[:, :HD] for t in dbuw]
        dbw = [t[:, HD:] for t in dbuw]
        da = [jnp.where(ms, -(_dot_nt(bu, u_ref[:, c]) + _dot_nt(bw, w_ref[:, c])), 0.0)
              for bu, bw, c in zip(dbu, dbw, cs)]
        dm = [a_ * d_ for a_, d_ in zip(da, dec)]
        dn_ = [p_ * d_ for p_, d_ in zip(dp, dec)]
        dbk = [_dot(m_, kk_[t]) for m_, t in zip(dm, qh)]
        dqs = [_dot(n_, kk_[t]) + e_ * q_ for n_, t, e_, q_ in zip(dn_, qh, eg, dqd)]
        dks = [_dot_tn(m_, b_ * kk_[t]) + _dot_tn(n_, qq_[t]) + el * dkd_ref[:, c] + b_ * (e_ * bw + bk)
               for m_, b_, t, n_, el, c, e_, bw, bk in zip(dm, b_c, qh, dn_, egl, cs, eg, dbw, dbk)]
        for t in range(qps):
            dq_ref[:, t * HD:(t + 1) * HD] = dqs[2 * t] + dqs[2 * t + 1]
            dk_ref[:, t * HD:(t + 1) * HD] = dks[2 * t] + dks[2 * t + 1]
        for h in hs:
            c, t, j = cs[h], qh[h], h % 2
            dv_ref[:, c] = b_c[h] * dbu[h]
            e = da[h] * (b_c[h] * kk[t] * dec[h]) + dp[h] * (qk[t] * dec[h])
            x = dkd_ref[:, c] * (egl[h] * kk_[t])
            egk = eg[h] * kk_[t]
            z = e + dqd[h] * (eg[h] * qq_[t]) - x + dbw[h] * (b_c[h] * egk)
            zb = dbw[h] * egk + dbu[h] * v_ref[:, c] + dbk[h] * kk_[t]
            sa = jnp.sum(jnp.sum(x[ra], axis=1, keepdims=True), axis=0, keepdims=True)
            sb = jnp.sum(jnp.sum(x[rb], axis=1, keepdims=True), axis=0, keepdims=True)
            la = sa + dgam_ref[0, t, 2 * j:2 * j + 1, :] * gam_ref[0, t, 2 * j:2 * j + 1, :]
            lb = sb + dgam_ref[0, t, 2 * j + 1:2 * j + 2, :] * gam_ref[0, t, 2 * j + 1:2 * j + 2, :]
            hg = nh * hstep + h
            acc[pl.ds(hg, 1), :] = _rsum(z.T - e)
            acc[pl.ds(NV + hg, 1), :] = _rsum(zb.T)
            acc[pl.ds(2 * NV + hg, 1), :] = jnp.where(lane < CH, la, lb)

        @pl.when(hstep == NQ // qps - 1)
        def _():
            drt_ref[...] = acc[...]

    return pl.pallas_call(
        body, name="dn_intra_bwd", grid=(nb, NQ // qps),
        in_specs=[blk, blk, blk2, rspec, blk2, blk2, blk2, gspec, gspec,
                  pl.BlockSpec((nh, 2 * HD, HD), lambda i, h: (h, i, 0)), blk2, blk2, blk2, blk2],
        out_specs=[blk, blk, blk2, rspec],
        out_shape=[jax.ShapeDtypeStruct((s, NQ * HD), F32), jax.ShapeDtypeStruct((s, NQ * HD), F32),
                   jax.ShapeDtypeStruct((s, NV * HD), F32), jax.ShapeDtypeStruct((3 * NV, s), F32)],
        scratch_shapes=[pltpu.VMEM((3 * NV, 2 * CH), F32)],
        compiler_params=_params(("parallel", "arbitrary")),
    )(q, k, v, rt, do, dvn, dkd, gam, dgam, ssave, tinv, u, w, vn)


def _l2n_heads_bwd(c, d_ref, dc_ref, off, scale):
    for hh in range(NQ):
        cs = slice(off + hh * HD, off + (hh + 1) * HD)
        x = c[:, cs]
        dy = d_ref[:, hh * HD:(hh + 1) * HD]
        r = lax.rsqrt(jnp.sum(x * x, axis=-1, keepdims=True) + L2_EPS)
        dc_ref[:, cs] = (scale * r) * (dy - x * (r * r) * jnp.sum(dy * x, axis=-1, keepdims=True))


def _dn_prep_bwd(dq, dk, dv, drt, pqkv, cpre, pab, cw, acol, dcol, dproj):
    s = pqkv.shape[0]
    tm = 128
    nt = s // tm
    wq = NQ * HD

    def body(dq_ref, dk_ref, dv_ref, drt_ref, x_ref, cp_ref, ab_ref, cw_ref, ac_ref,
             dc_ref, _, dx_ref, dab_ref, gcw_ref, gsc_ref, dcs, nxt):
        @pl.when(pl.program_id(0) == 0)
        def _():
            nxt[...] = jnp.zeros_like(nxt)
            gcw_ref[...] = jnp.zeros_like(gcw_ref)
            gsc_ref[...] = jnp.zeros_like(gsc_ref)
        cp = cp_ref[...]
        sg = _sigmoid(cp)
        c = cp * sg
        _l2n_heads_bwd(c, dq_ref, dcs, 0, Q_SCALE)
        _l2n_heads_bwd(c, dk_ref, dcs, wq, 1.0)
        dcs[:, 2 * wq:] = dv_ref[...]
        dcp = dcs[...] * (sg * (1.0 + cp * (1.0 - sg)))
        dx_ref[...] = _conv_causal_bwd(dcp, nxt[...], cw_ref, 4, x_ref[...], gcw_ref).astype(MXU_DT)
        nxt[...] = dcp[:8, :]
        lane = _iota((NV, tm), 1)
        dgt = drt_ref[0:NV, :] + jnp.where((lane & (CH - 1)) == CH - 1, drt_ref[2 * NV:3 * NV, :], 0.0)
        dg = _chunk_cumsum(dgt, 1, rev=True)
        abt = ab_ref[...].T
        zt = abt[0:NV, :] + dc_ref[...]
        gt = -jnp.exp(ac_ref[...]) * _softplus(zt)
        dat = dg * (-jnp.exp(ac_ref[...])) * _sigmoid(zt)
        bt = _sigmoid(abt[NV:2 * NV, :])
        dbt = drt_ref[NV:2 * NV, :] * bt * (1.0 - bt)
        full = jnp.concatenate([dat, dbt, jnp.zeros((128 - 2 * NV, tm), F32)], axis=0)
        dab_ref[...] = full.T.astype(MXU_DT)
        l2 = _iota((NV, 128), 1)
        gsc_ref[...] += jnp.where(l2 == 0, jnp.sum(dg * gt, axis=1, keepdims=True),
                                  jnp.where(l2 == 1, jnp.sum(dat, axis=1, keepdims=True), 0.0))

    return pl.pallas_call(
        body, name="dn_prep_bwd", grid=(nt,),
        in_specs=[_rows(tm, wq, 0, nt), _rows(tm, wq, 0, nt), _rows(tm, 2 * wq, 0, nt),
                  pl.BlockSpec((3 * NV, tm), lambda i: (0, nt - 1 - i)),
                  _rows(tm, 4 * wq, 0, nt), _rows(tm, 4 * wq, 0, nt), _rows(tm, 128, 0, nt),
                  _whole((8, 4 * wq)), _whole((NV, 1)), _whole((NV, 1)), _HBM],
        out_specs=[_rows(tm, 4 * wq, 0, nt), _rows(tm, 128, 0, nt), _whole((8, 4 * wq)), _whole((NV, 128))],
        out_shape=[jax.ShapeDtypeStruct(dproj.shape, MXU_DT), jax.ShapeDtypeStruct((s, 128), MXU_DT),
                   jax.ShapeDtypeStruct((8, 4 * wq), F32), jax.ShapeDtypeStruct((NV, 128), F32)],
        scratch_shapes=[pltpu.VMEM((tm, 4 * wq), F32), pltpu.VMEM((8, 4 * wq), F32)],
        compiler_params=_params(("arbitrary",)), input_output_aliases={10: 0},
    )(dq, dk, dv, drt, pqkv, cpre, pab, cw, acol, dcol, dproj)


def _rg_bwd(drec, px, xcs, h, cw, vec, wa, wx, wat, wxt, side=None):
    s = px.shape[0]
    tm = _pick(s, (256, 128))
    nt = s // tm
    w = D_RNN
    n_in = len(side.ins) if side else 0
    n_out = len(side.out_shapes) if side else 0

    def body(*refs):
        (dr_ref, xr_ref, xc_ref, gr_ref, h_ref, hb_ref, cw_ref, vec_ref, wa_ref, wx_ref, wat_ref,
         wxt_ref) = refs[:12]
        s_in = refs[12:12 + n_in]
        o_ref, gwa_ref, gwx_ref, gcw_ref, gvec_ref = refs[12 + n_in:17 + n_in]
        s_out = refs[17 + n_in:17 + n_in + n_out]
        nxt_a, nxt_l, nxt_d = refs[17 + n_in + n_out:20 + n_in + n_out]
        sems = refs[20 + n_in + n_out:]
        i = pl.program_id(0)
        if side:
            @pl.when(i == 0)
            def _():
                side.start(s_in, s_out, sems)

        @pl.when(i == 0)
        def _():
            nxt_a[...] = jnp.zeros_like(nxt_a)
            nxt_l[...] = jnp.zeros_like(nxt_l)
            nxt_d[...] = jnp.zeros_like(nxt_d)
            gwa_ref[...] = jnp.zeros_like(gwa_ref)
            gwx_ref[...] = jnp.zeros_like(gwx_ref)
            gcw_ref[...] = jnp.zeros_like(gcw_ref)
            gvec_ref[...] = jnp.zeros_like(gvec_ref)
        hbefore = jnp.where(i == nt - 1, 0.0, hb_ref[...])
        xc = xc_ref[...]
        r, ig, a, mult, sp, rmult = _rg_gates(xc, wa_ref, wx_ref, vec_ref)
        hh = h_ref[...]
        gel, dgel = _gelu_and_grad(gr_ref[...])
        drec_ = dr_ref[...]
        o_ref[:, w:] = (drec_ * hh * dgel).astype(MXU_DT)
        lam = _scan_rev(_shift_up(a, nxt_a[...], 1), drec_ * gel, nxt_l[0:1, :])
        nxt_a[...] = a[:8, :]
        nxt_l[...] = lam[:8, :]
        da = lam * _shift_down(hh, hbefore, 1)
        dxc = lam * mult * ig
        dla = da * a - (lam * ig * xc) * (a * a) * rmult
        dpr = dla * (-RG_C * sp) * r * (1.0 - r)
        dpi = (lam * mult * xc) * ig * (1.0 - ig)
        dprb = dpr.astype(MXU_DT)
        dpib = dpi.astype(MXU_DT)
        dxc = dxc + jnp.dot(dprb, wat_ref[...], preferred_element_type=F32) \
                  + jnp.dot(dpib, wxt_ref[...], preferred_element_type=F32)
        xcb = xc.astype(MXU_DT)
        gwa_ref[...] += _dot_tn(xcb, dprb)
        gwx_ref[...] += _dot_tn(xcb, dpib)
        o_ref[:, :w] = _conv_causal_bwd(dxc, nxt_d[...], cw_ref, 4, xr_ref[...], gcw_ref).astype(MXU_DT)
        nxt_d[...] = dxc[:8, :]
        gvec_ref[0:1, :] += _rsum(dxc)
        gvec_ref[1:2, :] += _rsum(dpr)
        gvec_ref[2:3, :] += _rsum(dpi)
        gvec_ref[3:4, :] += _rsum(dla * (-RG_C * r)) * (-_sigmoid(-vec_ref[3:4, :]))
        if side:
            @pl.when(i == nt - 1)
            def _():
                side.finish(s_in, s_out, sems)

    return pl.pallas_call(
        body, name="rg_bwd", grid=(nt,),
        in_specs=[_rows(tm, w, 0, nt), _rows(tm, w, 0, nt), _rows(tm, w, 0, nt), _rows(tm, w, 1, nt),
                  _rows(tm, w, 0, nt), _before(tm, w, 0, nt), _whole((8, w)), _whole((8, w)),
                  _whole((w, w)), _whole((w, w)), _whole((w, w)), _whole((w, w))] + [_HBM] * n_in,
        out_specs=[_rows(tm, 2 * w, 0, nt), _whole((w, w)), _whole((w, w)), _whole((8, w)), _whole((8, w))]
        + [_HBM] * n_out,
        out_shape=[jax.ShapeDtypeStruct((s, 2 * w), MXU_DT), jax.ShapeDtypeStruct((w, w), F32),
                   jax.ShapeDtypeStruct((w, w), F32), jax.ShapeDtypeStruct((8, w), F32),
                   jax.ShapeDtypeStruct((8, w), F32)] + (list(side.out_shapes) if side else []),
        scratch_shapes=[pltpu.VMEM((8, w), F32)] * 3 + (list(side.sems) if side else []),
        compiler_params=_params(("arbitrary",)),
    )(drec, px, xcs, px, h, h, cw, vec, wa, wx, wat, wxt, *(side.ins if side else []))


def _mm_epi(a, b, *, name, epi, extras=(), wholes=(), row_outs=(), acc_outs=(), tm=None, side=None):
    m, kk = a.shape
    n = b.shape[1]
    tm = min(tm, m) if tm else _pick(m, (512, 256, 128))
    tk = kk if kk <= 2816 else _pick(kk, (2816, 2176, 2048, 1024))
    nk, ni = kk // tk, m // tm
    n_ex, n_wh, n_ro, n_ao = len(extras), len(wholes), len(row_outs), len(acc_outs)
    n_si = len(side.ins) if side else 0
    n_so = len(side.out_shapes) if side else 0
    placed = [(j_, r_[2]) for j_, r_ in enumerate(row_outs) if len(r_) == 3]

    def body(*refs):
        a_ref, b_ref = refs[:2]
        p = 2
        ex, p = refs[p:p + n_ex], p + n_ex
        wh, p = refs[p:p + n_wh], p + n_wh
        s_in, p = refs[p:p + n_si], p + n_si
        p += len(placed)
        ro, p = refs[p:p + n_ro], p + n_ro
        ao, p = refs[p:p + n_ao], p + n_ao
        s_out, p = refs[p:p + n_so], p + n_so
        i, k = pl.program_id(0), pl.program_id(1)
        first = (i == 0) & (k == 0)
        if side:
            sems = refs[p + (1 if nk > 1 else 0):]

            @pl.when(first)
            def _():
                side.start(s_in, s_out, sems)
        if n_ao:
            @pl.when(first)
            def _():
                for r_ in ao:
                    r_[...] = jnp.zeros_like(r_)
        if nk == 1:
            epi(_dot(a_ref[...], b_ref[...]), ex, wh, ro, ao)
        else:
            acc = refs[p]

            @pl.when(k == 0)
            def _():
                acc[...] = jnp.zeros_like(acc)
            acc[...] += _dot(a_ref[...], b_ref[...])

            @pl.when(k == nk - 1)
            def _():
                epi(acc[...], ex, wh, ro, ao)
        if side:
            @pl.when((i == ni - 1) & (k == nk - 1))
            def _():
                side.finish(s_in, s_out, sems)

    in_specs = [pl.BlockSpec((tm, tk), lambda i, k: (i, k)), pl.BlockSpec((tk, n), lambda i, k: (k, 0))]
    in_specs += [pl.BlockSpec((tm, w_), functools.partial(lambda i, k, c_: (i, c_), c_=c_)) for _, w_, c_ in extras]
    in_specs += [_whole(x_.shape) for x_ in wholes] + [_HBM] * (n_si + len(placed))
    out_specs, out_shape = [], []
    for r_ in row_outs:
        blk = r_[2][1] if len(r_) == 3 else 0
        out_specs.append(pl.BlockSpec((tm, r_[0]), functools.partial(lambda i, k, c_: (i, c_), c_=blk)))
        out_shape.append(jax.ShapeDtypeStruct(r_[2][0].shape if len(r_) == 3 else (m, r_[0]), r_[1]))
    out_specs += [_whole(sh) for sh in acc_outs] + [_HBM] * n_so
    out_shape += [jax.ShapeDtypeStruct(sh, F32) for sh in acc_outs] + (list(side.out_shapes) if side else [])
    scratch = ([pltpu.VMEM((tm, n), F32)] if nk > 1 else []) + (list(side.sems) if side else [])
    first_placed = 2 + n_ex + n_wh + n_si
    return pl.pallas_call(
        body, name=name, grid=(ni, nk), in_specs=in_specs, out_specs=out_specs, out_shape=out_shape,
        scratch_shapes=scratch, compiler_params=_params(("arbitrary", "arbitrary")),
        input_output_aliases={first_placed + q_: j_ for q_, (j_, _) in enumerate(placed)},
    )(a, b, *[x_ for x_, _, _ in extras], *wholes, *(side.ins if side else []), *[pb_[0] for _, pb_ in placed])


def _mm_pro(b, *, name, pro, m, extras=(), wholes=(), row_outs=(), col_outs=(), tm=256):
    kk, n = b.shape
    tn = _pick(n, (1408, 1024, 512, 256, 128))
    tm = min(tm, m)
    n_ex, n_wh, n_ro, n_co = len(extras), len(wholes), len(row_outs), len(col_outs)

    def body(*refs):
        ex = refs[:n_ex]
        wh = refs[n_ex:n_ex + n_wh]
        b_ref, o_ref = refs[n_ex + n_wh], refs[n_ex + n_wh + 1]
        ro = refs[n_ex + n_wh + 2:n_ex + n_wh + 2 + n_ro]
        co = refs[n_ex + n_wh + 2 + n_ro:n_ex + n_wh + 2 + n_ro + n_co]
        a_scr = refs[-1]

        @pl.when(pl.program_id(1) == 0)
        def _():
            a_scr[...] = pro(ex, wh, ro, co).astype(MXU_DT)
        o_ref[...] = jnp.dot(a_scr[...], b_ref[...], preferred_element_type=F32)

    in_specs = [pl.BlockSpec((tm, w_), functools.partial(lambda i, j, c_: (i, c_), c_=c_)) for _, w_, c_ in extras]
    in_specs += [_whole(x_.shape) for x_ in wholes] + [pl.BlockSpec((kk, tn), lambda i, j: (0, j))]
    out_specs = [pl.BlockSpec((tm, tn), lambda i, j: (i, j))]
    out_specs += [pl.BlockSpec((tm, w_), lambda i, j: (i, 0)) for w_, _ in row_outs]
    out_specs += [pl.BlockSpec((w_, tm), lambda i, j: (0, i)) for w_, _ in col_outs]
    out_shape = [jax.ShapeDtypeStruct((m, n), F32)] + [jax.ShapeDtypeStruct((m, w_), dt) for w_, dt in row_outs]
    out_shape += [jax.ShapeDtypeStruct((w_, m), dt) for w_, dt in col_outs]
    return pl.pallas_call(
        body, name=name, grid=(m // tm, n // tn), in_specs=in_specs, out_specs=out_specs, out_shape=out_shape,
        scratch_shapes=[pltpu.VMEM((tm, kk), MXU_DT)], compiler_params=_params(("arbitrary", "arbitrary")),
    )(*[x_ for x_, _, _ in extras], *wholes, b.astype(MXU_DT))


def _merge_proj_out(pg, ya, yb, w_out):
    def pro(ex, wh, ro, co):
        ga_ref, gb_ref, ya_ref, yb_ref = ex
        mg = _sigmoid(ga_ref[...]) * ya_ref[...] + _sigmoid(gb_ref[...]) * yb_ref[...]
        co[0][...] = mg.T.astype(MXU_DT)
        return mg

    return _mm_pro(w_out, name="merge_proj_out", pro=pro, m=ya.shape[0],
                   extras=[(pg, D, 0), (pg, D, 1), (ya, D, 0), (yb, D, 0)], col_outs=[(D, MXU_DT)])


def _ffn_down_ln2_loss(act, w_down, x2, tgt, adam, lng, lnb):
    def epi(ff_, ex, wh, ro, ao):
        x_ref, t_ref = ex
        ada_ref, g_ref, b_ref = wh
        dff_ref, dx_ref = ro
        red_ref, = ao
        r = ALPHA * x_ref[...] + (1.0 + ada_ref[5:6, :]) * ff_
        xhat, rstd = _ln_stats(r)
        err = xhat * g_ref[...] + b_ref[...] - t_ref[...]
        dy = err * (1.0 / D)
        dr = _ln_bwd(dy, xhat, rstd, g_ref[...])
        dff_ref[...] = ((1.0 + ada_ref[5:6, :]) * dr).astype(MXU_DT)
        dx_ref[...] = ALPHA * dr
        red_ref[0:1, :] += _rsum(dy * xhat)
        red_ref[1:2, :] += _rsum(dy)
        red_ref[2:3, :] += _rsum(dr * ff_)
        red_ref[3:4, :] += jnp.sum(_rsum(err * err), axis=1, keepdims=True) * (0.5 / D)

    return _mm_epi(act, w_down, name="ffn_down_ln2_loss", epi=epi, extras=[(x2, D, 0), (tgt, D, 0)],
                   wholes=[adam, lng, lnb], row_outs=[(D, MXU_DT), (D, F32)], acc_outs=[(8, D)])


def _d_h1_modulate_bwd(dproj, w_t, dxa, x, adam, side):
    def epi(dh, ex, wh, ro, ao):
        dxa_ref, x_ref = ex
        ada_ref, = wh
        ro[0][...] = dxa_ref[...] + dh * (1.0 + ada_ref[1:2, :])
        ao[0][0:1, :] += _rsum(dh * x_ref[...])
        ao[0][1:2, :] += _rsum(dh)

    return _mm_epi(dproj, w_t, name="d_h1", epi=epi, extras=[(dxa, D, 0), (x, D, 0)], wholes=[adam],
                   row_outs=[(D, F32)], acc_outs=[(8, D)], side=side)


def _d_merged_bwd(dmix, w_out_t, pg, ya, yb, dproj):
    def epi(d, ex, wh, ro, ao):
        ga_ref, gb_ref, ya_ref, yb_ref = ex
        dya_ref, dyb_ref, dpg_ref = ro
        sa = _sigmoid(ga_ref[...])
        sb = _sigmoid(gb_ref[...])
        dya_ref[...] = (d * sa).astype(MXU_DT)
        dyb_ref[...] = (d * sb).astype(MXU_DT)
        dpg_ref[:, :D] = (d * ya_ref[...] * sa * (1.0 - sa)).astype(MXU_DT)
        dpg_ref[:, D:] = (d * yb_ref[...] * sb * (1.0 - sb)).astype(MXU_DT)

    return _mm_epi(dmix, w_out_t, name="d_merged", epi=epi,
                   extras=[(pg, D, 0), (pg, D, 1), (ya, D, 0), (yb, D, 0)],
                   row_outs=[(D, MXU_DT), (D, MXU_DT), (2 * D, MXU_DT, (dproj, 3))])


def _d_dn_post_bwd(dyb, w_pb_t, o, pz, nw, dproj):
    def epi(d_all, ex, wh, ro, ao):
        o_ref, z_ref = ex
        nw_ref, = wh
        do_ref, dz_ref = ro
        acc = jnp.zeros((1, HD), F32)
        for h in range(NV):
            cs = slice(h * HD, (h + 1) * HD)
            oh = o_ref[:, cs]
            z = z_ref[:, cs]
            d = d_all[:, cs]
            sg = _sigmoid(z)
            rs = lax.rsqrt(jnp.mean(oh * oh, axis=-1, keepdims=True) + RMS_EPS)
            n = oh * rs
            dz_ref[:, cs] = (d * n * nw_ref[...] * sg * (1.0 + z * (1.0 - sg))).astype(MXU_DT)
            dn_ = d * (z * sg)
            acc = acc + _rsum(dn_ * n)
            dnn = dn_ * nw_ref[...]
            do_ref[:, cs] = rs * (dnn - n * jnp.mean(dnn * n, axis=-1, keepdims=True))
        ao[0][...] += acc

    return _mm_epi(dyb, w_pb_t, name="d_dn", epi=epi, extras=[(o, NV * HD, 0), (pz, NV * HD, 0)], wholes=[nw],
                   row_outs=[(NV * HD, F32), (NV * HD, MXU_DT, (dproj, 2))], acc_outs=[(1, HD)], tm=256)


def _d_h2_ln1_bwd(dgu, w_gu_t, dx2a, x2, r1, mix, adam, lng):
    def epi(dh, ex, wh, ro, ao):
        dxa_ref, x2_ref, r_ref, m_ref = ex
        ada_ref, g_ref = wh
        dm_ref, dx_ref = ro
        red_ref, = ao
        dx2 = dxa_ref[...] + dh * (1.0 + ada_ref[4:5, :])
        xhat, rstd = _ln_stats(r_ref[...])
        dr = _ln_bwd(dx2, xhat, rstd, g_ref[...])
        dm_ref[...] = ((1.0 + ada_ref[2:3, :]) * dr).astype(MXU_DT)
        dx_ref[...] = ALPHA * dr
        red_ref[0:1, :] += _rsum(dh * x2_ref[...])
        red_ref[1:2, :] += _rsum(dh)
        red_ref[2:3, :] += _rsum(dx2 * xhat)
        red_ref[3:4, :] += _rsum(dx2)
        red_ref[4:5, :] += _rsum(dr * m_ref[...])

    return _mm_epi(dgu, w_gu_t, name="d_h2", epi=epi,
                   extras=[(dx2a, D, 0), (x2, D, 0), (r1, D, 0), (mix, D, 0)], wholes=[adam, lng],
                   row_outs=[(D, MXU_DT), (D, F32)], acc_outs=[(8, D)], tm=512)


def _adamw(parts, w, m, v, name):
    r, c = w.shape
    n_parts = len(parts)
    tm = _row_tile(r, c * 4 * (n_parts + 7))
    c1 = 1.0 - ADAM_B1 ** ADAM_STEP
    c2 = 1.0 - ADAM_B2 ** ADAM_STEP

    def body(*refs):
        g = refs[0][...]
        for p_ref in refs[1:n_parts]:
            g = g + p_ref[...]
        w_ref, m_ref, v_ref, g_out, d_out, m_out, v_out = refs[n_parts:]
        mn = ADAM_B1 * m_ref[...] + (1.0 - ADAM_B1) * g
        vn = ADAM_B2 * v_ref[...] + (1.0 - ADAM_B2) * (g * g)
        g_out[...] = g
        m_out[...] = mn
        v_out[...] = vn
        d_out[...] = -ADAM_LR * ((mn / c1) / (jnp.sqrt(vn / c2) + ADAM_EPS) + ADAM_WD * w_ref[...])

    spec = pl.BlockSpec((tm, c), lambda i: (i, 0))
    return pl.pallas_call(
        body, name=name, grid=(r // tm,),
        in_specs=[spec] * (n_parts + 3), out_specs=[spec] * 4,
        out_shape=[jax.ShapeDtypeStruct((r, c), F32)] * 4, compiler_params=_params(("parallel",)),
    )(*parts, w, m, v)


def _row_tile(r, bytes_per_row):
    for t in (512, 256, 128, 64, 32, 16):
        if r % t == 0 and 2 * t * bytes_per_row <= 20 * 1024 * 1024:
            return t
    return _pick(r, (16, 8))


def _sum_partials(own, recv, name):
    r, c = own.shape
    tm = _row_tile(r, c * (4 + 4 + 3 * 2))

    def body(o_ref, r_ref, out_ref):
        out_ref[...] = ((o_ref[...] + r_ref[0].astype(F32)) + r_ref[1].astype(F32)) + r_ref[2].astype(F32)

    return pl.pallas_call(
        body, name=name, grid=(r // tm,),
        in_specs=[pl.BlockSpec((tm, c), lambda i: (i, 0)), pl.BlockSpec((3, tm, c), lambda i: (0, i, 0))],
        out_specs=pl.BlockSpec((tm, c), lambda i: (i, 0)),
        out_shape=jax.ShapeDtypeStruct((r, c), F32), compiler_params=_params(("parallel",)),
    )(own, recv)


def _sum8(g, name):
    _, r, c = g.shape
    tm = _pick(r, (256, 128, 64, 32, 16, 8))

    def body(g_ref, o_ref):
        acc = g_ref[0]
        for k in range(1, 8):
            acc = acc + g_ref[k]
        o_ref[...] = acc

    return pl.pallas_call(
        body, name=name, grid=(r // tm,),
        in_specs=[pl.BlockSpec((8, tm, c), lambda i: (0, i, 0))],
        out_specs=pl.BlockSpec((tm, c), lambda i: (i, 0)),
        out_shape=jax.ShapeDtypeStruct((r, c), F32), compiler_params=_params(("parallel",)),
    )(g)


def _silu_rows(x):
    def body(x_ref, o_ref):
        xx = x_ref[...]
        o_ref[...] = xx * _sigmoid(xx)

    return pl.pallas_call(body, name="silu_rows", out_shape=jax.ShapeDtypeStruct(x.shape, F32))(x)


def _coords():
    return lax.axis_index("x"), lax.axis_index("y"), lax.axis_index("c")


def _allgather8(v, name):
    r, n = v.shape

    def body(v_ref, out_ref, send_sems, recv_sems):
        x, y, c = _coords()
        me = 4 * x + 2 * y + c
        out_ref[me] = v_ref[...]
        peers = []
        for k in range(1, 8):
            px = 1 - x if k & 4 else x
            py = 1 - y if k & 2 else y
            pc = 1 - c if k & 1 else c
            peers.append((px, py, pc))

        def copy(k, slot, to):
            return pltpu.make_async_remote_copy(
                src_ref=v_ref, dst_ref=out_ref.at[slot], send_sem=send_sems.at[k], recv_sem=recv_sems.at[k],
                device_id=to, device_id_type=MESH)

        sends = [copy(k, me, p) for k, p in enumerate(peers)]
        for cp in sends:
            cp.start()
        for k, (px, py, pc) in enumerate(peers):
            copy(k, 4 * px + 2 * py + pc, (px, py, pc)).wait_recv()
        for cp in sends:
            cp.wait_send()

    return pl.pallas_call(
        body, name=name, out_shape=jax.ShapeDtypeStruct((8, r, n), v.dtype),
        in_specs=[pl.BlockSpec(memory_space=pltpu.VMEM)], out_specs=pl.BlockSpec(memory_space=pltpu.VMEM),
        scratch_shapes=[pltpu.SemaphoreType.DMA((7,)), pltpu.SemaphoreType.DMA((7,))],
        compiler_params=pltpu.CompilerParams(vmem_limit_bytes=VMEM_LIMIT),
    )(v)


def _other_chips(x, y):
    return [(1 - x, y), (x, 1 - y), (1 - x, 1 - y)]


_HBM = pl.BlockSpec(memory_space=pl.ANY)


class _Exchange:
    def __init__(self, ins, out_shapes, sems, start, finish):
        self.ins, self.out_shapes, self.sems, self.start, self.finish = ins, out_shapes, sems, start, finish


def _run_exchange(ex, name):
    n_in, n_out = len(ex.ins), len(ex.out_shapes)

    def body(*refs):
        ins, outs, sems = refs[:n_in], refs[n_in:n_in + n_out], refs[n_in + n_out:]
        ex.start(ins, outs, sems)
        ex.finish(ins, outs, sems)

    return pl.pallas_call(body, name=name, out_shape=list(ex.out_shapes), in_specs=[_HBM] * n_in,
                          out_specs=[_HBM] * n_out, scratch_shapes=list(ex.sems))(*ex.ins)


def _gather_exchange(shards):
    n = len(shards)

    def plan(ins, outs, sems):
        send_sems, recv_sems, local_sems = sems
        x, y, c = _coords()
        s_me = 2 * x + y
        chips = _other_chips(x, y)

        def half(i, slot, hc):
            rh = shards[i].shape[0] // 2
            return outs[i].at[slot, pl.ds(pl.multiple_of(hc * rh, 16), rh), :]

        def copy(i, k, src, dst, to):
            return pltpu.make_async_remote_copy(src_ref=src, dst_ref=dst, send_sem=send_sems.at[6 * i + k],
                                                recv_sem=recv_sems.at[6 * i + k], device_id=to, device_id_type=MESH)

        local = [pltpu.make_async_copy(ins[i], outs[i].at[s_me], local_sems.at[i]) for i in range(n)]
        first = []
        for i in range(n):
            rh = shards[i].shape[0] // 2
            my_half = ins[i].at[pl.ds(pl.multiple_of(c * rh, 16), rh), :]
            first += [copy(i, j, my_half, half(i, s_me, c), (px, py, c)) for j, (px, py) in enumerate(chips)]
        return (x, y, c), chips, half, copy, local, first

    def start(ins, outs, sems):
        _, _, _, _, local, first = plan(ins, outs, sems)
        for cp in local + first:
            cp.start()

    def finish(ins, outs, sems):
        (x, y, c), chips, half, copy, local, first = plan(ins, outs, sems)
        sibling = (x, y, 1 - c)
        passed = []
        for i in range(n):
            for j, (px, py) in enumerate(chips):
                land = half(i, 2 * px + py, c)
                copy(i, j, land, land, (px, py, c)).wait_recv()
                fw = copy(i, 3 + j, land, land, sibling)
                fw.start()
                passed.append(fw)
        for i in range(n):
            for j, (px, py) in enumerate(chips):
                land = half(i, 2 * px + py, 1 - c)
                copy(i, 3 + j, land, land, sibling).wait_recv()
        for cp in first + passed:
            cp.wait_send()
        for cp in local:
            cp.wait()

    return _Exchange(list(shards), [jax.ShapeDtypeStruct((4,) + v.shape, v.dtype) for v in shards],
                     [pltpu.SemaphoreType.DMA((6 * n,)), pltpu.SemaphoreType.DMA((6 * n,)),
                      pltpu.SemaphoreType.DMA((n,))], start, finish)


def _scatter_exchange(gs):
    n = len(gs)

    def copies(ins, outs, sems):
        send_sems, recv_sems = sems
        x, y, c = _coords()
        return [pltpu.make_async_remote_copy(
                    src_ref=ins[i].at[2 * px + py], dst_ref=outs[i].at[j], send_sem=send_sems.at[3 * i + j],
                    recv_sem=recv_sems.at[3 * i + j], device_id=(px, py, c), device_id_type=MESH)
                for i in range(n) for j, (px, py) in enumerate(_other_chips(x, y))]

    def start(ins, outs, sems):
        for cp in copies(ins, outs, sems):
            cp.start()

    def finish(ins, outs, sems):
        cps = copies(ins, outs, sems)
        for cp in cps:
            cp.wait_recv()
        for cp in cps:
            cp.wait_send()

    return _Exchange(list(gs), [jax.ShapeDtypeStruct((3,) + g.shape[1:], g.dtype) for g in gs],
                     [pltpu.SemaphoreType.DMA((3 * n,)), pltpu.SemaphoreType.DMA((3 * n,))], start, finish)


def _allgather_exchange(v):
    def copies(ins, outs, sems):
        send_sems, recv_sems, _ = sems
        x, y, c = _coords()
        me = 4 * x + 2 * y + c
        cps = []
        for k in range(1, 8):
            peer = (1 - x if k & 4 else x, 1 - y if k & 2 else y, 1 - c if k & 1 else c)
            cps.append(pltpu.make_async_remote_copy(
                src_ref=ins[0], dst_ref=outs[0].at[me], send_sem=send_sems.at[k - 1], recv_sem=recv_sems.at[k - 1],
                device_id=peer, device_id_type=MESH))
        return me, cps

    def start(ins, outs, sems):
        me, cps = copies(ins, outs, sems)
        pltpu.make_async_copy(ins[0], outs[0].at[me], sems[2]).start()
        for cp in cps:
            cp.start()

    def finish(ins, outs, sems):
        me, cps = copies(ins, outs, sems)
        for cp in cps:
            cp.wait_recv()
        for cp in cps:
            cp.wait_send()
        pltpu.make_async_copy(ins[0], outs[0].at[me], sems[2]).wait()

    return _Exchange([v], [jax.ShapeDtypeStruct((8,) + v.shape, v.dtype)],
                     [pltpu.SemaphoreType.DMA((7,)), pltpu.SemaphoreType.DMA((7,)), pltpu.SemaphoreType.DMA],
                     start, finish)


def _swap_sibling(vs):
    n = len(vs)

    def body(*refs):
        ins, outs = refs[:n], refs[n:2 * n]
        send_sems, recv_sems = refs[2 * n:]
        x, y, c = _coords()
        cps = [pltpu.make_async_remote_copy(src_ref=ins[i], dst_ref=outs[i], send_sem=send_sems.at[i],
                                            recv_sem=recv_sems.at[i], device_id=(x, y, 1 - c), device_id_type=MESH)
               for i in range(n)]
        for cp in cps:
            cp.start()
        for cp in cps:
            cp.wait()

    return pl.pallas_call(
        body, name="swap_sibling", out_shape=[jax.ShapeDtypeStruct(v.shape, v.dtype) for v in vs],
        in_specs=[_HBM] * n, out_specs=[_HBM] * n,
        scratch_shapes=[pltpu.SemaphoreType.DMA((n,)), pltpu.SemaphoreType.DMA((n,))],
    )(*vs)


def _pad_rows(a, rows):
    return jnp.pad(a, ((0, rows - a.shape[0]), (0, 0)))


def _block_diag(w):
    eye = jnp.eye(RG_BLOCKS, dtype=w.dtype)
    return (eye[:, None, :, None] * w[:, :, None, :]).reshape(D_RNN, D_RNN)


def _diag_blocks(g):
    g4 = g.reshape(RG_BLOCKS, RG_BW, RG_BLOCKS, RG_BW)
    idx = jnp.arange(RG_BLOCKS)
    return g4[idx, :, idx, :]


def _prepare_rest(p):
    w = {}
    for k_, n_ in (("pa", "w_proj_a"), ("pb", "w_proj_b"), ("out", "w_out"), ("down", "ffn_w_down")):
        w[k_] = p[n_].astype(MXU_DT)
        w[k_ + "_t"] = w[k_].T
    w["gu"] = jnp.concatenate([p["ffn_w_gate"], p["ffn_w_up"]], axis=1).astype(MXU_DT)
    w["gu_t"] = w["gu"].T
    return w


def _prepare_first(p):
    w = {}
    wi = p["w_in"].astype(MXU_DT)
    cat = jnp.concatenate([wi[:, 2560:6656], wi[:, 6656:8704], wi[:, 8736:10784], wi[:, 0:2560],
                           wi[:, 8704:8736], jnp.zeros((D, 96), MXU_DT)], axis=1)
    w["in_cat"], w["in_cat_t"] = cat, cat.T
    w["rg_cw"] = _pad_rows(p["rg_conv_w"], 8)
    w["dn_cw"] = _pad_rows(p["dn_conv_w"], 8)
    w["ffn_cw"] = _pad_rows(p["ffn_conv_w"], 8)
    w["rg_vec"] = _pad_rows(jnp.stack([p["rg_conv_b"], p["rg_b_a"], p["rg_b_x"], p["rg_lambda"]]), 8)
    w["wa"] = _block_diag(p["rg_w_a"]).astype(MXU_DT)
    w["wx"] = _block_diag(p["rg_w_x"]).astype(MXU_DT)
    w["wa_t"], w["wx_t"] = w["wa"].T, w["wx"].T
    w["arow"] = jnp.pad(p["dn_a_log"], (0, 128 - NV))[None, :]
    w["drow"] = jnp.pad(p["dn_dt_bias"], (0, 128 - NV))[None, :]
    w["acol"] = p["dn_a_log"][:, None]
    w["dcol"] = p["dn_dt_bias"][:, None]
    w["nw"] = p["dn_norm_w"][None, :]
    w["ffn_cb"] = p["ffn_conv_b"][None, :]
    for n_ in ("ln1_g", "ln1_b", "ln2_g", "ln2_b"):
        w[n_] = p[n_][None, :]
    return w


def _mm_sided(a, b, side, **kw):
    if side is None:
        return _mm(a, b, **kw), []
    res = _mm(a, b, side=side, **kw)
    return res[0], res[1:]


_SMALL = ("rg_conv_w", "dn_conv_w", "ffn_conv_w", "rg_conv_b", "rg_w_a", "rg_b_a", "rg_w_x", "rg_b_x",
          "rg_lambda", "dn_a_log", "dn_dt_bias", "dn_norm_w", "ln1_g", "ln1_b", "ffn_conv_b", "ln2_g", "ln2_b")
_REST_A = ("w_proj_a", "w_proj_b", "w_out")
_REST_B = ("ffn_w_gate", "ffn_w_up", "ffn_w_down")


def _local_step(x, tgt, adam, p, shards=None):
    w = _prepare_first(p)
    side_a = side_b = None
    if shards is not None:
        side_a = _gather_exchange([shards[n] for n in _REST_A])
        side_b = _gather_exchange([shards[n] for n in _REST_B])
    h1, h1t = _modulate(x, adam)
    cat = w["in_cat"]
    pqkv, got_b = _mm_sided(h1, cat[:, 0:4096], side_b, name="proj_qkv")
    px, got_a = _mm_sided(h1, cat[:, 8192:10752], side_a, name="proj_x")
    if shards is not None:
        p = dict(p, **{n: _unstack_shards(n, a_) for n, a_ in zip(_REST_A + _REST_B, got_a + got_b)})
    w.update(_prepare_rest(p))
    pz = _mm(h1, cat[:, 4096:6144], name="proj_z")
    pg = _mm(h1, cat[:, 6144:8192], name="proj_g")
    pab = _mm(h1, cat[:, 10752:10880], name="proj_ab")
    hrec, xcs, rec, rec_t = _rg_fwd(px, w["rg_cw"], w["rg_vec"], w["wa"], w["wx"])
    q, k, v, rt, cpre = _dn_prep(pqkv, pab, w["dn_cw"], w["arow"], w["drow"])
    u, ww, qd, kd, pm, tinv, gam = _dn_intra(q, k, v, rt)
    o, vn, ssave, dn, dn_t = _dn_seq(u, ww, qd, kd, pm, gam, pz, w["nw"])
    ya = _mm(rec, w["pa"], name="proj_a")
    yb = _mm(dn, w["pb"], name="proj_b")
    mix, merged_t = _merge_proj_out(pg, ya, yb, w["out"])
    r1, x2, h2, h2_t = _ln1(x, mix, adam, w["ln1_g"], w["ln1_b"])
    gu = _mm(h2, w["gu"], name="ffn_gu")
    act, act_t, gcf = _ffn_act(gu, w["ffn_cw"], w["ffn_cb"])
    dff, dx2a, red2 = _ffn_down_ln2_loss(act, w["down"], x2, tgt, adam, w["ln2_g"], w["ln2_b"])
    g = {}
    dact = _mm(dff, w["down_t"], name="d_act")
    g["ffn_w_down"] = _mm(act_t, dff, name="g_down")
    dgu, gcw_f, gcb_f = _ffn_bwd(dact, gu, gcf, w["ffn_cw"])
    ggu = _mm(h2_t, dgu, name="g_gu")
    g["ffn_w_gate"], g["ffn_w_up"] = ggu[:, :D_FF], ggu[:, D_FF:]
    g["ffn_conv_w"], g["ffn_conv_b"] = gcw_f[0:3], gcb_f[0]
    dmix, dxa, red1 = _d_h2_ln1_bwd(dgu, w["gu_t"], dx2a, x2, r1, mix, adam, w["ln1_g"])
    g["w_out"] = _mm(merged_t, dmix, name="g_out")
    dproj = lax.empty((x.shape[0], N_CAT), MXU_DT)
    dya, dyb, dproj = _d_merged_bwd(dmix, w["out_t"], pg, ya, yb, dproj)
    drec = _mm(dya, w["pa_t"], name="d_rec")
    g["w_proj_a"] = _mm(rec_t, dya, name="g_pa")
    g["w_proj_b"] = _mm(dn_t, dyb, name="g_pb")
    do, dproj, gnw = _d_dn_post_bwd(dyb, w["pb_t"], o, pz, w["nw"], dproj)
    dvn, dkd, dgam = _dn_seq_bwd(do, qd, kd, pm, ww, vn, ssave, gam)
    dq, dk, dv, drt = _dn_intra_bwd(q, k, v, rt, do, dvn, dkd, gam, dgam, ssave, tinv, u, ww, vn)
    dproj, dpab, gcw_d, gsc = _dn_prep_bwd(dq, dk, dv, drt, pqkv, cpre, pab, w["dn_cw"], w["acol"], w["dcol"],
                                           dproj)
    side_r = None
    if shards is not None:
        side_r = _scatter_exchange([_stack_shards(n, g[n]).astype(MXU_DT) for n in _REST_A + _REST_B])
    dpx, gwa, gwx, gcw_r, gvec, *got_r = _rg_bwd(drec, px, xcs, hrec, w["rg_cw"], w["rg_vec"], w["wa"], w["wx"],
                                                 w["wa_t"], w["wx_t"], side_r)
    dproj = lax.dynamic_update_slice(dproj, jnp.concatenate([dpx, dpab], axis=1), (0, 8192))
    reorder = lambda gc: jnp.concatenate([gc[:, 8192:10752], gc[:, 0:4096], gc[:, 4096:6144], gc[:, 10752:10784],
                                          gc[:, 6144:8192]], axis=1)
    wire = lambda gh: _scatter_exchange([_stack_shards("w_in", gh).astype(MXU_DT)]) if shards is not None else None
    g["rg_conv_w"], g["rg_conv_b"] = gcw_r[0:4], gvec[0]
    g["rg_w_a"], g["rg_w_x"] = _diag_blocks(gwa), _diag_blocks(gwx)
    g["rg_b_a"], g["rg_b_x"], g["rg_lambda"] = gvec[1], gvec[2], gvec[3]
    g["dn_conv_w"] = gcw_d[0:4]
    g["dn_a_log"], g["dn_dt_bias"], g["dn_norm_w"] = gsc[:, 0], gsc[:, 1], gnw[0]
    g["ln1_g"], g["ln1_b"] = red1[2], red1[3]
    g["ln2_g"], g["ln2_b"] = red2[0], red2[1]
    side_s = None
    if shards is not None:
        side_s = _allgather_exchange(_pack([jnp.full((1,), red2[3, 0], F32)] + [g[n] for n in _SMALL], 128, 64))
    g_top, got_s = _mm_sided(h1t[:D // 2], dproj, side_s, name="g_in_top")
    g_top = reorder(g_top)
    g_bot, got_top = _mm_sided(h1t[D // 2:], dproj, wire(g_top), name="g_in_bot")
    g_bot = reorder(g_bot)
    gx, red0, *got_bot = _d_h1_modulate_bwd(dproj, w["in_cat_t"], dxa, x, adam, wire(g_bot))
    g["w_in"] = jnp.concatenate([g_top, g_bot], axis=0)
    got = dict(zip(_REST_A + _REST_B, got_r))
    if shards is not None:
        got["w_in"] = jnp.concatenate([got_top[0], got_bot[0]], axis=1)
        got["small"] = got_s[0]
    d_ada = jnp.concatenate([red0[1], red0[0], red1[4], red1[1], red1[0], red2[2]])
    return red2[3, 0], gx, g, d_ada, got


_BIG = ("w_in", "w_proj_a", "w_proj_b", "w_out", "ffn_w_gate", "ffn_w_up", "ffn_w_down")
_COL_SHARDED = ("w_in", "ffn_w_gate", "ffn_w_up")
_CONV = ("rg_conv_w", "dn_conv_w", "ffn_conv_w")
_REPL = ("b_ada", "rg_conv_b", "rg_w_a", "rg_b_a", "rg_w_x", "rg_b_x", "rg_lambda", "dn_a_log",
         "dn_dt_bias", "dn_norm_w", "ln1_g", "ln1_b", "ffn_conv_b", "ln2_g", "ln2_b")
_NAMES = ("w_ada", "b_ada", "w_in", "rg_conv_w", "rg_conv_b", "rg_w_a", "rg_b_a", "rg_w_x", "rg_b_x",
          "rg_lambda", "dn_conv_w", "dn_a_log", "dn_dt_bias", "dn_norm_w", "w_proj_a", "w_proj_b", "w_out",
          "ln1_g", "ln1_b", "ffn_w_gate", "ffn_w_up", "ffn_conv_w", "ffn_conv_b", "ffn_w_down", "ln2_g", "ln2_b")


def _pack(arrs, width, row_mult):
    pieces = []
    for a in arrs:
        f = a.reshape(-1)
        pieces.append(jnp.pad(f, (0, (-f.shape[0]) % (8 * width))).reshape(-1, width))
    rows = sum(p_.shape[0] for p_ in pieces)
    if rows % row_mult:
        pieces.append(jnp.zeros((row_mult - rows % row_mult, width), pieces[0].dtype))
    return jnp.concatenate(pieces, axis=0)


def _unpack(flat, shapes, width):
    out, row = [], 0
    for shp in shapes:
        n = 1
        for d_ in shp:
            n *= d_
        rows = -(-n // (8 * width)) * 8
        out.append(flat[row:row + rows].reshape(-1)[:n].reshape(shp))
        row += rows
    return out


def _stack_shards(name, full):
    if name in _COL_SHARDED or name in _CONV:
        r, ncol = full.shape
        return full.reshape(r, 4, ncol // 4).transpose(1, 0, 2)
    return full.reshape((4, full.shape[0] // 4) + full.shape[1:])


def _unstack_shards(name, st):
    if name in _COL_SHARDED or name in _CONV:
        return st.transpose(1, 0, 2).reshape(st.shape[1], 4 * st.shape[2])
    return st.reshape((4 * st.shape[1],) + st.shape[2:])


def kernel(x, c, w_ada, b_ada, w_in, rg_conv_w, rg_conv_b, rg_w_a, rg_b_a, rg_w_x, rg_b_x, rg_lambda, dn_conv_w, dn_a_log, dn_dt_bias, dn_norm_w, w_proj_a, w_proj_b, w_out, ln1_g, ln1_b, ffn_w_gate, ffn_w_up, ffn_conv_w, ffn_conv_b, ffn_w_down, ln2_g, ln2_b, loss_target, m_w_ada, m_b_ada, m_w_in, m_rg_conv_w, m_rg_conv_b, m_rg_w_a, m_rg_b_a, m_rg_w_x, m_rg_b_x, m_rg_lambda, m_dn_conv_w, m_dn_a_log, m_dn_dt_bias, m_dn_norm_w, m_w_proj_a, m_w_proj_b, m_w_out, m_ln1_g, m_ln1_b, m_ffn_w_gate, m_ffn_w_up, m_ffn_conv_w, m_ffn_conv_b, m_ffn_w_down, m_ln2_g, m_ln2_b, v_w_ada, v_b_ada, v_w_in, v_rg_conv_w, v_rg_conv_b, v_rg_w_a, v_rg_b_a, v_rg_w_x, v_rg_b_x, v_rg_lambda, v_dn_conv_w, v_dn_a_log, v_dn_dt_bias, v_dn_norm_w, v_w_proj_a, v_w_proj_b, v_w_out, v_ln1_g, v_ln1_b, v_ffn_w_gate, v_ffn_w_up, v_ffn_conv_w, v_ffn_conv_b, v_ffn_w_down, v_ln2_g, v_ln2_b):
    args = locals()
    wts = {n: args[n][0] for n in _NAMES}
    mom = {n: args["m_" + n][0] for n in _NAMES}
    var = {n: args["v_" + n][0] for n in _NAMES}
    xs, tgt = x[0], loss_target[0]
    ix, iy, ic = _coords()
    shard = 2 * ix + iy
    batch = 4 * ix + 2 * iy + ic

    c_all = _allgather8(_pad_rows(c, 8), "gather_c")[:, 0, :]
    sc16 = _pad_rows(_silu_rows(c_all), 16)
    ada_cols = _mm(sc16, w_ada[0], name="ada")[:8]
    ada_g = _allgather8(ada_cols, "gather_ada")
    ada_all = jnp.concatenate([ada_g[0], ada_g[2], ada_g[4], ada_g[6]], axis=1) + b_ada
    adam = _pad_rows(lax.dynamic_index_in_dim(ada_all, batch, 0, keepdims=False).reshape(6, D), 8)

    (w_in_all,) = _run_exchange(_gather_exchange([wts["w_in"].astype(MXU_DT)]), "gather_w_in")
    full = {"w_in": _unstack_shards("w_in", w_in_all)}
    conv_shard = _pack([wts[n] for n in _CONV], 128, 8)
    conv_all = _allgather8(conv_shard, "gather_conv")
    shapes_conv = [wts[n].shape for n in _CONV]
    per_shard = [_unpack(conv_all[2 * s], shapes_conv, 128) for s in range(4)]
    for i, n in enumerate(_CONV):
        full[n] = _unstack_shards(n, jnp.stack([per_shard[s][i] for s in range(4)]))
    for n in _REPL:
        full[n] = wts[n]

    shards = {n: wts[n].astype(MXU_DT) for n in _REST_A + _REST_B}
    loss_b, gx, g, d_ada, recv = _local_step(xs, tgt, adam, full, shards)

    parts = []
    for n in _BIG:
        r_ = recv[n]
        axis = 1 if n in _COL_SHARDED else 0
        width = wts[n].shape[axis]
        own = lax.dynamic_slice_in_dim(g[n], shard * width, width, axis=axis)
        parts.append(_sum_partials(own, r_, "sum_" + n))
    parts_sib = _swap_sibling(parts)
    out = {n: [] for n in _NAMES}
    for n, p_, q_ in zip(_BIG, parts, parts_sib):
        out[n] = list(_adamw([p_, q_], wts[n], mom[n], var[n], "adamw_" + n))

    tot = _unpack(_sum8(recv["small"], "sum_small"), [(1,)] + [full[n].shape for n in _SMALL], 128)
    gsum = dict(zip(_SMALL, tot[1:]))
    loss = tot[0][0]
    for n in _CONV:
        gsum[n] = lax.dynamic_index_in_dim(_stack_shards(n, gsum[n]), shard, 0, keepdims=False)
    d_ada_g = _allgather8(d_ada.reshape(6 * D // 128, 128), "gather_d_ada")
    gsum["b_ada"] = _sum8(d_ada_g, "sum_d_ada").reshape(6 * D)
    d_ada_all = d_ada_g.reshape(8, 6 * D)
    cols = lax.dynamic_slice_in_dim(d_ada_all, shard * (6 * D // 4), 6 * D // 4, axis=1)
    g_wada = _mm(sc16, _pad_rows(cols, 16), name="g_ada", trans_a=True)
    res = _adamw([g_wada], wts["w_ada"], mom["w_ada"], var["w_ada"], "adamw_ada")
    out["w_ada"] = list(res)
    names_s = _CONV + _REPL
    shapes_s = [wts[n].shape for n in names_s]
    pk = lambda d_: _pack([d_[n] for n in names_s], 128, 64)
    res_s = _adamw([pk(gsum)], pk(wts), pk(mom), pk(var), "adamw_small")
    for r_ in res_s:
        for n, a in zip(names_s, _unpack(r_, shapes_s, 128)):
            out[n].append(a)

    outs = [loss, gx[None]]
    for i in range(4):
        outs += [out[n][i][None] for n in _NAMES]
    return tuple(outs)
```

```python
import functools

import jax
import jax.numpy as jnp
from jax import lax
from jax.experimental import pallas as pl
from jax.experimental.pallas import tpu as pltpu

F32 = jnp.float32
BF16 = jnp.bfloat16
MXU_DT = BF16

D = 1024
D_RNN = 1280
RG_BLOCKS = 16
RG_BW = 80
RG_C = 8.0
NQ = 8
NV = 16
HD = 128
CH = 64
D_FF = 2816
LN_EPS = 1e-5
RMS_EPS = 1e-6
L2_EPS = 1e-6
ALPHA = 2.0 ** 0.25
Q_SCALE = HD ** -0.5
N_CAT = 10880
VMEM_LIMIT = 56 * 1024 * 1024
MM_VMEM_BUDGET = 36 * 1024 * 1024
SEQ_GROUP = 16
MESH = pl.DeviceIdType.MESH

ADAM_LR, ADAM_B1, ADAM_B2, ADAM_EPS, ADAM_WD, ADAM_STEP = 1e-3, 0.9, 0.999, 1e-8, 0.01, 10


def _sigmoid(x):
    return 0.5 * jnp.tanh(0.5 * x) + 0.5


def _softplus(x):
    return jnp.maximum(x, 0.0) + jnp.log1p(jnp.exp(-jnp.abs(x)))


_GC = 0.7978845608028654


def _gelu(x):
    return 0.5 * x * (1.0 + jnp.tanh(_GC * (x + 0.044715 * x * x * x)))


def _gelu_and_grad(x):
    t = jnp.tanh(_GC * (x + 0.044715 * x * x * x))
    g = 0.5 * x * (1.0 + t)
    dg = 0.5 * (1.0 + t) + 0.5 * x * (1.0 - t * t) * _GC * (1.0 + 3 * 0.044715 * x * x)
    return g, dg


def _neg_expm1(y):
    series = -y * (1.0 + 0.5 * y * (1.0 + y * (1.0 / 3.0)))
    return jnp.where(y > -0.01, series, 1.0 - jnp.exp(y))


def _dot(a, b):
    return jnp.dot(a.astype(MXU_DT), b.astype(MXU_DT), preferred_element_type=F32)


def _dot_nt(a, b):
    return lax.dot_general(a.astype(MXU_DT), b.astype(MXU_DT), (((1,), (1,)), ((), ())),
                           preferred_element_type=F32)


def _dot_tn(a, b):
    return lax.dot_general(a.astype(MXU_DT), b.astype(MXU_DT), (((0,), (0,)), ((), ())),
                           preferred_element_type=F32)


def _split(a):
    hi = a.astype(BF16)
    return hi, (a - hi.astype(F32)).astype(BF16)


def _dot3(a, b, dims=(((1,), (0,)), ((), ()))):
    ah, al = _split(a)
    bh, bl = _split(b)
    d = lambda p, q: lax.dot_general(p, q, dims, preferred_element_type=F32)
    return d(ah, bh) + (d(al, bh) + d(ah, bl))


def _dot3_tn(a, b):
    return _dot3(a, b, (((0,), (0,)), ((), ())))


def _iota(shape, dim):
    return lax.broadcasted_iota(jnp.int32, shape, dim)


def _shift_down(x, before, j):
    if j == 0:
        return x
    xr = pltpu.roll(x, j, 0)
    br = pltpu.roll(before, j, 0)
    top = jnp.where(_iota(br.shape, 0) < j, br, xr[:8])
    return jnp.concatenate([top, xr[8:]], axis=0)


def _shift_up(x, after, j):
    if j == 0:
        return x
    t = x.shape[0]
    xr = pltpu.roll(x, t - j, 0)
    ar = pltpu.roll(after, 8 - j, 0)
    bot = jnp.where(_iota(ar.shape, 0) >= 8 - j, ar, xr[t - 8:])
    return jnp.concatenate([xr[:t - 8], bot], axis=0)


def _taps(x, before, k):
    return [_shift_down(x, before, k - 1 - i) for i in range(k)]


def _conv_taps(taps, w_ref):
    y = w_ref[0:1, :] * taps[0]
    for i in range(1, len(taps)):
        y = y + w_ref[i:i + 1, :] * taps[i]
    return y


def _conv_causal(x, before, w_ref, k):
    return _conv_taps(_taps(x, before, k), w_ref)


def _conv_causal_bwd(dy, after, w_ref, k, x=None, gw_ref=None):
    dx = None
    for i in range(k):
        sh = _shift_up(dy, after, k - 1 - i)
        term = w_ref[i:i + 1, :] * sh
        dx = term if dx is None else dx + term
        if x is not None:
            gw_ref[i:i + 1, :] += _rsum(x * sh)
    return dx


def _scan_fwd(a, u, carry):
    t = a.shape[0]
    pos = _iota(a.shape, 0) & 7
    for d in (1, 2, 4):
        m = pos >= d
        u = u + jnp.where(m, a * pltpu.roll(u, d, 0), 0.0)
        a = jnp.where(m, a * pltpu.roll(a, d, 0), a)
    out = []
    for g in range(t // 8):
        hg = u[8 * g:8 * g + 8] + a[8 * g:8 * g + 8] * carry
        out.append(hg)
        carry = hg[7:8]
    return jnp.concatenate(out, axis=0)


def _scan_rev(a, u, carry):
    t = a.shape[0]
    pos = _iota(a.shape, 0) & 7
    for d in (1, 2, 4):
        m = pos < 8 - d
        u = u + jnp.where(m, a * pltpu.roll(u, t - d, 0), 0.0)
        a = jnp.where(m, a * pltpu.roll(a, t - d, 0), a)
    out = [None] * (t // 8)
    for g in reversed(range(t // 8)):
        lg = u[8 * g:8 * g + 8] + a[8 * g:8 * g + 8] * carry
        out[g] = lg
        carry = lg[0:1]
    return jnp.concatenate(out, axis=0)


def _chunk_cumsum(g, axis, rev=False):
    n = g.shape[axis]
    pos = _iota(g.shape, axis) & (CH - 1)
    d = 1
    while d < CH:
        if rev:
            g = g + jnp.where(pos < CH - d, pltpu.roll(g, n - d, axis), 0.0)
        else:
            g = g + jnp.where(pos >= d, pltpu.roll(g, d, axis), 0.0)
        d *= 2
    return g


def _ln_stats(r):
    mu = jnp.mean(r, axis=-1, keepdims=True)
    xc = r - mu
    var = jnp.mean(xc * xc, axis=-1, keepdims=True)
    rstd = lax.rsqrt(var + LN_EPS)
    return xc * rstd, rstd


def _ln_bwd(dy, xhat, rstd, g):
    dxh = dy * g
    return rstd * (dxh - jnp.mean(dxh, axis=-1, keepdims=True)
                   - xhat * jnp.mean(dxh * xhat, axis=-1, keepdims=True))


def _rsum(x):
    return jnp.sum(x, axis=0, keepdims=True)


def _params(sem):
    return pltpu.CompilerParams(dimension_semantics=sem, vmem_limit_bytes=VMEM_LIMIT)


def _pick(n, cands):
    for c in cands:
        if n % c == 0:
            return c
    return n


def _rows(tm, w, col=0, nt=None):
    if nt is None:
        return pl.BlockSpec((tm, w), lambda i: (i, col))
    return pl.BlockSpec((tm, w), lambda i: (nt - 1 - i, col))


def _before(tm, w, col=0, nt=None):
    r = tm // 8
    if nt is None:
        return pl.BlockSpec((8, w), lambda i: (jnp.maximum(i * r - 1, 0), col))
    return pl.BlockSpec((8, w), lambda i: (jnp.maximum((nt - 1 - i) * r - 1, 0), col))


def _cols(tm, w):
    return pl.BlockSpec((w, tm), lambda i: (0, i))


def _whole(shape):
    return pl.BlockSpec(shape, lambda *_: (0,) * len(shape))


def _mm_plan(a, b, out_dtype):
    m, kk = a.shape
    _, n = b.shape
    tm = _pick(m, (512, 256, 128))
    tk = kk if kk <= 5632 else _pick(kk, (2176, 2048, 1024))
    nk = kk // tk

    def vmem_bytes(tn):
        blocks = tm * tk * a.dtype.itemsize + tk * tn * b.dtype.itemsize + tm * tn * jnp.dtype(out_dtype).itemsize
        return 2 * blocks + (tm * tn * 4 if nk > 1 else 0)

    cands = [t for t in (1408, 1280, 1024, 640, 512, 256, 128) if n % t == 0] or [n]
    tn = next((t for t in cands if vmem_bytes(t) <= MM_VMEM_BUDGET), cands[-1])
    return tm, tn, tk, nk


def _mm_side(a, b, name, out_dtype, side):
    m, _ = a.shape
    _, n = b.shape
    tm, tn, tk, nk = _mm_plan(a, b, out_dtype)
    ni, nj = m // tm, n // tn
    n_in, n_out = len(side.ins), len(side.out_shapes)

    def body(*refs):
        a_ref, b_ref = refs[0], refs[1]
        s_in = refs[2:2 + n_in]
        o_ref = refs[2 + n_in]
        s_out = refs[3 + n_in:3 + n_in + n_out]
        acc = refs[3 + n_in + n_out]
        sems = refs[4 + n_in + n_out:]
        i, j, k = pl.program_id(0), pl.program_id(1), pl.program_id(2)

        @pl.when((i == 0) & (j == 0) & (k == 0))
        def _():
            side.start(s_in, s_out, sems)

        @pl.when(k == 0)
        def _():
            acc[...] = jnp.zeros_like(acc)
        acc[...] += _dot(a_ref[...], b_ref[...])

        @pl.when(k == nk - 1)
        def _():
            o_ref[...] = acc[...].astype(out_dtype)

        @pl.when((i == ni - 1) & (j == nj - 1) & (k == nk - 1))
        def _():
            side.finish(s_in, s_out, sems)

    return pl.pallas_call(
        body, name=name, grid=(ni, nj, nk),
        in_specs=[pl.BlockSpec((tm, tk), lambda i, j, k: (i, k)),
                  pl.BlockSpec((tk, tn), lambda i, j, k: (k, j))] + [_HBM] * n_in,
        out_specs=[pl.BlockSpec((tm, tn), lambda i, j, k: (i, j))] + [_HBM] * n_out,
        out_shape=[jax.ShapeDtypeStruct((m, n), out_dtype)] + list(side.out_shapes),
        scratch_shapes=[pltpu.VMEM((tm, tn), F32)] + list(side.sems),
        compiler_params=_params(("arbitrary", "arbitrary", "arbitrary")),
    )(a, b, *side.ins)


def _mm(a, b, *, name, trans_a=False, out_dtype=F32, side=None):
    if trans_a:
        return _mm(a.T, b, name=name, out_dtype=out_dtype, side=side)
    if side is not None:
        return _mm_side(a, b, name, out_dtype, side)
    m, kk = a.shape
    _, n = b.shape
    tm, tn, tk, nk = _mm_plan(a, b, out_dtype)

    if nk == 1:
        def body(a_ref, b_ref, o_ref):
            o_ref[...] = _dot(a_ref[...], b_ref[...]).astype(out_dtype)
        scratch = []
    else:
        def body(a_ref, b_ref, o_ref, acc):
            k = pl.program_id(2)

            @pl.when(k == 0)
            def _():
                acc[...] = jnp.zeros_like(acc)
            acc[...] += _dot(a_ref[...], b_ref[...])

            @pl.when(k == nk - 1)
            def _():
                o_ref[...] = acc[...].astype(out_dtype)
        scratch = [pltpu.VMEM((tm, tn), F32)]

    return pl.pallas_call(
        body, name=name, grid=(m // tm, n // tn, nk),
        in_specs=[pl.BlockSpec((tm, tk), lambda i, j, k: (i, k)),
                  pl.BlockSpec((tk, tn), lambda i, j, k: (k, j))],
        out_specs=pl.BlockSpec((tm, tn), lambda i, j, k: (i, j)),
        out_shape=jax.ShapeDtypeStruct((m, n), out_dtype),
        scratch_shapes=scratch,
        compiler_params=_params(("parallel", "parallel", "arbitrary")),
    )(a, b)


def _modulate(x, adam):
    s = x.shape[0]
    tm = _pick(s, (512, 256, 128))

    def body(x_ref, ada_ref, o_ref, ot_ref):
        h = x_ref[...] * (1.0 + ada_ref[1:2, :]) + ada_ref[0:1, :]
        o_ref[...] = h.astype(MXU_DT)
        ot_ref[...] = h.T.astype(MXU_DT)

    return pl.pallas_call(
        body, name="modulate1", grid=(s // tm,),
        in_specs=[_rows(tm, D), _whole((8, D))], out_specs=[_rows(tm, D), _cols(tm, D)],
        out_shape=[jax.ShapeDtypeStruct((s, D), MXU_DT), jax.ShapeDtypeStruct((D, s), MXU_DT)],
        compiler_params=_params(("parallel",)),
    )(x, adam)


def _rg_gates(xc, wa_ref, wx_ref, vec_ref):
    xb = xc.astype(MXU_DT)
    r = _sigmoid(jnp.dot(xb, wa_ref[...], preferred_element_type=F32) + vec_ref[1:2, :])
    ig = _sigmoid(jnp.dot(xb, wx_ref[...], preferred_element_type=F32) + vec_ref[2:3, :])
    sp = _softplus(-vec_ref[3:4, :])
    la = -RG_C * r * sp
    a = jnp.exp(la)
    n1 = _neg_expm1(2.0 * la)
    rmult = lax.rsqrt(jnp.maximum(n1, 1e-20))
    return r, ig, a, n1 * rmult, sp, rmult


def _rg_fwd(px, cw, vec, wa, wx):
    s = px.shape[0]
    tm = _pick(s, (256, 128))
    w = D_RNN

    def body(xr_ref, gr_ref, cw_ref, vec_ref, wa_ref, wx_ref, h_ref, xc_ref, rec_ref, rect_ref, prev_x, prev_h):
        @pl.when(pl.program_id(0) == 0)
        def _():
            prev_x[...] = jnp.zeros_like(prev_x)
            prev_h[...] = jnp.zeros_like(prev_h)
        x = xr_ref[...]
        xc = _conv_causal(x, prev_x[...], cw_ref, 4) + vec_ref[0:1, :]
        prev_x[...] = x[tm - 8:, :]
        xc_ref[...] = xc
        _, ig, a, mult, _, _ = _rg_gates(xc, wa_ref, wx_ref, vec_ref)
        h = _scan_fwd(a, mult * ig * xc, prev_h[7:8, :])
        prev_h[...] = h[tm - 8:, :]
        h_ref[...] = h
        rec = h * _gelu(gr_ref[...])
        rec_ref[...] = rec.astype(MXU_DT)
        rect_ref[...] = rec.T.astype(MXU_DT)

    return pl.pallas_call(
        body, name="rg_fwd", grid=(s // tm,),
        in_specs=[_rows(tm, w, 0), _rows(tm, w, 1), _whole((8, w)), _whole((8, w)),
                  _whole((w, w)), _whole((w, w))],
        out_specs=[_rows(tm, w), _rows(tm, w), _rows(tm, w), _cols(tm, w)],
        out_shape=[jax.ShapeDtypeStruct((s, w), F32), jax.ShapeDtypeStruct((s, w), F32),
                   jax.ShapeDtypeStruct((s, w), MXU_DT), jax.ShapeDtypeStruct((w, s), MXU_DT)],
        scratch_shapes=[pltpu.VMEM((8, w), F32), pltpu.VMEM((8, w), F32)],
        compiler_params=_params(("arbitrary",)),
    )(px, px, cw, vec, wa, wx)


def _dn_scalars(ab, arow, drow):
    lane = _iota(ab.shape, 1)
    g = jnp.where(lane < NV, -jnp.exp(arow) * _softplus(ab + drow), 0.0)
    beta = _sigmoid(ab)
    return lane, g, beta


def _l2n_heads(c, out_ref, off, scale):
    for hh in range(NQ):
        x = c[:, off + hh * HD: off + (hh + 1) * HD]
        r = lax.rsqrt(jnp.sum(x * x, axis=-1, keepdims=True) + L2_EPS)
        out_ref[:, hh * HD:(hh + 1) * HD] = x * (r * scale)


def _dn_prep(pqkv, pab, cw, arow, drow):
    s = pqkv.shape[0]
    tm = _pick(s, (256, 128))
    wq = NQ * HD

    def body(x_ref, ab_ref, cw_ref, a_ref, d_ref, q_ref, k_ref, v_ref, rt_ref, cp_ref, prev_x):
        @pl.when(pl.program_id(0) == 0)
        def _():
            prev_x[...] = jnp.zeros_like(prev_x)
        x = x_ref[...]
        cp = _conv_causal(x, prev_x[...], cw_ref, 4)
        prev_x[...] = x[tm - 8:, :]
        cp_ref[...] = cp
        c = cp * _sigmoid(cp)
        _l2n_heads(c, q_ref, 0, Q_SCALE)
        _l2n_heads(c, k_ref, wq, 1.0)
        v_ref[...] = c[:, 2 * wq:]
        lane, g, beta = _dn_scalars(ab_ref[...], a_ref[...], d_ref[...])
        gc = _chunk_cumsum(g, 0)
        gl = gc + _chunk_cumsum(g, 0, rev=True) - g
        pack = jnp.where(lane < NV, gc, jnp.where(lane < 2 * NV, beta,
                         jnp.where(lane < 3 * NV, pltpu.roll(gl, 2 * NV, 1), 0.0)))
        rt_ref[...] = pack.T[0:3 * NV, :]

    return pl.pallas_call(
        body, name="dn_prep", grid=(s // tm,),
        in_specs=[_rows(tm, 4 * wq), _rows(tm, 128), _whole((8, 4 * wq)), _whole((1, 128)), _whole((1, 128))],
        out_specs=[_rows(tm, wq), _rows(tm, wq), _rows(tm, 2 * wq),
                   pl.BlockSpec((3 * NV, tm), lambda i: (0, i)), _rows(tm, 4 * wq)],
        out_shape=[jax.ShapeDtypeStruct((s, wq), F32), jax.ShapeDtypeStruct((s, wq), F32),
                   jax.ShapeDtypeStruct((s, 2 * wq), F32), jax.ShapeDtypeStruct((3 * NV, s), F32),
                   jax.ShapeDtypeStruct((s, 4 * wq), F32)],
        scratch_shapes=[pltpu.VMEM((8, 4 * wq), F32)],
        compiler_params=_params(("arbitrary",)),
    )(pqkv, pab, cw, arow, drow)


def _pair_masks():
    i = _iota((2 * CH, 2 * CH), 0)
    j = _iota((2 * CH, 2 * CH), 1)
    same = (i >> 6) == (j >> 6)
    return same & (i >= j), same & (i > j)


def _head_cols(rt_ref, h):
    shp = (2 * CH, 2 * CH)
    g_r = jnp.broadcast_to(rt_ref[pl.ds(h, 1), :], shp)
    b_r = jnp.broadcast_to(rt_ref[pl.ds(NV + h, 1), :], shp)
    l_r = jnp.broadcast_to(rt_ref[pl.ds(2 * NV + h, 1), :], shp)
    return g_r, g_r.T, b_r.T, l_r, l_r.T


def _inv_unit_lower_many(a_list):
    n = a_list[0].shape[0]
    eye = (_iota((n, n), 0) == _iota((n, n), 1)).astype(F32)
    bs = [-a for a in a_list]
    xs = [eye + b for b in bs]
    for _ in range(5):
        bs = [_dot(b, b) for b in bs]
        xs = [x + _dot(x, b) for x, b in zip(xs, bs)]
    rs = [(eye - x) - _dot3(a, x) for a, x in zip(a_list, xs)]
    return [x + _dot(x, r) for x, r in zip(xs, rs)]


def _gam_rows(l_r):
    lrow = l_r[0:1, :]
    lane = _iota(lrow.shape, 1)
    other = pltpu.roll(lrow, CH, 1)
    return jnp.exp(jnp.where(lane < CH, lrow, other)), jnp.exp(jnp.where(lane >= CH, lrow, other))


def _dn_intra(q, k, v, rt):
    s = q.shape[0]
    nb = s // (2 * CH)
    qps = 8
    blk = pl.BlockSpec((2 * CH, qps * HD), lambda i, h: (i, h))
    blk2 = pl.BlockSpec((2 * CH, 2 * qps * HD), lambda i, h: (i, h))

    def body(q_ref, k_ref, v_ref, rt_ref, u_ref, w_ref, qd_ref, kd_ref, p_ref, ti_ref, gam_ref):
        hstep = pl.program_id(1)
        mc, ms = _pair_masks()
        gam_ref[...] = jnp.zeros_like(gam_ref)
        heads = []
        for qh in range(qps):
            qq = q_ref[:, qh * HD:(qh + 1) * HD]
            kk_ = k_ref[:, qh * HD:(qh + 1) * HD]
            kk = _dot_nt(kk_, kk_)
            qk = _dot_nt(qq, kk_)
            for j in range(2):
                idx = 2 * qh + j
                cs = slice(idx * HD, (idx + 1) * HD)
                g_r, g_c, b_c, l_r, l_c = _head_cols(rt_ref, 2 * qps * hstep + idx)
                dec = jnp.where(mc, jnp.exp(jnp.where(mc, g_c - g_r, 0.0)), 0.0)
                eg = jnp.exp(g_c)
                p_ref[:, cs] = jnp.where(mc, qk * dec, 0.0)
                qd_ref[:, cs] = eg * qq
                kd_ref[:, cs] = jnp.exp(l_c - g_c) * kk_
                ga, gb = _gam_rows(l_r)
                gam_ref[0, qh, 2 * j:2 * j + 1, :] = ga
                gam_ref[0, qh, 2 * j + 1:2 * j + 2, :] = gb
                rhs = jnp.concatenate([b_c * v_ref[:, cs], b_c * eg * kk_], axis=1)
                heads.append((cs, jnp.where(ms, b_c * kk * dec, 0.0), rhs))
        tinvs = _inv_unit_lower_many([a for _, a, _ in heads])
        uws = [_dot3(t, rhs) for t, (_, _, rhs) in zip(tinvs, heads)]
        for t, uw, (cs, _, _) in zip(tinvs, uws, heads):
            ti_ref[:, cs] = t
            u_ref[:, cs] = uw[:, :HD]
            w_ref[:, cs] = uw[:, HD:]

    big = jax.ShapeDtypeStruct((s, NV * HD), F32)
    return pl.pallas_call(
        body, name="dn_intra", grid=(nb, NQ // qps),
        in_specs=[blk, blk, blk2, pl.BlockSpec((3 * NV, 2 * CH), lambda i, h: (0, i))],
        out_specs=[blk2] * 6 + [pl.BlockSpec((1, qps, 8, 128), lambda i, h: (i, h, 0, 0))],
        out_shape=[big] * 6 + [jax.ShapeDtypeStruct((nb, NQ, 8, 128), F32)],
        compiler_params=_params(("parallel", "parallel")),
    )(q, k, v, rt)


def _dn_seq(u, w, qd, kd, p, gam, pz, nw):
    s = u.shape[0]
    nb = s // (2 * CH)
    wide = pl.BlockSpec((2 * CH, NV * HD), lambda i: (i, 0))

    def body(u_ref, w_ref, qd_ref, kd_ref, p_ref, gam_ref, z_ref, nw_ref, o_ref, vn_ref, ss_ref, y_ref, yt_ref, st):
        @pl.when(pl.program_id(0) == 0)
        def _():
            st[...] = jnp.zeros_like(st)
        ra, rb = slice(0, CH), slice(CH, 2 * CH)
        for g0 in range(0, NV, SEQ_GROUP):
            hs = list(range(g0, g0 + SEQ_GROUP))
            cs = [slice(h * HD, (h + 1) * HD) for h in hs]
            ga = [gam_ref[0, h // 2, 2 * (h % 2):2 * (h % 2) + 1, :] for h in hs]
            gb = [gam_ref[0, h // 2, 2 * (h % 2) + 1:2 * (h % 2) + 2, :] for h in hs]
            s0 = [st[h] for h in hs]
            vna = [u_ref[ra, c] - _dot(w_ref[ra, c], s) for c, s in zip(cs, s0)]
            s1 = [g * s + _dot_tn(kd_ref[ra, c], v) for g, s, c, v in zip(ga, s0, cs, vna)]
            vnb = [u_ref[rb, c] - _dot(w_ref[rb, c], s) for c, s in zip(cs, s1)]
            s2 = [g * s + _dot_tn(kd_ref[rb, c], v) for g, s, c, v in zip(gb, s1, cs, vnb)]
            for h, s in zip(hs, s2):
                st[h] = s
            oa = [_dot(qd_ref[ra, c], s) for c, s in zip(cs, s0)]
            ob = [_dot(qd_ref[rb, c], s) for c, s in zip(cs, s1)]
            for i_, h in enumerate(hs):
                vn = jnp.concatenate([vna[i_], vnb[i_]], axis=0)
                oh = jnp.concatenate([oa[i_], ob[i_]], axis=0) + _dot(p_ref[:, cs[i_]], vn)
                o_ref[:, cs[i_]] = oh
                vn_ref[:, cs[i_]] = vn
                ss_ref[h, 0:HD, :] = s0[i_]
                ss_ref[h, HD:2 * HD, :] = s1[i_]
                z = z_ref[:, cs[i_]]
                rs = lax.rsqrt(jnp.mean(oh * oh, axis=-1, keepdims=True) + RMS_EPS)
                y = oh * rs * nw_ref[...] * (z * _sigmoid(z))
                y_ref[:, cs[i_]] = y.astype(MXU_DT)
                yt_ref[cs[i_], :] = y.T.astype(MXU_DT)

    big = jax.ShapeDtypeStruct((s, NV * HD), F32)
    return pl.pallas_call(
        body, name="dn_seq", grid=(nb,),
        in_specs=[wide] * 5 + [pl.BlockSpec((1, NQ, 8, 128), lambda i: (i, 0, 0, 0)), wide, _whole((1, HD))],
        out_specs=[wide, wide, pl.BlockSpec((NV, 2 * HD, HD), lambda i: (0, i, 0)), wide, _cols(2 * CH, NV * HD)],
        out_shape=[big, big, jax.ShapeDtypeStruct((NV, 2 * s, HD), F32),
                   jax.ShapeDtypeStruct((s, NV * HD), MXU_DT), jax.ShapeDtypeStruct((NV * HD, s), MXU_DT)],
        scratch_shapes=[pltpu.VMEM((NV, HD, HD), F32)],
        compiler_params=_params(("arbitrary",)),
    )(u, w, qd, kd, p, gam, pz, nw)


def _ln1(x, mix, adam, lng, lnb):
    s = x.shape[0]
    tm = _pick(s, (512, 256, 128))

    def body(x_ref, m_ref, ada_ref, g_ref, b_ref, r_ref, x2_ref, h2_ref, h2t_ref):
        r = ALPHA * x_ref[...] + (1.0 + ada_ref[2:3, :]) * m_ref[...]
        xhat, _ = _ln_stats(r)
        x2 = xhat * g_ref[...] + b_ref[...]
        r_ref[...] = r
        x2_ref[...] = x2
        h2 = x2 * (1.0 + ada_ref[4:5, :]) + ada_ref[3:4, :]
        h2_ref[...] = h2.astype(MXU_DT)
        h2t_ref[...] = h2.T.astype(MXU_DT)

    return pl.pallas_call(
        body, name="ln1", grid=(s // tm,),
        in_specs=[_rows(tm, D), _rows(tm, D), _whole((8, D)), _whole((1, D)), _whole((1, D))],
        out_specs=[_rows(tm, D)] * 3 + [_cols(tm, D)],
        out_shape=[jax.ShapeDtypeStruct((s, D), F32), jax.ShapeDtypeStruct((s, D), F32),
                   jax.ShapeDtypeStruct((s, D), MXU_DT), jax.ShapeDtypeStruct((D, s), MXU_DT)],
        compiler_params=_params(("parallel",)),
    )(x, mix, adam, lng, lnb)


def _ffn_act(gu, cw, cb):
    s = gu.shape[0]
    tm = _pick(s, (256, 128))
    w = D_FF

    def body(g_ref, u_ref, cw_ref, cb_ref, o_ref, ot_ref, gc_ref, prev):
        @pl.when(pl.program_id(0) == 0)
        def _():
            prev[...] = jnp.zeros_like(prev)
        g = g_ref[...]
        gc = _conv_causal(g, prev[...], cw_ref, 3) + cb_ref[...]
        prev[...] = g[tm - 8:, :]
        gc_ref[...] = gc
        act = _gelu(gc) * u_ref[...]
        o_ref[...] = act.astype(MXU_DT)
        ot_ref[...] = act.T.astype(MXU_DT)

    return pl.pallas_call(
        body, name="ffn_act", grid=(s // tm,),
        in_specs=[_rows(tm, w, 0), _rows(tm, w, 1), _whole((8, w)), _whole((1, w))],
        out_specs=[_rows(tm, w), _cols(tm, w), _rows(tm, w)],
        out_shape=[jax.ShapeDtypeStruct((s, w), MXU_DT), jax.ShapeDtypeStruct((w, s), MXU_DT),
                   jax.ShapeDtypeStruct((s, w), F32)],
        scratch_shapes=[pltpu.VMEM((8, w), F32)],
        compiler_params=_params(("arbitrary",)),
    )(gu, gu, cw, cb)


def _ffn_bwd(dact, gu, gc, cw):
    s = dact.shape[0]
    tm = _pick(s, (256, 128))
    nt = s // tm
    w = D_FF

    def body(da_ref, g_ref, u_ref, gc_ref, cw_ref, o_ref, gcw_ref, gcb_ref, nxt):
        @pl.when(pl.program_id(0) == 0)
        def _():
            nxt[...] = jnp.zeros_like(nxt)
            gcw_ref[...] = jnp.zeros_like(gcw_ref)
            gcb_ref[...] = jnp.zeros_like(gcb_ref)
        gel, dgel = _gelu_and_grad(gc_ref[...])
        da = da_ref[...]
        dgc = da * u_ref[...] * dgel
        o_ref[:, w:] = (da * gel).astype(MXU_DT)
        o_ref[:, :w] = _conv_causal_bwd(dgc, nxt[...], cw_ref, 3, g_ref[...], gcw_ref).astype(MXU_DT)
        nxt[...] = dgc[:8, :]
        gcb_ref[...] += _rsum(dgc)

    return pl.pallas_call(
        body, name="ffn_bwd", grid=(nt,),
        in_specs=[_rows(tm, w, 0, nt), _rows(tm, w, 0, nt), _rows(tm, w, 1, nt), _rows(tm, w, 0, nt),
                  _whole((8, w))],
        out_specs=[_rows(tm, 2 * w, 0, nt), _whole((8, w)), _whole((1, w))],
        out_shape=[jax.ShapeDtypeStruct((s, 2 * w), MXU_DT), jax.ShapeDtypeStruct((8, w), F32),
                   jax.ShapeDtypeStruct((1, w), F32)],
        scratch_shapes=[pltpu.VMEM((8, w), F32)],
        compiler_params=_params(("arbitrary",)),
    )(dact, gu, gu, gc, cw)


def _dn_seq_bwd(do, qd, kd, p, w, vn, ssave, gam):
    s = do.shape[0]
    nb = s // (2 * CH)
    wide = pl.BlockSpec((2 * CH, NV * HD), lambda i: (nb - 1 - i, 0))
    gspec = pl.BlockSpec((1, NQ, 8, 128), lambda i: (nb - 1 - i, 0, 0, 0))

    def body(do_ref, qd_ref, kd_ref, p_ref, w_ref, vn_ref, ss_ref, gam_ref, dvn_ref, dkd_ref, dgam_ref, dst):
        @pl.when(pl.program_id(0) == 0)
        def _():
            dst[...] = jnp.zeros_like(dst)
        dgam_ref[...] = jnp.zeros_like(dgam_ref)
        ra, rb = slice(0, CH), slice(CH, 2 * CH)
        tot = lambda t: jnp.sum(jnp.sum(t, axis=1, keepdims=True), axis=0, keepdims=True)
        for g0 in range(0, NV, SEQ_GROUP):
            hs = list(range(g0, g0 + SEQ_GROUP))
            cs = [slice(h * HD, (h + 1) * HD) for h in hs]
            ga = [gam_ref[0, h // 2, 2 * (h % 2):2 * (h % 2) + 1, :] for h in hs]
            gb = [gam_ref[0, h // 2, 2 * (h % 2) + 1:2 * (h % 2) + 2, :] for h in hs]
            ds2 = [dst[h] for h in hs]
            pdo = [_dot_tn(p_ref[:, c], do_ref[:, c]) for c in cs]
            qdo_b = [_dot_tn(qd_ref[rb, c], do_ref[rb, c]) for c in cs]
            qdo_a = [_dot_tn(qd_ref[ra, c], do_ref[ra, c]) for c in cs]
            dvb = [p_[rb] + _dot(kd_ref[rb, c], d_) for p_, c, d_ in zip(pdo, cs, ds2)]
            ds1 = [g * d_ + q_ - _dot_tn(w_ref[rb, c], v_)
                   for g, d_, q_, c, v_ in zip(gb, ds2, qdo_b, cs, dvb)]
            dva = [p_[ra] + _dot(kd_ref[ra, c], d_) for p_, c, d_ in zip(pdo, cs, ds1)]
            ds0 = [g * d_ + q_ - _dot_tn(w_ref[ra, c], v_)
                   for g, d_, q_, c, v_ in zip(ga, ds1, qdo_a, cs, dva)]
            for h, d_ in zip(hs, ds0):
                dst[h] = d_
            for i_, h in enumerate(hs):
                c = cs[i_]
                row = 2 * (h % 2)
                dkd_ref[rb, c] = _dot_nt(vn_ref[rb, c], ds2[i_])
                dkd_ref[ra, c] = _dot_nt(vn_ref[ra, c], ds1[i_])
                dvn_ref[ra, c] = dva[i_]
                dvn_ref[rb, c] = dvb[i_]
                dgam_ref[0, h // 2, row:row + 1, :] = jnp.broadcast_to(tot(ds1[i_] * ss_ref[h, 0:HD, :]), (1, 128))
                dgam_ref[0, h // 2, row + 1:row + 2, :] = jnp.broadcast_to(
                    tot(ds2[i_] * ss_ref[h, HD:2 * HD, :]), (1, 128))

    big = jax.ShapeDtypeStruct((s, NV * HD), F32)
    return pl.pallas_call(
        body, name="dn_seq_bwd", grid=(nb,),
        in_specs=[wide] * 6 + [pl.BlockSpec((NV, 2 * HD, HD), lambda i: (0, nb - 1 - i, 0)), gspec],
        out_specs=[wide, wide, gspec],
        out_shape=[big, big, jax.ShapeDtypeStruct((nb, NQ, 8, 128), F32)],
        scratch_shapes=[pltpu.VMEM((NV, HD, HD), F32)],
        compiler_params=_params(("arbitrary",)),
    )(do, qd, kd, p, w, vn, ssave, gam)


def _dn_intra_bwd(q, k, v, rt, do, dvn, dkd, gam, dgam, ssave, tinv, u, w, vn):
    s = q.shape[0]
    nb = s // (2 * CH)
    qps = 8
    nh = 2 * qps
    blk = pl.BlockSpec((2 * CH, qps * HD), lambda i, h: (i, h))
    blk2 = pl.BlockSpec((2 * CH, nh * HD), lambda i, h: (i, h))
    gspec = pl.BlockSpec((1, qps, 8, 128), lambda i, h: (i, h, 0, 0))
    rspec = pl.BlockSpec((3 * NV, 2 * CH), lambda i, h: (0, i))

    def body(q_ref, k_ref, v_ref, rt_ref, do_ref, dvn_ref, dkd_ref, gam_ref, dgam_ref, ss_ref,
             ti_ref, u_ref, w_ref, vn_ref, dq_ref, dk_ref, dv_ref, drt_ref, acc):
        hstep = pl.program_id(1)

        @pl.when(hstep == 0)
        def _():
            acc[...] = jnp.zeros_like(acc)
        mc, ms = _pair_masks()
        ra, rb = slice(0, CH), slice(CH, 2 * CH)
        lane = _iota((1, 2 * CH), 1)
        hs = list(range(nh))
        qh = [h // 2 for h in hs]
        cs = [slice(h * HD, (h + 1) * HD) for h in hs]
        qq_ = [q_ref[:, t * HD:(t + 1) * HD] for t in range(qps)]
        kk_ = [k_ref[:, t * HD:(t + 1) * HD] for t in range(qps)]
        kk = [_dot_nt(k_, k_) for k_ in kk_]
        qk = [_dot_nt(q_, k_) for q_, k_ in zip(qq_, kk_)]
        cols = [_head_cols(rt_ref, nh * hstep + h) for h in hs]
        b_c = [c_[2] for c_ in cols]
        dec = [jnp.where(mc, jnp.exp(jnp.where(mc, c_[1] - c_[0], 0.0)), 0.0) for c_ in cols]
        eg = [jnp.exp(c_[1]) for c_ in cols]
        egl = [jnp.exp(c_[4] - c_[1]) for c_ in cols]
        dob = [do_ref[:, c] for c in cs]
        dvb = [dvn_ref[:, c] for c in cs]
        dqd = [jnp.concatenate([_dot_nt(d_[ra], ss_ref[h, 0:HD, :]), _dot_nt(d_[rb], ss_ref[h, HD:2 * HD, :])], axis=0)
               for h, d_ in zip(hs, dob)]
        dw = [-jnp.concatenate([_dot_nt(d_[ra], ss_ref[h, 0:HD, :]), _dot_nt(d_[rb], ss_ref[h, HD:2 * HD, :])], axis=0)
              for h, d_ in zip(hs, dvb)]
        dp = [jnp.where(mc, _dot_nt(d_, vn_ref[:, c]), 0.0) for d_, c in zip(dob, cs)]
        dbuw = [_dot3_tn(ti_ref[:, c], jnp.concatenate([d_, w_], axis=1)) for c, d_, w_ in zip(cs, dvb, dw)]
        dbu = [t[:, :HD] for t in dbuw]
        dbw = [t[:, HD:] for t in dbuw]
        da = [jnp.where(ms, -(_dot_nt(bu, u_ref[:, c]) + _dot_nt(bw, w_ref[:, c])), 0.0)
              for bu, bw, c in zip(dbu, dbw, cs)]
        dm = [a_ * d_ for a_, d_ in zip(da, dec)]
        dn_ = [p_ * d_ for p_, d_ in zip(dp, dec)]
        dbk = [_dot(m_, kk_[t]) for m_, t in zip(dm, qh)]
        dqs = [_dot(n_, kk_[t]) + e_ * q_ for n_, t, e_, q_ in zip(dn_, qh, eg, dqd)]
        dks = [_dot_tn(m_, b_ * kk_[t]) + _dot_tn(n_, qq_[t]) + el * dkd_ref[:, c] + b_ * (e_ * bw + bk)
               for m_, b_, t, n_, el, c, e_, bw, bk in zip(dm, b_c, qh, dn_, egl, cs, eg, dbw, dbk)]
        for t in range(qps):
            dq_ref[:, t * HD:(t + 1) * HD] = dqs[2 * t] + dqs[2 * t + 1]
            dk_ref[:, t * HD:(t + 1) * HD] = dks[2 * t] + dks[2 * t + 1]
        for h in hs:
            c, t, j = cs[h], qh[h], h % 2
            dv_ref[:, c] = b_c[h] * dbu[h]
            e = da[h] * (b_c[h] * kk[t] * dec[h]) + dp[h] * (qk[t] * dec[h])
            x = dkd_ref[:, c] * (egl[h] * kk_[t])
            egk = eg[h] * kk_[t]
            z = e + dqd[h] * (eg[h] * qq_[t]) - x + dbw[h] * (b_c[h] * egk)
            zb = dbw[h] * egk + dbu[h] * v_ref[:, c] + dbk[h] * kk_[t]
            sa = jnp.sum(jnp.sum(x[ra], axis=1, keepdims=True), axis=0, keepdims=True)
            sb = jnp.sum(jnp.sum(x[rb], axis=1, keepdims=True), axis=0, keepdims=True)
            la = sa + dgam_ref[0, t, 2 * j:2 * j + 1, :] * gam_ref[0, t, 2 * j:2 * j + 1, :]
            lb = sb + dgam_ref[0, t, 2 * j + 1:2 * j + 2, :] * gam_ref[0, t, 2 * j + 1:2 * j + 2, :]
            hg = nh * hstep + h
            acc[pl.ds(hg, 1), :] = _rsum(z.T - e)
            acc[pl.ds(NV + hg, 1), :] = _rsum(zb.T)
            acc[pl.ds(2 * NV + hg, 1), :] = jnp.where(lane < CH, la, lb)

        @pl.when(hstep == NQ // qps - 1)
        def _():
            drt_ref[...] = acc[...]

    return pl.pallas_call(
        body, name="dn_intra_bwd", grid=(nb, NQ // qps),
        in_specs=[blk, blk, blk2, rspec, blk2, blk2, blk2, gspec, gspec,
                  pl.BlockSpec((nh, 2 * HD, HD), lambda i, h: (h, i, 0)), blk2, blk2, blk2, blk2],
        out_specs=[blk, blk, blk2, rspec],
        out_shape=[jax.ShapeDtypeStruct((s, NQ * HD), F32), jax.ShapeDtypeStruct((s, NQ * HD), F32),
                   jax.ShapeDtypeStruct((s, NV * HD), F32), jax.ShapeDtypeStruct((3 * NV, s), F32)],
        scratch_shapes=[pltpu.VMEM((3 * NV, 2 * CH), F32)],
        compiler_params=_params(("parallel", "arbitrary")),
    )(q, k, v, rt, do, dvn, dkd, gam, dgam, ssave, tinv, u, w, vn)


def _l2n_heads_bwd(c, d_ref, dc_ref, off, scale):
    for hh in range(NQ):
        cs = slice(off + hh * HD, off + (hh + 1) * HD)
        x = c[:, cs]
        dy = d_ref[:, hh * HD:(hh + 1) * HD]
        r = lax.rsqrt(jnp.sum(x * x, axis=-1, keepdims=True) + L2_EPS)
        dc_ref[:, cs] = (scale * r) * (dy - x * (r * r) * jnp.sum(dy * x, axis=-1, keepdims=True))


def _dn_prep_bwd(dq, dk, dv, drt, pqkv, cpre, pab, cw, acol, dcol, dproj):
    s = pqkv.shape[0]
    tm = 128
    nt = s // tm
    wq = NQ * HD

    def body(dq_ref, dk_ref, dv_ref, drt_ref, x_ref, cp_ref, ab_ref, cw_ref, ac_ref,
             dc_ref, _, dx_ref, dab_ref, gcw_ref, gsc_ref, dcs, nxt):
        @pl.when(pl.program_id(0) == 0)
        def _():
            nxt[...] = jnp.zeros_like(nxt)
            gcw_ref[...] = jnp.zeros_like(gcw_ref)
            gsc_ref[...] = jnp.zeros_like(gsc_ref)
        cp = cp_ref[...]
        sg = _sigmoid(cp)
        c = cp * sg
        _l2n_heads_bwd(c, dq_ref, dcs, 0, Q_SCALE)
        _l2n_heads_bwd(c, dk_ref, dcs, wq, 1.0)
        dcs[:, 2 * wq:] = dv_ref[...]
        dcp = dcs[...] * (sg * (1.0 + cp * (1.0 - sg)))
        dx_ref[...] = _conv_causal_bwd(dcp, nxt[...], cw_ref, 4, x_ref[...], gcw_ref).astype(MXU_DT)
        nxt[...] = dcp[:8, :]
        lane = _iota((NV, tm), 1)
        dgt = drt_ref[0:NV, :] + jnp.where((lane & (CH - 1)) == CH - 1, drt_ref[2 * NV:3 * NV, :], 0.0)
        dg = _chunk_cumsum(dgt, 1, rev=True)
        abt = ab_ref[...].T
        zt = abt[0:NV, :] + dc_ref[...]
        gt = -jnp.exp(ac_ref[...]) * _softplus(zt)
        dat = dg * (-jnp.exp(ac_ref[...])) * _sigmoid(zt)
        bt = _sigmoid(abt[NV:2 * NV, :])
        dbt = drt_ref[NV:2 * NV, :] * bt * (1.0 - bt)
        full = jnp.concatenate([dat, dbt, jnp.zeros((128 - 2 * NV, tm), F32)], axis=0)
        dab_ref[...] = full.T.astype(MXU_DT)
        l2 = _iota((NV, 128), 1)
        gsc_ref[...] += jnp.where(l2 == 0, jnp.sum(dg * gt, axis=1, keepdims=True),
                                  jnp.where(l2 == 1, jnp.sum(dat, axis=1, keepdims=True), 0.0))

    return pl.pallas_call(
        body, name="dn_prep_bwd", grid=(nt,),
        in_specs=[_rows(tm, wq, 0, nt), _rows(tm, wq, 0, nt), _rows(tm, 2 * wq, 0, nt),
                  pl.BlockSpec((3 * NV, tm), lambda i: (0, nt - 1 - i)),
                  _rows(tm, 4 * wq, 0, nt), _rows(tm, 4 * wq, 0, nt), _rows(tm, 128, 0, nt),
                  _whole((8, 4 * wq)), _whole((NV, 1)), _whole((NV, 1)), _HBM],
        out_specs=[_rows(tm, 4 * wq, 0, nt), _rows(tm, 128, 0, nt), _whole((8, 4 * wq)), _whole((NV, 128))],
        out_shape=[jax.ShapeDtypeStruct(dproj.shape, MXU_DT), jax.ShapeDtypeStruct((s, 128), MXU_DT),
                   jax.ShapeDtypeStruct((8, 4 * wq), F32), jax.ShapeDtypeStruct((NV, 128), F32)],
        scratch_shapes=[pltpu.VMEM((tm, 4 * wq), F32), pltpu.VMEM((8, 4 * wq), F32)],
        compiler_params=_params(("arbitrary",)), input_output_aliases={10: 0},
    )(dq, dk, dv, drt, pqkv, cpre, pab, cw, acol, dcol, dproj)


def _rg_bwd(drec, px, xcs, h, cw, vec, wa, wx, wat, wxt, side=None):
    s = px.shape[0]
    tm = _pick(s, (256, 128))
    nt = s // tm
    w = D_RNN
    n_in = len(side.ins) if side else 0
    n_out = len(side.out_shapes) if side else 0

    def body(*refs):
        (dr_ref, xr_ref, xc_ref, gr_ref, h_ref, hb_ref, cw_ref, vec_ref, wa_ref, wx_ref, wat_ref,
         wxt_ref) = refs[:12]
        s_in = refs[12:12 + n_in]
        o_ref, gwa_ref, gwx_ref, gcw_ref, gvec_ref = refs[12 + n_in:17 + n_in]
        s_out = refs[17 + n_in:17 + n_in + n_out]
        nxt_a, nxt_l, nxt_d = refs[17 + n_in + n_out:20 + n_in + n_out]
        sems = refs[20 + n_in + n_out:]
        i = pl.program_id(0)
        if side:
            @pl.when(i == 0)
            def _():
                side.start(s_in, s_out, sems)

        @pl.when(i == 0)
        def _():
            nxt_a[...] = jnp.zeros_like(nxt_a)
            nxt_l[...] = jnp.zeros_like(nxt_l)
            nxt_d[...] = jnp.zeros_like(nxt_d)
            gwa_ref[...] = jnp.zeros_like(gwa_ref)
            gwx_ref[...] = jnp.zeros_like(gwx_ref)
            gcw_ref[...] = jnp.zeros_like(gcw_ref)
            gvec_ref[...] = jnp.zeros_like(gvec_ref)
        hbefore = jnp.where(i == nt - 1, 0.0, hb_ref[...])
        xc = xc_ref[...]
        r, ig, a, mult, sp, rmult = _rg_gates(xc, wa_ref, wx_ref, vec_ref)
        hh = h_ref[...]
        gel, dgel = _gelu_and_grad(gr_ref[...])
        drec_ = dr_ref[...]
        o_ref[:, w:] = (drec_ * hh * dgel).astype(MXU_DT)
        lam = _scan_rev(_shift_up(a, nxt_a[...], 1), drec_ * gel, nxt_l[0:1, :])
        nxt_a[...] = a[:8, :]
        nxt_l[...] = lam[:8, :]
        da = lam * _shift_down(hh, hbefore, 1)
        dxc = lam * mult * ig
        dla = da * a - (lam * ig * xc) * (a * a) * rmult
        dpr = dla * (-RG_C * sp) * r * (1.0 - r)
        dpi = (lam * mult * xc) * ig * (1.0 - ig)
        dprb = dpr.astype(MXU_DT)
        dpib = dpi.astype(MXU_DT)
        dxc = dxc + jnp.dot(dprb, wat_ref[...], preferred_element_type=F32) \
                  + jnp.dot(dpib, wxt_ref[...], preferred_element_type=F32)
        xcb = xc.astype(MXU_DT)
        for rb_ in range(w // 128):
            lo, hi = max(0, 128 * (rb_ - 1)), min(w, 128 * (rb_ + 2))
            rows_ = slice(128 * rb_, 128 * (rb_ + 1))
            gwa_ref[rows_, lo:hi] += _dot_tn(xcb[:, rows_], dprb[:, lo:hi])
            gwx_ref[rows_, lo:hi] += _dot_tn(xcb[:, rows_], dpib[:, lo:hi])
        o_ref[:, :w] = _conv_causal_bwd(dxc, nxt_d[...], cw_ref, 4, xr_ref[...], gcw_ref).astype(MXU_DT)
        nxt_d[...] = dxc[:8, :]
        gvec_ref[0:1, :] += _rsum(dxc)
        gvec_ref[1:2, :] += _rsum(dpr)
        gvec_ref[2:3, :] += _rsum(dpi)
        gvec_ref[3:4, :] += _rsum(dla * (-RG_C * r)) * (-_sigmoid(-vec_ref[3:4, :]))
        if side:
            @pl.when(i == nt - 1)
            def _():
                side.finish(s_in, s_out, sems)

    return pl.pallas_call(
        body, name="rg_bwd", grid=(nt,),
        in_specs=[_rows(tm, w, 0, nt), _rows(tm, w, 0, nt), _rows(tm, w, 0, nt), _rows(tm, w, 1, nt),
                  _rows(tm, w, 0, nt), _before(tm, w, 0, nt), _whole((8, w)), _whole((8, w)),
                  _whole((w, w)), _whole((w, w)), _whole((w, w)), _whole((w, w))] + [_HBM] * n_in,
        out_specs=[_rows(tm, 2 * w, 0, nt), _whole((w, w)), _whole((w, w)), _whole((8, w)), _whole((8, w))]
        + [_HBM] * n_out,
        out_shape=[jax.ShapeDtypeStruct((s, 2 * w), MXU_DT), jax.ShapeDtypeStruct((w, w), F32),
                   jax.ShapeDtypeStruct((w, w), F32), jax.ShapeDtypeStruct((8, w), F32),
                   jax.ShapeDtypeStruct((8, w), F32)] + (list(side.out_shapes) if side else []),
        scratch_shapes=[pltpu.VMEM((8, w), F32)] * 3 + (list(side.sems) if side else []),
        compiler_params=_params(("arbitrary",)),
    )(drec, px, xcs, px, h, h, cw, vec, wa, wx, wat, wxt, *(side.ins if side else []))


def _mm_epi(a, b, *, name, epi, extras=(), wholes=(), row_outs=(), acc_outs=(), tm=None, side=None):
    m, kk = a.shape
    n = b.shape[1]
    tm = min(tm, m) if tm else _pick(m, (512, 256, 128))
    tk = kk if kk <= 2816 else _pick(kk, (2816, 2176, 2048, 1024))
    nk, ni = kk // tk, m // tm
    n_ex, n_wh, n_ro, n_ao = len(extras), len(wholes), len(row_outs), len(acc_outs)
    n_si = len(side.ins) if side else 0
    n_so = len(side.out_shapes) if side else 0
    placed = [(j_, r_[2]) for j_, r_ in enumerate(row_outs) if len(r_) == 3]

    def body(*refs):
        a_ref, b_ref = refs[:2]
        p = 2
        ex, p = refs[p:p + n_ex], p + n_ex
        wh, p = refs[p:p + n_wh], p + n_wh
        s_in, p = refs[p:p + n_si], p + n_si
        p += len(placed)
        ro, p = refs[p:p + n_ro], p + n_ro
        ao, p = refs[p:p + n_ao], p + n_ao
        s_out, p = refs[p:p + n_so], p + n_so
        i, k = pl.program_id(0), pl.program_id(1)
        first = (i == 0) & (k == 0)
        if side:
            sems = refs[p + (1 if nk > 1 else 0):]

            @pl.when(first)
            def _():
                side.start(s_in, s_out, sems)
        if n_ao:
            @pl.when(first)
            def _():
                for r_ in ao:
                    r_[...] = jnp.zeros_like(r_)
        if nk == 1:
            epi(_dot(a_ref[...], b_ref[...]), ex, wh, ro, ao)
        else:
            acc = refs[p]

            @pl.when(k == 0)
            def _():
                acc[...] = jnp.zeros_like(acc)
            acc[...] += _dot(a_ref[...], b_ref[...])

            @pl.when(k == nk - 1)
            def _():
                epi(acc[...], ex, wh, ro, ao)
        if side:
            @pl.when((i == ni - 1) & (k == nk - 1))
            def _():
                side.finish(s_in, s_out, sems)

    in_specs = [pl.BlockSpec((tm, tk), lambda i, k: (i, k)), pl.BlockSpec((tk, n), lambda i, k: (k, 0))]
    in_specs += [pl.BlockSpec((tm, w_), functools.partial(lambda i, k, c_: (i, c_), c_=c_)) for _, w_, c_ in extras]
    in_specs += [_whole(x_.shape) for x_ in wholes] + [_HBM] * (n_si + len(placed))
    out_specs, out_shape = [], []
    for r_ in row_outs:
        blk = r_[2][1] if len(r_) == 3 else 0
        out_specs.append(pl.BlockSpec((tm, r_[0]), functools.partial(lambda i, k, c_: (i, c_), c_=blk)))
        out_shape.append(jax.ShapeDtypeStruct(r_[2][0].shape if len(r_) == 3 else (m, r_[0]), r_[1]))
    out_specs += [_whole(sh) for sh in acc_outs] + [_HBM] * n_so
    out_shape += [jax.ShapeDtypeStruct(sh, F32) for sh in acc_outs] + (list(side.out_shapes) if side else [])
    scratch = ([pltpu.VMEM((tm, n), F32)] if nk > 1 else []) + (list(side.sems) if side else [])
    first_placed = 2 + n_ex + n_wh + n_si
    return pl.pallas_call(
        body, name=name, grid=(ni, nk), in_specs=in_specs, out_specs=out_specs, out_shape=out_shape,
        scratch_shapes=scratch, compiler_params=_params(("arbitrary", "arbitrary")),
        input_output_aliases={first_placed + q_: j_ for q_, (j_, _) in enumerate(placed)},
    )(a, b, *[x_ for x_, _, _ in extras], *wholes, *(side.ins if side else []), *[pb_[0] for _, pb_ in placed])


def _mm_pro(b, *, name, pro, m, extras=(), wholes=(), row_outs=(), col_outs=(), tm=256):
    kk, n = b.shape
    tn = _pick(n, (1408, 1024, 512, 256, 128))
    tm = min(tm, m)
    n_ex, n_wh, n_ro, n_co = len(extras), len(wholes), len(row_outs), len(col_outs)

    def body(*refs):
        ex = refs[:n_ex]
        wh = refs[n_ex:n_ex + n_wh]
        b_ref, o_ref = refs[n_ex + n_wh], refs[n_ex + n_wh + 1]
        ro = refs[n_ex + n_wh + 2:n_ex + n_wh + 2 + n_ro]
        co = refs[n_ex + n_wh + 2 + n_ro:n_ex + n_wh + 2 + n_ro + n_co]
        a_scr = refs[-1]

        @pl.when(pl.program_id(1) == 0)
        def _():
            a_scr[...] = pro(ex, wh, ro, co).astype(MXU_DT)
        o_ref[...] = jnp.dot(a_scr[...], b_ref[...], preferred_element_type=F32)

    in_specs = [pl.BlockSpec((tm, w_), functools.partial(lambda i, j, c_: (i, c_), c_=c_)) for _, w_, c_ in extras]
    in_specs += [_whole(x_.shape) for x_ in wholes] + [pl.BlockSpec((kk, tn), lambda i, j: (0, j))]
    out_specs = [pl.BlockSpec((tm, tn), lambda i, j: (i, j))]
    out_specs += [pl.BlockSpec((tm, w_), lambda i, j: (i, 0)) for w_, _ in row_outs]
    out_specs += [pl.BlockSpec((w_, tm), lambda i, j: (0, i)) for w_, _ in col_outs]
    out_shape = [jax.ShapeDtypeStruct((m, n), F32)] + [jax.ShapeDtypeStruct((m, w_), dt) for w_, dt in row_outs]
    out_shape += [jax.ShapeDtypeStruct((w_, m), dt) for w_, dt in col_outs]
    return pl.pallas_call(
        body, name=name, grid=(m // tm, n // tn), in_specs=in_specs, out_specs=out_specs, out_shape=out_shape,
        scratch_shapes=[pltpu.VMEM((tm, kk), MXU_DT)], compiler_params=_params(("arbitrary", "arbitrary")),
    )(*[x_ for x_, _, _ in extras], *wholes, b.astype(MXU_DT))


def _merge_proj_out(pg, ya, yb, w_out):
    def pro(ex, wh, ro, co):
        ga_ref, gb_ref, ya_ref, yb_ref = ex
        mg = _sigmoid(ga_ref[...]) * ya_ref[...] + _sigmoid(gb_ref[...]) * yb_ref[...]
        co[0][...] = mg.T.astype(MXU_DT)
        return mg

    return _mm_pro(w_out, name="merge_proj_out", pro=pro, m=ya.shape[0],
                   extras=[(pg, D, 0), (pg, D, 1), (ya, D, 0), (yb, D, 0)], col_outs=[(D, MXU_DT)])


def _ffn_down_ln2_loss(act, w_down, x2, tgt, adam, lng, lnb):
    def epi(ff_, ex, wh, ro, ao):
        x_ref, t_ref = ex
        ada_ref, g_ref, b_ref = wh
        dff_ref, dx_ref = ro
        red_ref, = ao
        r = ALPHA * x_ref[...] + (1.0 + ada_ref[5:6, :]) * ff_
        xhat, rstd = _ln_stats(r)
        err = xhat * g_ref[...] + b_ref[...] - t_ref[...]
        dy = err * (1.0 / D)
        dr = _ln_bwd(dy, xhat, rstd, g_ref[...])
        dff_ref[...] = ((1.0 + ada_ref[5:6, :]) * dr).astype(MXU_DT)
        dx_ref[...] = ALPHA * dr
        red_ref[0:1, :] += _rsum(dy * xhat)
        red_ref[1:2, :] += _rsum(dy)
        red_ref[2:3, :] += _rsum(dr * ff_)
        red_ref[3:4, :] += jnp.sum(_rsum(err * err), axis=1, keepdims=True) * (0.5 / D)

    return _mm_epi(act, w_down, name="ffn_down_ln2_loss", epi=epi, extras=[(x2, D, 0), (tgt, D, 0)],
                   wholes=[adam, lng, lnb], row_outs=[(D, MXU_DT), (D, F32)], acc_outs=[(8, D)])


def _d_h1_modulate_bwd(dproj, w_t, dxa, x, adam, side):
    def epi(dh, ex, wh, ro, ao):
        dxa_ref, x_ref = ex
        ada_ref, = wh
        ro[0][...] = dxa_ref[...] + dh * (1.0 + ada_ref[1:2, :])
        ao[0][0:1, :] += _rsum(dh * x_ref[...])
        ao[0][1:2, :] += _rsum(dh)

    return _mm_epi(dproj, w_t, name="d_h1", epi=epi, extras=[(dxa, D, 0), (x, D, 0)], wholes=[adam],
                   row_outs=[(D, F32)], acc_outs=[(8, D)], side=side)


def _d_merged_bwd(dmix, w_out_t, pg, ya, yb, dproj):
    def epi(d, ex, wh, ro, ao):
        ga_ref, gb_ref, ya_ref, yb_ref = ex
        dya_ref, dyb_ref, dpg_ref = ro
        sa = _sigmoid(ga_ref[...])
        sb = _sigmoid(gb_ref[...])
        dya_ref[...] = (d * sa).astype(MXU_DT)
        dyb_ref[...] = (d * sb).astype(MXU_DT)
        dpg_ref[:, :D] = (d * ya_ref[...] * sa * (1.0 - sa)).astype(MXU_DT)
        dpg_ref[:, D:] = (d * yb_ref[...] * sb * (1.0 - sb)).astype(MXU_DT)

    return _mm_epi(dmix, w_out_t, name="d_merged", epi=epi,
                   extras=[(pg, D, 0), (pg, D, 1), (ya, D, 0), (yb, D, 0)],
                   row_outs=[(D, MXU_DT), (D, MXU_DT), (2 * D, MXU_DT, (dproj, 3))])


def _d_dn_post_bwd(dyb, w_pb_t, o, pz, nw, dproj):
    def epi(d_all, ex, wh, ro, ao):
        o_ref, z_ref = ex
        nw_ref, = wh
        do_ref, dz_ref = ro
        acc = jnp.zeros((1, HD), F32)
        for h in range(NV):
            cs = slice(h * HD, (h + 1) * HD)
            oh = o_ref[:, cs]
            z = z_ref[:, cs]
            d = d_all[:, cs]
            sg = _sigmoid(z)
            rs = lax.rsqrt(jnp.mean(oh * oh, axis=-1, keepdims=True) + RMS_EPS)
            n = oh * rs
            dz_ref[:, cs] = (d * n * nw_ref[...] * sg * (1.0 + z * (1.0 - sg))).astype(MXU_DT)
            dn_ = d * (z * sg)
            acc = acc + _rsum(dn_ * n)
            dnn = dn_ * nw_ref[...]
            do_ref[:, cs] = rs * (dnn - n * jnp.mean(dnn * n, axis=-1, keepdims=True))
        ao[0][...] += acc

    return _mm_epi(dyb, w_pb_t, name="d_dn", epi=epi, extras=[(o, NV * HD, 0), (pz, NV * HD, 0)], wholes=[nw],
                   row_outs=[(NV * HD, F32), (NV * HD, MXU_DT, (dproj, 2))], acc_outs=[(1, HD)], tm=512)


def _d_h2_ln1_bwd(dgu, w_gu_t, dx2a, x2, r1, mix, adam, lng):
    def epi(dh, ex, wh, ro, ao):
        dxa_ref, x2_ref, r_ref, m_ref = ex
        ada_ref, g_ref = wh
        dm_ref, dx_ref = ro
        red_ref, = ao
        dx2 = dxa_ref[...] + dh * (1.0 + ada_ref[4:5, :])
        xhat, rstd = _ln_stats(r_ref[...])
        dr = _ln_bwd(dx2, xhat, rstd, g_ref[...])
        dm_ref[...] = ((1.0 + ada_ref[2:3, :]) * dr).astype(MXU_DT)
        dx_ref[...] = ALPHA * dr
        red_ref[0:1, :] += _rsum(dh * x2_ref[...])
        red_ref[1:2, :] += _rsum(dh)
        red_ref[2:3, :] += _rsum(dx2 * xhat)
        red_ref[3:4, :] += _rsum(dx2)
        red_ref[4:5, :] += _rsum(dr * m_ref[...])

    return _mm_epi(dgu, w_gu_t, name="d_h2", epi=epi,
                   extras=[(dx2a, D, 0), (x2, D, 0), (r1, D, 0), (mix, D, 0)], wholes=[adam, lng],
                   row_outs=[(D, MXU_DT), (D, F32)], acc_outs=[(8, D)], tm=512)


def _adamw(parts, w, m, v, name):
    r, c = w.shape
    n_parts = len(parts)
    tm = _row_tile(r, c * 4 * (n_parts + 7))
    c1 = 1.0 - ADAM_B1 ** ADAM_STEP
    c2 = 1.0 - ADAM_B2 ** ADAM_STEP

    def body(*refs):
        g = refs[0][...]
        for p_ref in refs[1:n_parts]:
            g = g + p_ref[...]
        w_ref, m_ref, v_ref, g_out, d_out, m_out, v_out = refs[n_parts:]
        mn = ADAM_B1 * m_ref[...] + (1.0 - ADAM_B1) * g
        vn = ADAM_B2 * v_ref[...] + (1.0 - ADAM_B2) * (g * g)
        g_out[...] = g
        m_out[...] = mn
        v_out[...] = vn
        d_out[...] = -ADAM_LR * ((mn / c1) / (jnp.sqrt(vn / c2) + ADAM_EPS) + ADAM_WD * w_ref[...])

    spec = pl.BlockSpec((tm, c), lambda i: (i, 0))
    return pl.pallas_call(
        body, name=name, grid=(r // tm,),
        in_specs=[spec] * (n_parts + 3), out_specs=[spec] * 4,
        out_shape=[jax.ShapeDtypeStruct((r, c), F32)] * 4, compiler_params=_params(("parallel",)),
    )(*parts, w, m, v)


def _row_tile(r, bytes_per_row):
    for t in (512, 256, 128, 64, 32, 16):
        if r % t == 0 and 2 * t * bytes_per_row <= 20 * 1024 * 1024:
            return t
    return _pick(r, (16, 8))


def _sum_partials(own, recv, name):
    r, c = own.shape
    tm = _row_tile(r, c * (4 + 4 + 3 * 2))

    def body(o_ref, r_ref, out_ref):
        out_ref[...] = ((o_ref[...] + r_ref[0].astype(F32)) + r_ref[1].astype(F32)) + r_ref[2].astype(F32)

    return pl.pallas_call(
        body, name=name, grid=(r // tm,),
        in_specs=[pl.BlockSpec((tm, c), lambda i: (i, 0)), pl.BlockSpec((3, tm, c), lambda i: (0, i, 0))],
        out_specs=pl.BlockSpec((tm, c), lambda i: (i, 0)),
        out_shape=jax.ShapeDtypeStruct((r, c), F32), compiler_params=_params(("parallel",)),
    )(own, recv)


def _sum8(g, name):
    _, r, c = g.shape
    tm = _pick(r, (256, 128, 64, 32, 16, 8))

    def body(g_ref, o_ref):
        acc = g_ref[0]
        for k in range(1, 8):
            acc = acc + g_ref[k]
        o_ref[...] = acc

    return pl.pallas_call(
        body, name=name, grid=(r // tm,),
        in_specs=[pl.BlockSpec((8, tm, c), lambda i: (0, i, 0))],
        out_specs=pl.BlockSpec((tm, c), lambda i: (i, 0)),
        out_shape=jax.ShapeDtypeStruct((r, c), F32), compiler_params=_params(("parallel",)),
    )(g)


def _silu_rows(x):
    def body(x_ref, o_ref):
        xx = x_ref[...]
        o_ref[...] = xx * _sigmoid(xx)

    return pl.pallas_call(body, name="silu_rows", out_shape=jax.ShapeDtypeStruct(x.shape, F32))(x)


def _coords():
    return lax.axis_index("x"), lax.axis_index("y"), lax.axis_index("c")


def _allgather8(v, name):
    r, n = v.shape

    def body(v_ref, out_ref, send_sems, recv_sems):
        x, y, c = _coords()
        me = 4 * x + 2 * y + c
        out_ref[me] = v_ref[...]
        peers = []
        for k in range(1, 8):
            px = 1 - x if k & 4 else x
            py = 1 - y if k & 2 else y
            pc = 1 - c if k & 1 else c
            peers.append((px, py, pc))

        def copy(k, slot, to):
            return pltpu.make_async_remote_copy(
                src_ref=v_ref, dst_ref=out_ref.at[slot], send_sem=send_sems.at[k], recv_sem=recv_sems.at[k],
                device_id=to, device_id_type=MESH)

        sends = [copy(k, me, p) for k, p in enumerate(peers)]
        for cp in sends:
            cp.start()
        for k, (px, py, pc) in enumerate(peers):
            copy(k, 4 * px + 2 * py + pc, (px, py, pc)).wait_recv()
        for cp in sends:
            cp.wait_send()

    return pl.pallas_call(
        body, name=name, out_shape=jax.ShapeDtypeStruct((8, r, n), v.dtype),
        in_specs=[pl.BlockSpec(memory_space=pltpu.VMEM)], out_specs=pl.BlockSpec(memory_space=pltpu.VMEM),
        scratch_shapes=[pltpu.SemaphoreType.DMA((7,)), pltpu.SemaphoreType.DMA((7,))],
        compiler_params=pltpu.CompilerParams(vmem_limit_bytes=VMEM_LIMIT),
    )(v)


def _other_chips(x, y):
    return [(1 - x, y), (x, 1 - y), (1 - x, 1 - y)]


_HBM = pl.BlockSpec(memory_space=pl.ANY)


class _Exchange:
    def __init__(self, ins, out_shapes, sems, start, finish):
        self.ins, self.out_shapes, self.sems, self.start, self.finish = ins, out_shapes, sems, start, finish


def _run_exchange(ex, name):
    n_in, n_out = len(ex.ins), len(ex.out_shapes)

    def body(*refs):
        ins, outs, sems = refs[:n_in], refs[n_in:n_in + n_out], refs[n_in + n_out:]
        ex.start(ins, outs, sems)
        ex.finish(ins, outs, sems)

    return pl.pallas_call(body, name=name, out_shape=list(ex.out_shapes), in_specs=[_HBM] * n_in,
                          out_specs=[_HBM] * n_out, scratch_shapes=list(ex.sems))(*ex.ins)


def _gather_exchange(shards):
    n = len(shards)

    def plan(ins, outs, sems):
        send_sems, recv_sems, local_sems = sems
        x, y, c = _coords()
        s_me = 2 * x + y
        chips = _other_chips(x, y)

        def half(i, slot, hc):
            rh = shards[i].shape[0] // 2
            return outs[i].at[slot, pl.ds(pl.multiple_of(hc * rh, 16), rh), :]

        def copy(i, k, src, dst, to):
            return pltpu.make_async_remote_copy(src_ref=src, dst_ref=dst, send_sem=send_sems.at[6 * i + k],
                                                recv_sem=recv_sems.at[6 * i + k], device_id=to, device_id_type=MESH)

        local = [pltpu.make_async_copy(ins[i], outs[i].at[s_me], local_sems.at[i]) for i in range(n)]
        first = []
        for i in range(n):
            rh = shards[i].shape[0] // 2
            my_half = ins[i].at[pl.ds(pl.multiple_of(c * rh, 16), rh), :]
            first += [copy(i, j, my_half, half(i, s_me, c), (px, py, c)) for j, (px, py) in enumerate(chips)]
        return (x, y, c), chips, half, copy, local, first

    def start(ins, outs, sems):
        _, _, _, _, local, first = plan(ins, outs, sems)
        for cp in local + first:
            cp.start()

    def finish(ins, outs, sems):
        (x, y, c), chips, half, copy, local, first = plan(ins, outs, sems)
        sibling = (x, y, 1 - c)
        passed = []
        for i in range(n):
            for j, (px, py) in enumerate(chips):
                land = half(i, 2 * px + py, c)
                copy(i, j, land, land, (px, py, c)).wait_recv()
                fw = copy(i, 3 + j, land, land, sibling)
                fw.start()
                passed.append(fw)
        for i in range(n):
            for j, (px, py) in enumerate(chips):
                land = half(i, 2 * px + py, 1 - c)
                copy(i, 3 + j, land, land, sibling).wait_recv()
        for cp in first + passed:
            cp.wait_send()
        for cp in local:
            cp.wait()

    return _Exchange(list(shards), [jax.ShapeDtypeStruct((4,) + v.shape, v.dtype) for v in shards],
                     [pltpu.SemaphoreType.DMA((6 * n,)), pltpu.SemaphoreType.DMA((6 * n,)),
                      pltpu.SemaphoreType.DMA((n,))], start, finish)


def _scatter_exchange(gs):
    n = len(gs)

    def copies(ins, outs, sems):
        send_sems, recv_sems = sems
        x, y, c = _coords()
        return [pltpu.make_async_remote_copy(
                    src_ref=ins[i].at[2 * px + py], dst_ref=outs[i].at[j], send_sem=send_sems.at[3 * i + j],
                    recv_sem=recv_sems.at[3 * i + j], device_id=(px, py, c), device_id_type=MESH)
                for i in range(n) for j, (px, py) in enumerate(_other_chips(x, y))]

    def start(ins, outs, sems):
        for cp in copies(ins, outs, sems):
            cp.start()

    def finish(ins, outs, sems):
        cps = copies(ins, outs, sems)
        for cp in cps:
            cp.wait_recv()
        for cp in cps:
            cp.wait_send()

    return _Exchange(list(gs), [jax.ShapeDtypeStruct((3,) + g.shape[1:], g.dtype) for g in gs],
                     [pltpu.SemaphoreType.DMA((3 * n,)), pltpu.SemaphoreType.DMA((3 * n,))], start, finish)


def _allgather_exchange(v):
    def copies(ins, outs, sems):
        send_sems, recv_sems, _ = sems
        x, y, c = _coords()
        me = 4 * x + 2 * y + c
        cps = []
        for k in range(1, 8):
            peer = (1 - x if k & 4 else x, 1 - y if k & 2 else y, 1 - c if k & 1 else c)
            cps.append(pltpu.make_async_remote_copy(
                src_ref=ins[0], dst_ref=outs[0].at[me], send_sem=send_sems.at[k - 1], recv_sem=recv_sems.at[k - 1],
                device_id=peer, device_id_type=MESH))
        return me, cps

    def start(ins, outs, sems):
        me, cps = copies(ins, outs, sems)
        pltpu.make_async_copy(ins[0], outs[0].at[me], sems[2]).start()
        for cp in cps:
            cp.start()

    def finish(ins, outs, sems):
        me, cps = copies(ins, outs, sems)
        for cp in cps:
            cp.wait_recv()
        for cp in cps:
            cp.wait_send()
        pltpu.make_async_copy(ins[0], outs[0].at[me], sems[2]).wait()

    return _Exchange([v], [jax.ShapeDtypeStruct((8,) + v.shape, v.dtype)],
                     [pltpu.SemaphoreType.DMA((7,)), pltpu.SemaphoreType.DMA((7,)), pltpu.SemaphoreType.DMA],
                     start, finish)


def _swap_sibling(vs):
    n = len(vs)

    def body(*refs):
        ins, outs = refs[:n], refs[n:2 * n]
        send_sems, recv_sems = refs[2 * n:]
        x, y, c = _coords()
        cps = [pltpu.make_async_remote_copy(src_ref=ins[i], dst_ref=outs[i], send_sem=send_sems.at[i],
                                            recv_sem=recv_sems.at[i], device_id=(x, y, 1 - c), device_id_type=MESH)
               for i in range(n)]
        for cp in cps:
            cp.start()
        for cp in cps:
            cp.wait()

    return pl.pallas_call(
        body, name="swap_sibling", out_shape=[jax.ShapeDtypeStruct(v.shape, v.dtype) for v in vs],
        in_specs=[_HBM] * n, out_specs=[_HBM] * n,
        scratch_shapes=[pltpu.SemaphoreType.DMA((n,)), pltpu.SemaphoreType.DMA((n,))],
    )(*vs)


def _pad_rows(a, rows):
    return jnp.pad(a, ((0, rows - a.shape[0]), (0, 0)))


def _block_diag(w):
    eye = jnp.eye(RG_BLOCKS, dtype=w.dtype)
    return (eye[:, None, :, None] * w[:, :, None, :]).reshape(D_RNN, D_RNN)


def _diag_blocks(g):
    g4 = g.reshape(RG_BLOCKS, RG_BW, RG_BLOCKS, RG_BW)
    idx = jnp.arange(RG_BLOCKS)
    return g4[idx, :, idx, :]


def _prepare_rest(p):
    w = {}
    for k_, n_ in (("pa", "w_proj_a"), ("pb", "w_proj_b"), ("out", "w_out"), ("down", "ffn_w_down")):
        w[k_] = p[n_].astype(MXU_DT)
        w[k_ + "_t"] = w[k_].T
    w["gu"] = jnp.concatenate([p["ffn_w_gate"], p["ffn_w_up"]], axis=1).astype(MXU_DT)
    w["gu_t"] = w["gu"].T
    return w


def _prepare_first(p):
    w = {}
    wi = p["w_in"].astype(MXU_DT)
    cat = jnp.concatenate([wi[:, 2560:6656], wi[:, 6656:8704], wi[:, 8736:10784], wi[:, 0:2560],
                           wi[:, 8704:8736], jnp.zeros((D, 96), MXU_DT)], axis=1)
    w["in_cat"], w["in_cat_t"] = cat, cat.T
    w["rg_cw"] = _pad_rows(p["rg_conv_w"], 8)
    w["dn_cw"] = _pad_rows(p["dn_conv_w"], 8)
    w["ffn_cw"] = _pad_rows(p["ffn_conv_w"], 8)
    w["rg_vec"] = _pad_rows(jnp.stack([p["rg_conv_b"], p["rg_b_a"], p["rg_b_x"], p["rg_lambda"]]), 8)
    w["wa"] = _block_diag(p["rg_w_a"]).astype(MXU_DT)
    w["wx"] = _block_diag(p["rg_w_x"]).astype(MXU_DT)
    w["wa_t"], w["wx_t"] = w["wa"].T, w["wx"].T
    w["arow"] = jnp.pad(p["dn_a_log"], (0, 128 - NV))[None, :]
    w["drow"] = jnp.pad(p["dn_dt_bias"], (0, 128 - NV))[None, :]
    w["acol"] = p["dn_a_log"][:, None]
    w["dcol"] = p["dn_dt_bias"][:, None]
    w["nw"] = p["dn_norm_w"][None, :]
    w["ffn_cb"] = p["ffn_conv_b"][None, :]
    for n_ in ("ln1_g", "ln1_b", "ln2_g", "ln2_b"):
        w[n_] = p[n_][None, :]
    return w


def _mm_sided(a, b, side, **kw):
    if side is None:
        return _mm(a, b, **kw), []
    res = _mm(a, b, side=side, **kw)
    return res[0], res[1:]


_SMALL = ("rg_conv_w", "dn_conv_w", "ffn_conv_w", "rg_conv_b", "rg_w_a", "rg_b_a", "rg_w_x", "rg_b_x",
          "rg_lambda", "dn_a_log", "dn_dt_bias", "dn_norm_w", "ln1_g", "ln1_b", "ffn_conv_b", "ln2_g", "ln2_b")
_REST_A = ("w_proj_a", "w_proj_b", "w_out")
_REST_B = ("ffn_w_gate", "ffn_w_up", "ffn_w_down")


def _local_step(x, tgt, adam, p, shards=None):
    w = _prepare_first(p)
    side_a = side_b = None
    if shards is not None:
        side_a = _gather_exchange([shards[n] for n in _REST_A])
        side_b = _gather_exchange([shards[n] for n in _REST_B])
    h1, h1t = _modulate(x, adam)
    cat = w["in_cat"]
    pqkv, got_b = _mm_sided(h1, cat[:, 0:4096], side_b, name="proj_qkv")
    px, got_a = _mm_sided(h1, cat[:, 8192:10752], side_a, name="proj_x")
    if shards is not None:
        p = dict(p, **{n: _unstack_shards(n, a_) for n, a_ in zip(_REST_A + _REST_B, got_a + got_b)})
    w.update(_prepare_rest(p))
    pz = _mm(h1, cat[:, 4096:6144], name="proj_z")
    pg = _mm(h1, cat[:, 6144:8192], name="proj_g")
    pab = _mm(h1, cat[:, 10752:10880], name="proj_ab")
    hrec, xcs, rec, rec_t = _rg_fwd(px, w["rg_cw"], w["rg_vec"], w["wa"], w["wx"])
    q, k, v, rt, cpre = _dn_prep(pqkv, pab, w["dn_cw"], w["arow"], w["drow"])
    u, ww, qd, kd, pm, tinv, gam = _dn_intra(q, k, v, rt)
    o, vn, ssave, dn, dn_t = _dn_seq(u, ww, qd, kd, pm, gam, pz, w["nw"])
    ya = _mm(rec, w["pa"], name="proj_a")
    yb = _mm(dn, w["pb"], name="proj_b")
    mix, merged_t = _merge_proj_out(pg, ya, yb, w["out"])
    r1, x2, h2, h2_t = _ln1(x, mix, adam, w["ln1_g"], w["ln1_b"])
    gu = _mm(h2, w["gu"], name="ffn_gu")
    act, act_t, gcf = _ffn_act(gu, w["ffn_cw"], w["ffn_cb"])
    dff, dx2a, red2 = _ffn_down_ln2_loss(act, w["down"], x2, tgt, adam, w["ln2_g"], w["ln2_b"])
    g = {}
    dact = _mm(dff, w["down_t"], name="d_act")
    g["ffn_w_down"] = _mm(act_t, dff, name="g_down")
    dgu, gcw_f, gcb_f = _ffn_bwd(dact, gu, gcf, w["ffn_cw"])
    ggu = _mm(h2_t, dgu, name="g_gu")
    g["ffn_w_gate"], g["ffn_w_up"] = ggu[:, :D_FF], ggu[:, D_FF:]
    g["ffn_conv_w"], g["ffn_conv_b"] = gcw_f[0:3], gcb_f[0]
    dmix, dxa, red1 = _d_h2_ln1_bwd(dgu, w["gu_t"], dx2a, x2, r1, mix, adam, w["ln1_g"])
    g["w_out"] = _mm(merged_t, dmix, name="g_out")
    dproj = lax.empty((x.shape[0], N_CAT), MXU_DT)
    dya, dyb, dproj = _d_merged_bwd(dmix, w["out_t"], pg, ya, yb, dproj)
    drec = _mm(dya, w["pa_t"], name="d_rec")
    g["w_proj_a"] = _mm(rec_t, dya, name="g_pa")
    g["w_proj_b"] = _mm(dn_t, dyb, name="g_pb")
    do, dproj, gnw = _d_dn_post_bwd(dyb, w["pb_t"], o, pz, w["nw"], dproj)
    dvn, dkd, dgam = _dn_seq_bwd(do, qd, kd, pm, ww, vn, ssave, gam)
    dq, dk, dv, drt = _dn_intra_bwd(q, k, v, rt, do, dvn, dkd, gam, dgam, ssave, tinv, u, ww, vn)
    dproj, dpab, gcw_d, gsc = _dn_prep_bwd(dq, dk, dv, drt, pqkv, cpre, pab, w["dn_cw"], w["acol"], w["dcol"],
                                           dproj)
    side_r = None
    if shards is not None:
        side_r = _scatter_exchange([_stack_shards(n, g[n]).astype(MXU_DT) for n in _REST_A + _REST_B])
    dpx, gwa, gwx, gcw_r, gvec, *got_r = _rg_bwd(drec, px, xcs, hrec, w["rg_cw"], w["rg_vec"], w["wa"], w["wx"],
                                                 w["wa_t"], w["wx_t"], side_r)
    dproj = lax.dynamic_update_slice(dproj, jnp.concatenate([dpx, dpab], axis=1), (0, 8192))
    reorder = lambda gc: jnp.concatenate([gc[:, 8192:10752], gc[:, 0:4096], gc[:, 4096:6144], gc[:, 10752:10784],
                                          gc[:, 6144:8192]], axis=1)
    wire = lambda gh: _scatter_exchange([_stack_shards("w_in", gh).astype(MXU_DT)]) if shards is not None else None
    g["rg_conv_w"], g["rg_conv_b"] = gcw_r[0:4], gvec[0]
    g["rg_w_a"], g["rg_w_x"] = _diag_blocks(gwa), _diag_blocks(gwx)
    g["rg_b_a"], g["rg_b_x"], g["rg_lambda"] = gvec[1], gvec[2], gvec[3]
    g["dn_conv_w"] = gcw_d[0:4]
    g["dn_a_log"], g["dn_dt_bias"], g["dn_norm_w"] = gsc[:, 0], gsc[:, 1], gnw[0]
    g["ln1_g"], g["ln1_b"] = red1[2], red1[3]
    g["ln2_g"], g["ln2_b"] = red2[0], red2[1]
    side_s = None
    if shards is not None:
        side_s = _allgather_exchange(_pack([jnp.full((1,), red2[3, 0], F32)] + [g[n] for n in _SMALL], 128, 64))
    g_top, got_s = _mm_sided(h1t[:D // 2], dproj, side_s, name="g_in_top")
    g_top = reorder(g_top)
    g_bot, got_top = _mm_sided(h1t[D // 2:], dproj, wire(g_top), name="g_in_bot")
    g_bot = reorder(g_bot)
    gx, red0, *got_bot = _d_h1_modulate_bwd(dproj, w["in_cat_t"], dxa, x, adam, wire(g_bot))
    g["w_in"] = jnp.concatenate([g_top, g_bot], axis=0)
    got = dict(zip(_REST_A + _REST_B, got_r))
    if shards is not None:
        got["w_in"] = jnp.concatenate([got_top[0], got_bot[0]], axis=1)
        got["small"] = got_s[0]
    d_ada = jnp.concatenate([red0[1], red0[0], red1[4], red1[1], red1[0], red2[2]])
    return red2[3, 0], gx, g, d_ada, got


_BIG = ("w_in", "w_proj_a", "w_proj_b", "w_out", "ffn_w_gate", "ffn_w_up", "ffn_w_down")
_COL_SHARDED = ("w_in", "ffn_w_gate", "ffn_w_up")
_CONV = ("rg_conv_w", "dn_conv_w", "ffn_conv_w")
_REPL = ("b_ada", "rg_conv_b", "rg_w_a", "rg_b_a", "rg_w_x", "rg_b_x", "rg_lambda", "dn_a_log",
         "dn_dt_bias", "dn_norm_w", "ln1_g", "ln1_b", "ffn_conv_b", "ln2_g", "ln2_b")
_NAMES = ("w_ada", "b_ada", "w_in", "rg_conv_w", "rg_conv_b", "rg_w_a", "rg_b_a", "rg_w_x", "rg_b_x",
          "rg_lambda", "dn_conv_w", "dn_a_log", "dn_dt_bias", "dn_norm_w", "w_proj_a", "w_proj_b", "w_out",
          "ln1_g", "ln1_b", "ffn_w_gate", "ffn_w_up", "ffn_conv_w", "ffn_conv_b", "ffn_w_down", "ln2_g", "ln2_b")


def _pack(arrs, width, row_mult):
    pieces = []
    for a in arrs:
        f = a.reshape(-1)
        pieces.append(jnp.pad(f, (0, (-f.shape[0]) % (8 * width))).reshape(-1, width))
    rows = sum(p_.shape[0] for p_ in pieces)
    if rows % row_mult:
        pieces.append(jnp.zeros((row_mult - rows % row_mult, width), pieces[0].dtype))
    return jnp.concatenate(pieces, axis=0)


def _unpack(flat, shapes, width):
    out, row = [], 0
    for shp in shapes:
        n = 1
        for d_ in shp:
            n *= d_
        rows = -(-n // (8 * width)) * 8
        out.append(flat[row:row + rows].reshape(-1)[:n].reshape(shp))
        row += rows
    return out


def _stack_shards(name, full):
    if name in _COL_SHARDED or name in _CONV:
        r, ncol = full.shape
        return full.reshape(r, 4, ncol // 4).transpose(1, 0, 2)
    return full.reshape((4, full.shape[0] // 4) + full.shape[1:])


def _unstack_shards(name, st):
    if name in _COL_SHARDED or name in _CONV:
        return st.transpose(1, 0, 2).reshape(st.shape[1], 4 * st.shape[2])
    return st.reshape((4 * st.shape[1],) + st.shape[2:])


def kernel(x, c, w_ada, b_ada, w_in, rg_conv_w, rg_conv_b, rg_w_a, rg_b_a, rg_w_x, rg_b_x, rg_lambda, dn_conv_w, dn_a_log, dn_dt_bias, dn_norm_w, w_proj_a, w_proj_b, w_out, ln1_g, ln1_b, ffn_w_gate, ffn_w_up, ffn_conv_w, ffn_conv_b, ffn_w_down, ln2_g, ln2_b, loss_target, m_w_ada, m_b_ada, m_w_in, m_rg_conv_w, m_rg_conv_b, m_rg_w_a, m_rg_b_a, m_rg_w_x, m_rg_b_x, m_rg_lambda, m_dn_conv_w, m_dn_a_log, m_dn_dt_bias, m_dn_norm_w, m_w_proj_a, m_w_proj_b, m_w_out, m_ln1_g, m_ln1_b, m_ffn_w_gate, m_ffn_w_up, m_ffn_conv_w, m_ffn_conv_b, m_ffn_w_down, m_ln2_g, m_ln2_b, v_w_ada, v_b_ada, v_w_in, v_rg_conv_w, v_rg_conv_b, v_rg_w_a, v_rg_b_a, v_rg_w_x, v_rg_b_x, v_rg_lambda, v_dn_conv_w, v_dn_a_log, v_dn_dt_bias, v_dn_norm_w, v_w_proj_a, v_w_proj_b, v_w_out, v_ln1_g, v_ln1_b, v_ffn_w_gate, v_ffn_w_up, v_ffn_conv_w, v_ffn_conv_b, v_ffn_w_down, v_ln2_g, v_ln2_b):
    args = locals()
    wts = {n: args[n][0] for n in _NAMES}
    mom = {n: args["m_" + n][0] for n in _NAMES}
    var = {n: args["v_" + n][0] for n in _NAMES}
    xs, tgt = x[0], loss_target[0]
    ix, iy, ic = _coords()
    shard = 2 * ix + iy
    batch = 4 * ix + 2 * iy + ic

    c_all = _allgather8(_pad_rows(c, 8), "gather_c")[:, 0, :]
    sc16 = _pad_rows(_silu_rows(c_all), 16)
    ada_cols = _mm(sc16, w_ada[0], name="ada")[:8]
    ada_g = _allgather8(ada_cols, "gather_ada")
    ada_all = jnp.concatenate([ada_g[0], ada_g[2], ada_g[4], ada_g[6]], axis=1) + b_ada
    adam = _pad_rows(lax.dynamic_index_in_dim(ada_all, batch, 0, keepdims=False).reshape(6, D), 8)

    (w_in_all,) = _run_exchange(_gather_exchange([wts["w_in"].astype(MXU_DT)]), "gather_w_in")
    full = {"w_in": _unstack_shards("w_in", w_in_all)}
    conv_shard = _pack([wts[n] for n in _CONV], 128, 8)
    conv_all = _allgather8(conv_shard, "gather_conv")
    shapes_conv = [wts[n].shape for n in _CONV]
    per_shard = [_unpack(conv_all[2 * s], shapes_conv, 128) for s in range(4)]
    for i, n in enumerate(_CONV):
        full[n] = _unstack_shards(n, jnp.stack([per_shard[s][i] for s in range(4)]))
    for n in _REPL:
        full[n] = wts[n]

    shards = {n: wts[n].astype(MXU_DT) for n in _REST_A + _REST_B}
    loss_b, gx, g, d_ada, recv = _local_step(xs, tgt, adam, full, shards)

    parts = []
    for n in _BIG:
        r_ = recv[n]
        axis = 1 if n in _COL_SHARDED else 0
        width = wts[n].shape[axis]
        own = lax.dynamic_slice_in_dim(g[n], shard * width, width, axis=axis)
        parts.append(_sum_partials(own, r_, "sum_" + n))
    parts_sib = _swap_sibling(parts)
    out = {n: [] for n in _NAMES}
    for n, p_, q_ in zip(_BIG, parts, parts_sib):
        out[n] = list(_adamw([p_, q_], wts[n], mom[n], var[n], "adamw_" + n))

    tot = _unpack(_sum8(recv["small"], "sum_small"), [(1,)] + [full[n].shape for n in _SMALL], 128)
    gsum = dict(zip(_SMALL, tot[1:]))
    loss = tot[0][0]
    for n in _CONV:
        gsum[n] = lax.dynamic_index_in_dim(_stack_shards(n, gsum[n]), shard, 0, keepdims=False)
    d_ada_g = _allgather8(d_ada.reshape(6 * D // 128, 128), "gather_d_ada")
    gsum["b_ada"] = _sum8(d_ada_g, "sum_d_ada").reshape(6 * D)
    d_ada_all = d_ada_g.reshape(8, 6 * D)
    cols = lax.dynamic_slice_in_dim(d_ada_all, shard * (6 * D // 4), 6 * D // 4, axis=1)
    g_wada = _mm(sc16, _pad_rows(cols, 16), name="g_ada", trans_a=True)
    res = _adamw([g_wada], wts["w_ada"], mom["w_ada"], var["w_ada"], "adamw_ada")
    out["w_ada"] = list(res)
    names_s = _CONV + _REPL
    shapes_s = [wts[n].shape for n in names_s]
    pk = lambda d_: _pack([d_[n] for n in names_s], 128, 64)
    res_s = _adamw([pk(gsum)], pk(wts), pk(mom), pk(var), "adamw_small")
    for r_ in res_s:
        for n, a in zip(names_s, _unpack(r_, shapes_s, 128)):
            out[n].append(a)

    outs = [loss, gx[None]]
    for i in range(4):
        outs += [out[n][i][None] for n in _NAMES]
    return tuple(outs)
```

```python
import functools

import jax
import jax.numpy as jnp
from jax import lax
from jax.experimental import pallas as pl
from jax.experimental.pallas import tpu as pltpu

F32 = jnp.float32
BF16 = jnp.bfloat16
MXU_DT = BF16

D = 1024
D_RNN = 1280
RG_BLOCKS = 16
RG_BW = 80
RG_C = 8.0
NQ = 8
NV = 16
HD = 128
CH = 64
D_FF = 2816
LN_EPS = 1e-5
RMS_EPS = 1e-6
L2_EPS = 1e-6
ALPHA = 2.0 ** 0.25
Q_SCALE = HD ** -0.5
N_CAT = 10880
VMEM_LIMIT = 56 * 1024 * 1024
MM_VMEM_BUDGET = 36 * 1024 * 1024
SEQ_GROUP = 16
MESH = pl.DeviceIdType.MESH

ADAM_LR, ADAM_B1, ADAM_B2, ADAM_EPS, ADAM_WD, ADAM_STEP = 1e-3, 0.9, 0.999, 1e-8, 0.01, 10


def _sigmoid(x):
    return 0.5 * jnp.tanh(0.5 * x) + 0.5


def _softplus(x):
    return jnp.maximum(x, 0.0) + jnp.log1p(jnp.exp(-jnp.abs(x)))


_GC = 0.7978845608028654


def _gelu(x):
    return 0.5 * x * (1.0 + jnp.tanh(_GC * (x + 0.044715 * x * x * x)))


def _gelu_and_grad(x):
    t = jnp.tanh(_GC * (x + 0.044715 * x * x * x))
    g = 0.5 * x * (1.0 + t)
    dg = 0.5 * (1.0 + t) + 0.5 * x * (1.0 - t * t) * _GC * (1.0 + 3 * 0.044715 * x * x)
    return g, dg


def _neg_expm1(y):
    series = -y * (1.0 + 0.5 * y * (1.0 + y * (1.0 / 3.0)))
    return jnp.where(y > -0.01, series, 1.0 - jnp.exp(y))


def _dot(a, b):
    return jnp.dot(a.astype(MXU_DT), b.astype(MXU_DT), preferred_element_type=F32)


def _dot_nt(a, b):
    return lax.dot_general(a.astype(MXU_DT), b.astype(MXU_DT), (((1,), (1,)), ((), ())),
                           preferred_element_type=F32)


def _dot_tn(a, b):
    return lax.dot_general(a.astype(MXU_DT), b.astype(MXU_DT), (((0,), (0,)), ((), ())),
                           preferred_element_type=F32)


def _split(a):
    hi = a.astype(BF16)
    return hi, (a - hi.astype(F32)).astype(BF16)


def _dot3(a, b, dims=(((1,), (0,)), ((), ()))):
    ah, al = _split(a)
    bh, bl = _split(b)
    d = lambda p, q: lax.dot_general(p, q, dims, preferred_element_type=F32)
    return d(ah, bh) + (d(al, bh) + d(ah, bl))


def _dot3_tn(a, b):
    return _dot3(a, b, (((0,), (0,)), ((), ())))


def _iota(shape, dim):
    return lax.broadcasted_iota(jnp.int32, shape, dim)


def _shift_down(x, before, j):
    if j == 0:
        return x
    xr = pltpu.roll(x, j, 0)
    br = pltpu.roll(before, j, 0)
    top = jnp.where(_iota(br.shape, 0) < j, br, xr[:8])
    return jnp.concatenate([top, xr[8:]], axis=0)


def _shift_up(x, after, j):
    if j == 0:
        return x
    t = x.shape[0]
    xr = pltpu.roll(x, t - j, 0)
    ar = pltpu.roll(after, 8 - j, 0)
    bot = jnp.where(_iota(ar.shape, 0) >= 8 - j, ar, xr[t - 8:])
    return jnp.concatenate([xr[:t - 8], bot], axis=0)


def _taps(x, before, k):
    return [_shift_down(x, before, k - 1 - i) for i in range(k)]


def _conv_taps(taps, w_ref):
    y = w_ref[0:1, :] * taps[0]
    for i in range(1, len(taps)):
        y = y + w_ref[i:i + 1, :] * taps[i]
    return y


def _conv_causal(x, before, w_ref, k):
    return _conv_taps(_taps(x, before, k), w_ref)


def _conv_causal_bwd(dy, after, w_ref, k, x=None, gw_ref=None):
    dx = None
    for i in range(k):
        sh = _shift_up(dy, after, k - 1 - i)
        term = w_ref[i:i + 1, :] * sh
        dx = term if dx is None else dx + term
        if x is not None:
            gw_ref[i:i + 1, :] += _rsum(x * sh)
    return dx


def _scan_fwd(a, u, carry):
    t = a.shape[0]
    pos = _iota(a.shape, 0) & 7
    for d in (1, 2, 4):
        m = pos >= d
        u = u + jnp.where(m, a * pltpu.roll(u, d, 0), 0.0)
        a = jnp.where(m, a * pltpu.roll(a, d, 0), a)
    out = []
    for g in range(t // 8):
        hg = u[8 * g:8 * g + 8] + a[8 * g:8 * g + 8] * carry
        out.append(hg)
        carry = hg[7:8]
    return jnp.concatenate(out, axis=0)


def _scan_rev(a, u, carry):
    t = a.shape[0]
    pos = _iota(a.shape, 0) & 7
    for d in (1, 2, 4):
        m = pos < 8 - d
        u = u + jnp.where(m, a * pltpu.roll(u, t - d, 0), 0.0)
        a = jnp.where(m, a * pltpu.roll(a, t - d, 0), a)
    out = [None] * (t // 8)
    for g in reversed(range(t // 8)):
        lg = u[8 * g:8 * g + 8] + a[8 * g:8 * g + 8] * carry
        out[g] = lg
        carry = lg[0:1]
    return jnp.concatenate(out, axis=0)


def _chunk_cumsum(g, axis, rev=False):
    n = g.shape[axis]
    pos = _iota(g.shape, axis) & (CH - 1)
    d = 1
    while d < CH:
        if rev:
            g = g + jnp.where(pos < CH - d, pltpu.roll(g, n - d, axis), 0.0)
        else:
            g = g + jnp.where(pos >= d, pltpu.roll(g, d, axis), 0.0)
        d *= 2
    return g


def _ln_stats(r):
    mu = jnp.mean(r, axis=-1, keepdims=True)
    xc = r - mu
    var = jnp.mean(xc * xc, axis=-1, keepdims=True)
    rstd = lax.rsqrt(var + LN_EPS)
    return xc * rstd, rstd


def _ln_bwd(dy, xhat, rstd, g):
    dxh = dy * g
    return rstd * (dxh - jnp.mean(dxh, axis=-1, keepdims=True)
                   - xhat * jnp.mean(dxh * xhat, axis=-1, keepdims=True))


def _rsum(x):
    return jnp.sum(x, axis=0, keepdims=True)


def _params(sem):
    return pltpu.CompilerParams(dimension_semantics=sem, vmem_limit_bytes=VMEM_LIMIT)


def _pick(n, cands):
    for c in cands:
        if n % c == 0:
            return c
    return n


def _rows(tm, w, col=0, nt=None):
    if nt is None:
        return pl.BlockSpec((tm, w), lambda i: (i, col))
    return pl.BlockSpec((tm, w), lambda i: (nt - 1 - i, col))


def _before(tm, w, col=0, nt=None):
    r = tm // 8
    if nt is None:
        return pl.BlockSpec((8, w), lambda i: (jnp.maximum(i * r - 1, 0), col))
    return pl.BlockSpec((8, w), lambda i: (jnp.maximum((nt - 1 - i) * r - 1, 0), col))


def _cols(tm, w):
    return pl.BlockSpec((w, tm), lambda i: (0, i))


def _whole(shape):
    return pl.BlockSpec(shape, lambda *_: (0,) * len(shape))


def _mm_plan(a, b, out_dtype):
    m, kk = a.shape
    _, n = b.shape
    tm = _pick(m, (512, 256, 128))

    def vmem_bytes(tn, tk):
        blocks = tm * tk * a.dtype.itemsize + tk * tn * b.dtype.itemsize + tm * tn * jnp.dtype(out_dtype).itemsize
        return 2 * blocks + (tm * tn * 4 if tk < kk else 0)

    cands = [t for t in (1408, 1280, 1024, 640, 512, 256, 128) if n % t == 0] or [n]
    tks = [kk] if kk <= 5632 else [t for t in (4096, 2176, 2048, 1024) if kk % t == 0]
    tn, tk = max(((next((t for t in cands if vmem_bytes(t, k_) <= MM_VMEM_BUDGET), cands[-1]), k_) for k_ in tks))
    return tm, tn, tk, kk // tk


def _mm_side(a, b, name, out_dtype, side):
    m, _ = a.shape
    _, n = b.shape
    tm, tn, tk, nk = _mm_plan(a, b, out_dtype)
    ni, nj = m // tm, n // tn
    n_in, n_out = len(side.ins), len(side.out_shapes)

    def body(*refs):
        a_ref, b_ref = refs[0], refs[1]
        s_in = refs[2:2 + n_in]
        o_ref = refs[2 + n_in]
        s_out = refs[3 + n_in:3 + n_in + n_out]
        acc = refs[3 + n_in + n_out]
        sems = refs[4 + n_in + n_out:]
        i, j, k = pl.program_id(0), pl.program_id(1), pl.program_id(2)

        @pl.when((i == 0) & (j == 0) & (k == 0))
        def _():
            side.start(s_in, s_out, sems)

        @pl.when(k == 0)
        def _():
            acc[...] = jnp.zeros_like(acc)
        acc[...] += _dot(a_ref[...], b_ref[...])

        @pl.when(k == nk - 1)
        def _():
            o_ref[...] = acc[...].astype(out_dtype)

        @pl.when((i == ni - 1) & (j == nj - 1) & (k == nk - 1))
        def _():
            side.finish(s_in, s_out, sems)

    return pl.pallas_call(
        body, name=name, grid=(ni, nj, nk),
        in_specs=[pl.BlockSpec((tm, tk), lambda i, j, k: (i, k)),
                  pl.BlockSpec((tk, tn), lambda i, j, k: (k, j))] + [_HBM] * n_in,
        out_specs=[pl.BlockSpec((tm, tn), lambda i, j, k: (i, j))] + [_HBM] * n_out,
        out_shape=[jax.ShapeDtypeStruct((m, n), out_dtype)] + list(side.out_shapes),
        scratch_shapes=[pltpu.VMEM((tm, tn), F32)] + list(side.sems),
        compiler_params=_params(("arbitrary", "arbitrary", "arbitrary")),
    )(a, b, *side.ins)


def _mm(a, b, *, name, trans_a=False, out_dtype=F32, side=None):
    if trans_a:
        return _mm(a.T, b, name=name, out_dtype=out_dtype, side=side)
    if side is not None:
        return _mm_side(a, b, name, out_dtype, side)
    m, kk = a.shape
    _, n = b.shape
    tm, tn, tk, nk = _mm_plan(a, b, out_dtype)

    if nk == 1:
        def body(a_ref, b_ref, o_ref):
            o_ref[...] = _dot(a_ref[...], b_ref[...]).astype(out_dtype)
        scratch = []
    else:
        def body(a_ref, b_ref, o_ref, acc):
            k = pl.program_id(2)

            @pl.when(k == 0)
            def _():
                acc[...] = jnp.zeros_like(acc)
            acc[...] += _dot(a_ref[...], b_ref[...])

            @pl.when(k == nk - 1)
            def _():
                o_ref[...] = acc[...].astype(out_dtype)
        scratch = [pltpu.VMEM((tm, tn), F32)]

    return pl.pallas_call(
        body, name=name, grid=(m // tm, n // tn, nk),
        in_specs=[pl.BlockSpec((tm, tk), lambda i, j, k: (i, k)),
                  pl.BlockSpec((tk, tn), lambda i, j, k: (k, j))],
        out_specs=pl.BlockSpec((tm, tn), lambda i, j, k: (i, j)),
        out_shape=jax.ShapeDtypeStruct((m, n), out_dtype),
        scratch_shapes=scratch,
        compiler_params=_params(("parallel", "parallel", "arbitrary")),
    )(a, b)


def _modulate(x, adam):
    s = x.shape[0]
    tm = _pick(s, (512, 256, 128))

    def body(x_ref, ada_ref, o_ref, ot_ref):
        h = x_ref[...] * (1.0 + ada_ref[1:2, :]) + ada_ref[0:1, :]
        o_ref[...] = h.astype(MXU_DT)
        ot_ref[...] = h.T.astype(MXU_DT)

    return pl.pallas_call(
        body, name="modulate1", grid=(s // tm,),
        in_specs=[_rows(tm, D), _whole((8, D))], out_specs=[_rows(tm, D), _cols(tm, D)],
        out_shape=[jax.ShapeDtypeStruct((s, D), MXU_DT), jax.ShapeDtypeStruct((D, s), MXU_DT)],
        compiler_params=_params(("parallel",)),
    )(x, adam)


def _rg_gates(xc, wa_ref, wx_ref, vec_ref):
    xb = xc.astype(MXU_DT)
    r = _sigmoid(jnp.dot(xb, wa_ref[...], preferred_element_type=F32) + vec_ref[1:2, :])
    ig = _sigmoid(jnp.dot(xb, wx_ref[...], preferred_element_type=F32) + vec_ref[2:3, :])
    sp = _softplus(-vec_ref[3:4, :])
    la = -RG_C * r * sp
    a = jnp.exp(la)
    n1 = _neg_expm1(2.0 * la)
    rmult = lax.rsqrt(jnp.maximum(n1, 1e-20))
    return r, ig, a, n1 * rmult, sp, rmult


def _rg_fwd(px, cw, vec, wa, wx):
    s = px.shape[0]
    tm = _pick(s, (256, 128))
    w = D_RNN

    def body(xr_ref, gr_ref, cw_ref, vec_ref, wa_ref, wx_ref, h_ref, xc_ref, rec_ref, rect_ref, prev_x, prev_h):
        @pl.when(pl.program_id(0) == 0)
        def _():
            prev_x[...] = jnp.zeros_like(prev_x)
            prev_h[...] = jnp.zeros_like(prev_h)
        x = xr_ref[...]
        xc = _conv_causal(x, prev_x[...], cw_ref, 4) + vec_ref[0:1, :]
        prev_x[...] = x[tm - 8:, :]
        xc_ref[...] = xc
        _, ig, a, mult, _, _ = _rg_gates(xc, wa_ref, wx_ref, vec_ref)
        h = _scan_fwd(a, mult * ig * xc, prev_h[7:8, :])
        prev_h[...] = h[tm - 8:, :]
        h_ref[...] = h
        rec = h * _gelu(gr_ref[...])
        rec_ref[...] = rec.astype(MXU_DT)
        rect_ref[...] = rec.T.astype(MXU_DT)

    return pl.pallas_call(
        body, name="rg_fwd", grid=(s // tm,),
        in_specs=[_rows(tm, w, 0), _rows(tm, w, 1), _whole((8, w)), _whole((8, w)),
                  _whole((w, w)), _whole((w, w))],
        out_specs=[_rows(tm, w), _rows(tm, w), _rows(tm, w), _cols(tm, w)],
        out_shape=[jax.ShapeDtypeStruct((s, w), F32), jax.ShapeDtypeStruct((s, w), F32),
                   jax.ShapeDtypeStruct((s, w), MXU_DT), jax.ShapeDtypeStruct((w, s), MXU_DT)],
        scratch_shapes=[pltpu.VMEM((8, w), F32), pltpu.VMEM((8, w), F32)],
        compiler_params=_params(("arbitrary",)),
    )(px, px, cw, vec, wa, wx)


def _dn_scalars(ab, arow, drow):
    lane = _iota(ab.shape, 1)
    g = jnp.where(lane < NV, -jnp.exp(arow) * _softplus(ab + drow), 0.0)
    beta = _sigmoid(ab)
    return lane, g, beta


def _l2n_heads(c, out_ref, off, scale):
    for hh in range(NQ):
        x = c[:, off + hh * HD: off + (hh + 1) * HD]
        r = lax.rsqrt(jnp.sum(x * x, axis=-1, keepdims=True) + L2_EPS)
        out_ref[:, hh * HD:(hh + 1) * HD] = x * (r * scale)


def _dn_prep(pqkv, pab, cw, arow, drow):
    s = pqkv.shape[0]
    tm = _pick(s, (256, 128))
    wq = NQ * HD

    def body(x_ref, ab_ref, cw_ref, a_ref, d_ref, q_ref, k_ref, v_ref, rt_ref, cp_ref, prev_x):
        @pl.when(pl.program_id(0) == 0)
        def _():
            prev_x[...] = jnp.zeros_like(prev_x)
        x = x_ref[...]
        cp = _conv_causal(x, prev_x[...], cw_ref, 4)
        prev_x[...] = x[tm - 8:, :]
        cp_ref[...] = cp
        c = cp * _sigmoid(cp)
        _l2n_heads(c, q_ref, 0, Q_SCALE)
        _l2n_heads(c, k_ref, wq, 1.0)
        v_ref[...] = c[:, 2 * wq:]
        lane, g, beta = _dn_scalars(ab_ref[...], a_ref[...], d_ref[...])
        gc = _chunk_cumsum(g, 0)
        gl = gc + _chunk_cumsum(g, 0, rev=True) - g
        pack = jnp.where(lane < NV, gc, jnp.where(lane < 2 * NV, beta,
                         jnp.where(lane < 3 * NV, pltpu.roll(gl, 2 * NV, 1), 0.0)))
        rt_ref[...] = pack.T[0:3 * NV, :]

    return pl.pallas_call(
        body, name="dn_prep", grid=(s // tm,),
        in_specs=[_rows(tm, 4 * wq), _rows(tm, 128), _whole((8, 4 * wq)), _whole((1, 128)), _whole((1, 128))],
        out_specs=[_rows(tm, wq), _rows(tm, wq), _rows(tm, 2 * wq),
                   pl.BlockSpec((3 * NV, tm), lambda i: (0, i)), _rows(tm, 4 * wq)],
        out_shape=[jax.ShapeDtypeStruct((s, wq), F32), jax.ShapeDtypeStruct((s, wq), F32),
                   jax.ShapeDtypeStruct((s, 2 * wq), F32), jax.ShapeDtypeStruct((3 * NV, s), F32),
                   jax.ShapeDtypeStruct((s, 4 * wq), F32)],
        scratch_shapes=[pltpu.VMEM((8, 4 * wq), F32)],
        compiler_params=_params(("arbitrary",)),
    )(pqkv, pab, cw, arow, drow)


def _pair_masks():
    i = _iota((2 * CH, 2 * CH), 0)
    j = _iota((2 * CH, 2 * CH), 1)
    same = (i >> 6) == (j >> 6)
    return same & (i >= j), same & (i > j)


def _head_cols(rt_ref, h):
    shp = (2 * CH, 2 * CH)
    g_r = jnp.broadcast_to(rt_ref[pl.ds(h, 1), :], shp)
    b_r = jnp.broadcast_to(rt_ref[pl.ds(NV + h, 1), :], shp)
    l_r = jnp.broadcast_to(rt_ref[pl.ds(2 * NV + h, 1), :], shp)
    return g_r, g_r.T, b_r.T, l_r, l_r.T


def _inv_unit_lower_many(a_list):
    n = a_list[0].shape[0]
    eye = (_iota((n, n), 0) == _iota((n, n), 1)).astype(F32)
    bs = [-a for a in a_list]
    xs = [eye + b for b in bs]
    for _ in range(5):
        bs = [_dot(b, b) for b in bs]
        xs = [x + _dot(x, b) for x, b in zip(xs, bs)]
    rs = [(eye - x) - _dot3(a, x) for a, x in zip(a_list, xs)]
    return [x + _dot(x, r) for x, r in zip(xs, rs)]


def _gam_rows(l_r):
    lrow = l_r[0:1, :]
    lane = _iota(lrow.shape, 1)
    other = pltpu.roll(lrow, CH, 1)
    return jnp.exp(jnp.where(lane < CH, lrow, other)), jnp.exp(jnp.where(lane >= CH, lrow, other))


def _dn_intra(q, k, v, rt):
    s = q.shape[0]
    nb = s // (2 * CH)
    qps = 8
    blk = pl.BlockSpec((2 * CH, qps * HD), lambda i, h: (i, h))
    blk2 = pl.BlockSpec((2 * CH, 2 * qps * HD), lambda i, h: (i, h))

    def body(q_ref, k_ref, v_ref, rt_ref, u_ref, w_ref, qd_ref, kd_ref, p_ref, ti_ref, gam_ref):
        hstep = pl.program_id(1)
        mc, ms = _pair_masks()
        gam_ref[...] = jnp.zeros_like(gam_ref)
        heads = []
        for qh in range(qps):
            qq = q_ref[:, qh * HD:(qh + 1) * HD]
            kk_ = k_ref[:, qh * HD:(qh + 1) * HD]
            kk = _dot_nt(kk_, kk_)
            qk = _dot_nt(qq, kk_)
            for j in range(2):
                idx = 2 * qh + j
                cs = slice(idx * HD, (idx + 1) * HD)
                g_r, g_c, b_c, l_r, l_c = _head_cols(rt_ref, 2 * qps * hstep + idx)
                dec = jnp.where(mc, jnp.exp(jnp.where(mc, g_c - g_r, 0.0)), 0.0)
                eg = jnp.exp(g_c)
                p_ref[:, cs] = jnp.where(mc, qk * dec, 0.0)
                qd_ref[:, cs] = eg * qq
                kd_ref[:, cs] = jnp.exp(l_c - g_c) * kk_
                ga, gb = _gam_rows(l_r)
                gam_ref[0, qh, 2 * j:2 * j + 1, :] = ga
                gam_ref[0, qh, 2 * j + 1:2 * j + 2, :] = gb
                rhs = jnp.concatenate([b_c * v_ref[:, cs], b_c * eg * kk_], axis=1)
                heads.append((cs, jnp.where(ms, b_c * kk * dec, 0.0), rhs))
        tinvs = _inv_unit_lower_many([a for _, a, _ in heads])
        uws = [_dot3(t, rhs) for t, (_, _, rhs) in zip(tinvs, heads)]
        for t, uw, (cs, _, _) in zip(tinvs, uws, heads):
            ti_ref[:, cs] = t
            u_ref[:, cs] = uw[:, :HD]
            w_ref[:, cs] = uw[:, HD:]

    big = jax.ShapeDtypeStruct((s, NV * HD), F32)
    return pl.pallas_call(
        body, name="dn_intra", grid=(nb, NQ // qps),
        in_specs=[blk, blk, blk2, pl.BlockSpec((3 * NV, 2 * CH), lambda i, h: (0, i))],
        out_specs=[blk2] * 6 + [pl.BlockSpec((1, qps, 8, 128), lambda i, h: (i, h, 0, 0))],
        out_shape=[big] * 6 + [jax.ShapeDtypeStruct((nb, NQ, 8, 128), F32)],
        compiler_params=_params(("parallel", "parallel")),
    )(q, k, v, rt)


def _dn_seq(u, w, qd, kd, p, gam, pz, nw):
    s = u.shape[0]
    nb = s // (2 * CH)
    wide = pl.BlockSpec((2 * CH, NV * HD), lambda i: (i, 0))

    def body(u_ref, w_ref, qd_ref, kd_ref, p_ref, gam_ref, z_ref, nw_ref, o_ref, vn_ref, ss_ref, y_ref, yt_ref, st):
        @pl.when(pl.program_id(0) == 0)
        def _():
            st[...] = jnp.zeros_like(st)
        ra, rb = slice(0, CH), slice(CH, 2 * CH)
        for g0 in range(0, NV, SEQ_GROUP):
            hs = list(range(g0, g0 + SEQ_GROUP))
            cs = [slice(h * HD, (h + 1) * HD) for h in hs]
            ga = [gam_ref[0, h // 2, 2 * (h % 2):2 * (h % 2) + 1, :] for h in hs]
            gb = [gam_ref[0, h // 2, 2 * (h % 2) + 1:2 * (h % 2) + 2, :] for h in hs]
            s0 = [st[h] for h in hs]
            vna = [u_ref[ra, c] - _dot(w_ref[ra, c], s) for c, s in zip(cs, s0)]
            s1 = [g * s + _dot_tn(kd_ref[ra, c], v) for g, s, c, v in zip(ga, s0, cs, vna)]
            vnb = [u_ref[rb, c] - _dot(w_ref[rb, c], s) for c, s in zip(cs, s1)]
            s2 = [g * s + _dot_tn(kd_ref[rb, c], v) for g, s, c, v in zip(gb, s1, cs, vnb)]
            for h, s in zip(hs, s2):
                st[h] = s
            oa = [_dot(qd_ref[ra, c], s) for c, s in zip(cs, s0)]
            ob = [_dot(qd_ref[rb, c], s) for c, s in zip(cs, s1)]
            for i_, h in enumerate(hs):
                vn = jnp.concatenate([vna[i_], vnb[i_]], axis=0)
                oh = jnp.concatenate([oa[i_], ob[i_]], axis=0) + _dot(p_ref[:, cs[i_]], vn)
                o_ref[:, cs[i_]] = oh
                vn_ref[:, cs[i_]] = vn
                ss_ref[h, 0:HD, :] = s0[i_]
                ss_ref[h, HD:2 * HD, :] = s1[i_]
                z = z_ref[:, cs[i_]]
                rs = lax.rsqrt(jnp.mean(oh * oh, axis=-1, keepdims=True) + RMS_EPS)
                y = oh * rs * nw_ref[...] * (z * _sigmoid(z))
                y_ref[:, cs[i_]] = y.astype(MXU_DT)
                yt_ref[cs[i_], :] = y.T.astype(MXU_DT)

    big = jax.ShapeDtypeStruct((s, NV * HD), F32)
    return pl.pallas_call(
        body, name="dn_seq", grid=(nb,),
        in_specs=[wide] * 5 + [pl.BlockSpec((1, NQ, 8, 128), lambda i: (i, 0, 0, 0)), wide, _whole((1, HD))],
        out_specs=[wide, wide, pl.BlockSpec((NV, 2 * HD, HD), lambda i: (0, i, 0)), wide, _cols(2 * CH, NV * HD)],
        out_shape=[big, big, jax.ShapeDtypeStruct((NV, 2 * s, HD), F32),
                   jax.ShapeDtypeStruct((s, NV * HD), MXU_DT), jax.ShapeDtypeStruct((NV * HD, s), MXU_DT)],
        scratch_shapes=[pltpu.VMEM((NV, HD, HD), F32)],
        compiler_params=_params(("arbitrary",)),
    )(u, w, qd, kd, p, gam, pz, nw)


def _ln1(x, mix, adam, lng, lnb):
    s = x.shape[0]
    tm = _pick(s, (512, 256, 128))

    def body(x_ref, m_ref, ada_ref, g_ref, b_ref, r_ref, x2_ref, h2_ref, h2t_ref):
        r = ALPHA * x_ref[...] + (1.0 + ada_ref[2:3, :]) * m_ref[...]
        xhat, _ = _ln_stats(r)
        x2 = xhat * g_ref[...] + b_ref[...]
        r_ref[...] = r
        x2_ref[...] = x2
        h2 = x2 * (1.0 + ada_ref[4:5, :]) + ada_ref[3:4, :]
        h2_ref[...] = h2.astype(MXU_DT)
        h2t_ref[...] = h2.T.astype(MXU_DT)

    return pl.pallas_call(
        body, name="ln1", grid=(s // tm,),
        in_specs=[_rows(tm, D), _rows(tm, D), _whole((8, D)), _whole((1, D)), _whole((1, D))],
        out_specs=[_rows(tm, D)] * 3 + [_cols(tm, D)],
        out_shape=[jax.ShapeDtypeStruct((s, D), F32), jax.ShapeDtypeStruct((s, D), F32),
                   jax.ShapeDtypeStruct((s, D), MXU_DT), jax.ShapeDtypeStruct((D, s), MXU_DT)],
        compiler_params=_params(("parallel",)),
    )(x, mix, adam, lng, lnb)


def _ffn_act(gu, cw, cb):
    s = gu.shape[0]
    tm = _pick(s, (256, 128))
    w = D_FF

    def body(g_ref, u_ref, cw_ref, cb_ref, o_ref, ot_ref, gc_ref, prev):
        @pl.when(pl.program_id(0) == 0)
        def _():
            prev[...] = jnp.zeros_like(prev)
        g = g_ref[...]
        gc = _conv_causal(g, prev[...], cw_ref, 3) + cb_ref[...]
        prev[...] = g[tm - 8:, :]
        gc_ref[...] = gc
        act = _gelu(gc) * u_ref[...]
        o_ref[...] = act.astype(MXU_DT)
        ot_ref[...] = act.T.astype(MXU_DT)

    return pl.pallas_call(
        body, name="ffn_act", grid=(s // tm,),
        in_specs=[_rows(tm, w, 0), _rows(tm, w, 1), _whole((8, w)), _whole((1, w))],
        out_specs=[_rows(tm, w), _cols(tm, w), _rows(tm, w)],
        out_shape=[jax.ShapeDtypeStruct((s, w), MXU_DT), jax.ShapeDtypeStruct((w, s), MXU_DT),
                   jax.ShapeDtypeStruct((s, w), F32)],
        scratch_shapes=[pltpu.VMEM((8, w), F32)],
        compiler_params=_params(("arbitrary",)),
    )(gu, gu, cw, cb)


def _ffn_bwd(dact, gu, gc, cw):
    s = dact.shape[0]
    tm = _pick(s, (256, 128))
    nt = s // tm
    w = D_FF

    def body(da_ref, g_ref, u_ref, gc_ref, cw_ref, o_ref, gcw_ref, gcb_ref, nxt):
        @pl.when(pl.program_id(0) == 0)
        def _():
            nxt[...] = jnp.zeros_like(nxt)
            gcw_ref[...] = jnp.zeros_like(gcw_ref)
            gcb_ref[...] = jnp.zeros_like(gcb_ref)
        gel, dgel = _gelu_and_grad(gc_ref[...])
        da = da_ref[...]
        dgc = da * u_ref[...] * dgel
        o_ref[:, w:] = (da * gel).astype(MXU_DT)
        o_ref[:, :w] = _conv_causal_bwd(dgc, nxt[...], cw_ref, 3, g_ref[...], gcw_ref).astype(MXU_DT)
        nxt[...] = dgc[:8, :]
        gcb_ref[...] += _rsum(dgc)

    return pl.pallas_call(
        body, name="ffn_bwd", grid=(nt,),
        in_specs=[_rows(tm, w, 0, nt), _rows(tm, w, 0, nt), _rows(tm, w, 1, nt), _rows(tm, w, 0, nt),
                  _whole((8, w))],
        out_specs=[_rows(tm, 2 * w, 0, nt), _whole((8, w)), _whole((1, w))],
        out_shape=[jax.ShapeDtypeStruct((s, 2 * w), MXU_DT), jax.ShapeDtypeStruct((8, w), F32),
                   jax.ShapeDtypeStruct((1, w), F32)],
        scratch_shapes=[pltpu.VMEM((8, w), F32)],
        compiler_params=_params(("arbitrary",)),
    )(dact, gu, gu, gc, cw)


def _dn_seq_bwd(do, qd, kd, p, w, vn, ssave, gam):
    s = do.shape[0]
    nb = s // (2 * CH)
    wide = pl.BlockSpec((2 * CH, NV * HD), lambda i: (nb - 1 - i, 0))
    gspec = pl.BlockSpec((1, NQ, 8, 128), lambda i: (nb - 1 - i, 0, 0, 0))

    def body(do_ref, qd_ref, kd_ref, p_ref, w_ref, vn_ref, ss_ref, gam_ref, dvn_ref, dkd_ref, dgam_ref, dst):
        @pl.when(pl.program_id(0) == 0)
        def _():
            dst[...] = jnp.zeros_like(dst)
        dgam_ref[...] = jnp.zeros_like(dgam_ref)
        ra, rb = slice(0, CH), slice(CH, 2 * CH)
        tot = lambda t: jnp.sum(jnp.sum(t, axis=1, keepdims=True), axis=0, keepdims=True)
        for g0 in range(0, NV, SEQ_GROUP):
            hs = list(range(g0, g0 + SEQ_GROUP))
            cs = [slice(h * HD, (h + 1) * HD) for h in hs]
            ga = [gam_ref[0, h // 2, 2 * (h % 2):2 * (h % 2) + 1, :] for h in hs]
            gb = [gam_ref[0, h // 2, 2 * (h % 2) + 1:2 * (h % 2) + 2, :] for h in hs]
            ds2 = [dst[h] for h in hs]
            pdo = [_dot_tn(p_ref[:, c], do_ref[:, c]) for c in cs]
            qdo_b = [_dot_tn(qd_ref[rb, c], do_ref[rb, c]) for c in cs]
            qdo_a = [_dot_tn(qd_ref[ra, c], do_ref[ra, c]) for c in cs]
            dvb = [p_[rb] + _dot(kd_ref[rb, c], d_) for p_, c, d_ in zip(pdo, cs, ds2)]
            ds1 = [g * d_ + q_ - _dot_tn(w_ref[rb, c], v_)
                   for g, d_, q_, c, v_ in zip(gb, ds2, qdo_b, cs, dvb)]
            dva = [p_[ra] + _dot(kd_ref[ra, c], d_) for p_, c, d_ in zip(pdo, cs, ds1)]
            ds0 = [g * d_ + q_ - _dot_tn(w_ref[ra, c], v_)
                   for g, d_, q_, c, v_ in zip(ga, ds1, qdo_a, cs, dva)]
            for h, d_ in zip(hs, ds0):
                dst[h] = d_
            for i_, h in enumerate(hs):
                c = cs[i_]
                row = 2 * (h % 2)
                dkd_ref[rb, c] = _dot_nt(vn_ref[rb, c], ds2[i_])
                dkd_ref[ra, c] = _dot_nt(vn_ref[ra, c], ds1[i_])
                dvn_ref[ra, c] = dva[i_]
                dvn_ref[rb, c] = dvb[i_]
                dgam_ref[0, h // 2, row:row + 1, :] = jnp.broadcast_to(tot(ds1[i_] * ss_ref[h, 0:HD, :]), (1, 128))
                dgam_ref[0, h // 2, row + 1:row + 2, :] = jnp.broadcast_to(
                    tot(ds2[i_] * ss_ref[h, HD:2 * HD, :]), (1, 128))

    big = jax.ShapeDtypeStruct((s, NV * HD), F32)
    return pl.pallas_call(
        body, name="dn_seq_bwd", grid=(nb,),
        in_specs=[wide] * 6 + [pl.BlockSpec((NV, 2 * HD, HD), lambda i: (0, nb - 1 - i, 0)), gspec],
        out_specs=[wide, wide, gspec],
        out_shape=[big, big, jax.ShapeDtypeStruct((nb, NQ, 8, 128), F32)],
        scratch_shapes=[pltpu.VMEM((NV, HD, HD), F32)],
        compiler_params=_params(("arbitrary",)),
    )(do, qd, kd, p, w, vn, ssave, gam)


def _dn_intra_bwd(q, k, v, rt, do, dvn, dkd, gam, dgam, ssave, tinv, u, w, vn):
    s = q.shape[0]
    nb = s // (2 * CH)
    qps = 8
    nh = 2 * qps
    blk = pl.BlockSpec((2 * CH, qps * HD), lambda i, h: (i, h))
    blk2 = pl.BlockSpec((2 * CH, nh * HD), lambda i, h: (i, h))
    gspec = pl.BlockSpec((1, qps, 8, 128), lambda i, h: (i, h, 0, 0))
    rspec = pl.BlockSpec((3 * NV, 2 * CH), lambda i, h: (0, i))

    def body(q_ref, k_ref, v_ref, rt_ref, do_ref, dvn_ref, dkd_ref, gam_ref, dgam_ref, ss_ref,
             ti_ref, u_ref, w_ref, vn_ref, dq_ref, dk_ref, dv_ref, drt_ref, acc):
        hstep = pl.program_id(1)

        @pl.when(hstep == 0)
        def _():
            acc[...] = jnp.zeros_like(acc)
        mc, ms = _pair_masks()
        ra, rb = slice(0, CH), slice(CH, 2 * CH)
        lane = _iota((1, 2 * CH), 1)
        hs = list(range(nh))
        qh = [h // 2 for h in hs]
        cs = [slice(h * HD, (h + 1) * HD) for h in hs]
        qq_ = [q_ref[:, t * HD:(t + 1) * HD] for t in range(qps)]
        kk_ = [k_ref[:, t * HD:(t + 1) * HD] for t in range(qps)]
        kk = [_dot_nt(k_, k_) for k_ in kk_]
        qk = [_dot_nt(q_, k_) for q_, k_ in zip(qq_, kk_)]
        cols = [_head_cols(rt_ref, nh * hstep + h) for h in hs]
        b_c = [c_[2] for c_ in cols]
        dec = [jnp.where(mc, jnp.exp(jnp.where(mc, c_[1] - c_[0], 0.0)), 0.0) for c_ in cols]
        eg = [jnp.exp(c_[1]) for c_ in cols]
        egl = [jnp.exp(c_[4] - c_[1]) for c_ in cols]
        dob = [do_ref[:, c] for c in cs]
        dvb = [dvn_ref[:, c] for c in cs]
        dqd = [jnp.concatenate([_dot_nt(d_[ra], ss_ref[h, 0:HD, :]), _dot_nt(d_[rb], ss_ref[h, HD:2 * HD, :])], axis=0)
               for h, d_ in zip(hs, dob)]
        dw = [-jnp.concatenate([_dot_nt(d_[ra], ss_ref[h, 0:HD, :]), _dot_nt(d_[rb], ss_ref[h, HD:2 * HD, :])], axis=0)
              for h, d_ in zip(hs, dvb)]
        dp = [jnp.where(mc, _dot_nt(d_, vn_ref[:, c]), 0.0) for d_, c in zip(dob, cs)]
        dbuw = [_dot3_tn(ti_ref[:, c], jnp.concatenate([d_, w_], axis=1)) for c, d_, w_ in zip(cs, dvb, dw)]
        dbu = [t[:, :HD] for t in dbuw]
        dbw = [t[:, HD:] for t in dbuw]
        da = [jnp.where(ms, -(_dot_nt(bu, u_ref[:, c]) + _dot_nt(bw, w_ref[:, c])), 0.0)
              for bu, bw, c in zip(dbu, dbw, cs)]
        dm = [a_ * d_ for a_, d_ in zip(da, dec)]
        dn_ = [p_ * d_ for p_, d_ in zip(dp, dec)]
        dbk = [_dot(m_, kk_[t]) for m_, t in zip(dm, qh)]
        dqs = [_dot(n_, kk_[t]) + e_ * q_ for n_, t, e_, q_ in zip(dn_, qh, eg, dqd)]
        dks = [_dot_tn(m_, b_ * kk_[t]) + _dot_tn(n_, qq_[t]) + el * dkd_ref[:, c] + b_ * (e_ * bw + bk)
               for m_, b_, t, n_, el, c, e_, bw, bk in zip(dm, b_c, qh, dn_, egl, cs, eg, dbw, dbk)]
        for t in range(qps):
            dq_ref[:, t * HD:(t + 1) * HD] = dqs[2 * t] + dqs[2 * t + 1]
            dk_ref[:, t * HD:(t + 1) * HD] = dks[2 * t] + dks[2 * t + 1]
        for h in hs:
            c, t, j = cs[h], qh[h], h % 2
            dv_ref[:, c] = b_c[h] * dbu[h]
            e = da[h] * (b_c[h] * kk[t] * dec[h]) + dp[h] * (qk[t] * dec[h])
            x = dkd_ref[:, c] * (egl[h] * kk_[t])
            egk = eg[h] * kk_[t]
            z = e + dqd[h] * (eg[h] * qq_[t]) - x + dbw[h] * (b_c[h] * egk)
            zb = dbw[h] * egk + dbu[h] * v_ref[:, c] + dbk[h] * kk_[t]
            sa = jnp.sum(jnp.sum(x[ra], axis=1, keepdims=True), axis=0, keepdims=True)
            sb = jnp.sum(jnp.sum(x[rb], axis=1, keepdims=True), axis=0, keepdims=True)
            la = sa + dgam_ref[0, t, 2 * j:2 * j + 1, :] * gam_ref[0, t, 2 * j:2 * j + 1, :]
            lb = sb + dgam_ref[0, t, 2 * j + 1:2 * j + 2, :] * gam_ref[0, t, 2 * j + 1:2 * j + 2, :]
            hg = nh * hstep + h
            acc[pl.ds(hg, 1), :] = _rsum(z.T - e)
            acc[pl.ds(NV + hg, 1), :] = _rsum(zb.T)
            acc[pl.ds(2 * NV + hg, 1), :] = jnp.where(lane < CH, la, lb)

        @pl.when(hstep == NQ // qps - 1)
        def _():
            drt_ref[...] = acc[...]

    return pl.pallas_call(
        body, name="dn_intra_bwd", grid=(nb, NQ // qps),
        in_specs=[blk, blk, blk2, rspec, blk2, blk2, blk2, gspec, gspec,
                  pl.BlockSpec((nh, 2 * HD, HD), lambda i, h: (h, i, 0)), blk2, blk2, blk2, blk2],
        out_specs=[blk, blk, blk2, rspec],
        out_shape=[jax.ShapeDtypeStruct((s, NQ * HD), F32), jax.ShapeDtypeStruct((s, NQ * HD), F32),
                   jax.ShapeDtypeStruct((s, NV * HD), F32), jax.ShapeDtypeStruct((3 * NV, s), F32)],
        scratch_shapes=[pltpu.VMEM((3 * NV, 2 * CH), F32)],
        compiler_params=_params(("parallel", "arbitrary")),
    )(q, k, v, rt, do, dvn, dkd, gam, dgam, ssave, tinv, u, w, vn)


def _l2n_heads_bwd(c, d_ref, dc_ref, off, scale):
    for hh in range(NQ):
        cs = slice(off + hh * HD, off + (hh + 1) * HD)
        x = c[:, cs]
        dy = d_ref[:, hh * HD:(hh + 1) * HD]
        r = lax.rsqrt(jnp.sum(x * x, axis=-1, keepdims=True) + L2_EPS)
        dc_ref[:, cs] = (scale * r) * (dy - x * (r * r) * jnp.sum(dy * x, axis=-1, keepdims=True))


def _dn_prep_bwd(dq, dk, dv, drt, pqkv, cpre, pab, cw, acol, dcol, dproj):
    s = pqkv.shape[0]
    tm = 128
    nt = s // tm
    wq = NQ * HD

    def body(dq_ref, dk_ref, dv_ref, drt_ref, x_ref, cp_ref, ab_ref, cw_ref, ac_ref,
             dc_ref, _, dx_ref, dab_ref, gcw_ref, gsc_ref, dcs, nxt):
        @pl.when(pl.program_id(0) == 0)
        def _():
            nxt[...] = jnp.zeros_like(nxt)
            gcw_ref[...] = jnp.zeros_like(gcw_ref)
            gsc_ref[...] = jnp.zeros_like(gsc_ref)
        cp = cp_ref[...]
        sg = _sigmoid(cp)
        c = cp * sg
        _l2n_heads_bwd(c, dq_ref, dcs, 0, Q_SCALE)
        _l2n_heads_bwd(c, dk_ref, dcs, wq, 1.0)
        dcs[:, 2 * wq:] = dv_ref[...]
        dcp = dcs[...] * (sg * (1.0 + cp * (1.0 - sg)))
        dx_ref[...] = _conv_causal_bwd(dcp, nxt[...], cw_ref, 4, x_ref[...], gcw_ref).astype(MXU_DT)
        nxt[...] = dcp[:8, :]
        lane = _iota((NV, tm), 1)
        dgt = drt_ref[0:NV, :] + jnp.where((lane & (CH - 1)) == CH - 1, drt_ref[2 * NV:3 * NV, :], 0.0)
        dg = _chunk_cumsum(dgt, 1, rev=True)
        abt = ab_ref[...].T
        zt = abt[0:NV, :] + dc_ref[...]
        gt = -jnp.exp(ac_ref[...]) * _softplus(zt)
        dat = dg * (-jnp.exp(ac_ref[...])) * _sigmoid(zt)
        bt = _sigmoid(abt[NV:2 * NV, :])
        dbt = drt_ref[NV:2 * NV, :] * bt * (1.0 - bt)
        full = jnp.concatenate([dat, dbt, jnp.zeros((128 - 2 * NV, tm), F32)], axis=0)
        dab_ref[...] = full.T.astype(MXU_DT)
        l2 = _iota((NV, 128), 1)
        gsc_ref[...] += jnp.where(l2 == 0, jnp.sum(dg * gt, axis=1, keepdims=True),
                                  jnp.where(l2 == 1, jnp.sum(dat, axis=1, keepdims=True), 0.0))

    return pl.pallas_call(
        body, name="dn_prep_bwd", grid=(nt,),
        in_specs=[_rows(tm, wq, 0, nt), _rows(tm, wq, 0, nt), _rows(tm, 2 * wq, 0, nt),
                  pl.BlockSpec((3 * NV, tm), lambda i: (0, nt - 1 - i)),
                  _rows(tm, 4 * wq, 0, nt), _rows(tm, 4 * wq, 0, nt), _rows(tm, 128, 0, nt),
                  _whole((8, 4 * wq)), _whole((NV, 1)), _whole((NV, 1)), _HBM],
        out_specs=[_rows(tm, 4 * wq, 0, nt), _rows(tm, 128, 0, nt), _whole((8, 4 * wq)), _whole((NV, 128))],
        out_shape=[jax.ShapeDtypeStruct(dproj.shape, MXU_DT), jax.ShapeDtypeStruct((s, 128), MXU_DT),
                   jax.ShapeDtypeStruct((8, 4 * wq), F32), jax.ShapeDtypeStruct((NV, 128), F32)],
        scratch_shapes=[pltpu.VMEM((tm, 4 * wq), F32), pltpu.VMEM((8, 4 * wq), F32)],
        compiler_params=_params(("arbitrary",)), input_output_aliases={10: 0},
    )(dq, dk, dv, drt, pqkv, cpre, pab, cw, acol, dcol, dproj)


def _rg_bwd(drec, px, xcs, h, cw, vec, wa, wx, wat, wxt, side=None):
    s = px.shape[0]
    tm = _pick(s, (256, 128))
    nt = s // tm
    w = D_RNN
    n_in = len(side.ins) if side else 0
    n_out = len(side.out_shapes) if side else 0

    def body(*refs):
        (dr_ref, xr_ref, xc_ref, gr_ref, h_ref, hb_ref, cw_ref, vec_ref, wa_ref, wx_ref, wat_ref,
         wxt_ref) = refs[:12]
        s_in = refs[12:12 + n_in]
        o_ref, gwa_ref, gwx_ref, gcw_ref, gvec_ref = refs[12 + n_in:17 + n_in]
        s_out = refs[17 + n_in:17 + n_in + n_out]
        nxt_a, nxt_l, nxt_d = refs[17 + n_in + n_out:20 + n_in + n_out]
        sems = refs[20 + n_in + n_out:]
        i = pl.program_id(0)
        if side:
            @pl.when(i == 0)
            def _():
                side.start(s_in, s_out, sems)

        @pl.when(i == 0)
        def _():
            nxt_a[...] = jnp.zeros_like(nxt_a)
            nxt_l[...] = jnp.zeros_like(nxt_l)
            nxt_d[...] = jnp.zeros_like(nxt_d)
            gwa_ref[...] = jnp.zeros_like(gwa_ref)
            gwx_ref[...] = jnp.zeros_like(gwx_ref)
            gcw_ref[...] = jnp.zeros_like(gcw_ref)
            gvec_ref[...] = jnp.zeros_like(gvec_ref)
        hbefore = jnp.where(i == nt - 1, 0.0, hb_ref[...])
        xc = xc_ref[...]
        r, ig, a, mult, sp, rmult = _rg_gates(xc, wa_ref, wx_ref, vec_ref)
        hh = h_ref[...]
        gel, dgel = _gelu_and_grad(gr_ref[...])
        drec_ = dr_ref[...]
        o_ref[:, w:] = (drec_ * hh * dgel).astype(MXU_DT)
        lam = _scan_rev(_shift_up(a, nxt_a[...], 1), drec_ * gel, nxt_l[0:1, :])
        nxt_a[...] = a[:8, :]
        nxt_l[...] = lam[:8, :]
        da = lam * _shift_down(hh, hbefore, 1)
        dxc = lam * mult * ig
        dla = da * a - (lam * ig * xc) * (a * a) * rmult
        dpr = dla * (-RG_C * sp) * r * (1.0 - r)
        dpi = (lam * mult * xc) * ig * (1.0 - ig)
        dprb = dpr.astype(MXU_DT)
        dpib = dpi.astype(MXU_DT)
        dxc = dxc + jnp.dot(dprb, wat_ref[...], preferred_element_type=F32) \
                  + jnp.dot(dpib, wxt_ref[...], preferred_element_type=F32)
        xcb = xc.astype(MXU_DT)
        for rb_ in range(w // 128):
            lo, hi = max(0, 128 * (rb_ - 1)), min(w, 128 * (rb_ + 2))
            rows_ = slice(128 * rb_, 128 * (rb_ + 1))
            gwa_ref[rows_, lo:hi] += _dot_tn(xcb[:, rows_], dprb[:, lo:hi])
            gwx_ref[rows_, lo:hi] += _dot_tn(xcb[:, rows_], dpib[:, lo:hi])
        o_ref[:, :w] = _conv_causal_bwd(dxc, nxt_d[...], cw_ref, 4, xr_ref[...], gcw_ref).astype(MXU_DT)
        nxt_d[...] = dxc[:8, :]
        gvec_ref[0:1, :] += _rsum(dxc)
        gvec_ref[1:2, :] += _rsum(dpr)
        gvec_ref[2:3, :] += _rsum(dpi)
        gvec_ref[3:4, :] += _rsum(dla * (-RG_C * r)) * (-_sigmoid(-vec_ref[3:4, :]))
        if side:
            @pl.when(i == nt - 1)
            def _():
                side.finish(s_in, s_out, sems)

    return pl.pallas_call(
        body, name="rg_bwd", grid=(nt,),
        in_specs=[_rows(tm, w, 0, nt), _rows(tm, w, 0, nt), _rows(tm, w, 0, nt), _rows(tm, w, 1, nt),
                  _rows(tm, w, 0, nt), _before(tm, w, 0, nt), _whole((8, w)), _whole((8, w)),
                  _whole((w, w)), _whole((w, w)), _whole((w, w)), _whole((w, w))] + [_HBM] * n_in,
        out_specs=[_rows(tm, 2 * w, 0, nt), _whole((w, w)), _whole((w, w)), _whole((8, w)), _whole((8, w))]
        + [_HBM] * n_out,
        out_shape=[jax.ShapeDtypeStruct((s, 2 * w), MXU_DT), jax.ShapeDtypeStruct((w, w), F32),
                   jax.ShapeDtypeStruct((w, w), F32), jax.ShapeDtypeStruct((8, w), F32),
                   jax.ShapeDtypeStruct((8, w), F32)] + (list(side.out_shapes) if side else []),
        scratch_shapes=[pltpu.VMEM((8, w), F32)] * 3 + (list(side.sems) if side else []),
        compiler_params=_params(("arbitrary",)),
    )(drec, px, xcs, px, h, h, cw, vec, wa, wx, wat, wxt, *(side.ins if side else []))


def _mm_epi(a, b, *, name, epi, extras=(), wholes=(), row_outs=(), acc_outs=(), tm=None, side=None):
    m, kk = a.shape
    n = b.shape[1]
    tm = min(tm, m) if tm else _pick(m, (512, 256, 128))
    tk = kk if kk <= 2816 else _pick(kk, (2816, 2176, 2048, 1024))
    nk, ni = kk // tk, m // tm
    n_ex, n_wh, n_ro, n_ao = len(extras), len(wholes), len(row_outs), len(acc_outs)
    n_si = len(side.ins) if side else 0
    n_so = len(side.out_shapes) if side else 0
    placed = [(j_, r_[2]) for j_, r_ in enumerate(row_outs) if len(r_) == 3]

    def body(*refs):
        a_ref, b_ref = refs[:2]
        p = 2
        ex, p = refs[p:p + n_ex], p + n_ex
        wh, p = refs[p:p + n_wh], p + n_wh
        s_in, p = refs[p:p + n_si], p + n_si
        p += len(placed)
        ro, p = refs[p:p + n_ro], p + n_ro
        ao, p = refs[p:p + n_ao], p + n_ao
        s_out, p = refs[p:p + n_so], p + n_so
        i, k = pl.program_id(0), pl.program_id(1)
        first = (i == 0) & (k == 0)
        if side:
            sems = refs[p + (1 if nk > 1 else 0):]

            @pl.when(first)
            def _():
                side.start(s_in, s_out, sems)
        if n_ao:
            @pl.when(first)
            def _():
                for r_ in ao:
                    r_[...] = jnp.zeros_like(r_)
        if nk == 1:
            epi(_dot(a_ref[...], b_ref[...]), ex, wh, ro, ao)
        else:
            acc = refs[p]

            @pl.when(k == 0)
            def _():
                acc[...] = jnp.zeros_like(acc)
            acc[...] += _dot(a_ref[...], b_ref[...])

            @pl.when(k == nk - 1)
            def _():
                epi(acc[...], ex, wh, ro, ao)
        if side:
            @pl.when((i == ni - 1) & (k == nk - 1))
            def _():
                side.finish(s_in, s_out, sems)

    in_specs = [pl.BlockSpec((tm, tk), lambda i, k: (i, k)), pl.BlockSpec((tk, n), lambda i, k: (k, 0))]
    in_specs += [pl.BlockSpec((tm, w_), functools.partial(lambda i, k, c_: (i, c_), c_=c_)) for _, w_, c_ in extras]
    in_specs += [_whole(x_.shape) for x_ in wholes] + [_HBM] * (n_si + len(placed))
    out_specs, out_shape = [], []
    for r_ in row_outs:
        blk = r_[2][1] if len(r_) == 3 else 0
        out_specs.append(pl.BlockSpec((tm, r_[0]), functools.partial(lambda i, k, c_: (i, c_), c_=blk)))
        out_shape.append(jax.ShapeDtypeStruct(r_[2][0].shape if len(r_) == 3 else (m, r_[0]), r_[1]))
    out_specs += [_whole(sh) for sh in acc_outs] + [_HBM] * n_so
    out_shape += [jax.ShapeDtypeStruct(sh, F32) for sh in acc_outs] + (list(side.out_shapes) if side else [])
    scratch = ([pltpu.VMEM((tm, n), F32)] if nk > 1 else []) + (list(side.sems) if side else [])
    first_placed = 2 + n_ex + n_wh + n_si
    return pl.pallas_call(
        body, name=name, grid=(ni, nk), in_specs=in_specs, out_specs=out_specs, out_shape=out_shape,
        scratch_shapes=scratch, compiler_params=_params(("arbitrary", "arbitrary")),
        input_output_aliases={first_placed + q_: j_ for q_, (j_, _) in enumerate(placed)},
    )(a, b, *[x_ for x_, _, _ in extras], *wholes, *(side.ins if side else []), *[pb_[0] for _, pb_ in placed])


def _mm_pro(b, *, name, pro, m, extras=(), wholes=(), row_outs=(), col_outs=(), tm=256):
    kk, n = b.shape
    tn = _pick(n, (1408, 1024, 512, 256, 128))
    tm = min(tm, m)
    n_ex, n_wh, n_ro, n_co = len(extras), len(wholes), len(row_outs), len(col_outs)

    def body(*refs):
        ex = refs[:n_ex]
        wh = refs[n_ex:n_ex + n_wh]
        b_ref, o_ref = refs[n_ex + n_wh], refs[n_ex + n_wh + 1]
        ro = refs[n_ex + n_wh + 2:n_ex + n_wh + 2 + n_ro]
        co = refs[n_ex + n_wh + 2 + n_ro:n_ex + n_wh + 2 + n_ro + n_co]
        a_scr = refs[-1]

        @pl.when(pl.program_id(1) == 0)
        def _():
            a_scr[...] = pro(ex, wh, ro, co).astype(MXU_DT)
        o_ref[...] = jnp.dot(a_scr[...], b_ref[...], preferred_element_type=F32)

    in_specs = [pl.BlockSpec((tm, w_), functools.partial(lambda i, j, c_: (i, c_), c_=c_)) for _, w_, c_ in extras]
    in_specs += [_whole(x_.shape) for x_ in wholes] + [pl.BlockSpec((kk, tn), lambda i, j: (0, j))]
    out_specs = [pl.BlockSpec((tm, tn), lambda i, j: (i, j))]
    out_specs += [pl.BlockSpec((tm, w_), lambda i, j: (i, 0)) for w_, _ in row_outs]
    out_specs += [pl.BlockSpec((w_, tm), lambda i, j: (0, i)) for w_, _ in col_outs]
    out_shape = [jax.ShapeDtypeStruct((m, n), F32)] + [jax.ShapeDtypeStruct((m, w_), dt) for w_, dt in row_outs]
    out_shape += [jax.ShapeDtypeStruct((w_, m), dt) for w_, dt in col_outs]
    return pl.pallas_call(
        body, name=name, grid=(m // tm, n // tn), in_specs=in_specs, out_specs=out_specs, out_shape=out_shape,
        scratch_shapes=[pltpu.VMEM((tm, kk), MXU_DT)], compiler_params=_params(("arbitrary", "arbitrary")),
    )(*[x_ for x_, _, _ in extras], *wholes, b.astype(MXU_DT))


def _merge_proj_out(pg, ya, yb, w_out):
    def pro(ex, wh, ro, co):
        ga_ref, gb_ref, ya_ref, yb_ref = ex
        mg = _sigmoid(ga_ref[...]) * ya_ref[...] + _sigmoid(gb_ref[...]) * yb_ref[...]
        co[0][...] = mg.T.astype(MXU_DT)
        return mg

    return _mm_pro(w_out, name="merge_proj_out", pro=pro, m=ya.shape[0],
                   extras=[(pg, D, 0), (pg, D, 1), (ya, D, 0), (yb, D, 0)], col_outs=[(D, MXU_DT)])


def _ffn_down_ln2_loss(act, w_down, x2, tgt, adam, lng, lnb):
    def epi(ff_, ex, wh, ro, ao):
        x_ref, t_ref = ex
        ada_ref, g_ref, b_ref = wh
        dff_ref, dx_ref = ro
        red_ref, = ao
        r = ALPHA * x_ref[...] + (1.0 + ada_ref[5:6, :]) * ff_
        xhat, rstd = _ln_stats(r)
        err = xhat * g_ref[...] + b_ref[...] - t_ref[...]
        dy = err * (1.0 / D)
        dr = _ln_bwd(dy, xhat, rstd, g_ref[...])
        dff_ref[...] = ((1.0 + ada_ref[5:6, :]) * dr).astype(MXU_DT)
        dx_ref[...] = ALPHA * dr
        red_ref[0:1, :] += _rsum(dy * xhat)
        red_ref[1:2, :] += _rsum(dy)
        red_ref[2:3, :] += _rsum(dr * ff_)
        red_ref[3:4, :] += jnp.sum(_rsum(err * err), axis=1, keepdims=True) * (0.5 / D)

    return _mm_epi(act, w_down, name="ffn_down_ln2_loss", epi=epi, extras=[(x2, D, 0), (tgt, D, 0)],
                   wholes=[adam, lng, lnb], row_outs=[(D, MXU_DT), (D, F32)], acc_outs=[(8, D)])


def _d_h1_modulate_bwd(dproj, w_t, dxa, x, adam, side):
    def epi(dh, ex, wh, ro, ao):
        dxa_ref, x_ref = ex
        ada_ref, = wh
        ro[0][...] = dxa_ref[...] + dh * (1.0 + ada_ref[1:2, :])
        ao[0][0:1, :] += _rsum(dh * x_ref[...])
        ao[0][1:2, :] += _rsum(dh)

    return _mm_epi(dproj, w_t, name="d_h1", epi=epi, extras=[(dxa, D, 0), (x, D, 0)], wholes=[adam],
                   row_outs=[(D, F32)], acc_outs=[(8, D)], side=side)


def _d_merged_bwd(dmix, w_out_t, pg, ya, yb, dproj):
    def epi(d, ex, wh, ro, ao):
        ga_ref, gb_ref, ya_ref, yb_ref = ex
        dya_ref, dyb_ref, dpg_ref = ro
        sa = _sigmoid(ga_ref[...])
        sb = _sigmoid(gb_ref[...])
        dya_ref[...] = (d * sa).astype(MXU_DT)
        dyb_ref[...] = (d * sb).astype(MXU_DT)
        dpg_ref[:, :D] = (d * ya_ref[...] * sa * (1.0 - sa)).astype(MXU_DT)
        dpg_ref[:, D:] = (d * yb_ref[...] * sb * (1.0 - sb)).astype(MXU_DT)

    return _mm_epi(dmix, w_out_t, name="d_merged", epi=epi,
                   extras=[(pg, D, 0), (pg, D, 1), (ya, D, 0), (yb, D, 0)],
                   row_outs=[(D, MXU_DT), (D, MXU_DT), (2 * D, MXU_DT, (dproj, 3))])


def _d_dn_post_bwd(dyb, w_pb_t, o, pz, nw, dproj):
    def epi(d_all, ex, wh, ro, ao):
        o_ref, z_ref = ex
        nw_ref, = wh
        do_ref, dz_ref = ro
        acc = jnp.zeros((1, HD), F32)
        for h in range(NV):
            cs = slice(h * HD, (h + 1) * HD)
            oh = o_ref[:, cs]
            z = z_ref[:, cs]
            d = d_all[:, cs]
            sg = _sigmoid(z)
            rs = lax.rsqrt(jnp.mean(oh * oh, axis=-1, keepdims=True) + RMS_EPS)
            n = oh * rs
            dz_ref[:, cs] = (d * n * nw_ref[...] * sg * (1.0 + z * (1.0 - sg))).astype(MXU_DT)
            dn_ = d * (z * sg)
            acc = acc + _rsum(dn_ * n)
            dnn = dn_ * nw_ref[...]
            do_ref[:, cs] = rs * (dnn - n * jnp.mean(dnn * n, axis=-1, keepdims=True))
        ao[0][...] += acc

    return _mm_epi(dyb, w_pb_t, name="d_dn", epi=epi, extras=[(o, NV * HD, 0), (pz, NV * HD, 0)], wholes=[nw],
                   row_outs=[(NV * HD, F32), (NV * HD, MXU_DT, (dproj, 2))], acc_outs=[(1, HD)], tm=512)


def _d_h2_ln1_bwd(dgu, w_gu_t, dx2a, x2, r1, mix, adam, lng):
    def epi(dh, ex, wh, ro, ao):
        dxa_ref, x2_ref, r_ref, m_ref = ex
        ada_ref, g_ref = wh
        dm_ref, dx_ref = ro
        red_ref, = ao
        dx2 = dxa_ref[...] + dh * (1.0 + ada_ref[4:5, :])
        xhat, rstd = _ln_stats(r_ref[...])
        dr = _ln_bwd(dx2, xhat, rstd, g_ref[...])
        dm_ref[...] = ((1.0 + ada_ref[2:3, :]) * dr).astype(MXU_DT)
        dx_ref[...] = ALPHA * dr
        red_ref[0:1, :] += _rsum(dh * x2_ref[...])
        red_ref[1:2, :] += _rsum(dh)
        red_ref[2:3, :] += _rsum(dx2 * xhat)
        red_ref[3:4, :] += _rsum(dx2)
        red_ref[4:5, :] += _rsum(dr * m_ref[...])

    return _mm_epi(dgu, w_gu_t, name="d_h2", epi=epi,
                   extras=[(dx2a, D, 0), (x2, D, 0), (r1, D, 0), (mix, D, 0)], wholes=[adam, lng],
                   row_outs=[(D, MXU_DT), (D, F32)], acc_outs=[(8, D)], tm=512)


def _adamw(parts, w, m, v, name):
    r, c = w.shape
    n_parts = len(parts)
    tm = _row_tile(r, c * 4 * (n_parts + 7))
    c1 = 1.0 - ADAM_B1 ** ADAM_STEP
    c2 = 1.0 - ADAM_B2 ** ADAM_STEP

    def body(*refs):
        g = refs[0][...]
        for p_ref in refs[1:n_parts]:
            g = g + p_ref[...]
        w_ref, m_ref, v_ref, g_out, d_out, m_out, v_out = refs[n_parts:]
        mn = ADAM_B1 * m_ref[...] + (1.0 - ADAM_B1) * g
        vn = ADAM_B2 * v_ref[...] + (1.0 - ADAM_B2) * (g * g)
        g_out[...] = g
        m_out[...] = mn
        v_out[...] = vn
        d_out[...] = -ADAM_LR * ((mn / c1) / (jnp.sqrt(vn / c2) + ADAM_EPS) + ADAM_WD * w_ref[...])

    spec = pl.BlockSpec((tm, c), lambda i: (i, 0))
    return pl.pallas_call(
        body, name=name, grid=(r // tm,),
        in_specs=[spec] * (n_parts + 3), out_specs=[spec] * 4,
        out_shape=[jax.ShapeDtypeStruct((r, c), F32)] * 4, compiler_params=_params(("parallel",)),
    )(*parts, w, m, v)


def _row_tile(r, bytes_per_row):
    for t in (512, 256, 128, 64, 32, 16):
        if r % t == 0 and 2 * t * bytes_per_row <= 20 * 1024 * 1024:
            return t
    return _pick(r, (16, 8))


def _sum_partials(own, recv, name):
    r, c = own.shape
    tm = _row_tile(r, c * (4 + 4 + 3 * 2))

    def body(o_ref, r_ref, out_ref):
        out_ref[...] = ((o_ref[...] + r_ref[0].astype(F32)) + r_ref[1].astype(F32)) + r_ref[2].astype(F32)

    return pl.pallas_call(
        body, name=name, grid=(r // tm,),
        in_specs=[pl.BlockSpec((tm, c), lambda i: (i, 0)), pl.BlockSpec((3, tm, c), lambda i: (0, i, 0))],
        out_specs=pl.BlockSpec((tm, c), lambda i: (i, 0)),
        out_shape=jax.ShapeDtypeStruct((r, c), F32), compiler_params=_params(("parallel",)),
    )(own, recv)


def _sum8(g, name):
    _, r, c = g.shape
    tm = _pick(r, (256, 128, 64, 32, 16, 8))

    def body(g_ref, o_ref):
        acc = g_ref[0]
        for k in range(1, 8):
            acc = acc + g_ref[k]
        o_ref[...] = acc

    return pl.pallas_call(
        body, name=name, grid=(r // tm,),
        in_specs=[pl.BlockSpec((8, tm, c), lambda i: (0, i, 0))],
        out_specs=pl.BlockSpec((tm, c), lambda i: (i, 0)),
        out_shape=jax.ShapeDtypeStruct((r, c), F32), compiler_params=_params(("parallel",)),
    )(g)


def _silu_rows(x):
    def body(x_ref, o_ref):
        xx = x_ref[...]
        o_ref[...] = xx * _sigmoid(xx)

    return pl.pallas_call(body, name="silu_rows", out_shape=jax.ShapeDtypeStruct(x.shape, F32))(x)


def _coords():
    return lax.axis_index("x"), lax.axis_index("y"), lax.axis_index("c")


def _allgather8(v, name):
    r, n = v.shape

    def body(v_ref, out_ref, send_sems, recv_sems):
        x, y, c = _coords()
        me = 4 * x + 2 * y + c
        out_ref[me] = v_ref[...]
        peers = []
        for k in range(1, 8):
            px = 1 - x if k & 4 else x
            py = 1 - y if k & 2 else y
            pc = 1 - c if k & 1 else c
            peers.append((px, py, pc))

        def copy(k, slot, to):
            return pltpu.make_async_remote_copy(
                src_ref=v_ref, dst_ref=out_ref.at[slot], send_sem=send_sems.at[k], recv_sem=recv_sems.at[k],
                device_id=to, device_id_type=MESH)

        sends = [copy(k, me, p) for k, p in enumerate(peers)]
        for cp in sends:
            cp.start()
        for k, (px, py, pc) in enumerate(peers):
            copy(k, 4 * px + 2 * py + pc, (px, py, pc)).wait_recv()
        for cp in sends:
            cp.wait_send()

    return pl.pallas_call(
        body, name=name, out_shape=jax.ShapeDtypeStruct((8, r, n), v.dtype),
        in_specs=[pl.BlockSpec(memory_space=pltpu.VMEM)], out_specs=pl.BlockSpec(memory_space=pltpu.VMEM),
        scratch_shapes=[pltpu.SemaphoreType.DMA((7,)), pltpu.SemaphoreType.DMA((7,))],
        compiler_params=pltpu.CompilerParams(vmem_limit_bytes=VMEM_LIMIT),
    )(v)


def _other_chips(x, y):
    return [(1 - x, y), (x, 1 - y), (1 - x, 1 - y)]


_HBM = pl.BlockSpec(memory_space=pl.ANY)


class _Exchange:
    def __init__(self, ins, out_shapes, sems, start, finish):
        self.ins, self.out_shapes, self.sems, self.start, self.finish = ins, out_shapes, sems, start, finish


def _run_exchange(ex, name):
    n_in, n_out = len(ex.ins), len(ex.out_shapes)

    def body(*refs):
        ins, outs, sems = refs[:n_in], refs[n_in:n_in + n_out], refs[n_in + n_out:]
        ex.start(ins, outs, sems)
        ex.finish(ins, outs, sems)

    return pl.pallas_call(body, name=name, out_shape=list(ex.out_shapes), in_specs=[_HBM] * n_in,
                          out_specs=[_HBM] * n_out, scratch_shapes=list(ex.sems))(*ex.ins)


def _gather_exchange(shards):
    n = len(shards)

    def plan(ins, outs, sems):
        send_sems, recv_sems, local_sems = sems
        x, y, c = _coords()
        s_me = 2 * x + y
        chips = _other_chips(x, y)

        def half(i, slot, hc):
            rh = shards[i].shape[0] // 2
            return outs[i].at[slot, pl.ds(pl.multiple_of(hc * rh, 16), rh), :]

        def copy(i, k, src, dst, to):
            return pltpu.make_async_remote_copy(src_ref=src, dst_ref=dst, send_sem=send_sems.at[6 * i + k],
                                                recv_sem=recv_sems.at[6 * i + k], device_id=to, device_id_type=MESH)

        local = [pltpu.make_async_copy(ins[i], outs[i].at[s_me], local_sems.at[i]) for i in range(n)]
        first = []
        for i in range(n):
            rh = shards[i].shape[0] // 2
            my_half = ins[i].at[pl.ds(pl.multiple_of(c * rh, 16), rh), :]
            first += [copy(i, j, my_half, half(i, s_me, c), (px, py, c)) for j, (px, py) in enumerate(chips)]
        return (x, y, c), chips, half, copy, local, first

    def start(ins, outs, sems):
        _, _, _, _, local, first = plan(ins, outs, sems)
        for cp in local + first:
            cp.start()

    def finish(ins, outs, sems):
        (x, y, c), chips, half, copy, local, first = plan(ins, outs, sems)
        sibling = (x, y, 1 - c)
        passed = []
        for i in range(n):
            for j, (px, py) in enumerate(chips):
                land = half(i, 2 * px + py, c)
                copy(i, j, land, land, (px, py, c)).wait_recv()
                fw = copy(i, 3 + j, land, land, sibling)
                fw.start()
                passed.append(fw)
        for i in range(n):
            for j, (px, py) in enumerate(chips):
                land = half(i, 2 * px + py, 1 - c)
                copy(i, 3 + j, land, land, sibling).wait_recv()
        for cp in first + passed:
            cp.wait_send()
        for cp in local:
            cp.wait()

    return _Exchange(list(shards), [jax.ShapeDtypeStruct((4,) + v.shape, v.dtype) for v in shards],
                     [pltpu.SemaphoreType.DMA((6 * n,)), pltpu.SemaphoreType.DMA((6 * n,)),
                      pltpu.SemaphoreType.DMA((n,))], start, finish)


def _scatter_exchange(gs):
    n = len(gs)

    def copies(ins, outs, sems):
        send_sems, recv_sems = sems
        x, y, c = _coords()
        return [pltpu.make_async_remote_copy(
                    src_ref=ins[i].at[2 * px + py], dst_ref=outs[i].at[j], send_sem=send_sems.at[3 * i + j],
                    recv_sem=recv_sems.at[3 * i + j], device_id=(px, py, c), device_id_type=MESH)
                for i in range(n) for j, (px, py) in enumerate(_other_chips(x, y))]

    def start(ins, outs, sems):
        for cp in copies(ins, outs, sems):
            cp.start()

    def finish(ins, outs, sems):
        cps = copies(ins, outs, sems)
        for cp in cps:
            cp.wait_recv()
        for cp in cps:
            cp.wait_send()

    return _Exchange(list(gs), [jax.ShapeDtypeStruct((3,) + g.shape[1:], g.dtype) for g in gs],
                     [pltpu.SemaphoreType.DMA((3 * n,)), pltpu.SemaphoreType.DMA((3 * n,))], start, finish)


def _allgather_exchange(v):
    def copies(ins, outs, sems):
        send_sems, recv_sems, _ = sems
        x, y, c = _coords()
        me = 4 * x + 2 * y + c
        cps = []
        for k in range(1, 8):
            peer = (1 - x if k & 4 else x, 1 - y if k & 2 else y, 1 - c if k & 1 else c)
            cps.append(pltpu.make_async_remote_copy(
                src_ref=ins[0], dst_ref=outs[0].at[me], send_sem=send_sems.at[k - 1], recv_sem=recv_sems.at[k - 1],
                device_id=peer, device_id_type=MESH))
        return me, cps

    def start(ins, outs, sems):
        me, cps = copies(ins, outs, sems)
        pltpu.make_async_copy(ins[0], outs[0].at[me], sems[2]).start()
        for cp in cps:
            cp.start()

    def finish(ins, outs, sems):
        me, cps = copies(ins, outs, sems)
        for cp in cps:
            cp.wait_recv()
        for cp in cps:
            cp.wait_send()
        pltpu.make_async_copy(ins[0], outs[0].at[me], sems[2]).wait()

    return _Exchange([v], [jax.ShapeDtypeStruct((8,) + v.shape, v.dtype)],
                     [pltpu.SemaphoreType.DMA((7,)), pltpu.SemaphoreType.DMA((7,)), pltpu.SemaphoreType.DMA],
                     start, finish)


def _swap_sibling(vs):
    n = len(vs)

    def body(*refs):
        ins, outs = refs[:n], refs[n:2 * n]
        send_sems, recv_sems = refs[2 * n:]
        x, y, c = _coords()
        cps = [pltpu.make_async_remote_copy(src_ref=ins[i], dst_ref=outs[i], send_sem=send_sems.at[i],
                                            recv_sem=recv_sems.at[i], device_id=(x, y, 1 - c), device_id_type=MESH)
               for i in range(n)]
        for cp in cps:
            cp.start()
        for cp in cps:
            cp.wait()

    return pl.pallas_call(
        body, name="swap_sibling", out_shape=[jax.ShapeDtypeStruct(v.shape, v.dtype) for v in vs],
        in_specs=[_HBM] * n, out_specs=[_HBM] * n,
        scratch_shapes=[pltpu.SemaphoreType.DMA((n,)), pltpu.SemaphoreType.DMA((n,))],
    )(*vs)


def _pad_rows(a, rows):
    return jnp.pad(a, ((0, rows - a.shape[0]), (0, 0)))


def _block_diag(w):
    eye = jnp.eye(RG_BLOCKS, dtype=w.dtype)
    return (eye[:, None, :, None] * w[:, :, None, :]).reshape(D_RNN, D_RNN)


def _diag_blocks(g):
    g4 = g.reshape(RG_BLOCKS, RG_BW, RG_BLOCKS, RG_BW)
    idx = jnp.arange(RG_BLOCKS)
    return g4[idx, :, idx, :]


def _prepare_rest(p):
    w = {}
    for k_, n_ in (("pa", "w_proj_a"), ("pb", "w_proj_b"), ("out", "w_out"), ("down", "ffn_w_down")):
        w[k_] = p[n_].astype(MXU_DT)
        w[k_ + "_t"] = w[k_].T
    w["gu"] = jnp.concatenate([p["ffn_w_gate"], p["ffn_w_up"]], axis=1).astype(MXU_DT)
    w["gu_t"] = w["gu"].T
    return w


def _prepare_first(p):
    w = {}
    wi = p["w_in"].astype(MXU_DT)
    cat = jnp.concatenate([wi[:, 2560:6656], wi[:, 6656:8704], wi[:, 8736:10784], wi[:, 0:2560],
                           wi[:, 8704:8736], jnp.zeros((D, 96), MXU_DT)], axis=1)
    w["in_cat"], w["in_cat_t"] = cat, cat.T
    w["rg_cw"] = _pad_rows(p["rg_conv_w"], 8)
    w["dn_cw"] = _pad_rows(p["dn_conv_w"], 8)
    w["ffn_cw"] = _pad_rows(p["ffn_conv_w"], 8)
    w["rg_vec"] = _pad_rows(jnp.stack([p["rg_conv_b"], p["rg_b_a"], p["rg_b_x"], p["rg_lambda"]]), 8)
    w["wa"] = _block_diag(p["rg_w_a"]).astype(MXU_DT)
    w["wx"] = _block_diag(p["rg_w_x"]).astype(MXU_DT)
    w["wa_t"], w["wx_t"] = w["wa"].T, w["wx"].T
    w["arow"] = jnp.pad(p["dn_a_log"], (0, 128 - NV))[None, :]
    w["drow"] = jnp.pad(p["dn_dt_bias"], (0, 128 - NV))[None, :]
    w["acol"] = p["dn_a_log"][:, None]
    w["dcol"] = p["dn_dt_bias"][:, None]
    w["nw"] = p["dn_norm_w"][None, :]
    w["ffn_cb"] = p["ffn_conv_b"][None, :]
    for n_ in ("ln1_g", "ln1_b", "ln2_g", "ln2_b"):
        w[n_] = p[n_][None, :]
    return w


def _mm_sided(a, b, side, **kw):
    if side is None:
        return _mm(a, b, **kw), []
    res = _mm(a, b, side=side, **kw)
    return res[0], res[1:]


_SMALL = ("rg_conv_w", "dn_conv_w", "ffn_conv_w", "rg_conv_b", "rg_w_a", "rg_b_a", "rg_w_x", "rg_b_x",
          "rg_lambda", "dn_a_log", "dn_dt_bias", "dn_norm_w", "ln1_g", "ln1_b", "ffn_conv_b", "ln2_g", "ln2_b")
_REST_A = ("w_proj_a", "w_proj_b", "w_out")
_REST_B = ("ffn_w_gate", "ffn_w_up", "ffn_w_down")


def _local_step(x, tgt, adam, p, shards=None):
    w = _prepare_first(p)
    side_a = side_b = None
    if shards is not None:
        side_a = _gather_exchange([shards[n] for n in _REST_A])
        side_b = _gather_exchange([shards[n] for n in _REST_B])
    h1, h1t = _modulate(x, adam)
    cat = w["in_cat"]
    pqkv, got_b = _mm_sided(h1, cat[:, 0:4096], side_b, name="proj_qkv")
    px, got_a = _mm_sided(h1, cat[:, 8192:10752], side_a, name="proj_x")
    if shards is not None:
        p = dict(p, **{n: _unstack_shards(n, a_) for n, a_ in zip(_REST_A + _REST_B, got_a + got_b)})
    w.update(_prepare_rest(p))
    pz = _mm(h1, cat[:, 4096:6144], name="proj_z")
    pg = _mm(h1, cat[:, 6144:8192], name="proj_g")
    pab = _mm(h1, cat[:, 10752:10880], name="proj_ab")
    hrec, xcs, rec, rec_t = _rg_fwd(px, w["rg_cw"], w["rg_vec"], w["wa"], w["wx"])
    q, k, v, rt, cpre = _dn_prep(pqkv, pab, w["dn_cw"], w["arow"], w["drow"])
    u, ww, qd, kd, pm, tinv, gam = _dn_intra(q, k, v, rt)
    o, vn, ssave, dn, dn_t = _dn_seq(u, ww, qd, kd, pm, gam, pz, w["nw"])
    ya = _mm(rec, w["pa"], name="proj_a")
    yb = _mm(dn, w["pb"], name="proj_b")
    mix, merged_t = _merge_proj_out(pg, ya, yb, w["out"])
    r1, x2, h2, h2_t = _ln1(x, mix, adam, w["ln1_g"], w["ln1_b"])
    gu = _mm(h2, w["gu"], name="ffn_gu")
    act, act_t, gcf = _ffn_act(gu, w["ffn_cw"], w["ffn_cb"])
    dff, dx2a, red2 = _ffn_down_ln2_loss(act, w["down"], x2, tgt, adam, w["ln2_g"], w["ln2_b"])
    g = {}
    dact = _mm(dff, w["down_t"], name="d_act")
    g["ffn_w_down"] = _mm(act_t, dff, name="g_down")
    dgu, gcw_f, gcb_f = _ffn_bwd(dact, gu, gcf, w["ffn_cw"])
    ggu = _mm(h2_t, dgu, name="g_gu")
    g["ffn_w_gate"], g["ffn_w_up"] = ggu[:, :D_FF], ggu[:, D_FF:]
    g["ffn_conv_w"], g["ffn_conv_b"] = gcw_f[0:3], gcb_f[0]
    dmix, dxa, red1 = _d_h2_ln1_bwd(dgu, w["gu_t"], dx2a, x2, r1, mix, adam, w["ln1_g"])
    g["w_out"] = _mm(merged_t, dmix, name="g_out")
    dproj = lax.empty((x.shape[0], N_CAT), MXU_DT)
    dya, dyb, dproj = _d_merged_bwd(dmix, w["out_t"], pg, ya, yb, dproj)
    drec = _mm(dya, w["pa_t"], name="d_rec")
    g["w_proj_a"] = _mm(rec_t, dya, name="g_pa")
    g["w_proj_b"] = _mm(dn_t, dyb, name="g_pb")
    do, dproj, gnw = _d_dn_post_bwd(dyb, w["pb_t"], o, pz, w["nw"], dproj)
    dvn, dkd, dgam = _dn_seq_bwd(do, qd, kd, pm, ww, vn, ssave, gam)
    dq, dk, dv, drt = _dn_intra_bwd(q, k, v, rt, do, dvn, dkd, gam, dgam, ssave, tinv, u, ww, vn)
    dproj, dpab, gcw_d, gsc = _dn_prep_bwd(dq, dk, dv, drt, pqkv, cpre, pab, w["dn_cw"], w["acol"], w["dcol"],
                                           dproj)
    side_r = None
    if shards is not None:
        side_r = _scatter_exchange([_stack_shards(n, g[n]).astype(MXU_DT) for n in _REST_A + _REST_B])
    dpx, gwa, gwx, gcw_r, gvec, *got_r = _rg_bwd(drec, px, xcs, hrec, w["rg_cw"], w["rg_vec"], w["wa"], w["wx"],
                                                 w["wa_t"], w["wx_t"], side_r)
    dproj = lax.dynamic_update_slice(dproj, jnp.concatenate([dpx, dpab], axis=1), (0, 8192))
    reorder = lambda gc: jnp.concatenate([gc[:, 8192:10752], gc[:, 0:4096], gc[:, 4096:6144], gc[:, 10752:10784],
                                          gc[:, 6144:8192]], axis=1)
    wire = lambda gh: _scatter_exchange([_stack_shards("w_in", gh).astype(MXU_DT)]) if shards is not None else None
    g["rg_conv_w"], g["rg_conv_b"] = gcw_r[0:4], gvec[0]
    g["rg_w_a"], g["rg_w_x"] = _diag_blocks(gwa), _diag_blocks(gwx)
    g["rg_b_a"], g["rg_b_x"], g["rg_lambda"] = gvec[1], gvec[2], gvec[3]
    g["dn_conv_w"] = gcw_d[0:4]
    g["dn_a_log"], g["dn_dt_bias"], g["dn_norm_w"] = gsc[:, 0], gsc[:, 1], gnw[0]
    g["ln1_g"], g["ln1_b"] = red1[2], red1[3]
    g["ln2_g"], g["ln2_b"] = red2[0], red2[1]
    side_s = None
    if shards is not None:
        side_s = _allgather_exchange(_pack([jnp.full((1,), red2[3, 0], F32)] + [g[n] for n in _SMALL], 128, 64))
    g_top, got_s = _mm_sided(h1t[:D // 2], dproj, side_s, name="g_in_top")
    g_top = reorder(g_top)
    g_bot, got_top = _mm_sided(h1t[D // 2:], dproj, wire(g_top), name="g_in_bot")
    g_bot = reorder(g_bot)
    gx, red0, *got_bot = _d_h1_modulate_bwd(dproj, w["in_cat_t"], dxa, x, adam, wire(g_bot))
    g["w_in"] = jnp.concatenate([g_top, g_bot], axis=0)
    got = dict(zip(_REST_A + _REST_B, got_r))
    if shards is not None:
        got["w_in"] = jnp.concatenate([got_top[0], got_bot[0]], axis=1)
        got["small"] = got_s[0]
    d_ada = jnp.concatenate([red0[1], red0[0], red1[4], red1[1], red1[0], red2[2]])
    return red2[3, 0], gx, g, d_ada, got


_BIG = ("w_in", "w_proj_a", "w_proj_b", "w_out", "ffn_w_gate", "ffn_w_up", "ffn_w_down")
_COL_SHARDED = ("w_in", "ffn_w_gate", "ffn_w_up")
_CONV = ("rg_conv_w", "dn_conv_w", "ffn_conv_w")
_REPL = ("b_ada", "rg_conv_b", "rg_w_a", "rg_b_a", "rg_w_x", "rg_b_x", "rg_lambda", "dn_a_log",
         "dn_dt_bias", "dn_norm_w", "ln1_g", "ln1_b", "ffn_conv_b", "ln2_g", "ln2_b")
_NAMES = ("w_ada", "b_ada", "w_in", "rg_conv_w", "rg_conv_b", "rg_w_a", "rg_b_a", "rg_w_x", "rg_b_x",
          "rg_lambda", "dn_conv_w", "dn_a_log", "dn_dt_bias", "dn_norm_w", "w_proj_a", "w_proj_b", "w_out",
          "ln1_g", "ln1_b", "ffn_w_gate", "ffn_w_up", "ffn_conv_w", "ffn_conv_b", "ffn_w_down", "ln2_g", "ln2_b")


def _pack(arrs, width, row_mult):
    pieces = []
    for a in arrs:
        f = a.reshape(-1)
        pieces.append(jnp.pad(f, (0, (-f.shape[0]) % (8 * width))).reshape(-1, width))
    rows = sum(p_.shape[0] for p_ in pieces)
    if rows % row_mult:
        pieces.append(jnp.zeros((row_mult - rows % row_mult, width), pieces[0].dtype))
    return jnp.concatenate(pieces, axis=0)


def _unpack(flat, shapes, width):
    out, row = [], 0
    for shp in shapes:
        n = 1
        for d_ in shp:
            n *= d_
        rows = -(-n // (8 * width)) * 8
        out.append(flat[row:row + rows].reshape(-1)[:n].reshape(shp))
        row += rows
    return out


def _stack_shards(name, full):
    if name in _COL_SHARDED or name in _CONV:
        r, ncol = full.shape
        return full.reshape(r, 4, ncol // 4).transpose(1, 0, 2)
    return full.reshape((4, full.shape[0] // 4) + full.shape[1:])


def _unstack_shards(name, st):
    if name in _COL_SHARDED or name in _CONV:
        return st.transpose(1, 0, 2).reshape(st.shape[1], 4 * st.shape[2])
    return st.reshape((4 * st.shape[1],) + st.shape[2:])


def kernel(x, c, w_ada, b_ada, w_in, rg_conv_w, rg_conv_b, rg_w_a, rg_b_a, rg_w_x, rg_b_x, rg_lambda, dn_conv_w, dn_a_log, dn_dt_bias, dn_norm_w, w_proj_a, w_proj_b, w_out, ln1_g, ln1_b, ffn_w_gate, ffn_w_up, ffn_conv_w, ffn_conv_b, ffn_w_down, ln2_g, ln2_b, loss_target, m_w_ada, m_b_ada, m_w_in, m_rg_conv_w, m_rg_conv_b, m_rg_w_a, m_rg_b_a, m_rg_w_x, m_rg_b_x, m_rg_lambda, m_dn_conv_w, m_dn_a_log, m_dn_dt_bias, m_dn_norm_w, m_w_proj_a, m_w_proj_b, m_w_out, m_ln1_g, m_ln1_b, m_ffn_w_gate, m_ffn_w_up, m_ffn_conv_w, m_ffn_conv_b, m_ffn_w_down, m_ln2_g, m_ln2_b, v_w_ada, v_b_ada, v_w_in, v_rg_conv_w, v_rg_conv_b, v_rg_w_a, v_rg_b_a, v_rg_w_x, v_rg_b_x, v_rg_lambda, v_dn_conv_w, v_dn_a_log, v_dn_dt_bias, v_dn_norm_w, v_w_proj_a, v_w_proj_b, v_w_out, v_ln1_g, v_ln1_b, v_ffn_w_gate, v_ffn_w_up, v_ffn_conv_w, v_ffn_conv_b, v_ffn_w_down, v_ln2_g, v_ln2_b):
    args = locals()
    wts = {n: args[n][0] for n in _NAMES}
    mom = {n: args["m_" + n][0] for n in _NAMES}
    var = {n: args["v_" + n][0] for n in _NAMES}
    xs, tgt = x[0], loss_target[0]
    ix, iy, ic = _coords()
    shard = 2 * ix + iy
    batch = 4 * ix + 2 * iy + ic

    c_all = _allgather8(_pad_rows(c, 8), "gather_c")[:, 0, :]
    sc16 = _pad_rows(_silu_rows(c_all), 16)
    ada_cols = _mm(sc16, w_ada[0], name="ada")[:8]
    ada_g = _allgather8(ada_cols, "gather_ada")
    ada_all = jnp.concatenate([ada_g[0], ada_g[2], ada_g[4], ada_g[6]], axis=1) + b_ada
    adam = _pad_rows(lax.dynamic_index_in_dim(ada_all, batch, 0, keepdims=False).reshape(6, D), 8)

    (w_in_all,) = _run_exchange(_gather_exchange([wts["w_in"].astype(MXU_DT)]), "gather_w_in")
    full = {"w_in": _unstack_shards("w_in", w_in_all)}
    conv_shard = _pack([wts[n] for n in _CONV], 128, 8)
    conv_all = _allgather8(conv_shard, "gather_conv")
    shapes_conv = [wts[n].shape for n in _CONV]
    per_shard = [_unpack(conv_all[2 * s], shapes_conv, 128) for s in range(4)]
    for i, n in enumerate(_CONV):
        full[n] = _unstack_shards(n, jnp.stack([per_shard[s][i] for s in range(4)]))
    for n in _REPL:
        full[n] = wts[n]

    shards = {n: wts[n].astype(MXU_DT) for n in _REST_A + _REST_B}
    loss_b, gx, g, d_ada, recv = _local_step(xs, tgt, adam, full, shards)

    parts = []
    for n in _BIG:
        r_ = recv[n]
        axis = 1 if n in _COL_SHARDED else 0
        width = wts[n].shape[axis]
        own = lax.dynamic_slice_in_dim(g[n], shard * width, width, axis=axis)
        parts.append(_sum_partials(own, r_, "sum_" + n))
    parts_sib = _swap_sibling(parts)
    out = {n: [] for n in _NAMES}
    for n, p_, q_ in zip(_BIG, parts, parts_sib):
        out[n] = list(_adamw([p_, q_], wts[n], mom[n], var[n], "adamw_" + n))

    tot = _unpack(_sum8(recv["small"], "sum_small"), [(1,)] + [full[n].shape for n in _SMALL], 128)
    gsum = dict(zip(_SMALL, tot[1:]))
    loss = tot[0][0]
    for n in _CONV:
        gsum[n] = lax.dynamic_index_in_dim(_stack_shards(n, gsum[n]), shard, 0, keepdims=False)
    d_ada_g = _allgather8(d_ada.reshape(6 * D // 128, 128), "gather_d_ada")
    gsum["b_ada"] = _sum8(d_ada_g, "sum_d_ada").reshape(6 * D)
    d_ada_all = d_ada_g.reshape(8, 6 * D)
    cols = lax.dynamic_slice_in_dim(d_ada_all, shard * (6 * D // 4), 6 * D // 4, axis=1)
    g_wada = _mm(sc16, _pad_rows(cols, 16), name="g_ada", trans_a=True)
    res = _adamw([g_wada], wts["w_ada"], mom["w_ada"], var["w_ada"], "adamw_ada")
    out["w_ada"] = list(res)
    names_s = _CONV + _REPL
    shapes_s = [wts[n].shape for n in names_s]
    pk = lambda d_: _pack([d_[n] for n in names_s], 128, 64)
    res_s = _adamw([pk(gsum)], pk(wts), pk(mom), pk(var), "adamw_small")
    for r_ in res_s:
        for n, a in zip(names_s, _unpack(r_, shapes_s, 128)):
            out[n].append(a)

    outs = [loss, gx[None]]
    for i in range(4):
        outs += [out[n][i][None] for n in _NAMES]
    return tuple(outs)
```

```python
import functools

import jax
import jax.numpy as jnp
from jax import lax
from jax.experimental import pallas as pl
from jax.experimental.pallas import tpu as pltpu

F32 = jnp.float32
BF16 = jnp.bfloat16
MXU_DT = BF16

D = 1024
D_RNN = 1280
RG_BLOCKS = 16
RG_BW = 80
RG_C = 8.0
NQ = 8
NV = 16
HD = 128
CH = 64
D_FF = 2816
LN_EPS = 1e-5
RMS_EPS = 1e-6
L2_EPS = 1e-6
ALPHA = 2.0 ** 0.25
Q_SCALE = HD ** -0.5
N_CAT = 10880
VMEM_LIMIT = 56 * 1024 * 1024
MM_VMEM_BUDGET = 42 * 1024 * 1024
SEQ_GROUP = 16
MESH = pl.DeviceIdType.MESH

ADAM_LR, ADAM_B1, ADAM_B2, ADAM_EPS, ADAM_WD, ADAM_STEP = 1e-3, 0.9, 0.999, 1e-8, 0.01, 10


def _sigmoid(x):
    return 0.5 * jnp.tanh(0.5 * x) + 0.5


def _softplus(x):
    return jnp.maximum(x, 0.0) + jnp.log1p(jnp.exp(-jnp.abs(x)))


_GC = 0.7978845608028654


def _gelu(x):
    return 0.5 * x * (1.0 + jnp.tanh(_GC * (x + 0.044715 * x * x * x)))


def _gelu_and_grad(x):
    t = jnp.tanh(_GC * (x + 0.044715 * x * x * x))
    g = 0.5 * x * (1.0 + t)
    dg = 0.5 * (1.0 + t) + 0.5 * x * (1.0 - t * t) * _GC * (1.0 + 3 * 0.044715 * x * x)
    return g, dg


def _neg_expm1(y):
    series = -y * (1.0 + 0.5 * y * (1.0 + y * (1.0 / 3.0)))
    return jnp.where(y > -0.01, series, 1.0 - jnp.exp(y))


def _dot(a, b):
    return jnp.dot(a.astype(MXU_DT), b.astype(MXU_DT), preferred_element_type=F32)


def _dot_nt(a, b):
    return lax.dot_general(a.astype(MXU_DT), b.astype(MXU_DT), (((1,), (1,)), ((), ())),
                           preferred_element_type=F32)


def _dot_tn(a, b):
    return lax.dot_general(a.astype(MXU_DT), b.astype(MXU_DT), (((0,), (0,)), ((), ())),
                           preferred_element_type=F32)


def _split(a):
    hi = a.astype(BF16)
    return hi, (a - hi.astype(F32)).astype(BF16)


def _dot3(a, b, dims=(((1,), (0,)), ((), ()))):
    ah, al = _split(a)
    bh, bl = _split(b)
    d = lambda p, q: lax.dot_general(p, q, dims, preferred_element_type=F32)
    return d(ah, bh) + (d(al, bh) + d(ah, bl))


def _dot3_tn(a, b):
    return _dot3(a, b, (((0,), (0,)), ((), ())))


def _iota(shape, dim):
    return lax.broadcasted_iota(jnp.int32, shape, dim)


def _shift_down(x, before, j):
    if j == 0:
        return x
    xr = pltpu.roll(x, j, 0)
    br = pltpu.roll(before, j, 0)
    top = jnp.where(_iota(br.shape, 0) < j, br, xr[:8])
    return jnp.concatenate([top, xr[8:]], axis=0)


def _shift_up(x, after, j):
    if j == 0:
        return x
    t = x.shape[0]
    xr = pltpu.roll(x, t - j, 0)
    ar = pltpu.roll(after, 8 - j, 0)
    bot = jnp.where(_iota(ar.shape, 0) >= 8 - j, ar, xr[t - 8:])
    return jnp.concatenate([xr[:t - 8], bot], axis=0)


def _taps(x, before, k):
    return [_shift_down(x, before, k - 1 - i) for i in range(k)]


def _conv_taps(taps, w_ref):
    y = w_ref[0:1, :] * taps[0]
    for i in range(1, len(taps)):
        y = y + w_ref[i:i + 1, :] * taps[i]
    return y


def _conv_causal(x, before, w_ref, k):
    return _conv_taps(_taps(x, before, k), w_ref)


def _conv_causal_bwd(dy, after, w_ref, k, x=None, gw_ref=None):
    dx = None
    for i in range(k):
        sh = _shift_up(dy, after, k - 1 - i)
        term = w_ref[i:i + 1, :] * sh
        dx = term if dx is None else dx + term
        if x is not None:
            gw_ref[i:i + 1, :] += _rsum(x * sh)
    return dx


def _scan_fwd(a, u, carry):
    t = a.shape[0]
    pos = _iota(a.shape, 0) & 7
    for d in (1, 2, 4):
        m = pos >= d
        u = u + jnp.where(m, a * pltpu.roll(u, d, 0), 0.0)
        a = jnp.where(m, a * pltpu.roll(a, d, 0), a)
    out = []
    for g in range(t // 8):
        hg = u[8 * g:8 * g + 8] + a[8 * g:8 * g + 8] * carry
        out.append(hg)
        carry = hg[7:8]
    return jnp.concatenate(out, axis=0)


def _scan_rev(a, u, carry):
    t = a.shape[0]
    pos = _iota(a.shape, 0) & 7
    for d in (1, 2, 4):
        m = pos < 8 - d
        u = u + jnp.where(m, a * pltpu.roll(u, t - d, 0), 0.0)
        a = jnp.where(m, a * pltpu.roll(a, t - d, 0), a)
    out = [None] * (t // 8)
    for g in reversed(range(t // 8)):
        lg = u[8 * g:8 * g + 8] + a[8 * g:8 * g + 8] * carry
        out[g] = lg
        carry = lg[0:1]
    return jnp.concatenate(out, axis=0)


def _chunk_cumsum(g, axis, rev=False):
    n = g.shape[axis]
    pos = _iota(g.shape, axis) & (CH - 1)
    d = 1
    while d < CH:
        if rev:
            g = g + jnp.where(pos < CH - d, pltpu.roll(g, n - d, axis), 0.0)
        else:
            g = g + jnp.where(pos >= d, pltpu.roll(g, d, axis), 0.0)
        d *= 2
    return g


def _ln_stats(r):
    mu = jnp.mean(r, axis=-1, keepdims=True)
    xc = r - mu
    var = jnp.mean(xc * xc, axis=-1, keepdims=True)
    rstd = lax.rsqrt(var + LN_EPS)
    return xc * rstd, rstd


def _ln_bwd(dy, xhat, rstd, g):
    dxh = dy * g
    return rstd * (dxh - jnp.mean(dxh, axis=-1, keepdims=True)
                   - xhat * jnp.mean(dxh * xhat, axis=-1, keepdims=True))


def _rsum(x):
    return jnp.sum(x, axis=0, keepdims=True)


def _params(sem):
    return pltpu.CompilerParams(dimension_semantics=sem, vmem_limit_bytes=VMEM_LIMIT)


def _pick(n, cands):
    for c in cands:
        if n % c == 0:
            return c
    return n


def _rows(tm, w, col=0, nt=None):
    if nt is None:
        return pl.BlockSpec((tm, w), lambda i: (i, col))
    return pl.BlockSpec((tm, w), lambda i: (nt - 1 - i, col))


def _before(tm, w, col=0, nt=None):
    r = tm // 8
    if nt is None:
        return pl.BlockSpec((8, w), lambda i: (jnp.maximum(i * r - 1, 0), col))
    return pl.BlockSpec((8, w), lambda i: (jnp.maximum((nt - 1 - i) * r - 1, 0), col))


def _cols(tm, w):
    return pl.BlockSpec((w, tm), lambda i: (0, i))


def _whole(shape):
    return pl.BlockSpec(shape, lambda *_: (0,) * len(shape))


def _mm_plan(a, b, out_dtype):
    m, kk = a.shape
    _, n = b.shape
    tm = _pick(m, (512, 256, 128))

    def vmem_bytes(tn, tk):
        blocks = tm * tk * a.dtype.itemsize + tk * tn * b.dtype.itemsize + tm * tn * jnp.dtype(out_dtype).itemsize
        return 2 * blocks + (tm * tn * 4 if tk < kk else 0)

    cands = [t for t in (1408, 1280, 1024, 640, 512, 256, 128) if n % t == 0] or [n]
    tks = [kk] if kk <= 5632 else [t for t in (4096, 2176, 2048, 1024) if kk % t == 0]
    tn, tk = max(((next((t for t in cands if vmem_bytes(t, k_) <= MM_VMEM_BUDGET), cands[-1]), k_) for k_ in tks))
    return tm, tn, tk, kk // tk


def _mm_side(a, b, name, out_dtype, side):
    m, _ = a.shape
    _, n = b.shape
    tm, tn, tk, nk = _mm_plan(a, b, out_dtype)
    ni, nj = m // tm, n // tn
    n_in, n_out = len(side.ins), len(side.out_shapes)

    def body(*refs):
        a_ref, b_ref = refs[0], refs[1]
        s_in = refs[2:2 + n_in]
        o_ref = refs[2 + n_in]
        s_out = refs[3 + n_in:3 + n_in + n_out]
        acc = refs[3 + n_in + n_out]
        sems = refs[4 + n_in + n_out:]
        i, j, k = pl.program_id(0), pl.program_id(1), pl.program_id(2)

        @pl.when((i == 0) & (j == 0) & (k == 0))
        def _():
            side.start(s_in, s_out, sems)

        @pl.when(k == 0)
        def _():
            acc[...] = jnp.zeros_like(acc)
        acc[...] += _dot(a_ref[...], b_ref[...])

        @pl.when(k == nk - 1)
        def _():
            o_ref[...] = acc[...].astype(out_dtype)

        @pl.when((i == ni - 1) & (j == nj - 1) & (k == nk - 1))
        def _():
            side.finish(s_in, s_out, sems)

    return pl.pallas_call(
        body, name=name, grid=(ni, nj, nk),
        in_specs=[pl.BlockSpec((tm, tk), lambda i, j, k: (i, k)),
                  pl.BlockSpec((tk, tn), lambda i, j, k: (k, j))] + [_HBM] * n_in,
        out_specs=[pl.BlockSpec((tm, tn), lambda i, j, k: (i, j))] + [_HBM] * n_out,
        out_shape=[jax.ShapeDtypeStruct((m, n), out_dtype)] + list(side.out_shapes),
        scratch_shapes=[pltpu.VMEM((tm, tn), F32)] + list(side.sems),
        compiler_params=_params(("arbitrary", "arbitrary", "arbitrary")),
    )(a, b, *side.ins)


def _mm(a, b, *, name, trans_a=False, out_dtype=F32, side=None):
    if trans_a:
        return _mm(a.T, b, name=name, out_dtype=out_dtype, side=side)
    if side is not None:
        return _mm_side(a, b, name, out_dtype, side)
    m, kk = a.shape
    _, n = b.shape
    tm, tn, tk, nk = _mm_plan(a, b, out_dtype)

    if nk == 1:
        def body(a_ref, b_ref, o_ref):
            o_ref[...] = _dot(a_ref[...], b_ref[...]).astype(out_dtype)
        scratch = []
    else:
        def body(a_ref, b_ref, o_ref, acc):
            k = pl.program_id(2)

            @pl.when(k == 0)
            def _():
                acc[...] = jnp.zeros_like(acc)
            acc[...] += _dot(a_ref[...], b_ref[...])

            @pl.when(k == nk - 1)
            def _():
                o_ref[...] = acc[...].astype(out_dtype)
        scratch = [pltpu.VMEM((tm, tn), F32)]

    return pl.pallas_call(
        body, name=name, grid=(m // tm, n // tn, nk),
        in_specs=[pl.BlockSpec((tm, tk), lambda i, j, k: (i, k)),
                  pl.BlockSpec((tk, tn), lambda i, j, k: (k, j))],
        out_specs=pl.BlockSpec((tm, tn), lambda i, j, k: (i, j)),
        out_shape=jax.ShapeDtypeStruct((m, n), out_dtype),
        scratch_shapes=scratch,
        compiler_params=_params(("parallel", "parallel", "arbitrary")),
    )(a, b)


def _modulate(x, adam):
    s = x.shape[0]
    tm = _pick(s, (512, 256, 128))

    def body(x_ref, ada_ref, o_ref, ot_ref):
        h = x_ref[...] * (1.0 + ada_ref[1:2, :]) + ada_ref[0:1, :]
        o_ref[...] = h.astype(MXU_DT)
        ot_ref[...] = h.T.astype(MXU_DT)

    return pl.pallas_call(
        body, name="modulate1", grid=(s // tm,),
        in_specs=[_rows(tm, D), _whole((8, D))], out_specs=[_rows(tm, D), _cols(tm, D)],
        out_shape=[jax.ShapeDtypeStruct((s, D), MXU_DT), jax.ShapeDtypeStruct((D, s), MXU_DT)],
        compiler_params=_params(("parallel",)),
    )(x, adam)


def _rg_gates(xc, wa_ref, wx_ref, vec_ref):
    xb = xc.astype(MXU_DT)
    r = _sigmoid(jnp.dot(xb, wa_ref[...], preferred_element_type=F32) + vec_ref[1:2, :])
    ig = _sigmoid(jnp.dot(xb, wx_ref[...], preferred_element_type=F32) + vec_ref[2:3, :])
    sp = _softplus(-vec_ref[3:4, :])
    la = -RG_C * r * sp
    a = jnp.exp(la)
    n1 = _neg_expm1(2.0 * la)
    rmult = lax.rsqrt(jnp.maximum(n1, 1e-20))
    return r, ig, a, n1 * rmult, sp, rmult


def _rg_fwd(px, cw, vec, wa, wx):
    s = px.shape[0]
    tm = _pick(s, (256, 128))
    w = D_RNN

    def body(xr_ref, gr_ref, cw_ref, vec_ref, wa_ref, wx_ref, h_ref, xc_ref, rec_ref, rect_ref, prev_x, prev_h):
        @pl.when(pl.program_id(0) == 0)
        def _():
            prev_x[...] = jnp.zeros_like(prev_x)
            prev_h[...] = jnp.zeros_like(prev_h)
        x = xr_ref[...]
        xc = _conv_causal(x, prev_x[...], cw_ref, 4) + vec_ref[0:1, :]
        prev_x[...] = x[tm - 8:, :]
        xc_ref[...] = xc
        _, ig, a, mult, _, _ = _rg_gates(xc, wa_ref, wx_ref, vec_ref)
        h = _scan_fwd(a, mult * ig * xc, prev_h[7:8, :])
        prev_h[...] = h[tm - 8:, :]
        h_ref[...] = h
        rec = h * _gelu(gr_ref[...])
        rec_ref[...] = rec.astype(MXU_DT)
        rect_ref[...] = rec.T.astype(MXU_DT)

    return pl.pallas_call(
        body, name="rg_fwd", grid=(s // tm,),
        in_specs=[_rows(tm, w, 0), _rows(tm, w, 1), _whole((8, w)), _whole((8, w)),
                  _whole((w, w)), _whole((w, w))],
        out_specs=[_rows(tm, w), _rows(tm, w), _rows(tm, w), _cols(tm, w)],
        out_shape=[jax.ShapeDtypeStruct((s, w), F32), jax.ShapeDtypeStruct((s, w), F32),
                   jax.ShapeDtypeStruct((s, w), MXU_DT), jax.ShapeDtypeStruct((w, s), MXU_DT)],
        scratch_shapes=[pltpu.VMEM((8, w), F32), pltpu.VMEM((8, w), F32)],
        compiler_params=_params(("arbitrary",)),
    )(px, px, cw, vec, wa, wx)


def _dn_scalars(ab, arow, drow):
    lane = _iota(ab.shape, 1)
    g = jnp.where(lane < NV, -jnp.exp(arow) * _softplus(ab + drow), 0.0)
    beta = _sigmoid(ab)
    return lane, g, beta


def _l2n_heads(c, out_ref, off, scale):
    for hh in range(NQ):
        x = c[:, off + hh * HD: off + (hh + 1) * HD]
        r = lax.rsqrt(jnp.sum(x * x, axis=-1, keepdims=True) + L2_EPS)
        out_ref[:, hh * HD:(hh + 1) * HD] = x * (r * scale)


def _dn_prep(pqkv, pab, cw, arow, drow):
    s = pqkv.shape[0]
    tm = _pick(s, (256, 128))
    wq = NQ * HD

    def body(x_ref, ab_ref, cw_ref, a_ref, d_ref, q_ref, k_ref, v_ref, rt_ref, cp_ref, prev_x):
        @pl.when(pl.program_id(0) == 0)
        def _():
            prev_x[...] = jnp.zeros_like(prev_x)
        x = x_ref[...]
        cp = _conv_causal(x, prev_x[...], cw_ref, 4)
        prev_x[...] = x[tm - 8:, :]
        cp_ref[...] = cp
        c = cp * _sigmoid(cp)
        _l2n_heads(c, q_ref, 0, Q_SCALE)
        _l2n_heads(c, k_ref, wq, 1.0)
        v_ref[...] = c[:, 2 * wq:]
        lane, g, beta = _dn_scalars(ab_ref[...], a_ref[...], d_ref[...])
        gc = _chunk_cumsum(g, 0)
        gl = gc + _chunk_cumsum(g, 0, rev=True) - g
        pack = jnp.where(lane < NV, gc, jnp.where(lane < 2 * NV, beta,
                         jnp.where(lane < 3 * NV, pltpu.roll(gl, 2 * NV, 1), 0.0)))
        rt_ref[...] = pack.T[0:3 * NV, :]

    return pl.pallas_call(
        body, name="dn_prep", grid=(s // tm,),
        in_specs=[_rows(tm, 4 * wq), _rows(tm, 128), _whole((8, 4 * wq)), _whole((1, 128)), _whole((1, 128))],
        out_specs=[_rows(tm, wq), _rows(tm, wq), _rows(tm, 2 * wq),
                   pl.BlockSpec((3 * NV, tm), lambda i: (0, i)), _rows(tm, 4 * wq)],
        out_shape=[jax.ShapeDtypeStruct((s, wq), F32), jax.ShapeDtypeStruct((s, wq), F32),
                   jax.ShapeDtypeStruct((s, 2 * wq), F32), jax.ShapeDtypeStruct((3 * NV, s), F32),
                   jax.ShapeDtypeStruct((s, 4 * wq), F32)],
        scratch_shapes=[pltpu.VMEM((8, 4 * wq), F32)],
        compiler_params=_params(("arbitrary",)),
    )(pqkv, pab, cw, arow, drow)


def _pair_masks():
    i = _iota((2 * CH, 2 * CH), 0)
    j = _iota((2 * CH, 2 * CH), 1)
    same = (i >> 6) == (j >> 6)
    return same & (i >= j), same & (i > j)


def _head_cols(rt_ref, h):
    shp = (2 * CH, 2 * CH)
    g_r = jnp.broadcast_to(rt_ref[pl.ds(h, 1), :], shp)
    b_r = jnp.broadcast_to(rt_ref[pl.ds(NV + h, 1), :], shp)
    l_r = jnp.broadcast_to(rt_ref[pl.ds(2 * NV + h, 1), :], shp)
    return g_r, g_r.T, b_r.T, l_r, l_r.T


def _inv_unit_lower_many(a_list):
    n = a_list[0].shape[0]
    eye = (_iota((n, n), 0) == _iota((n, n), 1)).astype(F32)
    bs = [-a for a in a_list]
    xs = [eye + b for b in bs]
    for _ in range(5):
        bs = [_dot(b, b) for b in bs]
        xs = [x + _dot(x, b) for x, b in zip(xs, bs)]
    rs = [(eye - x) - _dot3(a, x) for a, x in zip(a_list, xs)]
    return [x + _dot(x, r) for x, r in zip(xs, rs)]


def _gam_rows(l_r):
    lrow = l_r[0:1, :]
    lane = _iota(lrow.shape, 1)
    other = pltpu.roll(lrow, CH, 1)
    return jnp.exp(jnp.where(lane < CH, lrow, other)), jnp.exp(jnp.where(lane >= CH, lrow, other))


def _dn_intra(q, k, v, rt):
    s = q.shape[0]
    nb = s // (2 * CH)
    qps = 8
    blk = pl.BlockSpec((2 * CH, qps * HD), lambda i, h: (i, h))
    blk2 = pl.BlockSpec((2 * CH, 2 * qps * HD), lambda i, h: (i, h))

    def body(q_ref, k_ref, v_ref, rt_ref, u_ref, w_ref, qd_ref, kd_ref, p_ref, ti_ref, gam_ref):
        hstep = pl.program_id(1)
        mc, ms = _pair_masks()
        gam_ref[...] = jnp.zeros_like(gam_ref)
        heads = []
        for qh in range(qps):
            qq = q_ref[:, qh * HD:(qh + 1) * HD]
            kk_ = k_ref[:, qh * HD:(qh + 1) * HD]
            kk = _dot_nt(kk_, kk_)
            qk = _dot_nt(qq, kk_)
            for j in range(2):
                idx = 2 * qh + j
                cs = slice(idx * HD, (idx + 1) * HD)
                g_r, g_c, b_c, l_r, l_c = _head_cols(rt_ref, 2 * qps * hstep + idx)
                dec = jnp.where(mc, jnp.exp(jnp.where(mc, g_c - g_r, 0.0)), 0.0)
                eg = jnp.exp(g_c)
                p_ref[:, cs] = jnp.where(mc, qk * dec, 0.0)
                qd_ref[:, cs] = eg * qq
                kd_ref[:, cs] = jnp.exp(l_c - g_c) * kk_
                ga, gb = _gam_rows(l_r)
                gam_ref[0, qh, 2 * j:2 * j + 1, :] = ga
                gam_ref[0, qh, 2 * j + 1:2 * j + 2, :] = gb
                rhs = jnp.concatenate([b_c * v_ref[:, cs], b_c * eg * kk_], axis=1)
                heads.append((cs, jnp.where(ms, b_c * kk * dec, 0.0), rhs))
        tinvs = _inv_unit_lower_many([a for _, a, _ in heads])
        uws = [_dot3(t, rhs) for t, (_, _, rhs) in zip(tinvs, heads)]
        for t, uw, (cs, _, _) in zip(tinvs, uws, heads):
            ti_ref[:, cs] = t
            u_ref[:, cs] = uw[:, :HD]
            w_ref[:, cs] = uw[:, HD:]

    big = jax.ShapeDtypeStruct((s, NV * HD), F32)
    return pl.pallas_call(
        body, name="dn_intra", grid=(nb, NQ // qps),
        in_specs=[blk, blk, blk2, pl.BlockSpec((3 * NV, 2 * CH), lambda i, h: (0, i))],
        out_specs=[blk2] * 6 + [pl.BlockSpec((1, qps, 8, 128), lambda i, h: (i, h, 0, 0))],
        out_shape=[big] * 6 + [jax.ShapeDtypeStruct((nb, NQ, 8, 128), F32)],
        compiler_params=_params(("parallel", "parallel")),
    )(q, k, v, rt)


def _dn_seq(u, w, qd, kd, p, gam, pz, nw):
    s = u.shape[0]
    nb = s // (2 * CH)
    wide = pl.BlockSpec((2 * CH, NV * HD), lambda i: (i, 0))

    def body(u_ref, w_ref, qd_ref, kd_ref, p_ref, gam_ref, z_ref, nw_ref, o_ref, vn_ref, ss_ref, y_ref, yt_ref, st):
        @pl.when(pl.program_id(0) == 0)
        def _():
            st[...] = jnp.zeros_like(st)
        ra, rb = slice(0, CH), slice(CH, 2 * CH)
        for g0 in range(0, NV, SEQ_GROUP):
            hs = list(range(g0, g0 + SEQ_GROUP))
            cs = [slice(h * HD, (h + 1) * HD) for h in hs]
            ga = [gam_ref[0, h // 2, 2 * (h % 2):2 * (h % 2) + 1, :] for h in hs]
            gb = [gam_ref[0, h // 2, 2 * (h % 2) + 1:2 * (h % 2) + 2, :] for h in hs]
            s0 = [st[h] for h in hs]
            vna = [u_ref[ra, c] - _dot(w_ref[ra, c], s) for c, s in zip(cs, s0)]
            s1 = [g * s + _dot_tn(kd_ref[ra, c], v) for g, s, c, v in zip(ga, s0, cs, vna)]
            vnb = [u_ref[rb, c] - _dot(w_ref[rb, c], s) for c, s in zip(cs, s1)]
            s2 = [g * s + _dot_tn(kd_ref[rb, c], v) for g, s, c, v in zip(gb, s1, cs, vnb)]
            for h, s in zip(hs, s2):
                st[h] = s
            oa = [_dot(qd_ref[ra, c], s) for c, s in zip(cs, s0)]
            ob = [_dot(qd_ref[rb, c], s) for c, s in zip(cs, s1)]
            for i_, h in enumerate(hs):
                vn = jnp.concatenate([vna[i_], vnb[i_]], axis=0)
                oh = jnp.concatenate([oa[i_], ob[i_]], axis=0) + _dot(p_ref[:, cs[i_]], vn)
                o_ref[:, cs[i_]] = oh
                vn_ref[:, cs[i_]] = vn
                ss_ref[h, 0:HD, :] = s0[i_]
                ss_ref[h, HD:2 * HD, :] = s1[i_]
                z = z_ref[:, cs[i_]]
                rs = lax.rsqrt(jnp.mean(oh * oh, axis=-1, keepdims=True) + RMS_EPS)
                y = oh * rs * nw_ref[...] * (z * _sigmoid(z))
                y_ref[:, cs[i_]] = y.astype(MXU_DT)
                yt_ref[cs[i_], :] = y.T.astype(MXU_DT)

    big = jax.ShapeDtypeStruct((s, NV * HD), F32)
    return pl.pallas_call(
        body, name="dn_seq", grid=(nb,),
        in_specs=[wide] * 5 + [pl.BlockSpec((1, NQ, 8, 128), lambda i: (i, 0, 0, 0)), wide, _whole((1, HD))],
        out_specs=[wide, wide, pl.BlockSpec((NV, 2 * HD, HD), lambda i: (0, i, 0)), wide, _cols(2 * CH, NV * HD)],
        out_shape=[big, big, jax.ShapeDtypeStruct((NV, 2 * s, HD), F32),
                   jax.ShapeDtypeStruct((s, NV * HD), MXU_DT), jax.ShapeDtypeStruct((NV * HD, s), MXU_DT)],
        scratch_shapes=[pltpu.VMEM((NV, HD, HD), F32)],
        compiler_params=_params(("arbitrary",)),
    )(u, w, qd, kd, p, gam, pz, nw)


def _ln1(x, mix, adam, lng, lnb):
    s = x.shape[0]
    tm = _pick(s, (512, 256, 128))

    def body(x_ref, m_ref, ada_ref, g_ref, b_ref, r_ref, x2_ref, h2_ref, h2t_ref):
        r = ALPHA * x_ref[...] + (1.0 + ada_ref[2:3, :]) * m_ref[...]
        xhat, _ = _ln_stats(r)
        x2 = xhat * g_ref[...] + b_ref[...]
        r_ref[...] = r
        x2_ref[...] = x2
        h2 = x2 * (1.0 + ada_ref[4:5, :]) + ada_ref[3:4, :]
        h2_ref[...] = h2.astype(MXU_DT)
        h2t_ref[...] = h2.T.astype(MXU_DT)

    return pl.pallas_call(
        body, name="ln1", grid=(s // tm,),
        in_specs=[_rows(tm, D), _rows(tm, D), _whole((8, D)), _whole((1, D)), _whole((1, D))],
        out_specs=[_rows(tm, D)] * 3 + [_cols(tm, D)],
        out_shape=[jax.ShapeDtypeStruct((s, D), F32), jax.ShapeDtypeStruct((s, D), F32),
                   jax.ShapeDtypeStruct((s, D), MXU_DT), jax.ShapeDtypeStruct((D, s), MXU_DT)],
        compiler_params=_params(("parallel",)),
    )(x, mix, adam, lng, lnb)


def _ffn_act(gu, cw, cb):
    s = gu.shape[0]
    tm = _pick(s, (256, 128))
    w = D_FF

    def body(g_ref, u_ref, cw_ref, cb_ref, o_ref, ot_ref, gc_ref, prev):
        @pl.when(pl.program_id(0) == 0)
        def _():
            prev[...] = jnp.zeros_like(prev)
        g = g_ref[...]
        gc = _conv_causal(g, prev[...], cw_ref, 3) + cb_ref[...]
        prev[...] = g[tm - 8:, :]
        gc_ref[...] = gc
        act = _gelu(gc) * u_ref[...]
        o_ref[...] = act.astype(MXU_DT)
        ot_ref[...] = act.T.astype(MXU_DT)

    return pl.pallas_call(
        body, name="ffn_act", grid=(s // tm,),
        in_specs=[_rows(tm, w, 0), _rows(tm, w, 1), _whole((8, w)), _whole((1, w))],
        out_specs=[_rows(tm, w), _cols(tm, w), _rows(tm, w)],
        out_shape=[jax.ShapeDtypeStruct((s, w), MXU_DT), jax.ShapeDtypeStruct((w, s), MXU_DT),
                   jax.ShapeDtypeStruct((s, w), F32)],
        scratch_shapes=[pltpu.VMEM((8, w), F32)],
        compiler_params=_params(("arbitrary",)),
    )(gu, gu, cw, cb)


def _ffn_bwd(dact, gu, gc, cw):
    s = dact.shape[0]
    tm = _pick(s, (256, 128))
    nt = s // tm
    w = D_FF

    def body(da_ref, g_ref, u_ref, gc_ref, cw_ref, o_ref, gcw_ref, gcb_ref, nxt):
        @pl.when(pl.program_id(0) == 0)
        def _():
            nxt[...] = jnp.zeros_like(nxt)
            gcw_ref[...] = jnp.zeros_like(gcw_ref)
            gcb_ref[...] = jnp.zeros_like(gcb_ref)
        gel, dgel = _gelu_and_grad(gc_ref[...])
        da = da_ref[...]
        dgc = da * u_ref[...] * dgel
        o_ref[:, w:] = (da * gel).astype(MXU_DT)
        o_ref[:, :w] = _conv_causal_bwd(dgc, nxt[...], cw_ref, 3, g_ref[...], gcw_ref).astype(MXU_DT)
        nxt[...] = dgc[:8, :]
        gcb_ref[...] += _rsum(dgc)

    return pl.pallas_call(
        body, name="ffn_bwd", grid=(nt,),
        in_specs=[_rows(tm, w, 0, nt), _rows(tm, w, 0, nt), _rows(tm, w, 1, nt), _rows(tm, w, 0, nt),
                  _whole((8, w))],
        out_specs=[_rows(tm, 2 * w, 0, nt), _whole((8, w)), _whole((1, w))],
        out_shape=[jax.ShapeDtypeStruct((s, 2 * w), MXU_DT), jax.ShapeDtypeStruct((8, w), F32),
                   jax.ShapeDtypeStruct((1, w), F32)],
        scratch_shapes=[pltpu.VMEM((8, w), F32)],
        compiler_params=_params(("arbitrary",)),
    )(dact, gu, gu, gc, cw)


def _dn_seq_bwd(do, qd, kd, p, w, vn, ssave, gam):
    s = do.shape[0]
    nb = s // (2 * CH)
    wide = pl.BlockSpec((2 * CH, NV * HD), lambda i: (nb - 1 - i, 0))
    gspec = pl.BlockSpec((1, NQ, 8, 128), lambda i: (nb - 1 - i, 0, 0, 0))

    def body(do_ref, qd_ref, kd_ref, p_ref, w_ref, vn_ref, ss_ref, gam_ref, dvn_ref, dkd_ref, dgam_ref, dst):
        @pl.when(pl.program_id(0) == 0)
        def _():
            dst[...] = jnp.zeros_like(dst)
        dgam_ref[...] = jnp.zeros_like(dgam_ref)
        ra, rb = slice(0, CH), slice(CH, 2 * CH)
        tot = lambda t: jnp.sum(jnp.sum(t, axis=1, keepdims=True), axis=0, keepdims=True)
        for g0 in range(0, NV, SEQ_GROUP):
            hs = list(range(g0, g0 + SEQ_GROUP))
            cs = [slice(h * HD, (h + 1) * HD) for h in hs]
            ga = [gam_ref[0, h // 2, 2 * (h % 2):2 * (h % 2) + 1, :] for h in hs]
            gb = [gam_ref[0, h // 2, 2 * (h % 2) + 1:2 * (h % 2) + 2, :] for h in hs]
            ds2 = [dst[h] for h in hs]
            pdo = [_dot_tn(p_ref[:, c], do_ref[:, c]) for c in cs]
            qdo_b = [_dot_tn(qd_ref[rb, c], do_ref[rb, c]) for c in cs]
            qdo_a = [_dot_tn(qd_ref[ra, c], do_ref[ra, c]) for c in cs]
            dvb = [p_[rb] + _dot(kd_ref[rb, c], d_) for p_, c, d_ in zip(pdo, cs, ds2)]
            ds1 = [g * d_ + q_ - _dot_tn(w_ref[rb, c], v_)
                   for g, d_, q_, c, v_ in zip(gb, ds2, qdo_b, cs, dvb)]
            dva = [p_[ra] + _dot(kd_ref[ra, c], d_) for p_, c, d_ in zip(pdo, cs, ds1)]
            ds0 = [g * d_ + q_ - _dot_tn(w_ref[ra, c], v_)
                   for g, d_, q_, c, v_ in zip(ga, ds1, qdo_a, cs, dva)]
            for h, d_ in zip(hs, ds0):
                dst[h] = d_
            for i_, h in enumerate(hs):
                c = cs[i_]
                row = 2 * (h % 2)
                dkd_ref[rb, c] = _dot_nt(vn_ref[rb, c], ds2[i_])
                dkd_ref[ra, c] = _dot_nt(vn_ref[ra, c], ds1[i_])
                dvn_ref[ra, c] = dva[i_]
                dvn_ref[rb, c] = dvb[i_]
                dgam_ref[0, h // 2, row:row + 1, :] = jnp.broadcast_to(tot(ds1[i_] * ss_ref[h, 0:HD, :]), (1, 128))
                dgam_ref[0, h // 2, row + 1:row + 2, :] = jnp.broadcast_to(
                    tot(ds2[i_] * ss_ref[h, HD:2 * HD, :]), (1, 128))

    big = jax.ShapeDtypeStruct((s, NV * HD), F32)
    return pl.pallas_call(
        body, name="dn_seq_bwd", grid=(nb,),
        in_specs=[wide] * 6 + [pl.BlockSpec((NV, 2 * HD, HD), lambda i: (0, nb - 1 - i, 0)), gspec],
        out_specs=[wide, wide, gspec],
        out_shape=[big, big, jax.ShapeDtypeStruct((nb, NQ, 8, 128), F32)],
        scratch_shapes=[pltpu.VMEM((NV, HD, HD), F32)],
        compiler_params=_params(("arbitrary",)),
    )(do, qd, kd, p, w, vn, ssave, gam)


def _dn_intra_bwd(q, k, v, rt, do, dvn, dkd, gam, dgam, ssave, tinv, u, w, vn):
    s = q.shape[0]
    nb = s // (2 * CH)
    qps = 8
    nh = 2 * qps
    blk = pl.BlockSpec((2 * CH, qps * HD), lambda i, h: (i, h))
    blk2 = pl.BlockSpec((2 * CH, nh * HD), lambda i, h: (i, h))
    gspec = pl.BlockSpec((1, qps, 8, 128), lambda i, h: (i, h, 0, 0))
    rspec = pl.BlockSpec((3 * NV, 2 * CH), lambda i, h: (0, i))

    def body(q_ref, k_ref, v_ref, rt_ref, do_ref, dvn_ref, dkd_ref, gam_ref, dgam_ref, ss_ref,
             ti_ref, u_ref, w_ref, vn_ref, dq_ref, dk_ref, dv_ref, drt_ref, acc):
        hstep = pl.program_id(1)

        @pl.when(hstep == 0)
        def _():
            acc[...] = jnp.zeros_like(acc)
        mc, ms = _pair_masks()
        ra, rb = slice(0, CH), slice(CH, 2 * CH)
        lane = _iota((1, 2 * CH), 1)
        hs = list(range(nh))
        qh = [h // 2 for h in hs]
        cs = [slice(h * HD, (h + 1) * HD) for h in hs]
        qq_ = [q_ref[:, t * HD:(t + 1) * HD] for t in range(qps)]
        kk_ = [k_ref[:, t * HD:(t + 1) * HD] for t in range(qps)]
        kk = [_dot_nt(k_, k_) for k_ in kk_]
        qk = [_dot_nt(q_, k_) for q_, k_ in zip(qq_, kk_)]
        cols = [_head_cols(rt_ref, nh * hstep + h) for h in hs]
        b_c = [c_[2] for c_ in cols]
        dec = [jnp.where(mc, jnp.exp(jnp.where(mc, c_[1] - c_[0], 0.0)), 0.0) for c_ in cols]
        eg = [jnp.exp(c_[1]) for c_ in cols]
        egl = [jnp.exp(c_[4] - c_[1]) for c_ in cols]
        dob = [do_ref[:, c] for c in cs]
        dvb = [dvn_ref[:, c] for c in cs]
        dqd = [jnp.concatenate([_dot_nt(d_[ra], ss_ref[h, 0:HD, :]), _dot_nt(d_[rb], ss_ref[h, HD:2 * HD, :])], axis=0)
               for h, d_ in zip(hs, dob)]
        dw = [-jnp.concatenate([_dot_nt(d_[ra], ss_ref[h, 0:HD, :]), _dot_nt(d_[rb], ss_ref[h, HD:2 * HD, :])], axis=0)
              for h, d_ in zip(hs, dvb)]
        dp = [jnp.where(mc, _dot_nt(d_, vn_ref[:, c]), 0.0) for d_, c in zip(dob, cs)]
        dbuw = [_dot3_tn(ti_ref[:, c], jnp.concatenate([d_, w_], axis=1)) for c, d_, w_ in zip(cs, dvb, dw)]
        dbu = [t[:, :HD] for t in dbuw]
        dbw = [t[:, HD:] for t in dbuw]
        da = [jnp.where(ms, -(_dot_nt(bu, u_ref[:, c]) + _dot_nt(bw, w_ref[:, c])), 0.0)
              for bu, bw, c in zip(dbu, dbw, cs)]
        dm = [a_ * d_ for a_, d_ in zip(da, dec)]
        dn_ = [p_ * d_ for p_, d_ in zip(dp, dec)]
        dbk = [_dot(m_, kk_[t]) for m_, t in zip(dm, qh)]
        dqs = [_dot(n_, kk_[t]) + e_ * q_ for n_, t, e_, q_ in zip(dn_, qh, eg, dqd)]
        dks = [_dot_tn(m_, b_ * kk_[t]) + _dot_tn(n_, qq_[t]) + el * dkd_ref[:, c] + b_ * (e_ * bw + bk)
               for m_, b_, t, n_, el, c, e_, bw, bk in zip(dm, b_c, qh, dn_, egl, cs, eg, dbw, dbk)]
        for t in range(qps):
            dq_ref[:, t * HD:(t + 1) * HD] = dqs[2 * t] + dqs[2 * t + 1]
            dk_ref[:, t * HD:(t + 1) * HD] = dks[2 * t] + dks[2 * t + 1]
        for h in hs:
            c, t, j = cs[h], qh[h], h % 2
            dv_ref[:, c] = b_c[h] * dbu[h]
            e = da[h] * (b_c[h] * kk[t] * dec[h]) + dp[h] * (qk[t] * dec[h])
            x = dkd_ref[:, c] * (egl[h] * kk_[t])
            egk = eg[h] * kk_[t]
            z = e + dqd[h] * (eg[h] * qq_[t]) - x + dbw[h] * (b_c[h] * egk)
            zb = dbw[h] * egk + dbu[h] * v_ref[:, c] + dbk[h] * kk_[t]
            sa = jnp.sum(jnp.sum(x[ra], axis=1, keepdims=True), axis=0, keepdims=True)
            sb = jnp.sum(jnp.sum(x[rb], axis=1, keepdims=True), axis=0, keepdims=True)
            la = sa + dgam_ref[0, t, 2 * j:2 * j + 1, :] * gam_ref[0, t, 2 * j:2 * j + 1, :]
            lb = sb + dgam_ref[0, t, 2 * j + 1:2 * j + 2, :] * gam_ref[0, t, 2 * j + 1:2 * j + 2, :]
            hg = nh * hstep + h
            acc[pl.ds(hg, 1), :] = _rsum(z.T - e)
            acc[pl.ds(NV + hg, 1), :] = _rsum(zb.T)
            acc[pl.ds(2 * NV + hg, 1), :] = jnp.where(lane < CH, la, lb)

        @pl.when(hstep == NQ // qps - 1)
        def _():
            drt_ref[...] = acc[...]

    return pl.pallas_call(
        body, name="dn_intra_bwd", grid=(nb, NQ // qps),
        in_specs=[blk, blk, blk2, rspec, blk2, blk2, blk2, gspec, gspec,
                  pl.BlockSpec((nh, 2 * HD, HD), lambda i, h: (h, i, 0)), blk2, blk2, blk2, blk2],
        out_specs=[blk, blk, blk2, rspec],
        out_shape=[jax.ShapeDtypeStruct((s, NQ * HD), F32), jax.ShapeDtypeStruct((s, NQ * HD), F32),
                   jax.ShapeDtypeStruct((s, NV * HD), F32), jax.ShapeDtypeStruct((3 * NV, s), F32)],
        scratch_shapes=[pltpu.VMEM((3 * NV, 2 * CH), F32)],
        compiler_params=_params(("parallel", "arbitrary")),
    )(q, k, v, rt, do, dvn, dkd, gam, dgam, ssave, tinv, u, w, vn)


def _l2n_heads_bwd(c, d_ref, dc_ref, off, scale):
    for hh in range(NQ):
        cs = slice(off + hh * HD, off + (hh + 1) * HD)
        x = c[:, cs]
        dy = d_ref[:, hh * HD:(hh + 1) * HD]
        r = lax.rsqrt(jnp.sum(x * x, axis=-1, keepdims=True) + L2_EPS)
        dc_ref[:, cs] = (scale * r) * (dy - x * (r * r) * jnp.sum(dy * x, axis=-1, keepdims=True))


def _dn_prep_bwd(dq, dk, dv, drt, pqkv, cpre, pab, cw, acol, dcol, dproj):
    s = pqkv.shape[0]
    tm = 128
    nt = s // tm
    wq = NQ * HD

    def body(dq_ref, dk_ref, dv_ref, drt_ref, x_ref, cp_ref, ab_ref, cw_ref, ac_ref,
             dc_ref, _, dx_ref, dab_ref, gcw_ref, gsc_ref, dcs, nxt):
        @pl.when(pl.program_id(0) == 0)
        def _():
            nxt[...] = jnp.zeros_like(nxt)
            gcw_ref[...] = jnp.zeros_like(gcw_ref)
            gsc_ref[...] = jnp.zeros_like(gsc_ref)
        cp = cp_ref[...]
        sg = _sigmoid(cp)
        c = cp * sg
        _l2n_heads_bwd(c, dq_ref, dcs, 0, Q_SCALE)
        _l2n_heads_bwd(c, dk_ref, dcs, wq, 1.0)
        dcs[:, 2 * wq:] = dv_ref[...]
        dcp = dcs[...] * (sg * (1.0 + cp * (1.0 - sg)))
        dx_ref[...] = _conv_causal_bwd(dcp, nxt[...], cw_ref, 4, x_ref[...], gcw_ref).astype(MXU_DT)
        nxt[...] = dcp[:8, :]
        lane = _iota((NV, tm), 1)
        dgt = drt_ref[0:NV, :] + jnp.where((lane & (CH - 1)) == CH - 1, drt_ref[2 * NV:3 * NV, :], 0.0)
        dg = _chunk_cumsum(dgt, 1, rev=True)
        abt = ab_ref[...].T
        zt = abt[0:NV, :] + dc_ref[...]
        gt = -jnp.exp(ac_ref[...]) * _softplus(zt)
        dat = dg * (-jnp.exp(ac_ref[...])) * _sigmoid(zt)
        bt = _sigmoid(abt[NV:2 * NV, :])
        dbt = drt_ref[NV:2 * NV, :] * bt * (1.0 - bt)
        full = jnp.concatenate([dat, dbt, jnp.zeros((128 - 2 * NV, tm), F32)], axis=0)
        dab_ref[...] = full.T.astype(MXU_DT)
        l2 = _iota((NV, 128), 1)
        gsc_ref[...] += jnp.where(l2 == 0, jnp.sum(dg * gt, axis=1, keepdims=True),
                                  jnp.where(l2 == 1, jnp.sum(dat, axis=1, keepdims=True), 0.0))

    return pl.pallas_call(
        body, name="dn_prep_bwd", grid=(nt,),
        in_specs=[_rows(tm, wq, 0, nt), _rows(tm, wq, 0, nt), _rows(tm, 2 * wq, 0, nt),
                  pl.BlockSpec((3 * NV, tm), lambda i: (0, nt - 1 - i)),
                  _rows(tm, 4 * wq, 0, nt), _rows(tm, 4 * wq, 0, nt), _rows(tm, 128, 0, nt),
                  _whole((8, 4 * wq)), _whole((NV, 1)), _whole((NV, 1)), _HBM],
        out_specs=[_rows(tm, 4 * wq, 0, nt), _rows(tm, 128, 0, nt), _whole((8, 4 * wq)), _whole((NV, 128))],
        out_shape=[jax.ShapeDtypeStruct(dproj.shape, MXU_DT), jax.ShapeDtypeStruct((s, 128), MXU_DT),
                   jax.ShapeDtypeStruct((8, 4 * wq), F32), jax.ShapeDtypeStruct((NV, 128), F32)],
        scratch_shapes=[pltpu.VMEM((tm, 4 * wq), F32), pltpu.VMEM((8, 4 * wq), F32)],
        compiler_params=_params(("arbitrary",)), input_output_aliases={10: 0},
    )(dq, dk, dv, drt, pqkv, cpre, pab, cw, acol, dcol, dproj)


def _rg_bwd(drec, px, xcs, h, cw, vec, wa, wx, wat, wxt, side=None):
    s = px.shape[0]
    tm = _pick(s, (256, 128))
    nt = s // tm
    w = D_RNN
    n_in = len(side.ins) if side else 0
    n_out = len(side.out_shapes) if side else 0

    def body(*refs):
        (dr_ref, xr_ref, xc_ref, gr_ref, h_ref, hb_ref, cw_ref, vec_ref, wa_ref, wx_ref, wat_ref,
         wxt_ref) = refs[:12]
        s_in = refs[12:12 + n_in]
        o_ref, gwa_ref, gwx_ref, gcw_ref, gvec_ref = refs[12 + n_in:17 + n_in]
        s_out = refs[17 + n_in:17 + n_in + n_out]
        nxt_a, nxt_l, nxt_d = refs[17 + n_in + n_out:20 + n_in + n_out]
        sems = refs[20 + n_in + n_out:]
        i = pl.program_id(0)
        if side:
            @pl.when(i == 0)
            def _():
                side.start(s_in, s_out, sems)

        @pl.when(i == 0)
        def _():
            nxt_a[...] = jnp.zeros_like(nxt_a)
            nxt_l[...] = jnp.zeros_like(nxt_l)
            nxt_d[...] = jnp.zeros_like(nxt_d)
            gwa_ref[...] = jnp.zeros_like(gwa_ref)
            gwx_ref[...] = jnp.zeros_like(gwx_ref)
            gcw_ref[...] = jnp.zeros_like(gcw_ref)
            gvec_ref[...] = jnp.zeros_like(gvec_ref)
        hbefore = jnp.where(i == nt - 1, 0.0, hb_ref[...])
        xc = xc_ref[...]
        r, ig, a, mult, sp, rmult = _rg_gates(xc, wa_ref, wx_ref, vec_ref)
        hh = h_ref[...]
        gel, dgel = _gelu_and_grad(gr_ref[...])
        drec_ = dr_ref[...]
        o_ref[:, w:] = (drec_ * hh * dgel).astype(MXU_DT)
        lam = _scan_rev(_shift_up(a, nxt_a[...], 1), drec_ * gel, nxt_l[0:1, :])
        nxt_a[...] = a[:8, :]
        nxt_l[...] = lam[:8, :]
        da = lam * _shift_down(hh, hbefore, 1)
        dxc = lam * mult * ig
        dla = da * a - (lam * ig * xc) * (a * a) * rmult
        dpr = dla * (-RG_C * sp) * r * (1.0 - r)
        dpi = (lam * mult * xc) * ig * (1.0 - ig)
        dprb = dpr.astype(MXU_DT)
        dpib = dpi.astype(MXU_DT)
        dxc = dxc + jnp.dot(dprb, wat_ref[...], preferred_element_type=F32) \
                  + jnp.dot(dpib, wxt_ref[...], preferred_element_type=F32)
        xcb = xc.astype(MXU_DT)
        for rb_ in range(w // 128):
            lo, hi = max(0, 128 * (rb_ - 1)), min(w, 128 * (rb_ + 2))
            rows_ = slice(128 * rb_, 128 * (rb_ + 1))
            gwa_ref[rows_, lo:hi] += _dot_tn(xcb[:, rows_], dprb[:, lo:hi])
            gwx_ref[rows_, lo:hi] += _dot_tn(xcb[:, rows_], dpib[:, lo:hi])
        o_ref[:, :w] = _conv_causal_bwd(dxc, nxt_d[...], cw_ref, 4, xr_ref[...], gcw_ref).astype(MXU_DT)
        nxt_d[...] = dxc[:8, :]
        gvec_ref[0:1, :] += _rsum(dxc)
        gvec_ref[1:2, :] += _rsum(dpr)
        gvec_ref[2:3, :] += _rsum(dpi)
        gvec_ref[3:4, :] += _rsum(dla * (-RG_C * r)) * (-_sigmoid(-vec_ref[3:4, :]))
        if side:
            @pl.when(i == nt - 1)
            def _():
                side.finish(s_in, s_out, sems)

    return pl.pallas_call(
        body, name="rg_bwd", grid=(nt,),
        in_specs=[_rows(tm, w, 0, nt), _rows(tm, w, 0, nt), _rows(tm, w, 0, nt), _rows(tm, w, 1, nt),
                  _rows(tm, w, 0, nt), _before(tm, w, 0, nt), _whole((8, w)), _whole((8, w)),
                  _whole((w, w)), _whole((w, w)), _whole((w, w)), _whole((w, w))] + [_HBM] * n_in,
        out_specs=[_rows(tm, 2 * w, 0, nt), _whole((w, w)), _whole((w, w)), _whole((8, w)), _whole((8, w))]
        + [_HBM] * n_out,
        out_shape=[jax.ShapeDtypeStruct((s, 2 * w), MXU_DT), jax.ShapeDtypeStruct((w, w), F32),
                   jax.ShapeDtypeStruct((w, w), F32), jax.ShapeDtypeStruct((8, w), F32),
                   jax.ShapeDtypeStruct((8, w), F32)] + (list(side.out_shapes) if side else []),
        scratch_shapes=[pltpu.VMEM((8, w), F32)] * 3 + (list(side.sems) if side else []),
        compiler_params=_params(("arbitrary",)),
    )(drec, px, xcs, px, h, h, cw, vec, wa, wx, wat, wxt, *(side.ins if side else []))


def _mm_epi(a, b, *, name, epi, extras=(), wholes=(), row_outs=(), acc_outs=(), tm=None, side=None):
    m, kk = a.shape
    n = b.shape[1]
    tm = min(tm, m) if tm else _pick(m, (512, 256, 128))
    tk = kk if kk <= 2816 else _pick(kk, (2816, 2176, 2048, 1024))
    nk, ni = kk // tk, m // tm
    n_ex, n_wh, n_ro, n_ao = len(extras), len(wholes), len(row_outs), len(acc_outs)
    n_si = len(side.ins) if side else 0
    n_so = len(side.out_shapes) if side else 0
    placed = [(j_, r_[2]) for j_, r_ in enumerate(row_outs) if len(r_) == 3]

    def body(*refs):
        a_ref, b_ref = refs[:2]
        p = 2
        ex, p = refs[p:p + n_ex], p + n_ex
        wh, p = refs[p:p + n_wh], p + n_wh
        s_in, p = refs[p:p + n_si], p + n_si
        p += len(placed)
        ro, p = refs[p:p + n_ro], p + n_ro
        ao, p = refs[p:p + n_ao], p + n_ao
        s_out, p = refs[p:p + n_so], p + n_so
        i, k = pl.program_id(0), pl.program_id(1)
        first = (i == 0) & (k == 0)
        if side:
            sems = refs[p + (1 if nk > 1 else 0):]

            @pl.when(first)
            def _():
                side.start(s_in, s_out, sems)
        if n_ao:
            @pl.when(first)
            def _():
                for r_ in ao:
                    r_[...] = jnp.zeros_like(r_)
        if nk == 1:
            epi(_dot(a_ref[...], b_ref[...]), ex, wh, ro, ao)
        else:
            acc = refs[p]

            @pl.when(k == 0)
            def _():
                acc[...] = jnp.zeros_like(acc)
            acc[...] += _dot(a_ref[...], b_ref[...])

            @pl.when(k == nk - 1)
            def _():
                epi(acc[...], ex, wh, ro, ao)
        if side:
            @pl.when((i == ni - 1) & (k == nk - 1))
            def _():
                side.finish(s_in, s_out, sems)

    in_specs = [pl.BlockSpec((tm, tk), lambda i, k: (i, k)), pl.BlockSpec((tk, n), lambda i, k: (k, 0))]
    in_specs += [pl.BlockSpec((tm, w_), functools.partial(lambda i, k, c_: (i, c_), c_=c_)) for _, w_, c_ in extras]
    in_specs += [_whole(x_.shape) for x_ in wholes] + [_HBM] * (n_si + len(placed))
    out_specs, out_shape = [], []
    for r_ in row_outs:
        blk = r_[2][1] if len(r_) == 3 else 0
        out_specs.append(pl.BlockSpec((tm, r_[0]), functools.partial(lambda i, k, c_: (i, c_), c_=blk)))
        out_shape.append(jax.ShapeDtypeStruct(r_[2][0].shape if len(r_) == 3 else (m, r_[0]), r_[1]))
    out_specs += [_whole(sh) for sh in acc_outs] + [_HBM] * n_so
    out_shape += [jax.ShapeDtypeStruct(sh, F32) for sh in acc_outs] + (list(side.out_shapes) if side else [])
    scratch = ([pltpu.VMEM((tm, n), F32)] if nk > 1 else []) + (list(side.sems) if side else [])
    first_placed = 2 + n_ex + n_wh + n_si
    return pl.pallas_call(
        body, name=name, grid=(ni, nk), in_specs=in_specs, out_specs=out_specs, out_shape=out_shape,
        scratch_shapes=scratch, compiler_params=_params(("arbitrary", "arbitrary")),
        input_output_aliases={first_placed + q_: j_ for q_, (j_, _) in enumerate(placed)},
    )(a, b, *[x_ for x_, _, _ in extras], *wholes, *(side.ins if side else []), *[pb_[0] for _, pb_ in placed])


def _mm_pro(b, *, name, pro, m, extras=(), wholes=(), row_outs=(), col_outs=(), tm=256):
    kk, n = b.shape
    tn = _pick(n, (1408, 1024, 512, 256, 128))
    tm = min(tm, m)
    n_ex, n_wh, n_ro, n_co = len(extras), len(wholes), len(row_outs), len(col_outs)

    def body(*refs):
        ex = refs[:n_ex]
        wh = refs[n_ex:n_ex + n_wh]
        b_ref, o_ref = refs[n_ex + n_wh], refs[n_ex + n_wh + 1]
        ro = refs[n_ex + n_wh + 2:n_ex + n_wh + 2 + n_ro]
        co = refs[n_ex + n_wh + 2 + n_ro:n_ex + n_wh + 2 + n_ro + n_co]
        a_scr = refs[-1]

        @pl.when(pl.program_id(1) == 0)
        def _():
            a_scr[...] = pro(ex, wh, ro, co).astype(MXU_DT)
        o_ref[...] = jnp.dot(a_scr[...], b_ref[...], preferred_element_type=F32)

    in_specs = [pl.BlockSpec((tm, w_), functools.partial(lambda i, j, c_: (i, c_), c_=c_)) for _, w_, c_ in extras]
    in_specs += [_whole(x_.shape) for x_ in wholes] + [pl.BlockSpec((kk, tn), lambda i, j: (0, j))]
    out_specs = [pl.BlockSpec((tm, tn), lambda i, j: (i, j))]
    out_specs += [pl.BlockSpec((tm, w_), lambda i, j: (i, 0)) for w_, _ in row_outs]
    out_specs += [pl.BlockSpec((w_, tm), lambda i, j: (0, i)) for w_, _ in col_outs]
    out_shape = [jax.ShapeDtypeStruct((m, n), F32)] + [jax.ShapeDtypeStruct((m, w_), dt) for w_, dt in row_outs]
    out_shape += [jax.ShapeDtypeStruct((w_, m), dt) for w_, dt in col_outs]
    return pl.pallas_call(
        body, name=name, grid=(m // tm, n // tn), in_specs=in_specs, out_specs=out_specs, out_shape=out_shape,
        scratch_shapes=[pltpu.VMEM((tm, kk), MXU_DT)], compiler_params=_params(("arbitrary", "arbitrary")),
    )(*[x_ for x_, _, _ in extras], *wholes, b.astype(MXU_DT))


def _merge_proj_out(pg, ya, yb, w_out):
    def pro(ex, wh, ro, co):
        ga_ref, gb_ref, ya_ref, yb_ref = ex
        mg = _sigmoid(ga_ref[...]) * ya_ref[...] + _sigmoid(gb_ref[...]) * yb_ref[...]
        co[0][...] = mg.T.astype(MXU_DT)
        return mg

    return _mm_pro(w_out, name="merge_proj_out", pro=pro, m=ya.shape[0],
                   extras=[(pg, D, 0), (pg, D, 1), (ya, D, 0), (yb, D, 0)], col_outs=[(D, MXU_DT)])


def _ffn_down_ln2_loss(act, w_down, x2, tgt, adam, lng, lnb):
    def epi(ff_, ex, wh, ro, ao):
        x_ref, t_ref = ex
        ada_ref, g_ref, b_ref = wh
        dff_ref, dx_ref = ro
        red_ref, = ao
        r = ALPHA * x_ref[...] + (1.0 + ada_ref[5:6, :]) * ff_
        xhat, rstd = _ln_stats(r)
        err = xhat * g_ref[...] + b_ref[...] - t_ref[...]
        dy = err * (1.0 / D)
        dr = _ln_bwd(dy, xhat, rstd, g_ref[...])
        dff_ref[...] = ((1.0 + ada_ref[5:6, :]) * dr).astype(MXU_DT)
        dx_ref[...] = ALPHA * dr
        red_ref[0:1, :] += _rsum(dy * xhat)
        red_ref[1:2, :] += _rsum(dy)
        red_ref[2:3, :] += _rsum(dr * ff_)
        red_ref[3:4, :] += jnp.sum(_rsum(err * err), axis=1, keepdims=True) * (0.5 / D)

    return _mm_epi(act, w_down, name="ffn_down_ln2_loss", epi=epi, extras=[(x2, D, 0), (tgt, D, 0)],
                   wholes=[adam, lng, lnb], row_outs=[(D, MXU_DT), (D, F32)], acc_outs=[(8, D)])


def _d_h1_modulate_bwd(dproj, w_t, dxa, x, adam, side):
    def epi(dh, ex, wh, ro, ao):
        dxa_ref, x_ref = ex
        ada_ref, = wh
        ro[0][...] = dxa_ref[...] + dh * (1.0 + ada_ref[1:2, :])
        ao[0][0:1, :] += _rsum(dh * x_ref[...])
        ao[0][1:2, :] += _rsum(dh)

    return _mm_epi(dproj, w_t, name="d_h1", epi=epi, extras=[(dxa, D, 0), (x, D, 0)], wholes=[adam],
                   row_outs=[(D, F32)], acc_outs=[(8, D)], side=side)


def _d_merged_bwd(dmix, w_out_t, pg, ya, yb, dproj):
    def epi(d, ex, wh, ro, ao):
        ga_ref, gb_ref, ya_ref, yb_ref = ex
        dya_ref, dyb_ref, dpg_ref = ro
        sa = _sigmoid(ga_ref[...])
        sb = _sigmoid(gb_ref[...])
        dya_ref[...] = (d * sa).astype(MXU_DT)
        dyb_ref[...] = (d * sb).astype(MXU_DT)
        dpg_ref[:, :D] = (d * ya_ref[...] * sa * (1.0 - sa)).astype(MXU_DT)
        dpg_ref[:, D:] = (d * yb_ref[...] * sb * (1.0 - sb)).astype(MXU_DT)

    return _mm_epi(dmix, w_out_t, name="d_merged", epi=epi,
                   extras=[(pg, D, 0), (pg, D, 1), (ya, D, 0), (yb, D, 0)],
                   row_outs=[(D, MXU_DT), (D, MXU_DT), (2 * D, MXU_DT, (dproj, 3))])


def _d_dn_post_bwd(dyb, w_pb_t, o, pz, nw, dproj):
    def epi(d_all, ex, wh, ro, ao):
        o_ref, z_ref = ex
        nw_ref, = wh
        do_ref, dz_ref = ro
        acc = jnp.zeros((1, HD), F32)
        for h in range(NV):
            cs = slice(h * HD, (h + 1) * HD)
            oh = o_ref[:, cs]
            z = z_ref[:, cs]
            d = d_all[:, cs]
            sg = _sigmoid(z)
            rs = lax.rsqrt(jnp.mean(oh * oh, axis=-1, keepdims=True) + RMS_EPS)
            n = oh * rs
            dz_ref[:, cs] = (d * n * nw_ref[...] * sg * (1.0 + z * (1.0 - sg))).astype(MXU_DT)
            dn_ = d * (z * sg)
            acc = acc + _rsum(dn_ * n)
            dnn = dn_ * nw_ref[...]
            do_ref[:, cs] = rs * (dnn - n * jnp.mean(dnn * n, axis=-1, keepdims=True))
        ao[0][...] += acc

    return _mm_epi(dyb, w_pb_t, name="d_dn", epi=epi, extras=[(o, NV * HD, 0), (pz, NV * HD, 0)], wholes=[nw],
                   row_outs=[(NV * HD, F32), (NV * HD, MXU_DT, (dproj, 2))], acc_outs=[(1, HD)], tm=512)


def _d_h2_ln1_bwd(dgu, w_gu_t, dx2a, x2, r1, mix, adam, lng):
    def epi(dh, ex, wh, ro, ao):
        dxa_ref, x2_ref, r_ref, m_ref = ex
        ada_ref, g_ref = wh
        dm_ref, dx_ref = ro
        red_ref, = ao
        dx2 = dxa_ref[...] + dh * (1.0 + ada_ref[4:5, :])
        xhat, rstd = _ln_stats(r_ref[...])
        dr = _ln_bwd(dx2, xhat, rstd, g_ref[...])
        dm_ref[...] = ((1.0 + ada_ref[2:3, :]) * dr).astype(MXU_DT)
        dx_ref[...] = ALPHA * dr
        red_ref[0:1, :] += _rsum(dh * x2_ref[...])
        red_ref[1:2, :] += _rsum(dh)
        red_ref[2:3, :] += _rsum(dx2 * xhat)
        red_ref[3:4, :] += _rsum(dx2)
        red_ref[4:5, :] += _rsum(dr * m_ref[...])

    return _mm_epi(dgu, w_gu_t, name="d_h2", epi=epi,
                   extras=[(dx2a, D, 0), (x2, D, 0), (r1, D, 0), (mix, D, 0)], wholes=[adam, lng],
                   row_outs=[(D, MXU_DT), (D, F32)], acc_outs=[(8, D)], tm=512)


def _adamw(parts, w, m, v, name):
    r, c = w.shape
    n_parts = len(parts)
    tm = _row_tile(r, c * 4 * (n_parts + 7))
    c1 = 1.0 - ADAM_B1 ** ADAM_STEP
    c2 = 1.0 - ADAM_B2 ** ADAM_STEP

    def body(*refs):
        g = refs[0][...]
        for p_ref in refs[1:n_parts]:
            g = g + p_ref[...]
        w_ref, m_ref, v_ref, g_out, d_out, m_out, v_out = refs[n_parts:]
        mn = ADAM_B1 * m_ref[...] + (1.0 - ADAM_B1) * g
        vn = ADAM_B2 * v_ref[...] + (1.0 - ADAM_B2) * (g * g)
        g_out[...] = g
        m_out[...] = mn
        v_out[...] = vn
        d_out[...] = -ADAM_LR * ((mn / c1) / (jnp.sqrt(vn / c2) + ADAM_EPS) + ADAM_WD * w_ref[...])

    spec = pl.BlockSpec((tm, c), lambda i: (i, 0))
    return pl.pallas_call(
        body, name=name, grid=(r // tm,),
        in_specs=[spec] * (n_parts + 3), out_specs=[spec] * 4,
        out_shape=[jax.ShapeDtypeStruct((r, c), F32)] * 4, compiler_params=_params(("parallel",)),
    )(*parts, w, m, v)


def _row_tile(r, bytes_per_row):
    for t in (512, 256, 128, 64, 32, 16):
        if r % t == 0 and 2 * t * bytes_per_row <= 20 * 1024 * 1024:
            return t
    return _pick(r, (16, 8))


def _sum_partials(own, recv, name):
    r, c = own.shape
    tm = _row_tile(r, c * (4 + 4 + 3 * 2))

    def body(o_ref, r_ref, out_ref):
        out_ref[...] = ((o_ref[...] + r_ref[0].astype(F32)) + r_ref[1].astype(F32)) + r_ref[2].astype(F32)

    return pl.pallas_call(
        body, name=name, grid=(r // tm,),
        in_specs=[pl.BlockSpec((tm, c), lambda i: (i, 0)), pl.BlockSpec((3, tm, c), lambda i: (0, i, 0))],
        out_specs=pl.BlockSpec((tm, c), lambda i: (i, 0)),
        out_shape=jax.ShapeDtypeStruct((r, c), F32), compiler_params=_params(("parallel",)),
    )(own, recv)


def _sum8(g, name):
    _, r, c = g.shape
    tm = _pick(r, (256, 128, 64, 32, 16, 8))

    def body(g_ref, o_ref):
        acc = g_ref[0]
        for k in range(1, 8):
            acc = acc + g_ref[k]
        o_ref[...] = acc

    return pl.pallas_call(
        body, name=name, grid=(r // tm,),
        in_specs=[pl.BlockSpec((8, tm, c), lambda i: (0, i, 0))],
        out_specs=pl.BlockSpec((tm, c), lambda i: (i, 0)),
        out_shape=jax.ShapeDtypeStruct((r, c), F32), compiler_params=_params(("parallel",)),
    )(g)


def _silu_rows(x):
    def body(x_ref, o_ref):
        xx = x_ref[...]
        o_ref[...] = xx * _sigmoid(xx)

    return pl.pallas_call(body, name="silu_rows", out_shape=jax.ShapeDtypeStruct(x.shape, F32))(x)


def _coords():
    return lax.axis_index("x"), lax.axis_index("y"), lax.axis_index("c")


def _allgather8(v, name):
    r, n = v.shape

    def body(v_ref, out_ref, send_sems, recv_sems):
        x, y, c = _coords()
        me = 4 * x + 2 * y + c
        out_ref[me] = v_ref[...]
        peers = []
        for k in range(1, 8):
            px = 1 - x if k & 4 else x
            py = 1 - y if k & 2 else y
            pc = 1 - c if k & 1 else c
            peers.append((px, py, pc))

        def copy(k, slot, to):
            return pltpu.make_async_remote_copy(
                src_ref=v_ref, dst_ref=out_ref.at[slot], send_sem=send_sems.at[k], recv_sem=recv_sems.at[k],
                device_id=to, device_id_type=MESH)

        sends = [copy(k, me, p) for k, p in enumerate(peers)]
        for cp in sends:
            cp.start()
        for k, (px, py, pc) in enumerate(peers):
            copy(k, 4 * px + 2 * py + pc, (px, py, pc)).wait_recv()
        for cp in sends:
            cp.wait_send()

    return pl.pallas_call(
        body, name=name, out_shape=jax.ShapeDtypeStruct((8, r, n), v.dtype),
        in_specs=[pl.BlockSpec(memory_space=pltpu.VMEM)], out_specs=pl.BlockSpec(memory_space=pltpu.VMEM),
        scratch_shapes=[pltpu.SemaphoreType.DMA((7,)), pltpu.SemaphoreType.DMA((7,))],
        compiler_params=pltpu.CompilerParams(vmem_limit_bytes=VMEM_LIMIT),
    )(v)


def _other_chips(x, y):
    return [(1 - x, y), (x, 1 - y), (1 - x, 1 - y)]


_HBM = pl.BlockSpec(memory_space=pl.ANY)


class _Exchange:
    def __init__(self, ins, out_shapes, sems, start, finish):
        self.ins, self.out_shapes, self.sems, self.start, self.finish = ins, out_shapes, sems, start, finish


def _run_exchange(ex, name):
    n_in, n_out = len(ex.ins), len(ex.out_shapes)

    def body(*refs):
        ins, outs, sems = refs[:n_in], refs[n_in:n_in + n_out], refs[n_in + n_out:]
        ex.start(ins, outs, sems)
        ex.finish(ins, outs, sems)

    return pl.pallas_call(body, name=name, out_shape=list(ex.out_shapes), in_specs=[_HBM] * n_in,
                          out_specs=[_HBM] * n_out, scratch_shapes=list(ex.sems))(*ex.ins)


def _gather_exchange(shards):
    n = len(shards)

    def plan(ins, outs, sems):
        send_sems, recv_sems, local_sems = sems
        x, y, c = _coords()
        s_me = 2 * x + y
        chips = _other_chips(x, y)

        def half(i, slot, hc):
            rh = shards[i].shape[0] // 2
            return outs[i].at[slot, pl.ds(pl.multiple_of(hc * rh, 16), rh), :]

        def copy(i, k, src, dst, to):
            return pltpu.make_async_remote_copy(src_ref=src, dst_ref=dst, send_sem=send_sems.at[6 * i + k],
                                                recv_sem=recv_sems.at[6 * i + k], device_id=to, device_id_type=MESH)

        local = [pltpu.make_async_copy(ins[i], outs[i].at[s_me], local_sems.at[i]) for i in range(n)]
        first = []
        for i in range(n):
            rh = shards[i].shape[0] // 2
            my_half = ins[i].at[pl.ds(pl.multiple_of(c * rh, 16), rh), :]
            first += [copy(i, j, my_half, half(i, s_me, c), (px, py, c)) for j, (px, py) in enumerate(chips)]
        return (x, y, c), chips, half, copy, local, first

    def start(ins, outs, sems):
        _, _, _, _, local, first = plan(ins, outs, sems)
        for cp in local + first:
            cp.start()

    def finish(ins, outs, sems):
        (x, y, c), chips, half, copy, local, first = plan(ins, outs, sems)
        sibling = (x, y, 1 - c)
        passed = []
        for i in range(n):
            for j, (px, py) in enumerate(chips):
                land = half(i, 2 * px + py, c)
                copy(i, j, land, land, (px, py, c)).wait_recv()
                fw = copy(i, 3 + j, land, land, sibling)
                fw.start()
                passed.append(fw)
        for i in range(n):
            for j, (px, py) in enumerate(chips):
                land = half(i, 2 * px + py, 1 - c)
                copy(i, 3 + j, land, land, sibling).wait_recv()
        for cp in first + passed:
            cp.wait_send()
        for cp in local:
            cp.wait()

    return _Exchange(list(shards), [jax.ShapeDtypeStruct((4,) + v.shape, v.dtype) for v in shards],
                     [pltpu.SemaphoreType.DMA((6 * n,)), pltpu.SemaphoreType.DMA((6 * n,)),
                      pltpu.SemaphoreType.DMA((n,))], start, finish)


def _scatter_exchange(gs):
    n = len(gs)

    def copies(ins, outs, sems):
        send_sems, recv_sems = sems
        x, y, c = _coords()
        return [pltpu.make_async_remote_copy(
                    src_ref=ins[i].at[2 * px + py], dst_ref=outs[i].at[j], send_sem=send_sems.at[3 * i + j],
                    recv_sem=recv_sems.at[3 * i + j], device_id=(px, py, c), device_id_type=MESH)
                for i in range(n) for j, (px, py) in enumerate(_other_chips(x, y))]

    def start(ins, outs, sems):
        for cp in copies(ins, outs, sems):
            cp.start()

    def finish(ins, outs, sems):
        cps = copies(ins, outs, sems)
        for cp in cps:
            cp.wait_recv()
        for cp in cps:
            cp.wait_send()

    return _Exchange(list(gs), [jax.ShapeDtypeStruct((3,) + g.shape[1:], g.dtype) for g in gs],
                     [pltpu.SemaphoreType.DMA((3 * n,)), pltpu.SemaphoreType.DMA((3 * n,))], start, finish)


def _allgather_exchange(v):
    def copies(ins, outs, sems):
        send_sems, recv_sems, _ = sems
        x, y, c = _coords()
        me = 4 * x + 2 * y + c
        cps = []
        for k in range(1, 8):
            peer = (1 - x if k & 4 else x, 1 - y if k & 2 else y, 1 - c if k & 1 else c)
            cps.append(pltpu.make_async_remote_copy(
                src_ref=ins[0], dst_ref=outs[0].at[me], send_sem=send_sems.at[k - 1], recv_sem=recv_sems.at[k - 1],
                device_id=peer, device_id_type=MESH))
        return me, cps

    def start(ins, outs, sems):
        me, cps = copies(ins, outs, sems)
        pltpu.make_async_copy(ins[0], outs[0].at[me], sems[2]).start()
        for cp in cps:
            cp.start()

    def finish(ins, outs, sems):
        me, cps = copies(ins, outs, sems)
        for cp in cps:
            cp.wait_recv()
        for cp in cps:
            cp.wait_send()
        pltpu.make_async_copy(ins[0], outs[0].at[me], sems[2]).wait()

    return _Exchange([v], [jax.ShapeDtypeStruct((8,) + v.shape, v.dtype)],
                     [pltpu.SemaphoreType.DMA((7,)), pltpu.SemaphoreType.DMA((7,)), pltpu.SemaphoreType.DMA],
                     start, finish)


def _swap_sibling(vs):
    n = len(vs)

    def body(*refs):
        ins, outs = refs[:n], refs[n:2 * n]
        send_sems, recv_sems = refs[2 * n:]
        x, y, c = _coords()
        cps = [pltpu.make_async_remote_copy(src_ref=ins[i], dst_ref=outs[i], send_sem=send_sems.at[i],
                                            recv_sem=recv_sems.at[i], device_id=(x, y, 1 - c), device_id_type=MESH)
               for i in range(n)]
        for cp in cps:
            cp.start()
        for cp in cps:
            cp.wait()

    return pl.pallas_call(
        body, name="swap_sibling", out_shape=[jax.ShapeDtypeStruct(v.shape, v.dtype) for v in vs],
        in_specs=[_HBM] * n, out_specs=[_HBM] * n,
        scratch_shapes=[pltpu.SemaphoreType.DMA((n,)), pltpu.SemaphoreType.DMA((n,))],
    )(*vs)


def _pad_rows(a, rows):
    return jnp.pad(a, ((0, rows - a.shape[0]), (0, 0)))


def _block_diag(w):
    eye = jnp.eye(RG_BLOCKS, dtype=w.dtype)
    return (eye[:, None, :, None] * w[:, :, None, :]).reshape(D_RNN, D_RNN)


def _diag_blocks(g):
    g4 = g.reshape(RG_BLOCKS, RG_BW, RG_BLOCKS, RG_BW)
    idx = jnp.arange(RG_BLOCKS)
    return g4[idx, :, idx, :]


def _prepare_rest(p):
    w = {}
    for k_, n_ in (("pa", "w_proj_a"), ("pb", "w_proj_b"), ("out", "w_out"), ("down", "ffn_w_down")):
        w[k_] = p[n_].astype(MXU_DT)
        w[k_ + "_t"] = w[k_].T
    w["gu"] = jnp.concatenate([p["ffn_w_gate"], p["ffn_w_up"]], axis=1).astype(MXU_DT)
    w["gu_t"] = w["gu"].T
    return w


def _prepare_first(p):
    w = {}
    wi = p["w_in"].astype(MXU_DT)
    cat = jnp.concatenate([wi[:, 2560:6656], wi[:, 6656:8704], wi[:, 8736:10784], wi[:, 0:2560],
                           wi[:, 8704:8736], jnp.zeros((D, 96), MXU_DT)], axis=1)
    w["in_cat"], w["in_cat_t"] = cat, cat.T
    w["rg_cw"] = _pad_rows(p["rg_conv_w"], 8)
    w["dn_cw"] = _pad_rows(p["dn_conv_w"], 8)
    w["ffn_cw"] = _pad_rows(p["ffn_conv_w"], 8)
    w["rg_vec"] = _pad_rows(jnp.stack([p["rg_conv_b"], p["rg_b_a"], p["rg_b_x"], p["rg_lambda"]]), 8)
    w["wa"] = _block_diag(p["rg_w_a"]).astype(MXU_DT)
    w["wx"] = _block_diag(p["rg_w_x"]).astype(MXU_DT)
    w["wa_t"], w["wx_t"] = w["wa"].T, w["wx"].T
    w["arow"] = jnp.pad(p["dn_a_log"], (0, 128 - NV))[None, :]
    w["drow"] = jnp.pad(p["dn_dt_bias"], (0, 128 - NV))[None, :]
    w["acol"] = p["dn_a_log"][:, None]
    w["dcol"] = p["dn_dt_bias"][:, None]
    w["nw"] = p["dn_norm_w"][None, :]
    w["ffn_cb"] = p["ffn_conv_b"][None, :]
    for n_ in ("ln1_g", "ln1_b", "ln2_g", "ln2_b"):
        w[n_] = p[n_][None, :]
    return w


def _mm_sided(a, b, side, **kw):
    if side is None:
        return _mm(a, b, **kw), []
    res = _mm(a, b, side=side, **kw)
    return res[0], res[1:]


_SMALL = ("rg_conv_w", "dn_conv_w", "ffn_conv_w", "rg_conv_b", "rg_w_a", "rg_b_a", "rg_w_x", "rg_b_x",
          "rg_lambda", "dn_a_log", "dn_dt_bias", "dn_norm_w", "ln1_g", "ln1_b", "ffn_conv_b", "ln2_g", "ln2_b")
_REST_A = ("w_proj_a", "w_proj_b", "w_out")
_REST_B = ("ffn_w_gate", "ffn_w_up", "ffn_w_down")


def _local_step(x, tgt, adam, p, shards=None):
    w = _prepare_first(p)
    side_a = side_b = None
    if shards is not None:
        side_a = _gather_exchange([shards[n] for n in _REST_A])
        side_b = _gather_exchange([shards[n] for n in _REST_B])
    h1, h1t = _modulate(x, adam)
    cat = w["in_cat"]
    pqkv, got_b = _mm_sided(h1, cat[:, 0:4096], side_b, name="proj_qkv")
    px, got_a = _mm_sided(h1, cat[:, 8192:10752], side_a, name="proj_x")
    if shards is not None:
        p = dict(p, **{n: _unstack_shards(n, a_) for n, a_ in zip(_REST_A + _REST_B, got_a + got_b)})
    w.update(_prepare_rest(p))
    pz = _mm(h1, cat[:, 4096:6144], name="proj_z")
    pg = _mm(h1, cat[:, 6144:8192], name="proj_g")
    pab = _mm(h1, cat[:, 10752:10880], name="proj_ab")
    hrec, xcs, rec, rec_t = _rg_fwd(px, w["rg_cw"], w["rg_vec"], w["wa"], w["wx"])
    q, k, v, rt, cpre = _dn_prep(pqkv, pab, w["dn_cw"], w["arow"], w["drow"])
    u, ww, qd, kd, pm, tinv, gam = _dn_intra(q, k, v, rt)
    o, vn, ssave, dn, dn_t = _dn_seq(u, ww, qd, kd, pm, gam, pz, w["nw"])
    ya = _mm(rec, w["pa"], name="proj_a")
    yb = _mm(dn, w["pb"], name="proj_b")
    mix, merged_t = _merge_proj_out(pg, ya, yb, w["out"])
    r1, x2, h2, h2_t = _ln1(x, mix, adam, w["ln1_g"], w["ln1_b"])
    gu = _mm(h2, w["gu"], name="ffn_gu")
    act, act_t, gcf = _ffn_act(gu, w["ffn_cw"], w["ffn_cb"])
    dff, dx2a, red2 = _ffn_down_ln2_loss(act, w["down"], x2, tgt, adam, w["ln2_g"], w["ln2_b"])
    g = {}
    dact = _mm(dff, w["down_t"], name="d_act")
    g["ffn_w_down"] = _mm(act_t, dff, name="g_down")
    dgu, gcw_f, gcb_f = _ffn_bwd(dact, gu, gcf, w["ffn_cw"])
    ggu = _mm(h2_t, dgu, name="g_gu")
    g["ffn_w_gate"], g["ffn_w_up"] = ggu[:, :D_FF], ggu[:, D_FF:]
    g["ffn_conv_w"], g["ffn_conv_b"] = gcw_f[0:3], gcb_f[0]
    dmix, dxa, red1 = _d_h2_ln1_bwd(dgu, w["gu_t"], dx2a, x2, r1, mix, adam, w["ln1_g"])
    g["w_out"] = _mm(merged_t, dmix, name="g_out")
    dproj = lax.empty((x.shape[0], N_CAT), MXU_DT)
    dya, dyb, dproj = _d_merged_bwd(dmix, w["out_t"], pg, ya, yb, dproj)
    drec = _mm(dya, w["pa_t"], name="d_rec")
    g["w_proj_a"] = _mm(rec_t, dya, name="g_pa")
    g["w_proj_b"] = _mm(dn_t, dyb, name="g_pb")
    do, dproj, gnw = _d_dn_post_bwd(dyb, w["pb_t"], o, pz, w["nw"], dproj)
    dvn, dkd, dgam = _dn_seq_bwd(do, qd, kd, pm, ww, vn, ssave, gam)
    dq, dk, dv, drt = _dn_intra_bwd(q, k, v, rt, do, dvn, dkd, gam, dgam, ssave, tinv, u, ww, vn)
    dproj, dpab, gcw_d, gsc = _dn_prep_bwd(dq, dk, dv, drt, pqkv, cpre, pab, w["dn_cw"], w["acol"], w["dcol"],
                                           dproj)
    side_r = None
    if shards is not None:
        side_r = _scatter_exchange([_stack_shards(n, g[n]).astype(MXU_DT) for n in _REST_A + _REST_B])
    dpx, gwa, gwx, gcw_r, gvec, *got_r = _rg_bwd(drec, px, xcs, hrec, w["rg_cw"], w["rg_vec"], w["wa"], w["wx"],
                                                 w["wa_t"], w["wx_t"], side_r)
    dproj = lax.dynamic_update_slice(dproj, jnp.concatenate([dpx, dpab], axis=1), (0, 8192))
    reorder = lambda gc: jnp.concatenate([gc[:, 8192:10752], gc[:, 0:4096], gc[:, 4096:6144], gc[:, 10752:10784],
                                          gc[:, 6144:8192]], axis=1)
    wire = lambda gh: _scatter_exchange([_stack_shards("w_in", gh).astype(MXU_DT)]) if shards is not None else None
    g["rg_conv_w"], g["rg_conv_b"] = gcw_r[0:4], gvec[0]
    g["rg_w_a"], g["rg_w_x"] = _diag_blocks(gwa), _diag_blocks(gwx)
    g["rg_b_a"], g["rg_b_x"], g["rg_lambda"] = gvec[1], gvec[2], gvec[3]
    g["dn_conv_w"] = gcw_d[0:4]
    g["dn_a_log"], g["dn_dt_bias"], g["dn_norm_w"] = gsc[:, 0], gsc[:, 1], gnw[0]
    g["ln1_g"], g["ln1_b"] = red1[2], red1[3]
    g["ln2_g"], g["ln2_b"] = red2[0], red2[1]
    side_s = None
    if shards is not None:
        side_s = _allgather_exchange(_pack([jnp.full((1,), red2[3, 0], F32)] + [g[n] for n in _SMALL], 128, 64))
    g_top, got_s = _mm_sided(h1t[:D // 2], dproj, side_s, name="g_in_top")
    g_top = reorder(g_top)
    g_bot, got_top = _mm_sided(h1t[D // 2:], dproj, wire(g_top), name="g_in_bot")
    g_bot = reorder(g_bot)
    gx, red0, *got_bot = _d_h1_modulate_bwd(dproj, w["in_cat_t"], dxa, x, adam, wire(g_bot))
    g["w_in"] = jnp.concatenate([g_top, g_bot], axis=0)
    got = dict(zip(_REST_A + _REST_B, got_r))
    if shards is not None:
        got["w_in"] = jnp.concatenate([got_top[0], got_bot[0]], axis=1)
        got["small"] = got_s[0]
    d_ada = jnp.concatenate([red0[1], red0[0], red1[4], red1[1], red1[0], red2[2]])
    return red2[3, 0], gx, g, d_ada, got


_BIG = ("w_in", "w_proj_a", "w_proj_b", "w_out", "ffn_w_gate", "ffn_w_up", "ffn_w_down")
_COL_SHARDED = ("w_in", "ffn_w_gate", "ffn_w_up")
_CONV = ("rg_conv_w", "dn_conv_w", "ffn_conv_w")
_REPL = ("b_ada", "rg_conv_b", "rg_w_a", "rg_b_a", "rg_w_x", "rg_b_x", "rg_lambda", "dn_a_log",
         "dn_dt_bias", "dn_norm_w", "ln1_g", "ln1_b", "ffn_conv_b", "ln2_g", "ln2_b")
_NAMES = ("w_ada", "b_ada", "w_in", "rg_conv_w", "rg_conv_b", "rg_w_a", "rg_b_a", "rg_w_x", "rg_b_x",
          "rg_lambda", "dn_conv_w", "dn_a_log", "dn_dt_bias", "dn_norm_w", "w_proj_a", "w_proj_b", "w_out",
          "ln1_g", "ln1_b", "ffn_w_gate", "ffn_w_up", "ffn_conv_w", "ffn_conv_b", "ffn_w_down", "ln2_g", "ln2_b")


def _pack(arrs, width, row_mult):
    pieces = []
    for a in arrs:
        f = a.reshape(-1)
        pieces.append(jnp.pad(f, (0, (-f.shape[0]) % (8 * width))).reshape(-1, width))
    rows = sum(p_.shape[0] for p_ in pieces)
    if rows % row_mult:
        pieces.append(jnp.zeros((row_mult - rows % row_mult, width), pieces[0].dtype))
    return jnp.concatenate(pieces, axis=0)


def _unpack(flat, shapes, width):
    out, row = [], 0
    for shp in shapes:
        n = 1
        for d_ in shp:
            n *= d_
        rows = -(-n // (8 * width)) * 8
        out.append(flat[row:row + rows].reshape(-1)[:n].reshape(shp))
        row += rows
    return out


def _stack_shards(name, full):
    if name in _COL_SHARDED or name in _CONV:
        r, ncol = full.shape
        return full.reshape(r, 4, ncol // 4).transpose(1, 0, 2)
    return full.reshape((4, full.shape[0] // 4) + full.shape[1:])


def _unstack_shards(name, st):
    if name in _COL_SHARDED or name in _CONV:
        return st.transpose(1, 0, 2).reshape(st.shape[1], 4 * st.shape[2])
    return st.reshape((4 * st.shape[1],) + st.shape[2:])


def kernel(x, c, w_ada, b_ada, w_in, rg_conv_w, rg_conv_b, rg_w_a, rg_b_a, rg_w_x, rg_b_x, rg_lambda, dn_conv_w, dn_a_log, dn_dt_bias, dn_norm_w, w_proj_a, w_proj_b, w_out, ln1_g, ln1_b, ffn_w_gate, ffn_w_up, ffn_conv_w, ffn_conv_b, ffn_w_down, ln2_g, ln2_b, loss_target, m_w_ada, m_b_ada, m_w_in, m_rg_conv_w, m_rg_conv_b, m_rg_w_a, m_rg_b_a, m_rg_w_x, m_rg_b_x, m_rg_lambda, m_dn_conv_w, m_dn_a_log, m_dn_dt_bias, m_dn_norm_w, m_w_proj_a, m_w_proj_b, m_w_out, m_ln1_g, m_ln1_b, m_ffn_w_gate, m_ffn_w_up, m_ffn_conv_w, m_ffn_conv_b, m_ffn_w_down, m_ln2_g, m_ln2_b, v_w_ada, v_b_ada, v_w_in, v_rg_conv_w, v_rg_conv_b, v_rg_w_a, v_rg_b_a, v_rg_w_x, v_rg_b_x, v_rg_lambda, v_dn_conv_w, v_dn_a_log, v_dn_dt_bias, v_dn_norm_w, v_w_proj_a, v_w_proj_b, v_w_out, v_ln1_g, v_ln1_b, v_ffn_w_gate, v_ffn_w_up, v_ffn_conv_w, v_ffn_conv_b, v_ffn_w_down, v_ln2_g, v_ln2_b):
    args = locals()
    wts = {n: args[n][0] for n in _NAMES}
    mom = {n: args["m_" + n][0] for n in _NAMES}
    var = {n: args["v_" + n][0] for n in _NAMES}
    xs, tgt = x[0], loss_target[0]
    ix, iy, ic = _coords()
    shard = 2 * ix + iy
    batch = 4 * ix + 2 * iy + ic

    c_all = _allgather8(_pad_rows(c, 8), "gather_c")[:, 0, :]
    sc16 = _pad_rows(_silu_rows(c_all), 16)
    ada_cols = _mm(sc16, w_ada[0], name="ada")[:8]
    ada_g = _allgather8(ada_cols, "gather_ada")
    ada_all = jnp.concatenate([ada_g[0], ada_g[2], ada_g[4], ada_g[6]], axis=1) + b_ada
    adam = _pad_rows(lax.dynamic_index_in_dim(ada_all, batch, 0, keepdims=False).reshape(6, D), 8)

    (w_in_all,) = _run_exchange(_gather_exchange([wts["w_in"].astype(MXU_DT)]), "gather_w_in")
    full = {"w_in": _unstack_shards("w_in", w_in_all)}
    conv_shard = _pack([wts[n] for n in _CONV], 128, 8)
    conv_all = _allgather8(conv_shard, "gather_conv")
    shapes_conv = [wts[n].shape for n in _CONV]
    per_shard = [_unpack(conv_all[2 * s], shapes_conv, 128) for s in range(4)]
    for i, n in enumerate(_CONV):
        full[n] = _unstack_shards(n, jnp.stack([per_shard[s][i] for s in range(4)]))
    for n in _REPL:
        full[n] = wts[n]

    shards = {n: wts[n].astype(MXU_DT) for n in _REST_A + _REST_B}
    loss_b, gx, g, d_ada, recv = _local_step(xs, tgt, adam, full, shards)

    parts = []
    for n in _BIG:
        r_ = recv[n]
        axis = 1 if n in _COL_SHARDED else 0
        width = wts[n].shape[axis]
        own = lax.dynamic_slice_in_dim(g[n], shard * width, width, axis=axis)
        parts.append(_sum_partials(own, r_, "sum_" + n))
    parts_sib = _swap_sibling(parts)
    out = {n: [] for n in _NAMES}
    for n, p_, q_ in zip(_BIG, parts, parts_sib):
        out[n] = list(_adamw([p_, q_], wts[n], mom[n], var[n], "adamw_" + n))

    tot = _unpack(_sum8(recv["small"], "sum_small"), [(1,)] + [full[n].shape for n in _SMALL], 128)
    gsum = dict(zip(_SMALL, tot[1:]))
    loss = tot[0][0]
    for n in _CONV:
        gsum[n] = lax.dynamic_index_in_dim(_stack_shards(n, gsum[n]), shard, 0, keepdims=False)
    d_ada_g = _allgather8(d_ada.reshape(6 * D // 128, 128), "gather_d_ada")
    gsum["b_ada"] = _sum8(d_ada_g, "sum_d_ada").reshape(6 * D)
    d_ada_all = d_ada_g.reshape(8, 6 * D)
    cols = lax.dynamic_slice_in_dim(d_ada_all, shard * (6 * D // 4), 6 * D // 4, axis=1)
    g_wada = _mm(sc16, _pad_rows(cols, 16), name="g_ada", trans_a=True)
    res = _adamw([g_wada], wts["w_ada"], mom["w_ada"], var["w_ada"], "adamw_ada")
    out["w_ada"] = list(res)
    names_s = _CONV + _REPL
    shapes_s = [wts[n].shape for n in names_s]
    pk = lambda d_: _pack([d_[n] for n in names_s], 128, 64)
    res_s = _adamw([pk(gsum)], pk(wts), pk(mom), pk(var), "adamw_small")
    for r_ in res_s:
        for n, a in zip(names_s, _unpack(r_, shapes_s, 128)):
            out[n].append(a)

    outs = [loss, gx[None]]
    for i in range(4):
        outs += [out[n][i][None] for n in _NAMES]
    return tuple(outs)
```

```python
import functools

import jax
import jax.numpy as jnp
from jax import lax
from jax.experimental import pallas as pl
from jax.experimental.pallas import tpu as pltpu

F32 = jnp.float32
BF16 = jnp.bfloat16
MXU_DT = BF16

D = 1024
D_RNN = 1280
RG_BLOCKS = 16
RG_BW = 80
RG_C = 8.0
NQ = 8
NV = 16
HD = 128
CH = 64
D_FF = 2816
LN_EPS = 1e-5
RMS_EPS = 1e-6
L2_EPS = 1e-6
ALPHA = 2.0 ** 0.25
Q_SCALE = HD ** -0.5
N_CAT = 10880
VMEM_LIMIT = 56 * 1024 * 1024
MM_VMEM_BUDGET = 36 * 1024 * 1024
COL_STRIP = 256
SEQ_GROUP = 16
MESH = pl.DeviceIdType.MESH

ADAM_LR, ADAM_B1, ADAM_B2, ADAM_EPS, ADAM_WD, ADAM_STEP = 1e-3, 0.9, 0.999, 1e-8, 0.01, 10


def _sigmoid(x):
    return 0.5 * jnp.tanh(0.5 * x) + 0.5


def _softplus(x):
    return jnp.maximum(x, 0.0) + jnp.log1p(jnp.exp(-jnp.abs(x)))


_GC = 0.7978845608028654


def _gelu(x):
    return 0.5 * x * (1.0 + jnp.tanh(_GC * (x + 0.044715 * x * x * x)))


def _gelu_and_grad(x):
    t = jnp.tanh(_GC * (x + 0.044715 * x * x * x))
    g = 0.5 * x * (1.0 + t)
    dg = 0.5 * (1.0 + t) + 0.5 * x * (1.0 - t * t) * _GC * (1.0 + 3 * 0.044715 * x * x)
    return g, dg


def _neg_expm1(y):
    series = -y * (1.0 + 0.5 * y * (1.0 + y * (1.0 / 3.0)))
    return jnp.where(y > -0.01, series, 1.0 - jnp.exp(y))


def _dot(a, b):
    return jnp.dot(a.astype(MXU_DT), b.astype(MXU_DT), preferred_element_type=F32)


def _dot_nt(a, b):
    return lax.dot_general(a.astype(MXU_DT), b.astype(MXU_DT), (((1,), (1,)), ((), ())),
                           preferred_element_type=F32)


def _dot_tn(a, b):
    return lax.dot_general(a.astype(MXU_DT), b.astype(MXU_DT), (((0,), (0,)), ((), ())),
                           preferred_element_type=F32)


def _split(a):
    hi = a.astype(BF16)
    return hi, (a - hi.astype(F32)).astype(BF16)


def _dot3(a, b, dims=(((1,), (0,)), ((), ()))):
    ah, al = _split(a)
    bh, bl = _split(b)
    d = lambda p, q: lax.dot_general(p, q, dims, preferred_element_type=F32)
    return d(ah, bh) + (d(al, bh) + d(ah, bl))


def _dot3_tn(a, b):
    return _dot3(a, b, (((0,), (0,)), ((), ())))


def _iota(shape, dim):
    return lax.broadcasted_iota(jnp.int32, shape, dim)


def _shift_down(x, before, j):
    if j == 0:
        return x
    xr = pltpu.roll(x, j, 0)
    br = pltpu.roll(before, j, 0)
    top = jnp.where(_iota(br.shape, 0) < j, br, xr[:8])
    return jnp.concatenate([top, xr[8:]], axis=0)


def _shift_up(x, after, j):
    if j == 0:
        return x
    t = x.shape[0]
    xr = pltpu.roll(x, t - j, 0)
    ar = pltpu.roll(after, 8 - j, 0)
    bot = jnp.where(_iota(ar.shape, 0) >= 8 - j, ar, xr[t - 8:])
    return jnp.concatenate([xr[:t - 8], bot], axis=0)


def _taps(x, before, k):
    return [_shift_down(x, before, k - 1 - i) for i in range(k)]


def _conv_taps(taps, w_ref):
    y = w_ref[0:1, :] * taps[0]
    for i in range(1, len(taps)):
        y = y + w_ref[i:i + 1, :] * taps[i]
    return y


def _conv_causal(x, before, w_ref, k):
    return _conv_taps(_taps(x, before, k), w_ref)


def _conv_causal_bwd(dy, after, w_ref, k, x=None, gw_ref=None):
    dx = None
    for i in range(k):
        sh = _shift_up(dy, after, k - 1 - i)
        term = w_ref[i:i + 1, :] * sh
        dx = term if dx is None else dx + term
        if x is not None:
            gw_ref[i:i + 1, :] += _rsum(x * sh)
    return dx


def _scan_fwd(a, u, carry):
    t = a.shape[0]
    pos = _iota(a.shape, 0) & 7
    for d in (1, 2, 4):
        m = pos >= d
        u = u + jnp.where(m, a * pltpu.roll(u, d, 0), 0.0)
        a = jnp.where(m, a * pltpu.roll(a, d, 0), a)
    out = []
    for g in range(t // 8):
        hg = u[8 * g:8 * g + 8] + a[8 * g:8 * g + 8] * carry
        out.append(hg)
        carry = hg[7:8]
    return jnp.concatenate(out, axis=0)


def _scan_rev(a, u, carry):
    t = a.shape[0]
    pos = _iota(a.shape, 0) & 7
    for d in (1, 2, 4):
        m = pos < 8 - d
        u = u + jnp.where(m, a * pltpu.roll(u, t - d, 0), 0.0)
        a = jnp.where(m, a * pltpu.roll(a, t - d, 0), a)
    out = [None] * (t // 8)
    for g in reversed(range(t // 8)):
        lg = u[8 * g:8 * g + 8] + a[8 * g:8 * g + 8] * carry
        out[g] = lg
        carry = lg[0:1]
    return jnp.concatenate(out, axis=0)


def _chunk_cumsum(g, axis, rev=False):
    n = g.shape[axis]
    pos = _iota(g.shape, axis) & (CH - 1)
    d = 1
    while d < CH:
        if rev:
            g = g + jnp.where(pos < CH - d, pltpu.roll(g, n - d, axis), 0.0)
        else:
            g = g + jnp.where(pos >= d, pltpu.roll(g, d, axis), 0.0)
        d *= 2
    return g


def _ln_stats(r):
    mu = jnp.mean(r, axis=-1, keepdims=True)
    xc = r - mu
    var = jnp.mean(xc * xc, axis=-1, keepdims=True)
    rstd = lax.rsqrt(var + LN_EPS)
    return xc * rstd, rstd


def _ln_bwd(dy, xhat, rstd, g):
    dxh = dy * g
    return rstd * (dxh - jnp.mean(dxh, axis=-1, keepdims=True)
                   - xhat * jnp.mean(dxh * xhat, axis=-1, keepdims=True))


def _rsum(x):
    return jnp.sum(x, axis=0, keepdims=True)


def _params(sem):
    return pltpu.CompilerParams(dimension_semantics=sem, vmem_limit_bytes=VMEM_LIMIT)


def _pick(n, cands):
    for c in cands:
        if n % c == 0:
            return c
    return n


def _rows(tm, w, col=0, nt=None):
    if nt is None:
        return pl.BlockSpec((tm, w), lambda i: (i, col))
    return pl.BlockSpec((tm, w), lambda i: (nt - 1 - i, col))


def _before(tm, w, col=0, nt=None):
    r = tm // 8
    if nt is None:
        return pl.BlockSpec((8, w), lambda i: (jnp.maximum(i * r - 1, 0), col))
    return pl.BlockSpec((8, w), lambda i: (jnp.maximum((nt - 1 - i) * r - 1, 0), col))


def _cols(tm, w):
    return pl.BlockSpec((w, tm), lambda i: (0, i))


def _whole(shape):
    return pl.BlockSpec(shape, lambda *_: (0,) * len(shape))


def _mm_plan(a, b, out_dtype):
    m, kk = a.shape
    _, n = b.shape
    tm = _pick(m, (512, 256, 128))

    def vmem_bytes(tn, tk):
        blocks = tm * tk * a.dtype.itemsize + tk * tn * b.dtype.itemsize + tm * tn * jnp.dtype(out_dtype).itemsize
        return 2 * blocks + (tm * tn * 4 if tk < kk else 0)

    cands = [t for t in (1408, 1280, 1024, 640, 512, 256, 128) if n % t == 0] or [n]
    tks = [kk] if kk <= 5632 else [t for t in (4096, 2176, 2048, 1024) if kk % t == 0]
    tn, tk = max(((next((t for t in cands if vmem_bytes(t, k_) <= MM_VMEM_BUDGET), cands[-1]), k_) for k_ in tks))
    return tm, tn, tk, kk // tk


def _mm_side(a, b, name, out_dtype, side):
    m, _ = a.shape
    _, n = b.shape
    tm, tn, tk, nk = _mm_plan(a, b, out_dtype)
    ni, nj = m // tm, n // tn
    n_in, n_out = len(side.ins), len(side.out_shapes)

    def body(*refs):
        a_ref, b_ref = refs[0], refs[1]
        s_in = refs[2:2 + n_in]
        o_ref = refs[2 + n_in]
        s_out = refs[3 + n_in:3 + n_in + n_out]
        acc = refs[3 + n_in + n_out]
        sems = refs[4 + n_in + n_out:]
        i, j, k = pl.program_id(0), pl.program_id(1), pl.program_id(2)

        @pl.when((i == 0) & (j == 0) & (k == 0))
        def _():
            side.start(s_in, s_out, sems)

        @pl.when(k == 0)
        def _():
            acc[...] = jnp.zeros_like(acc)
        acc[...] += _dot(a_ref[...], b_ref[...])

        @pl.when(k == nk - 1)
        def _():
            o_ref[...] = acc[...].astype(out_dtype)

        @pl.when((i == ni - 1) & (j == nj - 1) & (k == nk - 1))
        def _():
            side.finish(s_in, s_out, sems)

    return pl.pallas_call(
        body, name=name, grid=(ni, nj, nk),
        in_specs=[pl.BlockSpec((tm, tk), lambda i, j, k: (i, k)),
                  pl.BlockSpec((tk, tn), lambda i, j, k: (k, j))] + [_HBM] * n_in,
        out_specs=[pl.BlockSpec((tm, tn), lambda i, j, k: (i, j))] + [_HBM] * n_out,
        out_shape=[jax.ShapeDtypeStruct((m, n), out_dtype)] + list(side.out_shapes),
        scratch_shapes=[pltpu.VMEM((tm, tn), F32)] + list(side.sems),
        compiler_params=_params(("arbitrary", "arbitrary", "arbitrary")),
    )(a, b, *side.ins)


def _mm(a, b, *, name, trans_a=False, out_dtype=F32, side=None):
    if trans_a:
        return _mm(a.T, b, name=name, out_dtype=out_dtype, side=side)
    if side is not None:
        return _mm_side(a, b, name, out_dtype, side)
    m, kk = a.shape
    _, n = b.shape
    tm, tn, tk, nk = _mm_plan(a, b, out_dtype)

    if nk == 1:
        def body(a_ref, b_ref, o_ref):
            o_ref[...] = _dot(a_ref[...], b_ref[...]).astype(out_dtype)
        scratch = []
    else:
        def body(a_ref, b_ref, o_ref, acc):
            k = pl.program_id(2)

            @pl.when(k == 0)
            def _():
                acc[...] = jnp.zeros_like(acc)
            acc[...] += _dot(a_ref[...], b_ref[...])

            @pl.when(k == nk - 1)
            def _():
                o_ref[...] = acc[...].astype(out_dtype)
        scratch = [pltpu.VMEM((tm, tn), F32)]

    return pl.pallas_call(
        body, name=name, grid=(m // tm, n // tn, nk),
        in_specs=[pl.BlockSpec((tm, tk), lambda i, j, k: (i, k)),
                  pl.BlockSpec((tk, tn), lambda i, j, k: (k, j))],
        out_specs=pl.BlockSpec((tm, tn), lambda i, j, k: (i, j)),
        out_shape=jax.ShapeDtypeStruct((m, n), out_dtype),
        scratch_shapes=scratch,
        compiler_params=_params(("parallel", "parallel", "arbitrary")),
    )(a, b)


def _modulate(x, adam):
    s = x.shape[0]
    tm = _pick(s, (512, 256, 128))

    def body(x_ref, ada_ref, o_ref, ot_ref):
        h = x_ref[...] * (1.0 + ada_ref[1:2, :]) + ada_ref[0:1, :]
        o_ref[...] = h.astype(MXU_DT)
        ot_ref[...] = h.T.astype(MXU_DT)

    return pl.pallas_call(
        body, name="modulate1", grid=(s // tm,),
        in_specs=[_rows(tm, D), _whole((8, D))], out_specs=[_rows(tm, D), _cols(tm, D)],
        out_shape=[jax.ShapeDtypeStruct((s, D), MXU_DT), jax.ShapeDtypeStruct((D, s), MXU_DT)],
        compiler_params=_params(("parallel",)),
    )(x, adam)


def _rg_gates(xc, wa_ref, wx_ref, vec_ref):
    xb = xc.astype(MXU_DT)
    r = _sigmoid(jnp.dot(xb, wa_ref[...], preferred_element_type=F32) + vec_ref[1:2, :])
    ig = _sigmoid(jnp.dot(xb, wx_ref[...], preferred_element_type=F32) + vec_ref[2:3, :])
    sp = _softplus(-vec_ref[3:4, :])
    la = -RG_C * r * sp
    a = jnp.exp(la)
    n1 = _neg_expm1(2.0 * la)
    rmult = lax.rsqrt(jnp.maximum(n1, 1e-20))
    return r, ig, a, n1 * rmult, sp, rmult


def _rg_fwd(px, cw, vec, wa, wx):
    s = px.shape[0]
    tm = _pick(s, (256, 128))
    w = D_RNN

    def body(xr_ref, gr_ref, cw_ref, vec_ref, wa_ref, wx_ref, h_ref, xc_ref, rec_ref, rect_ref, prev_x, prev_h):
        @pl.when(pl.program_id(0) == 0)
        def _():
            prev_x[...] = jnp.zeros_like(prev_x)
            prev_h[...] = jnp.zeros_like(prev_h)
        x = xr_ref[...]
        xc = _conv_causal(x, prev_x[...], cw_ref, 4) + vec_ref[0:1, :]
        prev_x[...] = x[tm - 8:, :]
        xc_ref[...] = xc
        _, ig, a, mult, _, _ = _rg_gates(xc, wa_ref, wx_ref, vec_ref)
        h = _scan_fwd(a, mult * ig * xc, prev_h[7:8, :])
        prev_h[...] = h[tm - 8:, :]
        h_ref[...] = h
        rec = h * _gelu(gr_ref[...])
        rec_ref[...] = rec.astype(MXU_DT)
        rect_ref[...] = rec.T.astype(MXU_DT)

    return pl.pallas_call(
        body, name="rg_fwd", grid=(s // tm,),
        in_specs=[_rows(tm, w, 0), _rows(tm, w, 1), _whole((8, w)), _whole((8, w)),
                  _whole((w, w)), _whole((w, w))],
        out_specs=[_rows(tm, w), _rows(tm, w), _rows(tm, w), _cols(tm, w)],
        out_shape=[jax.ShapeDtypeStruct((s, w), F32), jax.ShapeDtypeStruct((s, w), F32),
                   jax.ShapeDtypeStruct((s, w), MXU_DT), jax.ShapeDtypeStruct((w, s), MXU_DT)],
        scratch_shapes=[pltpu.VMEM((8, w), F32), pltpu.VMEM((8, w), F32)],
        compiler_params=_params(("arbitrary",)),
    )(px, px, cw, vec, wa, wx)


def _dn_scalars(ab, arow, drow):
    lane = _iota(ab.shape, 1)
    g = jnp.where(lane < NV, -jnp.exp(arow) * _softplus(ab + drow), 0.0)
    beta = _sigmoid(ab)
    return lane, g, beta


def _l2n_heads(c, out_ref, off, scale):
    for hh in range(NQ):
        x = c[:, off + hh * HD: off + (hh + 1) * HD]
        r = lax.rsqrt(jnp.sum(x * x, axis=-1, keepdims=True) + L2_EPS)
        out_ref[:, hh * HD:(hh + 1) * HD] = x * (r * scale)


def _dn_prep(pqkv, pab, cw, arow, drow):
    s = pqkv.shape[0]
    tm = _pick(s, (256, 128))
    wq = NQ * HD

    def body(x_ref, ab_ref, cw_ref, a_ref, d_ref, q_ref, k_ref, v_ref, rt_ref, cp_ref, prev_x):
        @pl.when(pl.program_id(0) == 0)
        def _():
            prev_x[...] = jnp.zeros_like(prev_x)
        x = x_ref[...]
        cp = _conv_causal(x, prev_x[...], cw_ref, 4)
        prev_x[...] = x[tm - 8:, :]
        cp_ref[...] = cp
        c = cp * _sigmoid(cp)
        _l2n_heads(c, q_ref, 0, Q_SCALE)
        _l2n_heads(c, k_ref, wq, 1.0)
        v_ref[...] = c[:, 2 * wq:]
        lane, g, beta = _dn_scalars(ab_ref[...], a_ref[...], d_ref[...])
        gc = _chunk_cumsum(g, 0)
        gl = gc + _chunk_cumsum(g, 0, rev=True) - g
        pack = jnp.where(lane < NV, gc, jnp.where(lane < 2 * NV, beta,
                         jnp.where(lane < 3 * NV, pltpu.roll(gl, 2 * NV, 1), 0.0)))
        rt_ref[...] = pack.T[0:3 * NV, :]

    return pl.pallas_call(
        body, name="dn_prep", grid=(s // tm,),
        in_specs=[_rows(tm, 4 * wq), _rows(tm, 128), _whole((8, 4 * wq)), _whole((1, 128)), _whole((1, 128))],
        out_specs=[_rows(tm, wq), _rows(tm, wq), _rows(tm, 2 * wq),
                   pl.BlockSpec((3 * NV, tm), lambda i: (0, i)), _rows(tm, 4 * wq)],
        out_shape=[jax.ShapeDtypeStruct((s, wq), F32), jax.ShapeDtypeStruct((s, wq), F32),
                   jax.ShapeDtypeStruct((s, 2 * wq), F32), jax.ShapeDtypeStruct((3 * NV, s), F32),
                   jax.ShapeDtypeStruct((s, 4 * wq), F32)],
        scratch_shapes=[pltpu.VMEM((8, 4 * wq), F32)],
        compiler_params=_params(("arbitrary",)),
    )(pqkv, pab, cw, arow, drow)


def _pair_masks():
    i = _iota((2 * CH, 2 * CH), 0)
    j = _iota((2 * CH, 2 * CH), 1)
    same = (i >> 6) == (j >> 6)
    return same & (i >= j), same & (i > j)


def _head_cols(rt_ref, h):
    shp = (2 * CH, 2 * CH)
    g_r = jnp.broadcast_to(rt_ref[pl.ds(h, 1), :], shp)
    b_r = jnp.broadcast_to(rt_ref[pl.ds(NV + h, 1), :], shp)
    l_r = jnp.broadcast_to(rt_ref[pl.ds(2 * NV + h, 1), :], shp)
    return g_r, g_r.T, b_r.T, l_r, l_r.T


def _inv_unit_lower_many(a_list):
    n = a_list[0].shape[0]
    eye = (_iota((n, n), 0) == _iota((n, n), 1)).astype(F32)
    bs = [-a for a in a_list]
    xs = [eye + b for b in bs]
    for _ in range(5):
        bs = [_dot(b, b) for b in bs]
        xs = [x + _dot(x, b) for x, b in zip(xs, bs)]
    rs = [(eye - x) - _dot3(a, x) for a, x in zip(a_list, xs)]
    return [x + _dot(x, r) for x, r in zip(xs, rs)]


def _gam_rows(l_r):
    lrow = l_r[0:1, :]
    lane = _iota(lrow.shape, 1)
    other = pltpu.roll(lrow, CH, 1)
    return jnp.exp(jnp.where(lane < CH, lrow, other)), jnp.exp(jnp.where(lane >= CH, lrow, other))


def _dn_intra(q, k, v, rt):
    s = q.shape[0]
    nb = s // (2 * CH)
    qps = 8
    blk = pl.BlockSpec((2 * CH, qps * HD), lambda i, h: (i, h))
    blk2 = pl.BlockSpec((2 * CH, 2 * qps * HD), lambda i, h: (i, h))

    def body(q_ref, k_ref, v_ref, rt_ref, u_ref, w_ref, qd_ref, kd_ref, p_ref, ti_ref, gam_ref):
        hstep = pl.program_id(1)
        mc, ms = _pair_masks()
        gam_ref[...] = jnp.zeros_like(gam_ref)
        heads = []
        for qh in range(qps):
            qq = q_ref[:, qh * HD:(qh + 1) * HD]
            kk_ = k_ref[:, qh * HD:(qh + 1) * HD]
            kk = _dot_nt(kk_, kk_)
            qk = _dot_nt(qq, kk_)
            for j in range(2):
                idx = 2 * qh + j
                cs = slice(idx * HD, (idx + 1) * HD)
                g_r, g_c, b_c, l_r, l_c = _head_cols(rt_ref, 2 * qps * hstep + idx)
                dec = jnp.where(mc, jnp.exp(jnp.where(mc, g_c - g_r, 0.0)), 0.0)
                eg = jnp.exp(g_c)
                p_ref[:, cs] = jnp.where(mc, qk * dec, 0.0)
                qd_ref[:, cs] = eg * qq
                kd_ref[:, cs] = jnp.exp(l_c - g_c) * kk_
                ga, gb = _gam_rows(l_r)
                gam_ref[0, qh, 2 * j:2 * j + 1, :] = ga
                gam_ref[0, qh, 2 * j + 1:2 * j + 2, :] = gb
                rhs = jnp.concatenate([b_c * v_ref[:, cs], b_c * eg * kk_], axis=1)
                heads.append((cs, jnp.where(ms, b_c * kk * dec, 0.0), rhs))
        tinvs = _inv_unit_lower_many([a for _, a, _ in heads])
        uws = [_dot3(t, rhs) for t, (_, _, rhs) in zip(tinvs, heads)]
        for t, uw, (cs, _, _) in zip(tinvs, uws, heads):
            ti_ref[:, cs] = t
            u_ref[:, cs] = uw[:, :HD]
            w_ref[:, cs] = uw[:, HD:]

    big = jax.ShapeDtypeStruct((s, NV * HD), F32)
    return pl.pallas_call(
        body, name="dn_intra", grid=(nb, NQ // qps),
        in_specs=[blk, blk, blk2, pl.BlockSpec((3 * NV, 2 * CH), lambda i, h: (0, i))],
        out_specs=[blk2] * 6 + [pl.BlockSpec((1, qps, 8, 128), lambda i, h: (i, h, 0, 0))],
        out_shape=[big] * 6 + [jax.ShapeDtypeStruct((nb, NQ, 8, 128), F32)],
        compiler_params=_params(("parallel", "parallel")),
    )(q, k, v, rt)


def _dn_seq(u, w, qd, kd, p, gam, pz, nw):
    s = u.shape[0]
    nb = s // (2 * CH)
    wide = pl.BlockSpec((2 * CH, NV * HD), lambda i: (i, 0))

    def body(u_ref, w_ref, qd_ref, kd_ref, p_ref, gam_ref, z_ref, nw_ref, o_ref, vn_ref, ss_ref, y_ref, yt_ref, st):
        @pl.when(pl.program_id(0) == 0)
        def _():
            st[...] = jnp.zeros_like(st)
        ra, rb = slice(0, CH), slice(CH, 2 * CH)
        for g0 in range(0, NV, SEQ_GROUP):
            hs = list(range(g0, g0 + SEQ_GROUP))
            cs = [slice(h * HD, (h + 1) * HD) for h in hs]
            ga = [gam_ref[0, h // 2, 2 * (h % 2):2 * (h % 2) + 1, :] for h in hs]
            gb = [gam_ref[0, h // 2, 2 * (h % 2) + 1:2 * (h % 2) + 2, :] for h in hs]
            s0 = [st[h] for h in hs]
            vna = [u_ref[ra, c] - _dot(w_ref[ra, c], s) for c, s in zip(cs, s0)]
            s1 = [g * s + _dot_tn(kd_ref[ra, c], v) for g, s, c, v in zip(ga, s0, cs, vna)]
            vnb = [u_ref[rb, c] - _dot(w_ref[rb, c], s) for c, s in zip(cs, s1)]
            s2 = [g * s + _dot_tn(kd_ref[rb, c], v) for g, s, c, v in zip(gb, s1, cs, vnb)]
            for h, s in zip(hs, s2):
                st[h] = s
            oa = [_dot(qd_ref[ra, c], s) for c, s in zip(cs, s0)]
            ob = [_dot(qd_ref[rb, c], s) for c, s in zip(cs, s1)]
            for i_, h in enumerate(hs):
                vn = jnp.concatenate([vna[i_], vnb[i_]], axis=0)
                oh = jnp.concatenate([oa[i_], ob[i_]], axis=0) + _dot(p_ref[:, cs[i_]], vn)
                o_ref[:, cs[i_]] = oh
                vn_ref[:, cs[i_]] = vn
                ss_ref[h, 0:HD, :] = s0[i_]
                ss_ref[h, HD:2 * HD, :] = s1[i_]
                z = z_ref[:, cs[i_]]
                rs = lax.rsqrt(jnp.mean(oh * oh, axis=-1, keepdims=True) + RMS_EPS)
                y = oh * rs * nw_ref[...] * (z * _sigmoid(z))
                y_ref[:, cs[i_]] = y.astype(MXU_DT)
                yt_ref[cs[i_], :] = y.T.astype(MXU_DT)

    big = jax.ShapeDtypeStruct((s, NV * HD), F32)
    return pl.pallas_call(
        body, name="dn_seq", grid=(nb,),
        in_specs=[wide] * 5 + [pl.BlockSpec((1, NQ, 8, 128), lambda i: (i, 0, 0, 0)), wide, _whole((1, HD))],
        out_specs=[wide, wide, pl.BlockSpec((NV, 2 * HD, HD), lambda i: (0, i, 0)), wide, _cols(2 * CH, NV * HD)],
        out_shape=[big, big, jax.ShapeDtypeStruct((NV, 2 * s, HD), F32),
                   jax.ShapeDtypeStruct((s, NV * HD), MXU_DT), jax.ShapeDtypeStruct((NV * HD, s), MXU_DT)],
        scratch_shapes=[pltpu.VMEM((NV, HD, HD), F32)],
        compiler_params=_params(("arbitrary",)),
    )(u, w, qd, kd, p, gam, pz, nw)


def _ln1(x, mix, adam, lng, lnb):
    s = x.shape[0]
    tm = _pick(s, (512, 256, 128))

    def body(x_ref, m_ref, ada_ref, g_ref, b_ref, r_ref, x2_ref, h2_ref, h2t_ref):
        r = ALPHA * x_ref[...] + (1.0 + ada_ref[2:3, :]) * m_ref[...]
        xhat, _ = _ln_stats(r)
        x2 = xhat * g_ref[...] + b_ref[...]
        r_ref[...] = r
        x2_ref[...] = x2
        h2 = x2 * (1.0 + ada_ref[4:5, :]) + ada_ref[3:4, :]
        h2_ref[...] = h2.astype(MXU_DT)
        h2t_ref[...] = h2.T.astype(MXU_DT)

    return pl.pallas_call(
        body, name="ln1", grid=(s // tm,),
        in_specs=[_rows(tm, D), _rows(tm, D), _whole((8, D)), _whole((1, D)), _whole((1, D))],
        out_specs=[_rows(tm, D)] * 3 + [_cols(tm, D)],
        out_shape=[jax.ShapeDtypeStruct((s, D), F32), jax.ShapeDtypeStruct((s, D), F32),
                   jax.ShapeDtypeStruct((s, D), MXU_DT), jax.ShapeDtypeStruct((D, s), MXU_DT)],
        compiler_params=_params(("parallel",)),
    )(x, mix, adam, lng, lnb)


def _ffn_act(gu, cw, cb):
    s = gu.shape[0]
    tm = _pick(s, (256, 128))
    w = D_FF

    def body(g_ref, u_ref, cw_ref, cb_ref, o_ref, ot_ref, gc_ref, prev):
        @pl.when(pl.program_id(0) == 0)
        def _():
            prev[...] = jnp.zeros_like(prev)
        for c0 in range(0, w, COL_STRIP):
            cs = slice(c0, c0 + COL_STRIP)
            g = g_ref[:, cs]
            gc = _conv_causal(g, prev[:, cs], cw_ref.at[:, cs], 3) + cb_ref[:, cs]
            prev[:, cs] = g[tm - 8:, :]
            gc_ref[:, cs] = gc
            act = _gelu(gc) * u_ref[:, cs]
            o_ref[:, cs] = act.astype(MXU_DT)
            ot_ref[cs, :] = act.T.astype(MXU_DT)

    return pl.pallas_call(
        body, name="ffn_act", grid=(s // tm,),
        in_specs=[_rows(tm, w, 0), _rows(tm, w, 1), _whole((8, w)), _whole((1, w))],
        out_specs=[_rows(tm, w), _cols(tm, w), _rows(tm, w)],
        out_shape=[jax.ShapeDtypeStruct((s, w), MXU_DT), jax.ShapeDtypeStruct((w, s), MXU_DT),
                   jax.ShapeDtypeStruct((s, w), F32)],
        scratch_shapes=[pltpu.VMEM((8, w), F32)],
        compiler_params=_params(("arbitrary",)),
    )(gu, gu, cw, cb)


def _ffn_bwd(dact, gu, gc, cw):
    s = dact.shape[0]
    tm = _pick(s, (256, 128))
    nt = s // tm
    w = D_FF

    def body(da_ref, g_ref, u_ref, gc_ref, cw_ref, o_ref, gcw_ref, gcb_ref, nxt):
        @pl.when(pl.program_id(0) == 0)
        def _():
            nxt[...] = jnp.zeros_like(nxt)
            gcw_ref[...] = jnp.zeros_like(gcw_ref)
            gcb_ref[...] = jnp.zeros_like(gcb_ref)
        for c0 in range(0, w, COL_STRIP):
            cs = slice(c0, c0 + COL_STRIP)
            gel, dgel = _gelu_and_grad(gc_ref[:, cs])
            da = da_ref[:, cs]
            dgc = da * u_ref[:, cs] * dgel
            o_ref[:, w + c0:w + c0 + COL_STRIP] = (da * gel).astype(MXU_DT)
            o_ref[:, cs] = _conv_causal_bwd(dgc, nxt[:, cs], cw_ref.at[:, cs], 3, g_ref[:, cs],
                                            gcw_ref.at[:, cs]).astype(MXU_DT)
            nxt[:, cs] = dgc[:8, :]
            gcb_ref[:, cs] += _rsum(dgc)

    return pl.pallas_call(
        body, name="ffn_bwd", grid=(nt,),
        in_specs=[_rows(tm, w, 0, nt), _rows(tm, w, 0, nt), _rows(tm, w, 1, nt), _rows(tm, w, 0, nt),
                  _whole((8, w))],
        out_specs=[_rows(tm, 2 * w, 0, nt), _whole((8, w)), _whole((1, w))],
        out_shape=[jax.ShapeDtypeStruct((s, 2 * w), MXU_DT), jax.ShapeDtypeStruct((8, w), F32),
                   jax.ShapeDtypeStruct((1, w), F32)],
        scratch_shapes=[pltpu.VMEM((8, w), F32)],
        compiler_params=_params(("arbitrary",)),
    )(dact, gu, gu, gc, cw)


def _dn_seq_bwd(do, qd, kd, p, w, vn, ssave, gam):
    s = do.shape[0]
    nb = s // (2 * CH)
    wide = pl.BlockSpec((2 * CH, NV * HD), lambda i: (nb - 1 - i, 0))
    gspec = pl.BlockSpec((1, NQ, 8, 128), lambda i: (nb - 1 - i, 0, 0, 0))

    def body(do_ref, qd_ref, kd_ref, p_ref, w_ref, vn_ref, ss_ref, gam_ref, dvn_ref, dkd_ref, dgam_ref, dst):
        @pl.when(pl.program_id(0) == 0)
        def _():
            dst[...] = jnp.zeros_like(dst)
        dgam_ref[...] = jnp.zeros_like(dgam_ref)
        ra, rb = slice(0, CH), slice(CH, 2 * CH)
        tot = lambda t: jnp.sum(jnp.sum(t, axis=1, keepdims=True), axis=0, keepdims=True)
        for g0 in range(0, NV, SEQ_GROUP):
            hs = list(range(g0, g0 + SEQ_GROUP))
            cs = [slice(h * HD, (h + 1) * HD) for h in hs]
            ga = [gam_ref[0, h // 2, 2 * (h % 2):2 * (h % 2) + 1, :] for h in hs]
            gb = [gam_ref[0, h // 2, 2 * (h % 2) + 1:2 * (h % 2) + 2, :] for h in hs]
            ds2 = [dst[h] for h in hs]
            pdo = [_dot_tn(p_ref[:, c], do_ref[:, c]) for c in cs]
            qdo_b = [_dot_tn(qd_ref[rb, c], do_ref[rb, c]) for c in cs]
            qdo_a = [_dot_tn(qd_ref[ra, c], do_ref[ra, c]) for c in cs]
            dvb = [p_[rb] + _dot(kd_ref[rb, c], d_) for p_, c, d_ in zip(pdo, cs, ds2)]
            ds1 = [g * d_ + q_ - _dot_tn(w_ref[rb, c], v_)
                   for g, d_, q_, c, v_ in zip(gb, ds2, qdo_b, cs, dvb)]
            dva = [p_[ra] + _dot(kd_ref[ra, c], d_) for p_, c, d_ in zip(pdo, cs, ds1)]
            ds0 = [g * d_ + q_ - _dot_tn(w_ref[ra, c], v_)
                   for g, d_, q_, c, v_ in zip(ga, ds1, qdo_a, cs, dva)]
            for h, d_ in zip(hs, ds0):
                dst[h] = d_
            for i_, h in enumerate(hs):
                c = cs[i_]
                row = 2 * (h % 2)
                dkd_ref[rb, c] = _dot_nt(vn_ref[rb, c], ds2[i_])
                dkd_ref[ra, c] = _dot_nt(vn_ref[ra, c], ds1[i_])
                dvn_ref[ra, c] = dva[i_]
                dvn_ref[rb, c] = dvb[i_]
                dgam_ref[0, h // 2, row:row + 1, :] = jnp.broadcast_to(tot(ds1[i_] * ss_ref[h, 0:HD, :]), (1, 128))
                dgam_ref[0, h // 2, row + 1:row + 2, :] = jnp.broadcast_to(
                    tot(ds2[i_] * ss_ref[h, HD:2 * HD, :]), (1, 128))

    big = jax.ShapeDtypeStruct((s, NV * HD), F32)
    return pl.pallas_call(
        body, name="dn_seq_bwd", grid=(nb,),
        in_specs=[wide] * 6 + [pl.BlockSpec((NV, 2 * HD, HD), lambda i: (0, nb - 1 - i, 0)), gspec],
        out_specs=[wide, wide, gspec],
        out_shape=[big, big, jax.ShapeDtypeStruct((nb, NQ, 8, 128), F32)],
        scratch_shapes=[pltpu.VMEM((NV, HD, HD), F32)],
        compiler_params=_params(("arbitrary",)),
    )(do, qd, kd, p, w, vn, ssave, gam)


def _dn_intra_bwd(q, k, v, rt, do, dvn, dkd, gam, dgam, ssave, tinv, u, w, vn):
    s = q.shape[0]
    nb = s // (2 * CH)
    qps = 8
    nh = 2 * qps
    blk = pl.BlockSpec((2 * CH, qps * HD), lambda i, h: (i, h))
    blk2 = pl.BlockSpec((2 * CH, nh * HD), lambda i, h: (i, h))
    gspec = pl.BlockSpec((1, qps, 8, 128), lambda i, h: (i, h, 0, 0))
    rspec = pl.BlockSpec((3 * NV, 2 * CH), lambda i, h: (0, i))

    def body(q_ref, k_ref, v_ref, rt_ref, do_ref, dvn_ref, dkd_ref, gam_ref, dgam_ref, ss_ref,
             ti_ref, u_ref, w_ref, vn_ref, dq_ref, dk_ref, dv_ref, drt_ref, acc):
        hstep = pl.program_id(1)

        @pl.when(hstep == 0)
        def _():
            acc[...] = jnp.zeros_like(acc)
        mc, ms = _pair_masks()
        ra, rb = slice(0, CH), slice(CH, 2 * CH)
        lane = _iota((1, 2 * CH), 1)
        hs = list(range(nh))
        qh = [h // 2 for h in hs]
        cs = [slice(h * HD, (h + 1) * HD) for h in hs]
        qq_ = [q_ref[:, t * HD:(t + 1) * HD] for t in range(qps)]
        kk_ = [k_ref[:, t * HD:(t + 1) * HD] for t in range(qps)]
        kk = [_dot_nt(k_, k_) for k_ in kk_]
        qk = [_dot_nt(q_, k_) for q_, k_ in zip(qq_, kk_)]
        cols = [_head_cols(rt_ref, nh * hstep + h) for h in hs]
        b_c = [c_[2] for c_ in cols]
        dec = [jnp.where(mc, jnp.exp(jnp.where(mc, c_[1] - c_[0], 0.0)), 0.0) for c_ in cols]
        eg = [jnp.exp(c_[1]) for c_ in cols]
        egl = [jnp.exp(c_[4] - c_[1]) for c_ in cols]
        dob = [do_ref[:, c] for c in cs]
        dvb = [dvn_ref[:, c] for c in cs]
        dqd = [jnp.concatenate([_dot_nt(d_[ra], ss_ref[h, 0:HD, :]), _dot_nt(d_[rb], ss_ref[h, HD:2 * HD, :])], axis=0)
               for h, d_ in zip(hs, dob)]
        dw = [-jnp.concatenate([_dot_nt(d_[ra], ss_ref[h, 0:HD, :]), _dot_nt(d_[rb], ss_ref[h, HD:2 * HD, :])], axis=0)
              for h, d_ in zip(hs, dvb)]
        dp = [jnp.where(mc, _dot_nt(d_, vn_ref[:, c]), 0.0) for d_, c in zip(dob, cs)]
        dbuw = [_dot3_tn(ti_ref[:, c], jnp.concatenate([d_, w_], axis=1)) for c, d_, w_ in zip(cs, dvb, dw)]
        dbu = [t[:, :HD] for t in dbuw]
        dbw = [t[:, HD:] for t in dbuw]
        da = [jnp.where(ms, -(_dot_nt(bu, u_ref[:, c]) + _dot_nt(bw, w_ref[:, c])), 0.0)
              for bu, bw, c in zip(dbu, dbw, cs)]
        dm = [a_ * d_ for a_, d_ in zip(da, dec)]
        dn_ = [p_ * d_ for p_, d_ in zip(dp, dec)]
        dbk = [_dot(m_, kk_[t]) for m_, t in zip(dm, qh)]
        dqs = [_dot(n_, kk_[t]) + e_ * q_ for n_, t, e_, q_ in zip(dn_, qh, eg, dqd)]
        dks = [_dot_tn(m_, b_ * kk_[t]) + _dot_tn(n_, qq_[t]) + el * dkd_ref[:, c] + b_ * (e_ * bw + bk)
               for m_, b_, t, n_, el, c, e_, bw, bk in zip(dm, b_c, qh, dn_, egl, cs, eg, dbw, dbk)]
        for t in range(qps):
            dq_ref[:, t * HD:(t + 1) * HD] = dqs[2 * t] + dqs[2 * t + 1]
            dk_ref[:, t * HD:(t + 1) * HD] = dks[2 * t] + dks[2 * t + 1]
        for h in hs:
            c, t, j = cs[h], qh[h], h % 2
            dv_ref[:, c] = b_c[h] * dbu[h]
            e = da[h] * (b_c[h] * kk[t] * dec[h]) + dp[h] * (qk[t] * dec[h])
            x = dkd_ref[:, c] * (egl[h] * kk_[t])
            egk = eg[h] * kk_[t]
            z = e + dqd[h] * (eg[h] * qq_[t]) - x + dbw[h] * (b_c[h] * egk)
            zb = dbw[h] * egk + dbu[h] * v_ref[:, c] + dbk[h] * kk_[t]
            sa = jnp.sum(jnp.sum(x[ra], axis=1, keepdims=True), axis=0, keepdims=True)
            sb = jnp.sum(jnp.sum(x[rb], axis=1, keepdims=True), axis=0, keepdims=True)
            la = sa + dgam_ref[0, t, 2 * j:2 * j + 1, :] * gam_ref[0, t, 2 * j:2 * j + 1, :]
            lb = sb + dgam_ref[0, t, 2 * j + 1:2 * j + 2, :] * gam_ref[0, t, 2 * j + 1:2 * j + 2, :]
            hg = nh * hstep + h
            acc[pl.ds(hg, 1), :] = _rsum(z.T - e)
            acc[pl.ds(NV + hg, 1), :] = _rsum(zb.T)
            acc[pl.ds(2 * NV + hg, 1), :] = jnp.where(lane < CH, la, lb)

        @pl.when(hstep == NQ // qps - 1)
        def _():
            drt_ref[...] = acc[...]

    return pl.pallas_call(
        body, name="dn_intra_bwd", grid=(nb, NQ // qps),
        in_specs=[blk, blk, blk2, rspec, blk2, blk2, blk2, gspec, gspec,
                  pl.BlockSpec((nh, 2 * HD, HD), lambda i, h: (h, i, 0)), blk2, blk2, blk2, blk2],
        out_specs=[blk, blk, blk2, rspec],
        out_shape=[jax.ShapeDtypeStruct((s, NQ * HD), F32), jax.ShapeDtypeStruct((s, NQ * HD), F32),
                   jax.ShapeDtypeStruct((s, NV * HD), F32), jax.ShapeDtypeStruct((3 * NV, s), F32)],
        scratch_shapes=[pltpu.VMEM((3 * NV, 2 * CH), F32)],
        compiler_params=_params(("parallel", "arbitrary")),
    )(q, k, v, rt, do, dvn, dkd, gam, dgam, ssave, tinv, u, w, vn)


def _l2n_heads_bwd(c, d_ref, dc_ref, off, scale):
    for hh in range(NQ):
        cs = slice(off + hh * HD, off + (hh + 1) * HD)
        x = c[:, cs]
        dy = d_ref[:, hh * HD:(hh + 1) * HD]
        r = lax.rsqrt(jnp.sum(x * x, axis=-1, keepdims=True) + L2_EPS)
        dc_ref[:, cs] = (scale * r) * (dy - x * (r * r) * jnp.sum(dy * x, axis=-1, keepdims=True))


def _dn_prep_bwd(dq, dk, dv, drt, pqkv, cpre, pab, cw, acol, dcol, dproj):
    s = pqkv.shape[0]
    tm = 128
    nt = s // tm
    wq = NQ * HD

    def body(dq_ref, dk_ref, dv_ref, drt_ref, x_ref, cp_ref, ab_ref, cw_ref, ac_ref,
             dc_ref, _, dx_ref, dab_ref, gcw_ref, gsc_ref, dcs, nxt):
        @pl.when(pl.program_id(0) == 0)
        def _():
            nxt[...] = jnp.zeros_like(nxt)
            gcw_ref[...] = jnp.zeros_like(gcw_ref)
            gsc_ref[...] = jnp.zeros_like(gsc_ref)
        cp = cp_ref[...]
        sg = _sigmoid(cp)
        c = cp * sg
        _l2n_heads_bwd(c, dq_ref, dcs, 0, Q_SCALE)
        _l2n_heads_bwd(c, dk_ref, dcs, wq, 1.0)
        dcs[:, 2 * wq:] = dv_ref[...]
        dcp = dcs[...] * (sg * (1.0 + cp * (1.0 - sg)))
        dx_ref[...] = _conv_causal_bwd(dcp, nxt[...], cw_ref, 4, x_ref[...], gcw_ref).astype(MXU_DT)
        nxt[...] = dcp[:8, :]
        lane = _iota((NV, tm), 1)
        dgt = drt_ref[0:NV, :] + jnp.where((lane & (CH - 1)) == CH - 1, drt_ref[2 * NV:3 * NV, :], 0.0)
        dg = _chunk_cumsum(dgt, 1, rev=True)
        abt = ab_ref[...].T
        zt = abt[0:NV, :] + dc_ref[...]
        gt = -jnp.exp(ac_ref[...]) * _softplus(zt)
        dat = dg * (-jnp.exp(ac_ref[...])) * _sigmoid(zt)
        bt = _sigmoid(abt[NV:2 * NV, :])
        dbt = drt_ref[NV:2 * NV, :] * bt * (1.0 - bt)
        full = jnp.concatenate([dat, dbt, jnp.zeros((128 - 2 * NV, tm), F32)], axis=0)
        dab_ref[...] = full.T.astype(MXU_DT)
        l2 = _iota((NV, 128), 1)
        gsc_ref[...] += jnp.where(l2 == 0, jnp.sum(dg * gt, axis=1, keepdims=True),
                                  jnp.where(l2 == 1, jnp.sum(dat, axis=1, keepdims=True), 0.0))

    return pl.pallas_call(
        body, name="dn_prep_bwd", grid=(nt,),
        in_specs=[_rows(tm, wq, 0, nt), _rows(tm, wq, 0, nt), _rows(tm, 2 * wq, 0, nt),
                  pl.BlockSpec((3 * NV, tm), lambda i: (0, nt - 1 - i)),
                  _rows(tm, 4 * wq, 0, nt), _rows(tm, 4 * wq, 0, nt), _rows(tm, 128, 0, nt),
                  _whole((8, 4 * wq)), _whole((NV, 1)), _whole((NV, 1)), _HBM],
        out_specs=[_rows(tm, 4 * wq, 0, nt), _rows(tm, 128, 0, nt), _whole((8, 4 * wq)), _whole((NV, 128))],
        out_shape=[jax.ShapeDtypeStruct(dproj.shape, MXU_DT), jax.ShapeDtypeStruct((s, 128), MXU_DT),
                   jax.ShapeDtypeStruct((8, 4 * wq), F32), jax.ShapeDtypeStruct((NV, 128), F32)],
        scratch_shapes=[pltpu.VMEM((tm, 4 * wq), F32), pltpu.VMEM((8, 4 * wq), F32)],
        compiler_params=_params(("arbitrary",)), input_output_aliases={10: 0},
    )(dq, dk, dv, drt, pqkv, cpre, pab, cw, acol, dcol, dproj)


def _rg_bwd(drec, px, xcs, h, cw, vec, wa, wx, wat, wxt, side=None):
    s = px.shape[0]
    tm = _pick(s, (256, 128))
    nt = s // tm
    w = D_RNN
    n_in = len(side.ins) if side else 0
    n_out = len(side.out_shapes) if side else 0

    def body(*refs):
        (dr_ref, xr_ref, xc_ref, gr_ref, h_ref, hb_ref, cw_ref, vec_ref, wa_ref, wx_ref, wat_ref,
         wxt_ref) = refs[:12]
        s_in = refs[12:12 + n_in]
        o_ref, gwa_ref, gwx_ref, gcw_ref, gvec_ref = refs[12 + n_in:17 + n_in]
        s_out = refs[17 + n_in:17 + n_in + n_out]
        nxt_a, nxt_l, nxt_d = refs[17 + n_in + n_out:20 + n_in + n_out]
        sems = refs[20 + n_in + n_out:]
        i = pl.program_id(0)
        if side:
            @pl.when(i == 0)
            def _():
                side.start(s_in, s_out, sems)

        @pl.when(i == 0)
        def _():
            nxt_a[...] = jnp.zeros_like(nxt_a)
            nxt_l[...] = jnp.zeros_like(nxt_l)
            nxt_d[...] = jnp.zeros_like(nxt_d)
            gwa_ref[...] = jnp.zeros_like(gwa_ref)
            gwx_ref[...] = jnp.zeros_like(gwx_ref)
            gcw_ref[...] = jnp.zeros_like(gcw_ref)
            gvec_ref[...] = jnp.zeros_like(gvec_ref)
        hbefore = jnp.where(i == nt - 1, 0.0, hb_ref[...])
        xc = xc_ref[...]
        r, ig, a, mult, sp, rmult = _rg_gates(xc, wa_ref, wx_ref, vec_ref)
        hh = h_ref[...]
        gel, dgel = _gelu_and_grad(gr_ref[...])
        drec_ = dr_ref[...]
        o_ref[:, w:] = (drec_ * hh * dgel).astype(MXU_DT)
        lam = _scan_rev(_shift_up(a, nxt_a[...], 1), drec_ * gel, nxt_l[0:1, :])
        nxt_a[...] = a[:8, :]
        nxt_l[...] = lam[:8, :]
        da = lam * _shift_down(hh, hbefore, 1)
        dxc = lam * mult * ig
        dla = da * a - (lam * ig * xc) * (a * a) * rmult
        dpr = dla * (-RG_C * sp) * r * (1.0 - r)
        dpi = (lam * mult * xc) * ig * (1.0 - ig)
        dprb = dpr.astype(MXU_DT)
        dpib = dpi.astype(MXU_DT)
        dxc = dxc + jnp.dot(dprb, wat_ref[...], preferred_element_type=F32) \
                  + jnp.dot(dpib, wxt_ref[...], preferred_element_type=F32)
        xcb = xc.astype(MXU_DT)
        for rb_ in range(w // 128):
            lo, hi = max(0, 128 * (rb_ - 1)), min(w, 128 * (rb_ + 2))
            rows_ = slice(128 * rb_, 128 * (rb_ + 1))
            gwa_ref[rows_, lo:hi] += _dot_tn(xcb[:, rows_], dprb[:, lo:hi])
            gwx_ref[rows_, lo:hi] += _dot_tn(xcb[:, rows_], dpib[:, lo:hi])
        o_ref[:, :w] = _conv_causal_bwd(dxc, nxt_d[...], cw_ref, 4, xr_ref[...], gcw_ref).astype(MXU_DT)
        nxt_d[...] = dxc[:8, :]
        gvec_ref[0:1, :] += _rsum(dxc)
        gvec_ref[1:2, :] += _rsum(dpr)
        gvec_ref[2:3, :] += _rsum(dpi)
        gvec_ref[3:4, :] += _rsum(dla * (-RG_C * r)) * (-_sigmoid(-vec_ref[3:4, :]))
        if side:
            @pl.when(i == nt - 1)
            def _():
                side.finish(s_in, s_out, sems)

    return pl.pallas_call(
        body, name="rg_bwd", grid=(nt,),
        in_specs=[_rows(tm, w, 0, nt), _rows(tm, w, 0, nt), _rows(tm, w, 0, nt), _rows(tm, w, 1, nt),
                  _rows(tm, w, 0, nt), _before(tm, w, 0, nt), _whole((8, w)), _whole((8, w)),
                  _whole((w, w)), _whole((w, w)), _whole((w, w)), _whole((w, w))] + [_HBM] * n_in,
        out_specs=[_rows(tm, 2 * w, 0, nt), _whole((w, w)), _whole((w, w)), _whole((8, w)), _whole((8, w))]
        + [_HBM] * n_out,
        out_shape=[jax.ShapeDtypeStruct((s, 2 * w), MXU_DT), jax.ShapeDtypeStruct((w, w), F32),
                   jax.ShapeDtypeStruct((w, w), F32), jax.ShapeDtypeStruct((8, w), F32),
                   jax.ShapeDtypeStruct((8, w), F32)] + (list(side.out_shapes) if side else []),
        scratch_shapes=[pltpu.VMEM((8, w), F32)] * 3 + (list(side.sems) if side else []),
        compiler_params=_params(("arbitrary",)),
    )(drec, px, xcs, px, h, h, cw, vec, wa, wx, wat, wxt, *(side.ins if side else []))


def _mm_epi(a, b, *, name, epi, extras=(), wholes=(), row_outs=(), acc_outs=(), tm=None, side=None):
    m, kk = a.shape
    n = b.shape[1]
    tm = min(tm, m) if tm else _pick(m, (512, 256, 128))
    tk = kk if kk <= 2816 else _pick(kk, (2816, 2176, 2048, 1024))
    nk, ni = kk // tk, m // tm
    n_ex, n_wh, n_ro, n_ao = len(extras), len(wholes), len(row_outs), len(acc_outs)
    n_si = len(side.ins) if side else 0
    n_so = len(side.out_shapes) if side else 0
    placed = [(j_, r_[2]) for j_, r_ in enumerate(row_outs) if len(r_) == 3]

    def body(*refs):
        a_ref, b_ref = refs[:2]
        p = 2
        ex, p = refs[p:p + n_ex], p + n_ex
        wh, p = refs[p:p + n_wh], p + n_wh
        s_in, p = refs[p:p + n_si], p + n_si
        p += len(placed)
        ro, p = refs[p:p + n_ro], p + n_ro
        ao, p = refs[p:p + n_ao], p + n_ao
        s_out, p = refs[p:p + n_so], p + n_so
        i, k = pl.program_id(0), pl.program_id(1)
        first = (i == 0) & (k == 0)
        if side:
            sems = refs[p + (1 if nk > 1 else 0):]

            @pl.when(first)
            def _():
                side.start(s_in, s_out, sems)
        if n_ao:
            @pl.when(first)
            def _():
                for r_ in ao:
                    r_[...] = jnp.zeros_like(r_)
        if nk == 1:
            epi(_dot(a_ref[...], b_ref[...]), ex, wh, ro, ao)
        else:
            acc = refs[p]

            @pl.when(k == 0)
            def _():
                acc[...] = jnp.zeros_like(acc)
            acc[...] += _dot(a_ref[...], b_ref[...])

            @pl.when(k == nk - 1)
            def _():
                epi(acc[...], ex, wh, ro, ao)
        if side:
            @pl.when((i == ni - 1) & (k == nk - 1))
            def _():
                side.finish(s_in, s_out, sems)

    in_specs = [pl.BlockSpec((tm, tk), lambda i, k: (i, k)), pl.BlockSpec((tk, n), lambda i, k: (k, 0))]
    in_specs += [pl.BlockSpec((tm, w_), functools.partial(lambda i, k, c_: (i, c_), c_=c_)) for _, w_, c_ in extras]
    in_specs += [_whole(x_.shape) for x_ in wholes] + [_HBM] * (n_si + len(placed))
    out_specs, out_shape = [], []
    for r_ in row_outs:
        blk = r_[2][1] if len(r_) == 3 else 0
        out_specs.append(pl.BlockSpec((tm, r_[0]), functools.partial(lambda i, k, c_: (i, c_), c_=blk)))
        out_shape.append(jax.ShapeDtypeStruct(r_[2][0].shape if len(r_) == 3 else (m, r_[0]), r_[1]))
    out_specs += [_whole(sh) for sh in acc_outs] + [_HBM] * n_so
    out_shape += [jax.ShapeDtypeStruct(sh, F32) for sh in acc_outs] + (list(side.out_shapes) if side else [])
    scratch = ([pltpu.VMEM((tm, n), F32)] if nk > 1 else []) + (list(side.sems) if side else [])
    first_placed = 2 + n_ex + n_wh + n_si
    return pl.pallas_call(
        body, name=name, grid=(ni, nk), in_specs=in_specs, out_specs=out_specs, out_shape=out_shape,
        scratch_shapes=scratch, compiler_params=_params(("arbitrary", "arbitrary")),
        input_output_aliases={first_placed + q_: j_ for q_, (j_, _) in enumerate(placed)},
    )(a, b, *[x_ for x_, _, _ in extras], *wholes, *(side.ins if side else []), *[pb_[0] for _, pb_ in placed])


def _mm_pro(b, *, name, pro, m, extras=(), wholes=(), row_outs=(), col_outs=(), tm=256):
    kk, n = b.shape
    tn = _pick(n, (1408, 1024, 512, 256, 128))
    tm = min(tm, m)
    n_ex, n_wh, n_ro, n_co = len(extras), len(wholes), len(row_outs), len(col_outs)

    def body(*refs):
        ex = refs[:n_ex]
        wh = refs[n_ex:n_ex + n_wh]
        b_ref, o_ref = refs[n_ex + n_wh], refs[n_ex + n_wh + 1]
        ro = refs[n_ex + n_wh + 2:n_ex + n_wh + 2 + n_ro]
        co = refs[n_ex + n_wh + 2 + n_ro:n_ex + n_wh + 2 + n_ro + n_co]
        a_scr = refs[-1]

        @pl.when(pl.program_id(1) == 0)
        def _():
            a_scr[...] = pro(ex, wh, ro, co).astype(MXU_DT)
        o_ref[...] = jnp.dot(a_scr[...], b_ref[...], preferred_element_type=F32)

    in_specs = [pl.BlockSpec((tm, w_), functools.partial(lambda i, j, c_: (i, c_), c_=c_)) for _, w_, c_ in extras]
    in_specs += [_whole(x_.shape) for x_ in wholes] + [pl.BlockSpec((kk, tn), lambda i, j: (0, j))]
    out_specs = [pl.BlockSpec((tm, tn), lambda i, j: (i, j))]
    out_specs += [pl.BlockSpec((tm, w_), lambda i, j: (i, 0)) for w_, _ in row_outs]
    out_specs += [pl.BlockSpec((w_, tm), lambda i, j: (0, i)) for w_, _ in col_outs]
    out_shape = [jax.ShapeDtypeStruct((m, n), F32)] + [jax.ShapeDtypeStruct((m, w_), dt) for w_, dt in row_outs]
    out_shape += [jax.ShapeDtypeStruct((w_, m), dt) for w_, dt in col_outs]
    return pl.pallas_call(
        body, name=name, grid=(m // tm, n // tn), in_specs=in_specs, out_specs=out_specs, out_shape=out_shape,
        scratch_shapes=[pltpu.VMEM((tm, kk), MXU_DT)], compiler_params=_params(("arbitrary", "arbitrary")),
    )(*[x_ for x_, _, _ in extras], *wholes, b.astype(MXU_DT))


def _merge_proj_out(pg, ya, yb, w_out):
    def pro(ex, wh, ro, co):
        ga_ref, gb_ref, ya_ref, yb_ref = ex
        mg = _sigmoid(ga_ref[...]) * ya_ref[...] + _sigmoid(gb_ref[...]) * yb_ref[...]
        co[0][...] = mg.T.astype(MXU_DT)
        return mg

    return _mm_pro(w_out, name="merge_proj_out", pro=pro, m=ya.shape[0],
                   extras=[(pg, D, 0), (pg, D, 1), (ya, D, 0), (yb, D, 0)], col_outs=[(D, MXU_DT)])


def _ffn_down_ln2_loss(act, w_down, x2, tgt, adam, lng, lnb):
    def epi(ff_, ex, wh, ro, ao):
        x_ref, t_ref = ex
        ada_ref, g_ref, b_ref = wh
        dff_ref, dx_ref = ro
        red_ref, = ao
        r = ALPHA * x_ref[...] + (1.0 + ada_ref[5:6, :]) * ff_
        xhat, rstd = _ln_stats(r)
        err = xhat * g_ref[...] + b_ref[...] - t_ref[...]
        dy = err * (1.0 / D)
        dr = _ln_bwd(dy, xhat, rstd, g_ref[...])
        dff_ref[...] = ((1.0 + ada_ref[5:6, :]) * dr).astype(MXU_DT)
        dx_ref[...] = ALPHA * dr
        red_ref[0:1, :] += _rsum(dy * xhat)
        red_ref[1:2, :] += _rsum(dy)
        red_ref[2:3, :] += _rsum(dr * ff_)
        red_ref[3:4, :] += jnp.sum(_rsum(err * err), axis=1, keepdims=True) * (0.5 / D)

    return _mm_epi(act, w_down, name="ffn_down_ln2_loss", epi=epi, extras=[(x2, D, 0), (tgt, D, 0)],
                   wholes=[adam, lng, lnb], row_outs=[(D, MXU_DT), (D, F32)], acc_outs=[(8, D)])


def _d_h1_modulate_bwd(dproj, w_t, dxa, x, adam, side):
    def epi(dh, ex, wh, ro, ao):
        dxa_ref, x_ref = ex
        ada_ref, = wh
        ro[0][...] = dxa_ref[...] + dh * (1.0 + ada_ref[1:2, :])
        ao[0][0:1, :] += _rsum(dh * x_ref[...])
        ao[0][1:2, :] += _rsum(dh)

    return _mm_epi(dproj, w_t, name="d_h1", epi=epi, extras=[(dxa, D, 0), (x, D, 0)], wholes=[adam],
                   row_outs=[(D, F32)], acc_outs=[(8, D)], side=side)


def _d_merged_bwd(dmix, w_out_t, pg, ya, yb, dproj):
    def epi(d, ex, wh, ro, ao):
        ga_ref, gb_ref, ya_ref, yb_ref = ex
        dya_ref, dyb_ref, dpg_ref = ro
        sa = _sigmoid(ga_ref[...])
        sb = _sigmoid(gb_ref[...])
        dya_ref[...] = (d * sa).astype(MXU_DT)
        dyb_ref[...] = (d * sb).astype(MXU_DT)
        dpg_ref[:, :D] = (d * ya_ref[...] * sa * (1.0 - sa)).astype(MXU_DT)
        dpg_ref[:, D:] = (d * yb_ref[...] * sb * (1.0 - sb)).astype(MXU_DT)

    return _mm_epi(dmix, w_out_t, name="d_merged", epi=epi,
                   extras=[(pg, D, 0), (pg, D, 1), (ya, D, 0), (yb, D, 0)],
                   row_outs=[(D, MXU_DT), (D, MXU_DT), (2 * D, MXU_DT, (dproj, 3))])


def _d_dn_post_bwd(dyb, w_pb_t, o, pz, nw, dproj):
    def epi(d_all, ex, wh, ro, ao):
        o_ref, z_ref = ex
        nw_ref, = wh
        do_ref, dz_ref = ro
        acc = jnp.zeros((1, HD), F32)
        for h in range(NV):
            cs = slice(h * HD, (h + 1) * HD)
            oh = o_ref[:, cs]
            z = z_ref[:, cs]
            d = d_all[:, cs]
            sg = _sigmoid(z)
            rs = lax.rsqrt(jnp.mean(oh * oh, axis=-1, keepdims=True) + RMS_EPS)
            n = oh * rs
            dz_ref[:, cs] = (d * n * nw_ref[...] * sg * (1.0 + z * (1.0 - sg))).astype(MXU_DT)
            dn_ = d * (z * sg)
            acc = acc + _rsum(dn_ * n)
            dnn = dn_ * nw_ref[...]
            do_ref[:, cs] = rs * (dnn - n * jnp.mean(dnn * n, axis=-1, keepdims=True))
        ao[0][...] += acc

    return _mm_epi(dyb, w_pb_t, name="d_dn", epi=epi, extras=[(o, NV * HD, 0), (pz, NV * HD, 0)], wholes=[nw],
                   row_outs=[(NV * HD, F32), (NV * HD, MXU_DT, (dproj, 2))], acc_outs=[(1, HD)], tm=512)


def _d_h2_ln1_bwd(dgu, w_gu_t, dx2a, x2, r1, mix, adam, lng):
    def epi(dh, ex, wh, ro, ao):
        dxa_ref, x2_ref, r_ref, m_ref = ex
        ada_ref, g_ref = wh
        dm_ref, dx_ref = ro
        red_ref, = ao
        dx2 = dxa_ref[...] + dh * (1.0 + ada_ref[4:5, :])
        xhat, rstd = _ln_stats(r_ref[...])
        dr = _ln_bwd(dx2, xhat, rstd, g_ref[...])
        dm_ref[...] = ((1.0 + ada_ref[2:3, :]) * dr).astype(MXU_DT)
        dx_ref[...] = ALPHA * dr
        red_ref[0:1, :] += _rsum(dh * x2_ref[...])
        red_ref[1:2, :] += _rsum(dh)
        red_ref[2:3, :] += _rsum(dx2 * xhat)
        red_ref[3:4, :] += _rsum(dx2)
        red_ref[4:5, :] += _rsum(dr * m_ref[...])

    return _mm_epi(dgu, w_gu_t, name="d_h2", epi=epi,
                   extras=[(dx2a, D, 0), (x2, D, 0), (r1, D, 0), (mix, D, 0)], wholes=[adam, lng],
                   row_outs=[(D, MXU_DT), (D, F32)], acc_outs=[(8, D)], tm=512)


def _adamw(parts, w, m, v, name):
    r, c = w.shape
    n_parts = len(parts)
    tm = _row_tile(r, c * 4 * (n_parts + 7))
    c1 = 1.0 - ADAM_B1 ** ADAM_STEP
    c2 = 1.0 - ADAM_B2 ** ADAM_STEP

    def body(*refs):
        g = refs[0][...]
        for p_ref in refs[1:n_parts]:
            g = g + p_ref[...]
        w_ref, m_ref, v_ref, g_out, d_out, m_out, v_out = refs[n_parts:]
        mn = ADAM_B1 * m_ref[...] + (1.0 - ADAM_B1) * g
        vn = ADAM_B2 * v_ref[...] + (1.0 - ADAM_B2) * (g * g)
        g_out[...] = g
        m_out[...] = mn
        v_out[...] = vn
        d_out[...] = -ADAM_LR * ((mn / c1) / (jnp.sqrt(vn / c2) + ADAM_EPS) + ADAM_WD * w_ref[...])

    spec = pl.BlockSpec((tm, c), lambda i: (i, 0))
    return pl.pallas_call(
        body, name=name, grid=(r // tm,),
        in_specs=[spec] * (n_parts + 3), out_specs=[spec] * 4,
        out_shape=[jax.ShapeDtypeStruct((r, c), F32)] * 4, compiler_params=_params(("parallel",)),
    )(*parts, w, m, v)


def _row_tile(r, bytes_per_row):
    for t in (512, 256, 128, 64, 32, 16):
        if r % t == 0 and 2 * t * bytes_per_row <= 20 * 1024 * 1024:
            return t
    return _pick(r, (16, 8))


def _sum_partials(own, recv, name):
    r, c = own.shape
    tm = _row_tile(r, c * (4 + 4 + 3 * 2))

    def body(o_ref, r_ref, out_ref):
        out_ref[...] = ((o_ref[...] + r_ref[0].astype(F32)) + r_ref[1].astype(F32)) + r_ref[2].astype(F32)

    return pl.pallas_call(
        body, name=name, grid=(r // tm,),
        in_specs=[pl.BlockSpec((tm, c), lambda i: (i, 0)), pl.BlockSpec((3, tm, c), lambda i: (0, i, 0))],
        out_specs=pl.BlockSpec((tm, c), lambda i: (i, 0)),
        out_shape=jax.ShapeDtypeStruct((r, c), F32), compiler_params=_params(("parallel",)),
    )(own, recv)


def _sum8(g, name):
    _, r, c = g.shape
    tm = _pick(r, (256, 128, 64, 32, 16, 8))

    def body(g_ref, o_ref):
        acc = g_ref[0]
        for k in range(1, 8):
            acc = acc + g_ref[k]
        o_ref[...] = acc

    return pl.pallas_call(
        body, name=name, grid=(r // tm,),
        in_specs=[pl.BlockSpec((8, tm, c), lambda i: (0, i, 0))],
        out_specs=pl.BlockSpec((tm, c), lambda i: (i, 0)),
        out_shape=jax.ShapeDtypeStruct((r, c), F32), compiler_params=_params(("parallel",)),
    )(g)


def _silu_rows(x):
    def body(x_ref, o_ref):
        xx = x_ref[...]
        o_ref[...] = xx * _sigmoid(xx)

    return pl.pallas_call(body, name="silu_rows", out_shape=jax.ShapeDtypeStruct(x.shape, F32))(x)


def _coords():
    return lax.axis_index("x"), lax.axis_index("y"), lax.axis_index("c")


def _allgather8(v, name):
    r, n = v.shape

    def body(v_ref, out_ref, send_sems, recv_sems):
        x, y, c = _coords()
        me = 4 * x + 2 * y + c
        out_ref[me] = v_ref[...]
        peers = []
        for k in range(1, 8):
            px = 1 - x if k & 4 else x
            py = 1 - y if k & 2 else y
            pc = 1 - c if k & 1 else c
            peers.append((px, py, pc))

        def copy(k, slot, to):
            return pltpu.make_async_remote_copy(
                src_ref=v_ref, dst_ref=out_ref.at[slot], send_sem=send_sems.at[k], recv_sem=recv_sems.at[k],
                device_id=to, device_id_type=MESH)

        sends = [copy(k, me, p) for k, p in enumerate(peers)]
        for cp in sends:
            cp.start()
        for k, (px, py, pc) in enumerate(peers):
            copy(k, 4 * px + 2 * py + pc, (px, py, pc)).wait_recv()
        for cp in sends:
            cp.wait_send()

    return pl.pallas_call(
        body, name=name, out_shape=jax.ShapeDtypeStruct((8, r, n), v.dtype),
        in_specs=[pl.BlockSpec(memory_space=pltpu.VMEM)], out_specs=pl.BlockSpec(memory_space=pltpu.VMEM),
        scratch_shapes=[pltpu.SemaphoreType.DMA((7,)), pltpu.SemaphoreType.DMA((7,))],
        compiler_params=pltpu.CompilerParams(vmem_limit_bytes=VMEM_LIMIT),
    )(v)


def _other_chips(x, y):
    return [(1 - x, y), (x, 1 - y), (1 - x, 1 - y)]


_HBM = pl.BlockSpec(memory_space=pl.ANY)


class _Exchange:
    def __init__(self, ins, out_shapes, sems, start, finish):
        self.ins, self.out_shapes, self.sems, self.start, self.finish = ins, out_shapes, sems, start, finish


def _run_exchange(ex, name):
    n_in, n_out = len(ex.ins), len(ex.out_shapes)

    def body(*refs):
        ins, outs, sems = refs[:n_in], refs[n_in:n_in + n_out], refs[n_in + n_out:]
        ex.start(ins, outs, sems)
        ex.finish(ins, outs, sems)

    return pl.pallas_call(body, name=name, out_shape=list(ex.out_shapes), in_specs=[_HBM] * n_in,
                          out_specs=[_HBM] * n_out, scratch_shapes=list(ex.sems))(*ex.ins)


def _gather_exchange(shards):
    n = len(shards)

    def plan(ins, outs, sems):
        send_sems, recv_sems, local_sems = sems
        x, y, c = _coords()
        s_me = 2 * x + y
        chips = _other_chips(x, y)

        def half(i, slot, hc):
            rh = shards[i].shape[0] // 2
            return outs[i].at[slot, pl.ds(pl.multiple_of(hc * rh, 16), rh), :]

        def copy(i, k, src, dst, to):
            return pltpu.make_async_remote_copy(src_ref=src, dst_ref=dst, send_sem=send_sems.at[6 * i + k],
                                                recv_sem=recv_sems.at[6 * i + k], device_id=to, device_id_type=MESH)

        local = [pltpu.make_async_copy(ins[i], outs[i].at[s_me], local_sems.at[i]) for i in range(n)]
        first = []
        for i in range(n):
            rh = shards[i].shape[0] // 2
            my_half = ins[i].at[pl.ds(pl.multiple_of(c * rh, 16), rh), :]
            first += [copy(i, j, my_half, half(i, s_me, c), (px, py, c)) for j, (px, py) in enumerate(chips)]
        return (x, y, c), chips, half, copy, local, first

    def start(ins, outs, sems):
        _, _, _, _, local, first = plan(ins, outs, sems)
        for cp in local + first:
            cp.start()

    def finish(ins, outs, sems):
        (x, y, c), chips, half, copy, local, first = plan(ins, outs, sems)
        sibling = (x, y, 1 - c)
        passed = []
        for i in range(n):
            for j, (px, py) in enumerate(chips):
                land = half(i, 2 * px + py, c)
                copy(i, j, land, land, (px, py, c)).wait_recv()
                fw = copy(i, 3 + j, land, land, sibling)
                fw.start()
                passed.append(fw)
        for i in range(n):
            for j, (px, py) in enumerate(chips):
                land = half(i, 2 * px + py, 1 - c)
                copy(i, 3 + j, land, land, sibling).wait_recv()
        for cp in first + passed:
            cp.wait_send()
        for cp in local:
            cp.wait()

    return _Exchange(list(shards), [jax.ShapeDtypeStruct((4,) + v.shape, v.dtype) for v in shards],
                     [pltpu.SemaphoreType.DMA((6 * n,)), pltpu.SemaphoreType.DMA((6 * n,)),
                      pltpu.SemaphoreType.DMA((n,))], start, finish)


def _scatter_exchange(gs):
    n = len(gs)

    def copies(ins, outs, sems):
        send_sems, recv_sems = sems
        x, y, c = _coords()
        return [pltpu.make_async_remote_copy(
                    src_ref=ins[i].at[2 * px + py], dst_ref=outs[i].at[j], send_sem=send_sems.at[3 * i + j],
                    recv_sem=recv_sems.at[3 * i + j], device_id=(px, py, c), device_id_type=MESH)
                for i in range(n) for j, (px, py) in enumerate(_other_chips(x, y))]

    def start(ins, outs, sems):
        for cp in copies(ins, outs, sems):
            cp.start()

    def finish(ins, outs, sems):
        cps = copies(ins, outs, sems)
        for cp in cps:
            cp.wait_recv()
        for cp in cps:
            cp.wait_send()

    return _Exchange(list(gs), [jax.ShapeDtypeStruct((3,) + g.shape[1:], g.dtype) for g in gs],
                     [pltpu.SemaphoreType.DMA((3 * n,)), pltpu.SemaphoreType.DMA((3 * n,))], start, finish)


def _allgather_exchange(v):
    def copies(ins, outs, sems):
        send_sems, recv_sems, _ = sems
        x, y, c = _coords()
        me = 4 * x + 2 * y + c
        cps = []
        for k in range(1, 8):
            peer = (1 - x if k & 4 else x, 1 - y if k & 2 else y, 1 - c if k & 1 else c)
            cps.append(pltpu.make_async_remote_copy(
                src_ref=ins[0], dst_ref=outs[0].at[me], send_sem=send_sems.at[k - 1], recv_sem=recv_sems.at[k - 1],
                device_id=peer, device_id_type=MESH))
        return me, cps

    def start(ins, outs, sems):
        me, cps = copies(ins, outs, sems)
        pltpu.make_async_copy(ins[0], outs[0].at[me], sems[2]).start()
        for cp in cps:
            cp.start()

    def finish(ins, outs, sems):
        me, cps = copies(ins, outs, sems)
        for cp in cps:
            cp.wait_recv()
        for cp in cps:
            cp.wait_send()
        pltpu.make_async_copy(ins[0], outs[0].at[me], sems[2]).wait()

    return _Exchange([v], [jax.ShapeDtypeStruct((8,) + v.shape, v.dtype)],
                     [pltpu.SemaphoreType.DMA((7,)), pltpu.SemaphoreType.DMA((7,)), pltpu.SemaphoreType.DMA],
                     start, finish)


def _swap_sibling(vs):
    n = len(vs)

    def body(*refs):
        ins, outs = refs[:n], refs[n:2 * n]
        send_sems, recv_sems = refs[2 * n:]
        x, y, c = _coords()
        cps = [pltpu.make_async_remote_copy(src_ref=ins[i], dst_ref=outs[i], send_sem=send_sems.at[i],
                                            recv_sem=recv_sems.at[i], device_id=(x, y, 1 - c), device_id_type=MESH)
               for i in range(n)]
        for cp in cps:
            cp.start()
        for cp in cps:
            cp.wait()

    return pl.pallas_call(
        body, name="swap_sibling", out_shape=[jax.ShapeDtypeStruct(v.shape, v.dtype) for v in vs],
        in_specs=[_HBM] * n, out_specs=[_HBM] * n,
        scratch_shapes=[pltpu.SemaphoreType.DMA((n,)), pltpu.SemaphoreType.DMA((n,))],
    )(*vs)


def _pad_rows(a, rows):
    return jnp.pad(a, ((0, rows - a.shape[0]), (0, 0)))


def _block_diag(w):
    eye = jnp.eye(RG_BLOCKS, dtype=w.dtype)
    return (eye[:, None, :, None] * w[:, :, None, :]).reshape(D_RNN, D_RNN)


def _diag_blocks(g):
    g4 = g.reshape(RG_BLOCKS, RG_BW, RG_BLOCKS, RG_BW)
    idx = jnp.arange(RG_BLOCKS)
    return g4[idx, :, idx, :]


def _prepare_rest(p):
    w = {}
    for k_, n_ in (("pa", "w_proj_a"), ("pb", "w_proj_b"), ("out", "w_out"), ("down", "ffn_w_down")):
        w[k_] = p[n_].astype(MXU_DT)
        w[k_ + "_t"] = w[k_].T
    w["gu"] = jnp.concatenate([p["ffn_w_gate"], p["ffn_w_up"]], axis=1).astype(MXU_DT)
    w["gu_t"] = w["gu"].T
    return w


def _prepare_first(p):
    w = {}
    wi = p["w_in"].astype(MXU_DT)
    cat = jnp.concatenate([wi[:, 2560:6656], wi[:, 6656:8704], wi[:, 8736:10784], wi[:, 0:2560],
                           wi[:, 8704:8736], jnp.zeros((D, 96), MXU_DT)], axis=1)
    w["in_cat"], w["in_cat_t"] = cat, cat.T
    w["rg_cw"] = _pad_rows(p["rg_conv_w"], 8)
    w["dn_cw"] = _pad_rows(p["dn_conv_w"], 8)
    w["ffn_cw"] = _pad_rows(p["ffn_conv_w"], 8)
    w["rg_vec"] = _pad_rows(jnp.stack([p["rg_conv_b"], p["rg_b_a"], p["rg_b_x"], p["rg_lambda"]]), 8)
    w["wa"] = _block_diag(p["rg_w_a"]).astype(MXU_DT)
    w["wx"] = _block_diag(p["rg_w_x"]).astype(MXU_DT)
    w["wa_t"], w["wx_t"] = w["wa"].T, w["wx"].T
    w["arow"] = jnp.pad(p["dn_a_log"], (0, 128 - NV))[None, :]
    w["drow"] = jnp.pad(p["dn_dt_bias"], (0, 128 - NV))[None, :]
    w["acol"] = p["dn_a_log"][:, None]
    w["dcol"] = p["dn_dt_bias"][:, None]
    w["nw"] = p["dn_norm_w"][None, :]
    w["ffn_cb"] = p["ffn_conv_b"][None, :]
    for n_ in ("ln1_g", "ln1_b", "ln2_g", "ln2_b"):
        w[n_] = p[n_][None, :]
    return w


def _mm_sided(a, b, side, **kw):
    if side is None:
        return _mm(a, b, **kw), []
    res = _mm(a, b, side=side, **kw)
    return res[0], res[1:]


_SMALL = ("rg_conv_w", "dn_conv_w", "ffn_conv_w", "rg_conv_b", "rg_w_a", "rg_b_a", "rg_w_x", "rg_b_x",
          "rg_lambda", "dn_a_log", "dn_dt_bias", "dn_norm_w", "ln1_g", "ln1_b", "ffn_conv_b", "ln2_g", "ln2_b")
_REST_A = ("w_proj_a", "w_proj_b", "w_out")
_REST_B = ("ffn_w_gate", "ffn_w_up", "ffn_w_down")


def _local_step(x, tgt, adam, p, shards=None):
    w = _prepare_first(p)
    side_a = side_b = None
    if shards is not None:
        side_a = _gather_exchange([shards[n] for n in _REST_A])
        side_b = _gather_exchange([shards[n] for n in _REST_B])
    h1, h1t = _modulate(x, adam)
    cat = w["in_cat"]
    pqkv, got_b = _mm_sided(h1, cat[:, 0:4096], side_b, name="proj_qkv")
    px, got_a = _mm_sided(h1, cat[:, 8192:10752], side_a, name="proj_x")
    if shards is not None:
        p = dict(p, **{n: _unstack_shards(n, a_) for n, a_ in zip(_REST_A + _REST_B, got_a + got_b)})
    w.update(_prepare_rest(p))
    pz = _mm(h1, cat[:, 4096:6144], name="proj_z")
    pg = _mm(h1, cat[:, 6144:8192], name="proj_g")
    pab = _mm(h1, cat[:, 10752:10880], name="proj_ab")
    hrec, xcs, rec, rec_t = _rg_fwd(px, w["rg_cw"], w["rg_vec"], w["wa"], w["wx"])
    q, k, v, rt, cpre = _dn_prep(pqkv, pab, w["dn_cw"], w["arow"], w["drow"])
    u, ww, qd, kd, pm, tinv, gam = _dn_intra(q, k, v, rt)
    o, vn, ssave, dn, dn_t = _dn_seq(u, ww, qd, kd, pm, gam, pz, w["nw"])
    ya = _mm(rec, w["pa"], name="proj_a")
    yb = _mm(dn, w["pb"], name="proj_b")
    mix, merged_t = _merge_proj_out(pg, ya, yb, w["out"])
    r1, x2, h2, h2_t = _ln1(x, mix, adam, w["ln1_g"], w["ln1_b"])
    gu = _mm(h2, w["gu"], name="ffn_gu")
    act, act_t, gcf = _ffn_act(gu, w["ffn_cw"], w["ffn_cb"])
    dff, dx2a, red2 = _ffn_down_ln2_loss(act, w["down"], x2, tgt, adam, w["ln2_g"], w["ln2_b"])
    g = {}
    dact = _mm(dff, w["down_t"], name="d_act")
    g["ffn_w_down"] = _mm(act_t, dff, name="g_down")
    dgu, gcw_f, gcb_f = _ffn_bwd(dact, gu, gcf, w["ffn_cw"])
    ggu = _mm(h2_t, dgu, name="g_gu")
    g["ffn_w_gate"], g["ffn_w_up"] = ggu[:, :D_FF], ggu[:, D_FF:]
    g["ffn_conv_w"], g["ffn_conv_b"] = gcw_f[0:3], gcb_f[0]
    dmix, dxa, red1 = _d_h2_ln1_bwd(dgu, w["gu_t"], dx2a, x2, r1, mix, adam, w["ln1_g"])
    g["w_out"] = _mm(merged_t, dmix, name="g_out")
    dproj = lax.empty((x.shape[0], N_CAT), MXU_DT)
    dya, dyb, dproj = _d_merged_bwd(dmix, w["out_t"], pg, ya, yb, dproj)
    drec = _mm(dya, w["pa_t"], name="d_rec")
    g["w_proj_a"] = _mm(rec_t, dya, name="g_pa")
    g["w_proj_b"] = _mm(dn_t, dyb, name="g_pb")
    do, dproj, gnw = _d_dn_post_bwd(dyb, w["pb_t"], o, pz, w["nw"], dproj)
    dvn, dkd, dgam = _dn_seq_bwd(do, qd, kd, pm, ww, vn, ssave, gam)
    dq, dk, dv, drt = _dn_intra_bwd(q, k, v, rt, do, dvn, dkd, gam, dgam, ssave, tinv, u, ww, vn)
    dproj, dpab, gcw_d, gsc = _dn_prep_bwd(dq, dk, dv, drt, pqkv, cpre, pab, w["dn_cw"], w["acol"], w["dcol"],
                                           dproj)
    side_r = None
    if shards is not None:
        side_r = _scatter_exchange([_stack_shards(n, g[n]).astype(MXU_DT) for n in _REST_A + _REST_B])
    dpx, gwa, gwx, gcw_r, gvec, *got_r = _rg_bwd(drec, px, xcs, hrec, w["rg_cw"], w["rg_vec"], w["wa"], w["wx"],
                                                 w["wa_t"], w["wx_t"], side_r)
    dproj = lax.dynamic_update_slice(dproj, jnp.concatenate([dpx, dpab], axis=1), (0, 8192))
    reorder = lambda gc: jnp.concatenate([gc[:, 8192:10752], gc[:, 0:4096], gc[:, 4096:6144], gc[:, 10752:10784],
                                          gc[:, 6144:8192]], axis=1)
    wire = lambda gh: _scatter_exchange([_stack_shards("w_in", gh).astype(MXU_DT)]) if shards is not None else None
    g["rg_conv_w"], g["rg_conv_b"] = gcw_r[0:4], gvec[0]
    g["rg_w_a"], g["rg_w_x"] = _diag_blocks(gwa), _diag_blocks(gwx)
    g["rg_b_a"], g["rg_b_x"], g["rg_lambda"] = gvec[1], gvec[2], gvec[3]
    g["dn_conv_w"] = gcw_d[0:4]
    g["dn_a_log"], g["dn_dt_bias"], g["dn_norm_w"] = gsc[:, 0], gsc[:, 1], gnw[0]
    g["ln1_g"], g["ln1_b"] = red1[2], red1[3]
    g["ln2_g"], g["ln2_b"] = red2[0], red2[1]
    side_s = None
    if shards is not None:
        side_s = _allgather_exchange(_pack([jnp.full((1,), red2[3, 0], F32)] + [g[n] for n in _SMALL], 128, 64))
    g_top, got_s = _mm_sided(h1t[:D // 2], dproj, side_s, name="g_in_top")
    g_top = reorder(g_top)
    g_bot, got_top = _mm_sided(h1t[D // 2:], dproj, wire(g_top), name="g_in_bot")
    g_bot = reorder(g_bot)
    gx, red0, *got_bot = _d_h1_modulate_bwd(dproj, w["in_cat_t"], dxa, x, adam, wire(g_bot))
    g["w_in"] = jnp.concatenate([g_top, g_bot], axis=0)
    got = dict(zip(_REST_A + _REST_B, got_r))
    if shards is not None:
        got["w_in"] = jnp.concatenate([got_top[0], got_bot[0]], axis=1)
        got["small"] = got_s[0]
    d_ada = jnp.concatenate([red0[1], red0[0], red1[4], red1[1], red1[0], red2[2]])
    return red2[3, 0], gx, g, d_ada, got


_BIG = ("w_in", "w_proj_a", "w_proj_b", "w_out", "ffn_w_gate", "ffn_w_up", "ffn_w_down")
_COL_SHARDED = ("w_in", "ffn_w_gate", "ffn_w_up")
_CONV = ("rg_conv_w", "dn_conv_w", "ffn_conv_w")
_REPL = ("b_ada", "rg_conv_b", "rg_w_a", "rg_b_a", "rg_w_x", "rg_b_x", "rg_lambda", "dn_a_log",
         "dn_dt_bias", "dn_norm_w", "ln1_g", "ln1_b", "ffn_conv_b", "ln2_g", "ln2_b")
_NAMES = ("w_ada", "b_ada", "w_in", "rg_conv_w", "rg_conv_b", "rg_w_a", "rg_b_a", "rg_w_x", "rg_b_x",
          "rg_lambda", "dn_conv_w", "dn_a_log", "dn_dt_bias", "dn_norm_w", "w_proj_a", "w_proj_b", "w_out",
          "ln1_g", "ln1_b", "ffn_w_gate", "ffn_w_up", "ffn_conv_w", "ffn_conv_b", "ffn_w_down", "ln2_g", "ln2_b")


def _pack(arrs, width, row_mult):
    pieces = []
    for a in arrs:
        f = a.reshape(-1)
        pieces.append(jnp.pad(f, (0, (-f.shape[0]) % (8 * width))).reshape(-1, width))
    rows = sum(p_.shape[0] for p_ in pieces)
    if rows % row_mult:
        pieces.append(jnp.zeros((row_mult - rows % row_mult, width), pieces[0].dtype))
    return jnp.concatenate(pieces, axis=0)


def _unpack(flat, shapes, width):
    out, row = [], 0
    for shp in shapes:
        n = 1
        for d_ in shp:
            n *= d_
        rows = -(-n // (8 * width)) * 8
        out.append(flat[row:row + rows].reshape(-1)[:n].reshape(shp))
        row += rows
    return out


def _stack_shards(name, full):
    if name in _COL_SHARDED or name in _CONV:
        r, ncol = full.shape
        return full.reshape(r, 4, ncol // 4).transpose(1, 0, 2)
    return full.reshape((4, full.shape[0] // 4) + full.shape[1:])


def _unstack_shards(name, st):
    if name in _COL_SHARDED or name in _CONV:
        return st.transpose(1, 0, 2).reshape(st.shape[1], 4 * st.shape[2])
    return st.reshape((4 * st.shape[1],) + st.shape[2:])


def kernel(x, c, w_ada, b_ada, w_in, rg_conv_w, rg_conv_b, rg_w_a, rg_b_a, rg_w_x, rg_b_x, rg_lambda, dn_conv_w, dn_a_log, dn_dt_bias, dn_norm_w, w_proj_a, w_proj_b, w_out, ln1_g, ln1_b, ffn_w_gate, ffn_w_up, ffn_conv_w, ffn_conv_b, ffn_w_down, ln2_g, ln2_b, loss_target, m_w_ada, m_b_ada, m_w_in, m_rg_conv_w, m_rg_conv_b, m_rg_w_a, m_rg_b_a, m_rg_w_x, m_rg_b_x, m_rg_lambda, m_dn_conv_w, m_dn_a_log, m_dn_dt_bias, m_dn_norm_w, m_w_proj_a, m_w_proj_b, m_w_out, m_ln1_g, m_ln1_b, m_ffn_w_gate, m_ffn_w_up, m_ffn_conv_w, m_ffn_conv_b, m_ffn_w_down, m_ln2_g, m_ln2_b, v_w_ada, v_b_ada, v_w_in, v_rg_conv_w, v_rg_conv_b, v_rg_w_a, v_rg_b_a, v_rg_w_x, v_rg_b_x, v_rg_lambda, v_dn_conv_w, v_dn_a_log, v_dn_dt_bias, v_dn_norm_w, v_w_proj_a, v_w_proj_b, v_w_out, v_ln1_g, v_ln1_b, v_ffn_w_gate, v_ffn_w_up, v_ffn_conv_w, v_ffn_conv_b, v_ffn_w_down, v_ln2_g, v_ln2_b):
    args = locals()
    wts = {n: args[n][0] for n in _NAMES}
    mom = {n: args["m_" + n][0] for n in _NAMES}
    var = {n: args["v_" + n][0] for n in _NAMES}
    xs, tgt = x[0], loss_target[0]
    ix, iy, ic = _coords()
    shard = 2 * ix + iy
    batch = 4 * ix + 2 * iy + ic

    c_all = _allgather8(_pad_rows(c, 8), "gather_c")[:, 0, :]
    sc16 = _pad_rows(_silu_rows(c_all), 16)
    ada_cols = _mm(sc16, w_ada[0], name="ada")[:8]
    ada_g = _allgather8(ada_cols, "gather_ada")
    ada_all = jnp.concatenate([ada_g[0], ada_g[2], ada_g[4], ada_g[6]], axis=1) + b_ada
    adam = _pad_rows(lax.dynamic_index_in_dim(ada_all, batch, 0, keepdims=False).reshape(6, D), 8)

    (w_in_all,) = _run_exchange(_gather_exchange([wts["w_in"].astype(MXU_DT)]), "gather_w_in")
    full = {"w_in": _unstack_shards("w_in", w_in_all)}
    conv_shard = _pack([wts[n] for n in _CONV], 128, 8)
    conv_all = _allgather8(conv_shard, "gather_conv")
    shapes_conv = [wts[n].shape for n in _CONV]
    per_shard = [_unpack(conv_all[2 * s], shapes_conv, 128) for s in range(4)]
    for i, n in enumerate(_CONV):
        full[n] = _unstack_shards(n, jnp.stack([per_shard[s][i] for s in range(4)]))
    for n in _REPL:
        full[n] = wts[n]

    shards = {n: wts[n].astype(MXU_DT) for n in _REST_A + _REST_B}
    loss_b, gx, g, d_ada, recv = _local_step(xs, tgt, adam, full, shards)

    parts = []
    for n in _BIG:
        r_ = recv[n]
        axis = 1 if n in _COL_SHARDED else 0
        width = wts[n].shape[axis]
        own = lax.dynamic_slice_in_dim(g[n], shard * width, width, axis=axis)
        parts.append(_sum_partials(own, r_, "sum_" + n))
    parts_sib = _swap_sibling(parts)
    out = {n: [] for n in _NAMES}
    for n, p_, q_ in zip(_BIG, parts, parts_sib):
        out[n] = list(_adamw([p_, q_], wts[n], mom[n], var[n], "adamw_" + n))

    tot = _unpack(_sum8(recv["small"], "sum_small"), [(1,)] + [full[n].shape for n in _SMALL], 128)
    gsum = dict(zip(_SMALL, tot[1:]))
    loss = tot[0][0]
    for n in _CONV:
        gsum[n] = lax.dynamic_index_in_dim(_stack_shards(n, gsum[n]), shard, 0, keepdims=False)
    d_ada_g = _allgather8(d_ada.reshape(6 * D // 128, 128), "gather_d_ada")
    gsum["b_ada"] = _sum8(d_ada_g, "sum_d_ada").reshape(6 * D)
    d_ada_all = d_ada_g.reshape(8, 6 * D)
    cols = lax.dynamic_slice_in_dim(d_ada_all, shard * (6 * D // 4), 6 * D // 4, axis=1)
    g_wada = _mm(sc16, _pad_rows(cols, 16), name="g_ada", trans_a=True)
    res = _adamw([g_wada], wts["w_ada"], mom["w_ada"], var["w_ada"], "adamw_ada")
    out["w_ada"] = list(res)
    names_s = _CONV + _REPL
    shapes_s = [wts[n].shape for n in names_s]
    pk = lambda d_: _pack([d_[n] for n in names_s], 128, 64)
    res_s = _adamw([pk(gsum)], pk(wts), pk(mom), pk(var), "adamw_small")
    for r_ in res_s:
        for n, a in zip(names_s, _unpack(r_, shapes_s, 128)):
            out[n].append(a)

    outs = [loss, gx[None]]
    for i in range(4):
        outs += [out[n][i][None] for n in _NAMES]
    return tuple(outs)
```
